```python
import math
import jax, jax.numpy as jnp
from jax import lax
import numpy as np


D_MODEL = 1024
BATCH = 8
SEQ = 2048
DEPTH = 2

CHUNK = 64
N_A_LAYERS = DEPTH // 2
N_B_LAYERS = DEPTH - N_A_LAYERS
CONV_WIDTH = 31
HEAD_DIM = 64
N_HEADS = D_MODEL // (2 * HEAD_DIM)
QK_WIDTH = N_HEADS * 2 * HEAD_DIM
V_WIDTH = N_HEADS * 2 * HEAD_DIM
Q_BLOCK = 128
REL_BUCKETS = 32
REL_MAX_DIST = 128
N_EXPERTS = 32
TOP_K = 4
D_FF = D_MODEL
SWIGLU_LIMIT = 7.0
SWIGLU_ALPHA = 1.702
MOE_BLOCK = 128
LN_EPS = 1e-5
DEEPNORM_ALPHA = (2 * DEPTH) ** 0.25
DEEPNORM_BETA = (8 * DEPTH) ** -0.25

kernel_name = 'yoco_conformer_diffattn_moe_trunk'


def layer_norm(x, g, b):
    xf = x.astype(jnp.float32)
    mu = xf.mean(-1, keepdims=True)
    var = jnp.square(xf - mu).mean(-1, keepdims=True)
    return ((xf - mu) * lax.rsqrt(var + LN_EPS) * g + b).astype(x.dtype)


def modulate(x, shift, scale):
    return x * (1 + scale[:, None, :]) + shift[:, None, :]


def conformer_conv(h, w_pw1, b_pw1, w_dw, b_dw, ln_g, ln_b, w_pw2, b_pw2):
    u = h @ w_pw1 + b_pw1
    a, g = jnp.split(u, 2, axis=-1)
    u = a * jax.nn.sigmoid(g)
    u = lax.conv_general_dilated(
        u, w_dw[:, None, :].astype(u.dtype), window_strides=(1,),
        padding=[(CONV_WIDTH - 1, 0)],
        dimension_numbers=('NWC', 'WIO', 'NWC'),
        feature_group_count=u.shape[-1]) + b_dw
    u = jax.nn.silu(layer_norm(u, ln_g, ln_b))
    return u @ w_pw2 + b_pw2


def t5_bucket(rel):
    nb = REL_BUCKETS // 2
    ret = jnp.where(rel > 0, nb, 0)
    n = jnp.abs(rel)
    max_exact = nb // 2
    large = max_exact + (jnp.log(jnp.maximum(n, 1).astype(jnp.float32) / max_exact)
                         / math.log(REL_MAX_DIST / max_exact) * (nb - max_exact)).astype(jnp.int32)
    large = jnp.minimum(large, nb - 1)
    return ret + jnp.where(n < max_exact, n, large)


def diff_attention(h, k, v, w_q, lam_p, subln_g, w_o, rel_table, lambda_init):
    Bn, Sn, _ = h.shape
    nqb = Sn // Q_BLOCK
    q = (h @ w_q).reshape(Bn, nqb, Q_BLOCK, N_HEADS, 2, HEAD_DIM)
    q = q.transpose(1, 0, 3, 4, 2, 5)
    lp = lam_p.astype(jnp.float32)
    lam = jnp.exp(jnp.sum(lp[0] * lp[1])) - jnp.exp(jnp.sum(lp[2] * lp[3])) + lambda_init
    scale = HEAD_DIM ** -0.5
    kpos = jnp.arange(Sn)

    def block(args):
        qb, i = args
        qpos = i * Q_BLOCK + jnp.arange(Q_BLOCK)
        s = jnp.einsum('bhmqd,bhmkd->bhmqk', qb, k).astype(jnp.float32) * scale
        bias = jnp.take(rel_table, t5_bucket(kpos[None, :] - qpos[:, None]), axis=0)
        s = s + bias.transpose(2, 0, 1).astype(jnp.float32)[None, :, None]
        mask = (kpos[None, :] // CHUNK) <= (qpos[:, None] // CHUNK)
        s = jnp.where(mask, s, -jnp.inf)
        p = jax.nn.softmax(s, axis=-1)
        attn = p[:, :, 0] - lam * p[:, :, 1]
        return jnp.einsum('bhqk,bhke->bqhe', attn.astype(v.dtype), v)

    o = lax.map(block, (q, jnp.arange(nqb)))
    o = o.transpose(1, 0, 2, 3, 4).reshape(Bn, Sn, N_HEADS, 2 * HEAD_DIM)
    of = o.astype(jnp.float32)
    of = of * lax.rsqrt(jnp.mean(jnp.square(of), -1, keepdims=True) + LN_EPS) * subln_g
    of = of * (1.0 - lambda_init)
    return of.astype(h.dtype).reshape(Bn, Sn, N_HEADS * 2 * HEAD_DIM) @ w_o


def clamped_swiglu(gu):
    gate, lin = jnp.split(gu, 2, axis=-1)
    gate = jnp.minimum(gate, SWIGLU_LIMIT)
    lin = jnp.clip(lin, -SWIGLU_LIMIT, SWIGLU_LIMIT)
    return gate * jax.nn.sigmoid(SWIGLU_ALPHA * gate) * (lin + 1)


def moe_ffn(h, w_r, b_r, w_gu, b_gu, w_dn, b_dn):
    Bn, Sn, D = h.shape
    T = Bn * Sn
    hf = h.reshape(T, D)
    logits = (hf @ w_r + b_r).astype(jnp.float32)
    top_logit, top_idx = lax.top_k(logits, TOP_K)
    gates = jax.nn.softmax(top_logit, axis=-1)
    n_assign = T * TOP_K
    flat_e = top_idx.reshape(-1)
    flat_tok = jnp.arange(n_assign, dtype=jnp.int32) // TOP_K
    flat_g = gates.reshape(-1)
    order = jnp.argsort(flat_e)
    sorted_e = flat_e[order]
    counts = jnp.bincount(flat_e, length=N_EXPERTS)
    padded = (counts + MOE_BLOCK - 1) // MOE_BLOCK * MOE_BLOCK
    start = jnp.cumsum(counts) - counts
    pend = jnp.cumsum(padded)
    pstart = pend - padded
    ppos = pstart[sorted_e] + jnp.arange(n_assign) - start[sorted_e]
    n_rows = n_assign + N_EXPERTS * MOE_BLOCK
    n_blocks = n_rows // MOE_BLOCK
    row_tok = jnp.zeros((n_rows,), jnp.int32).at[ppos].set(flat_tok[order])
    row_g = jnp.zeros((n_rows,), h.dtype).at[ppos].set(flat_g[order].astype(h.dtype))
    block_e = jnp.minimum(
        jnp.searchsorted(pend, jnp.arange(n_blocks) * MOE_BLOCK, side='right'), N_EXPERTS - 1)
    xs = hf[row_tok].reshape(n_blocks, MOE_BLOCK, D)

    def expert_block(args):
        xb, e = args
        gu = xb @ w_gu[e] + b_gu[e]
        return clamped_swiglu(gu) @ w_dn[e] + b_dn[e]

    ys = lax.map(expert_block, (xs, block_e)).reshape(n_rows, D) * row_g[:, None]
    out = jnp.zeros((T, D), h.dtype).at[row_tok].add(ys)
    return out.reshape(Bn, Sn, D)


def setup_inputs(seed: int = 0) -> dict:
    key = jax.random.key(seed)
    ks = jax.random.split(key, 32)
    D = D_MODEL
    f32 = jnp.float32

    def nrm(k, shape, s):
        return jax.random.normal(k, shape, f32) * s

    w_k = nrm(ks[14], (D, QK_WIDTH), D ** -0.5)
    w_v = nrm(ks[15], (D, V_WIDTH), D ** -0.5 * DEEPNORM_BETA)
    return {
        'x': nrm(ks[0], (BATCH, SEQ, D), 1.0),
        'c': nrm(ks[1], (BATCH, D), 1.0),
        'ada_w': nrm(ks[2], (DEPTH, D, 6 * D), D ** -0.5),
        'ada_b': nrm(ks[3], (DEPTH, 6 * D), 0.02),
        'post_ln_g': 1.0 + nrm(ks[4], (DEPTH, 2, D), 0.02),
        'post_ln_b': nrm(ks[5], (DEPTH, 2, D), 0.02),
        'conv_w_pw1': nrm(ks[6], (N_A_LAYERS, D, 2 * D), D ** -0.5),
        'conv_b_pw1': nrm(ks[7], (N_A_LAYERS, 2 * D), 0.02),
        'conv_w_dw': nrm(ks[8], (N_A_LAYERS, CONV_WIDTH, D), CONV_WIDTH ** -0.5),
        'conv_b_dw': nrm(ks[9], (N_A_LAYERS, D), 0.02),
        'conv_ln_g': 1.0 + nrm(ks[10], (N_A_LAYERS, D), 0.02),
        'conv_ln_b': nrm(ks[11], (N_A_LAYERS, D), 0.02),
        'conv_w_pw2': nrm(ks[12], (N_A_LAYERS, D, D), D ** -0.5 * DEEPNORM_BETA),
        'conv_b_pw2': nrm(ks[13], (N_A_LAYERS, D), 0.02),
        'w_kv': jnp.concatenate([w_k, w_v], axis=-1),
        'attn_w_q': nrm(ks[16], (N_B_LAYERS, D, QK_WIDTH), D ** -0.5),
        'attn_lambda': nrm(ks[17], (N_B_LAYERS, 4, HEAD_DIM), 0.1),
        'attn_subln_g': 1.0 + nrm(ks[18], (N_B_LAYERS, 2 * HEAD_DIM), 0.02),
        'attn_w_o': nrm(ks[19], (N_B_LAYERS, V_WIDTH, D), V_WIDTH ** -0.5 * DEEPNORM_BETA),
        'rel_bias_table': nrm(ks[20], (REL_BUCKETS, N_HEADS), 0.5),
        'router_w': nrm(ks[21], (DEPTH, D, N_EXPERTS), D ** -0.5),
        'router_b': nrm(ks[22], (DEPTH, N_EXPERTS), 0.01),
        'expert_w_gate_up': nrm(ks[23], (DEPTH, N_EXPERTS, D, 2 * D_FF), D ** -0.5),
        'expert_b_gate_up': nrm(ks[24], (DEPTH, N_EXPERTS, 2 * D_FF), 0.01),
        'expert_w_down': nrm(ks[25], (DEPTH, N_EXPERTS, D_FF, D), D_FF ** -0.5 * DEEPNORM_BETA),
        'expert_b_down': nrm(ks[26], (DEPTH, N_EXPERTS, D), 0.01),
    }


def reference(x, c, ada_w, ada_b, post_ln_g, post_ln_b, conv_w_pw1, conv_b_pw1, conv_w_dw,
              conv_b_dw, conv_ln_g, conv_ln_b, conv_w_pw2, conv_b_pw2, w_kv, attn_w_q,
              attn_lambda, attn_subln_g, attn_w_o, rel_bias_table, router_w, router_b,
              expert_w_gate_up, expert_b_gate_up, expert_w_down, expert_b_down):
    Bn, Sn, _ = x.shape
    cond = jax.nn.silu(c)
    k_sh = None
    v_sh = None
    for l in range(DEPTH):
        mods = cond @ ada_w[l] + ada_b[l]
        sh1, sc1, g1, sh2, sc2, g2 = jnp.split(mods, 6, axis=-1)
        h = modulate(x, sh1, sc1)
        if l < N_A_LAYERS:
            i = l
            y = conformer_conv(h, conv_w_pw1[i], conv_b_pw1[i], conv_w_dw[i], conv_b_dw[i],
                               conv_ln_g[i], conv_ln_b[i], conv_w_pw2[i], conv_b_pw2[i])
        else:
            j = l - N_A_LAYERS
            if j == 0:
                kv = x @ w_kv
                k_sh = kv[..., :QK_WIDTH].reshape(Bn, Sn, N_HEADS, 2, HEAD_DIM).transpose(0, 2, 3, 1, 4)
                v_sh = kv[..., QK_WIDTH:].reshape(Bn, Sn, N_HEADS, 2 * HEAD_DIM).transpose(0, 2, 1, 3)
            lambda_init = 0.8 - 0.6 * math.exp(-0.3 * l)
            y = diff_attention(h, k_sh, v_sh, attn_w_q[j], attn_lambda[j], attn_subln_g[j],
                               attn_w_o[j], rel_bias_table, lambda_init)
        x = layer_norm(DEEPNORM_ALPHA * x + g1[:, None, :] * y, post_ln_g[l, 0], post_ln_b[l, 0])
        h = modulate(x, sh2, sc2)
        y = moe_ffn(h, router_w[l], router_b[l], expert_w_gate_up[l], expert_b_gate_up[l],
                    expert_w_down[l], expert_b_down[l])
        x = layer_norm(DEEPNORM_ALPHA * x + g2[:, None, :] * y, post_ln_g[l, 1], post_ln_b[l, 1])
    return x
```

```python
import functools
import math

import jax
import jax.numpy as jnp
from jax import lax
from jax.experimental import pallas as pl
from jax.experimental.pallas import tpu as pltpu

F32 = jnp.float32
BF16 = jnp.bfloat16
I32 = jnp.int32
HIGHEST = lax.Precision.HIGHEST

CHUNK = 64
CONV_WIDTH = 31
HEAD_DIM = 64
REL_BUCKETS = 32
REL_MAX_DIST = 128
TOP_K = 4
SWIGLU_LIMIT = 7.0
SWIGLU_ALPHA = 1.702
LN_EPS = 1e-5
MASK_VALUE = -1e30

SUBLANES = 8
LANES = 128
VMEM_LIMIT_BYTES = 56 * 1024 * 1024

ADA_TN = 1024
SEQ_TILE = 256
CONV_HALO = 32
CONV_ROWS = 64
CONV_COLS = 256
ROUTER_TILE = 512
DISPATCH_TILE = 256
MOE_TILE = 256
COMBINE_TILE = 256
Q_BLOCK = 128
CAST_ROWS = 128


def _params(*sem):
    return pltpu.CompilerParams(dimension_semantics=sem, vmem_limit_bytes=VMEM_LIMIT_BYTES)


def _layer_norm(x, g, b):
    mu = jnp.mean(x, axis=-1, keepdims=True)
    xc = x - mu
    var = jnp.mean(xc * xc, axis=-1, keepdims=True)
    return xc * lax.rsqrt(var + LN_EPS) * g + b


def _dot(a, b):
    return jnp.dot(a, b, preferred_element_type=F32)


def _ada_body(c_ref, w_ref, b_ref, o_ref):
    c = c_ref[...]
    cond = c * jax.nn.sigmoid(c)
    o_ref[...] = jnp.dot(cond, w_ref[...], preferred_element_type=F32, precision=HIGHEST) + b_ref[...]


def _ada(c, ada_w, ada_b):
    depth, d, n = ada_w.shape
    bsz = c.shape[0]
    tn = min(ADA_TN, n)
    return pl.pallas_call(
        _ada_body,
        grid=(depth, n // tn),
        in_specs=[
            pl.BlockSpec((bsz, d), lambda l, j: (0, 0)),
            pl.BlockSpec((None, d, tn), lambda l, j: (l, 0, j)),
            pl.BlockSpec((None, 1, tn), lambda l, j: (l, 0, j)),
        ],
        out_specs=pl.BlockSpec((None, bsz, tn), lambda l, j: (l, 0, j)),
        out_shape=jax.ShapeDtypeStruct((depth, bsz, n), F32),
        compiler_params=_params("parallel", "parallel"),
        name="ada",
    )(c, ada_w, ada_b.reshape(depth, 1, n))


def _conv_a_body(x_ref, mod_ref, w_ref, b_ref, u_ref):
    d = x_ref.shape[-1]
    h = (x_ref[...] * (1.0 + mod_ref[1:2, :]) + mod_ref[0:1, :]).astype(BF16)
    a = _dot(h, w_ref[:, :d]) + b_ref[:, :d]
    g = _dot(h, w_ref[:, d:]) + b_ref[:, d:]
    u_ref[...] = a * jax.nn.sigmoid(g)


def _conv_a(x, mods_l, w_pw1, b_pw1):
    bsz, s, d = x.shape
    ts = min(SEQ_TILE, s)
    return pl.pallas_call(
        _conv_a_body,
        grid=(bsz, s // ts),
        in_specs=[
            pl.BlockSpec((None, ts, d), lambda b, i: (b, i, 0)),
            pl.BlockSpec((None, 6, d), lambda b, i: (b, 0, 0)),
            pl.BlockSpec((d, 2 * d), lambda b, i: (0, 0)),
            pl.BlockSpec((1, 2 * d), lambda b, i: (0, 0)),
        ],
        out_specs=pl.BlockSpec((None, ts, d), lambda b, i: (b, i, 0)),
        out_shape=jax.ShapeDtypeStruct((bsz, s, d), F32),
        compiler_params=_params("parallel", "parallel"),
        name="conv_a",
    )(x, mods_l, w_pw1.astype(BF16), b_pw1.reshape(1, 2 * d))


def _residual_epilogue(x, y, mod_ref, lng_ref, lnb_ref, alpha, x1_ref, h2_ref):
    x1 = _layer_norm(alpha * x + mod_ref[2:3, :] * y, lng_ref[...], lnb_ref[...])
    x1_ref[...] = x1
    h2_ref[...] = x1 * (1.0 + mod_ref[4:5, :]) + mod_ref[3:4, :]


def _conv_b_body(u_ref, halo_ref, x_ref, mod_ref, wdw_ref, bdw_ref, cg_ref, cb_ref, w2_ref, b2_ref,
                 lng_ref, lnb_ref, x1_ref, h2_ref, win_ref, v_ref, *, alpha):
    ts, d = u_ref.shape
    i = pl.program_id(1)
    halo = halo_ref[...]
    win_ref[0:CONV_HALO, :] = jnp.where(i == 0, jnp.zeros_like(halo), halo)
    win_ref[CONV_HALO:, :] = u_ref[...]
    off = CONV_HALO - (CONV_WIDTH - 1)
    rows = min(CONV_ROWS, ts)
    cols = min(CONV_COLS, d)
    for c0 in range(0, d, cols):
        for r0 in range(0, ts, rows):
            acc = jnp.zeros((rows, cols), F32)
            for j in range(CONV_WIDTH):
                acc = acc + wdw_ref[j:j + 1, c0:c0 + cols] * win_ref[r0 + off + j:r0 + off + j + rows, c0:c0 + cols]
            v_ref[r0:r0 + rows, c0:c0 + cols] = acc
    v = _layer_norm(v_ref[...] + bdw_ref[...], cg_ref[...], cb_ref[...])
    v = (v * jax.nn.sigmoid(v)).astype(BF16)
    y = _dot(v, w2_ref[...]) + b2_ref[...]
    _residual_epilogue(x_ref[...], y, mod_ref, lng_ref, lnb_ref, alpha, x1_ref, h2_ref)


def _conv_b(u, x, mods_l, w_dw, b_dw, cln_g, cln_b, w_pw2, b_pw2, ln_g, ln_b, alpha):
    bsz, s, d = x.shape
    ts = min(SEQ_TILE, s)
    hb = ts // CONV_HALO
    row = lambda a: a.reshape(1, d)
    tile = pl.BlockSpec((None, ts, d), lambda b, i: (b, i, 0))
    vec = pl.BlockSpec((1, d), lambda b, i: (0, 0))
    return pl.pallas_call(
        functools.partial(_conv_b_body, alpha=alpha),
        grid=(bsz, s // ts),
        in_specs=[
            tile,
            pl.BlockSpec((None, CONV_HALO, d), lambda b, i: (b, jnp.maximum(i * hb - 1, 0), 0)),
            tile,
            pl.BlockSpec((None, 6, d), lambda b, i: (b, 0, 0)),
            pl.BlockSpec((CONV_WIDTH, d), lambda b, i: (0, 0)),
            vec, vec, vec,
            pl.BlockSpec((d, d), lambda b, i: (0, 0)),
            vec, vec, vec,
        ],
        out_specs=[tile, tile],
        out_shape=[jax.ShapeDtypeStruct((bsz, s, d), F32)] * 2,
        scratch_shapes=[pltpu.VMEM((ts + CONV_HALO, d), F32), pltpu.VMEM((ts, d), F32)],
        compiler_params=_params("parallel", "parallel"),
        name="conv_b",
    )(u, u, x, mods_l, w_dw, row(b_dw), row(cln_g), row(cln_b), w_pw2.astype(BF16), row(b_pw2),
      row(ln_g), row(ln_b))


def _qkv_body(x_ref, mod_ref, wkv_ref, wq_ref, q_ref, k_ref, v_ref):
    d = x_ref.shape[-1]
    x = x_ref[...]
    kv = _dot(x.astype(BF16), wkv_ref[...])
    k_ref[...] = kv[:, :d].astype(BF16)
    v_ref[...] = kv[:, d:].astype(BF16)
    h = (x * (1.0 + mod_ref[1:2, :]) + mod_ref[0:1, :]).astype(BF16)
    q_ref[...] = (_dot(h, wq_ref[...]) * (HEAD_DIM ** -0.5)).astype(BF16)


def _qkv(x, mods_l, w_kv, w_q):
    bsz, s, d = x.shape
    ts = min(SEQ_TILE, s)
    tile = pl.BlockSpec((None, ts, d), lambda b, i: (b, i, 0))
    return pl.pallas_call(
        _qkv_body,
        grid=(bsz, s // ts),
        in_specs=[
            tile,
            pl.BlockSpec((None, 6, d), lambda b, i: (b, 0, 0)),
            pl.BlockSpec((d, 2 * d), lambda b, i: (0, 0)),
            pl.BlockSpec((d, d), lambda b, i: (0, 0)),
        ],
        out_specs=[tile, tile, tile],
        out_shape=[jax.ShapeDtypeStruct((bsz, s, d), BF16)] * 3,
        compiler_params=_params("parallel", "parallel"),
        name="qkv",
    )(x, mods_l, w_kv.astype(BF16), w_q.astype(BF16))


def _t5_bucket(rel):
    nb = REL_BUCKETS // 2
    ret = jnp.where(rel > 0, nb, 0)
    n = jnp.abs(rel)
    max_exact = nb // 2
    large = max_exact + (jnp.log(jnp.maximum(n, 1).astype(F32) / max_exact)
                         / math.log(REL_MAX_DIST / max_exact) * (nb - max_exact)).astype(I32)
    large = jnp.minimum(large, nb - 1)
    return ret + jnp.where(n < max_exact, n, large)


def _bucket_strip(s):
    r = jnp.arange(Q_BLOCK, dtype=I32)[:, None]
    kp = jnp.arange(s, dtype=I32)[None, :] - (s - Q_BLOCK)
    bucket = _t5_bucket(kp - r)
    visible = jnp.floor_divide(kp, CHUNK) <= (r // CHUNK)
    return jnp.where(visible, bucket, REL_BUCKETS)


def _attn_body(tab_ref, q_ref, k_ref, v_ref, bkt_ref, lam_ref, sg_ref, o_ref, bias_ref, *, lambda_init, n_heads):
    h = pl.program_id(0)
    b = pl.program_id(1)
    s = q_ref.shape[0]

    @pl.when(b == 0)
    def _():
        bk = bkt_ref[...]
        acc = jnp.full(bk.shape, MASK_VALUE, F32)
        for r in range(REL_BUCKETS):
            acc = jnp.where(bk == r, tab_ref[r * n_heads + h], acc)
        bias_ref[...] = acc

    lp = lam_ref[...]
    lam = (jnp.exp(jnp.sum(lp[0:1, :] * lp[1:2, :], axis=-1, keepdims=True))
           - jnp.exp(jnp.sum(lp[2:3, :] * lp[3:4, :], axis=-1, keepdims=True)) + lambda_init)
    lane = lax.broadcasted_iota(I32, (Q_BLOCK, 2 * HEAD_DIM), 1)
    nt = (((1,), (1,)), ((), ()))
    for i in range(s // Q_BLOCK):
        n_keys = (i + 1) * Q_BLOCK
        q = q_ref[i * Q_BLOCK:(i + 1) * Q_BLOCK, :]
        qq = jnp.concatenate([jnp.where(lane < HEAD_DIM, q, jnp.zeros_like(q)),
                              jnp.where(lane >= HEAD_DIM, q, jnp.zeros_like(q))], axis=0)
        sc = lax.dot_general(qq, k_ref[0:n_keys, :], nt, preferred_element_type=F32)
        bias = bias_ref[:, s - n_keys:s]
        s0 = sc[:Q_BLOCK] + bias
        s1 = sc[Q_BLOCK:] + bias
        p0 = jnp.exp(s0 - jnp.max(s0, axis=-1, keepdims=True))
        p1 = jnp.exp(s1 - jnp.max(s1, axis=-1, keepdims=True))
        w0 = 1.0 / jnp.sum(p0, axis=-1, keepdims=True)
        w1 = lam / jnp.sum(p1, axis=-1, keepdims=True)
        attn = (p0 * w0 - p1 * w1).astype(BF16)
        o = _dot(attn, v_ref[0:n_keys, :])
        o = o * lax.rsqrt(jnp.mean(o * o, axis=-1, keepdims=True) + LN_EPS) * sg_ref[...]
        o_ref[i * Q_BLOCK:(i + 1) * Q_BLOCK, :] = (o * (1.0 - lambda_init)).astype(BF16)


def _attn(q, k, v, lam_p, subln_g, rel_table, lambda_init):
    bsz, s, d = q.shape
    hd2 = 2 * HEAD_DIM
    n_heads = d // hd2
    head = pl.BlockSpec((None, s, hd2), lambda h, b, tab: (b, 0, h))
    grid_spec = pltpu.PrefetchScalarGridSpec(
        num_scalar_prefetch=1,
        grid=(n_heads, bsz),
        in_specs=[
            head, head, head,
            pl.BlockSpec((Q_BLOCK, s), lambda h, b, tab: (0, 0)),
            pl.BlockSpec((4, HEAD_DIM), lambda h, b, tab: (0, 0)),
            pl.BlockSpec((1, hd2), lambda h, b, tab: (0, 0)),
        ],
        out_specs=head,
        scratch_shapes=[pltpu.VMEM((Q_BLOCK, s), F32)],
    )
    return pl.pallas_call(
        functools.partial(_attn_body, lambda_init=lambda_init, n_heads=n_heads),
        grid_spec=grid_spec,
        out_shape=jax.ShapeDtypeStruct((bsz, s, d), BF16),
        compiler_params=_params("arbitrary", "arbitrary"),
        name="attn",
    )(rel_table.reshape(-1), q, k, v, _bucket_strip(s), lam_p, subln_g.reshape(1, hd2))


def _attn_out_body(o_ref, x_ref, mod_ref, wo_ref, lng_ref, lnb_ref, x1_ref, h2_ref, *, alpha):
    y = _dot(o_ref[...], wo_ref[...])
    _residual_epilogue(x_ref[...], y, mod_ref, lng_ref, lnb_ref, alpha, x1_ref, h2_ref)


def _attn_out(o, x, mods_l, w_o, ln_g, ln_b, alpha):
    bsz, s, d = x.shape
    ts = min(SEQ_TILE, s)
    tile = pl.BlockSpec((None, ts, d), lambda b, i: (b, i, 0))
    vec = pl.BlockSpec((1, d), lambda b, i: (0, 0))
    return pl.pallas_call(
        functools.partial(_attn_out_body, alpha=alpha),
        grid=(bsz, s // ts),
        in_specs=[tile, tile, pl.BlockSpec((None, 6, d), lambda b, i: (b, 0, 0)),
                  pl.BlockSpec((d, d), lambda b, i: (0, 0)), vec, vec],
        out_specs=[tile, tile],
        out_shape=[jax.ShapeDtypeStruct((bsz, s, d), F32)] * 2,
        compiler_params=_params("parallel", "parallel"),
        name="attn_out",
    )(o, x, mods_l, w_o.astype(BF16), ln_g.reshape(1, d), ln_b.reshape(1, d))


def _router_body(h_ref, wt_ref, b_ref, tri_ref, idx_ref, gate_ref, rank_ref, cnt_ref, base_ref):
    @pl.when(pl.program_id(0) == 0)
    def _():
        base_ref[...] = jnp.zeros_like(base_ref)

    logits = lax.dot_general(wt_ref[...], h_ref[...], (((1,), (1,)), ((), ())),
                             preferred_element_type=F32, precision=HIGHEST) + b_ref[...]
    n_exp, tr = logits.shape
    eio = lax.broadcasted_iota(I32, (n_exp, tr), 0)
    work = logits
    vals, idxs = [], []
    for _ in range(TOP_K):
        m = jnp.max(work, axis=0, keepdims=True)
        am = jnp.min(jnp.where(work == m, eio, n_exp), axis=0, keepdims=True)
        vals.append(m)
        idxs.append(am)
        work = jnp.where(eio == am, -jnp.inf, work)
    ex = [jnp.exp(v - vals[0]) for v in vals]
    den = ex[0] + ex[1] + ex[2] + ex[3]
    onehot = jnp.zeros((n_exp, tr), F32)
    for k in range(TOP_K):
        onehot = onehot + (eio == idxs[k]).astype(F32)
    before = _dot(onehot.astype(BF16), tri_ref[...]) + base_ref[...]
    for k in range(TOP_K):
        idx_ref[k:k + 1, :] = idxs[k]
        gate_ref[k:k + 1, :] = ex[k] / den
        rank_ref[k:k + 1, :] = jnp.sum(jnp.where(eio == idxs[k], before, 0.0), axis=0, keepdims=True).astype(I32)
    base_ref[...] = base_ref[...] + jnp.sum(onehot, axis=1, keepdims=True)
    cnt_ref[...] = base_ref[...].astype(I32)


def _router(h2, w_r, b_r):
    t, d = h2.shape
    n_exp = w_r.shape[1]
    tr = min(ROUTER_TILE, t)
    pos = jnp.arange(tr, dtype=I32)
    tri = (pos[:, None] < pos[None, :]).astype(BF16)
    tok = pl.BlockSpec((TOP_K, tr), lambda i: (0, i))
    return pl.pallas_call(
        _router_body,
        grid=(t // tr,),
        in_specs=[
            pl.BlockSpec((tr, d), lambda i: (i, 0)),
            pl.BlockSpec((n_exp, d), lambda i: (0, 0)),
            pl.BlockSpec((n_exp, 1), lambda i: (0, 0)),
            pl.BlockSpec((tr, tr), lambda i: (0, 0)),
        ],
        out_specs=[tok, tok, tok, pl.BlockSpec((n_exp, 1), lambda i: (0, 0))],
        out_shape=[jax.ShapeDtypeStruct((TOP_K, t), I32), jax.ShapeDtypeStruct((TOP_K, t), F32),
                   jax.ShapeDtypeStruct((TOP_K, t), I32), jax.ShapeDtypeStruct((n_exp, 1), I32)],
        scratch_shapes=[pltpu.VMEM((n_exp, 1), F32)],
        compiler_params=_params("arbitrary"),
        name="router",
    )(h2, w_r.T, b_r.reshape(n_exp, 1), tri)


def _row_copy(src, src_row, dst, dst_row, sem):
    return pltpu.make_async_copy(src.at[pl.ds(src_row, 1)], dst.at[pl.ds(dst_row, 1)], sem)


def _zero_fill(zero_ref, xs_hbm, start, cnt, sem, wait):
    zr = zero_ref.shape[0]

    def copy(rows, dst_row):
        dma = pltpu.make_async_copy(zero_ref.at[pl.ds(0, rows)], xs_hbm.at[pl.ds(dst_row, rows)], sem)
        dma.wait() if wait else dma.start()

    head = jnp.minimum((-start) % SUBLANES, cnt)
    for r in range(SUBLANES - 1):
        @pl.when(r < head)
        def _(r=r):
            copy(1, start + r)

    def full(r, c):
        copy(zr, pl.multiple_of(start + head + r * zr, SUBLANES))
        return c

    n_full = (cnt - head) // zr
    lax.fori_loop(0, n_full, full, 0)
    rem = cnt - head - n_full * zr
    base = start + head + n_full * zr
    bit = zr // 2
    while bit >= SUBLANES:
        @pl.when((rem & bit) != 0)
        def _(bit=bit):
            copy(bit, pl.multiple_of(base + (rem // (2 * bit)) * (2 * bit), SUBLANES))
        bit //= 2


def _dispatch_body(ppos_ref, pad_start_ref, pad_cnt_ref, h_hbm, xs_hbm, zero_ref, sem, zsem, *, n_tok, td):
    i = pl.program_id(0)

    @pl.when(i == 0)
    def _():
        zero_ref[...] = jnp.zeros_like(zero_ref)
        for wait in (False, True):
            def per_range(e, carry, wait=wait):
                _zero_fill(zero_ref, xs_hbm, pad_start_ref[e], pad_cnt_ref[e], zsem, wait)
                return carry
            lax.fori_loop(0, pad_start_ref.shape[0], per_range, 0)

    def issue(t, carry):
        tok = i * td + t
        for k in range(TOP_K):
            _row_copy(h_hbm, tok, xs_hbm, ppos_ref[k * n_tok + tok], sem).start()
        return carry

    lax.fori_loop(0, td, issue, 0)
    pltpu.make_async_copy(xs_hbm.at[pl.ds(0, TOP_K * td)], xs_hbm.at[pl.ds(0, TOP_K * td)], sem).wait()


def _dispatch(h2, ppos_flat, pad_start, pad_cnt, n_rows):
    t, d = h2.shape
    td = min(DISPATCH_TILE, t)
    grid_spec = pltpu.PrefetchScalarGridSpec(
        num_scalar_prefetch=3,
        grid=(t // td,),
        in_specs=[pl.BlockSpec(memory_space=pl.ANY)],
        out_specs=pl.BlockSpec(memory_space=pl.ANY),
        scratch_shapes=[pltpu.VMEM((MOE_TILE, d), F32), pltpu.SemaphoreType.DMA, pltpu.SemaphoreType.DMA],
    )
    return pl.pallas_call(
        functools.partial(_dispatch_body, n_tok=t, td=td),
        grid_spec=grid_spec,
        out_shape=jax.ShapeDtypeStruct((n_rows, d), F32),
        compiler_params=_params("arbitrary"),
        name="dispatch",
    )(ppos_flat, pad_start, pad_cnt, h2)


def _experts_body(be_ref, bsrc_ref, nvalid_ref, xs_ref, wgu_ref, bgu_ref, wdn_ref, bdn_ref, ys_ref,
                  wgu_bf, wdn_bf):
    i = pl.program_id(0)
    e = be_ref[i]
    prev = be_ref[jnp.maximum(i - 1, 0)]
    f = wdn_ref.shape[0]

    @pl.when((i == 0) | (e != prev))
    def _():
        def cast(ref_in, ref_out):
            def step(r, c):
                rows = pl.ds(pl.multiple_of(r * CAST_ROWS, CAST_ROWS), CAST_ROWS)
                ref_out[rows, :] = ref_in[rows, :].astype(BF16)
                return c
            lax.fori_loop(0, ref_in.shape[0] // CAST_ROWS, step, 0)
        cast(wgu_ref, wgu_bf)
        cast(wdn_ref, wdn_bf)

    @pl.when(i < nvalid_ref[0])
    def _():
        x = xs_ref[...].astype(BF16)
        gate = jnp.minimum(_dot(x, wgu_bf[:, :f]) + bgu_ref[:, :f], SWIGLU_LIMIT)
        lin = jnp.clip(_dot(x, wgu_bf[:, f:]) + bgu_ref[:, f:], -SWIGLU_LIMIT, SWIGLU_LIMIT)
        act = (gate * jax.nn.sigmoid(SWIGLU_ALPHA * gate) * (lin + 1.0)).astype(BF16)
        ys_ref[...] = _dot(act, wdn_bf[...]) + bdn_ref[...]

    @pl.when(i >= nvalid_ref[0])
    def _():
        ys_ref[...] = jnp.zeros_like(ys_ref)


def _experts(xs, block_e, block_src, n_valid, w_gu, b_gu, w_dn, b_dn):
    n_rows, d = xs.shape
    n_exp, _, f2 = w_gu.shape
    f = f2 // 2
    tm = MOE_TILE
    grid_spec = pltpu.PrefetchScalarGridSpec(
        num_scalar_prefetch=3,
        grid=(n_rows // tm,),
        in_specs=[
            pl.BlockSpec((tm, d), lambda i, be, bs, nv: (bs[i], 0)),
            pl.BlockSpec((None, d, f2), lambda i, be, bs, nv: (be[i], 0, 0)),
            pl.BlockSpec((None, 1, f2), lambda i, be, bs, nv: (be[i], 0, 0)),
            pl.BlockSpec((None, f, d), lambda i, be, bs, nv: (be[i], 0, 0)),
            pl.BlockSpec((None, 1, d), lambda i, be, bs, nv: (be[i], 0, 0)),
        ],
        out_specs=pl.BlockSpec((tm, d), lambda i, be, bs, nv: (i, 0)),
        scratch_shapes=[pltpu.VMEM((d, f2), BF16), pltpu.VMEM((f, d), BF16)],
    )
    return pl.pallas_call(
        _experts_body,
        grid_spec=grid_spec,
        out_shape=jax.ShapeDtypeStruct((n_rows, d), F32),
        compiler_params=_params("arbitrary"),
        name="experts",
    )(block_e, block_src, n_valid, xs, w_gu, b_gu.reshape(n_exp, 1, f2), w_dn, b_dn.reshape(n_exp, 1, d))


def _combine_body(ppos_ref, ys_hbm, x_ref, gate_ref, mod_ref, lng_ref, lnb_ref, o_ref, buf_ref, sem,
                  *, n_tok, tt, alpha):
    i = pl.program_id(0)

    def issue(t, carry):
        tok = i * tt + t
        for k in range(TOP_K):
            _row_copy(ys_hbm, ppos_ref[k * n_tok + tok], buf_ref.at[k], t, sem).start()
        return carry

    lax.fori_loop(0, tt, issue, 0)
    pltpu.make_async_copy(buf_ref, buf_ref, sem).wait()
    gates = gate_ref[...]
    y = gates[:, 0:1] * buf_ref[0]
    for k in range(1, TOP_K):
        y = y + gates[:, k:k + 1] * buf_ref[k]
    o_ref[...] = _layer_norm(alpha * x_ref[...] + mod_ref[5:6, :] * y, lng_ref[...], lnb_ref[...])


def _combine(ys, ppos_flat, x1, gates_t, mods_l, ln_g, ln_b, alpha, seq_len):
    t, d = x1.shape
    tt = min(COMBINE_TILE, seq_len)
    per_seq = seq_len // tt
    vec = pl.BlockSpec((1, d), lambda i, pp: (0, 0))
    grid_spec = pltpu.PrefetchScalarGridSpec(
        num_scalar_prefetch=1,
        grid=(t // tt,),
        in_specs=[
            pl.BlockSpec(memory_space=pl.ANY),
            pl.BlockSpec((tt, d), lambda i, pp: (i, 0)),
            pl.BlockSpec((tt, TOP_K), lambda i, pp: (i, 0)),
            pl.BlockSpec((None, 6, d), lambda i, pp: (i // per_seq, 0, 0)),
            vec, vec,
        ],
        out_specs=pl.BlockSpec((tt, d), lambda i, pp: (i, 0)),
        scratch_shapes=[pltpu.VMEM((TOP_K, tt, d), F32), pltpu.SemaphoreType.DMA],
    )
    return pl.pallas_call(
        functools.partial(_combine_body, n_tok=t, tt=tt, alpha=alpha),
        grid_spec=grid_spec,
        out_shape=jax.ShapeDtypeStruct((t, d), F32),
        compiler_params=_params("arbitrary"),
        name="combine",
    )(ppos_flat, ys, x1, gates_t, mods_l, ln_g.reshape(1, d), ln_b.reshape(1, d))


def _moe_layer(x1, h2, mods_l, w_r, b_r, w_gu, b_gu, w_dn, b_dn, ln_g, ln_b, alpha, seq_len):
    t, d = x1.shape
    n_exp = w_r.shape[1]
    tm = MOE_TILE
    idx, gates, rank, cnt = _router(h2, w_r, b_r)
    counts = cnt[:, 0]
    padded = (counts + tm - 1) // tm * tm
    pend = jnp.cumsum(padded)
    pstart = pend - padded
    ppos_flat = (jnp.take(pstart, idx) + rank).reshape(-1)
    n_blocks = (t * TOP_K) // tm + n_exp
    n_valid = pend[-1] // tm
    blk = jnp.arange(n_blocks, dtype=I32)
    block_src = jnp.minimum(blk, n_valid - 1)
    block_e = jnp.minimum(jnp.searchsorted(pend, block_src * tm, side='right'), n_exp - 1).astype(I32)
    n_rows = n_blocks * tm
    pad_start = jnp.concatenate([pstart + counts, pend[-1:]]).astype(I32)
    pad_cnt = jnp.concatenate([padded - counts, n_rows - pend[-1:]]).astype(I32)
    xs = _dispatch(h2, ppos_flat, pad_start, pad_cnt, n_rows)
    ys = _experts(xs, block_e, block_src.astype(I32), n_valid.reshape(1).astype(I32), w_gu, b_gu, w_dn, b_dn)
    return _combine(ys, ppos_flat, x1, gates.T, mods_l, ln_g, ln_b, alpha, seq_len)


def kernel(x, c, ada_w, ada_b, post_ln_g, post_ln_b, conv_w_pw1, conv_b_pw1, conv_w_dw, conv_b_dw, conv_ln_g, conv_ln_b, conv_w_pw2, conv_b_pw2, w_kv, attn_w_q, attn_lambda, attn_subln_g, attn_w_o, rel_bias_table, router_w, router_b, expert_w_gate_up, expert_b_gate_up, expert_w_down, expert_b_down):
    bsz, s, d = x.shape
    depth = ada_w.shape[0]
    n_a = depth // 2
    alpha = (2 * depth) ** 0.25
    mods = _ada(c, ada_w, ada_b).reshape(depth, bsz, 6, d)
    q = k = v = None
    for l in range(depth):
        mods_l = mods[l]
        if l < n_a:
            u = _conv_a(x, mods_l, conv_w_pw1[l], conv_b_pw1[l])
            x1, h2 = _conv_b(u, x, mods_l, conv_w_dw[l], conv_b_dw[l], conv_ln_g[l], conv_ln_b[l],
                             conv_w_pw2[l], conv_b_pw2[l], post_ln_g[l, 0], post_ln_b[l, 0], alpha)
        else:
            j = l - n_a
            if j == 0:
                q, k, v = _qkv(x, mods_l, w_kv, attn_w_q[j])
            else:
                q = _qkv(x, mods_l, w_kv, attn_w_q[j])[0]
            lambda_init = 0.8 - 0.6 * math.exp(-0.3 * l)
            o = _attn(q, k, v, attn_lambda[j], attn_subln_g[j], rel_bias_table, lambda_init)
            x1, h2 = _attn_out(o, x, mods_l, attn_w_o[j], post_ln_g[l, 0], post_ln_b[l, 0], alpha)
        x = _moe_layer(x1.reshape(bsz * s, d), h2.reshape(bsz * s, d), mods_l, router_w[l], router_b[l],
                       expert_w_gate_up[l], expert_b_gate_up[l], expert_w_down[l], expert_b_down[l],
                       post_ln_g[l, 1], post_ln_b[l, 1], alpha, s).reshape(bsz, s, d)
    return x
```

```python
import functools
import math

import jax
import jax.numpy as jnp
from jax import lax
from jax.experimental import pallas as pl
from jax.experimental.pallas import tpu as pltpu

F32 = jnp.float32
BF16 = jnp.bfloat16
I32 = jnp.int32
HIGHEST = lax.Precision.HIGHEST

CHUNK = 64
CONV_WIDTH = 31
HEAD_DIM = 64
REL_BUCKETS = 32
REL_MAX_DIST = 128
TOP_K = 4
SWIGLU_LIMIT = 7.0
SWIGLU_ALPHA = 1.702
LN_EPS = 1e-5
MASK_VALUE = -1e30

SUBLANES = 8
LANES = 128
VMEM_LIMIT_BYTES = 56 * 1024 * 1024

ADA_TN = 1024
SEQ_TILE = 256
CONV_HALO = 32
CONV_ROWS = 64
CONV_COLS = 256
ROUTER_TILE = 512
DISPATCH_TILE = 256
MOE_TILE = 256
COMBINE_TILE = 256
Q_BLOCK = 128
CAST_ROWS = 128


def _params(*sem):
    return pltpu.CompilerParams(dimension_semantics=sem, vmem_limit_bytes=VMEM_LIMIT_BYTES)


def _layer_norm(x, g, b):
    mu = jnp.mean(x, axis=-1, keepdims=True)
    xc = x - mu
    var = jnp.mean(xc * xc, axis=-1, keepdims=True)
    return xc * lax.rsqrt(var + LN_EPS) * g + b


def _dot(a, b):
    return jnp.dot(a, b, preferred_element_type=F32)


def _ada_body(c_ref, w_ref, b_ref, o_ref):
    c = c_ref[...]
    cond = c * jax.nn.sigmoid(c)
    o_ref[...] = jnp.dot(cond, w_ref[...], preferred_element_type=F32, precision=HIGHEST) + b_ref[...]


def _ada(c, ada_w, ada_b):
    depth, d, n = ada_w.shape
    bsz = c.shape[0]
    tn = min(ADA_TN, n)
    return pl.pallas_call(
        _ada_body,
        grid=(depth, n // tn),
        in_specs=[
            pl.BlockSpec((bsz, d), lambda l, j: (0, 0)),
            pl.BlockSpec((None, d, tn), lambda l, j: (l, 0, j)),
            pl.BlockSpec((None, 1, tn), lambda l, j: (l, 0, j)),
        ],
        out_specs=pl.BlockSpec((None, bsz, tn), lambda l, j: (l, 0, j)),
        out_shape=jax.ShapeDtypeStruct((depth, bsz, n), F32),
        compiler_params=_params("parallel", "parallel"),
        name="ada",
    )(c, ada_w, ada_b.reshape(depth, 1, n))


def _conv_a_body(x_ref, mod_ref, w_ref, b_ref, u_ref):
    d = x_ref.shape[-1]
    h = (x_ref[...] * (1.0 + mod_ref[1:2, :]) + mod_ref[0:1, :]).astype(BF16)
    a = _dot(h, w_ref[:, :d]) + b_ref[:, :d]
    g = _dot(h, w_ref[:, d:]) + b_ref[:, d:]
    u_ref[...] = a * jax.nn.sigmoid(g)


def _conv_a(x, mods_l, w_pw1, b_pw1):
    bsz, s, d = x.shape
    ts = min(SEQ_TILE, s)
    return pl.pallas_call(
        _conv_a_body,
        grid=(bsz, s // ts),
        in_specs=[
            pl.BlockSpec((None, ts, d), lambda b, i: (b, i, 0)),
            pl.BlockSpec((None, 6, d), lambda b, i: (b, 0, 0)),
            pl.BlockSpec((d, 2 * d), lambda b, i: (0, 0)),
            pl.BlockSpec((1, 2 * d), lambda b, i: (0, 0)),
        ],
        out_specs=pl.BlockSpec((None, ts, d), lambda b, i: (b, i, 0)),
        out_shape=jax.ShapeDtypeStruct((bsz, s, d), F32),
        compiler_params=_params("parallel", "parallel"),
        name="conv_a",
    )(x, mods_l, w_pw1.astype(BF16), b_pw1.reshape(1, 2 * d))


def _residual_epilogue(x, y, mod_ref, lng_ref, lnb_ref, alpha, x1_ref, h2_ref):
    x1 = _layer_norm(alpha * x + mod_ref[2:3, :] * y, lng_ref[...], lnb_ref[...])
    x1_ref[...] = x1
    h2_ref[...] = x1 * (1.0 + mod_ref[4:5, :]) + mod_ref[3:4, :]


def _conv_b_body(u_ref, halo_ref, x_ref, mod_ref, wdw_ref, bdw_ref, cg_ref, cb_ref, w2_ref, b2_ref,
                 lng_ref, lnb_ref, x1_ref, h2_ref, win_ref, v_ref, *, alpha):
    ts, d = u_ref.shape
    i = pl.program_id(1)
    halo = halo_ref[...]
    win_ref[0:CONV_HALO, :] = jnp.where(i == 0, jnp.zeros_like(halo), halo)
    win_ref[CONV_HALO:, :] = u_ref[...]
    off = CONV_HALO - (CONV_WIDTH - 1)
    rows = min(CONV_ROWS, ts)
    cols = min(CONV_COLS, d)
    for c0 in range(0, d, cols):
        for r0 in range(0, ts, rows):
            acc = jnp.zeros((rows, cols), F32)
            for j in range(CONV_WIDTH):
                acc = acc + wdw_ref[j:j + 1, c0:c0 + cols] * win_ref[r0 + off + j:r0 + off + j + rows, c0:c0 + cols]
            v_ref[r0:r0 + rows, c0:c0 + cols] = acc
    v = _layer_norm(v_ref[...] + bdw_ref[...], cg_ref[...], cb_ref[...])
    v = (v * jax.nn.sigmoid(v)).astype(BF16)
    y = _dot(v, w2_ref[...]) + b2_ref[...]
    _residual_epilogue(x_ref[...], y, mod_ref, lng_ref, lnb_ref, alpha, x1_ref, h2_ref)


def _conv_b(u, x, mods_l, w_dw, b_dw, cln_g, cln_b, w_pw2, b_pw2, ln_g, ln_b, alpha):
    bsz, s, d = x.shape
    ts = min(SEQ_TILE, s)
    hb = ts // CONV_HALO
    row = lambda a: a.reshape(1, d)
    tile = pl.BlockSpec((None, ts, d), lambda b, i: (b, i, 0))
    vec = pl.BlockSpec((1, d), lambda b, i: (0, 0))
    return pl.pallas_call(
        functools.partial(_conv_b_body, alpha=alpha),
        grid=(bsz, s // ts),
        in_specs=[
            tile,
            pl.BlockSpec((None, CONV_HALO, d), lambda b, i: (b, jnp.maximum(i * hb - 1, 0), 0)),
            tile,
            pl.BlockSpec((None, 6, d), lambda b, i: (b, 0, 0)),
            pl.BlockSpec((CONV_WIDTH, d), lambda b, i: (0, 0)),
            vec, vec, vec,
            pl.BlockSpec((d, d), lambda b, i: (0, 0)),
            vec, vec, vec,
        ],
        out_specs=[tile, tile],
        out_shape=[jax.ShapeDtypeStruct((bsz, s, d), F32)] * 2,
        scratch_shapes=[pltpu.VMEM((ts + CONV_HALO, d), F32), pltpu.VMEM((ts, d), F32)],
        compiler_params=_params("parallel", "parallel"),
        name="conv_b",
    )(u, u, x, mods_l, w_dw, row(b_dw), row(cln_g), row(cln_b), w_pw2.astype(BF16), row(b_pw2),
      row(ln_g), row(ln_b))


def _qkv_body(x_ref, mod_ref, wkv_ref, wq_ref, q_ref, k_ref, v_ref):
    d = x_ref.shape[-1]
    x = x_ref[...]
    kv = _dot(x.astype(BF16), wkv_ref[...])
    k_ref[...] = kv[:, :d].astype(BF16)
    v_ref[...] = kv[:, d:].astype(BF16)
    h = (x * (1.0 + mod_ref[1:2, :]) + mod_ref[0:1, :]).astype(BF16)
    q_ref[...] = (_dot(h, wq_ref[...]) * (HEAD_DIM ** -0.5)).astype(BF16)


def _qkv(x, mods_l, w_kv, w_q):
    bsz, s, d = x.shape
    ts = min(SEQ_TILE, s)
    tile = pl.BlockSpec((None, ts, d), lambda b, i: (b, i, 0))
    return pl.pallas_call(
        _qkv_body,
        grid=(bsz, s // ts),
        in_specs=[
            tile,
            pl.BlockSpec((None, 6, d), lambda b, i: (b, 0, 0)),
            pl.BlockSpec((d, 2 * d), lambda b, i: (0, 0)),
            pl.BlockSpec((d, d), lambda b, i: (0, 0)),
        ],
        out_specs=[tile, tile, tile],
        out_shape=[jax.ShapeDtypeStruct((bsz, s, d), BF16)] * 3,
        compiler_params=_params("parallel", "parallel"),
        name="qkv",
    )(x, mods_l, w_kv.astype(BF16), w_q.astype(BF16))


def _t5_bucket(rel):
    nb = REL_BUCKETS // 2
    ret = jnp.where(rel > 0, nb, 0)
    n = jnp.abs(rel)
    max_exact = nb // 2
    large = max_exact + (jnp.log(jnp.maximum(n, 1).astype(F32) / max_exact)
                         / math.log(REL_MAX_DIST / max_exact) * (nb - max_exact)).astype(I32)
    large = jnp.minimum(large, nb - 1)
    return ret + jnp.where(n < max_exact, n, large)


def _bucket_strip(s):
    r = jnp.arange(Q_BLOCK, dtype=I32)[:, None]
    kp = jnp.arange(s, dtype=I32)[None, :] - (s - Q_BLOCK)
    bucket = _t5_bucket(kp - r)
    visible = jnp.floor_divide(kp, CHUNK) <= (r // CHUNK)
    return jnp.where(visible, bucket, REL_BUCKETS)


def _attn_body(tab_ref, q_ref, k_ref, v_ref, bkt_ref, lam_ref, sg_ref, o_ref, bias_ref, *, lambda_init, n_heads):
    h = pl.program_id(0)
    b = pl.program_id(1)
    s = q_ref.shape[0]

    @pl.when(b == 0)
    def _():
        bk = bkt_ref[...]
        acc = jnp.full(bk.shape, MASK_VALUE, F32)
        for r in range(REL_BUCKETS):
            acc = jnp.where(bk == r, tab_ref[r * n_heads + h], acc)
        bias_ref[...] = acc

    lp = lam_ref[...]
    lam = (jnp.exp(jnp.sum(lp[0:1, :] * lp[1:2, :], axis=-1, keepdims=True))
           - jnp.exp(jnp.sum(lp[2:3, :] * lp[3:4, :], axis=-1, keepdims=True)) + lambda_init)
    lane = lax.broadcasted_iota(I32, (Q_BLOCK, 2 * HEAD_DIM), 1)
    nt = (((1,), (1,)), ((), ()))
    for i in range(s // Q_BLOCK):
        n_keys = (i + 1) * Q_BLOCK
        q = q_ref[i * Q_BLOCK:(i + 1) * Q_BLOCK, :]
        qq = jnp.concatenate([jnp.where(lane < HEAD_DIM, q, jnp.zeros_like(q)),
                              jnp.where(lane >= HEAD_DIM, q, jnp.zeros_like(q))], axis=0)
        sc = lax.dot_general(qq, k_ref[0:n_keys, :], nt, preferred_element_type=F32)
        bias = bias_ref[:, s - n_keys:s]
        s0 = sc[:Q_BLOCK] + bias
        s1 = sc[Q_BLOCK:] + bias
        p0 = jnp.exp(s0 - jnp.max(s0, axis=-1, keepdims=True))
        p1 = jnp.exp(s1 - jnp.max(s1, axis=-1, keepdims=True))
        w0 = 1.0 / jnp.sum(p0, axis=-1, keepdims=True)
        w1 = lam / jnp.sum(p1, axis=-1, keepdims=True)
        attn = (p0 * w0 - p1 * w1).astype(BF16)
        o = _dot(attn, v_ref[0:n_keys, :])
        o = o * lax.rsqrt(jnp.mean(o * o, axis=-1, keepdims=True) + LN_EPS) * sg_ref[...]
        o_ref[i * Q_BLOCK:(i + 1) * Q_BLOCK, :] = (o * (1.0 - lambda_init)).astype(BF16)


def _attn(q, k, v, lam_p, subln_g, rel_table, lambda_init):
    bsz, s, d = q.shape
    hd2 = 2 * HEAD_DIM
    n_heads = d // hd2
    head = pl.BlockSpec((None, s, hd2), lambda h, b, tab: (b, 0, h))
    grid_spec = pltpu.PrefetchScalarGridSpec(
        num_scalar_prefetch=1,
        grid=(n_heads, bsz),
        in_specs=[
            head, head, head,
            pl.BlockSpec((Q_BLOCK, s), lambda h, b, tab: (0, 0)),
            pl.BlockSpec((4, HEAD_DIM), lambda h, b, tab: (0, 0)),
            pl.BlockSpec((1, hd2), lambda h, b, tab: (0, 0)),
        ],
        out_specs=head,
        scratch_shapes=[pltpu.VMEM((Q_BLOCK, s), F32)],
    )
    return pl.pallas_call(
        functools.partial(_attn_body, lambda_init=lambda_init, n_heads=n_heads),
        grid_spec=grid_spec,
        out_shape=jax.ShapeDtypeStruct((bsz, s, d), BF16),
        compiler_params=_params("arbitrary", "arbitrary"),
        name="attn",
    )(rel_table.reshape(-1), q, k, v, _bucket_strip(s), lam_p, subln_g.reshape(1, hd2))


def _attn_out_body(o_ref, x_ref, mod_ref, wo_ref, lng_ref, lnb_ref, x1_ref, h2_ref, *, alpha):
    y = _dot(o_ref[...], wo_ref[...])
    _residual_epilogue(x_ref[...], y, mod_ref, lng_ref, lnb_ref, alpha, x1_ref, h2_ref)


def _attn_out(o, x, mods_l, w_o, ln_g, ln_b, alpha):
    bsz, s, d = x.shape
    ts = min(SEQ_TILE, s)
    tile = pl.BlockSpec((None, ts, d), lambda b, i: (b, i, 0))
    vec = pl.BlockSpec((1, d), lambda b, i: (0, 0))
    return pl.pallas_call(
        functools.partial(_attn_out_body, alpha=alpha),
        grid=(bsz, s // ts),
        in_specs=[tile, tile, pl.BlockSpec((None, 6, d), lambda b, i: (b, 0, 0)),
                  pl.BlockSpec((d, d), lambda b, i: (0, 0)), vec, vec],
        out_specs=[tile, tile],
        out_shape=[jax.ShapeDtypeStruct((bsz, s, d), F32)] * 2,
        compiler_params=_params("parallel", "parallel"),
        name="attn_out",
    )(o, x, mods_l, w_o.astype(BF16), ln_g.reshape(1, d), ln_b.reshape(1, d))


def _router_body(h_ref, wt_ref, b_ref, tri_ref, idx_ref, gate_ref, rank_ref, cnt_ref, base_ref):
    @pl.when(pl.program_id(0) == 0)
    def _():
        base_ref[...] = jnp.zeros_like(base_ref)

    logits = lax.dot_general(wt_ref[...], h_ref[...], (((1,), (1,)), ((), ())),
                             preferred_element_type=F32, precision=HIGHEST) + b_ref[...]
    n_exp, tr = logits.shape
    eio = lax.broadcasted_iota(I32, (n_exp, tr), 0)
    work = logits
    vals, idxs = [], []
    for _ in range(TOP_K):
        m = jnp.max(work, axis=0, keepdims=True)
        am = jnp.min(jnp.where(work == m, eio, n_exp), axis=0, keepdims=True)
        vals.append(m)
        idxs.append(am)
        work = jnp.where(eio == am, -jnp.inf, work)
    ex = [jnp.exp(v - vals[0]) for v in vals]
    den = ex[0] + ex[1] + ex[2] + ex[3]
    onehot = jnp.zeros((n_exp, tr), F32)
    for k in range(TOP_K):
        onehot = onehot + (eio == idxs[k]).astype(F32)
    before = _dot(onehot.astype(BF16), tri_ref[...]) + base_ref[...]
    for k in range(TOP_K):
        idx_ref[k:k + 1, :] = idxs[k]
        gate_ref[k:k + 1, :] = ex[k] / den
        rank_ref[k:k + 1, :] = jnp.sum(jnp.where(eio == idxs[k], before, 0.0), axis=0, keepdims=True).astype(I32)
    base_ref[...] = base_ref[...] + jnp.sum(onehot, axis=1, keepdims=True)
    cnt_ref[...] = base_ref[...].astype(I32)


def _router(h2, w_r, b_r):
    t, d = h2.shape
    n_exp = w_r.shape[1]
    tr = min(ROUTER_TILE, t)
    pos = jnp.arange(tr, dtype=I32)
    tri = (pos[:, None] < pos[None, :]).astype(BF16)
    tok = pl.BlockSpec((TOP_K, tr), lambda i: (0, i))
    return pl.pallas_call(
        _router_body,
        grid=(t // tr,),
        in_specs=[
            pl.BlockSpec((tr, d), lambda i: (i, 0)),
            pl.BlockSpec((n_exp, d), lambda i: (0, 0)),
            pl.BlockSpec((n_exp, 1), lambda i: (0, 0)),
            pl.BlockSpec((tr, tr), lambda i: (0, 0)),
        ],
        out_specs=[tok, tok, tok, pl.BlockSpec((n_exp, 1), lambda i: (0, 0))],
        out_shape=[jax.ShapeDtypeStruct((TOP_K, t), I32), jax.ShapeDtypeStruct((TOP_K, t), F32),
                   jax.ShapeDtypeStruct((TOP_K, t), I32), jax.ShapeDtypeStruct((n_exp, 1), I32)],
        scratch_shapes=[pltpu.VMEM((n_exp, 1), F32)],
        compiler_params=_params("arbitrary"),
        name="router",
    )(h2, w_r.T, b_r.reshape(n_exp, 1), tri)


def _row_copy(src, src_row, dst, dst_row, sem):
    return pltpu.make_async_copy(src.at[pl.ds(src_row, 1)], dst.at[pl.ds(dst_row, 1)], sem)


def _zero_fill(zero_ref, xs_hbm, start, cnt, sem, wait):
    zr = zero_ref.shape[0]

    def copy(rows, dst_row):
        dma = pltpu.make_async_copy(zero_ref.at[pl.ds(0, rows)], xs_hbm.at[pl.ds(dst_row, rows)], sem)
        dma.wait() if wait else dma.start()

    head = jnp.minimum((-start) % SUBLANES, cnt)
    for r in range(SUBLANES - 1):
        @pl.when(r < head)
        def _(r=r):
            copy(1, start + r)

    def full(r, c):
        copy(zr, pl.multiple_of(start + head + r * zr, SUBLANES))
        return c

    n_full = (cnt - head) // zr
    lax.fori_loop(0, n_full, full, 0)
    rem = cnt - head - n_full * zr
    base = start + head + n_full * zr
    bit = zr // 2
    while bit >= SUBLANES:
        @pl.when((rem & bit) != 0)
        def _(bit=bit):
            copy(bit, pl.multiple_of(base + (rem // (2 * bit)) * (2 * bit), SUBLANES))
        bit //= 2


def _dispatch_body(ppos_ref, pad_start_ref, pad_cnt_ref, h_ref, xs_hbm, zero_ref, sem, zsem, *, n_tok, td):
    i = pl.program_id(0)

    @pl.when(i == 0)
    def _():
        zero_ref[...] = jnp.zeros_like(zero_ref)
        for wait in (False, True):
            def per_range(e, carry, wait=wait):
                _zero_fill(zero_ref, xs_hbm, pad_start_ref[e], pad_cnt_ref[e], zsem, wait)
                return carry
            lax.fori_loop(0, pad_start_ref.shape[0], per_range, 0)

    def issue(t, carry):
        tok = i * td + t
        for k in range(TOP_K):
            _row_copy(h_ref, t, xs_hbm, ppos_ref[k * n_tok + tok], sem).start()
        return carry

    lax.fori_loop(0, td, issue, 0)
    pltpu.make_async_copy(xs_hbm.at[pl.ds(0, TOP_K * td)], xs_hbm.at[pl.ds(0, TOP_K * td)], sem).wait()


def _dispatch(h2, ppos_flat, pad_start, pad_cnt, n_rows):
    t, d = h2.shape
    td = min(DISPATCH_TILE, t)
    grid_spec = pltpu.PrefetchScalarGridSpec(
        num_scalar_prefetch=3,
        grid=(t // td,),
        in_specs=[pl.BlockSpec((td, d), lambda i, pp, ps, pc: (i, 0))],
        out_specs=pl.BlockSpec(memory_space=pl.ANY),
        scratch_shapes=[pltpu.VMEM((MOE_TILE, d), F32), pltpu.SemaphoreType.DMA, pltpu.SemaphoreType.DMA],
    )
    return pl.pallas_call(
        functools.partial(_dispatch_body, n_tok=t, td=td),
        grid_spec=grid_spec,
        out_shape=jax.ShapeDtypeStruct((n_rows, d), F32),
        compiler_params=_params("arbitrary"),
        name="dispatch",
    )(ppos_flat, pad_start, pad_cnt, h2)


def _experts_body(be_ref, bsrc_ref, nvalid_ref, xs_ref, wgu_ref, bgu_ref, wdn_ref, bdn_ref, ys_ref,
                  wgu_bf, wdn_bf):
    i = pl.program_id(0)
    e = be_ref[i]
    prev = be_ref[jnp.maximum(i - 1, 0)]
    f = wdn_ref.shape[0]

    @pl.when((i == 0) | (e != prev))
    def _():
        def cast(ref_in, ref_out):
            def step(r, c):
                rows = pl.ds(pl.multiple_of(r * CAST_ROWS, CAST_ROWS), CAST_ROWS)
                ref_out[rows, :] = ref_in[rows, :].astype(BF16)
                return c
            lax.fori_loop(0, ref_in.shape[0] // CAST_ROWS, step, 0)
        cast(wgu_ref, wgu_bf)
        cast(wdn_ref, wdn_bf)

    @pl.when(i < nvalid_ref[0])
    def _():
        x = xs_ref[...].astype(BF16)
        gate = jnp.minimum(_dot(x, wgu_bf[:, :f]) + bgu_ref[:, :f], SWIGLU_LIMIT)
        lin = jnp.clip(_dot(x, wgu_bf[:, f:]) + bgu_ref[:, f:], -SWIGLU_LIMIT, SWIGLU_LIMIT)
        act = (gate * jax.nn.sigmoid(SWIGLU_ALPHA * gate) * (lin + 1.0)).astype(BF16)
        ys_ref[...] = _dot(act, wdn_bf[...]) + bdn_ref[...]

    @pl.when(i >= nvalid_ref[0])
    def _():
        ys_ref[...] = jnp.zeros_like(ys_ref)


def _experts(xs, block_e, block_src, n_valid, w_gu, b_gu, w_dn, b_dn, layer):
    n_rows, d = xs.shape
    depth, n_exp, _, f2 = w_gu.shape
    f = f2 // 2
    tm = MOE_TILE
    grid_spec = pltpu.PrefetchScalarGridSpec(
        num_scalar_prefetch=3,
        grid=(n_rows // tm,),
        in_specs=[
            pl.BlockSpec((tm, d), lambda i, be, bs, nv: (bs[i], 0)),
            pl.BlockSpec((None, None, d, f2), lambda i, be, bs, nv: (layer, be[i], 0, 0)),
            pl.BlockSpec((None, None, 1, f2), lambda i, be, bs, nv: (layer, be[i], 0, 0)),
            pl.BlockSpec((None, None, f, d), lambda i, be, bs, nv: (layer, be[i], 0, 0)),
            pl.BlockSpec((None, None, 1, d), lambda i, be, bs, nv: (layer, be[i], 0, 0)),
        ],
        out_specs=pl.BlockSpec((tm, d), lambda i, be, bs, nv: (i, 0)),
        scratch_shapes=[pltpu.VMEM((d, f2), BF16), pltpu.VMEM((f, d), BF16)],
    )
    return pl.pallas_call(
        _experts_body,
        grid_spec=grid_spec,
        out_shape=jax.ShapeDtypeStruct((n_rows, d), F32),
        compiler_params=_params("arbitrary"),
        name="experts",
    )(block_e, block_src, n_valid, xs, w_gu, b_gu.reshape(depth, n_exp, 1, f2), w_dn,
      b_dn.reshape(depth, n_exp, 1, d))


def _combine_body(ppos_ref, ys_hbm, x_ref, gate_ref, mod_ref, lng_ref, lnb_ref, o_ref, buf_ref, sem,
                  *, n_tok, tt, alpha):
    i = pl.program_id(0)

    def issue(t, carry):
        tok = i * tt + t
        for k in range(TOP_K):
            _row_copy(ys_hbm, ppos_ref[k * n_tok + tok], buf_ref.at[k], t, sem).start()
        return carry

    lax.fori_loop(0, tt, issue, 0)
    pltpu.make_async_copy(buf_ref, buf_ref, sem).wait()
    gates = gate_ref[...]
    y = gates[:, 0:1] * buf_ref[0]
    for k in range(1, TOP_K):
        y = y + gates[:, k:k + 1] * buf_ref[k]
    o_ref[...] = _layer_norm(alpha * x_ref[...] + mod_ref[5:6, :] * y, lng_ref[...], lnb_ref[...])


def _combine(ys, ppos_flat, x1, gates_t, mods_l, ln_g, ln_b, alpha, seq_len):
    t, d = x1.shape
    tt = min(COMBINE_TILE, seq_len)
    per_seq = seq_len // tt
    vec = pl.BlockSpec((1, d), lambda i, pp: (0, 0))
    grid_spec = pltpu.PrefetchScalarGridSpec(
        num_scalar_prefetch=1,
        grid=(t // tt,),
        in_specs=[
            pl.BlockSpec(memory_space=pl.ANY),
            pl.BlockSpec((tt, d), lambda i, pp: (i, 0)),
            pl.BlockSpec((tt, TOP_K), lambda i, pp: (i, 0)),
            pl.BlockSpec((None, 6, d), lambda i, pp: (i // per_seq, 0, 0)),
            vec, vec,
        ],
        out_specs=pl.BlockSpec((tt, d), lambda i, pp: (i, 0)),
        scratch_shapes=[pltpu.VMEM((TOP_K, tt, d), F32), pltpu.SemaphoreType.DMA],
    )
    return pl.pallas_call(
        functools.partial(_combine_body, n_tok=t, tt=tt, alpha=alpha),
        grid_spec=grid_spec,
        out_shape=jax.ShapeDtypeStruct((t, d), F32),
        compiler_params=_params("arbitrary"),
        name="combine",
    )(ppos_flat, ys, x1, gates_t, mods_l, ln_g.reshape(1, d), ln_b.reshape(1, d))


def _moe_layer(x1, h2, mods_l, w_r, b_r, w_gu, b_gu, w_dn, b_dn, layer, ln_g, ln_b, alpha, seq_len):
    t, d = x1.shape
    n_exp = w_r.shape[1]
    tm = MOE_TILE
    idx, gates, rank, cnt = _router(h2, w_r, b_r)
    counts = cnt[:, 0]
    padded = (counts + tm - 1) // tm * tm
    pend = jnp.cumsum(padded)
    pstart = pend - padded
    is_e = idx[..., None] == jnp.arange(n_exp, dtype=I32)
    ppos_flat = (jnp.sum(jnp.where(is_e, pstart, 0), axis=-1) + rank).reshape(-1)
    n_blocks = (t * TOP_K) // tm + n_exp
    n_valid = pend[-1] // tm
    blk = jnp.arange(n_blocks, dtype=I32)
    block_src = jnp.minimum(blk, n_valid - 1)
    block_e = jnp.sum(pend[None, :] <= (block_src * tm)[:, None], axis=1).astype(I32)
    n_rows = n_blocks * tm
    pad_start = jnp.concatenate([pstart + counts, pend[-1:]]).astype(I32)
    pad_cnt = jnp.concatenate([padded - counts, n_rows - pend[-1:]]).astype(I32)
    xs = _dispatch(h2, ppos_flat, pad_start, pad_cnt, n_rows)
    ys = _experts(xs, block_e, block_src.astype(I32), n_valid.reshape(1).astype(I32), w_gu, b_gu, w_dn, b_dn, layer)
    return _combine(ys, ppos_flat, x1, gates.T, mods_l, ln_g, ln_b, alpha, seq_len)


def kernel(x, c, ada_w, ada_b, post_ln_g, post_ln_b, conv_w_pw1, conv_b_pw1, conv_w_dw, conv_b_dw, conv_ln_g, conv_ln_b, conv_w_pw2, conv_b_pw2, w_kv, attn_w_q, attn_lambda, attn_subln_g, attn_w_o, rel_bias_table, router_w, router_b, expert_w_gate_up, expert_b_gate_up, expert_w_down, expert_b_down):
    bsz, s, d = x.shape
    depth = ada_w.shape[0]
    n_a = depth // 2
    alpha = (2 * depth) ** 0.25
    mods = _ada(c, ada_w, ada_b).reshape(depth, bsz, 6, d)
    q = k = v = None
    for l in range(depth):
        mods_l = mods[l]
        if l < n_a:
            u = _conv_a(x, mods_l, conv_w_pw1[l], conv_b_pw1[l])
            x1, h2 = _conv_b(u, x, mods_l, conv_w_dw[l], conv_b_dw[l], conv_ln_g[l], conv_ln_b[l],
                             conv_w_pw2[l], conv_b_pw2[l], post_ln_g[l, 0], post_ln_b[l, 0], alpha)
        else:
            j = l - n_a
            if j == 0:
                q, k, v = _qkv(x, mods_l, w_kv, attn_w_q[j])
            else:
                q = _qkv(x, mods_l, w_kv, attn_w_q[j])[0]
            lambda_init = 0.8 - 0.6 * math.exp(-0.3 * l)
            o = _attn(q, k, v, attn_lambda[j], attn_subln_g[j], rel_bias_table, lambda_init)
            x1, h2 = _attn_out(o, x, mods_l, attn_w_o[j], post_ln_g[l, 0], post_ln_b[l, 0], alpha)
        x = _moe_layer(x1.reshape(bsz * s, d), h2.reshape(bsz * s, d), mods_l, router_w[l], router_b[l],
                       expert_w_gate_up, expert_b_gate_up, expert_w_down, expert_b_down, l,
                       post_ln_g[l, 1], post_ln_b[l, 1], alpha, s).reshape(bsz, s, d)
    return x
```

```python
import functools
import math

import jax
import jax.numpy as jnp
from jax import lax
from jax.experimental import pallas as pl
from jax.experimental.pallas import tpu as pltpu

F32 = jnp.float32
BF16 = jnp.bfloat16
I32 = jnp.int32
HIGHEST = lax.Precision.HIGHEST

CHUNK = 64
CONV_WIDTH = 31
HEAD_DIM = 64
REL_BUCKETS = 32
REL_MAX_DIST = 128
TOP_K = 4
SWIGLU_LIMIT = 7.0
SWIGLU_ALPHA = 1.702
LN_EPS = 1e-5
MASK_VALUE = -1e30

SUBLANES = 8
LANES = 128
VMEM_LIMIT_BYTES = 56 * 1024 * 1024

ADA_TN = 1024
SEQ_TILE = 256
CONV_HALO = 32
CONV_ROWS = 64
CONV_COLS = 256
ROUTER_TILE = 512
DISPATCH_TILE = 256
MOE_TILE = 256
COMBINE_TILE = 256
Q_BLOCK = 128
CAST_ROWS = 128


def _params(*sem):
    return pltpu.CompilerParams(dimension_semantics=sem, vmem_limit_bytes=VMEM_LIMIT_BYTES)


def _layer_norm(x, g, b):
    mu = jnp.mean(x, axis=-1, keepdims=True)
    xc = x - mu
    var = jnp.mean(xc * xc, axis=-1, keepdims=True)
    return xc * lax.rsqrt(var + LN_EPS) * g + b


def _dot(a, b):
    return jnp.dot(a, b, preferred_element_type=F32)


def _ada_body(c_ref, w_ref, b_ref, o_ref):
    c = c_ref[...]
    cond = c * jax.nn.sigmoid(c)
    o_ref[...] = jnp.dot(cond, w_ref[...], preferred_element_type=F32, precision=HIGHEST) + b_ref[...]


def _ada(c, ada_w, ada_b):
    depth, d, n = ada_w.shape
    bsz = c.shape[0]
    tn = min(ADA_TN, n)
    return pl.pallas_call(
        _ada_body,
        grid=(depth, n // tn),
        in_specs=[
            pl.BlockSpec((bsz, d), lambda l, j: (0, 0)),
            pl.BlockSpec((None, d, tn), lambda l, j: (l, 0, j)),
            pl.BlockSpec((None, 1, tn), lambda l, j: (l, 0, j)),
        ],
        out_specs=pl.BlockSpec((None, bsz, tn), lambda l, j: (l, 0, j)),
        out_shape=jax.ShapeDtypeStruct((depth, bsz, n), F32),
        compiler_params=_params("parallel", "parallel"),
        name="ada",
    )(c, ada_w, ada_b.reshape(depth, 1, n))


def _conv_a_body(x_ref, mod_ref, w_ref, b_ref, u_ref):
    d = x_ref.shape[-1]
    h = (x_ref[...] * (1.0 + mod_ref[1:2, :]) + mod_ref[0:1, :]).astype(BF16)
    a = _dot(h, w_ref[:, :d]) + b_ref[:, :d]
    g = _dot(h, w_ref[:, d:]) + b_ref[:, d:]
    u_ref[...] = a * jax.nn.sigmoid(g)


def _conv_a(x, mods_l, w_pw1, b_pw1):
    bsz, s, d = x.shape
    ts = min(SEQ_TILE, s)
    return pl.pallas_call(
        _conv_a_body,
        grid=(bsz, s // ts),
        in_specs=[
            pl.BlockSpec((None, ts, d), lambda b, i: (b, i, 0)),
            pl.BlockSpec((None, 6, d), lambda b, i: (b, 0, 0)),
            pl.BlockSpec((d, 2 * d), lambda b, i: (0, 0)),
            pl.BlockSpec((1, 2 * d), lambda b, i: (0, 0)),
        ],
        out_specs=pl.BlockSpec((None, ts, d), lambda b, i: (b, i, 0)),
        out_shape=jax.ShapeDtypeStruct((bsz, s, d), F32),
        compiler_params=_params("parallel", "parallel"),
        name="conv_a",
    )(x, mods_l, w_pw1.astype(BF16), b_pw1.reshape(1, 2 * d))


def _residual_epilogue(x, y, mod_ref, lng_ref, lnb_ref, alpha, x1_ref, h2_ref):
    x1 = _layer_norm(alpha * x + mod_ref[2:3, :] * y, lng_ref[...], lnb_ref[...])
    x1_ref[...] = x1
    h2_ref[...] = x1 * (1.0 + mod_ref[4:5, :]) + mod_ref[3:4, :]


def _conv_b_body(u_ref, halo_ref, x_ref, mod_ref, wdw_ref, bdw_ref, cg_ref, cb_ref, w2_ref, b2_ref,
                 lng_ref, lnb_ref, x1_ref, h2_ref, win_ref, v_ref, *, alpha):
    ts, d = u_ref.shape
    i = pl.program_id(1)
    halo = halo_ref[...]
    win_ref[0, 0:CONV_HALO, :] = jnp.where(i == 0, jnp.zeros_like(halo), halo)
    win_ref[0, CONV_HALO:, :] = u_ref[...]
    n_shift = ts + CONV_HALO - SUBLANES
    for b in range(1, SUBLANES):
        win_ref[b, 0:n_shift, :] = win_ref[0, b:b + n_shift, :]
    off = CONV_HALO - (CONV_WIDTH - 1)
    rows = min(CONV_ROWS, ts)
    cols = min(CONV_COLS, d)
    for c0 in range(0, d, cols):
        for r0 in range(0, ts, rows):
            acc = jnp.zeros((rows, cols), F32)
            for j in range(CONV_WIDTH):
                a, b = divmod(off + j, SUBLANES)
                r = r0 + a * SUBLANES
                acc = acc + wdw_ref[j:j + 1, c0:c0 + cols] * win_ref[b, r:r + rows, c0:c0 + cols]
            v_ref[r0:r0 + rows, c0:c0 + cols] = acc
    v = _layer_norm(v_ref[...] + bdw_ref[...], cg_ref[...], cb_ref[...])
    v = (v * jax.nn.sigmoid(v)).astype(BF16)
    y = _dot(v, w2_ref[...]) + b2_ref[...]
    _residual_epilogue(x_ref[...], y, mod_ref, lng_ref, lnb_ref, alpha, x1_ref, h2_ref)


def _conv_b(u, x, mods_l, w_dw, b_dw, cln_g, cln_b, w_pw2, b_pw2, ln_g, ln_b, alpha):
    bsz, s, d = x.shape
    ts = min(SEQ_TILE, s)
    hb = ts // CONV_HALO
    row = lambda a: a.reshape(1, d)
    tile = pl.BlockSpec((None, ts, d), lambda b, i: (b, i, 0))
    vec = pl.BlockSpec((1, d), lambda b, i: (0, 0))
    return pl.pallas_call(
        functools.partial(_conv_b_body, alpha=alpha),
        grid=(bsz, s // ts),
        in_specs=[
            tile,
            pl.BlockSpec((None, CONV_HALO, d), lambda b, i: (b, jnp.maximum(i * hb - 1, 0), 0)),
            tile,
            pl.BlockSpec((None, 6, d), lambda b, i: (b, 0, 0)),
            pl.BlockSpec((CONV_WIDTH, d), lambda b, i: (0, 0)),
            vec, vec, vec,
            pl.BlockSpec((d, d), lambda b, i: (0, 0)),
            vec, vec, vec,
        ],
        out_specs=[tile, tile],
        out_shape=[jax.ShapeDtypeStruct((bsz, s, d), F32)] * 2,
        scratch_shapes=[pltpu.VMEM((SUBLANES, ts + CONV_HALO, d), F32), pltpu.VMEM((ts, d), F32)],
        compiler_params=_params("parallel", "parallel"),
        name="conv_b",
    )(u, u, x, mods_l, w_dw, row(b_dw), row(cln_g), row(cln_b), w_pw2.astype(BF16), row(b_pw2),
      row(ln_g), row(ln_b))


def _qkv_body(x_ref, mod_ref, wkv_ref, wq_ref, q_ref, k_ref, v_ref):
    d = x_ref.shape[-1]
    x = x_ref[...]
    kv = _dot(x.astype(BF16), wkv_ref[...])
    k_ref[...] = kv[:, :d].astype(BF16)
    v_ref[...] = kv[:, d:].astype(BF16)
    h = (x * (1.0 + mod_ref[1:2, :]) + mod_ref[0:1, :]).astype(BF16)
    q_ref[...] = (_dot(h, wq_ref[...]) * (HEAD_DIM ** -0.5)).astype(BF16)


def _qkv(x, mods_l, w_kv, w_q):
    bsz, s, d = x.shape
    ts = min(SEQ_TILE, s)
    tile = pl.BlockSpec((None, ts, d), lambda b, i: (b, i, 0))
    return pl.pallas_call(
        _qkv_body,
        grid=(bsz, s // ts),
        in_specs=[
            tile,
            pl.BlockSpec((None, 6, d), lambda b, i: (b, 0, 0)),
            pl.BlockSpec((d, 2 * d), lambda b, i: (0, 0)),
            pl.BlockSpec((d, d), lambda b, i: (0, 0)),
        ],
        out_specs=[tile, tile, tile],
        out_shape=[jax.ShapeDtypeStruct((bsz, s, d), BF16)] * 3,
        compiler_params=_params("parallel", "parallel"),
        name="qkv",
    )(x, mods_l, w_kv.astype(BF16), w_q.astype(BF16))


def _t5_bucket(rel):
    nb = REL_BUCKETS // 2
    ret = jnp.where(rel > 0, nb, 0)
    n = jnp.abs(rel)
    max_exact = nb // 2
    large = max_exact + (jnp.log(jnp.maximum(n, 1).astype(F32) / max_exact)
                         / math.log(REL_MAX_DIST / max_exact) * (nb - max_exact)).astype(I32)
    large = jnp.minimum(large, nb - 1)
    return ret + jnp.where(n < max_exact, n, large)


def _bucket_strip(s):
    r = jnp.arange(Q_BLOCK, dtype=I32)[:, None]
    kp = jnp.arange(s, dtype=I32)[None, :] - (s - Q_BLOCK)
    bucket = _t5_bucket(kp - r)
    visible = jnp.floor_divide(kp, CHUNK) <= (r // CHUNK)
    return jnp.where(visible, bucket, REL_BUCKETS)


def _attn_body(tab_ref, q_ref, k_ref, v_ref, bkt_ref, lam_ref, sg_ref, o_ref, bias_ref, *, lambda_init, n_heads):
    h = pl.program_id(0)
    b = pl.program_id(1)
    s = q_ref.shape[0]

    @pl.when(b == 0)
    def _():
        bk = bkt_ref[...]
        acc = jnp.full(bk.shape, MASK_VALUE, F32)
        for r in range(REL_BUCKETS):
            acc = jnp.where(bk == r, tab_ref[r * n_heads + h], acc)
        bias_ref[0:Q_BLOCK, :] = acc
        bias_ref[Q_BLOCK:, :] = acc

    lp = lam_ref[...]
    lam = (jnp.exp(jnp.sum(lp[0:1, :] * lp[1:2, :], axis=-1, keepdims=True))
           - jnp.exp(jnp.sum(lp[2:3, :] * lp[3:4, :], axis=-1, keepdims=True)) + lambda_init)
    lane = lax.broadcasted_iota(I32, (Q_BLOCK, 2 * HEAD_DIM), 1)
    nt = (((1,), (1,)), ((), ()))
    for i in range(s // Q_BLOCK):
        n_keys = (i + 1) * Q_BLOCK
        q = q_ref[i * Q_BLOCK:(i + 1) * Q_BLOCK, :]
        qq = jnp.concatenate([jnp.where(lane < HEAD_DIM, q, jnp.zeros_like(q)),
                              jnp.where(lane >= HEAD_DIM, q, jnp.zeros_like(q))], axis=0)
        sc = lax.dot_general(qq, k_ref[0:n_keys, :], nt, preferred_element_type=F32) + bias_ref[:, s - n_keys:s]
        p = jnp.exp(sc - jnp.max(sc, axis=-1, keepdims=True))
        denom = jnp.sum(p, axis=-1, keepdims=True)
        pv = _dot(p.astype(BF16), v_ref[0:n_keys, :])
        o = pv[:Q_BLOCK] * (1.0 / denom[:Q_BLOCK]) - pv[Q_BLOCK:] * (lam / denom[Q_BLOCK:])
        o = o * lax.rsqrt(jnp.mean(o * o, axis=-1, keepdims=True) + LN_EPS) * sg_ref[...]
        o_ref[i * Q_BLOCK:(i + 1) * Q_BLOCK, :] = (o * (1.0 - lambda_init)).astype(BF16)


def _attn(q, k, v, lam_p, subln_g, rel_table, lambda_init):
    bsz, s, d = q.shape
    hd2 = 2 * HEAD_DIM
    n_heads = d // hd2
    head = pl.BlockSpec((None, s, hd2), lambda h, b, tab: (b, 0, h))
    grid_spec = pltpu.PrefetchScalarGridSpec(
        num_scalar_prefetch=1,
        grid=(n_heads, bsz),
        in_specs=[
            head, head, head,
            pl.BlockSpec((Q_BLOCK, s), lambda h, b, tab: (0, 0)),
            pl.BlockSpec((4, HEAD_DIM), lambda h, b, tab: (0, 0)),
            pl.BlockSpec((1, hd2), lambda h, b, tab: (0, 0)),
        ],
        out_specs=head,
        scratch_shapes=[pltpu.VMEM((2 * Q_BLOCK, s), F32)],
    )
    return pl.pallas_call(
        functools.partial(_attn_body, lambda_init=lambda_init, n_heads=n_heads),
        grid_spec=grid_spec,
        out_shape=jax.ShapeDtypeStruct((bsz, s, d), BF16),
        compiler_params=_params("arbitrary", "arbitrary"),
        name="attn",
    )(rel_table.reshape(-1), q, k, v, _bucket_strip(s), lam_p, subln_g.reshape(1, hd2))


def _attn_out_body(o_ref, x_ref, mod_ref, wo_ref, lng_ref, lnb_ref, x1_ref, h2_ref, *, alpha):
    y = _dot(o_ref[...], wo_ref[...])
    _residual_epilogue(x_ref[...], y, mod_ref, lng_ref, lnb_ref, alpha, x1_ref, h2_ref)


def _attn_out(o, x, mods_l, w_o, ln_g, ln_b, alpha):
    bsz, s, d = x.shape
    ts = min(SEQ_TILE, s)
    tile = pl.BlockSpec((None, ts, d), lambda b, i: (b, i, 0))
    vec = pl.BlockSpec((1, d), lambda b, i: (0, 0))
    return pl.pallas_call(
        functools.partial(_attn_out_body, alpha=alpha),
        grid=(bsz, s // ts),
        in_specs=[tile, tile, pl.BlockSpec((None, 6, d), lambda b, i: (b, 0, 0)),
                  pl.BlockSpec((d, d), lambda b, i: (0, 0)), vec, vec],
        out_specs=[tile, tile],
        out_shape=[jax.ShapeDtypeStruct((bsz, s, d), F32)] * 2,
        compiler_params=_params("parallel", "parallel"),
        name="attn_out",
    )(o, x, mods_l, w_o.astype(BF16), ln_g.reshape(1, d), ln_b.reshape(1, d))


def _router_body(h_ref, wt_ref, b_ref, tri_ref, idx_ref, gate_ref, rank_ref, cnt_ref, base_ref):
    @pl.when(pl.program_id(0) == 0)
    def _():
        base_ref[...] = jnp.zeros_like(base_ref)

    logits = lax.dot_general(wt_ref[...], h_ref[...], (((1,), (1,)), ((), ())),
                             preferred_element_type=F32, precision=HIGHEST) + b_ref[...]
    n_exp, tr = logits.shape
    eio = lax.broadcasted_iota(I32, (n_exp, tr), 0)
    work = logits
    vals, idxs = [], []
    for _ in range(TOP_K):
        m = jnp.max(work, axis=0, keepdims=True)
        am = jnp.min(jnp.where(work == m, eio, n_exp), axis=0, keepdims=True)
        vals.append(m)
        idxs.append(am)
        work = jnp.where(eio == am, -jnp.inf, work)
    ex = [jnp.exp(v - vals[0]) for v in vals]
    den = ex[0] + ex[1] + ex[2] + ex[3]
    onehot = jnp.zeros((n_exp, tr), F32)
    for k in range(TOP_K):
        onehot = onehot + (eio == idxs[k]).astype(F32)
    before = _dot(onehot.astype(BF16), tri_ref[...]) + base_ref[...]
    for k in range(TOP_K):
        idx_ref[k:k + 1, :] = idxs[k]
        gate_ref[k:k + 1, :] = ex[k] / den
        rank_ref[k:k + 1, :] = jnp.sum(jnp.where(eio == idxs[k], before, 0.0), axis=0, keepdims=True).astype(I32)
    base_ref[...] = base_ref[...] + jnp.sum(onehot, axis=1, keepdims=True)
    cnt_ref[...] = base_ref[...].astype(I32)


def _router(h2, w_r, b_r):
    t, d = h2.shape
    n_exp = w_r.shape[1]
    tr = min(ROUTER_TILE, t)
    pos = jnp.arange(tr, dtype=I32)
    tri = (pos[:, None] < pos[None, :]).astype(BF16)
    tok = pl.BlockSpec((TOP_K, tr), lambda i: (0, i))
    return pl.pallas_call(
        _router_body,
        grid=(t // tr,),
        in_specs=[
            pl.BlockSpec((tr, d), lambda i: (i, 0)),
            pl.BlockSpec((n_exp, d), lambda i: (0, 0)),
            pl.BlockSpec((n_exp, 1), lambda i: (0, 0)),
            pl.BlockSpec((tr, tr), lambda i: (0, 0)),
        ],
        out_specs=[tok, tok, tok, pl.BlockSpec((n_exp, 1), lambda i: (0, 0))],
        out_shape=[jax.ShapeDtypeStruct((TOP_K, t), I32), jax.ShapeDtypeStruct((TOP_K, t), F32),
                   jax.ShapeDtypeStruct((TOP_K, t), I32), jax.ShapeDtypeStruct((n_exp, 1), I32)],
        scratch_shapes=[pltpu.VMEM((n_exp, 1), F32)],
        compiler_params=_params("arbitrary"),
        name="router",
    )(h2, w_r.T, b_r.reshape(n_exp, 1), tri)


def _row_copy(src, src_row, dst, dst_row, sem):
    return pltpu.make_async_copy(src.at[pl.ds(src_row, 1)], dst.at[pl.ds(dst_row, 1)], sem)


def _zero_fill(zero_ref, xs_hbm, start, cnt, sem, wait):
    zr = zero_ref.shape[0]

    def copy(rows, dst_row):
        dma = pltpu.make_async_copy(zero_ref.at[pl.ds(0, rows)], xs_hbm.at[pl.ds(dst_row, rows)], sem)
        dma.wait() if wait else dma.start()

    head = jnp.minimum((-start) % SUBLANES, cnt)
    for r in range(SUBLANES - 1):
        @pl.when(r < head)
        def _(r=r):
            copy(1, start + r)

    def full(r, c):
        copy(zr, pl.multiple_of(start + head + r * zr, SUBLANES))
        return c

    n_full = (cnt - head) // zr
    lax.fori_loop(0, n_full, full, 0)
    rem = cnt - head - n_full * zr
    base = start + head + n_full * zr
    bit = zr // 2
    while bit >= SUBLANES:
        @pl.when((rem & bit) != 0)
        def _(bit=bit):
            copy(bit, pl.multiple_of(base + (rem // (2 * bit)) * (2 * bit), SUBLANES))
        bit //= 2


def _dispatch_body(ppos_ref, pad_start_ref, pad_cnt_ref, h_ref, xs_hbm, zero_ref, sem, zsem, *, n_tok, td):
    i = pl.program_id(0)

    @pl.when(i == 0)
    def _():
        zero_ref[...] = jnp.zeros_like(zero_ref)
        for wait in (False, True):
            def per_range(e, carry, wait=wait):
                _zero_fill(zero_ref, xs_hbm, pad_start_ref[e], pad_cnt_ref[e], zsem, wait)
                return carry
            lax.fori_loop(0, pad_start_ref.shape[0], per_range, 0)

    def issue(t, carry):
        tok = i * td + t
        for k in range(TOP_K):
            _row_copy(h_ref, t, xs_hbm, ppos_ref[k * n_tok + tok], sem).start()
        return carry

    lax.fori_loop(0, td, issue, 0)
    pltpu.make_async_copy(xs_hbm.at[pl.ds(0, TOP_K * td)], xs_hbm.at[pl.ds(0, TOP_K * td)], sem).wait()


def _dispatch(h2, ppos_flat, pad_start, pad_cnt, n_rows):
    t, d = h2.shape
    td = min(DISPATCH_TILE, t)
    grid_spec = pltpu.PrefetchScalarGridSpec(
        num_scalar_prefetch=3,
        grid=(t // td,),
        in_specs=[pl.BlockSpec((td, d), lambda i, pp, ps, pc: (i, 0))],
        out_specs=pl.BlockSpec(memory_space=pl.ANY),
        scratch_shapes=[pltpu.VMEM((MOE_TILE, d), F32), pltpu.SemaphoreType.DMA, pltpu.SemaphoreType.DMA],
    )
    return pl.pallas_call(
        functools.partial(_dispatch_body, n_tok=t, td=td),
        grid_spec=grid_spec,
        out_shape=jax.ShapeDtypeStruct((n_rows, d), F32),
        compiler_params=_params("arbitrary"),
        name="dispatch",
    )(ppos_flat, pad_start, pad_cnt, h2)


def _experts_body(be_ref, bsrc_ref, nvalid_ref, xs_ref, wgu_ref, bgu_ref, wdn_ref, bdn_ref, ys_ref,
                  wgu_bf, wdn_bf):
    i = pl.program_id(0)
    e = be_ref[i]
    prev = be_ref[jnp.maximum(i - 1, 0)]
    f = wdn_ref.shape[0]

    @pl.when((i == 0) | (e != prev))
    def _():
        def cast(ref_in, ref_out):
            def step(r, c):
                rows = pl.ds(pl.multiple_of(r * CAST_ROWS, CAST_ROWS), CAST_ROWS)
                ref_out[rows, :] = ref_in[rows, :].astype(BF16)
                return c
            lax.fori_loop(0, ref_in.shape[0] // CAST_ROWS, step, 0)
        cast(wgu_ref, wgu_bf)
        cast(wdn_ref, wdn_bf)

    @pl.when(i < nvalid_ref[0])
    def _():
        x = xs_ref[...].astype(BF16)
        gate = jnp.minimum(_dot(x, wgu_bf[:, :f]) + bgu_ref[:, :f], SWIGLU_LIMIT)
        lin = jnp.clip(_dot(x, wgu_bf[:, f:]) + bgu_ref[:, f:], -SWIGLU_LIMIT, SWIGLU_LIMIT)
        act = (gate * jax.nn.sigmoid(SWIGLU_ALPHA * gate) * (lin + 1.0)).astype(BF16)
        ys_ref[...] = _dot(act, wdn_bf[...]) + bdn_ref[...]

    @pl.when(i >= nvalid_ref[0])
    def _():
        ys_ref[...] = jnp.zeros_like(ys_ref)


def _experts(xs, block_e, block_src, n_valid, w_gu, b_gu, w_dn, b_dn, layer):
    n_rows, d = xs.shape
    depth, n_exp, _, f2 = w_gu.shape
    f = f2 // 2
    tm = MOE_TILE
    grid_spec = pltpu.PrefetchScalarGridSpec(
        num_scalar_prefetch=3,
        grid=(n_rows // tm,),
        in_specs=[
            pl.BlockSpec((tm, d), lambda i, be, bs, nv: (bs[i], 0)),
            pl.BlockSpec((None, None, d, f2), lambda i, be, bs, nv: (layer, be[i], 0, 0)),
            pl.BlockSpec((None, None, 1, f2), lambda i, be, bs, nv: (layer, be[i], 0, 0)),
            pl.BlockSpec((None, None, f, d), lambda i, be, bs, nv: (layer, be[i], 0, 0)),
            pl.BlockSpec((None, None, 1, d), lambda i, be, bs, nv: (layer, be[i], 0, 0)),
        ],
        out_specs=pl.BlockSpec((tm, d), lambda i, be, bs, nv: (i, 0)),
        scratch_shapes=[pltpu.VMEM((d, f2), BF16), pltpu.VMEM((f, d), BF16)],
    )
    return pl.pallas_call(
        _experts_body,
        grid_spec=grid_spec,
        out_shape=jax.ShapeDtypeStruct((n_rows, d), F32),
        compiler_params=_params("arbitrary"),
        name="experts",
    )(block_e, block_src, n_valid, xs, w_gu, b_gu.reshape(depth, n_exp, 1, f2), w_dn,
      b_dn.reshape(depth, n_exp, 1, d))


def _combine_body(ppos_ref, ys_hbm, x_ref, gate_ref, mod_ref, lng_ref, lnb_ref, o_ref, buf_ref, sem,
                  *, n_tok, tt, alpha):
    i = pl.program_id(0)

    def issue(t, carry):
        tok = i * tt + t
        for k in range(TOP_K):
            _row_copy(ys_hbm, ppos_ref[k * n_tok + tok], buf_ref.at[k], t, sem).start()
        return carry

    lax.fori_loop(0, tt, issue, 0)
    pltpu.make_async_copy(buf_ref, buf_ref, sem).wait()
    gates = gate_ref[...]
    y = gates[:, 0:1] * buf_ref[0]
    for k in range(1, TOP_K):
        y = y + gates[:, k:k + 1] * buf_ref[k]
    o_ref[...] = _layer_norm(alpha * x_ref[...] + mod_ref[5:6, :] * y, lng_ref[...], lnb_ref[...])


def _combine(ys, ppos_flat, x1, gates_t, mods_l, ln_g, ln_b, alpha, seq_len):
    t, d = x1.shape
    tt = min(COMBINE_TILE, seq_len)
    per_seq = seq_len // tt
    vec = pl.BlockSpec((1, d), lambda i, pp: (0, 0))
    grid_spec = pltpu.PrefetchScalarGridSpec(
        num_scalar_prefetch=1,
        grid=(t // tt,),
        in_specs=[
            pl.BlockSpec(memory_space=pl.ANY),
            pl.BlockSpec((tt, d), lambda i, pp: (i, 0)),
            pl.BlockSpec((tt, TOP_K), lambda i, pp: (i, 0)),
            pl.BlockSpec((None, 6, d), lambda i, pp: (i // per_seq, 0, 0)),
            vec, vec,
        ],
        out_specs=pl.BlockSpec((tt, d), lambda i, pp: (i, 0)),
        scratch_shapes=[pltpu.VMEM((TOP_K, tt, d), F32), pltpu.SemaphoreType.DMA],
    )
    return pl.pallas_call(
        functools.partial(_combine_body, n_tok=t, tt=tt, alpha=alpha),
        grid_spec=grid_spec,
        out_shape=jax.ShapeDtypeStruct((t, d), F32),
        compiler_params=_params("arbitrary"),
        name="combine",
    )(ppos_flat, ys, x1, gates_t, mods_l, ln_g.reshape(1, d), ln_b.reshape(1, d))


def _moe_layer(x1, h2, mods_l, w_r, b_r, w_gu, b_gu, w_dn, b_dn, layer, ln_g, ln_b, alpha, seq_len):
    t, d = x1.shape
    n_exp = w_r.shape[1]
    tm = MOE_TILE
    idx, gates, rank, cnt = _router(h2, w_r, b_r)
    counts = cnt[:, 0]
    padded = (counts + tm - 1) // tm * tm
    pend = jnp.cumsum(padded)
    pstart = pend - padded
    is_e = idx[..., None] == jnp.arange(n_exp, dtype=I32)
    ppos_flat = (jnp.sum(jnp.where(is_e, pstart, 0), axis=-1) + rank).reshape(-1)
    n_blocks = (t * TOP_K) // tm + n_exp
    n_valid = pend[-1] // tm
    blk = jnp.arange(n_blocks, dtype=I32)
    block_src = jnp.minimum(blk, n_valid - 1)
    block_e = jnp.sum(pend[None, :] <= (block_src * tm)[:, None], axis=1).astype(I32)
    n_rows = n_blocks * tm
    pad_start = jnp.concatenate([pstart + counts, pend[-1:]]).astype(I32)
    pad_cnt = jnp.concatenate([padded - counts, n_rows - pend[-1:]]).astype(I32)
    xs = _dispatch(h2, ppos_flat, pad_start, pad_cnt, n_rows)
    ys = _experts(xs, block_e, block_src.astype(I32), n_valid.reshape(1).astype(I32), w_gu, b_gu, w_dn, b_dn, layer)
    return _combine(ys, ppos_flat, x1, gates.T, mods_l, ln_g, ln_b, alpha, seq_len)


def kernel(x, c, ada_w, ada_b, post_ln_g, post_ln_b, conv_w_pw1, conv_b_pw1, conv_w_dw, conv_b_dw, conv_ln_g, conv_ln_b, conv_w_pw2, conv_b_pw2, w_kv, attn_w_q, attn_lambda, attn_subln_g, attn_w_o, rel_bias_table, router_w, router_b, expert_w_gate_up, expert_b_gate_up, expert_w_down, expert_b_down):
    bsz, s, d = x.shape
    depth = ada_w.shape[0]
    n_a = depth // 2
    alpha = (2 * depth) ** 0.25
    mods = _ada(c, ada_w, ada_b).reshape(depth, bsz, 6, d)
    q = k = v = None
    for l in range(depth):
        mods_l = mods[l]
        if l < n_a:
            u = _conv_a(x, mods_l, conv_w_pw1[l], conv_b_pw1[l])
            x1, h2 = _conv_b(u, x, mods_l, conv_w_dw[l], conv_b_dw[l], conv_ln_g[l], conv_ln_b[l],
                             conv_w_pw2[l], conv_b_pw2[l], post_ln_g[l, 0], post_ln_b[l, 0], alpha)
        else:
            j = l - n_a
            if j == 0:
                q, k, v = _qkv(x, mods_l, w_kv, attn_w_q[j])
            else:
                q = _qkv(x, mods_l, w_kv, attn_w_q[j])[0]
            lambda_init = 0.8 - 0.6 * math.exp(-0.3 * l)
            o = _attn(q, k, v, attn_lambda[j], attn_subln_g[j], rel_bias_table, lambda_init)
            x1, h2 = _attn_out(o, x, mods_l, attn_w_o[j], post_ln_g[l, 0], post_ln_b[l, 0], alpha)
        x = _moe_layer(x1.reshape(bsz * s, d), h2.reshape(bsz * s, d), mods_l, router_w[l], router_b[l],
                       expert_w_gate_up, expert_b_gate_up, expert_w_down, expert_b_down, l,
                       post_ln_g[l, 1], post_ln_b[l, 1], alpha, s).reshape(bsz, s, d)
    return x
```

```python
import functools
import math

import jax
import jax.numpy as jnp
from jax import lax
from jax.experimental import pallas as pl
from jax.experimental.pallas import tpu as pltpu

F32 = jnp.float32
BF16 = jnp.bfloat16
I32 = jnp.int32
HIGHEST = lax.Precision.HIGHEST

CHUNK = 64
CONV_WIDTH = 31
HEAD_DIM = 64
REL_BUCKETS = 32
REL_MAX_DIST = 128
TOP_K = 4
SWIGLU_LIMIT = 7.0
SWIGLU_ALPHA = 1.702
LN_EPS = 1e-5
MASK_VALUE = -1e30

SUBLANES = 8
LANES = 128
VMEM_LIMIT_BYTES = 56 * 1024 * 1024

ADA_TN = 1024
SEQ_TILE = 256
CONV_HALO = 32
CONV_ROWS = 64
CONV_COLS = 256
MOE_TOKENS = 256
MOE_TILE = 256
Q_BLOCK = 128
CAST_ROWS = 128


def _params(*sem):
    return pltpu.CompilerParams(dimension_semantics=sem, vmem_limit_bytes=VMEM_LIMIT_BYTES)


def _layer_norm(x, g, b):
    mu = jnp.mean(x, axis=-1, keepdims=True)
    xc = x - mu
    var = jnp.mean(xc * xc, axis=-1, keepdims=True)
    return xc * lax.rsqrt(var + LN_EPS) * g + b


def _dot(a, b):
    return jnp.dot(a, b, preferred_element_type=F32)


def _ada_body(c_ref, w_ref, b_ref, o_ref):
    c = c_ref[...]
    cond = c * jax.nn.sigmoid(c)
    o_ref[...] = jnp.dot(cond, w_ref[...], preferred_element_type=F32, precision=HIGHEST) + b_ref[...]


def _ada(c, ada_w, ada_b):
    depth, d, n = ada_w.shape
    bsz = c.shape[0]
    tn = min(ADA_TN, n)
    return pl.pallas_call(
        _ada_body,
        grid=(depth, n // tn),
        in_specs=[
            pl.BlockSpec((bsz, d), lambda l, j: (0, 0)),
            pl.BlockSpec((None, d, tn), lambda l, j: (l, 0, j)),
            pl.BlockSpec((None, 1, tn), lambda l, j: (l, 0, j)),
        ],
        out_specs=pl.BlockSpec((None, bsz, tn), lambda l, j: (l, 0, j)),
        out_shape=jax.ShapeDtypeStruct((depth, bsz, n), F32),
        compiler_params=_params("parallel", "parallel"),
        name="ada",
    )(c, ada_w, ada_b.reshape(depth, 1, n))


def _conv_a_body(x_ref, mod_ref, w_ref, b_ref, u_ref):
    d = x_ref.shape[-1]
    h = (x_ref[...] * (1.0 + mod_ref[1:2, :]) + mod_ref[0:1, :]).astype(BF16)
    a = _dot(h, w_ref[:, :d]) + b_ref[:, :d]
    g = _dot(h, w_ref[:, d:]) + b_ref[:, d:]
    u_ref[...] = a * jax.nn.sigmoid(g)


def _conv_a(x, mods_l, w_pw1, b_pw1):
    bsz, s, d = x.shape
    ts = min(SEQ_TILE, s)
    return pl.pallas_call(
        _conv_a_body,
        grid=(bsz, s // ts),
        in_specs=[
            pl.BlockSpec((None, ts, d), lambda b, i: (b, i, 0)),
            pl.BlockSpec((None, 6, d), lambda b, i: (b, 0, 0)),
            pl.BlockSpec((d, 2 * d), lambda b, i: (0, 0)),
            pl.BlockSpec((1, 2 * d), lambda b, i: (0, 0)),
        ],
        out_specs=pl.BlockSpec((None, ts, d), lambda b, i: (b, i, 0)),
        out_shape=jax.ShapeDtypeStruct((bsz, s, d), F32),
        compiler_params=_params("parallel", "parallel"),
        name="conv_a",
    )(x, mods_l, w_pw1.astype(BF16), b_pw1.reshape(1, 2 * d))


def _residual_epilogue(x, y, mod_ref, lng_ref, lnb_ref, alpha, x1_ref, h2_ref):
    x1 = _layer_norm(alpha * x + mod_ref[2:3, :] * y, lng_ref[...], lnb_ref[...])
    x1_ref[...] = x1
    h2_ref[...] = x1 * (1.0 + mod_ref[4:5, :]) + mod_ref[3:4, :]


def _conv_b_body(u_ref, halo_ref, x_ref, mod_ref, wdw_ref, bdw_ref, cg_ref, cb_ref, w2_ref, b2_ref,
                 lng_ref, lnb_ref, x1_ref, h2_ref, win_ref, v_ref, *, alpha):
    ts, d = u_ref.shape
    i = pl.program_id(1)
    halo = halo_ref[...]
    win_ref[0, 0:CONV_HALO, :] = jnp.where(i == 0, jnp.zeros_like(halo), halo)
    win_ref[0, CONV_HALO:, :] = u_ref[...]
    n_shift = ts + CONV_HALO - SUBLANES
    for b in range(1, SUBLANES):
        win_ref[b, 0:n_shift, :] = win_ref[0, b:b + n_shift, :]
    off = CONV_HALO - (CONV_WIDTH - 1)
    rows = min(CONV_ROWS, ts)
    cols = min(CONV_COLS, d)
    for c0 in range(0, d, cols):
        for r0 in range(0, ts, rows):
            acc = jnp.zeros((rows, cols), F32)
            for j in range(CONV_WIDTH):
                a, b = divmod(off + j, SUBLANES)
                r = r0 + a * SUBLANES
                acc = acc + wdw_ref[j:j + 1, c0:c0 + cols] * win_ref[b, r:r + rows, c0:c0 + cols]
            v_ref[r0:r0 + rows, c0:c0 + cols] = acc
    v = _layer_norm(v_ref[...] + bdw_ref[...], cg_ref[...], cb_ref[...])
    v = (v * jax.nn.sigmoid(v)).astype(BF16)
    y = _dot(v, w2_ref[...]) + b2_ref[...]
    _residual_epilogue(x_ref[...], y, mod_ref, lng_ref, lnb_ref, alpha, x1_ref, h2_ref)


def _conv_b(u, x, mods_l, w_dw, b_dw, cln_g, cln_b, w_pw2, b_pw2, ln_g, ln_b, alpha):
    bsz, s, d = x.shape
    ts = min(SEQ_TILE, s)
    hb = ts // CONV_HALO
    row = lambda a: a.reshape(1, d)
    tile = pl.BlockSpec((None, ts, d), lambda b, i: (b, i, 0))
    vec = pl.BlockSpec((1, d), lambda b, i: (0, 0))
    return pl.pallas_call(
        functools.partial(_conv_b_body, alpha=alpha),
        grid=(bsz, s // ts),
        in_specs=[
            tile,
            pl.BlockSpec((None, CONV_HALO, d), lambda b, i: (b, jnp.maximum(i * hb - 1, 0), 0)),
            tile,
            pl.BlockSpec((None, 6, d), lambda b, i: (b, 0, 0)),
            pl.BlockSpec((CONV_WIDTH, d), lambda b, i: (0, 0)),
            vec, vec, vec,
            pl.BlockSpec((d, d), lambda b, i: (0, 0)),
            vec, vec, vec,
        ],
        out_specs=[tile, tile],
        out_shape=[jax.ShapeDtypeStruct((bsz, s, d), F32)] * 2,
        scratch_shapes=[pltpu.VMEM((SUBLANES, ts + CONV_HALO, d), F32), pltpu.VMEM((ts, d), F32)],
        compiler_params=_params("parallel", "parallel"),
        name="conv_b",
    )(u, u, x, mods_l, w_dw, row(b_dw), row(cln_g), row(cln_b), w_pw2.astype(BF16), row(b_pw2),
      row(ln_g), row(ln_b))


def _qkv_body(x_ref, mod_ref, wkv_ref, wq_ref, q_ref, k_ref, v_ref):
    d = x_ref.shape[-1]
    x = x_ref[...]
    kv = _dot(x.astype(BF16), wkv_ref[...])
    k_ref[...] = kv[:, :d].astype(BF16)
    v_ref[...] = kv[:, d:].astype(BF16)
    h = (x * (1.0 + mod_ref[1:2, :]) + mod_ref[0:1, :]).astype(BF16)
    q_ref[...] = (_dot(h, wq_ref[...]) * (HEAD_DIM ** -0.5)).astype(BF16)


def _qkv(x, mods_l, w_kv, w_q):
    bsz, s, d = x.shape
    ts = min(SEQ_TILE, s)
    tile = pl.BlockSpec((None, ts, d), lambda b, i: (b, i, 0))
    return pl.pallas_call(
        _qkv_body,
        grid=(bsz, s // ts),
        in_specs=[
            tile,
            pl.BlockSpec((None, 6, d), lambda b, i: (b, 0, 0)),
            pl.BlockSpec((d, 2 * d), lambda b, i: (0, 0)),
            pl.BlockSpec((d, d), lambda b, i: (0, 0)),
        ],
        out_specs=[tile, tile, tile],
        out_shape=[jax.ShapeDtypeStruct((bsz, s, d), BF16)] * 3,
        compiler_params=_params("parallel", "parallel"),
        name="qkv",
    )(x, mods_l, w_kv.astype(BF16), w_q.astype(BF16))


def _t5_bucket(rel):
    nb = REL_BUCKETS // 2
    ret = jnp.where(rel > 0, nb, 0)
    n = jnp.abs(rel)
    max_exact = nb // 2
    large = max_exact + (jnp.log(jnp.maximum(n, 1).astype(F32) / max_exact)
                         / math.log(REL_MAX_DIST / max_exact) * (nb - max_exact)).astype(I32)
    large = jnp.minimum(large, nb - 1)
    return ret + jnp.where(n < max_exact, n, large)


def _bucket_strip(s):
    r = jnp.arange(Q_BLOCK, dtype=I32)[:, None]
    kp = jnp.arange(s, dtype=I32)[None, :] - (s - Q_BLOCK)
    bucket = _t5_bucket(kp - r)
    visible = jnp.floor_divide(kp, CHUNK) <= (r // CHUNK)
    return jnp.where(visible, bucket, REL_BUCKETS)


def _attn_body(tab_ref, q_ref, k_ref, v_ref, bkt_ref, lam_ref, sg_ref, o_ref, bias_ref, *, lambda_init, n_heads):
    h = pl.program_id(0)
    b = pl.program_id(1)
    s = q_ref.shape[0]

    @pl.when(b == 0)
    def _():
        bk = bkt_ref[...]
        acc = jnp.full(bk.shape, MASK_VALUE, F32)
        for r in range(REL_BUCKETS):
            acc = jnp.where(bk == r, tab_ref[r * n_heads + h], acc)
        bias_ref[0:Q_BLOCK, :] = acc
        bias_ref[Q_BLOCK:, :] = acc

    lp = lam_ref[...]
    lam = (jnp.exp(jnp.sum(lp[0:1, :] * lp[1:2, :], axis=-1, keepdims=True))
           - jnp.exp(jnp.sum(lp[2:3, :] * lp[3:4, :], axis=-1, keepdims=True)) + lambda_init)
    lane = lax.broadcasted_iota(I32, (Q_BLOCK, 2 * HEAD_DIM), 1)
    nt = (((1,), (1,)), ((), ()))
    for i in range(s // Q_BLOCK):
        n_keys = (i + 1) * Q_BLOCK
        q = q_ref[i * Q_BLOCK:(i + 1) * Q_BLOCK, :]
        qq = jnp.concatenate([jnp.where(lane < HEAD_DIM, q, jnp.zeros_like(q)),
                              jnp.where(lane >= HEAD_DIM, q, jnp.zeros_like(q))], axis=0)
        sc = lax.dot_general(qq, k_ref[0:n_keys, :], nt, preferred_element_type=F32) + bias_ref[:, s - n_keys:s]
        p = jnp.exp(sc - jnp.max(sc, axis=-1, keepdims=True))
        denom = jnp.sum(p, axis=-1, keepdims=True)
        pv = _dot(p.astype(BF16), v_ref[0:n_keys, :])
        o = pv[:Q_BLOCK] * (1.0 / denom[:Q_BLOCK]) - pv[Q_BLOCK:] * (lam / denom[Q_BLOCK:])
        o = o * lax.rsqrt(jnp.mean(o * o, axis=-1, keepdims=True) + LN_EPS) * sg_ref[...]
        o_ref[i * Q_BLOCK:(i + 1) * Q_BLOCK, :] = (o * (1.0 - lambda_init)).astype(BF16)


def _attn(q, k, v, lam_p, subln_g, rel_table, lambda_init):
    bsz, s, d = q.shape
    hd2 = 2 * HEAD_DIM
    n_heads = d // hd2
    head = pl.BlockSpec((None, s, hd2), lambda h, b, tab: (b, 0, h))
    grid_spec = pltpu.PrefetchScalarGridSpec(
        num_scalar_prefetch=1,
        grid=(n_heads, bsz),
        in_specs=[
            head, head, head,
            pl.BlockSpec((Q_BLOCK, s), lambda h, b, tab: (0, 0)),
            pl.BlockSpec((4, HEAD_DIM), lambda h, b, tab: (0, 0)),
            pl.BlockSpec((1, hd2), lambda h, b, tab: (0, 0)),
        ],
        out_specs=head,
        scratch_shapes=[pltpu.VMEM((2 * Q_BLOCK, s), F32)],
    )
    return pl.pallas_call(
        functools.partial(_attn_body, lambda_init=lambda_init, n_heads=n_heads),
        grid_spec=grid_spec,
        out_shape=jax.ShapeDtypeStruct((bsz, s, d), BF16),
        compiler_params=_params("arbitrary", "arbitrary"),
        name="attn",
    )(rel_table.reshape(-1), q, k, v, _bucket_strip(s), lam_p, subln_g.reshape(1, hd2))


def _attn_out_body(o_ref, x_ref, mod_ref, wo_ref, lng_ref, lnb_ref, x1_ref, h2_ref, *, alpha):
    y = _dot(o_ref[...], wo_ref[...])
    _residual_epilogue(x_ref[...], y, mod_ref, lng_ref, lnb_ref, alpha, x1_ref, h2_ref)


def _attn_out(o, x, mods_l, w_o, ln_g, ln_b, alpha):
    bsz, s, d = x.shape
    ts = min(SEQ_TILE, s)
    tile = pl.BlockSpec((None, ts, d), lambda b, i: (b, i, 0))
    vec = pl.BlockSpec((1, d), lambda b, i: (0, 0))
    return pl.pallas_call(
        functools.partial(_attn_out_body, alpha=alpha),
        grid=(bsz, s // ts),
        in_specs=[tile, tile, pl.BlockSpec((None, 6, d), lambda b, i: (b, 0, 0)),
                  pl.BlockSpec((d, d), lambda b, i: (0, 0)), vec, vec],
        out_specs=[tile, tile],
        out_shape=[jax.ShapeDtypeStruct((bsz, s, d), F32)] * 2,
        compiler_params=_params("parallel", "parallel"),
        name="attn_out",
    )(o, x, mods_l, w_o.astype(BF16), ln_g.reshape(1, d), ln_b.reshape(1, d))


def _router_body(h_ref, wt_ref, b_ref, tri_ref, idx_ref, gate_ref, rank_ref, cnt_ref):
    logits = lax.dot_general(wt_ref[...], h_ref[...], (((1,), (1,)), ((), ())),
                             preferred_element_type=F32, precision=HIGHEST) + b_ref[...]
    n_exp, tr = logits.shape
    eio = lax.broadcasted_iota(I32, (n_exp, tr), 0)
    work = logits
    vals, idxs = [], []
    for _ in range(TOP_K):
        m = jnp.max(work, axis=0, keepdims=True)
        am = jnp.min(jnp.where(work == m, eio, n_exp), axis=0, keepdims=True)
        vals.append(m)
        idxs.append(am)
        work = jnp.where(eio == am, -jnp.inf, work)
    ex = [jnp.exp(v - vals[0]) for v in vals]
    den = ex[0] + ex[1] + ex[2] + ex[3]
    onehot = jnp.zeros((n_exp, tr), F32)
    for k in range(TOP_K):
        onehot = onehot + (eio == idxs[k]).astype(F32)
    before = _dot(onehot.astype(BF16), tri_ref[...])
    for k in range(TOP_K):
        idx_ref[k:k + 1, :] = idxs[k]
        gate_ref[k:k + 1, :] = ex[k] / den
        rank_ref[k:k + 1, :] = jnp.sum(jnp.where(eio == idxs[k], before, 0.0), axis=0, keepdims=True).astype(I32)
    cnt_ref[...] = jnp.sum(onehot, axis=1, keepdims=True).astype(I32)


def _router(h2, w_r, b_r):
    t, d = h2.shape
    n_exp = w_r.shape[1]
    tr = min(MOE_TOKENS, t)
    pos = jnp.arange(tr, dtype=I32)
    tri = (pos[:, None] < pos[None, :]).astype(BF16)
    tok = pl.BlockSpec((TOP_K, tr), lambda i: (0, i))
    return pl.pallas_call(
        _router_body,
        grid=(t // tr,),
        in_specs=[
            pl.BlockSpec((tr, d), lambda i: (i, 0)),
            pl.BlockSpec((n_exp, d), lambda i: (0, 0)),
            pl.BlockSpec((n_exp, 1), lambda i: (0, 0)),
            pl.BlockSpec((tr, tr), lambda i: (0, 0)),
        ],
        out_specs=[tok, tok, tok, pl.BlockSpec((None, n_exp, 1), lambda i: (i, 0, 0))],
        out_shape=[jax.ShapeDtypeStruct((TOP_K, t), I32), jax.ShapeDtypeStruct((TOP_K, t), F32),
                   jax.ShapeDtypeStruct((TOP_K, t), I32), jax.ShapeDtypeStruct((t // tr, n_exp, 1), I32)],
        compiler_params=_params("parallel"),
        name="router",
    )(h2, w_r.T, b_r.reshape(n_exp, 1), tri)


def _segment_copies(src, src_row, dst, dst_row, cnt, max_rows, sem, wait):
    bit = max_rows
    while bit >= SUBLANES:
        @pl.when((cnt & bit) != 0)
        def _(bit=bit):
            off = (cnt // (2 * bit)) * (2 * bit)
            dma = pltpu.make_async_copy(src.at[pl.ds(pl.multiple_of(src_row + off, SUBLANES), bit)],
                                        dst.at[pl.ds(pl.multiple_of(dst_row + off, SUBLANES), bit)], sem)
            dma.wait() if wait else dma.start()
        bit //= 2


def _zero_fill(zero_ref, xs_hbm, start, cnt, sem, wait):
    zr = zero_ref.shape[0]

    def full(r, c):
        dma = pltpu.make_async_copy(zero_ref, xs_hbm.at[pl.ds(pl.multiple_of(start + r * zr, SUBLANES), zr)], sem)
        dma.wait() if wait else dma.start()
        return c

    n_full = cnt // zr
    lax.fori_loop(0, n_full, full, 0)
    _segment_copies(zero_ref, 0, xs_hbm, start + n_full * zr, cnt - n_full * zr, zr // 2, sem, wait)


def _one_hot_hits(iota, pos_of):
    hit = iota == pos_of(0)
    for k in range(1, TOP_K):
        hit = hit | (iota == pos_of(k))
    return hit


def _dispatch_body(goff_ref, loff_ref, cp_ref, pad_start_ref, pad_cnt_ref, h_ref, lp_ref, gate_ref, xs_hbm,
                   xl_ref, zero_ref, sem, zsem, *, n_exp):
    j = pl.program_id(0)
    tl, d = h_ref.shape
    lr = xl_ref.shape[0]

    @pl.when(j == 0)
    def _():
        zero_ref[...] = jnp.zeros_like(zero_ref)
        for wait in (False, True):
            def per_range(e, carry, wait=wait):
                _zero_fill(zero_ref, xs_hbm, pad_start_ref[e], pad_cnt_ref[e], zsem, wait)
                return carry
            lax.fori_loop(0, pad_start_ref.shape[0], per_range, 0)

    riota = lax.broadcasted_iota(I32, (lr, tl), 0)
    perm = _one_hot_hits(riota, lambda k: lp_ref[k:k + 1, :]).astype(F32).astype(BF16)
    xl_ref[:, 0:d] = _dot(perm, h_ref[...].astype(BF16))
    gsel = jnp.zeros((lr, tl), F32)
    for k in range(TOP_K):
        gsel = gsel + jnp.where(riota == lp_ref[k:k + 1, :], gate_ref[k:k + 1, :], 0.0)
    xl_ref[:, d:] = jnp.broadcast_to(jnp.sum(gsel, axis=1, keepdims=True), (lr, LANES))
    for wait in (False, True):
        def per_expert(e, carry, wait=wait):
            seg = j * n_exp + e
            _segment_copies(xl_ref, loff_ref[seg], xs_hbm, goff_ref[seg], cp_ref[seg], tl, sem, wait)
            return carry
        lax.fori_loop(0, n_exp, per_expert, 0)


def _dispatch(h2, lp, gates, goff, loff, cp, pad_start, pad_cnt, n_rows):
    t, d = h2.shape
    tl = min(MOE_TOKENS, t)
    n_exp = goff.shape[0] // (t // tl)
    lr = TOP_K * tl + n_exp * SUBLANES
    tok = pl.BlockSpec((TOP_K, tl), lambda j, *_: (0, j))
    grid_spec = pltpu.PrefetchScalarGridSpec(
        num_scalar_prefetch=5,
        grid=(t // tl,),
        in_specs=[pl.BlockSpec((tl, d), lambda j, *_: (j, 0)), tok, tok],
        out_specs=pl.BlockSpec(memory_space=pl.ANY),
        scratch_shapes=[pltpu.VMEM((lr, d + LANES), F32), pltpu.VMEM((MOE_TILE, d + LANES), F32),
                        pltpu.SemaphoreType.DMA, pltpu.SemaphoreType.DMA],
    )
    return pl.pallas_call(
        functools.partial(_dispatch_body, n_exp=n_exp),
        grid_spec=grid_spec,
        out_shape=jax.ShapeDtypeStruct((n_rows, d + LANES), F32),
        compiler_params=_params("arbitrary"),
        name="dispatch",
    )(goff, loff, cp, pad_start, pad_cnt, h2, lp, gates)


def _experts_body(be_ref, bsrc_ref, nvalid_ref, xs_ref, wgu_ref, bgu_ref, wdn_ref, bdn_ref, ys_ref,
                  wgu_bf, wdn_bf):
    i = pl.program_id(0)
    e = be_ref[i]
    prev = be_ref[jnp.maximum(i - 1, 0)]
    f = wdn_ref.shape[0]

    @pl.when((i == 0) | (e != prev))
    def _():
        def cast(ref_in, ref_out):
            def step(r, c):
                rows = pl.ds(pl.multiple_of(r * CAST_ROWS, CAST_ROWS), CAST_ROWS)
                ref_out[rows, :] = ref_in[rows, :].astype(BF16)
                return c
            lax.fori_loop(0, ref_in.shape[0] // CAST_ROWS, step, 0)
        cast(wgu_ref, wgu_bf)
        cast(wdn_ref, wdn_bf)

    @pl.when(i < nvalid_ref[0])
    def _():
        d = wgu_ref.shape[0]
        x = xs_ref[:, 0:d].astype(BF16)
        gate = jnp.minimum(_dot(x, wgu_bf[:, :f]) + bgu_ref[:, :f], SWIGLU_LIMIT)
        lin = jnp.clip(_dot(x, wgu_bf[:, f:]) + bgu_ref[:, f:], -SWIGLU_LIMIT, SWIGLU_LIMIT)
        act = (gate * jax.nn.sigmoid(SWIGLU_ALPHA * gate) * (lin + 1.0)).astype(BF16)
        ys_ref[...] = (_dot(act, wdn_bf[...]) + bdn_ref[...]) * xs_ref[:, d:d + 1]

    @pl.when(i >= nvalid_ref[0])
    def _():
        ys_ref[...] = jnp.zeros_like(ys_ref)


def _experts(xs, block_e, block_src, n_valid, w_gu, b_gu, w_dn, b_dn, layer):
    n_rows, dx = xs.shape
    depth, n_exp, d, f2 = w_gu.shape
    f = f2 // 2
    tm = MOE_TILE
    grid_spec = pltpu.PrefetchScalarGridSpec(
        num_scalar_prefetch=3,
        grid=(n_rows // tm,),
        in_specs=[
            pl.BlockSpec((tm, dx), lambda i, be, bs, nv: (bs[i], 0)),
            pl.BlockSpec((None, None, d, f2), lambda i, be, bs, nv: (layer, be[i], 0, 0)),
            pl.BlockSpec((None, None, 1, f2), lambda i, be, bs, nv: (layer, be[i], 0, 0)),
            pl.BlockSpec((None, None, f, d), lambda i, be, bs, nv: (layer, be[i], 0, 0)),
            pl.BlockSpec((None, None, 1, d), lambda i, be, bs, nv: (layer, be[i], 0, 0)),
        ],
        out_specs=pl.BlockSpec((tm, d), lambda i, be, bs, nv: (i, 0)),
        scratch_shapes=[pltpu.VMEM((d, f2), BF16), pltpu.VMEM((f, d), BF16)],
    )
    return pl.pallas_call(
        _experts_body,
        grid_spec=grid_spec,
        out_shape=jax.ShapeDtypeStruct((n_rows, d), F32),
        compiler_params=_params("arbitrary"),
        name="experts",
    )(block_e, block_src, n_valid, xs, w_gu, b_gu.reshape(depth, n_exp, 1, f2), w_dn,
      b_dn.reshape(depth, n_exp, 1, d))


def _combine_body(goff_ref, loff_ref, cp_ref, ltot_ref, ys_hbm, x_ref, lp_ref, mod_ref, lng_ref, lnb_ref, o_ref,
                  yl_ref, sem, *, n_exp, alpha):
    j = pl.program_id(0)
    tl = x_ref.shape[0]
    lr = yl_ref.shape[0]

    def segments(wait):
        def per_expert(e, carry):
            seg = j * n_exp + e
            _segment_copies(ys_hbm, goff_ref[seg], yl_ref, loff_ref[seg], cp_ref[seg], tl, sem, wait)
            return carry
        lax.fori_loop(0, n_exp, per_expert, 0)

    segments(False)

    def zero_rows(r, carry):
        yl_ref[pl.ds(pl.multiple_of(r * SUBLANES, SUBLANES), SUBLANES), :] = jnp.zeros((SUBLANES, yl_ref.shape[1]), F32)
        return carry

    lax.fori_loop(ltot_ref[j] // SUBLANES, lr // SUBLANES, zero_rows, 0)
    liota = lax.broadcasted_iota(I32, (tl, lr), 1)
    pick = _one_hot_hits(liota, lambda k: lp_ref[:, k:k + 1]).astype(F32).astype(BF16)
    segments(True)
    y = _dot(pick, yl_ref[...].astype(BF16))
    o_ref[...] = _layer_norm(alpha * x_ref[...] + mod_ref[5:6, :] * y, lng_ref[...], lnb_ref[...])


def _combine(ys, lp_t, goff, loff, cp, ltot, x1, mods_l, ln_g, ln_b, alpha, seq_len):
    t, d = x1.shape
    tl = min(MOE_TOKENS, t)
    n_exp = goff.shape[0] // (t // tl)
    lr = TOP_K * tl + n_exp * SUBLANES
    per_seq = seq_len // tl
    vec = pl.BlockSpec((1, d), lambda j, *_: (0, 0))
    grid_spec = pltpu.PrefetchScalarGridSpec(
        num_scalar_prefetch=4,
        grid=(t // tl,),
        in_specs=[
            pl.BlockSpec(memory_space=pl.ANY),
            pl.BlockSpec((tl, d), lambda j, *_: (j, 0)),
            pl.BlockSpec((tl, TOP_K), lambda j, *_: (j, 0)),
            pl.BlockSpec((None, 6, d), lambda j, *_: (j // per_seq, 0, 0)),
            vec, vec,
        ],
        out_specs=pl.BlockSpec((tl, d), lambda j, *_: (j, 0)),
        scratch_shapes=[pltpu.VMEM((lr, d), F32), pltpu.SemaphoreType.DMA],
    )
    return pl.pallas_call(
        functools.partial(_combine_body, n_exp=n_exp, alpha=alpha),
        grid_spec=grid_spec,
        out_shape=jax.ShapeDtypeStruct((t, d), F32),
        compiler_params=_params("arbitrary"),
        name="combine",
    )(goff, loff, cp, ltot, ys, x1, lp_t, mods_l, ln_g.reshape(1, d), ln_b.reshape(1, d))


def _moe_layer(x1, h2, mods_l, w_r, b_r, w_gu, b_gu, w_dn, b_dn, layer, ln_g, ln_b, alpha, seq_len):
    t, d = x1.shape
    n_exp = w_r.shape[1]
    tm = MOE_TILE
    tl = min(MOE_TOKENS, t)
    n_tiles = t // tl
    idx, gates, rank, cnt = _router(h2, w_r, b_r)
    cp = (cnt[:, :, 0] + SUBLANES - 1) // SUBLANES * SUBLANES
    tot = jnp.sum(cp, axis=0)
    padded = (tot + tm - 1) // tm * tm
    pend = jnp.cumsum(padded)
    pstart = pend - padded
    goff = pstart[None, :] + jnp.cumsum(cp, axis=0) - cp
    loff = jnp.cumsum(cp, axis=1) - cp
    ltot = jnp.sum(cp, axis=1)
    is_e = idx[..., None] == jnp.arange(n_exp, dtype=I32)
    lp = jnp.sum(jnp.where(is_e, jnp.repeat(loff, tl, axis=0)[None], 0), axis=-1) + rank
    n_blocks = -(-(t * TOP_K + n_tiles * n_exp * (SUBLANES - 1)) // tm) + n_exp
    n_valid = pend[-1] // tm
    blk = jnp.arange(n_blocks, dtype=I32)
    block_src = jnp.minimum(blk, n_valid - 1)
    block_e = jnp.sum(pend[None, :] <= (block_src * tm)[:, None], axis=1).astype(I32)
    n_rows = n_blocks * tm
    pad_start = jnp.concatenate([pstart + tot, pend[-1:]]).astype(I32)
    pad_cnt = jnp.concatenate([padded - tot, n_rows - pend[-1:]]).astype(I32)
    flat = lambda a: a.reshape(-1).astype(I32)
    xs = _dispatch(h2, lp, gates, flat(goff), flat(loff), flat(cp), pad_start, pad_cnt, n_rows)
    ys = _experts(xs, block_e, block_src.astype(I32), n_valid.reshape(1).astype(I32), w_gu, b_gu, w_dn, b_dn, layer)
    return _combine(ys, lp.T, flat(goff), flat(loff), flat(cp), ltot.astype(I32), x1, mods_l, ln_g, ln_b, alpha,
                    seq_len)


def kernel(x, c, ada_w, ada_b, post_ln_g, post_ln_b, conv_w_pw1, conv_b_pw1, conv_w_dw, conv_b_dw, conv_ln_g, conv_ln_b, conv_w_pw2, conv_b_pw2, w_kv, attn_w_q, attn_lambda, attn_subln_g, attn_w_o, rel_bias_table, router_w, router_b, expert_w_gate_up, expert_b_gate_up, expert_w_down, expert_b_down):
    bsz, s, d = x.shape
    depth = ada_w.shape[0]
    n_a = depth // 2
    alpha = (2 * depth) ** 0.25
    mods = _ada(c, ada_w, ada_b).reshape(depth, bsz, 6, d)
    q = k = v = None
    for l in range(depth):
        mods_l = mods[l]
        if l < n_a:
            u = _conv_a(x, mods_l, conv_w_pw1[l], conv_b_pw1[l])
            x1, h2 = _conv_b(u, x, mods_l, conv_w_dw[l], conv_b_dw[l], conv_ln_g[l], conv_ln_b[l],
                             conv_w_pw2[l], conv_b_pw2[l], post_ln_g[l, 0], post_ln_b[l, 0], alpha)
        else:
            j = l - n_a
            if j == 0:
                q, k, v = _qkv(x, mods_l, w_kv, attn_w_q[j])
            else:
                q = _qkv(x, mods_l, w_kv, attn_w_q[j])[0]
            lambda_init = 0.8 - 0.6 * math.exp(-0.3 * l)
            o = _attn(q, k, v, attn_lambda[j], attn_subln_g[j], rel_bias_table, lambda_init)
            x1, h2 = _attn_out(o, x, mods_l, attn_w_o[j], post_ln_g[l, 0], post_ln_b[l, 0], alpha)
        x = _moe_layer(x1.reshape(bsz * s, d), h2.reshape(bsz * s, d), mods_l, router_w[l], router_b[l],
                       expert_w_gate_up, expert_b_gate_up, expert_w_down, expert_b_down, l,
                       post_ln_g[l, 1], post_ln_b[l, 1], alpha, s).reshape(bsz, s, d)
    return x
```

```python
import functools
import math

import jax
import jax.numpy as jnp
from jax import lax
from jax.experimental import pallas as pl
from jax.experimental.pallas import tpu as pltpu

F32 = jnp.float32
BF16 = jnp.bfloat16
I32 = jnp.int32
HIGHEST = lax.Precision.HIGHEST

CHUNK = 64
CONV_WIDTH = 31
HEAD_DIM = 64
REL_BUCKETS = 32
REL_MAX_DIST = 128
TOP_K = 4
SWIGLU_LIMIT = 7.0
SWIGLU_ALPHA = 1.702
LN_EPS = 1e-5
MASK_VALUE = -1e30

SUBLANES = 8
LANES = 128
VMEM_LIMIT_BYTES = 56 * 1024 * 1024

ADA_TN = 1024
SEQ_TILE = 256
CONV_HALO = 32
CONV_ROWS = 64
CONV_COLS = 256
MOE_TOKENS = 256
MOE_TILE = 512
Q_BLOCK = 128
CAST_ROWS = 128


def _params(*sem):
    return pltpu.CompilerParams(dimension_semantics=sem, vmem_limit_bytes=VMEM_LIMIT_BYTES)


def _layer_norm(x, g, b):
    mu = jnp.mean(x, axis=-1, keepdims=True)
    xc = x - mu
    var = jnp.mean(xc * xc, axis=-1, keepdims=True)
    return xc * lax.rsqrt(var + LN_EPS) * g + b


def _dot(a, b):
    return jnp.dot(a, b, preferred_element_type=F32)


def _ada_body(c_ref, w_ref, b_ref, o_ref):
    c = c_ref[...]
    cond = c * jax.nn.sigmoid(c)
    o_ref[...] = jnp.dot(cond, w_ref[...], preferred_element_type=F32, precision=HIGHEST) + b_ref[...]


def _ada(c, ada_w, ada_b):
    depth, d, n = ada_w.shape
    bsz = c.shape[0]
    tn = min(ADA_TN, n)
    return pl.pallas_call(
        _ada_body,
        grid=(depth, n // tn),
        in_specs=[
            pl.BlockSpec((bsz, d), lambda l, j: (0, 0)),
            pl.BlockSpec((None, d, tn), lambda l, j: (l, 0, j)),
            pl.BlockSpec((None, 1, tn), lambda l, j: (l, 0, j)),
        ],
        out_specs=pl.BlockSpec((None, bsz, tn), lambda l, j: (l, 0, j)),
        out_shape=jax.ShapeDtypeStruct((depth, bsz, n), F32),
        compiler_params=_params("parallel", "parallel"),
        name="ada",
    )(c, ada_w, ada_b.reshape(depth, 1, n))


def _conv_a_body(x_ref, mod_ref, w_ref, b_ref, u_ref):
    d = x_ref.shape[-1]
    h = (x_ref[...] * (1.0 + mod_ref[1:2, :]) + mod_ref[0:1, :]).astype(BF16)
    a = _dot(h, w_ref[:, :d]) + b_ref[:, :d]
    g = _dot(h, w_ref[:, d:]) + b_ref[:, d:]
    u_ref[...] = a * jax.nn.sigmoid(g)


def _conv_a(x, mods_l, w_pw1, b_pw1):
    bsz, s, d = x.shape
    ts = min(SEQ_TILE, s)
    return pl.pallas_call(
        _conv_a_body,
        grid=(bsz, s // ts),
        in_specs=[
            pl.BlockSpec((None, ts, d), lambda b, i: (b, i, 0)),
            pl.BlockSpec((None, 6, d), lambda b, i: (b, 0, 0)),
            pl.BlockSpec((d, 2 * d), lambda b, i: (0, 0)),
            pl.BlockSpec((1, 2 * d), lambda b, i: (0, 0)),
        ],
        out_specs=pl.BlockSpec((None, ts, d), lambda b, i: (b, i, 0)),
        out_shape=jax.ShapeDtypeStruct((bsz, s, d), F32),
        compiler_params=_params("parallel", "parallel"),
        name="conv_a",
    )(x, mods_l, w_pw1.astype(BF16), b_pw1.reshape(1, 2 * d))


def _residual_epilogue(x, y, mod_ref, lng_ref, lnb_ref, alpha, x1_ref, h2_ref):
    x1 = _layer_norm(alpha * x + mod_ref[2:3, :] * y, lng_ref[...], lnb_ref[...])
    x1_ref[...] = x1
    h2_ref[...] = x1 * (1.0 + mod_ref[4:5, :]) + mod_ref[3:4, :]


def _conv_b_body(u_ref, halo_ref, x_ref, mod_ref, wdw_ref, bdw_ref, cg_ref, cb_ref, w2_ref, b2_ref,
                 lng_ref, lnb_ref, x1_ref, h2_ref, win_ref, v_ref, *, alpha):
    ts, d = u_ref.shape
    i = pl.program_id(1)
    halo = halo_ref[...]
    win_ref[0, 0:CONV_HALO, :] = jnp.where(i == 0, jnp.zeros_like(halo), halo)
    win_ref[0, CONV_HALO:, :] = u_ref[...]
    n_shift = ts + CONV_HALO - SUBLANES
    for b in range(1, SUBLANES):
        win_ref[b, 0:n_shift, :] = win_ref[0, b:b + n_shift, :]
    off = CONV_HALO - (CONV_WIDTH - 1)
    rows = min(CONV_ROWS, ts)
    cols = min(CONV_COLS, d)
    for c0 in range(0, d, cols):
        for r0 in range(0, ts, rows):
            acc = jnp.zeros((rows, cols), F32)
            for j in range(CONV_WIDTH):
                a, b = divmod(off + j, SUBLANES)
                r = r0 + a * SUBLANES
                acc = acc + wdw_ref[j:j + 1, c0:c0 + cols] * win_ref[b, r:r + rows, c0:c0 + cols]
            v_ref[r0:r0 + rows, c0:c0 + cols] = acc
    v = _layer_norm(v_ref[...] + bdw_ref[...], cg_ref[...], cb_ref[...])
    v = (v * jax.nn.sigmoid(v)).astype(BF16)
    y = _dot(v, w2_ref[...]) + b2_ref[...]
    _residual_epilogue(x_ref[...], y, mod_ref, lng_ref, lnb_ref, alpha, x1_ref, h2_ref)


def _conv_b(u, x, mods_l, w_dw, b_dw, cln_g, cln_b, w_pw2, b_pw2, ln_g, ln_b, alpha):
    bsz, s, d = x.shape
    ts = min(SEQ_TILE, s)
    hb = ts // CONV_HALO
    row = lambda a: a.reshape(1, d)
    tile = pl.BlockSpec((None, ts, d), lambda b, i: (b, i, 0))
    vec = pl.BlockSpec((1, d), lambda b, i: (0, 0))
    return pl.pallas_call(
        functools.partial(_conv_b_body, alpha=alpha),
        grid=(bsz, s // ts),
        in_specs=[
            tile,
            pl.BlockSpec((None, CONV_HALO, d), lambda b, i: (b, jnp.maximum(i * hb - 1, 0), 0)),
            tile,
            pl.BlockSpec((None, 6, d), lambda b, i: (b, 0, 0)),
            pl.BlockSpec((CONV_WIDTH, d), lambda b, i: (0, 0)),
            vec, vec, vec,
            pl.BlockSpec((d, d), lambda b, i: (0, 0)),
            vec, vec, vec,
        ],
        out_specs=[tile, tile],
        out_shape=[jax.ShapeDtypeStruct((bsz, s, d), F32)] * 2,
        scratch_shapes=[pltpu.VMEM((SUBLANES, ts + CONV_HALO, d), F32), pltpu.VMEM((ts, d), F32)],
        compiler_params=_params("parallel", "parallel"),
        name="conv_b",
    )(u, u, x, mods_l, w_dw, row(b_dw), row(cln_g), row(cln_b), w_pw2.astype(BF16), row(b_pw2),
      row(ln_g), row(ln_b))


def _qkv_body(x_ref, mod_ref, wkv_ref, wq_ref, q_ref, k_ref, v_ref):
    d = x_ref.shape[-1]
    x = x_ref[...]
    kv = _dot(x.astype(BF16), wkv_ref[...])
    k_ref[...] = kv[:, :d].astype(BF16)
    v_ref[...] = kv[:, d:].astype(BF16)
    h = (x * (1.0 + mod_ref[1:2, :]) + mod_ref[0:1, :]).astype(BF16)
    q_ref[...] = (_dot(h, wq_ref[...]) * (HEAD_DIM ** -0.5)).astype(BF16)


def _qkv(x, mods_l, w_kv, w_q):
    bsz, s, d = x.shape
    ts = min(SEQ_TILE, s)
    tile = pl.BlockSpec((None, ts, d), lambda b, i: (b, i, 0))
    return pl.pallas_call(
        _qkv_body,
        grid=(bsz, s // ts),
        in_specs=[
            tile,
            pl.BlockSpec((None, 6, d), lambda b, i: (b, 0, 0)),
            pl.BlockSpec((d, 2 * d), lambda b, i: (0, 0)),
            pl.BlockSpec((d, d), lambda b, i: (0, 0)),
        ],
        out_specs=[tile, tile, tile],
        out_shape=[jax.ShapeDtypeStruct((bsz, s, d), BF16)] * 3,
        compiler_params=_params("parallel", "parallel"),
        name="qkv",
    )(x, mods_l, w_kv.astype(BF16), w_q.astype(BF16))


def _t5_bucket(rel):
    nb = REL_BUCKETS // 2
    ret = jnp.where(rel > 0, nb, 0)
    n = jnp.abs(rel)
    max_exact = nb // 2
    large = max_exact + (jnp.log(jnp.maximum(n, 1).astype(F32) / max_exact)
                         / math.log(REL_MAX_DIST / max_exact) * (nb - max_exact)).astype(I32)
    large = jnp.minimum(large, nb - 1)
    return ret + jnp.where(n < max_exact, n, large)


def _bucket_strip(s):
    r = jnp.arange(Q_BLOCK, dtype=I32)[:, None]
    kp = jnp.arange(s, dtype=I32)[None, :] - (s - Q_BLOCK)
    bucket = _t5_bucket(kp - r)
    visible = jnp.floor_divide(kp, CHUNK) <= (r // CHUNK)
    return jnp.where(visible, bucket, REL_BUCKETS)


def _attn_body(tab_ref, q_ref, k_ref, v_ref, bkt_ref, lam_ref, sg_ref, o_ref, bias_ref, *, lambda_init, n_heads):
    h = pl.program_id(0)
    b = pl.program_id(1)
    s = q_ref.shape[0]

    @pl.when(b == 0)
    def _():
        bk = bkt_ref[...]
        acc = jnp.full(bk.shape, MASK_VALUE, F32)
        for r in range(REL_BUCKETS):
            acc = jnp.where(bk == r, tab_ref[r * n_heads + h], acc)
        bias_ref[0:Q_BLOCK, :] = acc
        bias_ref[Q_BLOCK:, :] = acc

    lp = lam_ref[...]
    lam = (jnp.exp(jnp.sum(lp[0:1, :] * lp[1:2, :], axis=-1, keepdims=True))
           - jnp.exp(jnp.sum(lp[2:3, :] * lp[3:4, :], axis=-1, keepdims=True)) + lambda_init)
    lane = lax.broadcasted_iota(I32, (Q_BLOCK, 2 * HEAD_DIM), 1)
    nt = (((1,), (1,)), ((), ()))
    for i in range(s // Q_BLOCK):
        n_keys = (i + 1) * Q_BLOCK
        q = q_ref[i * Q_BLOCK:(i + 1) * Q_BLOCK, :]
        qq = jnp.concatenate([jnp.where(lane < HEAD_DIM, q, jnp.zeros_like(q)),
                              jnp.where(lane >= HEAD_DIM, q, jnp.zeros_like(q))], axis=0)
        sc = lax.dot_general(qq, k_ref[0:n_keys, :], nt, preferred_element_type=F32) + bias_ref[:, s - n_keys:s]
        p = jnp.exp(sc - jnp.max(sc, axis=-1, keepdims=True))
        denom = jnp.sum(p, axis=-1, keepdims=True)
        pv = _dot(p.astype(BF16), v_ref[0:n_keys, :])
        o = pv[:Q_BLOCK] * (1.0 / denom[:Q_BLOCK]) - pv[Q_BLOCK:] * (lam / denom[Q_BLOCK:])
        o = o * lax.rsqrt(jnp.mean(o * o, axis=-1, keepdims=True) + LN_EPS) * sg_ref[...]
        o_ref[i * Q_BLOCK:(i + 1) * Q_BLOCK, :] = (o * (1.0 - lambda_init)).astype(BF16)


def _attn(q, k, v, lam_p, subln_g, rel_table, lambda_init):
    bsz, s, d = q.shape
    hd2 = 2 * HEAD_DIM
    n_heads = d // hd2
    head = pl.BlockSpec((None, s, hd2), lambda h, b, tab: (b, 0, h))
    grid_spec = pltpu.PrefetchScalarGridSpec(
        num_scalar_prefetch=1,
        grid=(n_heads, bsz),
        in_specs=[
            head, head, head,
            pl.BlockSpec((Q_BLOCK, s), lambda h, b, tab: (0, 0)),
            pl.BlockSpec((4, HEAD_DIM), lambda h, b, tab: (0, 0)),
            pl.BlockSpec((1, hd2), lambda h, b, tab: (0, 0)),
        ],
        out_specs=head,
        scratch_shapes=[pltpu.VMEM((2 * Q_BLOCK, s), F32)],
    )
    return pl.pallas_call(
        functools.partial(_attn_body, lambda_init=lambda_init, n_heads=n_heads),
        grid_spec=grid_spec,
        out_shape=jax.ShapeDtypeStruct((bsz, s, d), BF16),
        compiler_params=_params("arbitrary", "arbitrary"),
        name="attn",
    )(rel_table.reshape(-1), q, k, v, _bucket_strip(s), lam_p, subln_g.reshape(1, hd2))


def _attn_out_body(o_ref, x_ref, mod_ref, wo_ref, lng_ref, lnb_ref, x1_ref, h2_ref, *, alpha):
    y = _dot(o_ref[...], wo_ref[...])
    _residual_epilogue(x_ref[...], y, mod_ref, lng_ref, lnb_ref, alpha, x1_ref, h2_ref)


def _attn_out(o, x, mods_l, w_o, ln_g, ln_b, alpha):
    bsz, s, d = x.shape
    ts = min(SEQ_TILE, s)
    tile = pl.BlockSpec((None, ts, d), lambda b, i: (b, i, 0))
    vec = pl.BlockSpec((1, d), lambda b, i: (0, 0))
    return pl.pallas_call(
        functools.partial(_attn_out_body, alpha=alpha),
        grid=(bsz, s // ts),
        in_specs=[tile, tile, pl.BlockSpec((None, 6, d), lambda b, i: (b, 0, 0)),
                  pl.BlockSpec((d, d), lambda b, i: (0, 0)), vec, vec],
        out_specs=[tile, tile],
        out_shape=[jax.ShapeDtypeStruct((bsz, s, d), F32)] * 2,
        compiler_params=_params("parallel", "parallel"),
        name="attn_out",
    )(o, x, mods_l, w_o.astype(BF16), ln_g.reshape(1, d), ln_b.reshape(1, d))


def _router_body(h_ref, wt_ref, b_ref, tri_ref, idx_ref, gate_ref, rank_ref, cnt_ref):
    logits = lax.dot_general(wt_ref[...], h_ref[...], (((1,), (1,)), ((), ())),
                             preferred_element_type=F32, precision=HIGHEST) + b_ref[...]
    n_exp, tr = logits.shape
    eio = lax.broadcasted_iota(I32, (n_exp, tr), 0)
    work = logits
    vals, idxs = [], []
    for _ in range(TOP_K):
        m = jnp.max(work, axis=0, keepdims=True)
        am = jnp.min(jnp.where(work == m, eio, n_exp), axis=0, keepdims=True)
        vals.append(m)
        idxs.append(am)
        work = jnp.where(eio == am, -jnp.inf, work)
    ex = [jnp.exp(v - vals[0]) for v in vals]
    den = ex[0] + ex[1] + ex[2] + ex[3]
    onehot = jnp.zeros((n_exp, tr), F32)
    for k in range(TOP_K):
        onehot = onehot + (eio == idxs[k]).astype(F32)
    before = _dot(onehot.astype(BF16), tri_ref[...])
    for k in range(TOP_K):
        idx_ref[k:k + 1, :] = idxs[k]
        gate_ref[k:k + 1, :] = ex[k] / den
        rank_ref[k:k + 1, :] = jnp.sum(jnp.where(eio == idxs[k], before, 0.0), axis=0, keepdims=True).astype(I32)
    cnt_ref[...] = jnp.sum(onehot, axis=1, keepdims=True).astype(I32)


def _router(h2, w_r, b_r):
    t, d = h2.shape
    n_exp = w_r.shape[1]
    tr = min(MOE_TOKENS, t)
    pos = jnp.arange(tr, dtype=I32)
    tri = (pos[:, None] < pos[None, :]).astype(BF16)
    tok = pl.BlockSpec((TOP_K, tr), lambda i: (0, i))
    return pl.pallas_call(
        _router_body,
        grid=(t // tr,),
        in_specs=[
            pl.BlockSpec((tr, d), lambda i: (i, 0)),
            pl.BlockSpec((n_exp, d), lambda i: (0, 0)),
            pl.BlockSpec((n_exp, 1), lambda i: (0, 0)),
            pl.BlockSpec((tr, tr), lambda i: (0, 0)),
        ],
        out_specs=[tok, tok, tok, pl.BlockSpec((None, n_exp, 1), lambda i: (i, 0, 0))],
        out_shape=[jax.ShapeDtypeStruct((TOP_K, t), I32), jax.ShapeDtypeStruct((TOP_K, t), F32),
                   jax.ShapeDtypeStruct((TOP_K, t), I32), jax.ShapeDtypeStruct((t // tr, n_exp, 1), I32)],
        compiler_params=_params("parallel"),
        name="router",
    )(h2, w_r.T, b_r.reshape(n_exp, 1), tri)


def _segment_copies(src, src_row, dst, dst_row, cnt, max_rows, sem, wait):
    bit = max_rows
    while bit >= SUBLANES:
        @pl.when((cnt & bit) != 0)
        def _(bit=bit):
            off = (cnt // (2 * bit)) * (2 * bit)
            dma = pltpu.make_async_copy(src.at[pl.ds(pl.multiple_of(src_row + off, SUBLANES), bit)],
                                        dst.at[pl.ds(pl.multiple_of(dst_row + off, SUBLANES), bit)], sem)
            dma.wait() if wait else dma.start()
        bit //= 2


def _zero_fill(zero_ref, xs_hbm, start, cnt, sem, wait):
    zr = zero_ref.shape[0]

    def full(r, c):
        dma = pltpu.make_async_copy(zero_ref, xs_hbm.at[pl.ds(pl.multiple_of(start + r * zr, SUBLANES), zr)], sem)
        dma.wait() if wait else dma.start()
        return c

    n_full = cnt // zr
    lax.fori_loop(0, n_full, full, 0)
    _segment_copies(zero_ref, 0, xs_hbm, start + n_full * zr, cnt - n_full * zr, zr // 2, sem, wait)


def _one_hot_hits(iota, pos_of):
    hit = iota == pos_of(0)
    for k in range(1, TOP_K):
        hit = hit | (iota == pos_of(k))
    return hit


def _dispatch_body(goff_ref, loff_ref, cp_ref, pad_start_ref, pad_cnt_ref, h_ref, lp_ref, gate_ref, xs_hbm,
                   xl_ref, zero_ref, sem, zsem, *, n_exp):
    j = pl.program_id(0)
    tl, d = h_ref.shape
    lr = xl_ref.shape[1]

    @pl.when(j == 0)
    def _():
        zero_ref[...] = jnp.zeros_like(zero_ref)
        for wait in (False, True):
            def per_range(e, carry, wait=wait):
                _zero_fill(zero_ref, xs_hbm, pad_start_ref[e], pad_cnt_ref[e], zsem, wait)
                return carry
            lax.fori_loop(0, pad_start_ref.shape[0], per_range, 0)

    def segments(tile, wait):
        buf = xl_ref.at[tile % 2]

        def per_expert(e, carry):
            seg = tile * n_exp + e
            _segment_copies(buf, loff_ref[seg], xs_hbm, goff_ref[seg], cp_ref[seg], tl, sem.at[tile % 2], wait)
            return carry
        lax.fori_loop(0, n_exp, per_expert, 0)

    @pl.when(j >= 2)
    def _():
        segments(j - 2, True)

    buf = xl_ref.at[j % 2]
    riota = lax.broadcasted_iota(I32, (lr, tl), 0)
    perm = _one_hot_hits(riota, lambda k: lp_ref[k:k + 1, :]).astype(F32).astype(BF16)
    buf[:, 0:d] = _dot(perm, h_ref[...].astype(BF16))
    gsel = jnp.zeros((lr, tl), F32)
    for k in range(TOP_K):
        gsel = gsel + jnp.where(riota == lp_ref[k:k + 1, :], gate_ref[k:k + 1, :], 0.0)
    buf[:, d:] = jnp.broadcast_to(jnp.sum(gsel, axis=1, keepdims=True), (lr, LANES))
    segments(j, False)

    @pl.when(j == pl.num_programs(0) - 1)
    def _():
        @pl.when(j >= 1)
        def _():
            segments(j - 1, True)
        segments(j, True)


def _dispatch(h2, lp, gates, goff, loff, cp, pad_start, pad_cnt, n_rows):
    t, d = h2.shape
    tl = min(MOE_TOKENS, t)
    n_exp = goff.shape[0] // (t // tl)
    lr = TOP_K * tl + n_exp * SUBLANES
    tok = pl.BlockSpec((TOP_K, tl), lambda j, *_: (0, j))
    grid_spec = pltpu.PrefetchScalarGridSpec(
        num_scalar_prefetch=5,
        grid=(t // tl,),
        in_specs=[pl.BlockSpec((tl, d), lambda j, *_: (j, 0)), tok, tok],
        out_specs=pl.BlockSpec(memory_space=pl.ANY),
        scratch_shapes=[pltpu.VMEM((2, lr, d + LANES), F32), pltpu.VMEM((MOE_TILE, d + LANES), F32),
                        pltpu.SemaphoreType.DMA((2,)), pltpu.SemaphoreType.DMA],
    )
    return pl.pallas_call(
        functools.partial(_dispatch_body, n_exp=n_exp),
        grid_spec=grid_spec,
        out_shape=jax.ShapeDtypeStruct((n_rows, d + LANES), F32),
        compiler_params=_params("arbitrary"),
        name="dispatch",
    )(goff, loff, cp, pad_start, pad_cnt, h2, lp, gates)


def _experts_body(be_ref, bsrc_ref, nvalid_ref, xs_ref, wgu_ref, bgu_ref, wdn_ref, bdn_ref, ys_ref,
                  wgu_bf, wdn_bf):
    i = pl.program_id(0)
    e = be_ref[i]
    prev = be_ref[jnp.maximum(i - 1, 0)]
    f = wdn_ref.shape[0]

    @pl.when((i == 0) | (e != prev))
    def _():
        def cast(ref_in, ref_out):
            def step(r, c):
                rows = pl.ds(pl.multiple_of(r * CAST_ROWS, CAST_ROWS), CAST_ROWS)
                ref_out[rows, :] = ref_in[rows, :].astype(BF16)
                return c
            lax.fori_loop(0, ref_in.shape[0] // CAST_ROWS, step, 0)
        cast(wgu_ref, wgu_bf)
        cast(wdn_ref, wdn_bf)

    @pl.when(i < nvalid_ref[0])
    def _():
        d = wgu_ref.shape[0]
        x = xs_ref[:, 0:d].astype(BF16)
        gate = jnp.minimum(_dot(x, wgu_bf[:, :f]) + bgu_ref[:, :f], SWIGLU_LIMIT)
        lin = jnp.clip(_dot(x, wgu_bf[:, f:]) + bgu_ref[:, f:], -SWIGLU_LIMIT, SWIGLU_LIMIT)
        act = (gate * jax.nn.sigmoid(SWIGLU_ALPHA * gate) * (lin + 1.0)).astype(BF16)
        ys_ref[...] = (_dot(act, wdn_bf[...]) + bdn_ref[...]) * xs_ref[:, d:d + 1]

    @pl.when(i >= nvalid_ref[0])
    def _():
        ys_ref[...] = jnp.zeros_like(ys_ref)


def _experts(xs, block_e, block_src, n_valid, w_gu, b_gu, w_dn, b_dn, layer):
    n_rows, dx = xs.shape
    depth, n_exp, d, f2 = w_gu.shape
    f = f2 // 2
    tm = MOE_TILE
    grid_spec = pltpu.PrefetchScalarGridSpec(
        num_scalar_prefetch=3,
        grid=(n_rows // tm,),
        in_specs=[
            pl.BlockSpec((tm, dx), lambda i, be, bs, nv: (bs[i], 0)),
            pl.BlockSpec((None, None, d, f2), lambda i, be, bs, nv: (layer, be[i], 0, 0)),
            pl.BlockSpec((None, None, 1, f2), lambda i, be, bs, nv: (layer, be[i], 0, 0)),
            pl.BlockSpec((None, None, f, d), lambda i, be, bs, nv: (layer, be[i], 0, 0)),
            pl.BlockSpec((None, None, 1, d), lambda i, be, bs, nv: (layer, be[i], 0, 0)),
        ],
        out_specs=pl.BlockSpec((tm, d), lambda i, be, bs, nv: (i, 0)),
        scratch_shapes=[pltpu.VMEM((d, f2), BF16), pltpu.VMEM((f, d), BF16)],
    )
    return pl.pallas_call(
        _experts_body,
        grid_spec=grid_spec,
        out_shape=jax.ShapeDtypeStruct((n_rows, d), F32),
        compiler_params=_params("arbitrary"),
        name="experts",
    )(block_e, block_src, n_valid, xs, w_gu, b_gu.reshape(depth, n_exp, 1, f2), w_dn,
      b_dn.reshape(depth, n_exp, 1, d))


def _combine_body(goff_ref, loff_ref, cp_ref, ltot_ref, ys_hbm, x_ref, lp_ref, mod_ref, lng_ref, lnb_ref, o_ref,
                  yl_ref, sem, *, n_exp, alpha):
    j = pl.program_id(0)
    tl = x_ref.shape[0]
    lr, d = yl_ref.shape[1:]

    def segments(tile, wait):
        buf = yl_ref.at[tile % 2]

        def per_expert(e, carry):
            seg = tile * n_exp + e
            _segment_copies(ys_hbm, goff_ref[seg], buf, loff_ref[seg], cp_ref[seg], tl, sem.at[tile % 2], wait)
            return carry
        lax.fori_loop(0, n_exp, per_expert, 0)

    def fetch(tile):
        segments(tile, False)
        buf = yl_ref.at[tile % 2]

        def zero_rows(r, carry):
            buf[pl.ds(pl.multiple_of(r * SUBLANES, SUBLANES), SUBLANES), :] = jnp.zeros((SUBLANES, d), F32)
            return carry
        lax.fori_loop(ltot_ref[tile] // SUBLANES, lr // SUBLANES, zero_rows, 0)

    @pl.when(j == 0)
    def _():
        fetch(j)

    @pl.when(j + 1 < pl.num_programs(0))
    def _():
        fetch(j + 1)

    liota = lax.broadcasted_iota(I32, (tl, lr), 1)
    pick = _one_hot_hits(liota, lambda k: lp_ref[:, k:k + 1]).astype(F32).astype(BF16)
    segments(j, True)
    y = _dot(pick, yl_ref[j % 2].astype(BF16))
    o_ref[...] = _layer_norm(alpha * x_ref[...] + mod_ref[5:6, :] * y, lng_ref[...], lnb_ref[...])


def _combine(ys, lp_t, goff, loff, cp, ltot, x1, mods_l, ln_g, ln_b, alpha, seq_len):
    t, d = x1.shape
    tl = min(MOE_TOKENS, t)
    n_exp = goff.shape[0] // (t // tl)
    lr = TOP_K * tl + n_exp * SUBLANES
    per_seq = seq_len // tl
    vec = pl.BlockSpec((1, d), lambda j, *_: (0, 0))
    grid_spec = pltpu.PrefetchScalarGridSpec(
        num_scalar_prefetch=4,
        grid=(t // tl,),
        in_specs=[
            pl.BlockSpec(memory_space=pl.ANY),
            pl.BlockSpec((tl, d), lambda j, *_: (j, 0)),
            pl.BlockSpec((tl, TOP_K), lambda j, *_: (j, 0)),
            pl.BlockSpec((None, 6, d), lambda j, *_: (j // per_seq, 0, 0)),
            vec, vec,
        ],
        out_specs=pl.BlockSpec((tl, d), lambda j, *_: (j, 0)),
        scratch_shapes=[pltpu.VMEM((2, lr, d), F32), pltpu.SemaphoreType.DMA((2,))],
    )
    return pl.pallas_call(
        functools.partial(_combine_body, n_exp=n_exp, alpha=alpha),
        grid_spec=grid_spec,
        out_shape=jax.ShapeDtypeStruct((t, d), F32),
        compiler_params=_params("arbitrary"),
        name="combine",
    )(goff, loff, cp, ltot, ys, x1, lp_t, mods_l, ln_g.reshape(1, d), ln_b.reshape(1, d))


def _moe_layer(x1, h2, mods_l, w_r, b_r, w_gu, b_gu, w_dn, b_dn, layer, ln_g, ln_b, alpha, seq_len):
    t, d = x1.shape
    n_exp = w_r.shape[1]
    tm = MOE_TILE
    tl = min(MOE_TOKENS, t)
    n_tiles = t // tl
    idx, gates, rank, cnt = _router(h2, w_r, b_r)
    cp = (cnt[:, :, 0] + SUBLANES - 1) // SUBLANES * SUBLANES
    tot = jnp.sum(cp, axis=0)
    padded = (tot + tm - 1) // tm * tm
    pend = jnp.cumsum(padded)
    pstart = pend - padded
    goff = pstart[None, :] + jnp.cumsum(cp, axis=0) - cp
    loff = jnp.cumsum(cp, axis=1) - cp
    ltot = jnp.sum(cp, axis=1)
    is_e = idx[..., None] == jnp.arange(n_exp, dtype=I32)
    lp = jnp.sum(jnp.where(is_e, jnp.repeat(loff, tl, axis=0)[None], 0), axis=-1) + rank
    n_blocks = -(-(t * TOP_K + n_tiles * n_exp * (SUBLANES - 1)) // tm) + n_exp
    n_valid = pend[-1] // tm
    blk = jnp.arange(n_blocks, dtype=I32)
    block_src = jnp.minimum(blk, n_valid - 1)
    block_e = jnp.sum(pend[None, :] <= (block_src * tm)[:, None], axis=1).astype(I32)
    n_rows = n_blocks * tm
    pad_start = jnp.concatenate([pstart + tot, pend[-1:]]).astype(I32)
    pad_cnt = jnp.concatenate([padded - tot, n_rows - pend[-1:]]).astype(I32)
    flat = lambda a: a.reshape(-1).astype(I32)
    xs = _dispatch(h2, lp, gates, flat(goff), flat(loff), flat(cp), pad_start, pad_cnt, n_rows)
    ys = _experts(xs, block_e, block_src.astype(I32), n_valid.reshape(1).astype(I32), w_gu, b_gu, w_dn, b_dn, layer)
    return _combine(ys, lp.T, flat(goff), flat(loff), flat(cp), ltot.astype(I32), x1, mods_l, ln_g, ln_b, alpha,
                    seq_len)


def kernel(x, c, ada_w, ada_b, post_ln_g, post_ln_b, conv_w_pw1, conv_b_pw1, conv_w_dw, conv_b_dw, conv_ln_g, conv_ln_b, conv_w_pw2, conv_b_pw2, w_kv, attn_w_q, attn_lambda, attn_subln_g, attn_w_o, rel_bias_table, router_w, router_b, expert_w_gate_up, expert_b_gate_up, expert_w_down, expert_b_down):
    bsz, s, d = x.shape
    depth = ada_w.shape[0]
    n_a = depth // 2
    alpha = (2 * depth) ** 0.25
    mods = _ada(c, ada_w, ada_b).reshape(depth, bsz, 6, d)
    q = k = v = None
    for l in range(depth):
        mods_l = mods[l]
        if l < n_a:
            u = _conv_a(x, mods_l, conv_w_pw1[l], conv_b_pw1[l])
            x1, h2 = _conv_b(u, x, mods_l, conv_w_dw[l], conv_b_dw[l], conv_ln_g[l], conv_ln_b[l],
                             conv_w_pw2[l], conv_b_pw2[l], post_ln_g[l, 0], post_ln_b[l, 0], alpha)
        else:
            j = l - n_a
            if j == 0:
                q, k, v = _qkv(x, mods_l, w_kv, attn_w_q[j])
            else:
                q = _qkv(x, mods_l, w_kv, attn_w_q[j])[0]
            lambda_init = 0.8 - 0.6 * math.exp(-0.3 * l)
            o = _attn(q, k, v, attn_lambda[j], attn_subln_g[j], rel_bias_table, lambda_init)
            x1, h2 = _attn_out(o, x, mods_l, attn_w_o[j], post_ln_g[l, 0], post_ln_b[l, 0], alpha)
        x = _moe_layer(x1.reshape(bsz * s, d), h2.reshape(bsz * s, d), mods_l, router_w[l], router_b[l],
                       expert_w_gate_up, expert_b_gate_up, expert_w_down, expert_b_down, l,
                       post_ln_g[l, 1], post_ln_b[l, 1], alpha, s).reshape(bsz, s, d)
    return x
```

```python
import functools
import math

import jax
import jax.numpy as jnp
from jax import lax
from jax.experimental import pallas as pl
from jax.experimental.pallas import tpu as pltpu

F32 = jnp.float32
BF16 = jnp.bfloat16
I32 = jnp.int32
HIGHEST = lax.Precision.HIGHEST

CHUNK = 64
CONV_WIDTH = 31
HEAD_DIM = 64
REL_BUCKETS = 32
REL_MAX_DIST = 128
TOP_K = 4
SWIGLU_LIMIT = 7.0
SWIGLU_ALPHA = 1.702
LN_EPS = 1e-5
MASK_VALUE = -1e30

SUBLANES = 8
LANES = 128
VMEM_LIMIT_BYTES = 56 * 1024 * 1024

ADA_TN = 1024
SEQ_TILE = 256
CONV_HALO = 32
CONV_ROWS = 64
CONV_COLS = 256
MOE_TOKENS = 256
MOE_TILE = 512
Q_BLOCK = 128
CAST_ROWS = 128


def _params(*sem):
    return pltpu.CompilerParams(dimension_semantics=sem, vmem_limit_bytes=VMEM_LIMIT_BYTES)


def _layer_norm(x, g, b):
    mu = jnp.mean(x, axis=-1, keepdims=True)
    xc = x - mu
    var = jnp.mean(xc * xc, axis=-1, keepdims=True)
    return xc * lax.rsqrt(var + LN_EPS) * g + b


def _dot(a, b):
    return jnp.dot(a, b, preferred_element_type=F32)


def _ada_body(c_ref, w_ref, b_ref, o_ref):
    c = c_ref[...]
    cond = c * jax.nn.sigmoid(c)
    o_ref[...] = jnp.dot(cond, w_ref[...], preferred_element_type=F32, precision=HIGHEST) + b_ref[...]


def _ada(c, ada_w, ada_b):
    depth, d, n = ada_w.shape
    bsz = c.shape[0]
    tn = min(ADA_TN, n)
    return pl.pallas_call(
        _ada_body,
        grid=(depth, n // tn),
        in_specs=[
            pl.BlockSpec((bsz, d), lambda l, j: (0, 0)),
            pl.BlockSpec((None, d, tn), lambda l, j: (l, 0, j)),
            pl.BlockSpec((None, 1, tn), lambda l, j: (l, 0, j)),
        ],
        out_specs=pl.BlockSpec((None, bsz, tn), lambda l, j: (l, 0, j)),
        out_shape=jax.ShapeDtypeStruct((depth, bsz, n), F32),
        compiler_params=_params("parallel", "parallel"),
        name="ada",
    )(c, ada_w, ada_b.reshape(depth, 1, n))


def _conv_a_body(x_ref, mod_ref, w_ref, b_ref, u_ref):
    d = x_ref.shape[-1]
    h = (x_ref[...] * (1.0 + mod_ref[1:2, :]) + mod_ref[0:1, :]).astype(BF16)
    a = _dot(h, w_ref[:, :d]) + b_ref[:, :d]
    g = _dot(h, w_ref[:, d:]) + b_ref[:, d:]
    u_ref[...] = a * jax.nn.sigmoid(g)


def _conv_a(x, mods_l, w_pw1, b_pw1):
    bsz, s, d = x.shape
    ts = min(SEQ_TILE, s)
    return pl.pallas_call(
        _conv_a_body,
        grid=(bsz, s // ts),
        in_specs=[
            pl.BlockSpec((None, ts, d), lambda b, i: (b, i, 0)),
            pl.BlockSpec((None, 6, d), lambda b, i: (b, 0, 0)),
            pl.BlockSpec((d, 2 * d), lambda b, i: (0, 0)),
            pl.BlockSpec((1, 2 * d), lambda b, i: (0, 0)),
        ],
        out_specs=pl.BlockSpec((None, ts, d), lambda b, i: (b, i, 0)),
        out_shape=jax.ShapeDtypeStruct((bsz, s, d), F32),
        compiler_params=_params("parallel", "parallel"),
        name="conv_a",
    )(x, mods_l, w_pw1.astype(BF16), b_pw1.reshape(1, 2 * d))


def _residual_epilogue(x, y, mod_ref, lng_ref, lnb_ref, alpha, x1_ref, h2_ref):
    x1 = _layer_norm(alpha * x + mod_ref[2:3, :] * y, lng_ref[...], lnb_ref[...])
    x1_ref[...] = x1
    h2_ref[...] = x1 * (1.0 + mod_ref[4:5, :]) + mod_ref[3:4, :]


def _conv_b_body(u_ref, halo_ref, x_ref, mod_ref, wdw_ref, bdw_ref, cg_ref, cb_ref, w2_ref, b2_ref,
                 lng_ref, lnb_ref, x1_ref, h2_ref, win_ref, v_ref, *, alpha):
    ts, d = u_ref.shape
    i = pl.program_id(1)
    halo = halo_ref[...]
    win_ref[0, 0:CONV_HALO, :] = jnp.where(i == 0, jnp.zeros_like(halo), halo)
    win_ref[0, CONV_HALO:, :] = u_ref[...]
    n_shift = ts + CONV_HALO - SUBLANES
    for b in range(1, SUBLANES):
        win_ref[b, 0:n_shift, :] = win_ref[0, b:b + n_shift, :]
    off = CONV_HALO - (CONV_WIDTH - 1)
    rows = min(CONV_ROWS, ts)
    cols = min(CONV_COLS, d)
    for c0 in range(0, d, cols):
        for r0 in range(0, ts, rows):
            acc = jnp.zeros((rows, cols), F32)
            for j in range(CONV_WIDTH):
                a, b = divmod(off + j, SUBLANES)
                r = r0 + a * SUBLANES
                acc = acc + wdw_ref[j:j + 1, c0:c0 + cols] * win_ref[b, r:r + rows, c0:c0 + cols]
            v_ref[r0:r0 + rows, c0:c0 + cols] = acc
    v = _layer_norm(v_ref[...] + bdw_ref[...], cg_ref[...], cb_ref[...])
    v = (v * jax.nn.sigmoid(v)).astype(BF16)
    y = _dot(v, w2_ref[...]) + b2_ref[...]
    _residual_epilogue(x_ref[...], y, mod_ref, lng_ref, lnb_ref, alpha, x1_ref, h2_ref)


def _conv_b(u, x, mods_l, w_dw, b_dw, cln_g, cln_b, w_pw2, b_pw2, ln_g, ln_b, alpha):
    bsz, s, d = x.shape
    ts = min(SEQ_TILE, s)
    hb = ts // CONV_HALO
    row = lambda a: a.reshape(1, d)
    tile = pl.BlockSpec((None, ts, d), lambda b, i: (b, i, 0))
    vec = pl.BlockSpec((1, d), lambda b, i: (0, 0))
    return pl.pallas_call(
        functools.partial(_conv_b_body, alpha=alpha),
        grid=(bsz, s // ts),
        in_specs=[
            tile,
            pl.BlockSpec((None, CONV_HALO, d), lambda b, i: (b, jnp.maximum(i * hb - 1, 0), 0)),
            tile,
            pl.BlockSpec((None, 6, d), lambda b, i: (b, 0, 0)),
            pl.BlockSpec((CONV_WIDTH, d), lambda b, i: (0, 0)),
            vec, vec, vec,
            pl.BlockSpec((d, d), lambda b, i: (0, 0)),
            vec, vec, vec,
        ],
        out_specs=[tile, tile],
        out_shape=[jax.ShapeDtypeStruct((bsz, s, d), F32)] * 2,
        scratch_shapes=[pltpu.VMEM((SUBLANES, ts + CONV_HALO, d), F32), pltpu.VMEM((ts, d), F32)],
        compiler_params=_params("parallel", "parallel"),
        name="conv_b",
    )(u, u, x, mods_l, w_dw, row(b_dw), row(cln_g), row(cln_b), w_pw2.astype(BF16), row(b_pw2),
      row(ln_g), row(ln_b))


def _qkv_body(x_ref, mod_ref, wkv_ref, wq_ref, q_ref, k_ref, v_ref):
    d = x_ref.shape[-1]
    x = x_ref[...]
    kv = _dot(x.astype(BF16), wkv_ref[...])
    k_ref[...] = kv[:, :d].astype(BF16)
    v_ref[...] = kv[:, d:].astype(BF16)
    h = (x * (1.0 + mod_ref[1:2, :]) + mod_ref[0:1, :]).astype(BF16)
    q_ref[...] = (_dot(h, wq_ref[...]) * (HEAD_DIM ** -0.5)).astype(BF16)


def _qkv(x, mods_l, w_kv, w_q):
    bsz, s, d = x.shape
    ts = min(SEQ_TILE, s)
    tile = pl.BlockSpec((None, ts, d), lambda b, i: (b, i, 0))
    return pl.pallas_call(
        _qkv_body,
        grid=(bsz, s // ts),
        in_specs=[
            tile,
            pl.BlockSpec((None, 6, d), lambda b, i: (b, 0, 0)),
            pl.BlockSpec((d, 2 * d), lambda b, i: (0, 0)),
            pl.BlockSpec((d, d), lambda b, i: (0, 0)),
        ],
        out_specs=[tile, tile, tile],
        out_shape=[jax.ShapeDtypeStruct((bsz, s, d), BF16)] * 3,
        compiler_params=_params("parallel", "parallel"),
        name="qkv",
    )(x, mods_l, w_kv.astype(BF16), w_q.astype(BF16))


def _t5_bucket(rel):
    nb = REL_BUCKETS // 2
    ret = jnp.where(rel > 0, nb, 0)
    n = jnp.abs(rel)
    max_exact = nb // 2
    large = max_exact + (jnp.log(jnp.maximum(n, 1).astype(F32) / max_exact)
                         / math.log(REL_MAX_DIST / max_exact) * (nb - max_exact)).astype(I32)
    large = jnp.minimum(large, nb - 1)
    return ret + jnp.where(n < max_exact, n, large)


def _bucket_strip(s):
    r = jnp.arange(Q_BLOCK, dtype=I32)[:, None]
    kp = jnp.arange(s, dtype=I32)[None, :] - (s - Q_BLOCK)
    bucket = _t5_bucket(kp - r)
    visible = jnp.floor_divide(kp, CHUNK) <= (r // CHUNK)
    return jnp.where(visible, bucket, REL_BUCKETS)


def _attn_body(tab_ref, q_ref, k_ref, v_ref, bkt_ref, lam_ref, sg_ref, o_ref, bias_ref, *, lambda_init, n_heads):
    h = pl.program_id(0)
    b = pl.program_id(1)
    s = q_ref.shape[0]

    @pl.when(b == 0)
    def _():
        bk = bkt_ref[...]
        acc = jnp.full(bk.shape, MASK_VALUE, F32)
        for r in range(REL_BUCKETS):
            acc = jnp.where(bk == r, tab_ref[r * n_heads + h], acc)
        bias_ref[0:Q_BLOCK, :] = acc
        bias_ref[Q_BLOCK:, :] = acc

    lp = lam_ref[...]
    lam = (jnp.exp(jnp.sum(lp[0:1, :] * lp[1:2, :], axis=-1, keepdims=True))
           - jnp.exp(jnp.sum(lp[2:3, :] * lp[3:4, :], axis=-1, keepdims=True)) + lambda_init)
    lane = lax.broadcasted_iota(I32, (Q_BLOCK, 2 * HEAD_DIM), 1)
    nt = (((1,), (1,)), ((), ()))
    for i in range(s // Q_BLOCK):
        n_keys = (i + 1) * Q_BLOCK
        q = q_ref[i * Q_BLOCK:(i + 1) * Q_BLOCK, :]
        qq = jnp.concatenate([jnp.where(lane < HEAD_DIM, q, jnp.zeros_like(q)),
                              jnp.where(lane >= HEAD_DIM, q, jnp.zeros_like(q))], axis=0)
        sc = lax.dot_general(qq, k_ref[0:n_keys, :], nt, preferred_element_type=F32) + bias_ref[:, s - n_keys:s]
        p = jnp.exp(sc - jnp.max(sc, axis=-1, keepdims=True))
        denom = jnp.sum(p, axis=-1, keepdims=True)
        pv = _dot(p.astype(BF16), v_ref[0:n_keys, :])
        o = pv[:Q_BLOCK] * (1.0 / denom[:Q_BLOCK]) - pv[Q_BLOCK:] * (lam / denom[Q_BLOCK:])
        o = o * lax.rsqrt(jnp.mean(o * o, axis=-1, keepdims=True) + LN_EPS) * sg_ref[...]
        o_ref[i * Q_BLOCK:(i + 1) * Q_BLOCK, :] = (o * (1.0 - lambda_init)).astype(BF16)


def _attn(q, k, v, lam_p, subln_g, rel_table, lambda_init):
    bsz, s, d = q.shape
    hd2 = 2 * HEAD_DIM
    n_heads = d // hd2
    head = pl.BlockSpec((None, s, hd2), lambda h, b, tab: (b, 0, h))
    grid_spec = pltpu.PrefetchScalarGridSpec(
        num_scalar_prefetch=1,
        grid=(n_heads, bsz),
        in_specs=[
            head, head, head,
            pl.BlockSpec((Q_BLOCK, s), lambda h, b, tab: (0, 0)),
            pl.BlockSpec((4, HEAD_DIM), lambda h, b, tab: (0, 0)),
            pl.BlockSpec((1, hd2), lambda h, b, tab: (0, 0)),
        ],
        out_specs=head,
        scratch_shapes=[pltpu.VMEM((2 * Q_BLOCK, s), F32)],
    )
    return pl.pallas_call(
        functools.partial(_attn_body, lambda_init=lambda_init, n_heads=n_heads),
        grid_spec=grid_spec,
        out_shape=jax.ShapeDtypeStruct((bsz, s, d), BF16),
        compiler_params=_params("arbitrary", "arbitrary"),
        name="attn",
    )(rel_table.reshape(-1), q, k, v, _bucket_strip(s), lam_p, subln_g.reshape(1, hd2))


def _attn_out_body(o_ref, x_ref, mod_ref, wo_ref, lng_ref, lnb_ref, x1_ref, h2_ref, *, alpha):
    y = _dot(o_ref[...], wo_ref[...])
    _residual_epilogue(x_ref[...], y, mod_ref, lng_ref, lnb_ref, alpha, x1_ref, h2_ref)


def _attn_out(o, x, mods_l, w_o, ln_g, ln_b, alpha):
    bsz, s, d = x.shape
    ts = min(SEQ_TILE, s)
    tile = pl.BlockSpec((None, ts, d), lambda b, i: (b, i, 0))
    vec = pl.BlockSpec((1, d), lambda b, i: (0, 0))
    return pl.pallas_call(
        functools.partial(_attn_out_body, alpha=alpha),
        grid=(bsz, s // ts),
        in_specs=[tile, tile, pl.BlockSpec((None, 6, d), lambda b, i: (b, 0, 0)),
                  pl.BlockSpec((d, d), lambda b, i: (0, 0)), vec, vec],
        out_specs=[tile, tile],
        out_shape=[jax.ShapeDtypeStruct((bsz, s, d), F32)] * 2,
        compiler_params=_params("parallel", "parallel"),
        name="attn_out",
    )(o, x, mods_l, w_o.astype(BF16), ln_g.reshape(1, d), ln_b.reshape(1, d))


def _router_body(h_ref, wt_ref, b_ref, tri_ref, idx_ref, gate_ref, rank_ref, cnt_ref):
    logits = lax.dot_general(wt_ref[...], h_ref[...], (((1,), (1,)), ((), ())),
                             preferred_element_type=F32, precision=HIGHEST) + b_ref[...]
    n_exp, tr = logits.shape
    eio = lax.broadcasted_iota(I32, (n_exp, tr), 0)
    work = logits
    vals, idxs = [], []
    for _ in range(TOP_K):
        m = jnp.max(work, axis=0, keepdims=True)
        am = jnp.min(jnp.where(work == m, eio, n_exp), axis=0, keepdims=True)
        vals.append(m)
        idxs.append(am)
        work = jnp.where(eio == am, -jnp.inf, work)
    ex = [jnp.exp(v - vals[0]) for v in vals]
    den = ex[0] + ex[1] + ex[2] + ex[3]
    onehot = jnp.zeros((n_exp, tr), F32)
    for k in range(TOP_K):
        onehot = onehot + (eio == idxs[k]).astype(F32)
    before = _dot(onehot.astype(BF16), tri_ref[...])
    for k in range(TOP_K):
        idx_ref[k:k + 1, :] = idxs[k]
        gate_ref[k:k + 1, :] = ex[k] / den
        rank_ref[k:k + 1, :] = jnp.sum(jnp.where(eio == idxs[k], before, 0.0), axis=0, keepdims=True).astype(I32)
    cnt_ref[...] = jnp.sum(onehot, axis=1, keepdims=True).astype(I32)


def _router(h2, w_r, b_r):
    t, d = h2.shape
    n_exp = w_r.shape[1]
    tr = min(MOE_TOKENS, t)
    pos = jnp.arange(tr, dtype=I32)
    tri = (pos[:, None] < pos[None, :]).astype(BF16)
    tok = pl.BlockSpec((TOP_K, tr), lambda i: (0, i))
    return pl.pallas_call(
        _router_body,
        grid=(t // tr,),
        in_specs=[
            pl.BlockSpec((tr, d), lambda i: (i, 0)),
            pl.BlockSpec((n_exp, d), lambda i: (0, 0)),
            pl.BlockSpec((n_exp, 1), lambda i: (0, 0)),
            pl.BlockSpec((tr, tr), lambda i: (0, 0)),
        ],
        out_specs=[tok, tok, tok, pl.BlockSpec((None, n_exp, 1), lambda i: (i, 0, 0))],
        out_shape=[jax.ShapeDtypeStruct((TOP_K, t), I32), jax.ShapeDtypeStruct((TOP_K, t), F32),
                   jax.ShapeDtypeStruct((TOP_K, t), I32), jax.ShapeDtypeStruct((t // tr, n_exp, 1), I32)],
        compiler_params=_params("parallel"),
        name="router",
    )(h2, w_r.T, b_r.reshape(n_exp, 1), tri)


def _rows(ref, start, cnt):
    aligned = lambda v: v if isinstance(v, int) else pl.multiple_of(v, SUBLANES)
    return ref.at[pl.ds(aligned(start), aligned(cnt))]


def _segment_copy(src, src_row, dst, dst_row, cnt, sem):
    @pl.when(cnt > 0)
    def _():
        pltpu.make_async_copy(_rows(src, src_row, cnt), _rows(dst, dst_row, cnt), sem).start()


def _wait_rows(ref, cnt, sem):
    @pl.when(cnt > 0)
    def _():
        pltpu.make_async_copy(_rows(ref, 0, cnt), _rows(ref, 0, cnt), sem).wait()


def _zero_fill(zero_ref, xs_hbm, start, cnt, sem):
    zr = zero_ref.shape[0]
    n_full = cnt // zr

    def full(r, c):
        _segment_copy(zero_ref, 0, xs_hbm, start + r * zr, zr, sem)
        return c

    lax.fori_loop(0, n_full, full, 0)
    _segment_copy(zero_ref, 0, xs_hbm, start + n_full * zr, cnt - n_full * zr, sem)
    _wait_rows(xs_hbm, cnt, sem)


def _one_hot_hits(iota, pos_of):
    hit = iota == pos_of(0)
    for k in range(1, TOP_K):
        hit = hit | (iota == pos_of(k))
    return hit


def _dispatch_body(goff_ref, loff_ref, cp_ref, ltot_ref, pad_start_ref, pad_cnt_ref, h_ref, lp_ref, gate_ref, xs_hbm,
                   xl_ref, zero_ref, sem, zsem, *, n_exp):
    j = pl.program_id(0)
    tl, d = h_ref.shape
    lr = xl_ref.shape[1]

    @pl.when(j == 0)
    def _():
        zero_ref[...] = jnp.zeros_like(zero_ref)

        def per_range(e, carry):
            _zero_fill(zero_ref, xs_hbm, pad_start_ref[e], pad_cnt_ref[e], zsem)
            return carry
        lax.fori_loop(0, pad_start_ref.shape[0], per_range, 0)

    def start_segments(tile):
        def per_expert(e, carry):
            seg = tile * n_exp + e
            _segment_copy(xl_ref.at[tile % 2], loff_ref[seg], xs_hbm, goff_ref[seg], cp_ref[seg], sem.at[tile % 2])
            return carry
        lax.fori_loop(0, n_exp, per_expert, 0)

    def wait_segments(tile):
        _wait_rows(xs_hbm, ltot_ref[tile], sem.at[tile % 2])

    @pl.when(j >= 2)
    def _():
        wait_segments(j - 2)

    buf = xl_ref.at[j % 2]
    riota = lax.broadcasted_iota(I32, (lr, tl), 0)
    perm = _one_hot_hits(riota, lambda k: lp_ref[k:k + 1, :]).astype(F32).astype(BF16)
    buf[:, 0:d] = _dot(perm, h_ref[...].astype(BF16))
    gsel = jnp.zeros((lr, tl), F32)
    for k in range(TOP_K):
        gsel = gsel + jnp.where(riota == lp_ref[k:k + 1, :], gate_ref[k:k + 1, :], 0.0)
    buf[:, d:] = jnp.broadcast_to(jnp.sum(gsel, axis=1, keepdims=True), (lr, LANES))
    start_segments(j)

    @pl.when(j == pl.num_programs(0) - 1)
    def _():
        @pl.when(j >= 1)
        def _():
            wait_segments(j - 1)
        wait_segments(j)


def _dispatch(h2, lp, gates, goff, loff, cp, ltot, pad_start, pad_cnt, n_rows):
    t, d = h2.shape
    tl = min(MOE_TOKENS, t)
    n_exp = goff.shape[0] // (t // tl)
    lr = TOP_K * tl + n_exp * SUBLANES
    tok = pl.BlockSpec((TOP_K, tl), lambda j, *_: (0, j))
    grid_spec = pltpu.PrefetchScalarGridSpec(
        num_scalar_prefetch=6,
        grid=(t // tl,),
        in_specs=[pl.BlockSpec((tl, d), lambda j, *_: (j, 0)), tok, tok],
        out_specs=pl.BlockSpec(memory_space=pl.ANY),
        scratch_shapes=[pltpu.VMEM((2, lr, d + LANES), F32), pltpu.VMEM((MOE_TILE, d + LANES), F32),
                        pltpu.SemaphoreType.DMA((2,)), pltpu.SemaphoreType.DMA],
    )
    return pl.pallas_call(
        functools.partial(_dispatch_body, n_exp=n_exp),
        grid_spec=grid_spec,
        out_shape=jax.ShapeDtypeStruct((n_rows, d + LANES), F32),
        compiler_params=_params("arbitrary"),
        name="dispatch",
    )(goff, loff, cp, ltot, pad_start, pad_cnt, h2, lp, gates)


def _experts_body(be_ref, bsrc_ref, nvalid_ref, xs_ref, wgu_ref, bgu_ref, wdn_ref, bdn_ref, ys_ref,
                  wgu_bf, wdn_bf):
    i = pl.program_id(0)
    e = be_ref[i]
    prev = be_ref[jnp.maximum(i - 1, 0)]
    f = wdn_ref.shape[0]

    @pl.when((i == 0) | (e != prev))
    def _():
        def cast(ref_in, ref_out):
            def step(r, c):
                rows = pl.ds(pl.multiple_of(r * CAST_ROWS, CAST_ROWS), CAST_ROWS)
                ref_out[rows, :] = ref_in[rows, :].astype(BF16)
                return c
            lax.fori_loop(0, ref_in.shape[0] // CAST_ROWS, step, 0)
        cast(wgu_ref, wgu_bf)
        cast(wdn_ref, wdn_bf)

    @pl.when(i < nvalid_ref[0])
    def _():
        d = wgu_ref.shape[0]
        x = xs_ref[:, 0:d].astype(BF16)
        gate = jnp.minimum(_dot(x, wgu_bf[:, :f]) + bgu_ref[:, :f], SWIGLU_LIMIT)
        lin = jnp.clip(_dot(x, wgu_bf[:, f:]) + bgu_ref[:, f:], -SWIGLU_LIMIT, SWIGLU_LIMIT)
        act = (gate * jax.nn.sigmoid(SWIGLU_ALPHA * gate) * (lin + 1.0)).astype(BF16)
        ys_ref[...] = (_dot(act, wdn_bf[...]) + bdn_ref[...]) * xs_ref[:, d:d + 1]

    @pl.when(i >= nvalid_ref[0])
    def _():
        ys_ref[...] = jnp.zeros_like(ys_ref)


def _experts(xs, block_e, block_src, n_valid, w_gu, b_gu, w_dn, b_dn, layer):
    n_rows, dx = xs.shape
    depth, n_exp, d, f2 = w_gu.shape
    f = f2 // 2
    tm = MOE_TILE
    grid_spec = pltpu.PrefetchScalarGridSpec(
        num_scalar_prefetch=3,
        grid=(n_rows // tm,),
        in_specs=[
            pl.BlockSpec((tm, dx), lambda i, be, bs, nv: (bs[i], 0)),
            pl.BlockSpec((None, None, d, f2), lambda i, be, bs, nv: (layer, be[i], 0, 0)),
            pl.BlockSpec((None, None, 1, f2), lambda i, be, bs, nv: (layer, be[i], 0, 0)),
            pl.BlockSpec((None, None, f, d), lambda i, be, bs, nv: (layer, be[i], 0, 0)),
            pl.BlockSpec((None, None, 1, d), lambda i, be, bs, nv: (layer, be[i], 0, 0)),
        ],
        out_specs=pl.BlockSpec((tm, d), lambda i, be, bs, nv: (i, 0)),
        scratch_shapes=[pltpu.VMEM((d, f2), BF16), pltpu.VMEM((f, d), BF16)],
    )
    return pl.pallas_call(
        _experts_body,
        grid_spec=grid_spec,
        out_shape=jax.ShapeDtypeStruct((n_rows, d), F32),
        compiler_params=_params("arbitrary"),
        name="experts",
    )(block_e, block_src, n_valid, xs, w_gu, b_gu.reshape(depth, n_exp, 1, f2), w_dn,
      b_dn.reshape(depth, n_exp, 1, d))


def _combine_body(goff_ref, loff_ref, cp_ref, ltot_ref, ys_hbm, x_ref, lp_ref, mod_ref, lng_ref, lnb_ref, o_ref,
                  yl_ref, sem, *, n_exp, alpha):
    j = pl.program_id(0)
    tl = x_ref.shape[0]
    lr, d = yl_ref.shape[1:]

    def fetch(tile):
        buf = yl_ref.at[tile % 2]

        def per_expert(e, carry):
            seg = tile * n_exp + e
            _segment_copy(ys_hbm, goff_ref[seg], buf, loff_ref[seg], cp_ref[seg], sem.at[tile % 2])
            return carry
        lax.fori_loop(0, n_exp, per_expert, 0)

        def zero_rows(r, carry):
            buf[pl.ds(pl.multiple_of(r * SUBLANES, SUBLANES), SUBLANES), :] = jnp.zeros((SUBLANES, d), F32)
            return carry
        lax.fori_loop(ltot_ref[tile] // SUBLANES, lr // SUBLANES, zero_rows, 0)

    @pl.when(j == 0)
    def _():
        fetch(j)

    @pl.when(j + 1 < pl.num_programs(0))
    def _():
        fetch(j + 1)

    liota = lax.broadcasted_iota(I32, (tl, lr), 1)
    pick = _one_hot_hits(liota, lambda k: lp_ref[:, k:k + 1]).astype(F32).astype(BF16)
    _wait_rows(ys_hbm, ltot_ref[j], sem.at[j % 2])
    y = _dot(pick, yl_ref[j % 2].astype(BF16))
    o_ref[...] = _layer_norm(alpha * x_ref[...] + mod_ref[5:6, :] * y, lng_ref[...], lnb_ref[...])


def _combine(ys, lp_t, goff, loff, cp, ltot, x1, mods_l, ln_g, ln_b, alpha, seq_len):
    t, d = x1.shape
    tl = min(MOE_TOKENS, t)
    n_exp = goff.shape[0] // (t // tl)
    lr = TOP_K * tl + n_exp * SUBLANES
    per_seq = seq_len // tl
    vec = pl.BlockSpec((1, d), lambda j, *_: (0, 0))
    grid_spec = pltpu.PrefetchScalarGridSpec(
        num_scalar_prefetch=4,
        grid=(t // tl,),
        in_specs=[
            pl.BlockSpec(memory_space=pl.ANY),
            pl.BlockSpec((tl, d), lambda j, *_: (j, 0)),
            pl.BlockSpec((tl, TOP_K), lambda j, *_: (j, 0)),
            pl.BlockSpec((None, 6, d), lambda j, *_: (j // per_seq, 0, 0)),
            vec, vec,
        ],
        out_specs=pl.BlockSpec((tl, d), lambda j, *_: (j, 0)),
        scratch_shapes=[pltpu.VMEM((2, lr, d), F32), pltpu.SemaphoreType.DMA((2,))],
    )
    return pl.pallas_call(
        functools.partial(_combine_body, n_exp=n_exp, alpha=alpha),
        grid_spec=grid_spec,
        out_shape=jax.ShapeDtypeStruct((t, d), F32),
        compiler_params=_params("arbitrary"),
        name="combine",
    )(goff, loff, cp, ltot, ys, x1, lp_t, mods_l, ln_g.reshape(1, d), ln_b.reshape(1, d))


def _moe_layer(x1, h2, mods_l, w_r, b_r, w_gu, b_gu, w_dn, b_dn, layer, ln_g, ln_b, alpha, seq_len):
    t, d = x1.shape
    n_exp = w_r.shape[1]
    tm = MOE_TILE
    tl = min(MOE_TOKENS, t)
    n_tiles = t // tl
    idx, gates, rank, cnt = _router(h2, w_r, b_r)
    cp = (cnt[:, :, 0] + SUBLANES - 1) // SUBLANES * SUBLANES
    tot = jnp.sum(cp, axis=0)
    padded = (tot + tm - 1) // tm * tm
    pend = jnp.cumsum(padded)
    pstart = pend - padded
    goff = pstart[None, :] + jnp.cumsum(cp, axis=0) - cp
    loff = jnp.cumsum(cp, axis=1) - cp
    ltot = jnp.sum(cp, axis=1)
    is_e = idx[..., None] == jnp.arange(n_exp, dtype=I32)
    lp = jnp.sum(jnp.where(is_e, jnp.repeat(loff, tl, axis=0)[None], 0), axis=-1) + rank
    n_blocks = -(-(t * TOP_K + n_tiles * n_exp * (SUBLANES - 1)) // tm) + n_exp
    n_valid = pend[-1] // tm
    blk = jnp.arange(n_blocks, dtype=I32)
    block_src = jnp.minimum(blk, n_valid - 1)
    block_e = jnp.sum(pend[None, :] <= (block_src * tm)[:, None], axis=1).astype(I32)
    n_rows = n_blocks * tm
    pad_start = jnp.concatenate([pstart + tot, pend[-1:]]).astype(I32)
    pad_cnt = jnp.concatenate([padded - tot, n_rows - pend[-1:]]).astype(I32)
    flat = lambda a: a.reshape(-1).astype(I32)
    xs = _dispatch(h2, lp, gates, flat(goff), flat(loff), flat(cp), ltot.astype(I32), pad_start, pad_cnt, n_rows)
    ys = _experts(xs, block_e, block_src.astype(I32), n_valid.reshape(1).astype(I32), w_gu, b_gu, w_dn, b_dn, layer)
    return _combine(ys, lp.T, flat(goff), flat(loff), flat(cp), ltot.astype(I32), x1, mods_l, ln_g, ln_b, alpha,
                    seq_len)


def kernel(x, c, ada_w, ada_b, post_ln_g, post_ln_b, conv_w_pw1, conv_b_pw1, conv_w_dw, conv_b_dw, conv_ln_g, conv_ln_b, conv_w_pw2, conv_b_pw2, w_kv, attn_w_q, attn_lambda, attn_subln_g, attn_w_o, rel_bias_table, router_w, router_b, expert_w_gate_up, expert_b_gate_up, expert_w_down, expert_b_down):
    bsz, s, d = x.shape
    depth = ada_w.shape[0]
    n_a = depth // 2
    alpha = (2 * depth) ** 0.25
    mods = _ada(c, ada_w, ada_b).reshape(depth, bsz, 6, d)
    q = k = v = None
    for l in range(depth):
        mods_l = mods[l]
        if l < n_a:
            u = _conv_a(x, mods_l, conv_w_pw1[l], conv_b_pw1[l])
            x1, h2 = _conv_b(u, x, mods_l, conv_w_dw[l], conv_b_dw[l], conv_ln_g[l], conv_ln_b[l],
                             conv_w_pw2[l], conv_b_pw2[l], post_ln_g[l, 0], post_ln_b[l, 0], alpha)
        else:
            j = l - n_a
            if j == 0:
                q, k, v = _qkv(x, mods_l, w_kv, attn_w_q[j])
            else:
                q = _qkv(x, mods_l, w_kv, attn_w_q[j])[0]
            lambda_init = 0.8 - 0.6 * math.exp(-0.3 * l)
            o = _attn(q, k, v, attn_lambda[j], attn_subln_g[j], rel_bias_table, lambda_init)
            x1, h2 = _attn_out(o, x, mods_l, attn_w_o[j], post_ln_g[l, 0], post_ln_b[l, 0], alpha)
        x = _moe_layer(x1.reshape(bsz * s, d), h2.reshape(bsz * s, d), mods_l, router_w[l], router_b[l],
                       expert_w_gate_up, expert_b_gate_up, expert_w_down, expert_b_down, l,
                       post_ln_g[l, 1], post_ln_b[l, 1], alpha, s).reshape(bsz, s, d)
    return x
```

```python
import functools
import math

import jax
import jax.numpy as jnp
from jax import lax
from jax.experimental import pallas as pl
from jax.experimental.pallas import tpu as pltpu

F32 = jnp.float32
BF16 = jnp.bfloat16
I32 = jnp.int32
HIGHEST = lax.Precision.HIGHEST

CHUNK = 64
CONV_WIDTH = 31
HEAD_DIM = 64
REL_BUCKETS = 32
REL_MAX_DIST = 128
TOP_K = 4
SWIGLU_LIMIT = 7.0
SWIGLU_ALPHA = 1.702
LN_EPS = 1e-5
MASK_VALUE = -1e30
LOG2_E = math.log2(math.e)

SUBLANES = 8
LANES = 128
VMEM_LIMIT_BYTES = 56 * 1024 * 1024

ADA_TN = 1024
SEQ_TILE = 256
CONV_HALO = 32
CONV_ROWS = 64
CONV_COLS = 256
MOE_TOKENS = 256
MOE_TILE = 512
Q_BLOCK = 128
CAST_ROWS = 128


def _params(*sem):
    return pltpu.CompilerParams(dimension_semantics=sem, vmem_limit_bytes=VMEM_LIMIT_BYTES)


def _layer_norm(x, g, b):
    mu = jnp.mean(x, axis=-1, keepdims=True)
    xc = x - mu
    var = jnp.mean(xc * xc, axis=-1, keepdims=True)
    return xc * lax.rsqrt(var + LN_EPS) * g + b


def _dot(a, b):
    return jnp.dot(a, b, preferred_element_type=F32)


def _ada_body(c_ref, w_ref, b_ref, o_ref):
    c = c_ref[...]
    cond = c * jax.nn.sigmoid(c)
    o_ref[...] = jnp.dot(cond, w_ref[...], preferred_element_type=F32, precision=HIGHEST) + b_ref[...]


def _ada(c, ada_w, ada_b):
    depth, d, n = ada_w.shape
    bsz = c.shape[0]
    tn = min(ADA_TN, n)
    return pl.pallas_call(
        _ada_body,
        grid=(depth, n // tn),
        in_specs=[
            pl.BlockSpec((bsz, d), lambda l, j: (0, 0)),
            pl.BlockSpec((None, d, tn), lambda l, j: (l, 0, j)),
            pl.BlockSpec((None, 1, tn), lambda l, j: (l, 0, j)),
        ],
        out_specs=pl.BlockSpec((None, bsz, tn), lambda l, j: (l, 0, j)),
        out_shape=jax.ShapeDtypeStruct((depth, bsz, n), F32),
        compiler_params=_params("parallel", "parallel"),
        name="ada",
    )(c, ada_w, ada_b.reshape(depth, 1, n))


def _conv_a_body(x_ref, mod_ref, w_ref, b_ref, u_ref):
    d = x_ref.shape[-1]
    h = (x_ref[...] * (1.0 + mod_ref[1:2, :]) + mod_ref[0:1, :]).astype(BF16)
    a = _dot(h, w_ref[:, :d]) + b_ref[:, :d]
    g = _dot(h, w_ref[:, d:]) + b_ref[:, d:]
    u_ref[...] = a * jax.nn.sigmoid(g)


def _conv_a(x, mods_l, w_pw1, b_pw1):
    bsz, s, d = x.shape
    ts = min(SEQ_TILE, s)
    return pl.pallas_call(
        _conv_a_body,
        grid=(bsz, s // ts),
        in_specs=[
            pl.BlockSpec((None, ts, d), lambda b, i: (b, i, 0)),
            pl.BlockSpec((None, 6, d), lambda b, i: (b, 0, 0)),
            pl.BlockSpec((d, 2 * d), lambda b, i: (0, 0)),
            pl.BlockSpec((1, 2 * d), lambda b, i: (0, 0)),
        ],
        out_specs=pl.BlockSpec((None, ts, d), lambda b, i: (b, i, 0)),
        out_shape=jax.ShapeDtypeStruct((bsz, s, d), F32),
        compiler_params=_params("parallel", "parallel"),
        name="conv_a",
    )(x, mods_l, w_pw1.astype(BF16), b_pw1.reshape(1, 2 * d))


def _residual_epilogue(x, y, mod_ref, lng_ref, lnb_ref, alpha, x1_ref, h2_ref):
    x1 = _layer_norm(alpha * x + mod_ref[2:3, :] * y, lng_ref[...], lnb_ref[...])
    x1_ref[...] = x1
    h2_ref[...] = x1 * (1.0 + mod_ref[4:5, :]) + mod_ref[3:4, :]


def _conv_b_body(u_ref, halo_ref, x_ref, mod_ref, wdw_ref, bdw_ref, cg_ref, cb_ref, w2_ref, b2_ref,
                 lng_ref, lnb_ref, x1_ref, h2_ref, win_ref, v_ref, *, alpha):
    ts, d = u_ref.shape
    i = pl.program_id(1)
    halo = halo_ref[...]
    win_ref[0, 0:CONV_HALO, :] = jnp.where(i == 0, jnp.zeros_like(halo), halo)
    win_ref[0, CONV_HALO:, :] = u_ref[...]
    n_shift = ts + CONV_HALO - SUBLANES
    for b in range(1, SUBLANES):
        win_ref[b, 0:n_shift, :] = win_ref[0, b:b + n_shift, :]
    off = CONV_HALO - (CONV_WIDTH - 1)
    rows = min(CONV_ROWS, ts)
    cols = min(CONV_COLS, d)
    for c0 in range(0, d, cols):
        for r0 in range(0, ts, rows):
            acc = jnp.zeros((rows, cols), F32)
            for j in range(CONV_WIDTH):
                a, b = divmod(off + j, SUBLANES)
                r = r0 + a * SUBLANES
                acc = acc + wdw_ref[j:j + 1, c0:c0 + cols] * win_ref[b, r:r + rows, c0:c0 + cols]
            v_ref[r0:r0 + rows, c0:c0 + cols] = acc
    v = _layer_norm(v_ref[...] + bdw_ref[...], cg_ref[...], cb_ref[...])
    v = (v * jax.nn.sigmoid(v)).astype(BF16)
    y = _dot(v, w2_ref[...]) + b2_ref[...]
    _residual_epilogue(x_ref[...], y, mod_ref, lng_ref, lnb_ref, alpha, x1_ref, h2_ref)


def _conv_b(u, x, mods_l, w_dw, b_dw, cln_g, cln_b, w_pw2, b_pw2, ln_g, ln_b, alpha):
    bsz, s, d = x.shape
    ts = min(SEQ_TILE, s)
    hb = ts // CONV_HALO
    row = lambda a: a.reshape(1, d)
    tile = pl.BlockSpec((None, ts, d), lambda b, i: (b, i, 0))
    vec = pl.BlockSpec((1, d), lambda b, i: (0, 0))
    return pl.pallas_call(
        functools.partial(_conv_b_body, alpha=alpha),
        grid=(bsz, s // ts),
        in_specs=[
            tile,
            pl.BlockSpec((None, CONV_HALO, d), lambda b, i: (b, jnp.maximum(i * hb - 1, 0), 0)),
            tile,
            pl.BlockSpec((None, 6, d), lambda b, i: (b, 0, 0)),
            pl.BlockSpec((CONV_WIDTH, d), lambda b, i: (0, 0)),
            vec, vec, vec,
            pl.BlockSpec((d, d), lambda b, i: (0, 0)),
            vec, vec, vec,
        ],
        out_specs=[tile, tile],
        out_shape=[jax.ShapeDtypeStruct((bsz, s, d), F32)] * 2,
        scratch_shapes=[pltpu.VMEM((SUBLANES, ts + CONV_HALO, d), F32), pltpu.VMEM((ts, d), F32)],
        compiler_params=_params("parallel", "parallel"),
        name="conv_b",
    )(u, u, x, mods_l, w_dw, row(b_dw), row(cln_g), row(cln_b), w_pw2.astype(BF16), row(b_pw2),
      row(ln_g), row(ln_b))


def _qkv_body(x_ref, mod_ref, wkv_ref, wq_ref, q_ref, k_ref, v_ref):
    d = x_ref.shape[-1]
    x = x_ref[...]
    kv = _dot(x.astype(BF16), wkv_ref[...])
    k_ref[...] = kv[:, :d].astype(BF16)
    v_ref[...] = kv[:, d:].astype(BF16)
    h = (x * (1.0 + mod_ref[1:2, :]) + mod_ref[0:1, :]).astype(BF16)
    q_ref[...] = (_dot(h, wq_ref[...]) * (HEAD_DIM ** -0.5 * LOG2_E)).astype(BF16)


def _qkv(x, mods_l, w_kv, w_q):
    bsz, s, d = x.shape
    ts = min(SEQ_TILE, s)
    tile = pl.BlockSpec((None, ts, d), lambda b, i: (b, i, 0))
    return pl.pallas_call(
        _qkv_body,
        grid=(bsz, s // ts),
        in_specs=[
            tile,
            pl.BlockSpec((None, 6, d), lambda b, i: (b, 0, 0)),
            pl.BlockSpec((d, 2 * d), lambda b, i: (0, 0)),
            pl.BlockSpec((d, d), lambda b, i: (0, 0)),
        ],
        out_specs=[tile, tile, tile],
        out_shape=[jax.ShapeDtypeStruct((bsz, s, d), BF16)] * 3,
        compiler_params=_params("parallel", "parallel"),
        name="qkv",
    )(x, mods_l, w_kv.astype(BF16), w_q.astype(BF16))


def _t5_bucket(rel):
    nb = REL_BUCKETS // 2
    ret = jnp.where(rel > 0, nb, 0)
    n = jnp.abs(rel)
    max_exact = nb // 2
    large = max_exact + (jnp.log(jnp.maximum(n, 1).astype(F32) / max_exact)
                         / math.log(REL_MAX_DIST / max_exact) * (nb - max_exact)).astype(I32)
    large = jnp.minimum(large, nb - 1)
    return ret + jnp.where(n < max_exact, n, large)


def _bucket_strip(s):
    r = jnp.arange(Q_BLOCK, dtype=I32)[:, None]
    kp = jnp.arange(s, dtype=I32)[None, :] - (s - Q_BLOCK)
    bucket = _t5_bucket(kp - r)
    visible = jnp.floor_divide(kp, CHUNK) <= (r // CHUNK)
    return jnp.where(visible, bucket, REL_BUCKETS)


def _attn_body(tab_ref, q_ref, k_ref, v_ref, bkt_ref, lam_ref, sg_ref, o_ref, bias_ref, *, lambda_init, n_heads):
    h = pl.program_id(0)
    b = pl.program_id(1)
    s = q_ref.shape[0]

    @pl.when(b == 0)
    def _():
        bk = bkt_ref[...]
        acc = jnp.full(bk.shape, MASK_VALUE, F32)
        for r in range(REL_BUCKETS):
            acc = jnp.where(bk == r, tab_ref[r * n_heads + h] * LOG2_E, acc)
        bias_ref[0:Q_BLOCK, :] = acc
        bias_ref[Q_BLOCK:, :] = acc

    lp = lam_ref[...]
    lam = (jnp.exp(jnp.sum(lp[0:1, :] * lp[1:2, :], axis=-1, keepdims=True))
           - jnp.exp(jnp.sum(lp[2:3, :] * lp[3:4, :], axis=-1, keepdims=True)) + lambda_init)
    lane = lax.broadcasted_iota(I32, (Q_BLOCK, 2 * HEAD_DIM), 1)
    nt = (((1,), (1,)), ((), ()))
    def scores(i):
        n_keys = (i + 1) * Q_BLOCK
        q = q_ref[i * Q_BLOCK:(i + 1) * Q_BLOCK, :]
        qq = jnp.concatenate([jnp.where(lane < HEAD_DIM, q, jnp.zeros_like(q)),
                              jnp.where(lane >= HEAD_DIM, q, jnp.zeros_like(q))], axis=0)
        return lax.dot_general(qq, k_ref[0:n_keys, :], nt, preferred_element_type=F32) + bias_ref[:, s - n_keys:s]

    n_blocks = s // Q_BLOCK
    sc_next = scores(0)
    for i in range(n_blocks):
        n_keys = (i + 1) * Q_BLOCK
        sc = sc_next
        if i + 1 < n_blocks:
            sc_next = scores(i + 1)
        p = jnp.exp2(sc - jnp.max(sc, axis=-1, keepdims=True))
        denom = jnp.sum(p, axis=-1, keepdims=True)
        pv = _dot(p.astype(BF16), v_ref[0:n_keys, :])
        o = pv[:Q_BLOCK] * (1.0 / denom[:Q_BLOCK]) - pv[Q_BLOCK:] * (lam / denom[Q_BLOCK:])
        o = o * lax.rsqrt(jnp.mean(o * o, axis=-1, keepdims=True) + LN_EPS) * sg_ref[...]
        o_ref[i * Q_BLOCK:(i + 1) * Q_BLOCK, :] = (o * (1.0 - lambda_init)).astype(BF16)


def _attn(q, k, v, lam_p, subln_g, rel_table, lambda_init):
    bsz, s, d = q.shape
    hd2 = 2 * HEAD_DIM
    n_heads = d // hd2
    head = pl.BlockSpec((None, s, hd2), lambda h, b, tab: (b, 0, h))
    grid_spec = pltpu.PrefetchScalarGridSpec(
        num_scalar_prefetch=1,
        grid=(n_heads, bsz),
        in_specs=[
            head, head, head,
            pl.BlockSpec((Q_BLOCK, s), lambda h, b, tab: (0, 0)),
            pl.BlockSpec((4, HEAD_DIM), lambda h, b, tab: (0, 0)),
            pl.BlockSpec((1, hd2), lambda h, b, tab: (0, 0)),
        ],
        out_specs=head,
        scratch_shapes=[pltpu.VMEM((2 * Q_BLOCK, s), F32)],
    )
    return pl.pallas_call(
        functools.partial(_attn_body, lambda_init=lambda_init, n_heads=n_heads),
        grid_spec=grid_spec,
        out_shape=jax.ShapeDtypeStruct((bsz, s, d), BF16),
        compiler_params=_params("arbitrary", "arbitrary"),
        name="attn",
    )(rel_table.reshape(-1), q, k, v, _bucket_strip(s), lam_p, subln_g.reshape(1, hd2))


def _attn_out_body(o_ref, x_ref, mod_ref, wo_ref, lng_ref, lnb_ref, x1_ref, h2_ref, *, alpha):
    y = _dot(o_ref[...], wo_ref[...])
    _residual_epilogue(x_ref[...], y, mod_ref, lng_ref, lnb_ref, alpha, x1_ref, h2_ref)


def _attn_out(o, x, mods_l, w_o, ln_g, ln_b, alpha):
    bsz, s, d = x.shape
    ts = min(SEQ_TILE, s)
    tile = pl.BlockSpec((None, ts, d), lambda b, i: (b, i, 0))
    vec = pl.BlockSpec((1, d), lambda b, i: (0, 0))
    return pl.pallas_call(
        functools.partial(_attn_out_body, alpha=alpha),
        grid=(bsz, s // ts),
        in_specs=[tile, tile, pl.BlockSpec((None, 6, d), lambda b, i: (b, 0, 0)),
                  pl.BlockSpec((d, d), lambda b, i: (0, 0)), vec, vec],
        out_specs=[tile, tile],
        out_shape=[jax.ShapeDtypeStruct((bsz, s, d), F32)] * 2,
        compiler_params=_params("parallel", "parallel"),
        name="attn_out",
    )(o, x, mods_l, w_o.astype(BF16), ln_g.reshape(1, d), ln_b.reshape(1, d))


def _router_body(h_ref, wt_ref, b_ref, tri_ref, idx_ref, gate_ref, rank_ref, cnt_ref):
    logits = lax.dot_general(wt_ref[...], h_ref[...], (((1,), (1,)), ((), ())),
                             preferred_element_type=F32, precision=HIGHEST) + b_ref[...]
    n_exp, tr = logits.shape
    eio = lax.broadcasted_iota(I32, (n_exp, tr), 0)
    work = logits
    vals, idxs = [], []
    for _ in range(TOP_K):
        m = jnp.max(work, axis=0, keepdims=True)
        am = jnp.min(jnp.where(work == m, eio, n_exp), axis=0, keepdims=True)
        vals.append(m)
        idxs.append(am)
        work = jnp.where(eio == am, -jnp.inf, work)
    ex = [jnp.exp(v - vals[0]) for v in vals]
    den = ex[0] + ex[1] + ex[2] + ex[3]
    onehot = jnp.zeros((n_exp, tr), F32)
    for k in range(TOP_K):
        onehot = onehot + (eio == idxs[k]).astype(F32)
    before = _dot(onehot.astype(BF16), tri_ref[...])
    for k in range(TOP_K):
        idx_ref[k:k + 1, :] = idxs[k]
        gate_ref[k:k + 1, :] = ex[k] / den
        rank_ref[k:k + 1, :] = jnp.sum(jnp.where(eio == idxs[k], before, 0.0), axis=0, keepdims=True).astype(I32)
    cnt_ref[...] = jnp.sum(onehot, axis=1, keepdims=True).astype(I32)


def _router(h2, w_r, b_r):
    t, d = h2.shape
    n_exp = w_r.shape[1]
    tr = min(MOE_TOKENS, t)
    pos = jnp.arange(tr, dtype=I32)
    tri = (pos[:, None] < pos[None, :]).astype(BF16)
    tok = pl.BlockSpec((TOP_K, tr), lambda i: (0, i))
    return pl.pallas_call(
        _router_body,
        grid=(t // tr,),
        in_specs=[
            pl.BlockSpec((tr, d), lambda i: (i, 0)),
            pl.BlockSpec((n_exp, d), lambda i: (0, 0)),
            pl.BlockSpec((n_exp, 1), lambda i: (0, 0)),
            pl.BlockSpec((tr, tr), lambda i: (0, 0)),
        ],
        out_specs=[tok, tok, tok, pl.BlockSpec((None, n_exp, 1), lambda i: (i, 0, 0))],
        out_shape=[jax.ShapeDtypeStruct((TOP_K, t), I32), jax.ShapeDtypeStruct((TOP_K, t), F32),
                   jax.ShapeDtypeStruct((TOP_K, t), I32), jax.ShapeDtypeStruct((t // tr, n_exp, 1), I32)],
        compiler_params=_params("parallel"),
        name="router",
    )(h2, w_r.T, b_r.reshape(n_exp, 1), tri)


def _rows(ref, start, cnt):
    aligned = lambda v: v if isinstance(v, int) else pl.multiple_of(v, SUBLANES)
    return ref.at[pl.ds(aligned(start), aligned(cnt))]


def _segment_copy(src, src_row, dst, dst_row, cnt, sem):
    @pl.when(cnt > 0)
    def _():
        pltpu.make_async_copy(_rows(src, src_row, cnt), _rows(dst, dst_row, cnt), sem).start()


def _wait_rows(ref, cnt, sem):
    @pl.when(cnt > 0)
    def _():
        pltpu.make_async_copy(_rows(ref, 0, cnt), _rows(ref, 0, cnt), sem).wait()


def _zero_fill(zero_ref, xs_hbm, start, cnt, sem):
    zr = zero_ref.shape[0]
    n_full = cnt // zr

    def full(r, c):
        _segment_copy(zero_ref, 0, xs_hbm, start + r * zr, zr, sem)
        return c

    lax.fori_loop(0, n_full, full, 0)
    _segment_copy(zero_ref, 0, xs_hbm, start + n_full * zr, cnt - n_full * zr, sem)
    _wait_rows(xs_hbm, cnt, sem)


def _one_hot_hits(iota, pos_of):
    hit = iota == pos_of(0)
    for k in range(1, TOP_K):
        hit = hit | (iota == pos_of(k))
    return hit


def _dispatch_body(goff_ref, loff_ref, cp_ref, ltot_ref, pad_start_ref, pad_cnt_ref, h_ref, lp_ref, gate_ref, xs_hbm,
                   xl_ref, zero_ref, sem, zsem, *, n_exp):
    j = pl.program_id(0)
    tl, d = h_ref.shape
    lr = xl_ref.shape[1]

    @pl.when(j == 0)
    def _():
        zero_ref[...] = jnp.zeros_like(zero_ref)

        def per_range(e, carry):
            _zero_fill(zero_ref, xs_hbm, pad_start_ref[e], pad_cnt_ref[e], zsem)
            return carry
        lax.fori_loop(0, pad_start_ref.shape[0], per_range, 0)

    def start_segments(tile):
        def per_expert(e, carry):
            seg = tile * n_exp + e
            _segment_copy(xl_ref.at[tile % 2], loff_ref[seg], xs_hbm, goff_ref[seg], cp_ref[seg], sem.at[tile % 2])
            return carry
        lax.fori_loop(0, n_exp, per_expert, 0)

    def wait_segments(tile):
        _wait_rows(xs_hbm, ltot_ref[tile], sem.at[tile % 2])

    @pl.when(j >= 2)
    def _():
        wait_segments(j - 2)

    buf = xl_ref.at[j % 2]
    riota = lax.broadcasted_iota(I32, (lr, tl), 0)
    perm = _one_hot_hits(riota, lambda k: lp_ref[k:k + 1, :]).astype(F32).astype(BF16)
    buf[:, 0:d] = _dot(perm, h_ref[...].astype(BF16))
    gsel = jnp.zeros((lr, tl), F32)
    for k in range(TOP_K):
        gsel = gsel + jnp.where(riota == lp_ref[k:k + 1, :], gate_ref[k:k + 1, :], 0.0)
    buf[:, d:] = jnp.broadcast_to(jnp.sum(gsel, axis=1, keepdims=True), (lr, LANES))
    start_segments(j)

    @pl.when(j == pl.num_programs(0) - 1)
    def _():
        @pl.when(j >= 1)
        def _():
            wait_segments(j - 1)
        wait_segments(j)


def _dispatch(h2, lp, gates, goff, loff, cp, ltot, pad_start, pad_cnt, n_rows):
    t, d = h2.shape
    tl = min(MOE_TOKENS, t)
    n_exp = goff.shape[0] // (t // tl)
    lr = TOP_K * tl + n_exp * SUBLANES
    tok = pl.BlockSpec((TOP_K, tl), lambda j, *_: (0, j))
    grid_spec = pltpu.PrefetchScalarGridSpec(
        num_scalar_prefetch=6,
        grid=(t // tl,),
        in_specs=[pl.BlockSpec((tl, d), lambda j, *_: (j, 0)), tok, tok],
        out_specs=pl.BlockSpec(memory_space=pl.ANY),
        scratch_shapes=[pltpu.VMEM((2, lr, d + LANES), F32), pltpu.VMEM((MOE_TILE, d + LANES), F32),
                        pltpu.SemaphoreType.DMA((2,)), pltpu.SemaphoreType.DMA],
    )
    return pl.pallas_call(
        functools.partial(_dispatch_body, n_exp=n_exp),
        grid_spec=grid_spec,
        out_shape=jax.ShapeDtypeStruct((n_rows, d + LANES), F32),
        compiler_params=_params("arbitrary"),
        name="dispatch",
    )(goff, loff, cp, ltot, pad_start, pad_cnt, h2, lp, gates)


def _experts_body(be_ref, bsrc_ref, nvalid_ref, xs_ref, wgu_ref, bgu_ref, wdn_ref, bdn_ref, ys_ref,
                  wgu_bf, wdn_bf):
    i = pl.program_id(0)
    e = be_ref[i]
    prev = be_ref[jnp.maximum(i - 1, 0)]
    f = wdn_ref.shape[0]

    @pl.when((i == 0) | (e != prev))
    def _():
        def cast(ref_in, ref_out):
            def step(r, c):
                rows = pl.ds(pl.multiple_of(r * CAST_ROWS, CAST_ROWS), CAST_ROWS)
                ref_out[rows, :] = ref_in[rows, :].astype(BF16)
                return c
            lax.fori_loop(0, ref_in.shape[0] // CAST_ROWS, step, 0)
        cast(wgu_ref, wgu_bf)
        cast(wdn_ref, wdn_bf)

    @pl.when(i < nvalid_ref[0])
    def _():
        d = wgu_ref.shape[0]
        x = xs_ref[:, 0:d].astype(BF16)
        gate = jnp.minimum(_dot(x, wgu_bf[:, :f]) + bgu_ref[:, :f], SWIGLU_LIMIT)
        lin = jnp.clip(_dot(x, wgu_bf[:, f:]) + bgu_ref[:, f:], -SWIGLU_LIMIT, SWIGLU_LIMIT)
        act = (gate * jax.nn.sigmoid(SWIGLU_ALPHA * gate) * (lin + 1.0)).astype(BF16)
        ys_ref[...] = (_dot(act, wdn_bf[...]) + bdn_ref[...]) * xs_ref[:, d:d + 1]

    @pl.when(i >= nvalid_ref[0])
    def _():
        ys_ref[...] = jnp.zeros_like(ys_ref)


def _experts(xs, block_e, block_src, n_valid, w_gu, b_gu, w_dn, b_dn, layer):
    n_rows, dx = xs.shape
    depth, n_exp, d, f2 = w_gu.shape
    f = f2 // 2
    tm = MOE_TILE
    grid_spec = pltpu.PrefetchScalarGridSpec(
        num_scalar_prefetch=3,
        grid=(n_rows // tm,),
        in_specs=[
            pl.BlockSpec((tm, dx), lambda i, be, bs, nv: (bs[i], 0)),
            pl.BlockSpec((None, None, d, f2), lambda i, be, bs, nv: (layer, be[i], 0, 0)),
            pl.BlockSpec((None, None, 1, f2), lambda i, be, bs, nv: (layer, be[i], 0, 0)),
            pl.BlockSpec((None, None, f, d), lambda i, be, bs, nv: (layer, be[i], 0, 0)),
            pl.BlockSpec((None, None, 1, d), lambda i, be, bs, nv: (layer, be[i], 0, 0)),
        ],
        out_specs=pl.BlockSpec((tm, d), lambda i, be, bs, nv: (i, 0)),
        scratch_shapes=[pltpu.VMEM((d, f2), BF16), pltpu.VMEM((f, d), BF16)],
    )
    return pl.pallas_call(
        _experts_body,
        grid_spec=grid_spec,
        out_shape=jax.ShapeDtypeStruct((n_rows, d), F32),
        compiler_params=_params("arbitrary"),
        name="experts",
    )(block_e, block_src, n_valid, xs, w_gu, b_gu.reshape(depth, n_exp, 1, f2), w_dn,
      b_dn.reshape(depth, n_exp, 1, d))


def _combine_body(goff_ref, loff_ref, cp_ref, ltot_ref, ys_hbm, x_ref, lp_ref, mod_ref, lng_ref, lnb_ref, o_ref,
                  yl_ref, sem, *, n_exp, alpha):
    j = pl.program_id(0)
    tl = x_ref.shape[0]
    lr, d = yl_ref.shape[1:]

    def fetch(tile):
        buf = yl_ref.at[tile % 2]

        def per_expert(e, carry):
            seg = tile * n_exp + e
            _segment_copy(ys_hbm, goff_ref[seg], buf, loff_ref[seg], cp_ref[seg], sem.at[tile % 2])
            return carry
        lax.fori_loop(0, n_exp, per_expert, 0)

        def zero_rows(r, carry):
            buf[pl.ds(pl.multiple_of(r * SUBLANES, SUBLANES), SUBLANES), :] = jnp.zeros((SUBLANES, d), F32)
            return carry
        lax.fori_loop(ltot_ref[tile] // SUBLANES, lr // SUBLANES, zero_rows, 0)

    @pl.when(j == 0)
    def _():
        fetch(j)

    @pl.when(j + 1 < pl.num_programs(0))
    def _():
        fetch(j + 1)

    liota = lax.broadcasted_iota(I32, (tl, lr), 1)
    pick = _one_hot_hits(liota, lambda k: lp_ref[:, k:k + 1]).astype(F32).astype(BF16)
    _wait_rows(ys_hbm, ltot_ref[j], sem.at[j % 2])
    y = _dot(pick, yl_ref[j % 2].astype(BF16))
    o_ref[...] = _layer_norm(alpha * x_ref[...] + mod_ref[5:6, :] * y, lng_ref[...], lnb_ref[...])


def _combine(ys, lp_t, goff, loff, cp, ltot, x1, mods_l, ln_g, ln_b, alpha, seq_len):
    t, d = x1.shape
    tl = min(MOE_TOKENS, t)
    n_exp = goff.shape[0] // (t // tl)
    lr = TOP_K * tl + n_exp * SUBLANES
    per_seq = seq_len // tl
    vec = pl.BlockSpec((1, d), lambda j, *_: (0, 0))
    grid_spec = pltpu.PrefetchScalarGridSpec(
        num_scalar_prefetch=4,
        grid=(t // tl,),
        in_specs=[
            pl.BlockSpec(memory_space=pl.ANY),
            pl.BlockSpec((tl, d), lambda j, *_: (j, 0)),
            pl.BlockSpec((tl, TOP_K), lambda j, *_: (j, 0)),
            pl.BlockSpec((None, 6, d), lambda j, *_: (j // per_seq, 0, 0)),
            vec, vec,
        ],
        out_specs=pl.BlockSpec((tl, d), lambda j, *_: (j, 0)),
        scratch_shapes=[pltpu.VMEM((2, lr, d), F32), pltpu.SemaphoreType.DMA((2,))],
    )
    return pl.pallas_call(
        functools.partial(_combine_body, n_exp=n_exp, alpha=alpha),
        grid_spec=grid_spec,
        out_shape=jax.ShapeDtypeStruct((t, d), F32),
        compiler_params=_params("arbitrary"),
        name="combine",
    )(goff, loff, cp, ltot, ys, x1, lp_t, mods_l, ln_g.reshape(1, d), ln_b.reshape(1, d))


def _moe_layer(x1, h2, mods_l, w_r, b_r, w_gu, b_gu, w_dn, b_dn, layer, ln_g, ln_b, alpha, seq_len):
    t, d = x1.shape
    n_exp = w_r.shape[1]
    tm = MOE_TILE
    tl = min(MOE_TOKENS, t)
    n_tiles = t // tl
    idx, gates, rank, cnt = _router(h2, w_r, b_r)
    cp = (cnt[:, :, 0] + SUBLANES - 1) // SUBLANES * SUBLANES
    tot = jnp.sum(cp, axis=0)
    padded = (tot + tm - 1) // tm * tm
    pend = jnp.cumsum(padded)
    pstart = pend - padded
    goff = pstart[None, :] + jnp.cumsum(cp, axis=0) - cp
    loff = jnp.cumsum(cp, axis=1) - cp
    ltot = jnp.sum(cp, axis=1)
    is_e = idx[..., None] == jnp.arange(n_exp, dtype=I32)
    lp = jnp.sum(jnp.where(is_e, jnp.repeat(loff, tl, axis=0)[None], 0), axis=-1) + rank
    n_blocks = -(-(t * TOP_K + n_tiles * n_exp * (SUBLANES - 1)) // tm) + n_exp
    n_valid = pend[-1] // tm
    blk = jnp.arange(n_blocks, dtype=I32)
    block_src = jnp.minimum(blk, n_valid - 1)
    block_e = jnp.sum(pend[None, :] <= (block_src * tm)[:, None], axis=1).astype(I32)
    n_rows = n_blocks * tm
    pad_start = jnp.concatenate([pstart + tot, pend[-1:]]).astype(I32)
    pad_cnt = jnp.concatenate([padded - tot, n_rows - pend[-1:]]).astype(I32)
    flat = lambda a: a.reshape(-1).astype(I32)
    xs = _dispatch(h2, lp, gates, flat(goff), flat(loff), flat(cp), ltot.astype(I32), pad_start, pad_cnt, n_rows)
    ys = _experts(xs, block_e, block_src.astype(I32), n_valid.reshape(1).astype(I32), w_gu, b_gu, w_dn, b_dn, layer)
    return _combine(ys, lp.T, flat(goff), flat(loff), flat(cp), ltot.astype(I32), x1, mods_l, ln_g, ln_b, alpha,
                    seq_len)


def kernel(x, c, ada_w, ada_b, post_ln_g, post_ln_b, conv_w_pw1, conv_b_pw1, conv_w_dw, conv_b_dw, conv_ln_g, conv_ln_b, conv_w_pw2, conv_b_pw2, w_kv, attn_w_q, attn_lambda, attn_subln_g, attn_w_o, rel_bias_table, router_w, router_b, expert_w_gate_up, expert_b_gate_up, expert_w_down, expert_b_down):
    bsz, s, d = x.shape
    depth = ada_w.shape[0]
    n_a = depth // 2
    alpha = (2 * depth) ** 0.25
    mods = _ada(c, ada_w, ada_b).reshape(depth, bsz, 6, d)
    q = k = v = None
    for l in range(depth):
        mods_l = mods[l]
        if l < n_a:
            u = _conv_a(x, mods_l, conv_w_pw1[l], conv_b_pw1[l])
            x1, h2 = _conv_b(u, x, mods_l, conv_w_dw[l], conv_b_dw[l], conv_ln_g[l], conv_ln_b[l],
                             conv_w_pw2[l], conv_b_pw2[l], post_ln_g[l, 0], post_ln_b[l, 0], alpha)
        else:
            j = l - n_a
            if j == 0:
                q, k, v = _qkv(x, mods_l, w_kv, attn_w_q[j])
            else:
                q = _qkv(x, mods_l, w_kv, attn_w_q[j])[0]
            lambda_init = 0.8 - 0.6 * math.exp(-0.3 * l)
            o = _attn(q, k, v, attn_lambda[j], attn_subln_g[j], rel_bias_table, lambda_init)
            x1, h2 = _attn_out(o, x, mods_l, attn_w_o[j], post_ln_g[l, 0], post_ln_b[l, 0], alpha)
        x = _moe_layer(x1.reshape(bsz * s, d), h2.reshape(bsz * s, d), mods_l, router_w[l], router_b[l],
                       expert_w_gate_up, expert_b_gate_up, expert_w_down, expert_b_down, l,
                       post_ln_g[l, 1], post_ln_b[l, 1], alpha, s).reshape(bsz, s, d)
    return x
```

```python
import functools
import math

import jax
import jax.numpy as jnp
from jax import lax
from jax.experimental import pallas as pl
from jax.experimental.pallas import tpu as pltpu

F32 = jnp.float32
BF16 = jnp.bfloat16
I32 = jnp.int32
HIGHEST = lax.Precision.HIGHEST

CHUNK = 64
CONV_WIDTH = 31
HEAD_DIM = 64
REL_BUCKETS = 32
REL_MAX_DIST = 128
TOP_K = 4
SWIGLU_LIMIT = 7.0
SWIGLU_ALPHA = 1.702
LN_EPS = 1e-5
MASK_VALUE = -1e30
LOG2_E = math.log2(math.e)

SUBLANES = 8
LANES = 128
VMEM_LIMIT_BYTES = 56 * 1024 * 1024

ADA_TN = 1024
SEQ_TILE = 256
PROJ_TILE = 512
CONV_HALO = 32
CONV_ROWS = 64
CONV_COLS = 256
MOE_TOKENS = 256
MOE_TILE = 512
Q_BLOCK = 128
CAST_ROWS = 128


def _params(*sem):
    return pltpu.CompilerParams(dimension_semantics=sem, vmem_limit_bytes=VMEM_LIMIT_BYTES)


def _layer_norm(x, g, b):
    mu = jnp.mean(x, axis=-1, keepdims=True)
    xc = x - mu
    var = jnp.mean(xc * xc, axis=-1, keepdims=True)
    return xc * lax.rsqrt(var + LN_EPS) * g + b


def _dot(a, b):
    return jnp.dot(a, b, preferred_element_type=F32)


def _ada_body(c_ref, w_ref, b_ref, o_ref):
    c = c_ref[...]
    cond = c * jax.nn.sigmoid(c)
    o_ref[...] = jnp.dot(cond, w_ref[...], preferred_element_type=F32, precision=HIGHEST) + b_ref[...]


def _ada(c, ada_w, ada_b):
    depth, d, n = ada_w.shape
    bsz = c.shape[0]
    tn = min(ADA_TN, n)
    return pl.pallas_call(
        _ada_body,
        grid=(depth, n // tn),
        in_specs=[
            pl.BlockSpec((bsz, d), lambda l, j: (0, 0)),
            pl.BlockSpec((None, d, tn), lambda l, j: (l, 0, j)),
            pl.BlockSpec((None, 1, tn), lambda l, j: (l, 0, j)),
        ],
        out_specs=pl.BlockSpec((None, bsz, tn), lambda l, j: (l, 0, j)),
        out_shape=jax.ShapeDtypeStruct((depth, bsz, n), F32),
        compiler_params=_params("parallel", "parallel"),
        name="ada",
    )(c, ada_w, ada_b.reshape(depth, 1, n))


def _conv_a_body(x_ref, mod_ref, w_ref, b_ref, u_ref):
    d = x_ref.shape[-1]
    h = (x_ref[...] * (1.0 + mod_ref[1:2, :]) + mod_ref[0:1, :]).astype(BF16)
    a = _dot(h, w_ref[:, :d]) + b_ref[:, :d]
    g = _dot(h, w_ref[:, d:]) + b_ref[:, d:]
    u_ref[...] = a * jax.nn.sigmoid(g)


def _conv_a(x, mods_l, w_pw1, b_pw1):
    bsz, s, d = x.shape
    ts = min(PROJ_TILE, s)
    return pl.pallas_call(
        _conv_a_body,
        grid=(bsz, s // ts),
        in_specs=[
            pl.BlockSpec((None, ts, d), lambda b, i: (b, i, 0)),
            pl.BlockSpec((None, 6, d), lambda b, i: (b, 0, 0)),
            pl.BlockSpec((d, 2 * d), lambda b, i: (0, 0)),
            pl.BlockSpec((1, 2 * d), lambda b, i: (0, 0)),
        ],
        out_specs=pl.BlockSpec((None, ts, d), lambda b, i: (b, i, 0)),
        out_shape=jax.ShapeDtypeStruct((bsz, s, d), F32),
        compiler_params=_params("parallel", "parallel"),
        name="conv_a",
    )(x, mods_l, w_pw1.astype(BF16), b_pw1.reshape(1, 2 * d))


def _residual_epilogue(x, y, mod_ref, lng_ref, lnb_ref, alpha, x1_ref):
    x1_ref[...] = _layer_norm(alpha * x + mod_ref[2:3, :] * y, lng_ref[...], lnb_ref[...])


def _moe_input(x1_ref, mod_ref):
    return x1_ref[...] * (1.0 + mod_ref[4:5, :]) + mod_ref[3:4, :]


def _conv_b_body(u_ref, halo_ref, x_ref, mod_ref, wdw_ref, bdw_ref, cg_ref, cb_ref, w2_ref, b2_ref,
                 lng_ref, lnb_ref, x1_ref, win_ref, v_ref, *, alpha):
    ts, d = u_ref.shape
    i = pl.program_id(1)
    halo = halo_ref[...]
    win_ref[0, 0:CONV_HALO, :] = jnp.where(i == 0, jnp.zeros_like(halo), halo)
    win_ref[0, CONV_HALO:, :] = u_ref[...]
    n_shift = ts + CONV_HALO - SUBLANES
    for b in range(1, SUBLANES):
        win_ref[b, 0:n_shift, :] = win_ref[0, b:b + n_shift, :]
    off = CONV_HALO - (CONV_WIDTH - 1)
    rows = min(CONV_ROWS, ts)
    cols = min(CONV_COLS, d)
    for c0 in range(0, d, cols):
        for r0 in range(0, ts, rows):
            acc = jnp.zeros((rows, cols), F32)
            for j in range(CONV_WIDTH):
                a, b = divmod(off + j, SUBLANES)
                r = r0 + a * SUBLANES
                acc = acc + wdw_ref[j:j + 1, c0:c0 + cols] * win_ref[b, r:r + rows, c0:c0 + cols]
            v_ref[r0:r0 + rows, c0:c0 + cols] = acc
    v = _layer_norm(v_ref[...] + bdw_ref[...], cg_ref[...], cb_ref[...])
    v = (v * jax.nn.sigmoid(v)).astype(BF16)
    y = _dot(v, w2_ref[...]) + b2_ref[...]
    _residual_epilogue(x_ref[...], y, mod_ref, lng_ref, lnb_ref, alpha, x1_ref)


def _conv_b(u, x, mods_l, w_dw, b_dw, cln_g, cln_b, w_pw2, b_pw2, ln_g, ln_b, alpha):
    bsz, s, d = x.shape
    ts = min(SEQ_TILE, s)
    hb = ts // CONV_HALO
    row = lambda a: a.reshape(1, d)
    tile = pl.BlockSpec((None, ts, d), lambda b, i: (b, i, 0))
    vec = pl.BlockSpec((1, d), lambda b, i: (0, 0))
    return pl.pallas_call(
        functools.partial(_conv_b_body, alpha=alpha),
        grid=(bsz, s // ts),
        in_specs=[
            tile,
            pl.BlockSpec((None, CONV_HALO, d), lambda b, i: (b, jnp.maximum(i * hb - 1, 0), 0)),
            tile,
            pl.BlockSpec((None, 6, d), lambda b, i: (b, 0, 0)),
            pl.BlockSpec((CONV_WIDTH, d), lambda b, i: (0, 0)),
            vec, vec, vec,
            pl.BlockSpec((d, d), lambda b, i: (0, 0)),
            vec, vec, vec,
        ],
        out_specs=tile,
        out_shape=jax.ShapeDtypeStruct((bsz, s, d), F32),
        scratch_shapes=[pltpu.VMEM((SUBLANES, ts + CONV_HALO, d), F32), pltpu.VMEM((ts, d), F32)],
        compiler_params=_params("parallel", "parallel"),
        name="conv_b",
    )(u, u, x, mods_l, w_dw, row(b_dw), row(cln_g), row(cln_b), w_pw2.astype(BF16), row(b_pw2),
      row(ln_g), row(ln_b))


def _qkv_body(x_ref, mod_ref, wkv_ref, wq_ref, q_ref, k_ref, v_ref):
    d = x_ref.shape[-1]
    x = x_ref[...]
    kv = _dot(x.astype(BF16), wkv_ref[...])
    k_ref[...] = kv[:, :d].astype(BF16)
    v_ref[...] = kv[:, d:].astype(BF16)
    h = (x * (1.0 + mod_ref[1:2, :]) + mod_ref[0:1, :]).astype(BF16)
    q_ref[...] = (_dot(h, wq_ref[...]) * (HEAD_DIM ** -0.5 * LOG2_E)).astype(BF16)


def _qkv(x, mods_l, w_kv, w_q):
    bsz, s, d = x.shape
    ts = min(PROJ_TILE, s)
    tile = pl.BlockSpec((None, ts, d), lambda b, i: (b, i, 0))
    return pl.pallas_call(
        _qkv_body,
        grid=(bsz, s // ts),
        in_specs=[
            tile,
            pl.BlockSpec((None, 6, d), lambda b, i: (b, 0, 0)),
            pl.BlockSpec((d, 2 * d), lambda b, i: (0, 0)),
            pl.BlockSpec((d, d), lambda b, i: (0, 0)),
        ],
        out_specs=[tile, tile, tile],
        out_shape=[jax.ShapeDtypeStruct((bsz, s, d), BF16)] * 3,
        compiler_params=_params("parallel", "parallel"),
        name="qkv",
    )(x, mods_l, w_kv.astype(BF16), w_q.astype(BF16))


def _t5_bucket(rel):
    nb = REL_BUCKETS // 2
    ret = jnp.where(rel > 0, nb, 0)
    n = jnp.abs(rel)
    max_exact = nb // 2
    large = max_exact + (jnp.log(jnp.maximum(n, 1).astype(F32) / max_exact)
                         / math.log(REL_MAX_DIST / max_exact) * (nb - max_exact)).astype(I32)
    large = jnp.minimum(large, nb - 1)
    return ret + jnp.where(n < max_exact, n, large)


def _bucket_strip(s):
    r = jnp.arange(Q_BLOCK, dtype=I32)[:, None]
    kp = jnp.arange(s, dtype=I32)[None, :] - (s - Q_BLOCK)
    bucket = _t5_bucket(kp - r)
    visible = jnp.floor_divide(kp, CHUNK) <= (r // CHUNK)
    return jnp.where(visible, bucket, REL_BUCKETS)


def _attn_body(tab_ref, q_ref, k_ref, v_ref, bkt_ref, lam_ref, sg_ref, o_ref, bias_ref, *, lambda_init, n_heads):
    h = pl.program_id(0)
    b = pl.program_id(1)
    s = q_ref.shape[0]

    @pl.when(b == 0)
    def _():
        bk = bkt_ref[...]
        acc = jnp.full(bk.shape, MASK_VALUE, F32)
        for r in range(REL_BUCKETS):
            acc = jnp.where(bk == r, tab_ref[r * n_heads + h] * LOG2_E, acc)
        bias_ref[0:Q_BLOCK, :] = acc
        bias_ref[Q_BLOCK:, :] = acc

    lp = lam_ref[...]
    lam = (jnp.exp(jnp.sum(lp[0:1, :] * lp[1:2, :], axis=-1, keepdims=True))
           - jnp.exp(jnp.sum(lp[2:3, :] * lp[3:4, :], axis=-1, keepdims=True)) + lambda_init)
    lane = lax.broadcasted_iota(I32, (Q_BLOCK, 2 * HEAD_DIM), 1)
    nt = (((1,), (1,)), ((), ()))

    def scores(i):
        n_keys = (i + 1) * Q_BLOCK
        q = q_ref[i * Q_BLOCK:(i + 1) * Q_BLOCK, :]
        qq = jnp.concatenate([jnp.where(lane < HEAD_DIM, q, jnp.zeros_like(q)),
                              jnp.where(lane >= HEAD_DIM, q, jnp.zeros_like(q))], axis=0)
        return lax.dot_general(qq, k_ref[0:n_keys, :], nt, preferred_element_type=F32) + bias_ref[:, s - n_keys:s]

    n_blocks = s // Q_BLOCK
    sc_next = scores(0)
    for i in range(n_blocks):
        n_keys = (i + 1) * Q_BLOCK
        sc = sc_next
        if i + 1 < n_blocks:
            sc_next = scores(i + 1)
        p = jnp.exp2(sc - jnp.max(sc, axis=-1, keepdims=True))
        denom = jnp.sum(p, axis=-1, keepdims=True)
        pv = _dot(p.astype(BF16), v_ref[0:n_keys, :])
        o = pv[:Q_BLOCK] * (1.0 / denom[:Q_BLOCK]) - pv[Q_BLOCK:] * (lam / denom[Q_BLOCK:])
        o = o * lax.rsqrt(jnp.mean(o * o, axis=-1, keepdims=True) + LN_EPS) * sg_ref[...]
        o_ref[i * Q_BLOCK:(i + 1) * Q_BLOCK, :] = (o * (1.0 - lambda_init)).astype(BF16)


def _attn(q, k, v, lam_p, subln_g, rel_table, lambda_init):
    bsz, s, d = q.shape
    hd2 = 2 * HEAD_DIM
    n_heads = d // hd2
    head = pl.BlockSpec((None, s, hd2), lambda h, b, tab: (b, 0, h))
    grid_spec = pltpu.PrefetchScalarGridSpec(
        num_scalar_prefetch=1,
        grid=(n_heads, bsz),
        in_specs=[
            head, head, head,
            pl.BlockSpec((Q_BLOCK, s), lambda h, b, tab: (0, 0)),
            pl.BlockSpec((4, HEAD_DIM), lambda h, b, tab: (0, 0)),
            pl.BlockSpec((1, hd2), lambda h, b, tab: (0, 0)),
        ],
        out_specs=head,
        scratch_shapes=[pltpu.VMEM((2 * Q_BLOCK, s), F32)],
    )
    return pl.pallas_call(
        functools.partial(_attn_body, lambda_init=lambda_init, n_heads=n_heads),
        grid_spec=grid_spec,
        out_shape=jax.ShapeDtypeStruct((bsz, s, d), BF16),
        compiler_params=_params("arbitrary", "arbitrary"),
        name="attn",
    )(rel_table.reshape(-1), q, k, v, _bucket_strip(s), lam_p, subln_g.reshape(1, hd2))


def _attn_out_body(o_ref, x_ref, mod_ref, wo_ref, lng_ref, lnb_ref, x1_ref, *, alpha):
    y = _dot(o_ref[...], wo_ref[...])
    _residual_epilogue(x_ref[...], y, mod_ref, lng_ref, lnb_ref, alpha, x1_ref)


def _attn_out(o, x, mods_l, w_o, ln_g, ln_b, alpha):
    bsz, s, d = x.shape
    ts = min(PROJ_TILE, s)
    tile = pl.BlockSpec((None, ts, d), lambda b, i: (b, i, 0))
    vec = pl.BlockSpec((1, d), lambda b, i: (0, 0))
    return pl.pallas_call(
        functools.partial(_attn_out_body, alpha=alpha),
        grid=(bsz, s // ts),
        in_specs=[tile, tile, pl.BlockSpec((None, 6, d), lambda b, i: (b, 0, 0)),
                  pl.BlockSpec((d, d), lambda b, i: (0, 0)), vec, vec],
        out_specs=tile,
        out_shape=jax.ShapeDtypeStruct((bsz, s, d), F32),
        compiler_params=_params("parallel", "parallel"),
        name="attn_out",
    )(o, x, mods_l, w_o.astype(BF16), ln_g.reshape(1, d), ln_b.reshape(1, d))


def _router_body(x1_ref, mod_ref, whi_ref, wlo_ref, b_ref, tri_ref, idx_ref, gate_ref, rank_ref, cnt_ref):
    h = _moe_input(x1_ref, mod_ref)
    h_hi = h.astype(BF16)
    h_lo = (h - h_hi.astype(F32)).astype(BF16)
    nt = (((1,), (1,)), ((), ()))
    logits = (lax.dot_general(whi_ref[...], h_hi, nt, preferred_element_type=F32)
              + (lax.dot_general(whi_ref[...], h_lo, nt, preferred_element_type=F32)
                 + lax.dot_general(wlo_ref[...], h_hi, nt, preferred_element_type=F32))) + b_ref[...]
    n_exp, tr = logits.shape
    eio = lax.broadcasted_iota(I32, (n_exp, tr), 0)
    work = logits
    vals, idxs = [], []
    for _ in range(TOP_K):
        m = jnp.max(work, axis=0, keepdims=True)
        am = jnp.min(jnp.where(work == m, eio, n_exp), axis=0, keepdims=True)
        vals.append(m)
        idxs.append(am)
        work = jnp.where(eio == am, -jnp.inf, work)
    ex = [jnp.exp(v - vals[0]) for v in vals]
    den = ex[0] + ex[1] + ex[2] + ex[3]
    onehot = jnp.zeros((n_exp, tr), F32)
    for k in range(TOP_K):
        onehot = onehot + (eio == idxs[k]).astype(F32)
    before = _dot(onehot.astype(BF16), tri_ref[...])
    for k in range(TOP_K):
        idx_ref[k:k + 1, :] = idxs[k]
        gate_ref[k:k + 1, :] = ex[k] / den
        rank_ref[k:k + 1, :] = jnp.sum(jnp.where(eio == idxs[k], before, 0.0), axis=0, keepdims=True).astype(I32)
    cnt_ref[...] = jnp.sum(onehot, axis=1, keepdims=True).astype(I32)


def _router(x1, mods_l, seq_len, w_r, b_r):
    t, d = x1.shape
    n_exp = w_r.shape[1]
    tr = min(MOE_TOKENS, t)
    per_seq = seq_len // tr
    wt = w_r.T
    wt_hi = wt.astype(BF16)
    wt_lo = (wt - wt_hi.astype(F32)).astype(BF16)
    pos = jnp.arange(tr, dtype=I32)
    tri = (pos[:, None] < pos[None, :]).astype(BF16)
    tok = pl.BlockSpec((TOP_K, tr), lambda i: (0, i))
    return pl.pallas_call(
        _router_body,
        grid=(t // tr,),
        in_specs=[
            pl.BlockSpec((tr, d), lambda i: (i, 0)),
            pl.BlockSpec((None, 6, d), lambda i: (i // per_seq, 0, 0)),
            pl.BlockSpec((n_exp, d), lambda i: (0, 0)),
            pl.BlockSpec((n_exp, d), lambda i: (0, 0)),
            pl.BlockSpec((n_exp, 1), lambda i: (0, 0)),
            pl.BlockSpec((tr, tr), lambda i: (0, 0)),
        ],
        out_specs=[tok, tok, tok, pl.BlockSpec((None, n_exp, 1), lambda i: (i, 0, 0))],
        out_shape=[jax.ShapeDtypeStruct((TOP_K, t), I32), jax.ShapeDtypeStruct((TOP_K, t), F32),
                   jax.ShapeDtypeStruct((TOP_K, t), I32), jax.ShapeDtypeStruct((t // tr, n_exp, 1), I32)],
        compiler_params=_params("parallel"),
        name="router",
    )(x1, mods_l, wt_hi, wt_lo, b_r.reshape(n_exp, 1), tri)


def _rows(ref, start, cnt):
    aligned = lambda v: v if isinstance(v, int) else pl.multiple_of(v, SUBLANES)
    return ref.at[pl.ds(aligned(start), aligned(cnt))]


def _segment_copy(src, src_row, dst, dst_row, cnt, sem):
    @pl.when(cnt > 0)
    def _():
        pltpu.make_async_copy(_rows(src, src_row, cnt), _rows(dst, dst_row, cnt), sem).start()


def _wait_rows(ref, cnt, sem):
    @pl.when(cnt > 0)
    def _():
        pltpu.make_async_copy(_rows(ref, 0, cnt), _rows(ref, 0, cnt), sem).wait()


def _zero_fill(zero_ref, xs_hbm, start, cnt, sem):
    zr = zero_ref.shape[0]
    n_full = cnt // zr

    def full(r, c):
        _segment_copy(zero_ref, 0, xs_hbm, start + r * zr, zr, sem)
        return c

    lax.fori_loop(0, n_full, full, 0)
    _segment_copy(zero_ref, 0, xs_hbm, start + n_full * zr, cnt - n_full * zr, sem)
    _wait_rows(xs_hbm, cnt, sem)


def _one_hot_hits(iota, pos_of):
    hit = iota == pos_of(0)
    for k in range(1, TOP_K):
        hit = hit | (iota == pos_of(k))
    return hit


def _dispatch_body(goff_ref, loff_ref, cp_ref, ltot_ref, pad_start_ref, pad_cnt_ref, x1_ref, mod_ref, lp_ref,
                   gate_ref, xs_hbm, xl_ref, zero_ref, sem, zsem, *, n_exp):
    j = pl.program_id(0)
    tl, d = x1_ref.shape
    lr = xl_ref.shape[1]

    @pl.when(j == 0)
    def _():
        zero_ref[...] = jnp.zeros_like(zero_ref)

        def per_range(e, carry):
            _zero_fill(zero_ref, xs_hbm, pad_start_ref[e], pad_cnt_ref[e], zsem)
            return carry
        lax.fori_loop(0, pad_start_ref.shape[0], per_range, 0)

    def start_segments(tile):
        def per_expert(e, carry):
            seg = tile * n_exp + e
            _segment_copy(xl_ref.at[tile % 2], loff_ref[seg], xs_hbm, goff_ref[seg], cp_ref[seg], sem.at[tile % 2])
            return carry
        lax.fori_loop(0, n_exp, per_expert, 0)

    def wait_segments(tile):
        _wait_rows(xs_hbm, ltot_ref[tile], sem.at[tile % 2])

    @pl.when(j >= 2)
    def _():
        wait_segments(j - 2)

    buf = xl_ref.at[j % 2]
    riota = lax.broadcasted_iota(I32, (lr, tl), 0)
    perm = _one_hot_hits(riota, lambda k: lp_ref[k:k + 1, :]).astype(F32).astype(BF16)
    buf[:, 0:d] = _dot(perm, _moe_input(x1_ref, mod_ref).astype(BF16))
    gsel = jnp.zeros((lr, tl), F32)
    for k in range(TOP_K):
        gsel = gsel + jnp.where(riota == lp_ref[k:k + 1, :], gate_ref[k:k + 1, :], 0.0)
    buf[:, d:] = jnp.broadcast_to(jnp.sum(gsel, axis=1, keepdims=True), (lr, LANES))
    start_segments(j)

    @pl.when(j == pl.num_programs(0) - 1)
    def _():
        @pl.when(j >= 1)
        def _():
            wait_segments(j - 1)
        wait_segments(j)


def _dispatch(x1, mods_l, seq_len, lp, gates, goff, loff, cp, ltot, pad_start, pad_cnt, n_rows):
    t, d = x1.shape
    tl = min(MOE_TOKENS, t)
    per_seq = seq_len // tl
    n_exp = goff.shape[0] // (t // tl)
    lr = TOP_K * tl + n_exp * SUBLANES
    tok = pl.BlockSpec((TOP_K, tl), lambda j, *_: (0, j))
    grid_spec = pltpu.PrefetchScalarGridSpec(
        num_scalar_prefetch=6,
        grid=(t // tl,),
        in_specs=[pl.BlockSpec((tl, d), lambda j, *_: (j, 0)),
                  pl.BlockSpec((None, 6, d), lambda j, *_: (j // per_seq, 0, 0)), tok, tok],
        out_specs=pl.BlockSpec(memory_space=pl.ANY),
        scratch_shapes=[pltpu.VMEM((2, lr, d + LANES), F32), pltpu.VMEM((MOE_TILE, d + LANES), F32),
                        pltpu.SemaphoreType.DMA((2,)), pltpu.SemaphoreType.DMA],
    )
    return pl.pallas_call(
        functools.partial(_dispatch_body, n_exp=n_exp),
        grid_spec=grid_spec,
        out_shape=jax.ShapeDtypeStruct((n_rows, d + LANES), F32),
        compiler_params=_params("arbitrary"),
        name="dispatch",
    )(goff, loff, cp, ltot, pad_start, pad_cnt, x1, mods_l, lp, gates)


def _experts_body(be_ref, bsrc_ref, nvalid_ref, xs_ref, wgu_ref, bgu_ref, wdn_ref, bdn_ref, ys_ref,
                  wgu_bf, wdn_bf):
    i = pl.program_id(0)
    e = be_ref[i]
    prev = be_ref[jnp.maximum(i - 1, 0)]
    f = wdn_ref.shape[0]

    @pl.when((i == 0) | (e != prev))
    def _():
        def cast(ref_in, ref_out):
            def step(r, c):
                rows = pl.ds(pl.multiple_of(r * CAST_ROWS, CAST_ROWS), CAST_ROWS)
                ref_out[rows, :] = ref_in[rows, :].astype(BF16)
                return c
            lax.fori_loop(0, ref_in.shape[0] // CAST_ROWS, step, 0)
        cast(wgu_ref, wgu_bf)
        cast(wdn_ref, wdn_bf)

    @pl.when(i < nvalid_ref[0])
    def _():
        d = wgu_ref.shape[0]
        x = xs_ref[:, 0:d].astype(BF16)
        gate = jnp.minimum(_dot(x, wgu_bf[:, :f]) + bgu_ref[:, :f], SWIGLU_LIMIT)
        lin = jnp.clip(_dot(x, wgu_bf[:, f:]) + bgu_ref[:, f:], -SWIGLU_LIMIT, SWIGLU_LIMIT)
        act = (gate * jax.nn.sigmoid(SWIGLU_ALPHA * gate) * (lin + 1.0)).astype(BF16)
        ys_ref[...] = (_dot(act, wdn_bf[...]) + bdn_ref[...]) * xs_ref[:, d:d + 1]

    @pl.when(i >= nvalid_ref[0])
    def _():
        ys_ref[...] = jnp.zeros_like(ys_ref)


def _experts(xs, block_e, block_src, n_valid, w_gu, b_gu, w_dn, b_dn, layer):
    n_rows, dx = xs.shape
    depth, n_exp, d, f2 = w_gu.shape
    f = f2 // 2
    tm = MOE_TILE
    grid_spec = pltpu.PrefetchScalarGridSpec(
        num_scalar_prefetch=3,
        grid=(n_rows // tm,),
        in_specs=[
            pl.BlockSpec((tm, dx), lambda i, be, bs, nv: (bs[i], 0)),
            pl.BlockSpec((None, None, d, f2), lambda i, be, bs, nv: (layer, be[i], 0, 0)),
            pl.BlockSpec((None, None, 1, f2), lambda i, be, bs, nv: (layer, be[i], 0, 0)),
            pl.BlockSpec((None, None, f, d), lambda i, be, bs, nv: (layer, be[i], 0, 0)),
            pl.BlockSpec((None, None, 1, d), lambda i, be, bs, nv: (layer, be[i], 0, 0)),
        ],
        out_specs=pl.BlockSpec((tm, d), lambda i, be, bs, nv: (i, 0)),
        scratch_shapes=[pltpu.VMEM((d, f2), BF16), pltpu.VMEM((f, d), BF16)],
    )
    return pl.pallas_call(
        _experts_body,
        grid_spec=grid_spec,
        out_shape=jax.ShapeDtypeStruct((n_rows, d), F32),
        compiler_params=_params("arbitrary"),
        name="experts",
    )(block_e, block_src, n_valid, xs, w_gu, b_gu.reshape(depth, n_exp, 1, f2), w_dn,
      b_dn.reshape(depth, n_exp, 1, d))


def _combine_body(goff_ref, loff_ref, cp_ref, ltot_ref, ys_hbm, x_ref, lp_ref, mod_ref, lng_ref, lnb_ref, o_ref,
                  yl_ref, sem, *, n_exp, alpha):
    j = pl.program_id(0)
    tl = x_ref.shape[0]
    lr, d = yl_ref.shape[1:]

    def fetch(tile):
        buf = yl_ref.at[tile % 2]

        def per_expert(e, carry):
            seg = tile * n_exp + e
            _segment_copy(ys_hbm, goff_ref[seg], buf, loff_ref[seg], cp_ref[seg], sem.at[tile % 2])
            return carry
        lax.fori_loop(0, n_exp, per_expert, 0)

        def zero_rows(r, carry):
            buf[pl.ds(pl.multiple_of(r * SUBLANES, SUBLANES), SUBLANES), :] = jnp.zeros((SUBLANES, d), F32)
            return carry
        lax.fori_loop(ltot_ref[tile] // SUBLANES, lr // SUBLANES, zero_rows, 0)

    @pl.when(j == 0)
    def _():
        fetch(j)

    @pl.when(j + 1 < pl.num_programs(0))
    def _():
        fetch(j + 1)

    liota = lax.broadcasted_iota(I32, (tl, lr), 1)
    pick = _one_hot_hits(liota, lambda k: lp_ref[:, k:k + 1]).astype(F32).astype(BF16)
    _wait_rows(ys_hbm, ltot_ref[j], sem.at[j % 2])
    y = _dot(pick, yl_ref[j % 2].astype(BF16))
    o_ref[...] = _layer_norm(alpha * x_ref[...] + mod_ref[5:6, :] * y, lng_ref[...], lnb_ref[...])


def _combine(ys, lp_t, goff, loff, cp, ltot, x1, mods_l, ln_g, ln_b, alpha, seq_len):
    t, d = x1.shape
    tl = min(MOE_TOKENS, t)
    n_exp = goff.shape[0] // (t // tl)
    lr = TOP_K * tl + n_exp * SUBLANES
    per_seq = seq_len // tl
    vec = pl.BlockSpec((1, d), lambda j, *_: (0, 0))
    grid_spec = pltpu.PrefetchScalarGridSpec(
        num_scalar_prefetch=4,
        grid=(t // tl,),
        in_specs=[
            pl.BlockSpec(memory_space=pl.ANY),
            pl.BlockSpec((tl, d), lambda j, *_: (j, 0)),
            pl.BlockSpec((tl, TOP_K), lambda j, *_: (j, 0)),
            pl.BlockSpec((None, 6, d), lambda j, *_: (j // per_seq, 0, 0)),
            vec, vec,
        ],
        out_specs=pl.BlockSpec((tl, d), lambda j, *_: (j, 0)),
        scratch_shapes=[pltpu.VMEM((2, lr, d), F32), pltpu.SemaphoreType.DMA((2,))],
    )
    return pl.pallas_call(
        functools.partial(_combine_body, n_exp=n_exp, alpha=alpha),
        grid_spec=grid_spec,
        out_shape=jax.ShapeDtypeStruct((t, d), F32),
        compiler_params=_params("arbitrary"),
        name="combine",
    )(goff, loff, cp, ltot, ys, x1, lp_t, mods_l, ln_g.reshape(1, d), ln_b.reshape(1, d))


def _moe_layer(x1, mods_l, w_r, b_r, w_gu, b_gu, w_dn, b_dn, layer, ln_g, ln_b, alpha, seq_len):
    t, d = x1.shape
    n_exp = w_r.shape[1]
    tm = MOE_TILE
    tl = min(MOE_TOKENS, t)
    n_tiles = t // tl
    idx, gates, rank, cnt = _router(x1, mods_l, seq_len, w_r, b_r)
    cp = (cnt[:, :, 0] + SUBLANES - 1) // SUBLANES * SUBLANES
    tot = jnp.sum(cp, axis=0)
    padded = (tot + tm - 1) // tm * tm
    pend = jnp.cumsum(padded)
    pstart = pend - padded
    goff = pstart[None, :] + jnp.cumsum(cp, axis=0) - cp
    loff = jnp.cumsum(cp, axis=1) - cp
    ltot = jnp.sum(cp, axis=1).astype(I32)
    is_e = idx[..., None] == jnp.arange(n_exp, dtype=I32)
    lp = jnp.sum(jnp.where(is_e, jnp.repeat(loff, tl, axis=0)[None], 0), axis=-1) + rank
    n_blocks = -(-(t * TOP_K + n_tiles * n_exp * (SUBLANES - 1)) // tm) + n_exp
    n_valid = pend[-1] // tm
    blk = jnp.arange(n_blocks, dtype=I32)
    block_src = jnp.minimum(blk, n_valid - 1)
    block_e = jnp.sum(pend[None, :] <= (block_src * tm)[:, None], axis=1).astype(I32)
    n_rows = n_blocks * tm
    pad_start = jnp.concatenate([pstart + tot, pend[-1:]]).astype(I32)
    pad_cnt = jnp.concatenate([padded - tot, n_rows - pend[-1:]]).astype(I32)
    flat = lambda a: a.reshape(-1).astype(I32)
    seg = (flat(goff), flat(loff), flat(cp), ltot)
    xs = _dispatch(x1, mods_l, seq_len, lp, gates, *seg, pad_start, pad_cnt, n_rows)
    ys = _experts(xs, block_e, block_src.astype(I32), n_valid.reshape(1).astype(I32), w_gu, b_gu, w_dn, b_dn, layer)
    return _combine(ys, lp.T, *seg, x1, mods_l, ln_g, ln_b, alpha, seq_len)


def kernel(x, c, ada_w, ada_b, post_ln_g, post_ln_b, conv_w_pw1, conv_b_pw1, conv_w_dw, conv_b_dw, conv_ln_g, conv_ln_b, conv_w_pw2, conv_b_pw2, w_kv, attn_w_q, attn_lambda, attn_subln_g, attn_w_o, rel_bias_table, router_w, router_b, expert_w_gate_up, expert_b_gate_up, expert_w_down, expert_b_down):
    bsz, s, d = x.shape
    depth = ada_w.shape[0]
    n_a = depth // 2
    alpha = (2 * depth) ** 0.25
    mods = _ada(c, ada_w, ada_b).reshape(depth, bsz, 6, d)
    q = k = v = None
    for l in range(depth):
        mods_l = mods[l]
        if l < n_a:
            u = _conv_a(x, mods_l, conv_w_pw1[l], conv_b_pw1[l])
            x1 = _conv_b(u, x, mods_l, conv_w_dw[l], conv_b_dw[l], conv_ln_g[l], conv_ln_b[l],
                         conv_w_pw2[l], conv_b_pw2[l], post_ln_g[l, 0], post_ln_b[l, 0], alpha)
        else:
            j = l - n_a
            if j == 0:
                q, k, v = _qkv(x, mods_l, w_kv, attn_w_q[j])
            else:
                q = _qkv(x, mods_l, w_kv, attn_w_q[j])[0]
            lambda_init = 0.8 - 0.6 * math.exp(-0.3 * l)
            o = _attn(q, k, v, attn_lambda[j], attn_subln_g[j], rel_bias_table, lambda_init)
            x1 = _attn_out(o, x, mods_l, attn_w_o[j], post_ln_g[l, 0], post_ln_b[l, 0], alpha)
        x = _moe_layer(x1.reshape(bsz * s, d), mods_l, router_w[l], router_b[l],
                       expert_w_gate_up, expert_b_gate_up, expert_w_down, expert_b_down, l,
                       post_ln_g[l, 1], post_ln_b[l, 1], alpha, s).reshape(bsz, s, d)
    return x
```

```python
import functools
import math

import jax
import jax.numpy as jnp
from jax import lax
from jax.experimental import pallas as pl
from jax.experimental.pallas import tpu as pltpu

F32 = jnp.float32
BF16 = jnp.bfloat16
I32 = jnp.int32
HIGHEST = lax.Precision.HIGHEST

CHUNK = 64
CONV_WIDTH = 31
HEAD_DIM = 64
REL_BUCKETS = 32
REL_MAX_DIST = 128
TOP_K = 4
SWIGLU_LIMIT = 7.0
SWIGLU_ALPHA = 1.702
LN_EPS = 1e-5
MASK_VALUE = -1e30
LOG2_E = math.log2(math.e)

SUBLANES = 8
LANES = 128
VMEM_LIMIT_BYTES = 56 * 1024 * 1024

ADA_TN = 1024
SEQ_TILE = 256
PROJ_TILE = 512
CONV_HALO = 32
CONV_ROWS = 64
CONV_COLS = 256
MOE_TOKENS = 256
MOE_TILE = 512
Q_BLOCK = 128
CAST_ROWS = 128


def _params(*sem):
    return pltpu.CompilerParams(dimension_semantics=sem, vmem_limit_bytes=VMEM_LIMIT_BYTES)


def _layer_norm(x, g, b):
    mu = jnp.mean(x, axis=-1, keepdims=True)
    xc = x - mu
    var = jnp.mean(xc * xc, axis=-1, keepdims=True)
    return xc * lax.rsqrt(var + LN_EPS) * g + b


def _dot(a, b):
    return jnp.dot(a, b, preferred_element_type=F32)


def _ada_body(c_ref, w_ref, b_ref, o_ref):
    c = c_ref[...]
    cond = c * jax.nn.sigmoid(c)
    o_ref[...] = jnp.dot(cond, w_ref[...], preferred_element_type=F32, precision=HIGHEST) + b_ref[...]


def _ada(c, ada_w, ada_b):
    depth, d, n = ada_w.shape
    bsz = c.shape[0]
    tn = min(ADA_TN, n)
    return pl.pallas_call(
        _ada_body,
        grid=(depth, n // tn),
        in_specs=[
            pl.BlockSpec((bsz, d), lambda l, j: (0, 0)),
            pl.BlockSpec((None, d, tn), lambda l, j: (l, 0, j)),
            pl.BlockSpec((None, 1, tn), lambda l, j: (l, 0, j)),
        ],
        out_specs=pl.BlockSpec((None, bsz, tn), lambda l, j: (l, 0, j)),
        out_shape=jax.ShapeDtypeStruct((depth, bsz, n), F32),
        compiler_params=_params("parallel", "parallel"),
        name="ada",
    )(c, ada_w, ada_b.reshape(depth, 1, n))


def _conv_a_body(x_ref, mod_ref, w_ref, b_ref, u_ref):
    d = x_ref.shape[-1]
    h = (x_ref[...] * (1.0 + mod_ref[1:2, :]) + mod_ref[0:1, :]).astype(BF16)
    a = _dot(h, w_ref[:, :d]) + b_ref[:, :d]
    g = _dot(h, w_ref[:, d:]) + b_ref[:, d:]
    u_ref[...] = a * jax.nn.sigmoid(g)


def _conv_a(x, mods_l, w_pw1, b_pw1):
    bsz, s, d = x.shape
    ts = min(PROJ_TILE, s)
    return pl.pallas_call(
        _conv_a_body,
        grid=(bsz, s // ts),
        in_specs=[
            pl.BlockSpec((None, ts, d), lambda b, i: (b, i, 0)),
            pl.BlockSpec((None, 6, d), lambda b, i: (b, 0, 0)),
            pl.BlockSpec((d, 2 * d), lambda b, i: (0, 0)),
            pl.BlockSpec((1, 2 * d), lambda b, i: (0, 0)),
        ],
        out_specs=pl.BlockSpec((None, ts, d), lambda b, i: (b, i, 0)),
        out_shape=jax.ShapeDtypeStruct((bsz, s, d), F32),
        compiler_params=_params("parallel", "parallel"),
        name="conv_a",
    )(x, mods_l, w_pw1.astype(BF16), b_pw1.reshape(1, 2 * d))


def _residual_epilogue(x, y, mod_ref, lng_ref, lnb_ref, alpha, x1_ref):
    x1_ref[...] = _layer_norm(alpha * x + mod_ref[2:3, :] * y, lng_ref[...], lnb_ref[...])


def _moe_input(x1_ref, mod_ref):
    return x1_ref[...] * (1.0 + mod_ref[4:5, :]) + mod_ref[3:4, :]


def _conv_b_body(u_ref, halo_ref, x_ref, mod_ref, wdw_ref, bdw_ref, cg_ref, cb_ref, w2_ref, b2_ref,
                 lng_ref, lnb_ref, x1_ref, win_ref, v_ref, *, alpha):
    ts, d = u_ref.shape
    i = pl.program_id(1)
    halo = halo_ref[...]
    win_ref[0, 0:CONV_HALO, :] = jnp.where(i == 0, jnp.zeros_like(halo), halo)
    win_ref[0, CONV_HALO:, :] = u_ref[...]
    n_shift = ts + CONV_HALO - SUBLANES
    for b in range(1, SUBLANES):
        win_ref[b, 0:n_shift, :] = win_ref[0, b:b + n_shift, :]
    off = CONV_HALO - (CONV_WIDTH - 1)
    rows = min(CONV_ROWS, ts)
    cols = min(CONV_COLS, d)
    for c0 in range(0, d, cols):
        for r0 in range(0, ts, rows):
            acc = jnp.zeros((rows, cols), F32)
            for j in range(CONV_WIDTH):
                a, b = divmod(off + j, SUBLANES)
                r = r0 + a * SUBLANES
                acc = acc + wdw_ref[j:j + 1, c0:c0 + cols] * win_ref[b, r:r + rows, c0:c0 + cols]
            v_ref[r0:r0 + rows, c0:c0 + cols] = acc
    v = _layer_norm(v_ref[...] + bdw_ref[...], cg_ref[...], cb_ref[...])
    v = (v * jax.nn.sigmoid(v)).astype(BF16)
    y = _dot(v, w2_ref[...]) + b2_ref[...]
    _residual_epilogue(x_ref[...], y, mod_ref, lng_ref, lnb_ref, alpha, x1_ref)


def _conv_b(u, x, mods_l, w_dw, b_dw, cln_g, cln_b, w_pw2, b_pw2, ln_g, ln_b, alpha):
    bsz, s, d = x.shape
    ts = min(SEQ_TILE, s)
    hb = ts // CONV_HALO
    row = lambda a: a.reshape(1, d)
    tile = pl.BlockSpec((None, ts, d), lambda b, i: (b, i, 0))
    vec = pl.BlockSpec((1, d), lambda b, i: (0, 0))
    return pl.pallas_call(
        functools.partial(_conv_b_body, alpha=alpha),
        grid=(bsz, s // ts),
        in_specs=[
            tile,
            pl.BlockSpec((None, CONV_HALO, d), lambda b, i: (b, jnp.maximum(i * hb - 1, 0), 0)),
            tile,
            pl.BlockSpec((None, 6, d), lambda b, i: (b, 0, 0)),
            pl.BlockSpec((CONV_WIDTH, d), lambda b, i: (0, 0)),
            vec, vec, vec,
            pl.BlockSpec((d, d), lambda b, i: (0, 0)),
            vec, vec, vec,
        ],
        out_specs=tile,
        out_shape=jax.ShapeDtypeStruct((bsz, s, d), F32),
        scratch_shapes=[pltpu.VMEM((SUBLANES, ts + CONV_HALO, d), F32), pltpu.VMEM((ts, d), F32)],
        compiler_params=_params("parallel", "parallel"),
        name="conv_b",
    )(u, u, x, mods_l, w_dw, row(b_dw), row(cln_g), row(cln_b), w_pw2.astype(BF16), row(b_pw2),
      row(ln_g), row(ln_b))


def _qkv_body(x_ref, mod_ref, wkv_ref, wq_ref, q_ref, k_ref, v_ref):
    d = x_ref.shape[-1]
    x = x_ref[...]
    kv = _dot(x.astype(BF16), wkv_ref[...])
    k_ref[...] = kv[:, :d].astype(BF16)
    v_ref[...] = kv[:, d:].astype(BF16)
    h = (x * (1.0 + mod_ref[1:2, :]) + mod_ref[0:1, :]).astype(BF16)
    q_ref[...] = (_dot(h, wq_ref[...]) * (HEAD_DIM ** -0.5 * LOG2_E)).astype(BF16)


def _qkv(x, mods_l, w_kv, w_q):
    bsz, s, d = x.shape
    ts = min(PROJ_TILE, s)
    tile = pl.BlockSpec((None, ts, d), lambda b, i: (b, i, 0))
    return pl.pallas_call(
        _qkv_body,
        grid=(bsz, s // ts),
        in_specs=[
            tile,
            pl.BlockSpec((None, 6, d), lambda b, i: (b, 0, 0)),
            pl.BlockSpec((d, 2 * d), lambda b, i: (0, 0)),
            pl.BlockSpec((d, d), lambda b, i: (0, 0)),
        ],
        out_specs=[tile, tile, tile],
        out_shape=[jax.ShapeDtypeStruct((bsz, s, d), BF16)] * 3,
        compiler_params=_params("parallel", "parallel"),
        name="qkv",
    )(x, mods_l, w_kv.astype(BF16), w_q.astype(BF16))


def _t5_bucket(rel):
    nb = REL_BUCKETS // 2
    ret = jnp.where(rel > 0, nb, 0)
    n = jnp.abs(rel)
    max_exact = nb // 2
    large = max_exact + (jnp.log(jnp.maximum(n, 1).astype(F32) / max_exact)
                         / math.log(REL_MAX_DIST / max_exact) * (nb - max_exact)).astype(I32)
    large = jnp.minimum(large, nb - 1)
    return ret + jnp.where(n < max_exact, n, large)


def _bucket_strip(s):
    r = jnp.arange(Q_BLOCK, dtype=I32)[:, None]
    kp = jnp.arange(s, dtype=I32)[None, :] - (s - Q_BLOCK)
    bucket = _t5_bucket(kp - r)
    visible = jnp.floor_divide(kp, CHUNK) <= (r // CHUNK)
    return jnp.where(visible, bucket, REL_BUCKETS)


def _attn_body(tab_ref, q_ref, k_ref, v_ref, bkt_ref, lam_ref, sg_ref, o_ref, bias_ref, *, lambda_init, n_heads):
    h = pl.program_id(0)
    b = pl.program_id(1)
    s = q_ref.shape[0]

    @pl.when(b == 0)
    def _():
        bk = bkt_ref[...]
        acc = jnp.full(bk.shape, MASK_VALUE, F32)
        for r in range(REL_BUCKETS):
            acc = jnp.where(bk == r, tab_ref[r * n_heads + h] * LOG2_E, acc)
        bias_ref[0:Q_BLOCK, :] = acc
        bias_ref[Q_BLOCK:, :] = acc

    lp = lam_ref[...]
    lam = (jnp.exp(jnp.sum(lp[0:1, :] * lp[1:2, :], axis=-1, keepdims=True))
           - jnp.exp(jnp.sum(lp[2:3, :] * lp[3:4, :], axis=-1, keepdims=True)) + lambda_init)
    lane = lax.broadcasted_iota(I32, (Q_BLOCK, 2 * HEAD_DIM), 1)
    nt = (((1,), (1,)), ((), ()))

    def scores(i):
        n_keys = (i + 1) * Q_BLOCK
        q = q_ref[i * Q_BLOCK:(i + 1) * Q_BLOCK, :]
        qq = jnp.concatenate([jnp.where(lane < HEAD_DIM, q, jnp.zeros_like(q)),
                              jnp.where(lane >= HEAD_DIM, q, jnp.zeros_like(q))], axis=0)
        return lax.dot_general(qq, k_ref[0:n_keys, :], nt, preferred_element_type=F32) + bias_ref[:, s - n_keys:s]

    n_blocks = s // Q_BLOCK
    sc_next = scores(0)
    for i in range(n_blocks):
        n_keys = (i + 1) * Q_BLOCK
        sc = sc_next
        if i + 1 < n_blocks:
            sc_next = scores(i + 1)
        p = jnp.exp2(sc - jnp.max(sc, axis=-1, keepdims=True))
        denom = jnp.sum(p, axis=-1, keepdims=True)
        pv = _dot(p.astype(BF16), v_ref[0:n_keys, :])
        o = pv[:Q_BLOCK] * (1.0 / denom[:Q_BLOCK]) - pv[Q_BLOCK:] * (lam / denom[Q_BLOCK:])
        o = o * lax.rsqrt(jnp.mean(o * o, axis=-1, keepdims=True) + LN_EPS) * sg_ref[...]
        o_ref[i * Q_BLOCK:(i + 1) * Q_BLOCK, :] = (o * (1.0 - lambda_init)).astype(BF16)


def _attn(q, k, v, lam_p, subln_g, rel_table, lambda_init):
    bsz, s, d = q.shape
    hd2 = 2 * HEAD_DIM
    n_heads = d // hd2
    head = pl.BlockSpec((None, s, hd2), lambda h, b, tab: (b, 0, h))
    grid_spec = pltpu.PrefetchScalarGridSpec(
        num_scalar_prefetch=1,
        grid=(n_heads, bsz),
        in_specs=[
            head, head, head,
            pl.BlockSpec((Q_BLOCK, s), lambda h, b, tab: (0, 0)),
            pl.BlockSpec((4, HEAD_DIM), lambda h, b, tab: (0, 0)),
            pl.BlockSpec((1, hd2), lambda h, b, tab: (0, 0)),
        ],
        out_specs=head,
        scratch_shapes=[pltpu.VMEM((2 * Q_BLOCK, s), F32)],
    )
    return pl.pallas_call(
        functools.partial(_attn_body, lambda_init=lambda_init, n_heads=n_heads),
        grid_spec=grid_spec,
        out_shape=jax.ShapeDtypeStruct((bsz, s, d), BF16),
        compiler_params=_params("arbitrary", "arbitrary"),
        name="attn",
    )(rel_table.reshape(-1), q, k, v, _bucket_strip(s), lam_p, subln_g.reshape(1, hd2))


def _attn_out_body(o_ref, x_ref, mod_ref, wo_ref, lng_ref, lnb_ref, x1_ref, *, alpha):
    y = _dot(o_ref[...], wo_ref[...])
    _residual_epilogue(x_ref[...], y, mod_ref, lng_ref, lnb_ref, alpha, x1_ref)


def _attn_out(o, x, mods_l, w_o, ln_g, ln_b, alpha):
    bsz, s, d = x.shape
    ts = min(PROJ_TILE, s)
    tile = pl.BlockSpec((None, ts, d), lambda b, i: (b, i, 0))
    vec = pl.BlockSpec((1, d), lambda b, i: (0, 0))
    return pl.pallas_call(
        functools.partial(_attn_out_body, alpha=alpha),
        grid=(bsz, s // ts),
        in_specs=[tile, tile, pl.BlockSpec((None, 6, d), lambda b, i: (b, 0, 0)),
                  pl.BlockSpec((d, d), lambda b, i: (0, 0)), vec, vec],
        out_specs=tile,
        out_shape=jax.ShapeDtypeStruct((bsz, s, d), F32),
        compiler_params=_params("parallel", "parallel"),
        name="attn_out",
    )(o, x, mods_l, w_o.astype(BF16), ln_g.reshape(1, d), ln_b.reshape(1, d))


def _router_body(x1_ref, mod_ref, whi_ref, wlo_ref, b_ref, tri_ref, idx_ref, gate_ref, rank_ref, cnt_ref):
    h = _moe_input(x1_ref, mod_ref)
    h_hi = h.astype(BF16)
    h_lo = (h - h_hi.astype(F32)).astype(BF16)
    nt = (((1,), (1,)), ((), ()))
    logits = (lax.dot_general(whi_ref[...], h_hi, nt, preferred_element_type=F32)
              + (lax.dot_general(whi_ref[...], h_lo, nt, preferred_element_type=F32)
                 + lax.dot_general(wlo_ref[...], h_hi, nt, preferred_element_type=F32))) + b_ref[...]
    n_exp, tr = logits.shape
    eio = lax.broadcasted_iota(I32, (n_exp, tr), 0)
    work = logits
    vals, idxs = [], []
    for _ in range(TOP_K):
        m = jnp.max(work, axis=0, keepdims=True)
        am = jnp.min(jnp.where(work == m, eio, n_exp), axis=0, keepdims=True)
        vals.append(m)
        idxs.append(am)
        work = jnp.where(eio == am, -jnp.inf, work)
    ex = [jnp.exp(v - vals[0]) for v in vals]
    den = ex[0] + ex[1] + ex[2] + ex[3]
    onehot = jnp.zeros((n_exp, tr), F32)
    for k in range(TOP_K):
        onehot = onehot + (eio == idxs[k]).astype(F32)
    before = _dot(onehot.astype(BF16), tri_ref[...])
    for k in range(TOP_K):
        idx_ref[k:k + 1, :] = idxs[k]
        gate_ref[k:k + 1, :] = ex[k] / den
        rank_ref[k:k + 1, :] = jnp.sum(jnp.where(eio == idxs[k], before, 0.0), axis=0, keepdims=True).astype(I32)
    cnt_ref[...] = jnp.sum(onehot, axis=1, keepdims=True).astype(I32)


def _router(x1, mods_l, seq_len, w_r, b_r):
    t, d = x1.shape
    n_exp = w_r.shape[1]
    tr = min(MOE_TOKENS, t)
    per_seq = seq_len // tr
    wt = w_r.T
    wt_hi = wt.astype(BF16)
    wt_lo = (wt - wt_hi.astype(F32)).astype(BF16)
    pos = jnp.arange(tr, dtype=I32)
    tri = (pos[:, None] < pos[None, :]).astype(BF16)
    tok = pl.BlockSpec((TOP_K, tr), lambda i: (0, i))
    return pl.pallas_call(
        _router_body,
        grid=(t // tr,),
        in_specs=[
            pl.BlockSpec((tr, d), lambda i: (i, 0)),
            pl.BlockSpec((None, 6, d), lambda i: (i // per_seq, 0, 0)),
            pl.BlockSpec((n_exp, d), lambda i: (0, 0)),
            pl.BlockSpec((n_exp, d), lambda i: (0, 0)),
            pl.BlockSpec((n_exp, 1), lambda i: (0, 0)),
            pl.BlockSpec((tr, tr), lambda i: (0, 0)),
        ],
        out_specs=[tok, tok, tok, pl.BlockSpec((None, n_exp, 1), lambda i: (i, 0, 0))],
        out_shape=[jax.ShapeDtypeStruct((TOP_K, t), I32), jax.ShapeDtypeStruct((TOP_K, t), F32),
                   jax.ShapeDtypeStruct((TOP_K, t), I32), jax.ShapeDtypeStruct((t // tr, n_exp, 1), I32)],
        compiler_params=_params("parallel"),
        name="router",
    )(x1, mods_l, wt_hi, wt_lo, b_r.reshape(n_exp, 1), tri)


def _rows(ref, start, cnt):
    aligned = lambda v: v if isinstance(v, int) else pl.multiple_of(v, SUBLANES)
    return ref.at[pl.ds(aligned(start), aligned(cnt))]


def _segment_copy(src, src_row, dst, dst_row, cnt, sem):
    @pl.when(cnt > 0)
    def _():
        pltpu.make_async_copy(_rows(src, src_row, cnt), _rows(dst, dst_row, cnt), sem).start()


def _wait_rows(ref, cnt, sem):
    @pl.when(cnt > 0)
    def _():
        pltpu.make_async_copy(_rows(ref, 0, cnt), _rows(ref, 0, cnt), sem).wait()


def _zero_fill(zero_ref, xs_hbm, start, cnt, sem):
    zr = zero_ref.shape[0]
    n_full = cnt // zr

    def full(r, c):
        _segment_copy(zero_ref, 0, xs_hbm, start + r * zr, zr, sem)
        return c

    lax.fori_loop(0, n_full, full, 0)
    _segment_copy(zero_ref, 0, xs_hbm, start + n_full * zr, cnt - n_full * zr, sem)
    _wait_rows(xs_hbm, cnt, sem)


def _one_hot_hits(iota, pos_of):
    hit = iota == pos_of(0)
    for k in range(1, TOP_K):
        hit = hit | (iota == pos_of(k))
    return hit


def _dispatch_body(goff_ref, loff_ref, cp_ref, ltot_ref, pad_start_ref, pad_cnt_ref, x1_ref, mod_ref, lp_ref,
                   gate_ref, xs_hbm, xl_ref, zero_ref, sem, zsem, *, n_exp):
    j = pl.program_id(0)
    tl, d = x1_ref.shape
    lr = xl_ref.shape[1]

    @pl.when(j == 0)
    def _():
        zero_ref[...] = jnp.zeros_like(zero_ref)

        def per_range(e, carry):
            _zero_fill(zero_ref, xs_hbm, pad_start_ref[e], pad_cnt_ref[e], zsem)
            return carry
        lax.fori_loop(0, pad_start_ref.shape[0], per_range, 0)

    def start_segments(tile):
        def per_expert(e, carry):
            seg = tile * n_exp + e
            _segment_copy(xl_ref.at[tile % 2], loff_ref[seg], xs_hbm, goff_ref[seg], cp_ref[seg], sem.at[tile % 2])
            return carry
        lax.fori_loop(0, n_exp, per_expert, 0)

    def wait_segments(tile):
        _wait_rows(xs_hbm, ltot_ref[tile], sem.at[tile % 2])

    @pl.when(j >= 2)
    def _():
        wait_segments(j - 2)

    buf = xl_ref.at[j % 2]
    riota = lax.broadcasted_iota(I32, (lr, tl), 0)
    perm = _one_hot_hits(riota, lambda k: lp_ref[k:k + 1, :]).astype(F32).astype(BF16)
    buf[:, 0:d] = _dot(perm, _moe_input(x1_ref, mod_ref).astype(BF16))
    gsel = jnp.zeros((lr, tl), F32)
    for k in range(TOP_K):
        gsel = gsel + jnp.where(riota == lp_ref[k:k + 1, :], gate_ref[k:k + 1, :], 0.0)
    buf[:, d:] = jnp.broadcast_to(jnp.sum(gsel, axis=1, keepdims=True), (lr, LANES))
    start_segments(j)

    @pl.when(j == pl.num_programs(0) - 1)
    def _():
        @pl.when(j >= 1)
        def _():
            wait_segments(j - 1)
        wait_segments(j)


def _dispatch(x1, mods_l, seq_len, lp, gates, goff, loff, cp, ltot, pad_start, pad_cnt, n_rows):
    t, d = x1.shape
    tl = min(MOE_TOKENS, t)
    per_seq = seq_len // tl
    n_exp = goff.shape[0] // (t // tl)
    lr = TOP_K * tl + n_exp * SUBLANES
    tok = pl.BlockSpec((TOP_K, tl), lambda j, *_: (0, j))
    grid_spec = pltpu.PrefetchScalarGridSpec(
        num_scalar_prefetch=6,
        grid=(t // tl,),
        in_specs=[pl.BlockSpec((tl, d), lambda j, *_: (j, 0)),
                  pl.BlockSpec((None, 6, d), lambda j, *_: (j // per_seq, 0, 0)), tok, tok],
        out_specs=pl.BlockSpec(memory_space=pl.ANY),
        scratch_shapes=[pltpu.VMEM((2, lr, d + LANES), F32), pltpu.VMEM((MOE_TILE, d + LANES), F32),
                        pltpu.SemaphoreType.DMA((2,)), pltpu.SemaphoreType.DMA],
    )
    return pl.pallas_call(
        functools.partial(_dispatch_body, n_exp=n_exp),
        grid_spec=grid_spec,
        out_shape=jax.ShapeDtypeStruct((n_rows, d + LANES), F32),
        compiler_params=_params("arbitrary"),
        name="dispatch",
    )(goff, loff, cp, ltot, pad_start, pad_cnt, x1, mods_l, lp, gates)


def _experts_body(be_ref, bsrc_ref, nvalid_ref, first_ref, slot_ref, next_ref, xs_ref, wgu_hbm, bgu_ref, wdn_hbm,
                  bdn_ref, ys_ref, wgu_f32, wdn_f32, wgu_bf, wdn_bf, sem_gu, sem_dn, *, layer):
    i = pl.program_id(0)
    e = be_ref[i]
    d, f2 = wgu_bf.shape
    f = f2 // 2

    def fetch(expert, slot, start):
        for hbm, buf, sem in ((wgu_hbm, wgu_f32, sem_gu), (wdn_hbm, wdn_f32, sem_dn)):
            dma = pltpu.make_async_copy(hbm.at[layer, expert], buf.at[slot], sem.at[slot])
            dma.start() if start else dma.wait()

    @pl.when(first_ref[i] == 1)
    def _():
        slot = slot_ref[i]

        @pl.when(i == 0)
        def _():
            fetch(e, slot, True)

        fetch(e, slot, False)

        @pl.when(next_ref[i] >= 0)
        def _():
            fetch(next_ref[i], 1 - slot, True)

        def cast(ref_in, ref_out):
            def step(r, c):
                rows = pl.ds(pl.multiple_of(r * CAST_ROWS, CAST_ROWS), CAST_ROWS)
                ref_out[rows, :] = ref_in[rows, :].astype(BF16)
                return c
            lax.fori_loop(0, ref_in.shape[0] // CAST_ROWS, step, 0)
        cast(wgu_f32.at[slot], wgu_bf)
        cast(wdn_f32.at[slot], wdn_bf)

    @pl.when(i < nvalid_ref[0])
    def _():
        x = xs_ref[:, 0:d].astype(BF16)
        gate = jnp.minimum(_dot(x, wgu_bf[:, :f]) + bgu_ref[:, :f], SWIGLU_LIMIT)
        lin = jnp.clip(_dot(x, wgu_bf[:, f:]) + bgu_ref[:, f:], -SWIGLU_LIMIT, SWIGLU_LIMIT)
        act = (gate * jax.nn.sigmoid(SWIGLU_ALPHA * gate) * (lin + 1.0)).astype(BF16)
        ys_ref[...] = (_dot(act, wdn_bf[...]) + bdn_ref[...]) * xs_ref[:, d:d + 1]

    @pl.when(i >= nvalid_ref[0])
    def _():
        ys_ref[...] = jnp.zeros_like(ys_ref)


def _experts(xs, block_e, block_src, n_valid, block_first, block_slot, block_next, w_gu, b_gu, w_dn, b_dn, layer):
    n_rows, dx = xs.shape
    depth, n_exp, d, f2 = w_gu.shape
    f = f2 // 2
    tm = MOE_TILE
    grid_spec = pltpu.PrefetchScalarGridSpec(
        num_scalar_prefetch=6,
        grid=(n_rows // tm,),
        in_specs=[
            pl.BlockSpec((tm, dx), lambda i, be, bs, *_: (bs[i], 0)),
            pl.BlockSpec(memory_space=pl.ANY),
            pl.BlockSpec((None, None, 1, f2), lambda i, be, *_: (layer, be[i], 0, 0)),
            pl.BlockSpec(memory_space=pl.ANY),
            pl.BlockSpec((None, None, 1, d), lambda i, be, *_: (layer, be[i], 0, 0)),
        ],
        out_specs=pl.BlockSpec((tm, d), lambda i, *_: (i, 0)),
        scratch_shapes=[pltpu.VMEM((2, d, f2), F32), pltpu.VMEM((2, f, d), F32),
                        pltpu.VMEM((d, f2), BF16), pltpu.VMEM((f, d), BF16),
                        pltpu.SemaphoreType.DMA((2,)), pltpu.SemaphoreType.DMA((2,))],
    )
    return pl.pallas_call(
        functools.partial(_experts_body, layer=layer),
        grid_spec=grid_spec,
        out_shape=jax.ShapeDtypeStruct((n_rows, d), F32),
        compiler_params=_params("arbitrary"),
        name="experts",
    )(block_e, block_src, n_valid, block_first, block_slot, block_next, xs, w_gu,
      b_gu.reshape(depth, n_exp, 1, f2), w_dn, b_dn.reshape(depth, n_exp, 1, d))


def _combine_body(goff_ref, loff_ref, cp_ref, ltot_ref, ys_hbm, x_ref, lp_ref, mod_ref, lng_ref, lnb_ref, o_ref,
                  yl_ref, sem, *, n_exp, alpha):
    j = pl.program_id(0)
    tl = x_ref.shape[0]
    lr, d = yl_ref.shape[1:]

    def fetch(tile):
        buf = yl_ref.at[tile % 2]

        def per_expert(e, carry):
            seg = tile * n_exp + e
            _segment_copy(ys_hbm, goff_ref[seg], buf, loff_ref[seg], cp_ref[seg], sem.at[tile % 2])
            return carry
        lax.fori_loop(0, n_exp, per_expert, 0)

        def zero_rows(r, carry):
            buf[pl.ds(pl.multiple_of(r * SUBLANES, SUBLANES), SUBLANES), :] = jnp.zeros((SUBLANES, d), F32)
            return carry
        lax.fori_loop(ltot_ref[tile] // SUBLANES, lr // SUBLANES, zero_rows, 0)

    @pl.when(j == 0)
    def _():
        fetch(j)

    @pl.when(j + 1 < pl.num_programs(0))
    def _():
        fetch(j + 1)

    liota = lax.broadcasted_iota(I32, (tl, lr), 1)
    pick = _one_hot_hits(liota, lambda k: lp_ref[:, k:k + 1]).astype(F32).astype(BF16)
    _wait_rows(ys_hbm, ltot_ref[j], sem.at[j % 2])
    y = _dot(pick, yl_ref[j % 2].astype(BF16))
    o_ref[...] = _layer_norm(alpha * x_ref[...] + mod_ref[5:6, :] * y, lng_ref[...], lnb_ref[...])


def _combine(ys, lp_t, goff, loff, cp, ltot, x1, mods_l, ln_g, ln_b, alpha, seq_len):
    t, d = x1.shape
    tl = min(MOE_TOKENS, t)
    n_exp = goff.shape[0] // (t // tl)
    lr = TOP_K * tl + n_exp * SUBLANES
    per_seq = seq_len // tl
    vec = pl.BlockSpec((1, d), lambda j, *_: (0, 0))
    grid_spec = pltpu.PrefetchScalarGridSpec(
        num_scalar_prefetch=4,
        grid=(t // tl,),
        in_specs=[
            pl.BlockSpec(memory_space=pl.ANY),
            pl.BlockSpec((tl, d), lambda j, *_: (j, 0)),
            pl.BlockSpec((tl, TOP_K), lambda j, *_: (j, 0)),
            pl.BlockSpec((None, 6, d), lambda j, *_: (j // per_seq, 0, 0)),
            vec, vec,
        ],
        out_specs=pl.BlockSpec((tl, d), lambda j, *_: (j, 0)),
        scratch_shapes=[pltpu.VMEM((2, lr, d), F32), pltpu.SemaphoreType.DMA((2,))],
    )
    return pl.pallas_call(
        functools.partial(_combine_body, n_exp=n_exp, alpha=alpha),
        grid_spec=grid_spec,
        out_shape=jax.ShapeDtypeStruct((t, d), F32),
        compiler_params=_params("arbitrary"),
        name="combine",
    )(goff, loff, cp, ltot, ys, x1, lp_t, mods_l, ln_g.reshape(1, d), ln_b.reshape(1, d))


def _moe_layer(x1, mods_l, w_r, b_r, w_gu, b_gu, w_dn, b_dn, layer, ln_g, ln_b, alpha, seq_len):
    t, d = x1.shape
    n_exp = w_r.shape[1]
    tm = MOE_TILE
    tl = min(MOE_TOKENS, t)
    n_tiles = t // tl
    idx, gates, rank, cnt = _router(x1, mods_l, seq_len, w_r, b_r)
    cp = (cnt[:, :, 0] + SUBLANES - 1) // SUBLANES * SUBLANES
    tot = jnp.sum(cp, axis=0)
    padded = (tot + tm - 1) // tm * tm
    pend = jnp.cumsum(padded)
    pstart = pend - padded
    goff = pstart[None, :] + jnp.cumsum(cp, axis=0) - cp
    loff = jnp.cumsum(cp, axis=1) - cp
    ltot = jnp.sum(cp, axis=1).astype(I32)
    is_e = idx[..., None] == jnp.arange(n_exp, dtype=I32)
    lp = jnp.sum(jnp.where(is_e, jnp.repeat(loff, tl, axis=0)[None], 0), axis=-1) + rank
    n_blocks = -(-(t * TOP_K + n_tiles * n_exp * (SUBLANES - 1)) // tm) + n_exp
    n_valid = pend[-1] // tm
    blk = jnp.arange(n_blocks, dtype=I32)
    block_src = jnp.minimum(blk, n_valid - 1)
    block_e = jnp.sum(pend[None, :] <= (block_src * tm)[:, None], axis=1).astype(I32)
    used = padded > 0
    experts = jnp.arange(n_exp, dtype=I32)
    next_used = jnp.flip(lax.cummin(jnp.flip(jnp.where(used, experts, n_exp))))
    next_used = jnp.concatenate([next_used[1:], jnp.full((1,), n_exp, I32)])
    next_used = jnp.where(next_used < n_exp, next_used, -1)
    slot_of = (jnp.cumsum(used.astype(I32)) - 1) % 2
    block_first = ((blk * tm == jnp.take(pstart, block_e)) & (blk < n_valid)).astype(I32)
    block_slot = jnp.take(slot_of, block_e).astype(I32)
    block_next = jnp.take(next_used, block_e).astype(I32)
    n_rows = n_blocks * tm
    pad_start = jnp.concatenate([pstart + tot, pend[-1:]]).astype(I32)
    pad_cnt = jnp.concatenate([padded - tot, n_rows - pend[-1:]]).astype(I32)
    flat = lambda a: a.reshape(-1).astype(I32)
    seg = (flat(goff), flat(loff), flat(cp), ltot)
    xs = _dispatch(x1, mods_l, seq_len, lp, gates, *seg, pad_start, pad_cnt, n_rows)
    ys = _experts(xs, block_e, block_src.astype(I32), n_valid.reshape(1).astype(I32), block_first, block_slot,
                  block_next, w_gu, b_gu, w_dn, b_dn, layer)
    return _combine(ys, lp.T, *seg, x1, mods_l, ln_g, ln_b, alpha, seq_len)


def kernel(x, c, ada_w, ada_b, post_ln_g, post_ln_b, conv_w_pw1, conv_b_pw1, conv_w_dw, conv_b_dw, conv_ln_g, conv_ln_b, conv_w_pw2, conv_b_pw2, w_kv, attn_w_q, attn_lambda, attn_subln_g, attn_w_o, rel_bias_table, router_w, router_b, expert_w_gate_up, expert_b_gate_up, expert_w_down, expert_b_down):
    bsz, s, d = x.shape
    depth = ada_w.shape[0]
    n_a = depth // 2
    alpha = (2 * depth) ** 0.25
    mods = _ada(c, ada_w, ada_b).reshape(depth, bsz, 6, d)
    q = k = v = None
    for l in range(depth):
        mods_l = mods[l]
        if l < n_a:
            u = _conv_a(x, mods_l, conv_w_pw1[l], conv_b_pw1[l])
            x1 = _conv_b(u, x, mods_l, conv_w_dw[l], conv_b_dw[l], conv_ln_g[l], conv_ln_b[l],
                         conv_w_pw2[l], conv_b_pw2[l], post_ln_g[l, 0], post_ln_b[l, 0], alpha)
        else:
            j = l - n_a
            if j == 0:
                q, k, v = _qkv(x, mods_l, w_kv, attn_w_q[j])
            else:
                q = _qkv(x, mods_l, w_kv, attn_w_q[j])[0]
            lambda_init = 0.8 - 0.6 * math.exp(-0.3 * l)
            o = _attn(q, k, v, attn_lambda[j], attn_subln_g[j], rel_bias_table, lambda_init)
            x1 = _attn_out(o, x, mods_l, attn_w_o[j], post_ln_g[l, 0], post_ln_b[l, 0], alpha)
        x = _moe_layer(x1.reshape(bsz * s, d), mods_l, router_w[l], router_b[l],
                       expert_w_gate_up, expert_b_gate_up, expert_w_down, expert_b_down, l,
                       post_ln_g[l, 1], post_ln_b[l, 1], alpha, s).reshape(bsz, s, d)
    return x
```

```python
import functools
import math

import jax
import jax.numpy as jnp
from jax import lax
from jax.experimental import pallas as pl
from jax.experimental.pallas import tpu as pltpu

F32 = jnp.float32
BF16 = jnp.bfloat16
I32 = jnp.int32
HIGHEST = lax.Precision.HIGHEST

CHUNK = 64
CONV_WIDTH = 31
HEAD_DIM = 64
REL_BUCKETS = 32
REL_MAX_DIST = 128
TOP_K = 4
SWIGLU_LIMIT = 7.0
SWIGLU_ALPHA = 1.702
LN_EPS = 1e-5
MASK_VALUE = -1e30
LOG2_E = math.log2(math.e)

SUBLANES = 8
LANES = 128
VMEM_LIMIT_BYTES = 56 * 1024 * 1024

ADA_TN = 1024
SEQ_TILE = 256
PROJ_TILE = 512
CONV_HALO = 32
CONV_ROWS = 64
CONV_COLS = 256
MOE_TOKENS = 256
ROUTER_TOKENS = 1024
MOE_TILE = 512
Q_BLOCK = 128
CAST_ROWS = 128


def _params(*sem):
    return pltpu.CompilerParams(dimension_semantics=sem, vmem_limit_bytes=VMEM_LIMIT_BYTES)


def _layer_norm(x, g, b):
    mu = jnp.mean(x, axis=-1, keepdims=True)
    xc = x - mu
    var = jnp.mean(xc * xc, axis=-1, keepdims=True)
    return xc * lax.rsqrt(var + LN_EPS) * g + b


def _dot(a, b):
    return jnp.dot(a, b, preferred_element_type=F32)


def _ada_body(c_ref, w_ref, b_ref, o_ref):
    c = c_ref[...]
    cond = c * jax.nn.sigmoid(c)
    o_ref[...] = jnp.dot(cond, w_ref[...], preferred_element_type=F32, precision=HIGHEST) + b_ref[...]


def _ada(c, ada_w, ada_b):
    depth, d, n = ada_w.shape
    bsz = c.shape[0]
    tn = min(ADA_TN, n)
    return pl.pallas_call(
        _ada_body,
        grid=(depth, n // tn),
        in_specs=[
            pl.BlockSpec((bsz, d), lambda l, j: (0, 0)),
            pl.BlockSpec((None, d, tn), lambda l, j: (l, 0, j)),
            pl.BlockSpec((None, 1, tn), lambda l, j: (l, 0, j)),
        ],
        out_specs=pl.BlockSpec((None, bsz, tn), lambda l, j: (l, 0, j)),
        out_shape=jax.ShapeDtypeStruct((depth, bsz, n), F32),
        compiler_params=_params("parallel", "parallel"),
        name="ada",
    )(c, ada_w, ada_b.reshape(depth, 1, n))


def _residual_epilogue(x, y, mod_ref, lng_ref, lnb_ref, alpha, x1_ref):
    x1_ref[...] = _layer_norm(alpha * x + mod_ref[2:3, :] * y, lng_ref[...], lnb_ref[...])


def _moe_input(x1_ref, mod_ref):
    return x1_ref[...] * (1.0 + mod_ref[4:5, :]) + mod_ref[3:4, :]


def _conv_body(x_ref, halo_ref, mod_ref, w1_ref, b1_ref, wdw_ref, bdw_ref, cg_ref, cb_ref, w2_ref, b2_ref,
               lng_ref, lnb_ref, x1_ref, win_ref, v_ref, *, alpha):
    ts, d = x_ref.shape
    i = pl.program_id(1)
    xw = jnp.concatenate([halo_ref[...], x_ref[...]], axis=0)
    h = (xw * (1.0 + mod_ref[1:2, :]) + mod_ref[0:1, :]).astype(BF16)
    rows = min(CONV_ROWS, ts)
    cols = min(CONV_COLS, d)

    def glu(c0):
        a = _dot(h, w1_ref[:, c0:c0 + cols]) + b1_ref[:, c0:c0 + cols]
        g = _dot(h, w1_ref[:, d + c0:d + c0 + cols]) + b1_ref[:, d + c0:d + c0 + cols]
        return a * jax.nn.sigmoid(g)

    off = CONV_HALO - (CONV_WIDTH - 1)
    n_shift = ts + CONV_HALO - SUBLANES
    in_seq = (lax.broadcasted_iota(I32, (ts + CONV_HALO, cols), 0) >= CONV_HALO) | (i > 0)
    u_next = glu(0)
    for c0 in range(0, d, cols):
        chan = slice(c0, c0 + cols)
        win_ref[0, :, chan] = jnp.where(in_seq, u_next, 0.0)
        if c0 + cols < d:
            u_next = glu(c0 + cols)
        for b in range(1, SUBLANES):
            win_ref[b, 0:n_shift, chan] = win_ref[0, b:b + n_shift, chan]
        for r0 in range(0, ts, rows):
            acc = jnp.zeros((rows, cols), F32)
            for j in range(CONV_WIDTH):
                a, b = divmod(off + j, SUBLANES)
                r = r0 + a * SUBLANES
                acc = acc + wdw_ref[j:j + 1, c0:c0 + cols] * win_ref[b, r:r + rows, c0:c0 + cols]
            v_ref[r0:r0 + rows, c0:c0 + cols] = acc
    v = _layer_norm(v_ref[...] + bdw_ref[...], cg_ref[...], cb_ref[...])
    v = (v * jax.nn.sigmoid(v)).astype(BF16)
    y = _dot(v, w2_ref[...]) + b2_ref[...]
    _residual_epilogue(x_ref[...], y, mod_ref, lng_ref, lnb_ref, alpha, x1_ref)


def _conv(x, mods_l, w_pw1, b_pw1, w_dw, b_dw, cln_g, cln_b, w_pw2, b_pw2, ln_g, ln_b, alpha):
    bsz, s, d = x.shape
    ts = min(SEQ_TILE, s)
    hb = ts // CONV_HALO
    row = lambda a: a.reshape(1, d)
    tile = pl.BlockSpec((None, ts, d), lambda b, i: (b, i, 0))
    vec = pl.BlockSpec((1, d), lambda b, i: (0, 0))
    return pl.pallas_call(
        functools.partial(_conv_body, alpha=alpha),
        grid=(bsz, s // ts),
        in_specs=[
            tile,
            pl.BlockSpec((None, CONV_HALO, d), lambda b, i: (b, jnp.maximum(i * hb - 1, 0), 0)),
            pl.BlockSpec((None, 6, d), lambda b, i: (b, 0, 0)),
            pl.BlockSpec((d, 2 * d), lambda b, i: (0, 0)),
            pl.BlockSpec((1, 2 * d), lambda b, i: (0, 0)),
            pl.BlockSpec((CONV_WIDTH, d), lambda b, i: (0, 0)),
            vec, vec, vec,
            pl.BlockSpec((d, d), lambda b, i: (0, 0)),
            vec, vec, vec,
        ],
        out_specs=tile,
        out_shape=jax.ShapeDtypeStruct((bsz, s, d), F32),
        scratch_shapes=[pltpu.VMEM((SUBLANES, ts + CONV_HALO, d), F32), pltpu.VMEM((ts, d), F32)],
        compiler_params=_params("parallel", "parallel"),
        name="conv",
    )(x, x, mods_l, w_pw1.astype(BF16), b_pw1.reshape(1, 2 * d), w_dw, row(b_dw), row(cln_g), row(cln_b),
      w_pw2.astype(BF16), row(b_pw2), row(ln_g), row(ln_b))


def _qkv_body(x_ref, mod_ref, wkv_ref, wq_ref, q_ref, k_ref, v_ref):
    d = x_ref.shape[-1]
    x = x_ref[...]
    kv = _dot(x.astype(BF16), wkv_ref[...])
    k_ref[...] = kv[:, :d].astype(BF16)
    v_ref[...] = kv[:, d:].astype(BF16)
    h = (x * (1.0 + mod_ref[1:2, :]) + mod_ref[0:1, :]).astype(BF16)
    q_ref[...] = (_dot(h, wq_ref[...]) * (HEAD_DIM ** -0.5 * LOG2_E)).astype(BF16)


def _qkv(x, mods_l, w_kv, w_q):
    bsz, s, d = x.shape
    ts = min(PROJ_TILE, s)
    tile = pl.BlockSpec((None, ts, d), lambda b, i: (b, i, 0))
    return pl.pallas_call(
        _qkv_body,
        grid=(bsz, s // ts),
        in_specs=[
            tile,
            pl.BlockSpec((None, 6, d), lambda b, i: (b, 0, 0)),
            pl.BlockSpec((d, 2 * d), lambda b, i: (0, 0)),
            pl.BlockSpec((d, d), lambda b, i: (0, 0)),
        ],
        out_specs=[tile, tile, tile],
        out_shape=[jax.ShapeDtypeStruct((bsz, s, d), BF16)] * 3,
        compiler_params=_params("parallel", "parallel"),
        name="qkv",
    )(x, mods_l, w_kv.astype(BF16), w_q.astype(BF16))


def _t5_bucket(rel):
    nb = REL_BUCKETS // 2
    ret = jnp.where(rel > 0, nb, 0)
    n = jnp.abs(rel)
    max_exact = nb // 2
    large = max_exact + (jnp.log(jnp.maximum(n, 1).astype(F32) / max_exact)
                         / math.log(REL_MAX_DIST / max_exact) * (nb - max_exact)).astype(I32)
    large = jnp.minimum(large, nb - 1)
    return ret + jnp.where(n < max_exact, n, large)


def _bucket_strip(s):
    r = jnp.arange(Q_BLOCK, dtype=I32)[:, None]
    kp = jnp.arange(s, dtype=I32)[None, :] - (s - Q_BLOCK)
    bucket = _t5_bucket(kp - r)
    visible = jnp.floor_divide(kp, CHUNK) <= (r // CHUNK)
    return jnp.where(visible, bucket, REL_BUCKETS)


def _attn_body(tab_ref, q_ref, k_ref, v_ref, bkt_ref, lam_ref, sg_ref, o_ref, bias_ref, *, lambda_init, n_heads):
    h = pl.program_id(0)
    b = pl.program_id(1)
    s = q_ref.shape[0]

    @pl.when(b == 0)
    def _():
        bk = bkt_ref[...]
        acc = jnp.full(bk.shape, MASK_VALUE, F32)
        for r in range(REL_BUCKETS):
            acc = jnp.where(bk == r, tab_ref[r * n_heads + h] * LOG2_E, acc)
        bias_ref[0:Q_BLOCK, :] = acc
        bias_ref[Q_BLOCK:, :] = acc

    lp = lam_ref[...]
    lam = (jnp.exp(jnp.sum(lp[0:1, :] * lp[1:2, :], axis=-1, keepdims=True))
           - jnp.exp(jnp.sum(lp[2:3, :] * lp[3:4, :], axis=-1, keepdims=True)) + lambda_init)
    lane = lax.broadcasted_iota(I32, (Q_BLOCK, 2 * HEAD_DIM), 1)
    nt = (((1,), (1,)), ((), ()))

    def scores(i):
        n_keys = (i + 1) * Q_BLOCK
        q = q_ref[i * Q_BLOCK:(i + 1) * Q_BLOCK, :]
        qq = jnp.concatenate([jnp.where(lane < HEAD_DIM, q, jnp.zeros_like(q)),
                              jnp.where(lane >= HEAD_DIM, q, jnp.zeros_like(q))], axis=0)
        return lax.dot_general(qq, k_ref[0:n_keys, :], nt, preferred_element_type=F32) + bias_ref[:, s - n_keys:s]

    n_blocks = s // Q_BLOCK
    sc_next = scores(0)
    for i in range(n_blocks):
        n_keys = (i + 1) * Q_BLOCK
        sc = sc_next
        if i + 1 < n_blocks:
            sc_next = scores(i + 1)
        p = jnp.exp2(sc - jnp.max(sc, axis=-1, keepdims=True))
        denom = jnp.sum(p, axis=-1, keepdims=True)
        pv = _dot(p.astype(BF16), v_ref[0:n_keys, :])
        o = pv[:Q_BLOCK] * (1.0 / denom[:Q_BLOCK]) - pv[Q_BLOCK:] * (lam / denom[Q_BLOCK:])
        o = o * lax.rsqrt(jnp.mean(o * o, axis=-1, keepdims=True) + LN_EPS) * sg_ref[...]
        o_ref[i * Q_BLOCK:(i + 1) * Q_BLOCK, :] = (o * (1.0 - lambda_init)).astype(BF16)


def _attn(q, k, v, lam_p, subln_g, rel_table, lambda_init):
    bsz, s, d = q.shape
    hd2 = 2 * HEAD_DIM
    n_heads = d // hd2
    head = pl.BlockSpec((None, s, hd2), lambda h, b, tab: (b, 0, h))
    grid_spec = pltpu.PrefetchScalarGridSpec(
        num_scalar_prefetch=1,
        grid=(n_heads, bsz),
        in_specs=[
            head, head, head,
            pl.BlockSpec((Q_BLOCK, s), lambda h, b, tab: (0, 0)),
            pl.BlockSpec((4, HEAD_DIM), lambda h, b, tab: (0, 0)),
            pl.BlockSpec((1, hd2), lambda h, b, tab: (0, 0)),
        ],
        out_specs=head,
        scratch_shapes=[pltpu.VMEM((2 * Q_BLOCK, s), F32)],
    )
    return pl.pallas_call(
        functools.partial(_attn_body, lambda_init=lambda_init, n_heads=n_heads),
        grid_spec=grid_spec,
        out_shape=jax.ShapeDtypeStruct((bsz, s, d), BF16),
        compiler_params=_params("arbitrary", "arbitrary"),
        name="attn",
    )(rel_table.reshape(-1), q, k, v, _bucket_strip(s), lam_p, subln_g.reshape(1, hd2))


def _attn_out_body(o_ref, x_ref, mod_ref, wo_ref, lng_ref, lnb_ref, x1_ref, *, alpha):
    y = _dot(o_ref[...], wo_ref[...])
    _residual_epilogue(x_ref[...], y, mod_ref, lng_ref, lnb_ref, alpha, x1_ref)


def _attn_out(o, x, mods_l, w_o, ln_g, ln_b, alpha):
    bsz, s, d = x.shape
    ts = min(PROJ_TILE, s)
    tile = pl.BlockSpec((None, ts, d), lambda b, i: (b, i, 0))
    vec = pl.BlockSpec((1, d), lambda b, i: (0, 0))
    return pl.pallas_call(
        functools.partial(_attn_out_body, alpha=alpha),
        grid=(bsz, s // ts),
        in_specs=[tile, tile, pl.BlockSpec((None, 6, d), lambda b, i: (b, 0, 0)),
                  pl.BlockSpec((d, d), lambda b, i: (0, 0)), vec, vec],
        out_specs=tile,
        out_shape=jax.ShapeDtypeStruct((bsz, s, d), F32),
        compiler_params=_params("parallel", "parallel"),
        name="attn_out",
    )(o, x, mods_l, w_o.astype(BF16), ln_g.reshape(1, d), ln_b.reshape(1, d))


def _router_body(x1_ref, mod_ref, whi_ref, wlo_ref, b_ref, tri_ref, idx_ref, gate_ref, rank_ref, cnt_ref):
    h = _moe_input(x1_ref, mod_ref)
    h_hi = h.astype(BF16)
    h_lo = (h - h_hi.astype(F32)).astype(BF16)
    nt = (((1,), (1,)), ((), ()))
    logits = (lax.dot_general(whi_ref[...], h_hi, nt, preferred_element_type=F32)
              + (lax.dot_general(whi_ref[...], h_lo, nt, preferred_element_type=F32)
                 + lax.dot_general(wlo_ref[...], h_hi, nt, preferred_element_type=F32))) + b_ref[...]
    n_exp, tr = logits.shape
    eio = lax.broadcasted_iota(I32, (n_exp, tr), 0)
    work = logits
    vals, idxs = [], []
    for _ in range(TOP_K):
        m = jnp.max(work, axis=0, keepdims=True)
        am = jnp.min(jnp.where(work == m, eio, n_exp), axis=0, keepdims=True)
        vals.append(m)
        idxs.append(am)
        work = jnp.where(eio == am, -jnp.inf, work)
    ex = [jnp.exp(v - vals[0]) for v in vals]
    den = ex[0] + ex[1] + ex[2] + ex[3]
    onehot = jnp.zeros((n_exp, tr), F32)
    for k in range(TOP_K):
        onehot = onehot + (eio == idxs[k]).astype(F32)
    before = _dot(onehot.astype(BF16), tri_ref[...])
    for k in range(TOP_K):
        idx_ref[k:k + 1, :] = idxs[k]
        gate_ref[k:k + 1, :] = ex[k] / den
        rank_ref[k:k + 1, :] = jnp.sum(jnp.where(eio == idxs[k], before, 0.0), axis=0, keepdims=True).astype(I32)
    tl = tr // cnt_ref.shape[0]
    for j in range(cnt_ref.shape[0]):
        cnt_ref[j] = jnp.sum(onehot[:, j * tl:(j + 1) * tl], axis=1, keepdims=True).astype(I32)


def _router(x1, mods_l, seq_len, w_r, b_r):
    t, d = x1.shape
    n_exp = w_r.shape[1]
    tl = min(MOE_TOKENS, t)
    tr = min(ROUTER_TOKENS, seq_len)
    per_seq = seq_len // tr
    wt = w_r.T
    wt_hi = wt.astype(BF16)
    wt_lo = (wt - wt_hi.astype(F32)).astype(BF16)
    pos = jnp.arange(tr, dtype=I32)
    tri = ((pos[:, None] < pos[None, :]) & (pos[:, None] // tl == pos[None, :] // tl)).astype(BF16)
    tok = pl.BlockSpec((TOP_K, tr), lambda i: (0, i))
    return pl.pallas_call(
        _router_body,
        grid=(t // tr,),
        in_specs=[
            pl.BlockSpec((tr, d), lambda i: (i, 0)),
            pl.BlockSpec((None, 6, d), lambda i: (i // per_seq, 0, 0)),
            pl.BlockSpec((n_exp, d), lambda i: (0, 0)),
            pl.BlockSpec((n_exp, d), lambda i: (0, 0)),
            pl.BlockSpec((n_exp, 1), lambda i: (0, 0)),
            pl.BlockSpec((tr, tr), lambda i: (0, 0)),
        ],
        out_specs=[tok, tok, tok, pl.BlockSpec((tr // tl, n_exp, 1), lambda i: (i, 0, 0))],
        out_shape=[jax.ShapeDtypeStruct((TOP_K, t), I32), jax.ShapeDtypeStruct((TOP_K, t), F32),
                   jax.ShapeDtypeStruct((TOP_K, t), I32), jax.ShapeDtypeStruct((t // tl, n_exp, 1), I32)],
        compiler_params=_params("parallel"),
        name="router",
    )(x1, mods_l, wt_hi, wt_lo, b_r.reshape(n_exp, 1), tri)


def _rows(ref, start, cnt):
    aligned = lambda v: v if isinstance(v, int) else pl.multiple_of(v, SUBLANES)
    return ref.at[pl.ds(aligned(start), aligned(cnt))]


def _segment_copy(src, src_row, dst, dst_row, cnt, sem):
    @pl.when(cnt > 0)
    def _():
        pltpu.make_async_copy(_rows(src, src_row, cnt), _rows(dst, dst_row, cnt), sem).start()


def _wait_rows(ref, cnt, sem):
    @pl.when(cnt > 0)
    def _():
        pltpu.make_async_copy(_rows(ref, 0, cnt), _rows(ref, 0, cnt), sem).wait()


def _zero_fill(zero_ref, xs_hbm, start, cnt, sem):
    zr = zero_ref.shape[0]
    n_full = cnt // zr

    def full(r, c):
        _segment_copy(zero_ref, 0, xs_hbm, start + r * zr, zr, sem)
        return c

    lax.fori_loop(0, n_full, full, 0)
    _segment_copy(zero_ref, 0, xs_hbm, start + n_full * zr, cnt - n_full * zr, sem)
    _wait_rows(xs_hbm, cnt, sem)


def _one_hot_hits(iota, pos_of):
    hit = iota == pos_of(0)
    for k in range(1, TOP_K):
        hit = hit | (iota == pos_of(k))
    return hit


def _dispatch_body(goff_ref, loff_ref, cp_ref, ltot_ref, pad_start_ref, pad_cnt_ref, x1_ref, mod_ref, lp_ref,
                   gate_ref, xs_hbm, xl_ref, zero_ref, sem, zsem, *, n_exp):
    j = pl.program_id(0)
    tl, d = x1_ref.shape
    lr = xl_ref.shape[1]

    @pl.when(j == 0)
    def _():
        zero_ref[...] = jnp.zeros_like(zero_ref)

        def per_range(e, carry):
            _zero_fill(zero_ref, xs_hbm, pad_start_ref[e], pad_cnt_ref[e], zsem)
            return carry
        lax.fori_loop(0, pad_start_ref.shape[0], per_range, 0)

    def start_segments(tile):
        def per_expert(e, carry):
            seg = tile * n_exp + e
            _segment_copy(xl_ref.at[tile % 2], loff_ref[seg], xs_hbm, goff_ref[seg], cp_ref[seg], sem.at[tile % 2])
            return carry
        lax.fori_loop(0, n_exp, per_expert, 0)

    def wait_segments(tile):
        _wait_rows(xs_hbm, ltot_ref[tile], sem.at[tile % 2])

    @pl.when(j >= 2)
    def _():
        wait_segments(j - 2)

    buf = xl_ref.at[j % 2]
    riota = lax.broadcasted_iota(I32, (lr, tl), 0)
    perm = _one_hot_hits(riota, lambda k: lp_ref[k:k + 1, :]).astype(F32).astype(BF16)
    buf[:, 0:d] = _dot(perm, _moe_input(x1_ref, mod_ref).astype(BF16))
    gsel = jnp.zeros((lr, tl), F32)
    for k in range(TOP_K):
        gsel = gsel + jnp.where(riota == lp_ref[k:k + 1, :], gate_ref[k:k + 1, :], 0.0)
    buf[:, d:] = jnp.broadcast_to(jnp.sum(gsel, axis=1, keepdims=True), (lr, LANES))
    start_segments(j)

    @pl.when(j == pl.num_programs(0) - 1)
    def _():
        @pl.when(j >= 1)
        def _():
            wait_segments(j - 1)
        wait_segments(j)


def _dispatch(x1, mods_l, seq_len, lp, gates, goff, loff, cp, ltot, pad_start, pad_cnt, n_rows):
    t, d = x1.shape
    tl = min(MOE_TOKENS, t)
    per_seq = seq_len // tl
    n_exp = goff.shape[0] // (t // tl)
    lr = TOP_K * tl + n_exp * SUBLANES
    tok = pl.BlockSpec((TOP_K, tl), lambda j, *_: (0, j))
    grid_spec = pltpu.PrefetchScalarGridSpec(
        num_scalar_prefetch=6,
        grid=(t // tl,),
        in_specs=[pl.BlockSpec((tl, d), lambda j, *_: (j, 0)),
                  pl.BlockSpec((None, 6, d), lambda j, *_: (j // per_seq, 0, 0)), tok, tok],
        out_specs=pl.BlockSpec(memory_space=pl.ANY),
        scratch_shapes=[pltpu.VMEM((2, lr, d + LANES), F32), pltpu.VMEM((MOE_TILE, d + LANES), F32),
                        pltpu.SemaphoreType.DMA((2,)), pltpu.SemaphoreType.DMA],
    )
    return pl.pallas_call(
        functools.partial(_dispatch_body, n_exp=n_exp),
        grid_spec=grid_spec,
        out_shape=jax.ShapeDtypeStruct((n_rows, d + LANES), F32),
        compiler_params=_params("arbitrary"),
        name="dispatch",
    )(goff, loff, cp, ltot, pad_start, pad_cnt, x1, mods_l, lp, gates)


def _experts_body(be_ref, bsrc_ref, nvalid_ref, first_ref, slot_ref, next_ref, xs_ref, wgu_hbm, bgu_ref, wdn_hbm,
                  bdn_ref, ys_ref, wgu_f32, wdn_f32, wgu_bf, wdn_bf, sem_gu, sem_dn, *, layer):
    i = pl.program_id(0)
    e = be_ref[i]
    d, f2 = wgu_bf.shape
    f = f2 // 2

    def fetch(expert, slot, start):
        for hbm, buf, sem in ((wgu_hbm, wgu_f32, sem_gu), (wdn_hbm, wdn_f32, sem_dn)):
            dma = pltpu.make_async_copy(hbm.at[layer, expert], buf.at[slot], sem.at[slot])
            dma.start() if start else dma.wait()

    @pl.when(first_ref[i] == 1)
    def _():
        slot = slot_ref[i]

        @pl.when(i == 0)
        def _():
            fetch(e, slot, True)

        fetch(e, slot, False)

        @pl.when(next_ref[i] >= 0)
        def _():
            fetch(next_ref[i], 1 - slot, True)

        def cast(ref_in, ref_out):
            def step(r, c):
                rows = pl.ds(pl.multiple_of(r * CAST_ROWS, CAST_ROWS), CAST_ROWS)
                ref_out[rows, :] = ref_in[rows, :].astype(BF16)
                return c
            lax.fori_loop(0, ref_in.shape[0] // CAST_ROWS, step, 0)
        cast(wgu_f32.at[slot], wgu_bf)
        cast(wdn_f32.at[slot], wdn_bf)

    @pl.when(i < nvalid_ref[0])
    def _():
        x = xs_ref[:, 0:d].astype(BF16)
        gate = jnp.minimum(_dot(x, wgu_bf[:, :f]) + bgu_ref[:, :f], SWIGLU_LIMIT)
        lin = jnp.clip(_dot(x, wgu_bf[:, f:]) + bgu_ref[:, f:], -SWIGLU_LIMIT, SWIGLU_LIMIT)
        act = (gate * jax.nn.sigmoid(SWIGLU_ALPHA * gate) * (lin + 1.0)).astype(BF16)
        ys_ref[...] = (_dot(act, wdn_bf[...]) + bdn_ref[...]) * xs_ref[:, d:d + 1]

    @pl.when(i >= nvalid_ref[0])
    def _():
        ys_ref[...] = jnp.zeros_like(ys_ref)


def _experts(xs, block_e, block_src, n_valid, block_first, block_slot, block_next, w_gu, b_gu, w_dn, b_dn, layer):
    n_rows, dx = xs.shape
    depth, n_exp, d, f2 = w_gu.shape
    f = f2 // 2
    tm = MOE_TILE
    grid_spec = pltpu.PrefetchScalarGridSpec(
        num_scalar_prefetch=6,
        grid=(n_rows // tm,),
        in_specs=[
            pl.BlockSpec((tm, dx), lambda i, be, bs, *_: (bs[i], 0)),
            pl.BlockSpec(memory_space=pl.ANY),
            pl.BlockSpec((None, None, 1, f2), lambda i, be, *_: (layer, be[i], 0, 0)),
            pl.BlockSpec(memory_space=pl.ANY),
            pl.BlockSpec((None, None, 1, d), lambda i, be, *_: (layer, be[i], 0, 0)),
        ],
        out_specs=pl.BlockSpec((tm, d), lambda i, *_: (i, 0)),
        scratch_shapes=[pltpu.VMEM((2, d, f2), F32), pltpu.VMEM((2, f, d), F32),
                        pltpu.VMEM((d, f2), BF16), pltpu.VMEM((f, d), BF16),
                        pltpu.SemaphoreType.DMA((2,)), pltpu.SemaphoreType.DMA((2,))],
    )
    return pl.pallas_call(
        functools.partial(_experts_body, layer=layer),
        grid_spec=grid_spec,
        out_shape=jax.ShapeDtypeStruct((n_rows, d), F32),
        compiler_params=_params("arbitrary"),
        name="experts",
    )(block_e, block_src, n_valid, block_first, block_slot, block_next, xs, w_gu,
      b_gu.reshape(depth, n_exp, 1, f2), w_dn, b_dn.reshape(depth, n_exp, 1, d))


def _combine_body(goff_ref, loff_ref, cp_ref, ltot_ref, ys_hbm, x_ref, lp_ref, mod_ref, lng_ref, lnb_ref, o_ref,
                  yl_ref, sem, *, n_exp, alpha):
    j = pl.program_id(0)
    tl = x_ref.shape[0]
    lr, d = yl_ref.shape[1:]

    def fetch(tile):
        buf = yl_ref.at[tile % 2]

        def per_expert(e, carry):
            seg = tile * n_exp + e
            _segment_copy(ys_hbm, goff_ref[seg], buf, loff_ref[seg], cp_ref[seg], sem.at[tile % 2])
            return carry
        lax.fori_loop(0, n_exp, per_expert, 0)

        def zero_rows(r, carry):
            buf[pl.ds(pl.multiple_of(r * SUBLANES, SUBLANES), SUBLANES), :] = jnp.zeros((SUBLANES, d), F32)
            return carry
        lax.fori_loop(ltot_ref[tile] // SUBLANES, lr // SUBLANES, zero_rows, 0)

    @pl.when(j == 0)
    def _():
        fetch(j)

    @pl.when(j + 1 < pl.num_programs(0))
    def _():
        fetch(j + 1)

    liota = lax.broadcasted_iota(I32, (tl, lr), 1)
    pick = _one_hot_hits(liota, lambda k: lp_ref[:, k:k + 1]).astype(F32).astype(BF16)
    _wait_rows(ys_hbm, ltot_ref[j], sem.at[j % 2])
    y = _dot(pick, yl_ref[j % 2].astype(BF16))
    o_ref[...] = _layer_norm(alpha * x_ref[...] + mod_ref[5:6, :] * y, lng_ref[...], lnb_ref[...])


def _combine(ys, lp_t, goff, loff, cp, ltot, x1, mods_l, ln_g, ln_b, alpha, seq_len):
    t, d = x1.shape
    tl = min(MOE_TOKENS, t)
    n_exp = goff.shape[0] // (t // tl)
    lr = TOP_K * tl + n_exp * SUBLANES
    per_seq = seq_len // tl
    vec = pl.BlockSpec((1, d), lambda j, *_: (0, 0))
    grid_spec = pltpu.PrefetchScalarGridSpec(
        num_scalar_prefetch=4,
        grid=(t // tl,),
        in_specs=[
            pl.BlockSpec(memory_space=pl.ANY),
            pl.BlockSpec((tl, d), lambda j, *_: (j, 0)),
            pl.BlockSpec((tl, TOP_K), lambda j, *_: (j, 0)),
            pl.BlockSpec((None, 6, d), lambda j, *_: (j // per_seq, 0, 0)),
            vec, vec,
        ],
        out_specs=pl.BlockSpec((tl, d), lambda j, *_: (j, 0)),
        scratch_shapes=[pltpu.VMEM((2, lr, d), F32), pltpu.SemaphoreType.DMA((2,))],
    )
    return pl.pallas_call(
        functools.partial(_combine_body, n_exp=n_exp, alpha=alpha),
        grid_spec=grid_spec,
        out_shape=jax.ShapeDtypeStruct((t, d), F32),
        compiler_params=_params("arbitrary"),
        name="combine",
    )(goff, loff, cp, ltot, ys, x1, lp_t, mods_l, ln_g.reshape(1, d), ln_b.reshape(1, d))


def _moe_layer(x1, mods_l, w_r, b_r, w_gu, b_gu, w_dn, b_dn, layer, ln_g, ln_b, alpha, seq_len):
    t, d = x1.shape
    n_exp = w_r.shape[1]
    tm = MOE_TILE
    tl = min(MOE_TOKENS, t)
    n_tiles = t // tl
    idx, gates, rank, cnt = _router(x1, mods_l, seq_len, w_r, b_r)
    cp = (cnt[:, :, 0] + SUBLANES - 1) // SUBLANES * SUBLANES
    tot = jnp.sum(cp, axis=0)
    padded = (tot + tm - 1) // tm * tm
    pend = jnp.cumsum(padded)
    pstart = pend - padded
    goff = pstart[None, :] + jnp.cumsum(cp, axis=0) - cp
    loff = jnp.cumsum(cp, axis=1) - cp
    ltot = jnp.sum(cp, axis=1).astype(I32)
    is_e = idx[..., None] == jnp.arange(n_exp, dtype=I32)
    lp = jnp.sum(jnp.where(is_e, jnp.repeat(loff, tl, axis=0)[None], 0), axis=-1) + rank
    n_blocks = -(-(t * TOP_K + n_tiles * n_exp * (SUBLANES - 1)) // tm) + n_exp
    n_valid = pend[-1] // tm
    blk = jnp.arange(n_blocks, dtype=I32)
    block_src = jnp.minimum(blk, n_valid - 1)
    block_e = jnp.sum(pend[None, :] <= (block_src * tm)[:, None], axis=1).astype(I32)
    used = padded > 0
    experts = jnp.arange(n_exp, dtype=I32)
    next_used = jnp.flip(lax.cummin(jnp.flip(jnp.where(used, experts, n_exp))))
    next_used = jnp.concatenate([next_used[1:], jnp.full((1,), n_exp, I32)])
    next_used = jnp.where(next_used < n_exp, next_used, -1)
    slot_of = (jnp.cumsum(used.astype(I32)) - 1) % 2
    of_block = lambda table: jnp.sum(jnp.where(block_e[:, None] == experts[None, :], table[None, :], 0), axis=1)
    block_first = ((blk * tm == of_block(pstart)) & (blk < n_valid)).astype(I32)
    block_slot = of_block(slot_of).astype(I32)
    block_next = of_block(next_used).astype(I32)
    n_rows = n_blocks * tm
    pad_start = jnp.concatenate([pstart + tot, pend[-1:]]).astype(I32)
    pad_cnt = jnp.concatenate([padded - tot, n_rows - pend[-1:]]).astype(I32)
    flat = lambda a: a.reshape(-1).astype(I32)
    seg = (flat(goff), flat(loff), flat(cp), ltot)
    xs = _dispatch(x1, mods_l, seq_len, lp, gates, *seg, pad_start, pad_cnt, n_rows)
    ys = _experts(xs, block_e, block_src.astype(I32), n_valid.reshape(1).astype(I32), block_first, block_slot,
                  block_next, w_gu, b_gu, w_dn, b_dn, layer)
    return _combine(ys, lp.T, *seg, x1, mods_l, ln_g, ln_b, alpha, seq_len)


def kernel(x, c, ada_w, ada_b, post_ln_g, post_ln_b, conv_w_pw1, conv_b_pw1, conv_w_dw, conv_b_dw, conv_ln_g, conv_ln_b, conv_w_pw2, conv_b_pw2, w_kv, attn_w_q, attn_lambda, attn_subln_g, attn_w_o, rel_bias_table, router_w, router_b, expert_w_gate_up, expert_b_gate_up, expert_w_down, expert_b_down):
    bsz, s, d = x.shape
    depth = ada_w.shape[0]
    n_a = depth // 2
    alpha = (2 * depth) ** 0.25
    mods = _ada(c, ada_w, ada_b).reshape(depth, bsz, 6, d)
    q = k = v = None
    for l in range(depth):
        mods_l = mods[l]
        if l < n_a:
            x1 = _conv(x, mods_l, conv_w_pw1[l], conv_b_pw1[l], conv_w_dw[l], conv_b_dw[l], conv_ln_g[l],
                       conv_ln_b[l], conv_w_pw2[l], conv_b_pw2[l], post_ln_g[l, 0], post_ln_b[l, 0], alpha)
        else:
            j = l - n_a
            if j == 0:
                q, k, v = _qkv(x, mods_l, w_kv, attn_w_q[j])
            else:
                q = _qkv(x, mods_l, w_kv, attn_w_q[j])[0]
            lambda_init = 0.8 - 0.6 * math.exp(-0.3 * l)
            o = _attn(q, k, v, attn_lambda[j], attn_subln_g[j], rel_bias_table, lambda_init)
            x1 = _attn_out(o, x, mods_l, attn_w_o[j], post_ln_g[l, 0], post_ln_b[l, 0], alpha)
        x = _moe_layer(x1.reshape(bsz * s, d), mods_l, router_w[l], router_b[l],
                       expert_w_gate_up, expert_b_gate_up, expert_w_down, expert_b_down, l,
                       post_ln_g[l, 1], post_ln_b[l, 1], alpha, s).reshape(bsz, s, d)
    return x
```

```python
import functools
import math

import jax
import jax.numpy as jnp
from jax import lax
from jax.experimental import pallas as pl
from jax.experimental.pallas import tpu as pltpu

F32 = jnp.float32
BF16 = jnp.bfloat16
I32 = jnp.int32
U32 = jnp.uint32
HIGHEST = lax.Precision.HIGHEST

CHUNK = 64
CONV_WIDTH = 31
HEAD_DIM = 64
REL_BUCKETS = 32
REL_MAX_DIST = 128
TOP_K = 4
SWIGLU_LIMIT = 7.0
SWIGLU_ALPHA = 1.702
LN_EPS = 1e-5
MASK_VALUE = -1e30
LOG2_E = math.log2(math.e)

SUBLANES = 8
LANES = 128
VMEM_LIMIT_BYTES = 56 * 1024 * 1024

ADA_TN = 1024
SEQ_TILE = 256
PROJ_TILE = 512
CONV_HALO = 32
CONV_ROWS = 64
CONV_COLS = 256
MOE_TOKENS = 256
ROUTER_TOKENS = 1024
MOE_TILE = 512
Q_BLOCK = 128
CAST_ROWS = 128


def _params(*sem):
    return pltpu.CompilerParams(dimension_semantics=sem, vmem_limit_bytes=VMEM_LIMIT_BYTES)


def _layer_norm(x, g, b):
    mu = jnp.mean(x, axis=-1, keepdims=True)
    xc = x - mu
    var = jnp.mean(xc * xc, axis=-1, keepdims=True)
    return xc * lax.rsqrt(var + LN_EPS) * g + b


def _dot(a, b):
    return jnp.dot(a, b, preferred_element_type=F32)


def _ada_body(c_ref, w_ref, b_ref, o_ref):
    c = c_ref[...]
    cond = c * jax.nn.sigmoid(c)
    o_ref[...] = jnp.dot(cond, w_ref[...], preferred_element_type=F32, precision=HIGHEST) + b_ref[...]


def _ada(c, ada_w, ada_b):
    depth, d, n = ada_w.shape
    bsz = c.shape[0]
    tn = min(ADA_TN, n)
    return pl.pallas_call(
        _ada_body,
        grid=(depth, n // tn),
        in_specs=[
            pl.BlockSpec((bsz, d), lambda l, j: (0, 0)),
            pl.BlockSpec((None, d, tn), lambda l, j: (l, 0, j)),
            pl.BlockSpec((None, 1, tn), lambda l, j: (l, 0, j)),
        ],
        out_specs=pl.BlockSpec((None, bsz, tn), lambda l, j: (l, 0, j)),
        out_shape=jax.ShapeDtypeStruct((depth, bsz, n), F32),
        compiler_params=_params("parallel", "parallel"),
        name="ada",
    )(c, ada_w, ada_b.reshape(depth, 1, n))


def _residual_epilogue(x, y, mod_ref, lng_ref, lnb_ref, alpha, x1_ref):
    x1_ref[...] = _layer_norm(alpha * x + mod_ref[2:3, :] * y, lng_ref[...], lnb_ref[...])


def _moe_input(x1_ref, mod_ref):
    return x1_ref[...] * (1.0 + mod_ref[4:5, :]) + mod_ref[3:4, :]


def _conv_body(x_ref, halo_ref, mod_ref, w1_ref, b1_ref, wdw_ref, bdw_ref, cg_ref, cb_ref, w2_ref, b2_ref,
               lng_ref, lnb_ref, x1_ref, win_ref, v_ref, *, alpha):
    ts, d = x_ref.shape
    i = pl.program_id(1)
    xw = jnp.concatenate([halo_ref[...], x_ref[...]], axis=0)
    h = (xw * (1.0 + mod_ref[1:2, :]) + mod_ref[0:1, :]).astype(BF16)
    rows = min(CONV_ROWS, ts)
    cols = min(CONV_COLS, d)

    def glu(c0):
        a = _dot(h, w1_ref[:, c0:c0 + cols]) + b1_ref[:, c0:c0 + cols]
        g = _dot(h, w1_ref[:, d + c0:d + c0 + cols]) + b1_ref[:, d + c0:d + c0 + cols]
        return a * jax.nn.sigmoid(g)

    off = CONV_HALO - (CONV_WIDTH - 1)
    n_shift = ts + CONV_HALO - SUBLANES
    in_seq = (lax.broadcasted_iota(I32, (ts + CONV_HALO, cols), 0) >= CONV_HALO) | (i > 0)
    u_next = glu(0)
    for c0 in range(0, d, cols):
        chan = slice(c0, c0 + cols)
        win_ref[0, :, chan] = jnp.where(in_seq, u_next, 0.0)
        if c0 + cols < d:
            u_next = glu(c0 + cols)
        for b in range(1, SUBLANES):
            win_ref[b, 0:n_shift, chan] = win_ref[0, b:b + n_shift, chan]
        for r0 in range(0, ts, rows):
            acc = jnp.zeros((rows, cols), F32)
            for j in range(CONV_WIDTH):
                a, b = divmod(off + j, SUBLANES)
                r = r0 + a * SUBLANES
                acc = acc + wdw_ref[j:j + 1, c0:c0 + cols] * win_ref[b, r:r + rows, c0:c0 + cols]
            v_ref[r0:r0 + rows, c0:c0 + cols] = acc
    v = _layer_norm(v_ref[...] + bdw_ref[...], cg_ref[...], cb_ref[...])
    v = (v * jax.nn.sigmoid(v)).astype(BF16)
    y = _dot(v, w2_ref[...]) + b2_ref[...]
    _residual_epilogue(x_ref[...], y, mod_ref, lng_ref, lnb_ref, alpha, x1_ref)


def _conv(x, mods_l, w_pw1, b_pw1, w_dw, b_dw, cln_g, cln_b, w_pw2, b_pw2, ln_g, ln_b, alpha):
    bsz, s, d = x.shape
    ts = min(SEQ_TILE, s)
    hb = ts // CONV_HALO
    row = lambda a: a.reshape(1, d)
    tile = pl.BlockSpec((None, ts, d), lambda b, i: (b, i, 0))
    vec = pl.BlockSpec((1, d), lambda b, i: (0, 0))
    return pl.pallas_call(
        functools.partial(_conv_body, alpha=alpha),
        grid=(bsz, s // ts),
        in_specs=[
            tile,
            pl.BlockSpec((None, CONV_HALO, d), lambda b, i: (b, jnp.maximum(i * hb - 1, 0), 0)),
            pl.BlockSpec((None, 6, d), lambda b, i: (b, 0, 0)),
            pl.BlockSpec((d, 2 * d), lambda b, i: (0, 0)),
            pl.BlockSpec((1, 2 * d), lambda b, i: (0, 0)),
            pl.BlockSpec((CONV_WIDTH, d), lambda b, i: (0, 0)),
            vec, vec, vec,
            pl.BlockSpec((d, d), lambda b, i: (0, 0)),
            vec, vec, vec,
        ],
        out_specs=tile,
        out_shape=jax.ShapeDtypeStruct((bsz, s, d), F32),
        scratch_shapes=[pltpu.VMEM((SUBLANES, ts + CONV_HALO, d), F32), pltpu.VMEM((ts, d), F32)],
        compiler_params=_params("parallel", "parallel"),
        name="conv",
    )(x, x, mods_l, w_pw1.astype(BF16), b_pw1.reshape(1, 2 * d), w_dw, row(b_dw), row(cln_g), row(cln_b),
      w_pw2.astype(BF16), row(b_pw2), row(ln_g), row(ln_b))


def _qkv_body(x_ref, mod_ref, wkv_ref, wq_ref, q_ref, k_ref, v_ref):
    d = x_ref.shape[-1]
    x = x_ref[...]
    kv = _dot(x.astype(BF16), wkv_ref[...])
    k_ref[...] = kv[:, :d].astype(BF16)
    v_ref[...] = kv[:, d:].astype(BF16)
    h = (x * (1.0 + mod_ref[1:2, :]) + mod_ref[0:1, :]).astype(BF16)
    q_ref[...] = (_dot(h, wq_ref[...]) * (HEAD_DIM ** -0.5 * LOG2_E)).astype(BF16)


def _qkv(x, mods_l, w_kv, w_q):
    bsz, s, d = x.shape
    ts = min(PROJ_TILE, s)
    tile = pl.BlockSpec((None, ts, d), lambda b, i: (b, i, 0))
    return pl.pallas_call(
        _qkv_body,
        grid=(bsz, s // ts),
        in_specs=[
            tile,
            pl.BlockSpec((None, 6, d), lambda b, i: (b, 0, 0)),
            pl.BlockSpec((d, 2 * d), lambda b, i: (0, 0)),
            pl.BlockSpec((d, d), lambda b, i: (0, 0)),
        ],
        out_specs=[tile, tile, tile],
        out_shape=[jax.ShapeDtypeStruct((bsz, s, d), BF16)] * 3,
        compiler_params=_params("parallel", "parallel"),
        name="qkv",
    )(x, mods_l, w_kv.astype(BF16), w_q.astype(BF16))


def _t5_bucket(rel):
    nb = REL_BUCKETS // 2
    ret = jnp.where(rel > 0, nb, 0)
    n = jnp.abs(rel)
    max_exact = nb // 2
    large = max_exact + (jnp.log(jnp.maximum(n, 1).astype(F32) / max_exact)
                         / math.log(REL_MAX_DIST / max_exact) * (nb - max_exact)).astype(I32)
    large = jnp.minimum(large, nb - 1)
    return ret + jnp.where(n < max_exact, n, large)


def _bucket_strip(s):
    r = jnp.arange(Q_BLOCK, dtype=I32)[:, None]
    kp = jnp.arange(s, dtype=I32)[None, :] - (s - Q_BLOCK)
    bucket = _t5_bucket(kp - r)
    visible = jnp.floor_divide(kp, CHUNK) <= (r // CHUNK)
    return jnp.where(visible, bucket, REL_BUCKETS)


def _attn_body(tab_ref, q_ref, k_ref, v_ref, bkt_ref, lam_ref, sg_ref, o_ref, bias_ref, *, lambda_init, n_heads):
    h = pl.program_id(0)
    b = pl.program_id(1)
    s = q_ref.shape[0]

    @pl.when(b == 0)
    def _():
        bk = bkt_ref[...]
        acc = jnp.full(bk.shape, MASK_VALUE, F32)
        for r in range(REL_BUCKETS):
            acc = jnp.where(bk == r, tab_ref[r * n_heads + h] * LOG2_E, acc)
        bias_ref[0:Q_BLOCK, :] = acc
        bias_ref[Q_BLOCK:, :] = acc

    lp = lam_ref[...]
    lam = (jnp.exp(jnp.sum(lp[0:1, :] * lp[1:2, :], axis=-1, keepdims=True))
           - jnp.exp(jnp.sum(lp[2:3, :] * lp[3:4, :], axis=-1, keepdims=True)) + lambda_init)
    lane = lax.broadcasted_iota(I32, (Q_BLOCK, 2 * HEAD_DIM), 1)
    nt = (((1,), (1,)), ((), ()))

    def scores(i):
        n_keys = (i + 1) * Q_BLOCK
        q = q_ref[i * Q_BLOCK:(i + 1) * Q_BLOCK, :]
        qq = jnp.concatenate([jnp.where(lane < HEAD_DIM, q, jnp.zeros_like(q)),
                              jnp.where(lane >= HEAD_DIM, q, jnp.zeros_like(q))], axis=0)
        return lax.dot_general(qq, k_ref[0:n_keys, :], nt, preferred_element_type=F32) + bias_ref[:, s - n_keys:s]

    n_blocks = s // Q_BLOCK
    sc_next = scores(0)
    for i in range(n_blocks):
        n_keys = (i + 1) * Q_BLOCK
        sc = sc_next
        if i + 1 < n_blocks:
            sc_next = scores(i + 1)
        p = jnp.exp2(sc - jnp.max(sc, axis=-1, keepdims=True))
        denom = jnp.sum(p, axis=-1, keepdims=True)
        pv = _dot(p.astype(BF16), v_ref[0:n_keys, :])
        o = pv[:Q_BLOCK] * (1.0 / denom[:Q_BLOCK]) - pv[Q_BLOCK:] * (lam / denom[Q_BLOCK:])
        o = o * lax.rsqrt(jnp.mean(o * o, axis=-1, keepdims=True) + LN_EPS) * sg_ref[...]
        o_ref[i * Q_BLOCK:(i + 1) * Q_BLOCK, :] = (o * (1.0 - lambda_init)).astype(BF16)


def _attn(q, k, v, lam_p, subln_g, rel_table, lambda_init):
    bsz, s, d = q.shape
    hd2 = 2 * HEAD_DIM
    n_heads = d // hd2
    head = pl.BlockSpec((None, s, hd2), lambda h, b, tab: (b, 0, h))
    grid_spec = pltpu.PrefetchScalarGridSpec(
        num_scalar_prefetch=1,
        grid=(n_heads, bsz),
        in_specs=[
            head, head, head,
            pl.BlockSpec((Q_BLOCK, s), lambda h, b, tab: (0, 0)),
            pl.BlockSpec((4, HEAD_DIM), lambda h, b, tab: (0, 0)),
            pl.BlockSpec((1, hd2), lambda h, b, tab: (0, 0)),
        ],
        out_specs=head,
        scratch_shapes=[pltpu.VMEM((2 * Q_BLOCK, s), F32)],
    )
    return pl.pallas_call(
        functools.partial(_attn_body, lambda_init=lambda_init, n_heads=n_heads),
        grid_spec=grid_spec,
        out_shape=jax.ShapeDtypeStruct((bsz, s, d), BF16),
        compiler_params=_params("arbitrary", "arbitrary"),
        name="attn",
    )(rel_table.reshape(-1), q, k, v, _bucket_strip(s), lam_p, subln_g.reshape(1, hd2))


def _attn_out_body(o_ref, x_ref, mod_ref, wo_ref, lng_ref, lnb_ref, x1_ref, *, alpha):
    y = _dot(o_ref[...], wo_ref[...])
    _residual_epilogue(x_ref[...], y, mod_ref, lng_ref, lnb_ref, alpha, x1_ref)


def _attn_out(o, x, mods_l, w_o, ln_g, ln_b, alpha):
    bsz, s, d = x.shape
    ts = min(PROJ_TILE, s)
    tile = pl.BlockSpec((None, ts, d), lambda b, i: (b, i, 0))
    vec = pl.BlockSpec((1, d), lambda b, i: (0, 0))
    return pl.pallas_call(
        functools.partial(_attn_out_body, alpha=alpha),
        grid=(bsz, s // ts),
        in_specs=[tile, tile, pl.BlockSpec((None, 6, d), lambda b, i: (b, 0, 0)),
                  pl.BlockSpec((d, d), lambda b, i: (0, 0)), vec, vec],
        out_specs=tile,
        out_shape=jax.ShapeDtypeStruct((bsz, s, d), F32),
        compiler_params=_params("parallel", "parallel"),
        name="attn_out",
    )(o, x, mods_l, w_o.astype(BF16), ln_g.reshape(1, d), ln_b.reshape(1, d))


def _router_body(x1_ref, mod_ref, whi_ref, wlo_ref, b_ref, tri_ref, idx_ref, gate_ref, rank_ref, cnt_ref):
    h = _moe_input(x1_ref, mod_ref)
    h_hi = h.astype(BF16)
    h_lo = (h - h_hi.astype(F32)).astype(BF16)
    nt = (((1,), (1,)), ((), ()))
    logits = (lax.dot_general(whi_ref[...], h_hi, nt, preferred_element_type=F32)
              + (lax.dot_general(whi_ref[...], h_lo, nt, preferred_element_type=F32)
                 + lax.dot_general(wlo_ref[...], h_hi, nt, preferred_element_type=F32))) + b_ref[...]
    n_exp, tr = logits.shape
    eio = lax.broadcasted_iota(I32, (n_exp, tr), 0)
    work = logits
    vals, idxs = [], []
    for _ in range(TOP_K):
        m = jnp.max(work, axis=0, keepdims=True)
        am = jnp.min(jnp.where(work == m, eio, n_exp), axis=0, keepdims=True)
        vals.append(m)
        idxs.append(am)
        work = jnp.where(eio == am, -jnp.inf, work)
    ex = [jnp.exp(v - vals[0]) for v in vals]
    den = ex[0] + ex[1] + ex[2] + ex[3]
    onehot = jnp.zeros((n_exp, tr), F32)
    for k in range(TOP_K):
        onehot = onehot + (eio == idxs[k]).astype(F32)
    before = _dot(onehot.astype(BF16), tri_ref[...])
    for k in range(TOP_K):
        idx_ref[k:k + 1, :] = idxs[k]
        gate_ref[k:k + 1, :] = ex[k] / den
        rank_ref[k:k + 1, :] = jnp.sum(jnp.where(eio == idxs[k], before, 0.0), axis=0, keepdims=True).astype(I32)
    tl = tr // cnt_ref.shape[0]
    for j in range(cnt_ref.shape[0]):
        cnt_ref[j] = jnp.sum(onehot[:, j * tl:(j + 1) * tl], axis=1, keepdims=True).astype(I32)


def _router(x1, mods_l, seq_len, w_r, b_r):
    t, d = x1.shape
    n_exp = w_r.shape[1]
    tl = min(MOE_TOKENS, t)
    tr = min(ROUTER_TOKENS, seq_len)
    per_seq = seq_len // tr
    wt = w_r.T
    wt_hi = wt.astype(BF16)
    wt_lo = (wt - wt_hi.astype(F32)).astype(BF16)
    pos = jnp.arange(tr, dtype=I32)
    tri = ((pos[:, None] < pos[None, :]) & (pos[:, None] // tl == pos[None, :] // tl)).astype(BF16)
    tok = pl.BlockSpec((TOP_K, tr), lambda i: (0, i))
    return pl.pallas_call(
        _router_body,
        grid=(t // tr,),
        in_specs=[
            pl.BlockSpec((tr, d), lambda i: (i, 0)),
            pl.BlockSpec((None, 6, d), lambda i: (i // per_seq, 0, 0)),
            pl.BlockSpec((n_exp, d), lambda i: (0, 0)),
            pl.BlockSpec((n_exp, d), lambda i: (0, 0)),
            pl.BlockSpec((n_exp, 1), lambda i: (0, 0)),
            pl.BlockSpec((tr, tr), lambda i: (0, 0)),
        ],
        out_specs=[tok, tok, tok, pl.BlockSpec((tr // tl, n_exp, 1), lambda i: (i, 0, 0))],
        out_shape=[jax.ShapeDtypeStruct((TOP_K, t), I32), jax.ShapeDtypeStruct((TOP_K, t), F32),
                   jax.ShapeDtypeStruct((TOP_K, t), I32), jax.ShapeDtypeStruct((t // tl, n_exp, 1), I32)],
        compiler_params=_params("parallel"),
        name="router",
    )(x1, mods_l, wt_hi, wt_lo, b_r.reshape(n_exp, 1), tri)


def _rows(ref, start, cnt):
    aligned = lambda v: v if isinstance(v, int) else pl.multiple_of(v, SUBLANES)
    return ref.at[pl.ds(aligned(start), aligned(cnt))]


def _segment_copy(src, src_row, dst, dst_row, cnt, sem):
    @pl.when(cnt > 0)
    def _():
        pltpu.make_async_copy(_rows(src, src_row, cnt), _rows(dst, dst_row, cnt), sem).start()


def _wait_rows(ref, cnt, sem):
    @pl.when(cnt > 0)
    def _():
        pltpu.make_async_copy(_rows(ref, 0, cnt), _rows(ref, 0, cnt), sem).wait()


def _zero_fill(zero_ref, xs_hbm, start, cnt, sem):
    zr = zero_ref.shape[0]
    n_full = cnt // zr

    def full(r, c):
        _segment_copy(zero_ref, 0, xs_hbm, start + r * zr, zr, sem)
        return c

    lax.fori_loop(0, n_full, full, 0)
    _segment_copy(zero_ref, 0, xs_hbm, start + n_full * zr, cnt - n_full * zr, sem)
    _wait_rows(xs_hbm, cnt, sem)


def _pack_halves(x):
    hd = x.shape[1] // 2
    lo = lax.bitcast_convert_type(x[:, :hd], U32)
    hi = lax.bitcast_convert_type(x[:, hd:], U32)
    return (lo >> 16) | (hi & jnp.uint32(0xFFFF0000))


def _unpack_halves(words):
    lo = lax.bitcast_convert_type(words << 16, F32).astype(BF16)
    hi = lax.bitcast_convert_type(words & jnp.uint32(0xFFFF0000), F32).astype(BF16)
    return lo, hi


def _one_hot_hits(iota, pos_of):
    hit = iota == pos_of(0)
    for k in range(1, TOP_K):
        hit = hit | (iota == pos_of(k))
    return hit


def _dispatch_body(goff_ref, loff_ref, cp_ref, ltot_ref, pad_start_ref, pad_cnt_ref, x1_ref, mod_ref, lp_ref,
                   gate_ref, xs_hbm, xl_ref, zero_ref, sem, zsem, *, n_exp):
    j = pl.program_id(0)
    tl, d = x1_ref.shape
    lr = xl_ref.shape[1]

    @pl.when(j == 0)
    def _():
        zero_ref[...] = jnp.zeros_like(zero_ref)

        def per_range(e, carry):
            _zero_fill(zero_ref, xs_hbm, pad_start_ref[e], pad_cnt_ref[e], zsem)
            return carry
        lax.fori_loop(0, pad_start_ref.shape[0], per_range, 0)

    def start_segments(tile):
        def per_expert(e, carry):
            seg = tile * n_exp + e
            _segment_copy(xl_ref.at[tile % 2], loff_ref[seg], xs_hbm, goff_ref[seg], cp_ref[seg], sem.at[tile % 2])
            return carry
        lax.fori_loop(0, n_exp, per_expert, 0)

    def wait_segments(tile):
        _wait_rows(xs_hbm, ltot_ref[tile], sem.at[tile % 2])

    @pl.when(j >= 2)
    def _():
        wait_segments(j - 2)

    buf = xl_ref.at[j % 2]
    riota = lax.broadcasted_iota(I32, (lr, tl), 0)
    perm = _one_hot_hits(riota, lambda k: lp_ref[k:k + 1, :]).astype(F32).astype(BF16)
    rows = _dot(perm, _moe_input(x1_ref, mod_ref).astype(BF16))
    buf[:, 0:d // 2] = _pack_halves(rows)
    gsel = jnp.zeros((lr, tl), F32)
    for k in range(TOP_K):
        gsel = gsel + jnp.where(riota == lp_ref[k:k + 1, :], gate_ref[k:k + 1, :], 0.0)
    rowg = jnp.broadcast_to(jnp.sum(gsel, axis=1, keepdims=True), (lr, LANES))
    buf[:, d // 2:] = lax.bitcast_convert_type(rowg, U32)
    start_segments(j)

    @pl.when(j == pl.num_programs(0) - 1)
    def _():
        @pl.when(j >= 1)
        def _():
            wait_segments(j - 1)
        wait_segments(j)


def _dispatch(x1, mods_l, seq_len, lp, gates, goff, loff, cp, ltot, pad_start, pad_cnt, n_rows):
    t, d = x1.shape
    tl = min(MOE_TOKENS, t)
    per_seq = seq_len // tl
    n_exp = goff.shape[0] // (t // tl)
    lr = TOP_K * tl + n_exp * SUBLANES
    tok = pl.BlockSpec((TOP_K, tl), lambda j, *_: (0, j))
    grid_spec = pltpu.PrefetchScalarGridSpec(
        num_scalar_prefetch=6,
        grid=(t // tl,),
        in_specs=[pl.BlockSpec((tl, d), lambda j, *_: (j, 0)),
                  pl.BlockSpec((None, 6, d), lambda j, *_: (j // per_seq, 0, 0)), tok, tok],
        out_specs=pl.BlockSpec(memory_space=pl.ANY),
        scratch_shapes=[pltpu.VMEM((2, lr, d // 2 + LANES), U32), pltpu.VMEM((MOE_TILE, d // 2 + LANES), U32),
                        pltpu.SemaphoreType.DMA((2,)), pltpu.SemaphoreType.DMA],
    )
    return pl.pallas_call(
        functools.partial(_dispatch_body, n_exp=n_exp),
        grid_spec=grid_spec,
        out_shape=jax.ShapeDtypeStruct((n_rows, d // 2 + LANES), U32),
        compiler_params=_params("arbitrary"),
        name="dispatch",
    )(goff, loff, cp, ltot, pad_start, pad_cnt, x1, mods_l, lp, gates)


def _experts_body(be_ref, bsrc_ref, nvalid_ref, first_ref, slot_ref, next_ref, xs_ref, wgu_hbm, bgu_ref, wdn_hbm,
                  bdn_ref, ys_ref, wgu_f32, wdn_f32, wgu_bf, wdn_bf, sem_gu, sem_dn, *, layer):
    i = pl.program_id(0)
    e = be_ref[i]
    d, f2 = wgu_bf.shape
    f = f2 // 2

    def fetch(expert, slot, start):
        for hbm, buf, sem in ((wgu_hbm, wgu_f32, sem_gu), (wdn_hbm, wdn_f32, sem_dn)):
            dma = pltpu.make_async_copy(hbm.at[layer, expert], buf.at[slot], sem.at[slot])
            dma.start() if start else dma.wait()

    @pl.when(first_ref[i] == 1)
    def _():
        slot = slot_ref[i]

        @pl.when(i == 0)
        def _():
            fetch(e, slot, True)

        fetch(e, slot, False)

        @pl.when(next_ref[i] >= 0)
        def _():
            fetch(next_ref[i], 1 - slot, True)

        def cast(ref_in, ref_out):
            def step(r, c):
                rows = pl.ds(pl.multiple_of(r * CAST_ROWS, CAST_ROWS), CAST_ROWS)
                ref_out[rows, :] = ref_in[rows, :].astype(BF16)
                return c
            lax.fori_loop(0, ref_in.shape[0] // CAST_ROWS, step, 0)
        cast(wgu_f32.at[slot], wgu_bf)
        cast(wdn_f32.at[slot], wdn_bf)

    @pl.when(i < nvalid_ref[0])
    def _():
        hd = d // 2
        x_lo, x_hi = _unpack_halves(xs_ref[:, 0:hd])
        proj = lambda cols: _dot(x_lo, wgu_bf[0:hd, cols]) + _dot(x_hi, wgu_bf[hd:, cols]) + bgu_ref[:, cols]
        gate = jnp.minimum(proj(slice(0, f)), SWIGLU_LIMIT)
        lin = jnp.clip(proj(slice(f, f2)), -SWIGLU_LIMIT, SWIGLU_LIMIT)
        act = (gate * jax.nn.sigmoid(SWIGLU_ALPHA * gate) * (lin + 1.0)).astype(BF16)
        row_gate = lax.bitcast_convert_type(xs_ref[:, hd:hd + 1], F32)
        ys_ref[...] = (_dot(act, wdn_bf[...]) + bdn_ref[...]) * row_gate

    @pl.when(i >= nvalid_ref[0])
    def _():
        ys_ref[...] = jnp.zeros_like(ys_ref)


def _experts(xs, block_e, block_src, n_valid, block_first, block_slot, block_next, w_gu, b_gu, w_dn, b_dn, layer):
    n_rows, dx = xs.shape
    depth, n_exp, d, f2 = w_gu.shape
    f = f2 // 2
    tm = MOE_TILE
    grid_spec = pltpu.PrefetchScalarGridSpec(
        num_scalar_prefetch=6,
        grid=(n_rows // tm,),
        in_specs=[
            pl.BlockSpec((tm, dx), lambda i, be, bs, *_: (bs[i], 0)),
            pl.BlockSpec(memory_space=pl.ANY),
            pl.BlockSpec((None, None, 1, f2), lambda i, be, *_: (layer, be[i], 0, 0)),
            pl.BlockSpec(memory_space=pl.ANY),
            pl.BlockSpec((None, None, 1, d), lambda i, be, *_: (layer, be[i], 0, 0)),
        ],
        out_specs=pl.BlockSpec((tm, d), lambda i, *_: (i, 0)),
        scratch_shapes=[pltpu.VMEM((2, d, f2), F32), pltpu.VMEM((2, f, d), F32),
                        pltpu.VMEM((d, f2), BF16), pltpu.VMEM((f, d), BF16),
                        pltpu.SemaphoreType.DMA((2,)), pltpu.SemaphoreType.DMA((2,))],
    )
    return pl.pallas_call(
        functools.partial(_experts_body, layer=layer),
        grid_spec=grid_spec,
        out_shape=jax.ShapeDtypeStruct((n_rows, d), F32),
        compiler_params=_params("arbitrary"),
        name="experts",
    )(block_e, block_src, n_valid, block_first, block_slot, block_next, xs, w_gu,
      b_gu.reshape(depth, n_exp, 1, f2), w_dn, b_dn.reshape(depth, n_exp, 1, d))


def _combine_body(goff_ref, loff_ref, cp_ref, ltot_ref, ys_hbm, x_ref, lp_ref, mod_ref, lng_ref, lnb_ref, o_ref,
                  yl_ref, sem, *, n_exp, alpha):
    j = pl.program_id(0)
    tl = x_ref.shape[0]
    lr, d = yl_ref.shape[1:]

    def fetch(tile):
        buf = yl_ref.at[tile % 2]

        def per_expert(e, carry):
            seg = tile * n_exp + e
            _segment_copy(ys_hbm, goff_ref[seg], buf, loff_ref[seg], cp_ref[seg], sem.at[tile % 2])
            return carry
        lax.fori_loop(0, n_exp, per_expert, 0)

        def zero_rows(r, carry):
            buf[pl.ds(pl.multiple_of(r * SUBLANES, SUBLANES), SUBLANES), :] = jnp.zeros((SUBLANES, d), F32)
            return carry
        lax.fori_loop(ltot_ref[tile] // SUBLANES, lr // SUBLANES, zero_rows, 0)

    @pl.when(j == 0)
    def _():
        fetch(j)

    @pl.when(j + 1 < pl.num_programs(0))
    def _():
        fetch(j + 1)

    liota = lax.broadcasted_iota(I32, (tl, lr), 1)
    pick = _one_hot_hits(liota, lambda k: lp_ref[:, k:k + 1]).astype(F32).astype(BF16)
    _wait_rows(ys_hbm, ltot_ref[j], sem.at[j % 2])
    y = _dot(pick, yl_ref[j % 2].astype(BF16))
    o_ref[...] = _layer_norm(alpha * x_ref[...] + mod_ref[5:6, :] * y, lng_ref[...], lnb_ref[...])


def _combine(ys, lp_t, goff, loff, cp, ltot, x1, mods_l, ln_g, ln_b, alpha, seq_len):
    t, d = x1.shape
    tl = min(MOE_TOKENS, t)
    n_exp = goff.shape[0] // (t // tl)
    lr = TOP_K * tl + n_exp * SUBLANES
    per_seq = seq_len // tl
    vec = pl.BlockSpec((1, d), lambda j, *_: (0, 0))
    grid_spec = pltpu.PrefetchScalarGridSpec(
        num_scalar_prefetch=4,
        grid=(t // tl,),
        in_specs=[
            pl.BlockSpec(memory_space=pl.ANY),
            pl.BlockSpec((tl, d), lambda j, *_: (j, 0)),
            pl.BlockSpec((tl, TOP_K), lambda j, *_: (j, 0)),
            pl.BlockSpec((None, 6, d), lambda j, *_: (j // per_seq, 0, 0)),
            vec, vec,
        ],
        out_specs=pl.BlockSpec((tl, d), lambda j, *_: (j, 0)),
        scratch_shapes=[pltpu.VMEM((2, lr, d), F32), pltpu.SemaphoreType.DMA((2,))],
    )
    return pl.pallas_call(
        functools.partial(_combine_body, n_exp=n_exp, alpha=alpha),
        grid_spec=grid_spec,
        out_shape=jax.ShapeDtypeStruct((t, d), F32),
        compiler_params=_params("arbitrary"),
        name="combine",
    )(goff, loff, cp, ltot, ys, x1, lp_t, mods_l, ln_g.reshape(1, d), ln_b.reshape(1, d))


def _moe_layer(x1, mods_l, w_r, b_r, w_gu, b_gu, w_dn, b_dn, layer, ln_g, ln_b, alpha, seq_len):
    t, d = x1.shape
    n_exp = w_r.shape[1]
    tm = MOE_TILE
    tl = min(MOE_TOKENS, t)
    n_tiles = t // tl
    idx, gates, rank, cnt = _router(x1, mods_l, seq_len, w_r, b_r)
    cp = (cnt[:, :, 0] + SUBLANES - 1) // SUBLANES * SUBLANES
    tot = jnp.sum(cp, axis=0)
    padded = (tot + tm - 1) // tm * tm
    pend = jnp.cumsum(padded)
    pstart = pend - padded
    goff = pstart[None, :] + jnp.cumsum(cp, axis=0) - cp
    loff = jnp.cumsum(cp, axis=1) - cp
    ltot = jnp.sum(cp, axis=1).astype(I32)
    is_e = idx[..., None] == jnp.arange(n_exp, dtype=I32)
    lp = jnp.sum(jnp.where(is_e, jnp.repeat(loff, tl, axis=0)[None], 0), axis=-1) + rank
    n_blocks = -(-(t * TOP_K + n_tiles * n_exp * (SUBLANES - 1)) // tm) + n_exp
    n_valid = pend[-1] // tm
    blk = jnp.arange(n_blocks, dtype=I32)
    block_src = jnp.minimum(blk, n_valid - 1)
    block_e = jnp.sum(pend[None, :] <= (block_src * tm)[:, None], axis=1).astype(I32)
    used = padded > 0
    experts = jnp.arange(n_exp, dtype=I32)
    next_used = jnp.flip(lax.cummin(jnp.flip(jnp.where(used, experts, n_exp))))
    next_used = jnp.concatenate([next_used[1:], jnp.full((1,), n_exp, I32)])
    next_used = jnp.where(next_used < n_exp, next_used, -1)
    slot_of = (jnp.cumsum(used.astype(I32)) - 1) % 2
    of_block = lambda table: jnp.sum(jnp.where(block_e[:, None] == experts[None, :], table[None, :], 0), axis=1)
    block_first = ((blk * tm == of_block(pstart)) & (blk < n_valid)).astype(I32)
    block_slot = of_block(slot_of).astype(I32)
    block_next = of_block(next_used).astype(I32)
    n_rows = n_blocks * tm
    pad_start = jnp.concatenate([pstart + tot, pend[-1:]]).astype(I32)
    pad_cnt = jnp.concatenate([padded - tot, n_rows - pend[-1:]]).astype(I32)
    flat = lambda a: a.reshape(-1).astype(I32)
    seg = (flat(goff), flat(loff), flat(cp), ltot)
    xs = _dispatch(x1, mods_l, seq_len, lp, gates, *seg, pad_start, pad_cnt, n_rows)
    ys = _experts(xs, block_e, block_src.astype(I32), n_valid.reshape(1).astype(I32), block_first, block_slot,
                  block_next, w_gu, b_gu, w_dn, b_dn, layer)
    return _combine(ys, lp.T, *seg, x1, mods_l, ln_g, ln_b, alpha, seq_len)


def kernel(x, c, ada_w, ada_b, post_ln_g, post_ln_b, conv_w_pw1, conv_b_pw1, conv_w_dw, conv_b_dw, conv_ln_g, conv_ln_b, conv_w_pw2, conv_b_pw2, w_kv, attn_w_q, attn_lambda, attn_subln_g, attn_w_o, rel_bias_table, router_w, router_b, expert_w_gate_up, expert_b_gate_up, expert_w_down, expert_b_down):
    bsz, s, d = x.shape
    depth = ada_w.shape[0]
    n_a = depth // 2
    alpha = (2 * depth) ** 0.25
    mods = _ada(c, ada_w, ada_b).reshape(depth, bsz, 6, d)
    q = k = v = None
    for l in range(depth):
        mods_l = mods[l]
        if l < n_a:
            x1 = _conv(x, mods_l, conv_w_pw1[l], conv_b_pw1[l], conv_w_dw[l], conv_b_dw[l], conv_ln_g[l],
                       conv_ln_b[l], conv_w_pw2[l], conv_b_pw2[l], post_ln_g[l, 0], post_ln_b[l, 0], alpha)
        else:
            j = l - n_a
            if j == 0:
                q, k, v = _qkv(x, mods_l, w_kv, attn_w_q[j])
            else:
                q = _qkv(x, mods_l, w_kv, attn_w_q[j])[0]
            lambda_init = 0.8 - 0.6 * math.exp(-0.3 * l)
            o = _attn(q, k, v, attn_lambda[j], attn_subln_g[j], rel_bias_table, lambda_init)
            x1 = _attn_out(o, x, mods_l, attn_w_o[j], post_ln_g[l, 0], post_ln_b[l, 0], alpha)
        x = _moe_layer(x1.reshape(bsz * s, d), mods_l, router_w[l], router_b[l],
                       expert_w_gate_up, expert_b_gate_up, expert_w_down, expert_b_down, l,
                       post_ln_g[l, 1], post_ln_b[l, 1], alpha, s).reshape(bsz, s, d)
    return x
```

```python
import functools
import math

import jax
import jax.numpy as jnp
from jax import lax
from jax.experimental import pallas as pl
from jax.experimental.pallas import tpu as pltpu

F32 = jnp.float32
BF16 = jnp.bfloat16
I32 = jnp.int32
U32 = jnp.uint32
HIGHEST = lax.Precision.HIGHEST

CHUNK = 64
CONV_WIDTH = 31
HEAD_DIM = 64
REL_BUCKETS = 32
REL_MAX_DIST = 128
TOP_K = 4
SWIGLU_LIMIT = 7.0
SWIGLU_ALPHA = 1.702
LN_EPS = 1e-5
MASK_VALUE = -1e30
LOG2_E = math.log2(math.e)

SUBLANES = 8
LANES = 128
VMEM_LIMIT_BYTES = 56 * 1024 * 1024

ADA_TN = 1024
SEQ_TILE = 256
PROJ_TILE = 512
CONV_HALO = 32
CONV_ROWS = 64
CONV_COLS = 256
MOE_TOKENS = 256
ROUTER_TOKENS = 1024
MOE_TILE = 512
Q_BLOCK = 128
CAST_ROWS = 128


def _params(*sem):
    return pltpu.CompilerParams(dimension_semantics=sem, vmem_limit_bytes=VMEM_LIMIT_BYTES)


def _layer_norm(x, g, b):
    mu = jnp.mean(x, axis=-1, keepdims=True)
    xc = x - mu
    var = jnp.mean(xc * xc, axis=-1, keepdims=True)
    return xc * lax.rsqrt(var + LN_EPS) * g + b


def _dot(a, b):
    return jnp.dot(a, b, preferred_element_type=F32)


def _ada_body(c_ref, w_ref, b_ref, o_ref):
    c = c_ref[...]
    cond = c * jax.nn.sigmoid(c)
    o_ref[...] = jnp.dot(cond, w_ref[...], preferred_element_type=F32, precision=HIGHEST) + b_ref[...]


def _ada(c, ada_w, ada_b):
    depth, d, n = ada_w.shape
    bsz = c.shape[0]
    tn = min(ADA_TN, n)
    return pl.pallas_call(
        _ada_body,
        grid=(depth, n // tn),
        in_specs=[
            pl.BlockSpec((bsz, d), lambda l, j: (0, 0)),
            pl.BlockSpec((None, d, tn), lambda l, j: (l, 0, j)),
            pl.BlockSpec((None, 1, tn), lambda l, j: (l, 0, j)),
        ],
        out_specs=pl.BlockSpec((None, bsz, tn), lambda l, j: (l, 0, j)),
        out_shape=jax.ShapeDtypeStruct((depth, bsz, n), F32),
        compiler_params=_params("parallel", "parallel"),
        name="ada",
    )(c, ada_w, ada_b.reshape(depth, 1, n))


def _residual_epilogue(x, y, mod_ref, lng_ref, lnb_ref, alpha, x1_ref):
    x1_ref[...] = _layer_norm(alpha * x + mod_ref[2:3, :] * y, lng_ref[...], lnb_ref[...])


def _moe_input(x1_ref, mod_ref):
    return x1_ref[...] * (1.0 + mod_ref[4:5, :]) + mod_ref[3:4, :]


def _conv_body(x_ref, halo_ref, mod_ref, w1_ref, b1_ref, wdw_ref, bdw_ref, cg_ref, cb_ref, w2_ref, b2_ref,
               lng_ref, lnb_ref, x1_ref, win_ref, v_ref, *, alpha):
    ts, d = x_ref.shape
    i = pl.program_id(1)
    xw = jnp.concatenate([halo_ref[...], x_ref[...]], axis=0)
    h = (xw * (1.0 + mod_ref[1:2, :]) + mod_ref[0:1, :]).astype(BF16)
    rows = min(CONV_ROWS, ts)
    cols = min(CONV_COLS, d)

    def glu(c0):
        a = _dot(h, w1_ref[:, c0:c0 + cols]) + b1_ref[:, c0:c0 + cols]
        g = _dot(h, w1_ref[:, d + c0:d + c0 + cols]) + b1_ref[:, d + c0:d + c0 + cols]
        return a * jax.nn.sigmoid(g)

    off = CONV_HALO - (CONV_WIDTH - 1)
    n_shift = ts + CONV_HALO - SUBLANES
    in_seq = (lax.broadcasted_iota(I32, (ts + CONV_HALO, cols), 0) >= CONV_HALO) | (i > 0)
    u_next = glu(0)
    for c0 in range(0, d, cols):
        chan = slice(c0, c0 + cols)
        win_ref[0, :, chan] = jnp.where(in_seq, u_next, 0.0)
        if c0 + cols < d:
            u_next = glu(c0 + cols)
        for b in range(1, SUBLANES):
            win_ref[b, 0:n_shift, chan] = win_ref[0, b:b + n_shift, chan]
        for r0 in range(0, ts, rows):
            acc = jnp.zeros((rows, cols), F32)
            for j in range(CONV_WIDTH):
                a, b = divmod(off + j, SUBLANES)
                r = r0 + a * SUBLANES
                acc = acc + wdw_ref[j:j + 1, c0:c0 + cols] * win_ref[b, r:r + rows, c0:c0 + cols]
            v_ref[r0:r0 + rows, c0:c0 + cols] = acc
    v = _layer_norm(v_ref[...] + bdw_ref[...], cg_ref[...], cb_ref[...])
    v = (v * jax.nn.sigmoid(v)).astype(BF16)
    y = _dot(v, w2_ref[...]) + b2_ref[...]
    _residual_epilogue(x_ref[...], y, mod_ref, lng_ref, lnb_ref, alpha, x1_ref)


def _conv(x, mods_l, w_pw1, b_pw1, w_dw, b_dw, cln_g, cln_b, w_pw2, b_pw2, ln_g, ln_b, alpha):
    bsz, s, d = x.shape
    ts = min(SEQ_TILE, s)
    hb = ts // CONV_HALO
    row = lambda a: a.reshape(1, d)
    tile = pl.BlockSpec((None, ts, d), lambda b, i: (b, i, 0))
    vec = pl.BlockSpec((1, d), lambda b, i: (0, 0))
    return pl.pallas_call(
        functools.partial(_conv_body, alpha=alpha),
        grid=(bsz, s // ts),
        in_specs=[
            tile,
            pl.BlockSpec((None, CONV_HALO, d), lambda b, i: (b, jnp.maximum(i * hb - 1, 0), 0)),
            pl.BlockSpec((None, 6, d), lambda b, i: (b, 0, 0)),
            pl.BlockSpec((d, 2 * d), lambda b, i: (0, 0)),
            pl.BlockSpec((1, 2 * d), lambda b, i: (0, 0)),
            pl.BlockSpec((CONV_WIDTH, d), lambda b, i: (0, 0)),
            vec, vec, vec,
            pl.BlockSpec((d, d), lambda b, i: (0, 0)),
            vec, vec, vec,
        ],
        out_specs=tile,
        out_shape=jax.ShapeDtypeStruct((bsz, s, d), F32),
        scratch_shapes=[pltpu.VMEM((SUBLANES, ts + CONV_HALO, d), F32), pltpu.VMEM((ts, d), F32)],
        compiler_params=_params("parallel", "parallel"),
        name="conv",
    )(x, x, mods_l, w_pw1.astype(BF16), b_pw1.reshape(1, 2 * d), w_dw, row(b_dw), row(cln_g), row(cln_b),
      w_pw2.astype(BF16), row(b_pw2), row(ln_g), row(ln_b))


def _qkv_body(x_ref, mod_ref, wkv_ref, wq_ref, q_ref, k_ref, v_ref):
    d = x_ref.shape[-1]
    x = x_ref[...]
    kv = _dot(x.astype(BF16), wkv_ref[...])
    k_ref[...] = kv[:, :d].astype(BF16)
    v_ref[...] = kv[:, d:].astype(BF16)
    h = (x * (1.0 + mod_ref[1:2, :]) + mod_ref[0:1, :]).astype(BF16)
    q_ref[...] = (_dot(h, wq_ref[...]) * (HEAD_DIM ** -0.5 * LOG2_E)).astype(BF16)


def _qkv(x, mods_l, w_kv, w_q):
    bsz, s, d = x.shape
    ts = min(PROJ_TILE, s)
    tile = pl.BlockSpec((None, ts, d), lambda b, i: (b, i, 0))
    return pl.pallas_call(
        _qkv_body,
        grid=(bsz, s // ts),
        in_specs=[
            tile,
            pl.BlockSpec((None, 6, d), lambda b, i: (b, 0, 0)),
            pl.BlockSpec((d, 2 * d), lambda b, i: (0, 0)),
            pl.BlockSpec((d, d), lambda b, i: (0, 0)),
        ],
        out_specs=[tile, tile, tile],
        out_shape=[jax.ShapeDtypeStruct((bsz, s, d), BF16)] * 3,
        compiler_params=_params("parallel", "parallel"),
        name="qkv",
    )(x, mods_l, w_kv.astype(BF16), w_q.astype(BF16))


def _t5_bucket(rel):
    nb = REL_BUCKETS // 2
    ret = jnp.where(rel > 0, nb, 0)
    n = jnp.abs(rel)
    max_exact = nb // 2
    large = max_exact + (jnp.log(jnp.maximum(n, 1).astype(F32) / max_exact)
                         / math.log(REL_MAX_DIST / max_exact) * (nb - max_exact)).astype(I32)
    large = jnp.minimum(large, nb - 1)
    return ret + jnp.where(n < max_exact, n, large)


def _bucket_strip(s):
    r = jnp.arange(Q_BLOCK, dtype=I32)[:, None]
    kp = jnp.arange(s, dtype=I32)[None, :] - (s - Q_BLOCK)
    bucket = _t5_bucket(kp - r)
    visible = jnp.floor_divide(kp, CHUNK) <= (r // CHUNK)
    return jnp.where(visible, bucket, REL_BUCKETS)


def _attn_body(tab_ref, q_ref, k_ref, v_ref, bkt_ref, lam_ref, sg_ref, o_ref, bias_ref, *, lambda_init, n_heads):
    h = pl.program_id(0)
    b = pl.program_id(1)
    s = q_ref.shape[0]

    @pl.when(b == 0)
    def _():
        bk = bkt_ref[...]
        acc = jnp.full(bk.shape, MASK_VALUE, F32)
        for r in range(REL_BUCKETS):
            acc = jnp.where(bk == r, tab_ref[r * n_heads + h] * LOG2_E, acc)
        bias_ref[0:Q_BLOCK, :] = acc
        bias_ref[Q_BLOCK:, :] = acc

    lp = lam_ref[...]
    lam = (jnp.exp(jnp.sum(lp[0:1, :] * lp[1:2, :], axis=-1, keepdims=True))
           - jnp.exp(jnp.sum(lp[2:3, :] * lp[3:4, :], axis=-1, keepdims=True)) + lambda_init)
    lane = lax.broadcasted_iota(I32, (Q_BLOCK, 2 * HEAD_DIM), 1)
    nt = (((1,), (1,)), ((), ()))

    def scores(i):
        n_keys = (i + 1) * Q_BLOCK
        q = q_ref[i * Q_BLOCK:(i + 1) * Q_BLOCK, :]
        qq = jnp.concatenate([jnp.where(lane < HEAD_DIM, q, jnp.zeros_like(q)),
                              jnp.where(lane >= HEAD_DIM, q, jnp.zeros_like(q))], axis=0)
        return lax.dot_general(qq, k_ref[0:n_keys, :], nt, preferred_element_type=F32) + bias_ref[:, s - n_keys:s]

    n_blocks = s // Q_BLOCK
    sc_next = scores(0)
    for i in range(n_blocks):
        n_keys = (i + 1) * Q_BLOCK
        sc = sc_next
        if i + 1 < n_blocks:
            sc_next = scores(i + 1)
        p = jnp.exp2(sc - jnp.max(sc, axis=-1, keepdims=True))
        denom = jnp.sum(p, axis=-1, keepdims=True)
        pv = _dot(p.astype(BF16), v_ref[0:n_keys, :])
        o = pv[:Q_BLOCK] * (1.0 / denom[:Q_BLOCK]) - pv[Q_BLOCK:] * (lam / denom[Q_BLOCK:])
        o = o * lax.rsqrt(jnp.mean(o * o, axis=-1, keepdims=True) + LN_EPS) * sg_ref[...]
        o_ref[i * Q_BLOCK:(i + 1) * Q_BLOCK, :] = (o * (1.0 - lambda_init)).astype(BF16)


def _attn(q, k, v, lam_p, subln_g, rel_table, lambda_init):
    bsz, s, d = q.shape
    hd2 = 2 * HEAD_DIM
    n_heads = d // hd2
    head = pl.BlockSpec((None, s, hd2), lambda h, b, tab: (b, 0, h))
    grid_spec = pltpu.PrefetchScalarGridSpec(
        num_scalar_prefetch=1,
        grid=(n_heads, bsz),
        in_specs=[
            head, head, head,
            pl.BlockSpec((Q_BLOCK, s), lambda h, b, tab: (0, 0)),
            pl.BlockSpec((4, HEAD_DIM), lambda h, b, tab: (0, 0)),
            pl.BlockSpec((1, hd2), lambda h, b, tab: (0, 0)),
        ],
        out_specs=head,
        scratch_shapes=[pltpu.VMEM((2 * Q_BLOCK, s), F32)],
    )
    return pl.pallas_call(
        functools.partial(_attn_body, lambda_init=lambda_init, n_heads=n_heads),
        grid_spec=grid_spec,
        out_shape=jax.ShapeDtypeStruct((bsz, s, d), BF16),
        compiler_params=_params("arbitrary", "arbitrary"),
        name="attn",
    )(rel_table.reshape(-1), q, k, v, _bucket_strip(s), lam_p, subln_g.reshape(1, hd2))


def _attn_out_body(o_ref, x_ref, mod_ref, wo_ref, lng_ref, lnb_ref, x1_ref, *, alpha):
    y = _dot(o_ref[...], wo_ref[...])
    _residual_epilogue(x_ref[...], y, mod_ref, lng_ref, lnb_ref, alpha, x1_ref)


def _attn_out(o, x, mods_l, w_o, ln_g, ln_b, alpha):
    bsz, s, d = x.shape
    ts = min(PROJ_TILE, s)
    tile = pl.BlockSpec((None, ts, d), lambda b, i: (b, i, 0))
    vec = pl.BlockSpec((1, d), lambda b, i: (0, 0))
    return pl.pallas_call(
        functools.partial(_attn_out_body, alpha=alpha),
        grid=(bsz, s // ts),
        in_specs=[tile, tile, pl.BlockSpec((None, 6, d), lambda b, i: (b, 0, 0)),
                  pl.BlockSpec((d, d), lambda b, i: (0, 0)), vec, vec],
        out_specs=tile,
        out_shape=jax.ShapeDtypeStruct((bsz, s, d), F32),
        compiler_params=_params("parallel", "parallel"),
        name="attn_out",
    )(o, x, mods_l, w_o.astype(BF16), ln_g.reshape(1, d), ln_b.reshape(1, d))


def _router_body(x1_ref, mod_ref, whi_ref, wlo_ref, b_ref, tri_ref, idx_ref, gate_ref, rank_ref, cnt_ref):
    h = _moe_input(x1_ref, mod_ref)
    h_hi = h.astype(BF16)
    h_lo = (h - h_hi.astype(F32)).astype(BF16)
    nt = (((1,), (1,)), ((), ()))
    logits = (lax.dot_general(whi_ref[...], h_hi, nt, preferred_element_type=F32)
              + (lax.dot_general(whi_ref[...], h_lo, nt, preferred_element_type=F32)
                 + lax.dot_general(wlo_ref[...], h_hi, nt, preferred_element_type=F32))) + b_ref[...]
    n_exp, tr = logits.shape
    eio = lax.broadcasted_iota(I32, (n_exp, tr), 0)
    work = logits
    vals, idxs = [], []
    for _ in range(TOP_K):
        m = jnp.max(work, axis=0, keepdims=True)
        am = jnp.min(jnp.where(work == m, eio, n_exp), axis=0, keepdims=True)
        vals.append(m)
        idxs.append(am)
        work = jnp.where(eio == am, -jnp.inf, work)
    ex = [jnp.exp(v - vals[0]) for v in vals]
    den = ex[0] + ex[1] + ex[2] + ex[3]
    onehot = jnp.zeros((n_exp, tr), F32)
    for k in range(TOP_K):
        onehot = onehot + (eio == idxs[k]).astype(F32)
    before = _dot(onehot.astype(BF16), tri_ref[...])
    for k in range(TOP_K):
        idx_ref[k:k + 1, :] = idxs[k]
        gate_ref[k:k + 1, :] = ex[k] / den
        rank_ref[k:k + 1, :] = jnp.sum(jnp.where(eio == idxs[k], before, 0.0), axis=0, keepdims=True).astype(I32)
    tl = tr // cnt_ref.shape[0]
    for j in range(cnt_ref.shape[0]):
        cnt_ref[j] = jnp.sum(onehot[:, j * tl:(j + 1) * tl], axis=1, keepdims=True).astype(I32)


def _router(x1, mods_l, seq_len, w_r, b_r):
    t, d = x1.shape
    n_exp = w_r.shape[1]
    tl = min(MOE_TOKENS, t)
    tr = min(ROUTER_TOKENS, seq_len)
    per_seq = seq_len // tr
    wt = w_r.T
    wt_hi = wt.astype(BF16)
    wt_lo = (wt - wt_hi.astype(F32)).astype(BF16)
    pos = jnp.arange(tr, dtype=I32)
    tri = ((pos[:, None] < pos[None, :]) & (pos[:, None] // tl == pos[None, :] // tl)).astype(BF16)
    tok = pl.BlockSpec((TOP_K, tr), lambda i: (0, i))
    return pl.pallas_call(
        _router_body,
        grid=(t // tr,),
        in_specs=[
            pl.BlockSpec((tr, d), lambda i: (i, 0)),
            pl.BlockSpec((None, 6, d), lambda i: (i // per_seq, 0, 0)),
            pl.BlockSpec((n_exp, d), lambda i: (0, 0)),
            pl.BlockSpec((n_exp, d), lambda i: (0, 0)),
            pl.BlockSpec((n_exp, 1), lambda i: (0, 0)),
            pl.BlockSpec((tr, tr), lambda i: (0, 0)),
        ],
        out_specs=[tok, tok, tok, pl.BlockSpec((tr // tl, n_exp, 1), lambda i: (i, 0, 0))],
        out_shape=[jax.ShapeDtypeStruct((TOP_K, t), I32), jax.ShapeDtypeStruct((TOP_K, t), F32),
                   jax.ShapeDtypeStruct((TOP_K, t), I32), jax.ShapeDtypeStruct((t // tl, n_exp, 1), I32)],
        compiler_params=_params("parallel"),
        name="router",
    )(x1, mods_l, wt_hi, wt_lo, b_r.reshape(n_exp, 1), tri)


def _rows(ref, start, cnt):
    aligned = lambda v: v if isinstance(v, int) else pl.multiple_of(v, SUBLANES)
    return ref.at[pl.ds(aligned(start), aligned(cnt))]


def _segment_copy(src, src_row, dst, dst_row, cnt, sem):
    @pl.when(cnt > 0)
    def _():
        pltpu.make_async_copy(_rows(src, src_row, cnt), _rows(dst, dst_row, cnt), sem).start()


def _wait_rows(ref, cnt, sem):
    @pl.when(cnt > 0)
    def _():
        pltpu.make_async_copy(_rows(ref, 0, cnt), _rows(ref, 0, cnt), sem).wait()


def _zero_fill(zero_ref, xs_hbm, start, cnt, sem):
    zr = zero_ref.shape[0]
    n_full = cnt // zr

    def full(r, c):
        _segment_copy(zero_ref, 0, xs_hbm, start + r * zr, zr, sem)
        return c

    lax.fori_loop(0, n_full, full, 0)
    _segment_copy(zero_ref, 0, xs_hbm, start + n_full * zr, cnt - n_full * zr, sem)
    _wait_rows(xs_hbm, cnt, sem)


def _pack_halves(x):
    hd = x.shape[1] // 2
    lo = lax.bitcast_convert_type(x[:, :hd], U32)
    hi = lax.bitcast_convert_type(x[:, hd:], U32)
    return (lo >> 16) | (hi & jnp.uint32(0xFFFF0000))


def _unpack_halves(words):
    lo = lax.bitcast_convert_type(words << 16, F32).astype(BF16)
    hi = lax.bitcast_convert_type(words & jnp.uint32(0xFFFF0000), F32).astype(BF16)
    return lo, hi


def _one_hot_hits(iota, pos_of):
    hit = iota == pos_of(0)
    for k in range(1, TOP_K):
        hit = hit | (iota == pos_of(k))
    return hit


def _dispatch_body(goff_ref, loff_ref, cp_ref, ltot_ref, pad_start_ref, pad_cnt_ref, x1_ref, mod_ref, lp_ref,
                   gate_ref, xs_hbm, xl_ref, zero_ref, sem, zsem, *, n_exp):
    j = pl.program_id(0)
    tl, d = x1_ref.shape
    lr = xl_ref.shape[1]

    @pl.when(j == 0)
    def _():
        zero_ref[...] = jnp.zeros_like(zero_ref)

        def per_range(e, carry):
            _zero_fill(zero_ref, xs_hbm, pad_start_ref[e], pad_cnt_ref[e], zsem)
            return carry
        lax.fori_loop(0, pad_start_ref.shape[0], per_range, 0)

    def start_segments(tile):
        def per_expert(e, carry):
            seg = tile * n_exp + e
            _segment_copy(xl_ref.at[tile % 2], loff_ref[seg], xs_hbm, goff_ref[seg], cp_ref[seg], sem.at[tile % 2])
            return carry
        lax.fori_loop(0, n_exp, per_expert, 0)

    def wait_segments(tile):
        _wait_rows(xs_hbm, ltot_ref[tile], sem.at[tile % 2])

    @pl.when(j >= 2)
    def _():
        wait_segments(j - 2)

    buf = xl_ref.at[j % 2]
    riota = lax.broadcasted_iota(I32, (lr, tl), 0)
    perm = _one_hot_hits(riota, lambda k: lp_ref[k:k + 1, :]).astype(F32).astype(BF16)
    rows = _dot(perm, _moe_input(x1_ref, mod_ref).astype(BF16))
    buf[:, 0:d // 2] = _pack_halves(rows)
    gsel = jnp.zeros((lr, tl), F32)
    for k in range(TOP_K):
        gsel = gsel + jnp.where(riota == lp_ref[k:k + 1, :], gate_ref[k:k + 1, :], 0.0)
    rowg = jnp.broadcast_to(jnp.sum(gsel, axis=1, keepdims=True), (lr, LANES))
    buf[:, d // 2:] = lax.bitcast_convert_type(rowg, U32)
    start_segments(j)

    @pl.when(j == pl.num_programs(0) - 1)
    def _():
        @pl.when(j >= 1)
        def _():
            wait_segments(j - 1)
        wait_segments(j)


def _dispatch(x1, mods_l, seq_len, lp, gates, goff, loff, cp, ltot, pad_start, pad_cnt, n_rows):
    t, d = x1.shape
    tl = min(MOE_TOKENS, t)
    per_seq = seq_len // tl
    n_exp = goff.shape[0] // (t // tl)
    lr = TOP_K * tl + n_exp * SUBLANES
    tok = pl.BlockSpec((TOP_K, tl), lambda j, *_: (0, j))
    grid_spec = pltpu.PrefetchScalarGridSpec(
        num_scalar_prefetch=6,
        grid=(t // tl,),
        in_specs=[pl.BlockSpec((tl, d), lambda j, *_: (j, 0)),
                  pl.BlockSpec((None, 6, d), lambda j, *_: (j // per_seq, 0, 0)), tok, tok],
        out_specs=pl.BlockSpec(memory_space=pl.ANY),
        scratch_shapes=[pltpu.VMEM((2, lr, d // 2 + LANES), U32), pltpu.VMEM((MOE_TILE, d // 2 + LANES), U32),
                        pltpu.SemaphoreType.DMA((2,)), pltpu.SemaphoreType.DMA],
    )
    return pl.pallas_call(
        functools.partial(_dispatch_body, n_exp=n_exp),
        grid_spec=grid_spec,
        out_shape=jax.ShapeDtypeStruct((n_rows, d // 2 + LANES), U32),
        compiler_params=_params("arbitrary"),
        name="dispatch",
    )(goff, loff, cp, ltot, pad_start, pad_cnt, x1, mods_l, lp, gates)


def _experts_body(be_ref, bsrc_ref, nvalid_ref, first_ref, slot_ref, next_ref, xs_ref, wgu_hbm, bgu_ref, wdn_hbm,
                  bdn_ref, ys_ref, wgu_f32, wdn_f32, wgu_bf, wdn_bf, sem_gu, sem_dn, *, layer):
    i = pl.program_id(0)
    e = be_ref[i]
    d, f2 = wgu_bf.shape
    f = f2 // 2

    def fetch(expert, slot, start):
        for hbm, buf, sem in ((wgu_hbm, wgu_f32, sem_gu), (wdn_hbm, wdn_f32, sem_dn)):
            dma = pltpu.make_async_copy(hbm.at[layer, expert], buf.at[slot], sem.at[slot])
            dma.start() if start else dma.wait()

    @pl.when(first_ref[i] == 1)
    def _():
        slot = slot_ref[i]

        @pl.when(i == 0)
        def _():
            fetch(e, slot, True)

        fetch(e, slot, False)

        @pl.when(next_ref[i] >= 0)
        def _():
            fetch(next_ref[i], 1 - slot, True)

        def cast(ref_in, ref_out):
            def step(r, c):
                rows = pl.ds(pl.multiple_of(r * CAST_ROWS, CAST_ROWS), CAST_ROWS)
                ref_out[rows, :] = ref_in[rows, :].astype(BF16)
                return c
            lax.fori_loop(0, ref_in.shape[0] // CAST_ROWS, step, 0)
        cast(wgu_f32.at[slot], wgu_bf)
        cast(wdn_f32.at[slot], wdn_bf)

    @pl.when(i < nvalid_ref[0])
    def _():
        hd = d // 2
        x_lo, x_hi = _unpack_halves(xs_ref[:, 0:hd])
        proj = lambda cols: _dot(x_lo, wgu_bf[0:hd, cols]) + _dot(x_hi, wgu_bf[hd:, cols]) + bgu_ref[:, cols]
        gate = jnp.minimum(proj(slice(0, f)), SWIGLU_LIMIT)
        lin = jnp.clip(proj(slice(f, f2)), -SWIGLU_LIMIT, SWIGLU_LIMIT)
        act = (gate * jax.nn.sigmoid(SWIGLU_ALPHA * gate) * (lin + 1.0)).astype(BF16)
        row_gate = lax.bitcast_convert_type(xs_ref[:, hd:hd + 1], F32)
        y = (_dot(act, wdn_bf[...]) + bdn_ref[...]) * row_gate
        ys_ref[...] = _pack_halves(y.astype(BF16).astype(F32))

    @pl.when(i >= nvalid_ref[0])
    def _():
        ys_ref[...] = jnp.zeros_like(ys_ref)


def _experts(xs, block_e, block_src, n_valid, block_first, block_slot, block_next, w_gu, b_gu, w_dn, b_dn, layer):
    n_rows, dx = xs.shape
    depth, n_exp, d, f2 = w_gu.shape
    f = f2 // 2
    tm = MOE_TILE
    grid_spec = pltpu.PrefetchScalarGridSpec(
        num_scalar_prefetch=6,
        grid=(n_rows // tm,),
        in_specs=[
            pl.BlockSpec((tm, dx), lambda i, be, bs, *_: (bs[i], 0)),
            pl.BlockSpec(memory_space=pl.ANY),
            pl.BlockSpec((None, None, 1, f2), lambda i, be, *_: (layer, be[i], 0, 0)),
            pl.BlockSpec(memory_space=pl.ANY),
            pl.BlockSpec((None, None, 1, d), lambda i, be, *_: (layer, be[i], 0, 0)),
        ],
        out_specs=pl.BlockSpec((tm, d // 2), lambda i, *_: (i, 0)),
        scratch_shapes=[pltpu.VMEM((2, d, f2), F32), pltpu.VMEM((2, f, d), F32),
                        pltpu.VMEM((d, f2), BF16), pltpu.VMEM((f, d), BF16),
                        pltpu.SemaphoreType.DMA((2,)), pltpu.SemaphoreType.DMA((2,))],
    )
    return pl.pallas_call(
        functools.partial(_experts_body, layer=layer),
        grid_spec=grid_spec,
        out_shape=jax.ShapeDtypeStruct((n_rows, d // 2), U32),
        compiler_params=_params("arbitrary"),
        name="experts",
    )(block_e, block_src, n_valid, block_first, block_slot, block_next, xs, w_gu,
      b_gu.reshape(depth, n_exp, 1, f2), w_dn, b_dn.reshape(depth, n_exp, 1, d))


def _combine_body(goff_ref, loff_ref, cp_ref, ltot_ref, ys_hbm, x_ref, lp_ref, mod_ref, lng_ref, lnb_ref, o_ref,
                  yl_ref, sem, *, n_exp, alpha):
    j = pl.program_id(0)
    tl = x_ref.shape[0]
    lr, hd = yl_ref.shape[1:]

    def fetch(tile):
        buf = yl_ref.at[tile % 2]

        def per_expert(e, carry):
            seg = tile * n_exp + e
            _segment_copy(ys_hbm, goff_ref[seg], buf, loff_ref[seg], cp_ref[seg], sem.at[tile % 2])
            return carry
        lax.fori_loop(0, n_exp, per_expert, 0)

        def zero_rows(r, carry):
            buf[pl.ds(pl.multiple_of(r * SUBLANES, SUBLANES), SUBLANES), :] = jnp.zeros((SUBLANES, hd), U32)
            return carry
        lax.fori_loop(ltot_ref[tile] // SUBLANES, lr // SUBLANES, zero_rows, 0)

    @pl.when(j == 0)
    def _():
        fetch(j)

    @pl.when(j + 1 < pl.num_programs(0))
    def _():
        fetch(j + 1)

    liota = lax.broadcasted_iota(I32, (tl, lr), 1)
    pick = _one_hot_hits(liota, lambda k: lp_ref[:, k:k + 1]).astype(F32).astype(BF16)
    _wait_rows(ys_hbm, ltot_ref[j], sem.at[j % 2])
    y_lo, y_hi = _unpack_halves(yl_ref[j % 2])
    y = jnp.concatenate([_dot(pick, y_lo), _dot(pick, y_hi)], axis=1)
    o_ref[...] = _layer_norm(alpha * x_ref[...] + mod_ref[5:6, :] * y, lng_ref[...], lnb_ref[...])


def _combine(ys, lp_t, goff, loff, cp, ltot, x1, mods_l, ln_g, ln_b, alpha, seq_len):
    t, d = x1.shape
    tl = min(MOE_TOKENS, t)
    n_exp = goff.shape[0] // (t // tl)
    lr = TOP_K * tl + n_exp * SUBLANES
    per_seq = seq_len // tl
    vec = pl.BlockSpec((1, d), lambda j, *_: (0, 0))
    grid_spec = pltpu.PrefetchScalarGridSpec(
        num_scalar_prefetch=4,
        grid=(t // tl,),
        in_specs=[
            pl.BlockSpec(memory_space=pl.ANY),
            pl.BlockSpec((tl, d), lambda j, *_: (j, 0)),
            pl.BlockSpec((tl, TOP_K), lambda j, *_: (j, 0)),
            pl.BlockSpec((None, 6, d), lambda j, *_: (j // per_seq, 0, 0)),
            vec, vec,
        ],
        out_specs=pl.BlockSpec((tl, d), lambda j, *_: (j, 0)),
        scratch_shapes=[pltpu.VMEM((2, lr, d // 2), U32), pltpu.SemaphoreType.DMA((2,))],
    )
    return pl.pallas_call(
        functools.partial(_combine_body, n_exp=n_exp, alpha=alpha),
        grid_spec=grid_spec,
        out_shape=jax.ShapeDtypeStruct((t, d), F32),
        compiler_params=_params("arbitrary"),
        name="combine",
    )(goff, loff, cp, ltot, ys, x1, lp_t, mods_l, ln_g.reshape(1, d), ln_b.reshape(1, d))


def _moe_layer(x1, mods_l, w_r, b_r, w_gu, b_gu, w_dn, b_dn, layer, ln_g, ln_b, alpha, seq_len):
    t, d = x1.shape
    n_exp = w_r.shape[1]
    tm = MOE_TILE
    tl = min(MOE_TOKENS, t)
    n_tiles = t // tl
    idx, gates, rank, cnt = _router(x1, mods_l, seq_len, w_r, b_r)
    cp = (cnt[:, :, 0] + SUBLANES - 1) // SUBLANES * SUBLANES
    tot = jnp.sum(cp, axis=0)
    padded = (tot + tm - 1) // tm * tm
    pend = jnp.cumsum(padded)
    pstart = pend - padded
    goff = pstart[None, :] + jnp.cumsum(cp, axis=0) - cp
    loff = jnp.cumsum(cp, axis=1) - cp
    ltot = jnp.sum(cp, axis=1).astype(I32)
    is_e = idx[..., None] == jnp.arange(n_exp, dtype=I32)
    lp = jnp.sum(jnp.where(is_e, jnp.repeat(loff, tl, axis=0)[None], 0), axis=-1) + rank
    n_blocks = -(-(t * TOP_K + n_tiles * n_exp * (SUBLANES - 1)) // tm) + n_exp
    n_valid = pend[-1] // tm
    blk = jnp.arange(n_blocks, dtype=I32)
    block_src = jnp.minimum(blk, n_valid - 1)
    block_e = jnp.sum(pend[None, :] <= (block_src * tm)[:, None], axis=1).astype(I32)
    used = padded > 0
    experts = jnp.arange(n_exp, dtype=I32)
    next_used = jnp.flip(lax.cummin(jnp.flip(jnp.where(used, experts, n_exp))))
    next_used = jnp.concatenate([next_used[1:], jnp.full((1,), n_exp, I32)])
    next_used = jnp.where(next_used < n_exp, next_used, -1)
    slot_of = (jnp.cumsum(used.astype(I32)) - 1) % 2
    of_block = lambda table: jnp.sum(jnp.where(block_e[:, None] == experts[None, :], table[None, :], 0), axis=1)
    block_first = ((blk * tm == of_block(pstart)) & (blk < n_valid)).astype(I32)
    block_slot = of_block(slot_of).astype(I32)
    block_next = of_block(next_used).astype(I32)
    n_rows = n_blocks * tm
    pad_start = jnp.concatenate([pstart + tot, pend[-1:]]).astype(I32)
    pad_cnt = jnp.concatenate([padded - tot, n_rows - pend[-1:]]).astype(I32)
    flat = lambda a: a.reshape(-1).astype(I32)
    seg = (flat(goff), flat(loff), flat(cp), ltot)
    xs = _dispatch(x1, mods_l, seq_len, lp, gates, *seg, pad_start, pad_cnt, n_rows)
    ys = _experts(xs, block_e, block_src.astype(I32), n_valid.reshape(1).astype(I32), block_first, block_slot,
                  block_next, w_gu, b_gu, w_dn, b_dn, layer)
    return _combine(ys, lp.T, *seg, x1, mods_l, ln_g, ln_b, alpha, seq_len)


def kernel(x, c, ada_w, ada_b, post_ln_g, post_ln_b, conv_w_pw1, conv_b_pw1, conv_w_dw, conv_b_dw, conv_ln_g, conv_ln_b, conv_w_pw2, conv_b_pw2, w_kv, attn_w_q, attn_lambda, attn_subln_g, attn_w_o, rel_bias_table, router_w, router_b, expert_w_gate_up, expert_b_gate_up, expert_w_down, expert_b_down):
    bsz, s, d = x.shape
    depth = ada_w.shape[0]
    n_a = depth // 2
    alpha = (2 * depth) ** 0.25
    mods = _ada(c, ada_w, ada_b).reshape(depth, bsz, 6, d)
    q = k = v = None
    for l in range(depth):
        mods_l = mods[l]
        if l < n_a:
            x1 = _conv(x, mods_l, conv_w_pw1[l], conv_b_pw1[l], conv_w_dw[l], conv_b_dw[l], conv_ln_g[l],
                       conv_ln_b[l], conv_w_pw2[l], conv_b_pw2[l], post_ln_g[l, 0], post_ln_b[l, 0], alpha)
        else:
            j = l - n_a
            if j == 0:
                q, k, v = _qkv(x, mods_l, w_kv, attn_w_q[j])
            else:
                q = _qkv(x, mods_l, w_kv, attn_w_q[j])[0]
            lambda_init = 0.8 - 0.6 * math.exp(-0.3 * l)
            o = _attn(q, k, v, attn_lambda[j], attn_subln_g[j], rel_bias_table, lambda_init)
            x1 = _attn_out(o, x, mods_l, attn_w_o[j], post_ln_g[l, 0], post_ln_b[l, 0], alpha)
        x = _moe_layer(x1.reshape(bsz * s, d), mods_l, router_w[l], router_b[l],
                       expert_w_gate_up, expert_b_gate_up, expert_w_down, expert_b_down, l,
                       post_ln_g[l, 1], post_ln_b[l, 1], alpha, s).reshape(bsz, s, d)
    return x
```

```python
import functools
import math

import jax
import jax.numpy as jnp
from jax import lax
from jax.experimental import pallas as pl
from jax.experimental.pallas import tpu as pltpu

F32 = jnp.float32
BF16 = jnp.bfloat16
I32 = jnp.int32
U32 = jnp.uint32
HIGHEST = lax.Precision.HIGHEST

CHUNK = 64
CONV_WIDTH = 31
HEAD_DIM = 64
REL_BUCKETS = 32
REL_MAX_DIST = 128
TOP_K = 4
SWIGLU_LIMIT = 7.0
SWIGLU_ALPHA = 1.702
LN_EPS = 1e-5
MASK_VALUE = -1e30
LOG2_E = math.log2(math.e)

SUBLANES = 8
LANES = 128
VMEM_LIMIT_BYTES = 56 * 1024 * 1024

ADA_TN = 1024
SEQ_TILE = 256
PROJ_TILE = 512
CONV_HALO = 32
CONV_ROWS = 64
CONV_COLS = 256
MOE_TOKENS = 256
ROUTER_TOKENS = 1024
MOE_TILE = 512
Q_BLOCK = 128
CAST_ROWS = 128


def _params(*sem):
    return pltpu.CompilerParams(dimension_semantics=sem, vmem_limit_bytes=VMEM_LIMIT_BYTES)


def _layer_norm(x, g, b):
    mu = jnp.mean(x, axis=-1, keepdims=True)
    xc = x - mu
    var = jnp.mean(xc * xc, axis=-1, keepdims=True)
    return xc * lax.rsqrt(var + LN_EPS) * g + b


def _dot(a, b):
    return jnp.dot(a, b, preferred_element_type=F32)


def _ada_body(c_ref, w_ref, b_ref, o_ref):
    c = c_ref[...]
    cond = c * jax.nn.sigmoid(c)
    o_ref[...] = jnp.dot(cond, w_ref[...], preferred_element_type=F32, precision=HIGHEST) + b_ref[...]


def _ada(c, ada_w, ada_b):
    depth, d, n = ada_w.shape
    bsz = c.shape[0]
    tn = min(ADA_TN, n)
    return pl.pallas_call(
        _ada_body,
        grid=(depth, n // tn),
        in_specs=[
            pl.BlockSpec((bsz, d), lambda l, j: (0, 0)),
            pl.BlockSpec((None, d, tn), lambda l, j: (l, 0, j)),
            pl.BlockSpec((None, 1, tn), lambda l, j: (l, 0, j)),
        ],
        out_specs=pl.BlockSpec((None, bsz, tn), lambda l, j: (l, 0, j)),
        out_shape=jax.ShapeDtypeStruct((depth, bsz, n), F32),
        compiler_params=_params("parallel", "parallel"),
        name="ada",
    )(c, ada_w, ada_b.reshape(depth, 1, n))


def _residual_epilogue(x, y, mod_ref, lng_ref, lnb_ref, alpha, x1_ref):
    x1_ref[...] = _layer_norm(alpha * x + mod_ref[2:3, :] * y, lng_ref[...], lnb_ref[...])


def _moe_input(x1_ref, mod_ref):
    return x1_ref[...] * (1.0 + mod_ref[4:5, :]) + mod_ref[3:4, :]


def _conv_body(x_ref, halo_ref, mod_ref, w1_ref, b1_ref, wdw_ref, bdw_ref, cg_ref, cb_ref, w2_ref, b2_ref,
               lng_ref, lnb_ref, x1_ref, win_ref, v_ref, *, alpha):
    ts, d = x_ref.shape
    i = pl.program_id(1)
    xw = jnp.concatenate([halo_ref[...], x_ref[...]], axis=0)
    h = (xw * (1.0 + mod_ref[1:2, :]) + mod_ref[0:1, :]).astype(BF16)
    rows = min(CONV_ROWS, ts)
    cols = min(CONV_COLS, d)

    def glu(c0):
        a = _dot(h, w1_ref[:, c0:c0 + cols]) + b1_ref[:, c0:c0 + cols]
        g = _dot(h, w1_ref[:, d + c0:d + c0 + cols]) + b1_ref[:, d + c0:d + c0 + cols]
        return a * jax.nn.sigmoid(g)

    off = CONV_HALO - (CONV_WIDTH - 1)
    n_shift = ts + CONV_HALO - SUBLANES
    in_seq = (lax.broadcasted_iota(I32, (ts + CONV_HALO, cols), 0) >= CONV_HALO) | (i > 0)
    u_next = glu(0)
    for c0 in range(0, d, cols):
        chan = slice(c0, c0 + cols)
        win_ref[0, :, chan] = jnp.where(in_seq, u_next, 0.0)
        if c0 + cols < d:
            u_next = glu(c0 + cols)
        for b in range(1, SUBLANES):
            win_ref[b, 0:n_shift, chan] = win_ref[0, b:b + n_shift, chan]
        for r0 in range(0, ts, rows):
            acc = jnp.zeros((rows, cols), F32)
            for j in range(CONV_WIDTH):
                a, b = divmod(off + j, SUBLANES)
                r = r0 + a * SUBLANES
                acc = acc + wdw_ref[j:j + 1, c0:c0 + cols] * win_ref[b, r:r + rows, c0:c0 + cols]
            v_ref[r0:r0 + rows, c0:c0 + cols] = acc
    v = _layer_norm(v_ref[...] + bdw_ref[...], cg_ref[...], cb_ref[...])
    v = (v * jax.nn.sigmoid(v)).astype(BF16)
    y = _dot(v, w2_ref[...]) + b2_ref[...]
    _residual_epilogue(x_ref[...], y, mod_ref, lng_ref, lnb_ref, alpha, x1_ref)


def _conv(x, mods_l, w_pw1, b_pw1, w_dw, b_dw, cln_g, cln_b, w_pw2, b_pw2, ln_g, ln_b, alpha):
    bsz, s, d = x.shape
    ts = min(SEQ_TILE, s)
    hb = ts // CONV_HALO
    row = lambda a: a.reshape(1, d)
    tile = pl.BlockSpec((None, ts, d), lambda b, i: (b, i, 0))
    vec = pl.BlockSpec((1, d), lambda b, i: (0, 0))
    return pl.pallas_call(
        functools.partial(_conv_body, alpha=alpha),
        grid=(bsz, s // ts),
        in_specs=[
            tile,
            pl.BlockSpec((None, CONV_HALO, d), lambda b, i: (b, jnp.maximum(i * hb - 1, 0), 0)),
            pl.BlockSpec((None, 6, d), lambda b, i: (b, 0, 0)),
            pl.BlockSpec((d, 2 * d), lambda b, i: (0, 0)),
            pl.BlockSpec((1, 2 * d), lambda b, i: (0, 0)),
            pl.BlockSpec((CONV_WIDTH, d), lambda b, i: (0, 0)),
            vec, vec, vec,
            pl.BlockSpec((d, d), lambda b, i: (0, 0)),
            vec, vec, vec,
        ],
        out_specs=tile,
        out_shape=jax.ShapeDtypeStruct((bsz, s, d), F32),
        scratch_shapes=[pltpu.VMEM((SUBLANES, ts + CONV_HALO, d), F32), pltpu.VMEM((ts, d), F32)],
        compiler_params=_params("parallel", "parallel"),
        name="conv",
    )(x, x, mods_l, w_pw1.astype(BF16), b_pw1.reshape(1, 2 * d), w_dw, row(b_dw), row(cln_g), row(cln_b),
      w_pw2.astype(BF16), row(b_pw2), row(ln_g), row(ln_b))


def _qkv_body(x_ref, mod_ref, wkv_ref, wq_ref, q_ref, k_ref, vt_ref):
    d = x_ref.shape[-1]
    x = x_ref[...]
    kv = _dot(x.astype(BF16), wkv_ref[...])
    k_ref[...] = kv[:, :d].astype(BF16)
    hd2 = 2 * HEAD_DIM
    for c0 in range(0, d, hd2):
        vt_ref[c0:c0 + hd2, :] = kv[:, d + c0:d + c0 + hd2].T.astype(BF16)
    h = (x * (1.0 + mod_ref[1:2, :]) + mod_ref[0:1, :]).astype(BF16)
    q_ref[...] = (_dot(h, wq_ref[...]) * (HEAD_DIM ** -0.5 * LOG2_E)).astype(BF16)


def _qkv(x, mods_l, w_kv, w_q):
    bsz, s, d = x.shape
    ts = min(PROJ_TILE, s)
    tile = pl.BlockSpec((None, ts, d), lambda b, i: (b, i, 0))
    return pl.pallas_call(
        _qkv_body,
        grid=(bsz, s // ts),
        in_specs=[
            tile,
            pl.BlockSpec((None, 6, d), lambda b, i: (b, 0, 0)),
            pl.BlockSpec((d, 2 * d), lambda b, i: (0, 0)),
            pl.BlockSpec((d, d), lambda b, i: (0, 0)),
        ],
        out_specs=[tile, tile, pl.BlockSpec((None, d, ts), lambda b, i: (b, 0, i))],
        out_shape=[jax.ShapeDtypeStruct((bsz, s, d), BF16)] * 2 + [jax.ShapeDtypeStruct((bsz, d, s), BF16)],
        compiler_params=_params("parallel", "parallel"),
        name="qkv",
    )(x, mods_l, w_kv.astype(BF16), w_q.astype(BF16))


def _t5_bucket(rel):
    nb = REL_BUCKETS // 2
    ret = jnp.where(rel > 0, nb, 0)
    n = jnp.abs(rel)
    max_exact = nb // 2
    large = max_exact + (jnp.log(jnp.maximum(n, 1).astype(F32) / max_exact)
                         / math.log(REL_MAX_DIST / max_exact) * (nb - max_exact)).astype(I32)
    large = jnp.minimum(large, nb - 1)
    return ret + jnp.where(n < max_exact, n, large)


def _bucket_strip(s):
    r = jnp.arange(Q_BLOCK, dtype=I32)[:, None]
    kp = jnp.arange(s, dtype=I32)[None, :] - (s - Q_BLOCK)
    bucket = _t5_bucket(kp - r)
    visible = jnp.floor_divide(kp, CHUNK) <= (r // CHUNK)
    return jnp.where(visible, bucket, REL_BUCKETS)


def _attn_body(tab_ref, q_ref, k_ref, vt_ref, bkt_ref, lam_ref, sg_ref, ot_ref, bias_ref, *, lambda_init, n_heads):
    h = pl.program_id(0)
    b = pl.program_id(1)
    s = q_ref.shape[0]

    @pl.when(b == 0)
    def _():
        bk = bkt_ref[...]
        acc = jnp.full(bk.shape, MASK_VALUE, F32)
        for r in range(REL_BUCKETS):
            acc = jnp.where(bk == r, tab_ref[r * n_heads + h] * LOG2_E, acc)
        bias_ref[:, 0:Q_BLOCK] = acc
        bias_ref[:, Q_BLOCK:] = acc

    lp = lam_ref[...]
    lam = (jnp.exp(jnp.sum(lp[0:1, :] * lp[1:2, :], axis=-1, keepdims=True))
           - jnp.exp(jnp.sum(lp[2:3, :] * lp[3:4, :], axis=-1, keepdims=True)) + lambda_init)
    lane = lax.broadcasted_iota(I32, (Q_BLOCK, 2 * HEAD_DIM), 1)
    nt = (((1,), (1,)), ((), ()))

    def scores(i):
        n_keys = (i + 1) * Q_BLOCK
        q = q_ref[i * Q_BLOCK:(i + 1) * Q_BLOCK, :]
        qq = jnp.concatenate([jnp.where(lane < HEAD_DIM, q, jnp.zeros_like(q)),
                              jnp.where(lane >= HEAD_DIM, q, jnp.zeros_like(q))], axis=0)
        return lax.dot_general(k_ref[0:n_keys, :], qq, nt, preferred_element_type=F32) + bias_ref[s - n_keys:s, :]

    n_blocks = s // Q_BLOCK
    sc_next = scores(0)
    for i in range(n_blocks):
        n_keys = (i + 1) * Q_BLOCK
        sc = sc_next
        if i + 1 < n_blocks:
            sc_next = scores(i + 1)
        p = jnp.exp2(sc - jnp.max(sc, axis=0, keepdims=True))
        denom = jnp.sum(p, axis=0, keepdims=True)
        pv = _dot(vt_ref[:, 0:n_keys], p.astype(BF16))
        o = pv[:, :Q_BLOCK] * (1.0 / denom[:, :Q_BLOCK]) - pv[:, Q_BLOCK:] * (lam / denom[:, Q_BLOCK:])
        o = o * lax.rsqrt(jnp.mean(o * o, axis=0, keepdims=True) + LN_EPS) * sg_ref[...]
        ot_ref[:, i * Q_BLOCK:(i + 1) * Q_BLOCK] = (o * (1.0 - lambda_init)).astype(BF16)


def _attn(q, k, vt, lam_p, subln_g, rel_table, lambda_init):
    bsz, s, d = q.shape
    hd2 = 2 * HEAD_DIM
    n_heads = d // hd2
    head = pl.BlockSpec((None, s, hd2), lambda h, b, tab: (b, 0, h))
    head_t = pl.BlockSpec((None, hd2, s), lambda h, b, tab: (b, h, 0))
    grid_spec = pltpu.PrefetchScalarGridSpec(
        num_scalar_prefetch=1,
        grid=(n_heads, bsz),
        in_specs=[
            head, head, head_t,
            pl.BlockSpec((s, Q_BLOCK), lambda h, b, tab: (0, 0)),
            pl.BlockSpec((4, HEAD_DIM), lambda h, b, tab: (0, 0)),
            pl.BlockSpec((hd2, 1), lambda h, b, tab: (0, 0)),
        ],
        out_specs=head_t,
        scratch_shapes=[pltpu.VMEM((s, 2 * Q_BLOCK), F32)],
    )
    return pl.pallas_call(
        functools.partial(_attn_body, lambda_init=lambda_init, n_heads=n_heads),
        grid_spec=grid_spec,
        out_shape=jax.ShapeDtypeStruct((bsz, d, s), BF16),
        compiler_params=_params("arbitrary", "arbitrary"),
        name="attn",
    )(rel_table.reshape(-1), q, k, vt, _bucket_strip(s).T, lam_p, subln_g.reshape(hd2, 1))


def _attn_out_body(ot_ref, x_ref, mod_ref, wo_ref, lng_ref, lnb_ref, x1_ref, *, alpha):
    y = lax.dot_general(ot_ref[...], wo_ref[...], (((0,), (0,)), ((), ())), preferred_element_type=F32)
    _residual_epilogue(x_ref[...], y, mod_ref, lng_ref, lnb_ref, alpha, x1_ref)


def _attn_out(o, x, mods_l, w_o, ln_g, ln_b, alpha):
    bsz, s, d = x.shape
    ts = min(PROJ_TILE, s)
    tile = pl.BlockSpec((None, ts, d), lambda b, i: (b, i, 0))
    vec = pl.BlockSpec((1, d), lambda b, i: (0, 0))
    return pl.pallas_call(
        functools.partial(_attn_out_body, alpha=alpha),
        grid=(bsz, s // ts),
        in_specs=[pl.BlockSpec((None, d, ts), lambda b, i: (b, 0, i)), tile,
                  pl.BlockSpec((None, 6, d), lambda b, i: (b, 0, 0)),
                  pl.BlockSpec((d, d), lambda b, i: (0, 0)), vec, vec],
        out_specs=tile,
        out_shape=jax.ShapeDtypeStruct((bsz, s, d), F32),
        compiler_params=_params("parallel", "parallel"),
        name="attn_out",
    )(o, x, mods_l, w_o.astype(BF16), ln_g.reshape(1, d), ln_b.reshape(1, d))


def _router_body(x1_ref, mod_ref, whi_ref, wlo_ref, b_ref, tri_ref, idx_ref, gate_ref, rank_ref, cnt_ref):
    h = _moe_input(x1_ref, mod_ref)
    h_hi = h.astype(BF16)
    h_lo = (h - h_hi.astype(F32)).astype(BF16)
    nt = (((1,), (1,)), ((), ()))
    logits = (lax.dot_general(whi_ref[...], h_hi, nt, preferred_element_type=F32)
              + (lax.dot_general(whi_ref[...], h_lo, nt, preferred_element_type=F32)
                 + lax.dot_general(wlo_ref[...], h_hi, nt, preferred_element_type=F32))) + b_ref[...]
    n_exp, tr = logits.shape
    eio = lax.broadcasted_iota(I32, (n_exp, tr), 0)
    work = logits
    vals, idxs = [], []
    for _ in range(TOP_K):
        m = jnp.max(work, axis=0, keepdims=True)
        am = jnp.min(jnp.where(work == m, eio, n_exp), axis=0, keepdims=True)
        vals.append(m)
        idxs.append(am)
        work = jnp.where(eio == am, -jnp.inf, work)
    ex = [jnp.exp(v - vals[0]) for v in vals]
    den = ex[0] + ex[1] + ex[2] + ex[3]
    onehot = jnp.zeros((n_exp, tr), F32)
    for k in range(TOP_K):
        onehot = onehot + (eio == idxs[k]).astype(F32)
    before = _dot(onehot.astype(BF16), tri_ref[...])
    for k in range(TOP_K):
        idx_ref[k:k + 1, :] = idxs[k]
        gate_ref[k:k + 1, :] = ex[k] / den
        rank_ref[k:k + 1, :] = jnp.sum(jnp.where(eio == idxs[k], before, 0.0), axis=0, keepdims=True).astype(I32)
    tl = tr // cnt_ref.shape[0]
    for j in range(cnt_ref.shape[0]):
        cnt_ref[j] = jnp.sum(onehot[:, j * tl:(j + 1) * tl], axis=1, keepdims=True).astype(I32)


def _router(x1, mods_l, seq_len, w_r, b_r):
    t, d = x1.shape
    n_exp = w_r.shape[1]
    tl = min(MOE_TOKENS, t)
    tr = min(ROUTER_TOKENS, seq_len)
    per_seq = seq_len // tr
    wt = w_r.T
    wt_hi = wt.astype(BF16)
    wt_lo = (wt - wt_hi.astype(F32)).astype(BF16)
    pos = jnp.arange(tr, dtype=I32)
    tri = ((pos[:, None] < pos[None, :]) & (pos[:, None] // tl == pos[None, :] // tl)).astype(BF16)
    tok = pl.BlockSpec((TOP_K, tr), lambda i: (0, i))
    return pl.pallas_call(
        _router_body,
        grid=(t // tr,),
        in_specs=[
            pl.BlockSpec((tr, d), lambda i: (i, 0)),
            pl.BlockSpec((None, 6, d), lambda i: (i // per_seq, 0, 0)),
            pl.BlockSpec((n_exp, d), lambda i: (0, 0)),
            pl.BlockSpec((n_exp, d), lambda i: (0, 0)),
            pl.BlockSpec((n_exp, 1), lambda i: (0, 0)),
            pl.BlockSpec((tr, tr), lambda i: (0, 0)),
        ],
        out_specs=[tok, tok, tok, pl.BlockSpec((tr // tl, n_exp, 1), lambda i: (i, 0, 0))],
        out_shape=[jax.ShapeDtypeStruct((TOP_K, t), I32), jax.ShapeDtypeStruct((TOP_K, t), F32),
                   jax.ShapeDtypeStruct((TOP_K, t), I32), jax.ShapeDtypeStruct((t // tl, n_exp, 1), I32)],
        compiler_params=_params("parallel"),
        name="router",
    )(x1, mods_l, wt_hi, wt_lo, b_r.reshape(n_exp, 1), tri)


def _rows(ref, start, cnt):
    aligned = lambda v: v if isinstance(v, int) else pl.multiple_of(v, SUBLANES)
    return ref.at[pl.ds(aligned(start), aligned(cnt))]


def _segment_copy(src, src_row, dst, dst_row, cnt, sem):
    @pl.when(cnt > 0)
    def _():
        pltpu.make_async_copy(_rows(src, src_row, cnt), _rows(dst, dst_row, cnt), sem).start()


def _wait_rows(ref, cnt, sem):
    @pl.when(cnt > 0)
    def _():
        pltpu.make_async_copy(_rows(ref, 0, cnt), _rows(ref, 0, cnt), sem).wait()


def _zero_fill(zero_ref, xs_hbm, start, cnt, sem):
    zr = zero_ref.shape[0]
    n_full = cnt // zr

    def full(r, c):
        _segment_copy(zero_ref, 0, xs_hbm, start + r * zr, zr, sem)
        return c

    lax.fori_loop(0, n_full, full, 0)
    _segment_copy(zero_ref, 0, xs_hbm, start + n_full * zr, cnt - n_full * zr, sem)
    _wait_rows(xs_hbm, cnt, sem)


def _pack_halves(x):
    hd = x.shape[1] // 2
    lo = lax.bitcast_convert_type(x[:, :hd], U32)
    hi = lax.bitcast_convert_type(x[:, hd:], U32)
    return (lo >> 16) | (hi & jnp.uint32(0xFFFF0000))


def _unpack_halves(words):
    lo = lax.bitcast_convert_type(words << 16, F32).astype(BF16)
    hi = lax.bitcast_convert_type(words & jnp.uint32(0xFFFF0000), F32).astype(BF16)
    return lo, hi


def _one_hot_hits(iota, pos_of):
    hit = iota == pos_of(0)
    for k in range(1, TOP_K):
        hit = hit | (iota == pos_of(k))
    return hit


def _dispatch_body(goff_ref, loff_ref, cp_ref, ltot_ref, pad_start_ref, pad_cnt_ref, x1_ref, mod_ref, lp_ref,
                   gate_ref, xs_hbm, xl_ref, zero_ref, sem, zsem, *, n_exp):
    j = pl.program_id(0)
    tl, d = x1_ref.shape
    lr = xl_ref.shape[1]

    @pl.when(j == 0)
    def _():
        zero_ref[...] = jnp.zeros_like(zero_ref)

        def per_range(e, carry):
            _zero_fill(zero_ref, xs_hbm, pad_start_ref[e], pad_cnt_ref[e], zsem)
            return carry
        lax.fori_loop(0, pad_start_ref.shape[0], per_range, 0)

    def start_segments(tile):
        def per_expert(e, carry):
            seg = tile * n_exp + e
            _segment_copy(xl_ref.at[tile % 2], loff_ref[seg], xs_hbm, goff_ref[seg], cp_ref[seg], sem.at[tile % 2])
            return carry
        lax.fori_loop(0, n_exp, per_expert, 0)

    def wait_segments(tile):
        _wait_rows(xs_hbm, ltot_ref[tile], sem.at[tile % 2])

    @pl.when(j >= 2)
    def _():
        wait_segments(j - 2)

    buf = xl_ref.at[j % 2]
    riota = lax.broadcasted_iota(I32, (lr, tl), 0)
    perm = _one_hot_hits(riota, lambda k: lp_ref[k:k + 1, :]).astype(F32).astype(BF16)
    rows = _dot(perm, _moe_input(x1_ref, mod_ref).astype(BF16))
    buf[:, 0:d // 2] = _pack_halves(rows)
    gsel = jnp.zeros((lr, tl), F32)
    for k in range(TOP_K):
        gsel = gsel + jnp.where(riota == lp_ref[k:k + 1, :], gate_ref[k:k + 1, :], 0.0)
    rowg = jnp.broadcast_to(jnp.sum(gsel, axis=1, keepdims=True), (lr, LANES))
    buf[:, d // 2:] = lax.bitcast_convert_type(rowg, U32)
    start_segments(j)

    @pl.when(j == pl.num_programs(0) - 1)
    def _():
        @pl.when(j >= 1)
        def _():
            wait_segments(j - 1)
        wait_segments(j)


def _dispatch(x1, mods_l, seq_len, lp, gates, goff, loff, cp, ltot, pad_start, pad_cnt, n_rows):
    t, d = x1.shape
    tl = min(MOE_TOKENS, t)
    per_seq = seq_len // tl
    n_exp = goff.shape[0] // (t // tl)
    lr = TOP_K * tl + n_exp * SUBLANES
    tok = pl.BlockSpec((TOP_K, tl), lambda j, *_: (0, j))
    grid_spec = pltpu.PrefetchScalarGridSpec(
        num_scalar_prefetch=6,
        grid=(t // tl,),
        in_specs=[pl.BlockSpec((tl, d), lambda j, *_: (j, 0)),
                  pl.BlockSpec((None, 6, d), lambda j, *_: (j // per_seq, 0, 0)), tok, tok],
        out_specs=pl.BlockSpec(memory_space=pl.ANY),
        scratch_shapes=[pltpu.VMEM((2, lr, d // 2 + LANES), U32), pltpu.VMEM((MOE_TILE, d // 2 + LANES), U32),
                        pltpu.SemaphoreType.DMA((2,)), pltpu.SemaphoreType.DMA],
    )
    return pl.pallas_call(
        functools.partial(_dispatch_body, n_exp=n_exp),
        grid_spec=grid_spec,
        out_shape=jax.ShapeDtypeStruct((n_rows, d // 2 + LANES), U32),
        compiler_params=_params("arbitrary"),
        name="dispatch",
    )(goff, loff, cp, ltot, pad_start, pad_cnt, x1, mods_l, lp, gates)


def _experts_body(be_ref, bsrc_ref, nvalid_ref, first_ref, slot_ref, next_ref, xs_ref, wgu_hbm, bgu_ref, wdn_hbm,
                  bdn_ref, ys_ref, wgu_f32, wdn_f32, wgu_bf, wdn_bf, sem_gu, sem_dn, *, layer):
    i = pl.program_id(0)
    e = be_ref[i]
    d, f2 = wgu_bf.shape
    f = f2 // 2

    def fetch(expert, slot, start):
        for hbm, buf, sem in ((wgu_hbm, wgu_f32, sem_gu), (wdn_hbm, wdn_f32, sem_dn)):
            dma = pltpu.make_async_copy(hbm.at[layer, expert], buf.at[slot], sem.at[slot])
            dma.start() if start else dma.wait()

    @pl.when(first_ref[i] == 1)
    def _():
        slot = slot_ref[i]

        @pl.when(i == 0)
        def _():
            fetch(e, slot, True)

        fetch(e, slot, False)

        @pl.when(next_ref[i] >= 0)
        def _():
            fetch(next_ref[i], 1 - slot, True)

        def cast(ref_in, ref_out):
            def step(r, c):
                rows = pl.ds(pl.multiple_of(r * CAST_ROWS, CAST_ROWS), CAST_ROWS)
                ref_out[rows, :] = ref_in[rows, :].astype(BF16)
                return c
            lax.fori_loop(0, ref_in.shape[0] // CAST_ROWS, step, 0)
        cast(wgu_f32.at[slot], wgu_bf)
        cast(wdn_f32.at[slot], wdn_bf)

    @pl.when(i < nvalid_ref[0])
    def _():
        hd = d // 2
        x_lo, x_hi = _unpack_halves(xs_ref[:, 0:hd])
        proj = lambda cols: _dot(x_lo, wgu_bf[0:hd, cols]) + _dot(x_hi, wgu_bf[hd:, cols]) + bgu_ref[:, cols]
        gate = jnp.minimum(proj(slice(0, f)), SWIGLU_LIMIT)
        lin = jnp.clip(proj(slice(f, f2)), -SWIGLU_LIMIT, SWIGLU_LIMIT)
        act = (gate * jax.nn.sigmoid(SWIGLU_ALPHA * gate) * (lin + 1.0)).astype(BF16)
        row_gate = lax.bitcast_convert_type(xs_ref[:, hd:hd + 1], F32)
        y = (_dot(act, wdn_bf[...]) + bdn_ref[...]) * row_gate
        ys_ref[...] = _pack_halves(y.astype(BF16).astype(F32))

    @pl.when(i >= nvalid_ref[0])
    def _():
        ys_ref[...] = jnp.zeros_like(ys_ref)


def _experts(xs, block_e, block_src, n_valid, block_first, block_slot, block_next, w_gu, b_gu, w_dn, b_dn, layer):
    n_rows, dx = xs.shape
    depth, n_exp, d, f2 = w_gu.shape
    f = f2 // 2
    tm = MOE_TILE
    grid_spec = pltpu.PrefetchScalarGridSpec(
        num_scalar_prefetch=6,
        grid=(n_rows // tm,),
        in_specs=[
            pl.BlockSpec((tm, dx), lambda i, be, bs, *_: (bs[i], 0)),
            pl.BlockSpec(memory_space=pl.ANY),
            pl.BlockSpec((None, None, 1, f2), lambda i, be, *_: (layer, be[i], 0, 0)),
            pl.BlockSpec(memory_space=pl.ANY),
            pl.BlockSpec((None, None, 1, d), lambda i, be, *_: (layer, be[i], 0, 0)),
        ],
        out_specs=pl.BlockSpec((tm, d // 2), lambda i, *_: (i, 0)),
        scratch_shapes=[pltpu.VMEM((2, d, f2), F32), pltpu.VMEM((2, f, d), F32),
                        pltpu.VMEM((d, f2), BF16), pltpu.VMEM((f, d), BF16),
                        pltpu.SemaphoreType.DMA((2,)), pltpu.SemaphoreType.DMA((2,))],
    )
    return pl.pallas_call(
        functools.partial(_experts_body, layer=layer),
        grid_spec=grid_spec,
        out_shape=jax.ShapeDtypeStruct((n_rows, d // 2), U32),
        compiler_params=_params("arbitrary"),
        name="experts",
    )(block_e, block_src, n_valid, block_first, block_slot, block_next, xs, w_gu,
      b_gu.reshape(depth, n_exp, 1, f2), w_dn, b_dn.reshape(depth, n_exp, 1, d))


def _combine_body(goff_ref, loff_ref, cp_ref, ltot_ref, ys_hbm, x_ref, lp_ref, mod_ref, lng_ref, lnb_ref, o_ref,
                  yl_ref, sem, *, n_exp, alpha):
    j = pl.program_id(0)
    tl = x_ref.shape[0]
    lr, hd = yl_ref.shape[1:]

    def fetch(tile):
        buf = yl_ref.at[tile % 2]

        def per_expert(e, carry):
            seg = tile * n_exp + e
            _segment_copy(ys_hbm, goff_ref[seg], buf, loff_ref[seg], cp_ref[seg], sem.at[tile % 2])
            return carry
        lax.fori_loop(0, n_exp, per_expert, 0)

        def zero_rows(r, carry):
            buf[pl.ds(pl.multiple_of(r * SUBLANES, SUBLANES), SUBLANES), :] = jnp.zeros((SUBLANES, hd), U32)
            return carry
        lax.fori_loop(ltot_ref[tile] // SUBLANES, lr // SUBLANES, zero_rows, 0)

    @pl.when(j == 0)
    def _():
        fetch(j)

    @pl.when(j + 1 < pl.num_programs(0))
    def _():
        fetch(j + 1)

    liota = lax.broadcasted_iota(I32, (tl, lr), 1)
    pick = _one_hot_hits(liota, lambda k: lp_ref[:, k:k + 1]).astype(F32).astype(BF16)
    _wait_rows(ys_hbm, ltot_ref[j], sem.at[j % 2])
    y_lo, y_hi = _unpack_halves(yl_ref[j % 2])
    y = jnp.concatenate([_dot(pick, y_lo), _dot(pick, y_hi)], axis=1)
    o_ref[...] = _layer_norm(alpha * x_ref[...] + mod_ref[5:6, :] * y, lng_ref[...], lnb_ref[...])


def _combine(ys, lp_t, goff, loff, cp, ltot, x1, mods_l, ln_g, ln_b, alpha, seq_len):
    t, d = x1.shape
    tl = min(MOE_TOKENS, t)
    n_exp = goff.shape[0] // (t // tl)
    lr = TOP_K * tl + n_exp * SUBLANES
    per_seq = seq_len // tl
    vec = pl.BlockSpec((1, d), lambda j, *_: (0, 0))
    grid_spec = pltpu.PrefetchScalarGridSpec(
        num_scalar_prefetch=4,
        grid=(t // tl,),
        in_specs=[
            pl.BlockSpec(memory_space=pl.ANY),
            pl.BlockSpec((tl, d), lambda j, *_: (j, 0)),
            pl.BlockSpec((tl, TOP_K), lambda j, *_: (j, 0)),
            pl.BlockSpec((None, 6, d), lambda j, *_: (j // per_seq, 0, 0)),
            vec, vec,
        ],
        out_specs=pl.BlockSpec((tl, d), lambda j, *_: (j, 0)),
        scratch_shapes=[pltpu.VMEM((2, lr, d // 2), U32), pltpu.SemaphoreType.DMA((2,))],
    )
    return pl.pallas_call(
        functools.partial(_combine_body, n_exp=n_exp, alpha=alpha),
        grid_spec=grid_spec,
        out_shape=jax.ShapeDtypeStruct((t, d), F32),
        compiler_params=_params("arbitrary"),
        name="combine",
    )(goff, loff, cp, ltot, ys, x1, lp_t, mods_l, ln_g.reshape(1, d), ln_b.reshape(1, d))


def _moe_layer(x1, mods_l, w_r, b_r, w_gu, b_gu, w_dn, b_dn, layer, ln_g, ln_b, alpha, seq_len):
    t, d = x1.shape
    n_exp = w_r.shape[1]
    tm = MOE_TILE
    tl = min(MOE_TOKENS, t)
    n_tiles = t // tl
    idx, gates, rank, cnt = _router(x1, mods_l, seq_len, w_r, b_r)
    cp = (cnt[:, :, 0] + SUBLANES - 1) // SUBLANES * SUBLANES
    tot = jnp.sum(cp, axis=0)
    padded = (tot + tm - 1) // tm * tm
    pend = jnp.cumsum(padded)
    pstart = pend - padded
    goff = pstart[None, :] + jnp.cumsum(cp, axis=0) - cp
    loff = jnp.cumsum(cp, axis=1) - cp
    ltot = jnp.sum(cp, axis=1).astype(I32)
    is_e = idx[..., None] == jnp.arange(n_exp, dtype=I32)
    lp = jnp.sum(jnp.where(is_e, jnp.repeat(loff, tl, axis=0)[None], 0), axis=-1) + rank
    n_blocks = -(-(t * TOP_K + n_tiles * n_exp * (SUBLANES - 1)) // tm) + n_exp
    n_valid = pend[-1] // tm
    blk = jnp.arange(n_blocks, dtype=I32)
    block_src = jnp.minimum(blk, n_valid - 1)
    block_e = jnp.sum(pend[None, :] <= (block_src * tm)[:, None], axis=1).astype(I32)
    used = padded > 0
    experts = jnp.arange(n_exp, dtype=I32)
    next_used = jnp.flip(lax.cummin(jnp.flip(jnp.where(used, experts, n_exp))))
    next_used = jnp.concatenate([next_used[1:], jnp.full((1,), n_exp, I32)])
    next_used = jnp.where(next_used < n_exp, next_used, -1)
    slot_of = (jnp.cumsum(used.astype(I32)) - 1) % 2
    of_block = lambda table: jnp.sum(jnp.where(block_e[:, None] == experts[None, :], table[None, :], 0), axis=1)
    block_first = ((blk * tm == of_block(pstart)) & (blk < n_valid)).astype(I32)
    block_slot = of_block(slot_of).astype(I32)
    block_next = of_block(next_used).astype(I32)
    n_rows = n_blocks * tm
    pad_start = jnp.concatenate([pstart + tot, pend[-1:]]).astype(I32)
    pad_cnt = jnp.concatenate([padded - tot, n_rows - pend[-1:]]).astype(I32)
    flat = lambda a: a.reshape(-1).astype(I32)
    seg = (flat(goff), flat(loff), flat(cp), ltot)
    xs = _dispatch(x1, mods_l, seq_len, lp, gates, *seg, pad_start, pad_cnt, n_rows)
    ys = _experts(xs, block_e, block_src.astype(I32), n_valid.reshape(1).astype(I32), block_first, block_slot,
                  block_next, w_gu, b_gu, w_dn, b_dn, layer)
    return _combine(ys, lp.T, *seg, x1, mods_l, ln_g, ln_b, alpha, seq_len)


def kernel(x, c, ada_w, ada_b, post_ln_g, post_ln_b, conv_w_pw1, conv_b_pw1, conv_w_dw, conv_b_dw, conv_ln_g, conv_ln_b, conv_w_pw2, conv_b_pw2, w_kv, attn_w_q, attn_lambda, attn_subln_g, attn_w_o, rel_bias_table, router_w, router_b, expert_w_gate_up, expert_b_gate_up, expert_w_down, expert_b_down):
    bsz, s, d = x.shape
    depth = ada_w.shape[0]
    n_a = depth // 2
    alpha = (2 * depth) ** 0.25
    mods = _ada(c, ada_w, ada_b).reshape(depth, bsz, 6, d)
    q = k = vt = None
    for l in range(depth):
        mods_l = mods[l]
        if l < n_a:
            x1 = _conv(x, mods_l, conv_w_pw1[l], conv_b_pw1[l], conv_w_dw[l], conv_b_dw[l], conv_ln_g[l],
                       conv_ln_b[l], conv_w_pw2[l], conv_b_pw2[l], post_ln_g[l, 0], post_ln_b[l, 0], alpha)
        else:
            j = l - n_a
            if j == 0:
                q, k, vt = _qkv(x, mods_l, w_kv, attn_w_q[j])
            else:
                q = _qkv(x, mods_l, w_kv, attn_w_q[j])[0]
            lambda_init = 0.8 - 0.6 * math.exp(-0.3 * l)
            ot = _attn(q, k, vt, attn_lambda[j], attn_subln_g[j], rel_bias_table, lambda_init)
            x1 = _attn_out(ot, x, mods_l, attn_w_o[j], post_ln_g[l, 0], post_ln_b[l, 0], alpha)
        x = _moe_layer(x1.reshape(bsz * s, d), mods_l, router_w[l], router_b[l],
                       expert_w_gate_up, expert_b_gate_up, expert_w_down, expert_b_down, l,
                       post_ln_g[l, 1], post_ln_b[l, 1], alpha, s).reshape(bsz, s, d)
    return x
```

```python
import functools
import math

import jax
import jax.numpy as jnp
from jax import lax
from jax.experimental import pallas as pl
from jax.experimental.pallas import tpu as pltpu

F32 = jnp.float32
BF16 = jnp.bfloat16
I32 = jnp.int32
U32 = jnp.uint32
HIGHEST = lax.Precision.HIGHEST

CHUNK = 64
CONV_WIDTH = 31
HEAD_DIM = 64
REL_BUCKETS = 32
REL_MAX_DIST = 128
TOP_K = 4
SWIGLU_LIMIT = 7.0
SWIGLU_ALPHA = 1.702
LN_EPS = 1e-5
MASK_VALUE = -1e30
LOG2_E = math.log2(math.e)

SUBLANES = 8
LANES = 128
VMEM_LIMIT_BYTES = 56 * 1024 * 1024

ADA_TN = 1024
SEQ_TILE = 256
PROJ_TILE = 512
CONV_HALO = 32
CONV_ROWS = 256
CONV_COLS = 128
MOE_TOKENS = 256
ROUTER_TOKENS = 1024
MOE_TILE = 512
Q_BLOCK = 256
CAST_ROWS = 128


def _params(*sem):
    return pltpu.CompilerParams(dimension_semantics=sem, vmem_limit_bytes=VMEM_LIMIT_BYTES)


def _layer_norm(x, g, b):
    mu = jnp.mean(x, axis=-1, keepdims=True)
    xc = x - mu
    var = jnp.mean(xc * xc, axis=-1, keepdims=True)
    return xc * lax.rsqrt(var + LN_EPS) * g + b


def _dot(a, b):
    return jnp.dot(a, b, preferred_element_type=F32)


def _ada_body(c_ref, w_ref, b_ref, o_ref):
    c = c_ref[...]
    cond = c * jax.nn.sigmoid(c)
    o_ref[...] = jnp.dot(cond, w_ref[...], preferred_element_type=F32, precision=HIGHEST) + b_ref[...]


def _ada(c, ada_w, ada_b):
    depth, d, n = ada_w.shape
    bsz = c.shape[0]
    tn = min(ADA_TN, n)
    return pl.pallas_call(
        _ada_body,
        grid=(depth, n // tn),
        in_specs=[
            pl.BlockSpec((bsz, d), lambda l, j: (0, 0)),
            pl.BlockSpec((None, d, tn), lambda l, j: (l, 0, j)),
            pl.BlockSpec((None, 1, tn), lambda l, j: (l, 0, j)),
        ],
        out_specs=pl.BlockSpec((None, bsz, tn), lambda l, j: (l, 0, j)),
        out_shape=jax.ShapeDtypeStruct((depth, bsz, n), F32),
        compiler_params=_params("parallel", "parallel"),
        name="ada",
    )(c, ada_w, ada_b.reshape(depth, 1, n))


def _residual_epilogue(x, y, mod_ref, lng_ref, lnb_ref, alpha, x1_ref):
    x1_ref[...] = _layer_norm(alpha * x + mod_ref[2:3, :] * y, lng_ref[...], lnb_ref[...])


def _moe_input(x1_ref, mod_ref):
    return x1_ref[...] * (1.0 + mod_ref[4:5, :]) + mod_ref[3:4, :]


def _conv_body(x_ref, halo_ref, mod_ref, w1_ref, b1_ref, wdw_ref, bdw_ref, cg_ref, cb_ref, w2_ref, b2_ref,
               lng_ref, lnb_ref, x1_ref, win_ref, v_ref, *, alpha):
    ts, d = x_ref.shape
    i = pl.program_id(1)
    xw = jnp.concatenate([halo_ref[...], x_ref[...]], axis=0)
    h = (xw * (1.0 + mod_ref[1:2, :]) + mod_ref[0:1, :]).astype(BF16)
    rows = min(CONV_ROWS, ts)
    cols = min(CONV_COLS, d)

    def glu(c0):
        a = _dot(h, w1_ref[:, c0:c0 + cols]) + b1_ref[:, c0:c0 + cols]
        g = _dot(h, w1_ref[:, d + c0:d + c0 + cols]) + b1_ref[:, d + c0:d + c0 + cols]
        return a * jax.nn.sigmoid(g)

    off = CONV_HALO - (CONV_WIDTH - 1)
    n_shift = ts + CONV_HALO - SUBLANES
    in_seq = (lax.broadcasted_iota(I32, (ts + CONV_HALO, cols), 0) >= CONV_HALO) | (i > 0)
    u_next = glu(0)
    for c0 in range(0, d, cols):
        chan = slice(c0, c0 + cols)
        win_ref[0, :, chan] = jnp.where(in_seq, u_next, 0.0)
        if c0 + cols < d:
            u_next = glu(c0 + cols)
        for b in range(1, SUBLANES):
            win_ref[b, 0:n_shift, chan] = win_ref[0, b:b + n_shift, chan]
        for r0 in range(0, ts, rows):
            acc = jnp.zeros((rows, cols), F32)
            for j in range(CONV_WIDTH):
                a, b = divmod(off + j, SUBLANES)
                r = r0 + a * SUBLANES
                acc = acc + wdw_ref[j:j + 1, c0:c0 + cols] * win_ref[b, r:r + rows, c0:c0 + cols]
            v_ref[r0:r0 + rows, c0:c0 + cols] = acc
    v = _layer_norm(v_ref[...] + bdw_ref[...], cg_ref[...], cb_ref[...])
    v = (v * jax.nn.sigmoid(v)).astype(BF16)
    y = _dot(v, w2_ref[...]) + b2_ref[...]
    _residual_epilogue(x_ref[...], y, mod_ref, lng_ref, lnb_ref, alpha, x1_ref)


def _conv(x, mods_l, w_pw1, b_pw1, w_dw, b_dw, cln_g, cln_b, w_pw2, b_pw2, ln_g, ln_b, alpha):
    bsz, s, d = x.shape
    ts = min(SEQ_TILE, s)
    hb = ts // CONV_HALO
    row = lambda a: a.reshape(1, d)
    tile = pl.BlockSpec((None, ts, d), lambda b, i: (b, i, 0))
    vec = pl.BlockSpec((1, d), lambda b, i: (0, 0))
    return pl.pallas_call(
        functools.partial(_conv_body, alpha=alpha),
        grid=(bsz, s // ts),
        in_specs=[
            tile,
            pl.BlockSpec((None, CONV_HALO, d), lambda b, i: (b, jnp.maximum(i * hb - 1, 0), 0)),
            pl.BlockSpec((None, 6, d), lambda b, i: (b, 0, 0)),
            pl.BlockSpec((d, 2 * d), lambda b, i: (0, 0)),
            pl.BlockSpec((1, 2 * d), lambda b, i: (0, 0)),
            pl.BlockSpec((CONV_WIDTH, d), lambda b, i: (0, 0)),
            vec, vec, vec,
            pl.BlockSpec((d, d), lambda b, i: (0, 0)),
            vec, vec, vec,
        ],
        out_specs=tile,
        out_shape=jax.ShapeDtypeStruct((bsz, s, d), F32),
        scratch_shapes=[pltpu.VMEM((SUBLANES, ts + CONV_HALO, d), F32), pltpu.VMEM((ts, d), F32)],
        compiler_params=_params("parallel", "parallel"),
        name="conv",
    )(x, x, mods_l, w_pw1.astype(BF16), b_pw1.reshape(1, 2 * d), w_dw, row(b_dw), row(cln_g), row(cln_b),
      w_pw2.astype(BF16), row(b_pw2), row(ln_g), row(ln_b))


def _qkv_body(x_ref, mod_ref, wkv_ref, wq_ref, q_ref, k_ref, vt_ref):
    d = x_ref.shape[-1]
    x = x_ref[...]
    kv = _dot(x.astype(BF16), wkv_ref[...])
    k_ref[...] = kv[:, :d].astype(BF16)
    hd2 = 2 * HEAD_DIM
    for c0 in range(0, d, hd2):
        vt_ref[c0:c0 + hd2, :] = kv[:, d + c0:d + c0 + hd2].T.astype(BF16)
    h = (x * (1.0 + mod_ref[1:2, :]) + mod_ref[0:1, :]).astype(BF16)
    q_ref[...] = (_dot(h, wq_ref[...]) * (HEAD_DIM ** -0.5 * LOG2_E)).astype(BF16)


def _qkv(x, mods_l, w_kv, w_q):
    bsz, s, d = x.shape
    ts = min(PROJ_TILE, s)
    tile = pl.BlockSpec((None, ts, d), lambda b, i: (b, i, 0))
    return pl.pallas_call(
        _qkv_body,
        grid=(bsz, s // ts),
        in_specs=[
            tile,
            pl.BlockSpec((None, 6, d), lambda b, i: (b, 0, 0)),
            pl.BlockSpec((d, 2 * d), lambda b, i: (0, 0)),
            pl.BlockSpec((d, d), lambda b, i: (0, 0)),
        ],
        out_specs=[tile, tile, pl.BlockSpec((None, d, ts), lambda b, i: (b, 0, i))],
        out_shape=[jax.ShapeDtypeStruct((bsz, s, d), BF16)] * 2 + [jax.ShapeDtypeStruct((bsz, d, s), BF16)],
        compiler_params=_params("parallel", "parallel"),
        name="qkv",
    )(x, mods_l, w_kv.astype(BF16), w_q.astype(BF16))


def _t5_bucket(rel):
    nb = REL_BUCKETS // 2
    ret = jnp.where(rel > 0, nb, 0)
    n = jnp.abs(rel)
    max_exact = nb // 2
    large = max_exact + (jnp.log(jnp.maximum(n, 1).astype(F32) / max_exact)
                         / math.log(REL_MAX_DIST / max_exact) * (nb - max_exact)).astype(I32)
    large = jnp.minimum(large, nb - 1)
    return ret + jnp.where(n < max_exact, n, large)


def _bucket_strip(s):
    r = jnp.arange(Q_BLOCK, dtype=I32)[:, None]
    kp = jnp.arange(s, dtype=I32)[None, :] - (s - Q_BLOCK)
    bucket = _t5_bucket(kp - r)
    visible = jnp.floor_divide(kp, CHUNK) <= (r // CHUNK)
    return jnp.where(visible, bucket, REL_BUCKETS)


def _attn_body(tab_ref, q_ref, k_ref, vt_ref, bkt_ref, lam_ref, sg_ref, ot_ref, bias_ref, *, lambda_init, n_heads):
    h = pl.program_id(0)
    b = pl.program_id(1)
    s = q_ref.shape[0]

    @pl.when(b == 0)
    def _():
        bk = bkt_ref[...]
        acc = jnp.full(bk.shape, MASK_VALUE, F32)
        for r in range(REL_BUCKETS):
            acc = jnp.where(bk == r, tab_ref[r * n_heads + h] * LOG2_E, acc)
        bias_ref[:, 0:Q_BLOCK] = acc
        bias_ref[:, Q_BLOCK:] = acc

    lp = lam_ref[...]
    lam = (jnp.exp(jnp.sum(lp[0:1, :] * lp[1:2, :], axis=-1, keepdims=True))
           - jnp.exp(jnp.sum(lp[2:3, :] * lp[3:4, :], axis=-1, keepdims=True)) + lambda_init)
    lane = lax.broadcasted_iota(I32, (Q_BLOCK, 2 * HEAD_DIM), 1)
    nt = (((1,), (1,)), ((), ()))

    def scores(i):
        n_keys = (i + 1) * Q_BLOCK
        q = q_ref[i * Q_BLOCK:(i + 1) * Q_BLOCK, :]
        qq = jnp.concatenate([jnp.where(lane < HEAD_DIM, q, jnp.zeros_like(q)),
                              jnp.where(lane >= HEAD_DIM, q, jnp.zeros_like(q))], axis=0)
        return lax.dot_general(k_ref[0:n_keys, :], qq, nt, preferred_element_type=F32) + bias_ref[s - n_keys:s, :]

    n_blocks = s // Q_BLOCK
    sc_next = scores(0)
    for i in range(n_blocks):
        n_keys = (i + 1) * Q_BLOCK
        sc = sc_next
        if i + 1 < n_blocks:
            sc_next = scores(i + 1)
        p = jnp.exp2(sc - jnp.max(sc, axis=0, keepdims=True))
        denom = jnp.sum(p, axis=0, keepdims=True)
        pv = _dot(vt_ref[:, 0:n_keys], p.astype(BF16))
        o = pv[:, :Q_BLOCK] * (1.0 / denom[:, :Q_BLOCK]) - pv[:, Q_BLOCK:] * (lam / denom[:, Q_BLOCK:])
        o = o * lax.rsqrt(jnp.mean(o * o, axis=0, keepdims=True) + LN_EPS) * sg_ref[...]
        ot_ref[:, i * Q_BLOCK:(i + 1) * Q_BLOCK] = (o * (1.0 - lambda_init)).astype(BF16)


def _attn(q, k, vt, lam_p, subln_g, rel_table, lambda_init):
    bsz, s, d = q.shape
    hd2 = 2 * HEAD_DIM
    n_heads = d // hd2
    head = pl.BlockSpec((None, s, hd2), lambda h, b, tab: (b, 0, h))
    head_t = pl.BlockSpec((None, hd2, s), lambda h, b, tab: (b, h, 0))
    grid_spec = pltpu.PrefetchScalarGridSpec(
        num_scalar_prefetch=1,
        grid=(n_heads, bsz),
        in_specs=[
            head, head, head_t,
            pl.BlockSpec((s, Q_BLOCK), lambda h, b, tab: (0, 0)),
            pl.BlockSpec((4, HEAD_DIM), lambda h, b, tab: (0, 0)),
            pl.BlockSpec((hd2, 1), lambda h, b, tab: (0, 0)),
        ],
        out_specs=head_t,
        scratch_shapes=[pltpu.VMEM((s, 2 * Q_BLOCK), F32)],
    )
    return pl.pallas_call(
        functools.partial(_attn_body, lambda_init=lambda_init, n_heads=n_heads),
        grid_spec=grid_spec,
        out_shape=jax.ShapeDtypeStruct((bsz, d, s), BF16),
        compiler_params=_params("arbitrary", "arbitrary"),
        name="attn",
    )(rel_table.reshape(-1), q, k, vt, _bucket_strip(s).T, lam_p, subln_g.reshape(hd2, 1))


def _attn_out_body(ot_ref, x_ref, mod_ref, wo_ref, lng_ref, lnb_ref, x1_ref, *, alpha):
    y = lax.dot_general(ot_ref[...], wo_ref[...], (((0,), (0,)), ((), ())), preferred_element_type=F32)
    _residual_epilogue(x_ref[...], y, mod_ref, lng_ref, lnb_ref, alpha, x1_ref)


def _attn_out(o, x, mods_l, w_o, ln_g, ln_b, alpha):
    bsz, s, d = x.shape
    ts = min(PROJ_TILE, s)
    tile = pl.BlockSpec((None, ts, d), lambda b, i: (b, i, 0))
    vec = pl.BlockSpec((1, d), lambda b, i: (0, 0))
    return pl.pallas_call(
        functools.partial(_attn_out_body, alpha=alpha),
        grid=(bsz, s // ts),
        in_specs=[pl.BlockSpec((None, d, ts), lambda b, i: (b, 0, i)), tile,
                  pl.BlockSpec((None, 6, d), lambda b, i: (b, 0, 0)),
                  pl.BlockSpec((d, d), lambda b, i: (0, 0)), vec, vec],
        out_specs=tile,
        out_shape=jax.ShapeDtypeStruct((bsz, s, d), F32),
        compiler_params=_params("parallel", "parallel"),
        name="attn_out",
    )(o, x, mods_l, w_o.astype(BF16), ln_g.reshape(1, d), ln_b.reshape(1, d))


def _router_body(x1_ref, mod_ref, whi_ref, wlo_ref, b_ref, tri_ref, idx_ref, gate_ref, rank_ref, cnt_ref):
    h = _moe_input(x1_ref, mod_ref)
    h_hi = h.astype(BF16)
    h_lo = (h - h_hi.astype(F32)).astype(BF16)
    nt = (((1,), (1,)), ((), ()))
    logits = (lax.dot_general(whi_ref[...], h_hi, nt, preferred_element_type=F32)
              + (lax.dot_general(whi_ref[...], h_lo, nt, preferred_element_type=F32)
                 + lax.dot_general(wlo_ref[...], h_hi, nt, preferred_element_type=F32))) + b_ref[...]
    n_exp, tr = logits.shape
    eio = lax.broadcasted_iota(I32, (n_exp, tr), 0)
    work = logits
    vals, idxs = [], []
    for _ in range(TOP_K):
        m = jnp.max(work, axis=0, keepdims=True)
        am = jnp.min(jnp.where(work == m, eio, n_exp), axis=0, keepdims=True)
        vals.append(m)
        idxs.append(am)
        work = jnp.where(eio == am, -jnp.inf, work)
    ex = [jnp.exp(v - vals[0]) for v in vals]
    den = ex[0] + ex[1] + ex[2] + ex[3]
    onehot = jnp.zeros((n_exp, tr), F32)
    for k in range(TOP_K):
        onehot = onehot + (eio == idxs[k]).astype(F32)
    before = _dot(onehot.astype(BF16), tri_ref[...])
    for k in range(TOP_K):
        idx_ref[k:k + 1, :] = idxs[k]
        gate_ref[k:k + 1, :] = ex[k] / den
        rank_ref[k:k + 1, :] = jnp.sum(jnp.where(eio == idxs[k], before, 0.0), axis=0, keepdims=True).astype(I32)
    tl = tr // cnt_ref.shape[0]
    for j in range(cnt_ref.shape[0]):
        cnt_ref[j] = jnp.sum(onehot[:, j * tl:(j + 1) * tl], axis=1, keepdims=True).astype(I32)


def _router(x1, mods_l, seq_len, w_r, b_r):
    t, d = x1.shape
    n_exp = w_r.shape[1]
    tl = min(MOE_TOKENS, t)
    tr = min(ROUTER_TOKENS, seq_len)
    per_seq = seq_len // tr
    wt = w_r.T
    wt_hi = wt.astype(BF16)
    wt_lo = (wt - wt_hi.astype(F32)).astype(BF16)
    pos = jnp.arange(tr, dtype=I32)
    tri = ((pos[:, None] < pos[None, :]) & (pos[:, None] // tl == pos[None, :] // tl)).astype(BF16)
    tok = pl.BlockSpec((TOP_K, tr), lambda i: (0, i))
    return pl.pallas_call(
        _router_body,
        grid=(t // tr,),
        in_specs=[
            pl.BlockSpec((tr, d), lambda i: (i, 0)),
            pl.BlockSpec((None, 6, d), lambda i: (i // per_seq, 0, 0)),
            pl.BlockSpec((n_exp, d), lambda i: (0, 0)),
            pl.BlockSpec((n_exp, d), lambda i: (0, 0)),
            pl.BlockSpec((n_exp, 1), lambda i: (0, 0)),
            pl.BlockSpec((tr, tr), lambda i: (0, 0)),
        ],
        out_specs=[tok, tok, tok, pl.BlockSpec((tr // tl, n_exp, 1), lambda i: (i, 0, 0))],
        out_shape=[jax.ShapeDtypeStruct((TOP_K, t), I32), jax.ShapeDtypeStruct((TOP_K, t), F32),
                   jax.ShapeDtypeStruct((TOP_K, t), I32), jax.ShapeDtypeStruct((t // tl, n_exp, 1), I32)],
        compiler_params=_params("parallel"),
        name="router",
    )(x1, mods_l, wt_hi, wt_lo, b_r.reshape(n_exp, 1), tri)


def _rows(ref, start, cnt):
    aligned = lambda v: v if isinstance(v, int) else pl.multiple_of(v, SUBLANES)
    return ref.at[pl.ds(aligned(start), aligned(cnt))]


def _segment_copy(src, src_row, dst, dst_row, cnt, sem):
    @pl.when(cnt > 0)
    def _():
        pltpu.make_async_copy(_rows(src, src_row, cnt), _rows(dst, dst_row, cnt), sem).start()


def _wait_rows(ref, cnt, sem):
    @pl.when(cnt > 0)
    def _():
        pltpu.make_async_copy(_rows(ref, 0, cnt), _rows(ref, 0, cnt), sem).wait()


def _zero_fill(zero_ref, xs_hbm, start, cnt, sem):
    zr = zero_ref.shape[0]
    n_full = cnt // zr

    def full(r, c):
        _segment_copy(zero_ref, 0, xs_hbm, start + r * zr, zr, sem)
        return c

    lax.fori_loop(0, n_full, full, 0)
    _segment_copy(zero_ref, 0, xs_hbm, start + n_full * zr, cnt - n_full * zr, sem)


def _pack_halves(x):
    hd = x.shape[1] // 2
    lo = lax.bitcast_convert_type(x[:, :hd], U32)
    hi = lax.bitcast_convert_type(x[:, hd:], U32)
    return (lo >> 16) | (hi & jnp.uint32(0xFFFF0000))


def _unpack_halves(words):
    lo = lax.bitcast_convert_type(words << 16, F32).astype(BF16)
    hi = lax.bitcast_convert_type(words & jnp.uint32(0xFFFF0000), F32).astype(BF16)
    return lo, hi


def _one_hot_hits(iota, pos_of):
    hit = iota == pos_of(0)
    for k in range(1, TOP_K):
        hit = hit | (iota == pos_of(k))
    return hit


def _dispatch_body(goff_ref, loff_ref, cp_ref, ltot_ref, pad_start_ref, pad_cnt_ref, x1_ref, mod_ref, lp_ref,
                   gate_ref, xs_hbm, xl_ref, zero_ref, sem, zsem, *, n_exp):
    j = pl.program_id(0)
    tl, d = x1_ref.shape
    lr = xl_ref.shape[1]

    @pl.when(j == 0)
    def _():
        zero_ref[...] = jnp.zeros_like(zero_ref)

        def per_range(e, carry):
            _zero_fill(zero_ref, xs_hbm, pad_start_ref[e], pad_cnt_ref[e], zsem)
            return carry
        lax.fori_loop(0, pad_start_ref.shape[0], per_range, 0)

    def start_segments(tile):
        def per_expert(e, carry):
            seg = tile * n_exp + e
            _segment_copy(xl_ref.at[tile % 2], loff_ref[seg], xs_hbm, goff_ref[seg], cp_ref[seg], sem.at[tile % 2])
            return carry
        lax.fori_loop(0, n_exp, per_expert, 0)

    def wait_segments(tile):
        _wait_rows(xs_hbm, ltot_ref[tile], sem.at[tile % 2])

    @pl.when(j >= 2)
    def _():
        wait_segments(j - 2)

    buf = xl_ref.at[j % 2]
    riota = lax.broadcasted_iota(I32, (lr, tl), 0)
    perm = _one_hot_hits(riota, lambda k: lp_ref[k:k + 1, :]).astype(F32).astype(BF16)
    rows = _dot(perm, _moe_input(x1_ref, mod_ref).astype(BF16))
    buf[:, 0:d // 2] = _pack_halves(rows)
    gsel = jnp.zeros((lr, tl), F32)
    for k in range(TOP_K):
        gsel = gsel + jnp.where(riota == lp_ref[k:k + 1, :], gate_ref[k:k + 1, :], 0.0)
    rowg = jnp.broadcast_to(jnp.sum(gsel, axis=1, keepdims=True), (lr, LANES))
    buf[:, d // 2:] = lax.bitcast_convert_type(rowg, U32)
    start_segments(j)

    @pl.when(j == pl.num_programs(0) - 1)
    def _():
        @pl.when(j >= 1)
        def _():
            wait_segments(j - 1)
        wait_segments(j)
        n_zero = lax.fori_loop(0, pad_cnt_ref.shape[0], lambda e, acc: acc + pad_cnt_ref[e], jnp.int32(0))
        _wait_rows(xs_hbm, n_zero, zsem)


def _dispatch(x1, mods_l, seq_len, lp, gates, goff, loff, cp, ltot, pad_start, pad_cnt, n_rows):
    t, d = x1.shape
    tl = min(MOE_TOKENS, t)
    per_seq = seq_len // tl
    n_exp = goff.shape[0] // (t // tl)
    lr = TOP_K * tl + n_exp * SUBLANES
    tok = pl.BlockSpec((TOP_K, tl), lambda j, *_: (0, j))
    grid_spec = pltpu.PrefetchScalarGridSpec(
        num_scalar_prefetch=6,
        grid=(t // tl,),
        in_specs=[pl.BlockSpec((tl, d), lambda j, *_: (j, 0)),
                  pl.BlockSpec((None, 6, d), lambda j, *_: (j // per_seq, 0, 0)), tok, tok],
        out_specs=pl.BlockSpec(memory_space=pl.ANY),
        scratch_shapes=[pltpu.VMEM((2, lr, d // 2 + LANES), U32), pltpu.VMEM((MOE_TILE, d // 2 + LANES), U32),
                        pltpu.SemaphoreType.DMA((2,)), pltpu.SemaphoreType.DMA],
    )
    return pl.pallas_call(
        functools.partial(_dispatch_body, n_exp=n_exp),
        grid_spec=grid_spec,
        out_shape=jax.ShapeDtypeStruct((n_rows, d // 2 + LANES), U32),
        compiler_params=_params("arbitrary"),
        name="dispatch",
    )(goff, loff, cp, ltot, pad_start, pad_cnt, x1, mods_l, lp, gates)


def _experts_body(be_ref, bsrc_ref, nvalid_ref, first_ref, slot_ref, next_ref, xs_ref, wgu_hbm, bgu_ref, wdn_hbm,
                  bdn_ref, ys_ref, wgu_f32, wdn_f32, wgu_bf, wdn_bf, sem_gu, sem_dn, *, layer):
    i = pl.program_id(0)
    e = be_ref[i]
    d, f2 = wgu_bf.shape
    f = f2 // 2

    def fetch(expert, slot, start):
        for hbm, buf, sem in ((wgu_hbm, wgu_f32, sem_gu), (wdn_hbm, wdn_f32, sem_dn)):
            dma = pltpu.make_async_copy(hbm.at[layer, expert], buf.at[slot], sem.at[slot])
            dma.start() if start else dma.wait()

    @pl.when(first_ref[i] == 1)
    def _():
        slot = slot_ref[i]

        @pl.when(i == 0)
        def _():
            fetch(e, slot, True)

        fetch(e, slot, False)

        @pl.when(next_ref[i] >= 0)
        def _():
            fetch(next_ref[i], 1 - slot, True)

        def cast(ref_in, ref_out):
            def step(r, c):
                rows = pl.ds(pl.multiple_of(r * CAST_ROWS, CAST_ROWS), CAST_ROWS)
                ref_out[rows, :] = ref_in[rows, :].astype(BF16)
                return c
            lax.fori_loop(0, ref_in.shape[0] // CAST_ROWS, step, 0)
        cast(wgu_f32.at[slot], wgu_bf)
        cast(wdn_f32.at[slot], wdn_bf)

    @pl.when(i < nvalid_ref[0])
    def _():
        hd = d // 2
        x_lo, x_hi = _unpack_halves(xs_ref[:, 0:hd])
        proj = lambda cols: _dot(x_lo, wgu_bf[0:hd, cols]) + _dot(x_hi, wgu_bf[hd:, cols]) + bgu_ref[:, cols]
        gate = jnp.minimum(proj(slice(0, f)), SWIGLU_LIMIT)
        lin = jnp.clip(proj(slice(f, f2)), -SWIGLU_LIMIT, SWIGLU_LIMIT)
        act = (gate * jax.nn.sigmoid(SWIGLU_ALPHA * gate) * (lin + 1.0)).astype(BF16)
        row_gate = lax.bitcast_convert_type(xs_ref[:, hd:hd + 1], F32)
        y = (_dot(act, wdn_bf[...]) + bdn_ref[...]) * row_gate
        ys_ref[...] = _pack_halves(y.astype(BF16).astype(F32))

    @pl.when(i >= nvalid_ref[0])
    def _():
        ys_ref[...] = jnp.zeros_like(ys_ref)


def _experts(xs, block_e, block_src, n_valid, block_first, block_slot, block_next, w_gu, b_gu, w_dn, b_dn, layer):
    n_rows, dx = xs.shape
    depth, n_exp, d, f2 = w_gu.shape
    f = f2 // 2
    tm = MOE_TILE
    grid_spec = pltpu.PrefetchScalarGridSpec(
        num_scalar_prefetch=6,
        grid=(n_rows // tm,),
        in_specs=[
            pl.BlockSpec((tm, dx), lambda i, be, bs, *_: (bs[i], 0)),
            pl.BlockSpec(memory_space=pl.ANY),
            pl.BlockSpec((None, None, 1, f2), lambda i, be, *_: (layer, be[i], 0, 0)),
            pl.BlockSpec(memory_space=pl.ANY),
            pl.BlockSpec((None, None, 1, d), lambda i, be, *_: (layer, be[i], 0, 0)),
        ],
        out_specs=pl.BlockSpec((tm, d // 2), lambda i, *_: (i, 0)),
        scratch_shapes=[pltpu.VMEM((2, d, f2), F32), pltpu.VMEM((2, f, d), F32),
                        pltpu.VMEM((d, f2), BF16), pltpu.VMEM((f, d), BF16),
                        pltpu.SemaphoreType.DMA((2,)), pltpu.SemaphoreType.DMA((2,))],
    )
    return pl.pallas_call(
        functools.partial(_experts_body, layer=layer),
        grid_spec=grid_spec,
        out_shape=jax.ShapeDtypeStruct((n_rows, d // 2), U32),
        compiler_params=_params("arbitrary"),
        name="experts",
    )(block_e, block_src, n_valid, block_first, block_slot, block_next, xs, w_gu,
      b_gu.reshape(depth, n_exp, 1, f2), w_dn, b_dn.reshape(depth, n_exp, 1, d))


def _combine_body(goff_ref, loff_ref, cp_ref, ltot_ref, ys_hbm, x_ref, lp_ref, mod_ref, lng_ref, lnb_ref, o_ref,
                  yl_ref, sem, *, n_exp, alpha):
    j = pl.program_id(0)
    tl = x_ref.shape[0]
    lr, hd = yl_ref.shape[1:]

    def fetch(tile):
        buf = yl_ref.at[tile % 2]

        def per_expert(e, carry):
            seg = tile * n_exp + e
            _segment_copy(ys_hbm, goff_ref[seg], buf, loff_ref[seg], cp_ref[seg], sem.at[tile % 2])
            return carry
        lax.fori_loop(0, n_exp, per_expert, 0)

        def zero_rows(r, carry):
            buf[pl.ds(pl.multiple_of(r * SUBLANES, SUBLANES), SUBLANES), :] = jnp.zeros((SUBLANES, hd), U32)
            return carry
        lax.fori_loop(ltot_ref[tile] // SUBLANES, lr // SUBLANES, zero_rows, 0)

    @pl.when(j == 0)
    def _():
        fetch(j)

    @pl.when(j + 1 < pl.num_programs(0))
    def _():
        fetch(j + 1)

    liota = lax.broadcasted_iota(I32, (tl, lr), 1)
    pick = _one_hot_hits(liota, lambda k: lp_ref[:, k:k + 1]).astype(F32).astype(BF16)
    _wait_rows(ys_hbm, ltot_ref[j], sem.at[j % 2])
    y_lo, y_hi = _unpack_halves(yl_ref[j % 2])
    y = jnp.concatenate([_dot(pick, y_lo), _dot(pick, y_hi)], axis=1)
    o_ref[...] = _layer_norm(alpha * x_ref[...] + mod_ref[5:6, :] * y, lng_ref[...], lnb_ref[...])


def _combine(ys, lp_t, goff, loff, cp, ltot, x1, mods_l, ln_g, ln_b, alpha, seq_len):
    t, d = x1.shape
    tl = min(MOE_TOKENS, t)
    n_exp = goff.shape[0] // (t // tl)
    lr = TOP_K * tl + n_exp * SUBLANES
    per_seq = seq_len // tl
    vec = pl.BlockSpec((1, d), lambda j, *_: (0, 0))
    grid_spec = pltpu.PrefetchScalarGridSpec(
        num_scalar_prefetch=4,
        grid=(t // tl,),
        in_specs=[
            pl.BlockSpec(memory_space=pl.ANY),
            pl.BlockSpec((tl, d), lambda j, *_: (j, 0)),
            pl.BlockSpec((tl, TOP_K), lambda j, *_: (j, 0)),
            pl.BlockSpec((None, 6, d), lambda j, *_: (j // per_seq, 0, 0)),
            vec, vec,
        ],
        out_specs=pl.BlockSpec((tl, d), lambda j, *_: (j, 0)),
        scratch_shapes=[pltpu.VMEM((2, lr, d // 2), U32), pltpu.SemaphoreType.DMA((2,))],
    )
    return pl.pallas_call(
        functools.partial(_combine_body, n_exp=n_exp, alpha=alpha),
        grid_spec=grid_spec,
        out_shape=jax.ShapeDtypeStruct((t, d), F32),
        compiler_params=_params("arbitrary"),
        name="combine",
    )(goff, loff, cp, ltot, ys, x1, lp_t, mods_l, ln_g.reshape(1, d), ln_b.reshape(1, d))


def _moe_layer(x1, mods_l, w_r, b_r, w_gu, b_gu, w_dn, b_dn, layer, ln_g, ln_b, alpha, seq_len):
    t, d = x1.shape
    n_exp = w_r.shape[1]
    tm = MOE_TILE
    tl = min(MOE_TOKENS, t)
    n_tiles = t // tl
    idx, gates, rank, cnt = _router(x1, mods_l, seq_len, w_r, b_r)
    cp = (cnt[:, :, 0] + SUBLANES - 1) // SUBLANES * SUBLANES
    tot = jnp.sum(cp, axis=0)
    padded = (tot + tm - 1) // tm * tm
    pend = jnp.cumsum(padded)
    pstart = pend - padded
    goff = pstart[None, :] + jnp.cumsum(cp, axis=0) - cp
    loff = jnp.cumsum(cp, axis=1) - cp
    ltot = jnp.sum(cp, axis=1).astype(I32)
    is_e = idx[..., None] == jnp.arange(n_exp, dtype=I32)
    lp = jnp.sum(jnp.where(is_e, jnp.repeat(loff, tl, axis=0)[None], 0), axis=-1) + rank
    n_blocks = -(-(t * TOP_K + n_tiles * n_exp * (SUBLANES - 1)) // tm) + n_exp
    n_valid = pend[-1] // tm
    blk = jnp.arange(n_blocks, dtype=I32)
    block_src = jnp.minimum(blk, n_valid - 1)
    block_e = jnp.sum(pend[None, :] <= (block_src * tm)[:, None], axis=1).astype(I32)
    used = padded > 0
    experts = jnp.arange(n_exp, dtype=I32)
    next_used = jnp.flip(lax.cummin(jnp.flip(jnp.where(used, experts, n_exp))))
    next_used = jnp.concatenate([next_used[1:], jnp.full((1,), n_exp, I32)])
    next_used = jnp.where(next_used < n_exp, next_used, -1)
    slot_of = (jnp.cumsum(used.astype(I32)) - 1) % 2
    of_block = lambda table: jnp.sum(jnp.where(block_e[:, None] == experts[None, :], table[None, :], 0), axis=1)
    block_first = ((blk * tm == of_block(pstart)) & (blk < n_valid)).astype(I32)
    block_slot = of_block(slot_of).astype(I32)
    block_next = of_block(next_used).astype(I32)
    n_rows = n_blocks * tm
    pad_start = jnp.concatenate([pstart + tot, pend[-1:]]).astype(I32)
    pad_cnt = jnp.concatenate([padded - tot, n_rows - pend[-1:]]).astype(I32)
    flat = lambda a: a.reshape(-1).astype(I32)
    seg = (flat(goff), flat(loff), flat(cp), ltot)
    xs = _dispatch(x1, mods_l, seq_len, lp, gates, *seg, pad_start, pad_cnt, n_rows)
    ys = _experts(xs, block_e, block_src.astype(I32), n_valid.reshape(1).astype(I32), block_first, block_slot,
                  block_next, w_gu, b_gu, w_dn, b_dn, layer)
    return _combine(ys, lp.T, *seg, x1, mods_l, ln_g, ln_b, alpha, seq_len)


def kernel(x, c, ada_w, ada_b, post_ln_g, post_ln_b, conv_w_pw1, conv_b_pw1, conv_w_dw, conv_b_dw, conv_ln_g, conv_ln_b, conv_w_pw2, conv_b_pw2, w_kv, attn_w_q, attn_lambda, attn_subln_g, attn_w_o, rel_bias_table, router_w, router_b, expert_w_gate_up, expert_b_gate_up, expert_w_down, expert_b_down):
    bsz, s, d = x.shape
    depth = ada_w.shape[0]
    n_a = depth // 2
    alpha = (2 * depth) ** 0.25
    mods = _ada(c, ada_w, ada_b).reshape(depth, bsz, 6, d)
    q = k = vt = None
    for l in range(depth):
        mods_l = mods[l]
        if l < n_a:
            x1 = _conv(x, mods_l, conv_w_pw1[l], conv_b_pw1[l], conv_w_dw[l], conv_b_dw[l], conv_ln_g[l],
                       conv_ln_b[l], conv_w_pw2[l], conv_b_pw2[l], post_ln_g[l, 0], post_ln_b[l, 0], alpha)
        else:
            j = l - n_a
            if j == 0:
                q, k, vt = _qkv(x, mods_l, w_kv, attn_w_q[j])
            else:
                q = _qkv(x, mods_l, w_kv, attn_w_q[j])[0]
            lambda_init = 0.8 - 0.6 * math.exp(-0.3 * l)
            ot = _attn(q, k, vt, attn_lambda[j], attn_subln_g[j], rel_bias_table, lambda_init)
            x1 = _attn_out(ot, x, mods_l, attn_w_o[j], post_ln_g[l, 0], post_ln_b[l, 0], alpha)
        x = _moe_layer(x1.reshape(bsz * s, d), mods_l, router_w[l], router_b[l],
                       expert_w_gate_up, expert_b_gate_up, expert_w_down, expert_b_down, l,
                       post_ln_g[l, 1], post_ln_b[l, 1], alpha, s).reshape(bsz, s, d)
    return x
```

```python
import functools
import math

import jax
import jax.numpy as jnp
from jax import lax
from jax.experimental import pallas as pl
from jax.experimental.pallas import tpu as pltpu

F32 = jnp.float32
BF16 = jnp.bfloat16
I32 = jnp.int32
U32 = jnp.uint32
HIGHEST = lax.Precision.HIGHEST

CHUNK = 64
CONV_WIDTH = 31
HEAD_DIM = 64
REL_BUCKETS = 32
REL_MAX_DIST = 128
TOP_K = 4
SWIGLU_LIMIT = 7.0
SWIGLU_ALPHA = 1.702
LN_EPS = 1e-5
MASK_VALUE = -1e30
LOG2_E = math.log2(math.e)

SUBLANES = 8
LANES = 128
VMEM_LIMIT_BYTES = 56 * 1024 * 1024

ADA_TN = 1024
SEQ_TILE = 256
PROJ_TILE = 512
CONV_HALO = 32
CONV_ROWS = 256
CONV_COLS = 128
MOE_TOKENS = 256
ROUTER_TOKENS = 1024
MOE_TILE = 512
Q_BLOCK = 128
GLU_LOOKAHEAD = 1
SCORE_LOOKAHEAD = 3
CAST_ROWS = 128


def _params(*sem):
    return pltpu.CompilerParams(dimension_semantics=sem, vmem_limit_bytes=VMEM_LIMIT_BYTES)


def _layer_norm(x, g, b):
    mu = jnp.mean(x, axis=-1, keepdims=True)
    xc = x - mu
    var = jnp.mean(xc * xc, axis=-1, keepdims=True)
    return xc * lax.rsqrt(var + LN_EPS) * g + b


def _dot(a, b):
    return jnp.dot(a, b, preferred_element_type=F32)


def _ada_body(c_ref, w_ref, b_ref, o_ref):
    c = c_ref[...]
    cond = c * jax.nn.sigmoid(c)
    o_ref[...] = jnp.dot(cond, w_ref[...], preferred_element_type=F32, precision=HIGHEST) + b_ref[...]


def _ada(c, ada_w, ada_b):
    depth, d, n = ada_w.shape
    bsz = c.shape[0]
    tn = min(ADA_TN, n)
    return pl.pallas_call(
        _ada_body,
        grid=(depth, n // tn),
        in_specs=[
            pl.BlockSpec((bsz, d), lambda l, j: (0, 0)),
            pl.BlockSpec((None, d, tn), lambda l, j: (l, 0, j)),
            pl.BlockSpec((None, 1, tn), lambda l, j: (l, 0, j)),
        ],
        out_specs=pl.BlockSpec((None, bsz, tn), lambda l, j: (l, 0, j)),
        out_shape=jax.ShapeDtypeStruct((depth, bsz, n), F32),
        compiler_params=_params("parallel", "parallel"),
        name="ada",
    )(c, ada_w, ada_b.reshape(depth, 1, n))


def _residual_epilogue(x, y, mod_ref, lng_ref, lnb_ref, alpha, x1_ref):
    x1_ref[...] = _layer_norm(alpha * x + mod_ref[2:3, :] * y, lng_ref[...], lnb_ref[...])


def _moe_input(x1_ref, mod_ref):
    return x1_ref[...] * (1.0 + mod_ref[4:5, :]) + mod_ref[3:4, :]


def _conv_body(x_ref, halo_ref, mod_ref, w1_ref, b1_ref, wdw_ref, bdw_ref, cg_ref, cb_ref, w2_ref, b2_ref,
               lng_ref, lnb_ref, x1_ref, win_ref, v_ref, *, alpha):
    ts, d = x_ref.shape
    i = pl.program_id(1)
    xw = jnp.concatenate([halo_ref[...], x_ref[...]], axis=0)
    h = (xw * (1.0 + mod_ref[1:2, :]) + mod_ref[0:1, :]).astype(BF16)
    rows = min(CONV_ROWS, ts)
    cols = min(CONV_COLS, d)

    def glu(c0):
        a = _dot(h, w1_ref[:, c0:c0 + cols]) + b1_ref[:, c0:c0 + cols]
        g = _dot(h, w1_ref[:, d + c0:d + c0 + cols]) + b1_ref[:, d + c0:d + c0 + cols]
        return a * jax.nn.sigmoid(g)

    off = CONV_HALO - (CONV_WIDTH - 1)
    n_shift = ts + CONV_HALO - SUBLANES
    in_seq = (lax.broadcasted_iota(I32, (ts + CONV_HALO, cols), 0) >= CONV_HALO) | (i > 0)
    starts = list(range(0, d, cols))
    ahead = [glu(c) for c in starts[:GLU_LOOKAHEAD]]
    for n, c0 in enumerate(starts):
        chan = slice(c0, c0 + cols)
        win_ref[0, :, chan] = jnp.where(in_seq, ahead.pop(0), 0.0)
        if n + GLU_LOOKAHEAD < len(starts):
            ahead.append(glu(starts[n + GLU_LOOKAHEAD]))
        for b in range(1, SUBLANES):
            win_ref[b, 0:n_shift, chan] = win_ref[0, b:b + n_shift, chan]
        for r0 in range(0, ts, rows):
            acc = jnp.zeros((rows, cols), F32)
            for j in range(CONV_WIDTH):
                a, b = divmod(off + j, SUBLANES)
                r = r0 + a * SUBLANES
                acc = acc + wdw_ref[j:j + 1, c0:c0 + cols] * win_ref[b, r:r + rows, c0:c0 + cols]
            v_ref[r0:r0 + rows, c0:c0 + cols] = acc
    v = _layer_norm(v_ref[...] + bdw_ref[...], cg_ref[...], cb_ref[...])
    v = (v * jax.nn.sigmoid(v)).astype(BF16)
    y = _dot(v, w2_ref[...]) + b2_ref[...]
    _residual_epilogue(x_ref[...], y, mod_ref, lng_ref, lnb_ref, alpha, x1_ref)


def _conv(x, mods_l, w_pw1, b_pw1, w_dw, b_dw, cln_g, cln_b, w_pw2, b_pw2, ln_g, ln_b, alpha):
    bsz, s, d = x.shape
    ts = min(SEQ_TILE, s)
    hb = ts // CONV_HALO
    row = lambda a: a.reshape(1, d)
    tile = pl.BlockSpec((None, ts, d), lambda b, i: (b, i, 0))
    vec = pl.BlockSpec((1, d), lambda b, i: (0, 0))
    return pl.pallas_call(
        functools.partial(_conv_body, alpha=alpha),
        grid=(bsz, s // ts),
        in_specs=[
            tile,
            pl.BlockSpec((None, CONV_HALO, d), lambda b, i: (b, jnp.maximum(i * hb - 1, 0), 0)),
            pl.BlockSpec((None, 6, d), lambda b, i: (b, 0, 0)),
            pl.BlockSpec((d, 2 * d), lambda b, i: (0, 0)),
            pl.BlockSpec((1, 2 * d), lambda b, i: (0, 0)),
            pl.BlockSpec((CONV_WIDTH, d), lambda b, i: (0, 0)),
            vec, vec, vec,
            pl.BlockSpec((d, d), lambda b, i: (0, 0)),
            vec, vec, vec,
        ],
        out_specs=tile,
        out_shape=jax.ShapeDtypeStruct((bsz, s, d), F32),
        scratch_shapes=[pltpu.VMEM((SUBLANES, ts + CONV_HALO, d), F32), pltpu.VMEM((ts, d), F32)],
        compiler_params=_params("parallel", "parallel"),
        name="conv",
    )(x, x, mods_l, w_pw1.astype(BF16), b_pw1.reshape(1, 2 * d), w_dw, row(b_dw), row(cln_g), row(cln_b),
      w_pw2.astype(BF16), row(b_pw2), row(ln_g), row(ln_b))


def _qkv_body(x_ref, mod_ref, wkv_ref, wq_ref, q_ref, k_ref, vt_ref):
    d = x_ref.shape[-1]
    x = x_ref[...]
    kv = _dot(x.astype(BF16), wkv_ref[...])
    k_ref[...] = kv[:, :d].astype(BF16)
    hd2 = 2 * HEAD_DIM
    for c0 in range(0, d, hd2):
        vt_ref[c0:c0 + hd2, :] = kv[:, d + c0:d + c0 + hd2].T.astype(BF16)
    h = (x * (1.0 + mod_ref[1:2, :]) + mod_ref[0:1, :]).astype(BF16)
    q_ref[...] = (_dot(h, wq_ref[...]) * (HEAD_DIM ** -0.5 * LOG2_E)).astype(BF16)


def _qkv(x, mods_l, w_kv, w_q):
    bsz, s, d = x.shape
    ts = min(PROJ_TILE, s)
    tile = pl.BlockSpec((None, ts, d), lambda b, i: (b, i, 0))
    return pl.pallas_call(
        _qkv_body,
        grid=(bsz, s // ts),
        in_specs=[
            tile,
            pl.BlockSpec((None, 6, d), lambda b, i: (b, 0, 0)),
            pl.BlockSpec((d, 2 * d), lambda b, i: (0, 0)),
            pl.BlockSpec((d, d), lambda b, i: (0, 0)),
        ],
        out_specs=[tile, tile, pl.BlockSpec((None, d, ts), lambda b, i: (b, 0, i))],
        out_shape=[jax.ShapeDtypeStruct((bsz, s, d), BF16)] * 2 + [jax.ShapeDtypeStruct((bsz, d, s), BF16)],
        compiler_params=_params("parallel", "parallel"),
        name="qkv",
    )(x, mods_l, w_kv.astype(BF16), w_q.astype(BF16))


def _t5_bucket(rel):
    nb = REL_BUCKETS // 2
    ret = jnp.where(rel > 0, nb, 0)
    n = jnp.abs(rel)
    max_exact = nb // 2
    large = max_exact + (jnp.log(jnp.maximum(n, 1).astype(F32) / max_exact)
                         / math.log(REL_MAX_DIST / max_exact) * (nb - max_exact)).astype(I32)
    large = jnp.minimum(large, nb - 1)
    return ret + jnp.where(n < max_exact, n, large)


def _bucket_strip(s):
    r = jnp.arange(Q_BLOCK, dtype=I32)[:, None]
    kp = jnp.arange(s, dtype=I32)[None, :] - (s - Q_BLOCK)
    bucket = _t5_bucket(kp - r)
    visible = jnp.floor_divide(kp, CHUNK) <= (r // CHUNK)
    return jnp.where(visible, bucket, REL_BUCKETS)


def _attn_body(tab_ref, q_ref, k_ref, vt_ref, bkt_ref, lam_ref, sg_ref, ot_ref, bias_ref, *, lambda_init, n_heads):
    h = pl.program_id(0)
    b = pl.program_id(1)
    s = q_ref.shape[0]

    @pl.when(b == 0)
    def _():
        bk = bkt_ref[...]
        acc = jnp.full(bk.shape, MASK_VALUE, F32)
        for r in range(REL_BUCKETS):
            acc = jnp.where(bk == r, tab_ref[r * n_heads + h] * LOG2_E, acc)
        bias_ref[:, 0:Q_BLOCK] = acc
        bias_ref[:, Q_BLOCK:] = acc

    lp = lam_ref[...]
    lam = (jnp.exp(jnp.sum(lp[0:1, :] * lp[1:2, :], axis=-1, keepdims=True))
           - jnp.exp(jnp.sum(lp[2:3, :] * lp[3:4, :], axis=-1, keepdims=True)) + lambda_init)
    lane = lax.broadcasted_iota(I32, (Q_BLOCK, 2 * HEAD_DIM), 1)
    nt = (((1,), (1,)), ((), ()))

    def scores(i):
        n_keys = (i + 1) * Q_BLOCK
        q = q_ref[i * Q_BLOCK:(i + 1) * Q_BLOCK, :]
        qq = jnp.concatenate([jnp.where(lane < HEAD_DIM, q, jnp.zeros_like(q)),
                              jnp.where(lane >= HEAD_DIM, q, jnp.zeros_like(q))], axis=0)
        return lax.dot_general(k_ref[0:n_keys, :], qq, nt, preferred_element_type=F32) + bias_ref[s - n_keys:s, :]

    def values(i, p, denom):
        pv = _dot(vt_ref[:, 0:(i + 1) * Q_BLOCK], p)
        o = pv[:, :Q_BLOCK] * (1.0 / denom[:, :Q_BLOCK]) - pv[:, Q_BLOCK:] * (lam / denom[:, Q_BLOCK:])
        o = o * lax.rsqrt(jnp.mean(o * o, axis=0, keepdims=True) + LN_EPS) * sg_ref[...]
        ot_ref[:, i * Q_BLOCK:(i + 1) * Q_BLOCK] = (o * (1.0 - lambda_init)).astype(BF16)

    n_blocks = s // Q_BLOCK
    ahead = [scores(i) for i in range(min(SCORE_LOOKAHEAD, n_blocks))]
    for i in range(n_blocks):
        sc = ahead.pop(0)
        if i + SCORE_LOOKAHEAD < n_blocks:
            ahead.append(scores(i + SCORE_LOOKAHEAD))
        p = jnp.exp2(sc - jnp.max(sc, axis=0, keepdims=True))
        values(i, p.astype(BF16), jnp.sum(p, axis=0, keepdims=True))


def _attn(q, k, vt, lam_p, subln_g, rel_table, lambda_init):
    bsz, s, d = q.shape
    hd2 = 2 * HEAD_DIM
    n_heads = d // hd2
    head = pl.BlockSpec((None, s, hd2), lambda h, b, tab: (b, 0, h))
    head_t = pl.BlockSpec((None, hd2, s), lambda h, b, tab: (b, h, 0))
    grid_spec = pltpu.PrefetchScalarGridSpec(
        num_scalar_prefetch=1,
        grid=(n_heads, bsz),
        in_specs=[
            head, head, head_t,
            pl.BlockSpec((s, Q_BLOCK), lambda h, b, tab: (0, 0)),
            pl.BlockSpec((4, HEAD_DIM), lambda h, b, tab: (0, 0)),
            pl.BlockSpec((hd2, 1), lambda h, b, tab: (0, 0)),
        ],
        out_specs=head_t,
        scratch_shapes=[pltpu.VMEM((s, 2 * Q_BLOCK), F32)],
    )
    return pl.pallas_call(
        functools.partial(_attn_body, lambda_init=lambda_init, n_heads=n_heads),
        grid_spec=grid_spec,
        out_shape=jax.ShapeDtypeStruct((bsz, d, s), BF16),
        compiler_params=_params("arbitrary", "arbitrary"),
        name="attn",
    )(rel_table.reshape(-1), q, k, vt, _bucket_strip(s).T, lam_p, subln_g.reshape(hd2, 1))


def _attn_out_body(ot_ref, x_ref, mod_ref, wo_ref, lng_ref, lnb_ref, x1_ref, *, alpha):
    y = lax.dot_general(ot_ref[...], wo_ref[...], (((0,), (0,)), ((), ())), preferred_element_type=F32)
    _residual_epilogue(x_ref[...], y, mod_ref, lng_ref, lnb_ref, alpha, x1_ref)


def _attn_out(o, x, mods_l, w_o, ln_g, ln_b, alpha):
    bsz, s, d = x.shape
    ts = min(PROJ_TILE, s)
    tile = pl.BlockSpec((None, ts, d), lambda b, i: (b, i, 0))
    vec = pl.BlockSpec((1, d), lambda b, i: (0, 0))
    return pl.pallas_call(
        functools.partial(_attn_out_body, alpha=alpha),
        grid=(bsz, s // ts),
        in_specs=[pl.BlockSpec((None, d, ts), lambda b, i: (b, 0, i)), tile,
                  pl.BlockSpec((None, 6, d), lambda b, i: (b, 0, 0)),
                  pl.BlockSpec((d, d), lambda b, i: (0, 0)), vec, vec],
        out_specs=tile,
        out_shape=jax.ShapeDtypeStruct((bsz, s, d), F32),
        compiler_params=_params("parallel", "parallel"),
        name="attn_out",
    )(o, x, mods_l, w_o.astype(BF16), ln_g.reshape(1, d), ln_b.reshape(1, d))


def _router_body(x1_ref, mod_ref, whi_ref, wlo_ref, b_ref, tri_ref, idx_ref, gate_ref, rank_ref, cnt_ref):
    h = _moe_input(x1_ref, mod_ref)
    h_hi = h.astype(BF16)
    h_lo = (h - h_hi.astype(F32)).astype(BF16)
    nt = (((1,), (1,)), ((), ()))
    logits = (lax.dot_general(whi_ref[...], h_hi, nt, preferred_element_type=F32)
              + (lax.dot_general(whi_ref[...], h_lo, nt, preferred_element_type=F32)
                 + lax.dot_general(wlo_ref[...], h_hi, nt, preferred_element_type=F32))) + b_ref[...]
    n_exp, tr = logits.shape
    eio = lax.broadcasted_iota(I32, (n_exp, tr), 0)
    work = logits
    vals, idxs = [], []
    for _ in range(TOP_K):
        m = jnp.max(work, axis=0, keepdims=True)
        am = jnp.min(jnp.where(work == m, eio, n_exp), axis=0, keepdims=True)
        vals.append(m)
        idxs.append(am)
        work = jnp.where(eio == am, -jnp.inf, work)
    ex = [jnp.exp(v - vals[0]) for v in vals]
    den = ex[0] + ex[1] + ex[2] + ex[3]
    onehot = jnp.zeros((n_exp, tr), F32)
    for k in range(TOP_K):
        onehot = onehot + (eio == idxs[k]).astype(F32)
    before = _dot(onehot.astype(BF16), tri_ref[...])
    for k in range(TOP_K):
        idx_ref[k:k + 1, :] = idxs[k]
        gate_ref[k:k + 1, :] = ex[k] / den
        rank_ref[k:k + 1, :] = jnp.sum(jnp.where(eio == idxs[k], before, 0.0), axis=0, keepdims=True).astype(I32)
    tl = tr // cnt_ref.shape[0]
    for j in range(cnt_ref.shape[0]):
        cnt_ref[j] = jnp.sum(onehot[:, j * tl:(j + 1) * tl], axis=1, keepdims=True).astype(I32)


def _router(x1, mods_l, seq_len, w_r, b_r):
    t, d = x1.shape
    n_exp = w_r.shape[1]
    tl = min(MOE_TOKENS, t)
    tr = min(ROUTER_TOKENS, seq_len)
    per_seq = seq_len // tr
    wt = w_r.T
    wt_hi = wt.astype(BF16)
    wt_lo = (wt - wt_hi.astype(F32)).astype(BF16)
    pos = jnp.arange(tr, dtype=I32)
    tri = ((pos[:, None] < pos[None, :]) & (pos[:, None] // tl == pos[None, :] // tl)).astype(BF16)
    tok = pl.BlockSpec((TOP_K, tr), lambda i: (0, i))
    return pl.pallas_call(
        _router_body,
        grid=(t // tr,),
        in_specs=[
            pl.BlockSpec((tr, d), lambda i: (i, 0)),
            pl.BlockSpec((None, 6, d), lambda i: (i // per_seq, 0, 0)),
            pl.BlockSpec((n_exp, d), lambda i: (0, 0)),
            pl.BlockSpec((n_exp, d), lambda i: (0, 0)),
            pl.BlockSpec((n_exp, 1), lambda i: (0, 0)),
            pl.BlockSpec((tr, tr), lambda i: (0, 0)),
        ],
        out_specs=[tok, tok, tok, pl.BlockSpec((tr // tl, n_exp, 1), lambda i: (i, 0, 0))],
        out_shape=[jax.ShapeDtypeStruct((TOP_K, t), I32), jax.ShapeDtypeStruct((TOP_K, t), F32),
                   jax.ShapeDtypeStruct((TOP_K, t), I32), jax.ShapeDtypeStruct((t // tl, n_exp, 1), I32)],
        compiler_params=_params("parallel"),
        name="router",
    )(x1, mods_l, wt_hi, wt_lo, b_r.reshape(n_exp, 1), tri)


def _rows(ref, start, cnt):
    aligned = lambda v: v if isinstance(v, int) else pl.multiple_of(v, SUBLANES)
    return ref.at[pl.ds(aligned(start), aligned(cnt))]


def _segment_copy(src, src_row, dst, dst_row, cnt, sem):
    @pl.when(cnt > 0)
    def _():
        pltpu.make_async_copy(_rows(src, src_row, cnt), _rows(dst, dst_row, cnt), sem).start()


def _wait_rows(ref, cnt, sem):
    @pl.when(cnt > 0)
    def _():
        pltpu.make_async_copy(_rows(ref, 0, cnt), _rows(ref, 0, cnt), sem).wait()


def _zero_fill(zero_ref, xs_hbm, start, cnt, sem):
    zr = zero_ref.shape[0]
    n_full = cnt // zr

    def full(r, c):
        _segment_copy(zero_ref, 0, xs_hbm, start + r * zr, zr, sem)
        return c

    lax.fori_loop(0, n_full, full, 0)
    _segment_copy(zero_ref, 0, xs_hbm, start + n_full * zr, cnt - n_full * zr, sem)


def _pack_halves(x):
    hd = x.shape[1] // 2
    lo = lax.bitcast_convert_type(x[:, :hd], U32)
    hi = lax.bitcast_convert_type(x[:, hd:], U32)
    return (lo >> 16) | (hi & jnp.uint32(0xFFFF0000))


def _unpack_halves(words):
    lo = lax.bitcast_convert_type(words << 16, F32).astype(BF16)
    hi = lax.bitcast_convert_type(words & jnp.uint32(0xFFFF0000), F32).astype(BF16)
    return lo, hi


def _one_hot_hits(iota, pos_of):
    hit = iota == pos_of(0)
    for k in range(1, TOP_K):
        hit = hit | (iota == pos_of(k))
    return hit


def _dispatch_body(goff_ref, loff_ref, cp_ref, ltot_ref, pad_start_ref, pad_cnt_ref, x1_ref, mod_ref, lp_ref,
                   gate_ref, xs_hbm, xl_ref, zero_ref, sem, zsem, *, n_exp):
    j = pl.program_id(0)
    tl, d = x1_ref.shape
    lr = xl_ref.shape[1]

    @pl.when(j == 0)
    def _():
        zero_ref[...] = jnp.zeros_like(zero_ref)

        def per_range(e, carry):
            _zero_fill(zero_ref, xs_hbm, pad_start_ref[e], pad_cnt_ref[e], zsem)
            return carry
        lax.fori_loop(0, pad_start_ref.shape[0], per_range, 0)

    def start_segments(tile):
        def per_expert(e, carry):
            seg = tile * n_exp + e
            _segment_copy(xl_ref.at[tile % 2], loff_ref[seg], xs_hbm, goff_ref[seg], cp_ref[seg], sem.at[tile % 2])
            return carry
        lax.fori_loop(0, n_exp, per_expert, 0)

    def wait_segments(tile):
        _wait_rows(xs_hbm, ltot_ref[tile], sem.at[tile % 2])

    @pl.when(j >= 2)
    def _():
        wait_segments(j - 2)

    buf = xl_ref.at[j % 2]
    riota = lax.broadcasted_iota(I32, (lr, tl), 0)
    perm = _one_hot_hits(riota, lambda k: lp_ref[k:k + 1, :]).astype(F32).astype(BF16)
    rows = _dot(perm, _moe_input(x1_ref, mod_ref).astype(BF16))
    buf[:, 0:d // 2] = _pack_halves(rows)
    gsel = jnp.zeros((lr, tl), F32)
    for k in range(TOP_K):
        gsel = gsel + jnp.where(riota == lp_ref[k:k + 1, :], gate_ref[k:k + 1, :], 0.0)
    rowg = jnp.broadcast_to(jnp.sum(gsel, axis=1, keepdims=True), (lr, LANES))
    buf[:, d // 2:] = lax.bitcast_convert_type(rowg, U32)
    start_segments(j)

    @pl.when(j == pl.num_programs(0) - 1)
    def _():
        @pl.when(j >= 1)
        def _():
            wait_segments(j - 1)
        wait_segments(j)
        n_zero = lax.fori_loop(0, pad_cnt_ref.shape[0], lambda e, acc: acc + pad_cnt_ref[e], jnp.int32(0))
        _wait_rows(xs_hbm, n_zero, zsem)


def _dispatch(x1, mods_l, seq_len, lp, gates, goff, loff, cp, ltot, pad_start, pad_cnt, n_rows):
    t, d = x1.shape
    tl = min(MOE_TOKENS, t)
    per_seq = seq_len // tl
    n_exp = goff.shape[0] // (t // tl)
    lr = TOP_K * tl + n_exp * SUBLANES
    tok = pl.BlockSpec((TOP_K, tl), lambda j, *_: (0, j))
    grid_spec = pltpu.PrefetchScalarGridSpec(
        num_scalar_prefetch=6,
        grid=(t // tl,),
        in_specs=[pl.BlockSpec((tl, d), lambda j, *_: (j, 0)),
                  pl.BlockSpec((None, 6, d), lambda j, *_: (j // per_seq, 0, 0)), tok, tok],
        out_specs=pl.BlockSpec(memory_space=pl.ANY),
        scratch_shapes=[pltpu.VMEM((2, lr, d // 2 + LANES), U32), pltpu.VMEM((MOE_TILE, d // 2 + LANES), U32),
                        pltpu.SemaphoreType.DMA((2,)), pltpu.SemaphoreType.DMA],
    )
    return pl.pallas_call(
        functools.partial(_dispatch_body, n_exp=n_exp),
        grid_spec=grid_spec,
        out_shape=jax.ShapeDtypeStruct((n_rows, d // 2 + LANES), U32),
        compiler_params=_params("arbitrary"),
        name="dispatch",
    )(goff, loff, cp, ltot, pad_start, pad_cnt, x1, mods_l, lp, gates)


def _experts_body(be_ref, bsrc_ref, nvalid_ref, first_ref, slot_ref, next_ref, xs_ref, wgu_hbm, bgu_ref, wdn_hbm,
                  bdn_ref, ys_ref, wgu_f32, wdn_f32, wgu_bf, wdn_bf, sem_gu, sem_dn, *, layer):
    i = pl.program_id(0)
    e = be_ref[i]
    d, f2 = wgu_bf.shape
    f = f2 // 2

    def fetch(expert, slot, start):
        for hbm, buf, sem in ((wgu_hbm, wgu_f32, sem_gu), (wdn_hbm, wdn_f32, sem_dn)):
            dma = pltpu.make_async_copy(hbm.at[layer, expert], buf.at[slot], sem.at[slot])
            dma.start() if start else dma.wait()

    @pl.when(first_ref[i] == 1)
    def _():
        slot = slot_ref[i]

        @pl.when(i == 0)
        def _():
            fetch(e, slot, True)

        fetch(e, slot, False)

        @pl.when(next_ref[i] >= 0)
        def _():
            fetch(next_ref[i], 1 - slot, True)

        def cast(ref_in, ref_out):
            def step(r, c):
                rows = pl.ds(pl.multiple_of(r * CAST_ROWS, CAST_ROWS), CAST_ROWS)
                ref_out[rows, :] = ref_in[rows, :].astype(BF16)
                return c
            lax.fori_loop(0, ref_in.shape[0] // CAST_ROWS, step, 0)
        cast(wgu_f32.at[slot], wgu_bf)
        cast(wdn_f32.at[slot], wdn_bf)

    @pl.when(i < nvalid_ref[0])
    def _():
        hd = d // 2
        x_lo, x_hi = _unpack_halves(xs_ref[:, 0:hd])
        proj = lambda cols: _dot(x_lo, wgu_bf[0:hd, cols]) + _dot(x_hi, wgu_bf[hd:, cols]) + bgu_ref[:, cols]
        gate = jnp.minimum(proj(slice(0, f)), SWIGLU_LIMIT)
        lin = jnp.clip(proj(slice(f, f2)), -SWIGLU_LIMIT, SWIGLU_LIMIT)
        act = (gate * jax.nn.sigmoid(SWIGLU_ALPHA * gate) * (lin + 1.0)).astype(BF16)
        row_gate = lax.bitcast_convert_type(xs_ref[:, hd:hd + 1], F32)
        y = (_dot(act, wdn_bf[...]) + bdn_ref[...]) * row_gate
        ys_ref[...] = _pack_halves(y.astype(BF16).astype(F32))

    @pl.when(i >= nvalid_ref[0])
    def _():
        ys_ref[...] = jnp.zeros_like(ys_ref)


def _experts(xs, block_e, block_src, n_valid, block_first, block_slot, block_next, w_gu, b_gu, w_dn, b_dn, layer):
    n_rows, dx = xs.shape
    depth, n_exp, d, f2 = w_gu.shape
    f = f2 // 2
    tm = MOE_TILE
    grid_spec = pltpu.PrefetchScalarGridSpec(
        num_scalar_prefetch=6,
        grid=(n_rows // tm,),
        in_specs=[
            pl.BlockSpec((tm, dx), lambda i, be, bs, *_: (bs[i], 0)),
            pl.BlockSpec(memory_space=pl.ANY),
            pl.BlockSpec((None, None, 1, f2), lambda i, be, *_: (layer, be[i], 0, 0)),
            pl.BlockSpec(memory_space=pl.ANY),
            pl.BlockSpec((None, None, 1, d), lambda i, be, *_: (layer, be[i], 0, 0)),
        ],
        out_specs=pl.BlockSpec((tm, d // 2), lambda i, *_: (i, 0)),
        scratch_shapes=[pltpu.VMEM((2, d, f2), F32), pltpu.VMEM((2, f, d), F32),
                        pltpu.VMEM((d, f2), BF16), pltpu.VMEM((f, d), BF16),
                        pltpu.SemaphoreType.DMA((2,)), pltpu.SemaphoreType.DMA((2,))],
    )
    return pl.pallas_call(
        functools.partial(_experts_body, layer=layer),
        grid_spec=grid_spec,
        out_shape=jax.ShapeDtypeStruct((n_rows, d // 2), U32),
        compiler_params=_params("arbitrary"),
        name="experts",
    )(block_e, block_src, n_valid, block_first, block_slot, block_next, xs, w_gu,
      b_gu.reshape(depth, n_exp, 1, f2), w_dn, b_dn.reshape(depth, n_exp, 1, d))


def _combine_body(goff_ref, loff_ref, cp_ref, ltot_ref, ys_hbm, x_ref, lp_ref, mod_ref, lng_ref, lnb_ref, o_ref,
                  yl_ref, sem, *, n_exp, alpha):
    j = pl.program_id(0)
    tl = x_ref.shape[0]
    lr, hd = yl_ref.shape[1:]

    def fetch(tile):
        buf = yl_ref.at[tile % 2]

        def per_expert(e, carry):
            seg = tile * n_exp + e
            _segment_copy(ys_hbm, goff_ref[seg], buf, loff_ref[seg], cp_ref[seg], sem.at[tile % 2])
            return carry
        lax.fori_loop(0, n_exp, per_expert, 0)

        def zero_rows(r, carry):
            buf[pl.ds(pl.multiple_of(r * SUBLANES, SUBLANES), SUBLANES), :] = jnp.zeros((SUBLANES, hd), U32)
            return carry
        lax.fori_loop(ltot_ref[tile] // SUBLANES, lr // SUBLANES, zero_rows, 0)

    @pl.when(j == 0)
    def _():
        fetch(j)

    @pl.when(j + 1 < pl.num_programs(0))
    def _():
        fetch(j + 1)

    liota = lax.broadcasted_iota(I32, (tl, lr), 1)
    pick = _one_hot_hits(liota, lambda k: lp_ref[:, k:k + 1]).astype(F32).astype(BF16)
    _wait_rows(ys_hbm, ltot_ref[j], sem.at[j % 2])
    y_lo, y_hi = _unpack_halves(yl_ref[j % 2])
    y = jnp.concatenate([_dot(pick, y_lo), _dot(pick, y_hi)], axis=1)
    o_ref[...] = _layer_norm(alpha * x_ref[...] + mod_ref[5:6, :] * y, lng_ref[...], lnb_ref[...])


def _combine(ys, lp_t, goff, loff, cp, ltot, x1, mods_l, ln_g, ln_b, alpha, seq_len):
    t, d = x1.shape
    tl = min(MOE_TOKENS, t)
    n_exp = goff.shape[0] // (t // tl)
    lr = TOP_K * tl + n_exp * SUBLANES
    per_seq = seq_len // tl
    vec = pl.BlockSpec((1, d), lambda j, *_: (0, 0))
    grid_spec = pltpu.PrefetchScalarGridSpec(
        num_scalar_prefetch=4,
        grid=(t // tl,),
        in_specs=[
            pl.BlockSpec(memory_space=pl.ANY),
            pl.BlockSpec((tl, d), lambda j, *_: (j, 0)),
            pl.BlockSpec((tl, TOP_K), lambda j, *_: (j, 0)),
            pl.BlockSpec((None, 6, d), lambda j, *_: (j // per_seq, 0, 0)),
            vec, vec,
        ],
        out_specs=pl.BlockSpec((tl, d), lambda j, *_: (j, 0)),
        scratch_shapes=[pltpu.VMEM((2, lr, d // 2), U32), pltpu.SemaphoreType.DMA((2,))],
    )
    return pl.pallas_call(
        functools.partial(_combine_body, n_exp=n_exp, alpha=alpha),
        grid_spec=grid_spec,
        out_shape=jax.ShapeDtypeStruct((t, d), F32),
        compiler_params=_params("arbitrary"),
        name="combine",
    )(goff, loff, cp, ltot, ys, x1, lp_t, mods_l, ln_g.reshape(1, d), ln_b.reshape(1, d))


def _moe_layer(x1, mods_l, w_r, b_r, w_gu, b_gu, w_dn, b_dn, layer, ln_g, ln_b, alpha, seq_len):
    t, d = x1.shape
    n_exp = w_r.shape[1]
    tm = MOE_TILE
    tl = min(MOE_TOKENS, t)
    n_tiles = t // tl
    idx, gates, rank, cnt = _router(x1, mods_l, seq_len, w_r, b_r)
    cp = (cnt[:, :, 0] + SUBLANES - 1) // SUBLANES * SUBLANES
    tot = jnp.sum(cp, axis=0)
    padded = (tot + tm - 1) // tm * tm
    pend = jnp.cumsum(padded)
    pstart = pend - padded
    goff = pstart[None, :] + jnp.cumsum(cp, axis=0) - cp
    loff = jnp.cumsum(cp, axis=1) - cp
    ltot = jnp.sum(cp, axis=1).astype(I32)
    is_e = idx[..., None] == jnp.arange(n_exp, dtype=I32)
    lp = jnp.sum(jnp.where(is_e, jnp.repeat(loff, tl, axis=0)[None], 0), axis=-1) + rank
    n_blocks = -(-(t * TOP_K + n_tiles * n_exp * (SUBLANES - 1)) // tm) + n_exp
    n_valid = pend[-1] // tm
    blk = jnp.arange(n_blocks, dtype=I32)
    block_src = jnp.minimum(blk, n_valid - 1)
    block_e = jnp.sum(pend[None, :] <= (block_src * tm)[:, None], axis=1).astype(I32)
    used = padded > 0
    experts = jnp.arange(n_exp, dtype=I32)
    next_used = jnp.flip(lax.cummin(jnp.flip(jnp.where(used, experts, n_exp))))
    next_used = jnp.concatenate([next_used[1:], jnp.full((1,), n_exp, I32)])
    next_used = jnp.where(next_used < n_exp, next_used, -1)
    slot_of = (jnp.cumsum(used.astype(I32)) - 1) % 2
    of_block = lambda table: jnp.sum(jnp.where(block_e[:, None] == experts[None, :], table[None, :], 0), axis=1)
    block_first = ((blk * tm == of_block(pstart)) & (blk < n_valid)).astype(I32)
    block_slot = of_block(slot_of).astype(I32)
    block_next = of_block(next_used).astype(I32)
    n_rows = n_blocks * tm
    pad_start = jnp.concatenate([pstart + tot, pend[-1:]]).astype(I32)
    pad_cnt = jnp.concatenate([padded - tot, n_rows - pend[-1:]]).astype(I32)
    flat = lambda a: a.reshape(-1).astype(I32)
    seg = (flat(goff), flat(loff), flat(cp), ltot)
    xs = _dispatch(x1, mods_l, seq_len, lp, gates, *seg, pad_start, pad_cnt, n_rows)
    ys = _experts(xs, block_e, block_src.astype(I32), n_valid.reshape(1).astype(I32), block_first, block_slot,
                  block_next, w_gu, b_gu, w_dn, b_dn, layer)
    return _combine(ys, lp.T, *seg, x1, mods_l, ln_g, ln_b, alpha, seq_len)


def kernel(x, c, ada_w, ada_b, post_ln_g, post_ln_b, conv_w_pw1, conv_b_pw1, conv_w_dw, conv_b_dw, conv_ln_g, conv_ln_b, conv_w_pw2, conv_b_pw2, w_kv, attn_w_q, attn_lambda, attn_subln_g, attn_w_o, rel_bias_table, router_w, router_b, expert_w_gate_up, expert_b_gate_up, expert_w_down, expert_b_down):
    bsz, s, d = x.shape
    depth = ada_w.shape[0]
    n_a = depth // 2
    alpha = (2 * depth) ** 0.25
    mods = _ada(c, ada_w, ada_b).reshape(depth, bsz, 6, d)
    q = k = vt = None
    for l in range(depth):
        mods_l = mods[l]
        if l < n_a:
            x1 = _conv(x, mods_l, conv_w_pw1[l], conv_b_pw1[l], conv_w_dw[l], conv_b_dw[l], conv_ln_g[l],
                       conv_ln_b[l], conv_w_pw2[l], conv_b_pw2[l], post_ln_g[l, 0], post_ln_b[l, 0], alpha)
        else:
            j = l - n_a
            if j == 0:
                q, k, vt = _qkv(x, mods_l, w_kv, attn_w_q[j])
            else:
                q = _qkv(x, mods_l, w_kv, attn_w_q[j])[0]
            lambda_init = 0.8 - 0.6 * math.exp(-0.3 * l)
            ot = _attn(q, k, vt, attn_lambda[j], attn_subln_g[j], rel_bias_table, lambda_init)
            x1 = _attn_out(ot, x, mods_l, attn_w_o[j], post_ln_g[l, 0], post_ln_b[l, 0], alpha)
        x = _moe_layer(x1.reshape(bsz * s, d), mods_l, router_w[l], router_b[l],
                       expert_w_gate_up, expert_b_gate_up, expert_w_down, expert_b_down, l,
                       post_ln_g[l, 1], post_ln_b[l, 1], alpha, s).reshape(bsz, s, d)
    return x
```

```python
import functools
import math

import jax
import jax.numpy as jnp
from jax import lax
from jax.experimental import pallas as pl
from jax.experimental.pallas import tpu as pltpu

F32 = jnp.float32
BF16 = jnp.bfloat16
I32 = jnp.int32
U32 = jnp.uint32
HIGHEST = lax.Precision.HIGHEST

CHUNK = 64
CONV_WIDTH = 31
HEAD_DIM = 64
REL_BUCKETS = 32
REL_MAX_DIST = 128
TOP_K = 4
SWIGLU_LIMIT = 7.0
SWIGLU_ALPHA = 1.702
LN_EPS = 1e-5
MASK_VALUE = -1e30
LOG2_E = math.log2(math.e)

SUBLANES = 8
LANES = 128
VMEM_LIMIT_BYTES = 56 * 1024 * 1024

ADA_TN = 1024
SEQ_TILE = 512
PROJ_TILE = 512
CONV_HALO = 32
CONV_ROWS = 256
CONV_COLS = 128
MOE_TOKENS = 256
ROUTER_TOKENS = 1024
MOE_TILE = 512
Q_BLOCK = 128
GLU_LOOKAHEAD = 1
SCORE_LOOKAHEAD = 3
CAST_ROWS = 128


def _params(*sem):
    return pltpu.CompilerParams(dimension_semantics=sem, vmem_limit_bytes=VMEM_LIMIT_BYTES)


def _layer_norm(x, g, b):
    mu = jnp.mean(x, axis=-1, keepdims=True)
    xc = x - mu
    var = jnp.mean(xc * xc, axis=-1, keepdims=True)
    return xc * lax.rsqrt(var + LN_EPS) * g + b


def _dot(a, b):
    return jnp.dot(a, b, preferred_element_type=F32)


def _ada_body(c_ref, w_ref, b_ref, o_ref):
    c = c_ref[...]
    cond = c * jax.nn.sigmoid(c)
    o_ref[...] = jnp.dot(cond, w_ref[...], preferred_element_type=F32, precision=HIGHEST) + b_ref[...]


def _ada(c, ada_w, ada_b):
    depth, d, n = ada_w.shape
    bsz = c.shape[0]
    tn = min(ADA_TN, n)
    return pl.pallas_call(
        _ada_body,
        grid=(depth, n // tn),
        in_specs=[
            pl.BlockSpec((bsz, d), lambda l, j: (0, 0)),
            pl.BlockSpec((None, d, tn), lambda l, j: (l, 0, j)),
            pl.BlockSpec((None, 1, tn), lambda l, j: (l, 0, j)),
        ],
        out_specs=pl.BlockSpec((None, bsz, tn), lambda l, j: (l, 0, j)),
        out_shape=jax.ShapeDtypeStruct((depth, bsz, n), F32),
        compiler_params=_params("parallel", "parallel"),
        name="ada",
    )(c, ada_w, ada_b.reshape(depth, 1, n))


def _residual_epilogue(x, y, mod_ref, lng_ref, lnb_ref, alpha, x1_ref):
    x1_ref[...] = _layer_norm(alpha * x + mod_ref[2:3, :] * y, lng_ref[...], lnb_ref[...])


def _moe_input(x1_ref, mod_ref):
    return x1_ref[...] * (1.0 + mod_ref[4:5, :]) + mod_ref[3:4, :]


def _conv_body(x_ref, halo_ref, mod_ref, w1_ref, b1_ref, wdw_ref, bdw_ref, cg_ref, cb_ref, w2_ref, b2_ref,
               lng_ref, lnb_ref, x1_ref, win_ref, v_ref, *, alpha):
    ts, d = x_ref.shape
    i = pl.program_id(1)
    xw = jnp.concatenate([halo_ref[...], x_ref[...]], axis=0)
    h = (xw * (1.0 + mod_ref[1:2, :]) + mod_ref[0:1, :]).astype(BF16)
    rows = min(CONV_ROWS, ts)
    cols = min(CONV_COLS, d)

    def glu(c0):
        a = _dot(h, w1_ref[:, c0:c0 + cols]) + b1_ref[:, c0:c0 + cols]
        g = _dot(h, w1_ref[:, d + c0:d + c0 + cols]) + b1_ref[:, d + c0:d + c0 + cols]
        return a * jax.nn.sigmoid(g)

    off = CONV_HALO - (CONV_WIDTH - 1)
    n_shift = ts + CONV_HALO - SUBLANES
    in_seq = (lax.broadcasted_iota(I32, (ts + CONV_HALO, cols), 0) >= CONV_HALO) | (i > 0)
    starts = list(range(0, d, cols))
    ahead = [glu(c) for c in starts[:GLU_LOOKAHEAD]]
    for n, c0 in enumerate(starts):
        chan = slice(c0, c0 + cols)
        win_ref[0, :, chan] = jnp.where(in_seq, ahead.pop(0), 0.0)
        if n + GLU_LOOKAHEAD < len(starts):
            ahead.append(glu(starts[n + GLU_LOOKAHEAD]))
        for b in range(1, SUBLANES):
            win_ref[b, 0:n_shift, chan] = win_ref[0, b:b + n_shift, chan]
        for r0 in range(0, ts, rows):
            acc = jnp.zeros((rows, cols), F32)
            for j in range(CONV_WIDTH):
                a, b = divmod(off + j, SUBLANES)
                r = r0 + a * SUBLANES
                acc = acc + wdw_ref[j:j + 1, c0:c0 + cols] * win_ref[b, r:r + rows, c0:c0 + cols]
            v_ref[r0:r0 + rows, c0:c0 + cols] = acc
    v = _layer_norm(v_ref[...] + bdw_ref[...], cg_ref[...], cb_ref[...])
    v = (v * jax.nn.sigmoid(v)).astype(BF16)
    y = _dot(v, w2_ref[...]) + b2_ref[...]
    _residual_epilogue(x_ref[...], y, mod_ref, lng_ref, lnb_ref, alpha, x1_ref)


def _conv(x, mods_l, w_pw1, b_pw1, w_dw, b_dw, cln_g, cln_b, w_pw2, b_pw2, ln_g, ln_b, alpha):
    bsz, s, d = x.shape
    ts = min(SEQ_TILE, s)
    hb = ts // CONV_HALO
    row = lambda a: a.reshape(1, d)
    tile = pl.BlockSpec((None, ts, d), lambda b, i: (b, i, 0))
    vec = pl.BlockSpec((1, d), lambda b, i: (0, 0))
    return pl.pallas_call(
        functools.partial(_conv_body, alpha=alpha),
        grid=(bsz, s // ts),
        in_specs=[
            tile,
            pl.BlockSpec((None, CONV_HALO, d), lambda b, i: (b, jnp.maximum(i * hb - 1, 0), 0)),
            pl.BlockSpec((None, 6, d), lambda b, i: (b, 0, 0)),
            pl.BlockSpec((d, 2 * d), lambda b, i: (0, 0)),
            pl.BlockSpec((1, 2 * d), lambda b, i: (0, 0)),
            pl.BlockSpec((CONV_WIDTH, d), lambda b, i: (0, 0)),
            vec, vec, vec,
            pl.BlockSpec((d, d), lambda b, i: (0, 0)),
            vec, vec, vec,
        ],
        out_specs=tile,
        out_shape=jax.ShapeDtypeStruct((bsz, s, d), F32),
        scratch_shapes=[pltpu.VMEM((SUBLANES, ts + CONV_HALO, d), F32), pltpu.VMEM((ts, d), F32)],
        compiler_params=_params("parallel", "parallel"),
        name="conv",
    )(x, x, mods_l, w_pw1.astype(BF16), b_pw1.reshape(1, 2 * d), w_dw, row(b_dw), row(cln_g), row(cln_b),
      w_pw2.astype(BF16), row(b_pw2), row(ln_g), row(ln_b))


def _qkv_body(x_ref, mod_ref, wkv_ref, wq_ref, q_ref, k_ref, vt_ref):
    d = x_ref.shape[-1]
    x = x_ref[...]
    kv = _dot(x.astype(BF16), wkv_ref[...])
    k_ref[...] = kv[:, :d].astype(BF16)
    hd2 = 2 * HEAD_DIM
    for c0 in range(0, d, hd2):
        vt_ref[c0:c0 + hd2, :] = kv[:, d + c0:d + c0 + hd2].T.astype(BF16)
    h = (x * (1.0 + mod_ref[1:2, :]) + mod_ref[0:1, :]).astype(BF16)
    q_ref[...] = (_dot(h, wq_ref[...]) * (HEAD_DIM ** -0.5 * LOG2_E)).astype(BF16)


def _qkv(x, mods_l, w_kv, w_q):
    bsz, s, d = x.shape
    ts = min(PROJ_TILE, s)
    tile = pl.BlockSpec((None, ts, d), lambda b, i: (b, i, 0))
    return pl.pallas_call(
        _qkv_body,
        grid=(bsz, s // ts),
        in_specs=[
            tile,
            pl.BlockSpec((None, 6, d), lambda b, i: (b, 0, 0)),
            pl.BlockSpec((d, 2 * d), lambda b, i: (0, 0)),
            pl.BlockSpec((d, d), lambda b, i: (0, 0)),
        ],
        out_specs=[tile, tile, pl.BlockSpec((None, d, ts), lambda b, i: (b, 0, i))],
        out_shape=[jax.ShapeDtypeStruct((bsz, s, d), BF16)] * 2 + [jax.ShapeDtypeStruct((bsz, d, s), BF16)],
        compiler_params=_params("parallel", "parallel"),
        name="qkv",
    )(x, mods_l, w_kv.astype(BF16), w_q.astype(BF16))


def _t5_bucket(rel):
    nb = REL_BUCKETS // 2
    ret = jnp.where(rel > 0, nb, 0)
    n = jnp.abs(rel)
    max_exact = nb // 2
    large = max_exact + (jnp.log(jnp.maximum(n, 1).astype(F32) / max_exact)
                         / math.log(REL_MAX_DIST / max_exact) * (nb - max_exact)).astype(I32)
    large = jnp.minimum(large, nb - 1)
    return ret + jnp.where(n < max_exact, n, large)


def _bucket_strip(s):
    r = jnp.arange(Q_BLOCK, dtype=I32)[:, None]
    kp = jnp.arange(s, dtype=I32)[None, :] - (s - Q_BLOCK)
    bucket = _t5_bucket(kp - r)
    visible = jnp.floor_divide(kp, CHUNK) <= (r // CHUNK)
    return jnp.where(visible, bucket, REL_BUCKETS)


def _attn_body(tab_ref, q_ref, k_ref, vt_ref, bkt_ref, lam_ref, sg_ref, ot_ref, bias_ref, *, lambda_init, n_heads):
    h = pl.program_id(0)
    b = pl.program_id(1)
    s = q_ref.shape[0]

    @pl.when(b == 0)
    def _():
        bk = bkt_ref[...]
        acc = jnp.full(bk.shape, MASK_VALUE, F32)
        for r in range(REL_BUCKETS):
            acc = jnp.where(bk == r, tab_ref[r * n_heads + h] * LOG2_E, acc)
        bias_ref[:, 0:Q_BLOCK] = acc
        bias_ref[:, Q_BLOCK:] = acc

    lp = lam_ref[...]
    lam = (jnp.exp(jnp.sum(lp[0:1, :] * lp[1:2, :], axis=-1, keepdims=True))
           - jnp.exp(jnp.sum(lp[2:3, :] * lp[3:4, :], axis=-1, keepdims=True)) + lambda_init)
    lane = lax.broadcasted_iota(I32, (Q_BLOCK, 2 * HEAD_DIM), 1)
    nt = (((1,), (1,)), ((), ()))

    def scores(i):
        n_keys = (i + 1) * Q_BLOCK
        q = q_ref[i * Q_BLOCK:(i + 1) * Q_BLOCK, :]
        qq = jnp.concatenate([jnp.where(lane < HEAD_DIM, q, jnp.zeros_like(q)),
                              jnp.where(lane >= HEAD_DIM, q, jnp.zeros_like(q))], axis=0)
        return lax.dot_general(k_ref[0:n_keys, :], qq, nt, preferred_element_type=F32) + bias_ref[s - n_keys:s, :]

    def values(i, p, denom):
        pv = _dot(vt_ref[:, 0:(i + 1) * Q_BLOCK], p)
        o = pv[:, :Q_BLOCK] * (1.0 / denom[:, :Q_BLOCK]) - pv[:, Q_BLOCK:] * (lam / denom[:, Q_BLOCK:])
        o = o * lax.rsqrt(jnp.mean(o * o, axis=0, keepdims=True) + LN_EPS) * sg_ref[...]
        ot_ref[:, i * Q_BLOCK:(i + 1) * Q_BLOCK] = (o * (1.0 - lambda_init)).astype(BF16)

    n_blocks = s // Q_BLOCK
    ahead = [scores(i) for i in range(min(SCORE_LOOKAHEAD, n_blocks))]
    for i in range(n_blocks):
        sc = ahead.pop(0)
        if i + SCORE_LOOKAHEAD < n_blocks:
            ahead.append(scores(i + SCORE_LOOKAHEAD))
        p = jnp.exp2(sc - jnp.max(sc, axis=0, keepdims=True))
        values(i, p.astype(BF16), jnp.sum(p, axis=0, keepdims=True))


def _attn(q, k, vt, lam_p, subln_g, rel_table, lambda_init):
    bsz, s, d = q.shape
    hd2 = 2 * HEAD_DIM
    n_heads = d // hd2
    head = pl.BlockSpec((None, s, hd2), lambda h, b, tab: (b, 0, h))
    head_t = pl.BlockSpec((None, hd2, s), lambda h, b, tab: (b, h, 0))
    grid_spec = pltpu.PrefetchScalarGridSpec(
        num_scalar_prefetch=1,
        grid=(n_heads, bsz),
        in_specs=[
            head, head, head_t,
            pl.BlockSpec((s, Q_BLOCK), lambda h, b, tab: (0, 0)),
            pl.BlockSpec((4, HEAD_DIM), lambda h, b, tab: (0, 0)),
            pl.BlockSpec((hd2, 1), lambda h, b, tab: (0, 0)),
        ],
        out_specs=head_t,
        scratch_shapes=[pltpu.VMEM((s, 2 * Q_BLOCK), F32)],
    )
    return pl.pallas_call(
        functools.partial(_attn_body, lambda_init=lambda_init, n_heads=n_heads),
        grid_spec=grid_spec,
        out_shape=jax.ShapeDtypeStruct((bsz, d, s), BF16),
        compiler_params=_params("arbitrary", "arbitrary"),
        name="attn",
    )(rel_table.reshape(-1), q, k, vt, _bucket_strip(s).T, lam_p, subln_g.reshape(hd2, 1))


def _attn_out_body(ot_ref, x_ref, mod_ref, wo_ref, lng_ref, lnb_ref, x1_ref, *, alpha):
    y = lax.dot_general(ot_ref[...], wo_ref[...], (((0,), (0,)), ((), ())), preferred_element_type=F32)
    _residual_epilogue(x_ref[...], y, mod_ref, lng_ref, lnb_ref, alpha, x1_ref)


def _attn_out(o, x, mods_l, w_o, ln_g, ln_b, alpha):
    bsz, s, d = x.shape
    ts = min(PROJ_TILE, s)
    tile = pl.BlockSpec((None, ts, d), lambda b, i: (b, i, 0))
    vec = pl.BlockSpec((1, d), lambda b, i: (0, 0))
    return pl.pallas_call(
        functools.partial(_attn_out_body, alpha=alpha),
        grid=(bsz, s // ts),
        in_specs=[pl.BlockSpec((None, d, ts), lambda b, i: (b, 0, i)), tile,
                  pl.BlockSpec((None, 6, d), lambda b, i: (b, 0, 0)),
                  pl.BlockSpec((d, d), lambda b, i: (0, 0)), vec, vec],
        out_specs=tile,
        out_shape=jax.ShapeDtypeStruct((bsz, s, d), F32),
        compiler_params=_params("parallel", "parallel"),
        name="attn_out",
    )(o, x, mods_l, w_o.astype(BF16), ln_g.reshape(1, d), ln_b.reshape(1, d))


def _router_body(x1_ref, mod_ref, whi_ref, wlo_ref, b_ref, tri_ref, idx_ref, gate_ref, rank_ref, cnt_ref):
    h = _moe_input(x1_ref, mod_ref)
    h_hi = h.astype(BF16)
    h_lo = (h - h_hi.astype(F32)).astype(BF16)
    nt = (((1,), (1,)), ((), ()))
    logits = (lax.dot_general(whi_ref[...], h_hi, nt, preferred_element_type=F32)
              + (lax.dot_general(whi_ref[...], h_lo, nt, preferred_element_type=F32)
                 + lax.dot_general(wlo_ref[...], h_hi, nt, preferred_element_type=F32))) + b_ref[...]
    n_exp, tr = logits.shape
    eio = lax.broadcasted_iota(I32, (n_exp, tr), 0)
    work = logits
    vals, idxs = [], []
    for _ in range(TOP_K):
        m = jnp.max(work, axis=0, keepdims=True)
        am = jnp.min(jnp.where(work == m, eio, n_exp), axis=0, keepdims=True)
        vals.append(m)
        idxs.append(am)
        work = jnp.where(eio == am, -jnp.inf, work)
    ex = [jnp.exp(v - vals[0]) for v in vals]
    den = ex[0] + ex[1] + ex[2] + ex[3]
    onehot = jnp.zeros((n_exp, tr), F32)
    for k in range(TOP_K):
        onehot = onehot + (eio == idxs[k]).astype(F32)
    before = _dot(onehot.astype(BF16), tri_ref[...])
    for k in range(TOP_K):
        idx_ref[k:k + 1, :] = idxs[k]
        gate_ref[k:k + 1, :] = ex[k] / den
        rank_ref[k:k + 1, :] = jnp.sum(jnp.where(eio == idxs[k], before, 0.0), axis=0, keepdims=True).astype(I32)
    tl = tr // cnt_ref.shape[0]
    for j in range(cnt_ref.shape[0]):
        cnt_ref[j] = jnp.sum(onehot[:, j * tl:(j + 1) * tl], axis=1, keepdims=True).astype(I32)


def _router(x1, mods_l, seq_len, w_r, b_r):
    t, d = x1.shape
    n_exp = w_r.shape[1]
    tl = min(MOE_TOKENS, t)
    tr = min(ROUTER_TOKENS, seq_len)
    per_seq = seq_len // tr
    wt = w_r.T
    wt_hi = wt.astype(BF16)
    wt_lo = (wt - wt_hi.astype(F32)).astype(BF16)
    pos = jnp.arange(tr, dtype=I32)
    tri = ((pos[:, None] < pos[None, :]) & (pos[:, None] // tl == pos[None, :] // tl)).astype(BF16)
    tok = pl.BlockSpec((TOP_K, tr), lambda i: (0, i))
    return pl.pallas_call(
        _router_body,
        grid=(t // tr,),
        in_specs=[
            pl.BlockSpec((tr, d), lambda i: (i, 0)),
            pl.BlockSpec((None, 6, d), lambda i: (i // per_seq, 0, 0)),
            pl.BlockSpec((n_exp, d), lambda i: (0, 0)),
            pl.BlockSpec((n_exp, d), lambda i: (0, 0)),
            pl.BlockSpec((n_exp, 1), lambda i: (0, 0)),
            pl.BlockSpec((tr, tr), lambda i: (0, 0)),
        ],
        out_specs=[tok, tok, tok, pl.BlockSpec((tr // tl, n_exp, 1), lambda i: (i, 0, 0))],
        out_shape=[jax.ShapeDtypeStruct((TOP_K, t), I32), jax.ShapeDtypeStruct((TOP_K, t), F32),
                   jax.ShapeDtypeStruct((TOP_K, t), I32), jax.ShapeDtypeStruct((t // tl, n_exp, 1), I32)],
        compiler_params=_params("parallel"),
        name="router",
    )(x1, mods_l, wt_hi, wt_lo, b_r.reshape(n_exp, 1), tri)


def _rows(ref, start, cnt):
    aligned = lambda v: v if isinstance(v, int) else pl.multiple_of(v, SUBLANES)
    return ref.at[pl.ds(aligned(start), aligned(cnt))]


def _segment_copy(src, src_row, dst, dst_row, cnt, sem):
    @pl.when(cnt > 0)
    def _():
        pltpu.make_async_copy(_rows(src, src_row, cnt), _rows(dst, dst_row, cnt), sem).start()


def _wait_rows(ref, cnt, sem):
    @pl.when(cnt > 0)
    def _():
        pltpu.make_async_copy(_rows(ref, 0, cnt), _rows(ref, 0, cnt), sem).wait()


def _zero_fill(zero_ref, xs_hbm, start, cnt, sem):
    zr = zero_ref.shape[0]
    n_full = cnt // zr

    def full(r, c):
        _segment_copy(zero_ref, 0, xs_hbm, start + r * zr, zr, sem)
        return c

    lax.fori_loop(0, n_full, full, 0)
    _segment_copy(zero_ref, 0, xs_hbm, start + n_full * zr, cnt - n_full * zr, sem)


def _pack_halves(x):
    hd = x.shape[1] // 2
    lo = lax.bitcast_convert_type(x[:, :hd], U32)
    hi = lax.bitcast_convert_type(x[:, hd:], U32)
    return (lo >> 16) | (hi & jnp.uint32(0xFFFF0000))


def _unpack_halves(words):
    lo = lax.bitcast_convert_type(words << 16, F32).astype(BF16)
    hi = lax.bitcast_convert_type(words & jnp.uint32(0xFFFF0000), F32).astype(BF16)
    return lo, hi


def _one_hot_hits(iota, pos_of):
    hit = iota == pos_of(0)
    for k in range(1, TOP_K):
        hit = hit | (iota == pos_of(k))
    return hit


def _dispatch_body(goff_ref, loff_ref, cp_ref, ltot_ref, pad_start_ref, pad_cnt_ref, x1_ref, mod_ref, lp_ref,
                   gate_ref, xs_hbm, xl_ref, zero_ref, sem, zsem, *, n_exp):
    j = pl.program_id(0)
    tl, d = x1_ref.shape
    lr = xl_ref.shape[1]

    @pl.when(j == 0)
    def _():
        zero_ref[...] = jnp.zeros_like(zero_ref)

        def per_range(e, carry):
            _zero_fill(zero_ref, xs_hbm, pad_start_ref[e], pad_cnt_ref[e], zsem)
            return carry
        lax.fori_loop(0, pad_start_ref.shape[0], per_range, 0)

    def start_segments(tile):
        def per_expert(e, carry):
            seg = tile * n_exp + e
            _segment_copy(xl_ref.at[tile % 2], loff_ref[seg], xs_hbm, goff_ref[seg], cp_ref[seg], sem.at[tile % 2])
            return carry
        lax.fori_loop(0, n_exp, per_expert, 0)

    def wait_segments(tile):
        _wait_rows(xs_hbm, ltot_ref[tile], sem.at[tile % 2])

    @pl.when(j >= 2)
    def _():
        wait_segments(j - 2)

    buf = xl_ref.at[j % 2]
    riota = lax.broadcasted_iota(I32, (lr, tl), 0)
    perm = _one_hot_hits(riota, lambda k: lp_ref[k:k + 1, :]).astype(F32).astype(BF16)
    rows = _dot(perm, _moe_input(x1_ref, mod_ref).astype(BF16))
    gsel = jnp.zeros((lr, tl), F32)
    for k in range(TOP_K):
        gsel = gsel + jnp.where(riota == lp_ref[k:k + 1, :], gate_ref[k:k + 1, :], 0.0)
    rowg = jnp.broadcast_to(jnp.sum(gsel, axis=1, keepdims=True), (lr, LANES))
    buf[:, d // 2:] = lax.bitcast_convert_type(rowg, U32)
    buf[:, 0:d // 2] = _pack_halves(rows)
    start_segments(j)

    @pl.when(j == pl.num_programs(0) - 1)
    def _():
        @pl.when(j >= 1)
        def _():
            wait_segments(j - 1)
        wait_segments(j)
        n_zero = lax.fori_loop(0, pad_cnt_ref.shape[0], lambda e, acc: acc + pad_cnt_ref[e], jnp.int32(0))
        _wait_rows(xs_hbm, n_zero, zsem)


def _dispatch(x1, mods_l, seq_len, lp, gates, goff, loff, cp, ltot, pad_start, pad_cnt, n_rows):
    t, d = x1.shape
    tl = min(MOE_TOKENS, t)
    per_seq = seq_len // tl
    n_exp = goff.shape[0] // (t // tl)
    lr = TOP_K * tl + n_exp * SUBLANES
    tok = pl.BlockSpec((TOP_K, tl), lambda j, *_: (0, j))
    grid_spec = pltpu.PrefetchScalarGridSpec(
        num_scalar_prefetch=6,
        grid=(t // tl,),
        in_specs=[pl.BlockSpec((tl, d), lambda j, *_: (j, 0)),
                  pl.BlockSpec((None, 6, d), lambda j, *_: (j // per_seq, 0, 0)), tok, tok],
        out_specs=pl.BlockSpec(memory_space=pl.ANY),
        scratch_shapes=[pltpu.VMEM((2, lr, d // 2 + LANES), U32), pltpu.VMEM((MOE_TILE, d // 2 + LANES), U32),
                        pltpu.SemaphoreType.DMA((2,)), pltpu.SemaphoreType.DMA],
    )
    return pl.pallas_call(
        functools.partial(_dispatch_body, n_exp=n_exp),
        grid_spec=grid_spec,
        out_shape=jax.ShapeDtypeStruct((n_rows, d // 2 + LANES), U32),
        compiler_params=_params("arbitrary"),
        name="dispatch",
    )(goff, loff, cp, ltot, pad_start, pad_cnt, x1, mods_l, lp, gates)


def _experts_body(be_ref, bsrc_ref, nvalid_ref, first_ref, slot_ref, next_ref, xs_ref, wgu_hbm, bgu_ref, wdn_hbm,
                  bdn_ref, ys_ref, wgu_f32, wdn_f32, wgu_bf, wdn_bf, sem_gu, sem_dn, *, layer):
    i = pl.program_id(0)
    e = be_ref[i]
    d, f2 = wgu_bf.shape
    f = f2 // 2

    def fetch(expert, slot, start):
        for hbm, buf, sem in ((wgu_hbm, wgu_f32, sem_gu), (wdn_hbm, wdn_f32, sem_dn)):
            dma = pltpu.make_async_copy(hbm.at[layer, expert], buf.at[slot], sem.at[slot])
            dma.start() if start else dma.wait()

    @pl.when(first_ref[i] == 1)
    def _():
        slot = slot_ref[i]

        @pl.when(i == 0)
        def _():
            fetch(e, slot, True)

        fetch(e, slot, False)

        @pl.when(next_ref[i] >= 0)
        def _():
            fetch(next_ref[i], 1 - slot, True)

        def cast(ref_in, ref_out):
            def step(r, c):
                rows = pl.ds(pl.multiple_of(r * CAST_ROWS, CAST_ROWS), CAST_ROWS)
                ref_out[rows, :] = ref_in[rows, :].astype(BF16)
                return c
            lax.fori_loop(0, ref_in.shape[0] // CAST_ROWS, step, 0)
        cast(wgu_f32.at[slot], wgu_bf)
        cast(wdn_f32.at[slot], wdn_bf)

    @pl.when(i < nvalid_ref[0])
    def _():
        hd = d // 2
        x_lo, x_hi = _unpack_halves(xs_ref[:, 0:hd])
        proj = lambda cols: _dot(x_lo, wgu_bf[0:hd, cols]) + _dot(x_hi, wgu_bf[hd:, cols]) + bgu_ref[:, cols]
        gate = jnp.minimum(proj(slice(0, f)), SWIGLU_LIMIT)
        lin = jnp.clip(proj(slice(f, f2)), -SWIGLU_LIMIT, SWIGLU_LIMIT)
        act = (gate * jax.nn.sigmoid(SWIGLU_ALPHA * gate) * (lin + 1.0)).astype(BF16)
        row_gate = lax.bitcast_convert_type(xs_ref[:, hd:hd + 1], F32)
        y = (_dot(act, wdn_bf[...]) + bdn_ref[...]) * row_gate
        ys_ref[...] = _pack_halves(y.astype(BF16).astype(F32))

    @pl.when(i >= nvalid_ref[0])
    def _():
        ys_ref[...] = jnp.zeros_like(ys_ref)


def _experts(xs, block_e, block_src, n_valid, block_first, block_slot, block_next, w_gu, b_gu, w_dn, b_dn, layer):
    n_rows, dx = xs.shape
    depth, n_exp, d, f2 = w_gu.shape
    f = f2 // 2
    tm = MOE_TILE
    grid_spec = pltpu.PrefetchScalarGridSpec(
        num_scalar_prefetch=6,
        grid=(n_rows // tm,),
        in_specs=[
            pl.BlockSpec((tm, dx), lambda i, be, bs, *_: (bs[i], 0)),
            pl.BlockSpec(memory_space=pl.ANY),
            pl.BlockSpec((None, None, 1, f2), lambda i, be, *_: (layer, be[i], 0, 0)),
            pl.BlockSpec(memory_space=pl.ANY),
            pl.BlockSpec((None, None, 1, d), lambda i, be, *_: (layer, be[i], 0, 0)),
        ],
        out_specs=pl.BlockSpec((tm, d // 2), lambda i, *_: (i, 0)),
        scratch_shapes=[pltpu.VMEM((2, d, f2), F32), pltpu.VMEM((2, f, d), F32),
                        pltpu.VMEM((d, f2), BF16), pltpu.VMEM((f, d), BF16),
                        pltpu.SemaphoreType.DMA((2,)), pltpu.SemaphoreType.DMA((2,))],
    )
    return pl.pallas_call(
        functools.partial(_experts_body, layer=layer),
        grid_spec=grid_spec,
        out_shape=jax.ShapeDtypeStruct((n_rows, d // 2), U32),
        compiler_params=_params("arbitrary"),
        name="experts",
    )(block_e, block_src, n_valid, block_first, block_slot, block_next, xs, w_gu,
      b_gu.reshape(depth, n_exp, 1, f2), w_dn, b_dn.reshape(depth, n_exp, 1, d))


def _combine_body(goff_ref, loff_ref, cp_ref, ltot_ref, ys_hbm, x_ref, lp_ref, mod_ref, lng_ref, lnb_ref, o_ref,
                  yl_ref, sem, *, n_exp, alpha):
    j = pl.program_id(0)
    tl = x_ref.shape[0]
    lr, hd = yl_ref.shape[1:]

    def fetch(tile):
        buf = yl_ref.at[tile % 2]

        def per_expert(e, carry):
            seg = tile * n_exp + e
            _segment_copy(ys_hbm, goff_ref[seg], buf, loff_ref[seg], cp_ref[seg], sem.at[tile % 2])
            return carry
        lax.fori_loop(0, n_exp, per_expert, 0)

        def zero_rows(r, carry):
            buf[pl.ds(pl.multiple_of(r * SUBLANES, SUBLANES), SUBLANES), :] = jnp.zeros((SUBLANES, hd), U32)
            return carry
        lax.fori_loop(ltot_ref[tile] // SUBLANES, lr // SUBLANES, zero_rows, 0)

    @pl.when(j == 0)
    def _():
        fetch(j)

    @pl.when(j + 1 < pl.num_programs(0))
    def _():
        fetch(j + 1)

    liota = lax.broadcasted_iota(I32, (tl, lr), 1)
    pick = _one_hot_hits(liota, lambda k: lp_ref[:, k:k + 1]).astype(F32).astype(BF16)
    _wait_rows(ys_hbm, ltot_ref[j], sem.at[j % 2])
    y_lo, y_hi = _unpack_halves(yl_ref[j % 2])
    y = jnp.concatenate([_dot(pick, y_lo), _dot(pick, y_hi)], axis=1)
    o_ref[...] = _layer_norm(alpha * x_ref[...] + mod_ref[5:6, :] * y, lng_ref[...], lnb_ref[...])


def _combine(ys, lp_t, goff, loff, cp, ltot, x1, mods_l, ln_g, ln_b, alpha, seq_len):
    t, d = x1.shape
    tl = min(MOE_TOKENS, t)
    n_exp = goff.shape[0] // (t // tl)
    lr = TOP_K * tl + n_exp * SUBLANES
    per_seq = seq_len // tl
    vec = pl.BlockSpec((1, d), lambda j, *_: (0, 0))
    grid_spec = pltpu.PrefetchScalarGridSpec(
        num_scalar_prefetch=4,
        grid=(t // tl,),
        in_specs=[
            pl.BlockSpec(memory_space=pl.ANY),
            pl.BlockSpec((tl, d), lambda j, *_: (j, 0)),
            pl.BlockSpec((tl, TOP_K), lambda j, *_: (j, 0)),
            pl.BlockSpec((None, 6, d), lambda j, *_: (j // per_seq, 0, 0)),
            vec, vec,
        ],
        out_specs=pl.BlockSpec((tl, d), lambda j, *_: (j, 0)),
        scratch_shapes=[pltpu.VMEM((2, lr, d // 2), U32), pltpu.SemaphoreType.DMA((2,))],
    )
    return pl.pallas_call(
        functools.partial(_combine_body, n_exp=n_exp, alpha=alpha),
        grid_spec=grid_spec,
        out_shape=jax.ShapeDtypeStruct((t, d), F32),
        compiler_params=_params("arbitrary"),
        name="combine",
    )(goff, loff, cp, ltot, ys, x1, lp_t, mods_l, ln_g.reshape(1, d), ln_b.reshape(1, d))


def _moe_layer(x1, mods_l, w_r, b_r, w_gu, b_gu, w_dn, b_dn, layer, ln_g, ln_b, alpha, seq_len):
    t, d = x1.shape
    n_exp = w_r.shape[1]
    tm = MOE_TILE
    tl = min(MOE_TOKENS, t)
    n_tiles = t // tl
    idx, gates, rank, cnt = _router(x1, mods_l, seq_len, w_r, b_r)
    cp = (cnt[:, :, 0] + SUBLANES - 1) // SUBLANES * SUBLANES
    tot = jnp.sum(cp, axis=0)
    padded = (tot + tm - 1) // tm * tm
    pend = jnp.cumsum(padded)
    pstart = pend - padded
    goff = pstart[None, :] + jnp.cumsum(cp, axis=0) - cp
    loff = jnp.cumsum(cp, axis=1) - cp
    ltot = jnp.sum(cp, axis=1).astype(I32)
    is_e = idx[..., None] == jnp.arange(n_exp, dtype=I32)
    lp = jnp.sum(jnp.where(is_e, jnp.repeat(loff, tl, axis=0)[None], 0), axis=-1) + rank
    n_blocks = -(-(t * TOP_K + n_tiles * n_exp * (SUBLANES - 1)) // tm) + n_exp
    n_valid = pend[-1] // tm
    blk = jnp.arange(n_blocks, dtype=I32)
    block_src = jnp.minimum(blk, n_valid - 1)
    block_e = jnp.sum(pend[None, :] <= (block_src * tm)[:, None], axis=1).astype(I32)
    used = padded > 0
    experts = jnp.arange(n_exp, dtype=I32)
    next_used = jnp.flip(lax.cummin(jnp.flip(jnp.where(used, experts, n_exp))))
    next_used = jnp.concatenate([next_used[1:], jnp.full((1,), n_exp, I32)])
    next_used = jnp.where(next_used < n_exp, next_used, -1)
    slot_of = (jnp.cumsum(used.astype(I32)) - 1) % 2
    of_block = lambda table: jnp.sum(jnp.where(block_e[:, None] == experts[None, :], table[None, :], 0), axis=1)
    block_first = ((blk * tm == of_block(pstart)) & (blk < n_valid)).astype(I32)
    block_slot = of_block(slot_of).astype(I32)
    block_next = of_block(next_used).astype(I32)
    n_rows = n_blocks * tm
    pad_start = jnp.concatenate([pstart + tot, pend[-1:]]).astype(I32)
    pad_cnt = jnp.concatenate([padded - tot, n_rows - pend[-1:]]).astype(I32)
    flat = lambda a: a.reshape(-1).astype(I32)
    seg = (flat(goff), flat(loff), flat(cp), ltot)
    xs = _dispatch(x1, mods_l, seq_len, lp, gates, *seg, pad_start, pad_cnt, n_rows)
    ys = _experts(xs, block_e, block_src.astype(I32), n_valid.reshape(1).astype(I32), block_first, block_slot,
                  block_next, w_gu, b_gu, w_dn, b_dn, layer)
    return _combine(ys, lp.T, *seg, x1, mods_l, ln_g, ln_b, alpha, seq_len)


def kernel(x, c, ada_w, ada_b, post_ln_g, post_ln_b, conv_w_pw1, conv_b_pw1, conv_w_dw, conv_b_dw, conv_ln_g, conv_ln_b, conv_w_pw2, conv_b_pw2, w_kv, attn_w_q, attn_lambda, attn_subln_g, attn_w_o, rel_bias_table, router_w, router_b, expert_w_gate_up, expert_b_gate_up, expert_w_down, expert_b_down):
    bsz, s, d = x.shape
    depth = ada_w.shape[0]
    n_a = depth // 2
    alpha = (2 * depth) ** 0.25
    mods = _ada(c, ada_w, ada_b).reshape(depth, bsz, 6, d)
    q = k = vt = None
    for l in range(depth):
        mods_l = mods[l]
        if l < n_a:
            x1 = _conv(x, mods_l, conv_w_pw1[l], conv_b_pw1[l], conv_w_dw[l], conv_b_dw[l], conv_ln_g[l],
                       conv_ln_b[l], conv_w_pw2[l], conv_b_pw2[l], post_ln_g[l, 0], post_ln_b[l, 0], alpha)
        else:
            j = l - n_a
            if j == 0:
                q, k, vt = _qkv(x, mods_l, w_kv, attn_w_q[j])
            else:
                q = _qkv(x, mods_l, w_kv, attn_w_q[j])[0]
            lambda_init = 0.8 - 0.6 * math.exp(-0.3 * l)
            ot = _attn(q, k, vt, attn_lambda[j], attn_subln_g[j], rel_bias_table, lambda_init)
            x1 = _attn_out(ot, x, mods_l, attn_w_o[j], post_ln_g[l, 0], post_ln_b[l, 0], alpha)
        x = _moe_layer(x1.reshape(bsz * s, d), mods_l, router_w[l], router_b[l],
                       expert_w_gate_up, expert_b_gate_up, expert_w_down, expert_b_down, l,
                       post_ln_g[l, 1], post_ln_b[l, 1], alpha, s).reshape(bsz, s, d)
    return x
```

```python
import functools
import math

import jax
import jax.numpy as jnp
from jax import lax
from jax.experimental import pallas as pl
from jax.experimental.pallas import tpu as pltpu

F32 = jnp.float32
BF16 = jnp.bfloat16
I32 = jnp.int32
U32 = jnp.uint32
HIGHEST = lax.Precision.HIGHEST

CHUNK = 64
CONV_WIDTH = 31
HEAD_DIM = 64
REL_BUCKETS = 32
REL_MAX_DIST = 128
TOP_K = 4
SWIGLU_LIMIT = 7.0
SWIGLU_ALPHA = 1.702
LN_EPS = 1e-5
MASK_VALUE = -1e30
LOG2_E = math.log2(math.e)

SUBLANES = 8
LANES = 128
VMEM_LIMIT_BYTES = 56 * 1024 * 1024

ADA_TN = 1024
SEQ_TILE = 512
PROJ_TILE = 512
CONV_HALO = 32
CONV_ROWS = 256
CONV_COLS = 128
MOE_TOKENS = 256
ROUTER_TOKENS = 1024
MOE_TILE = 512
Q_BLOCK = 128
EXPERT_ROW_SPLITS = (1, 2, 4)
GLU_LOOKAHEAD = 1
SCORE_LOOKAHEAD = 3
CAST_ROWS = 128


def _params(*sem):
    return pltpu.CompilerParams(dimension_semantics=sem, vmem_limit_bytes=VMEM_LIMIT_BYTES)


def _layer_norm(x, g, b):
    mu = jnp.mean(x, axis=-1, keepdims=True)
    xc = x - mu
    var = jnp.mean(xc * xc, axis=-1, keepdims=True)
    return xc * lax.rsqrt(var + LN_EPS) * g + b


def _dot(a, b):
    return jnp.dot(a, b, preferred_element_type=F32)


def _ada_body(c_ref, w_ref, b_ref, o_ref):
    c = c_ref[...]
    cond = c * jax.nn.sigmoid(c)
    o_ref[...] = jnp.dot(cond, w_ref[...], preferred_element_type=F32, precision=HIGHEST) + b_ref[...]


def _ada(c, ada_w, ada_b):
    depth, d, n = ada_w.shape
    bsz = c.shape[0]
    tn = min(ADA_TN, n)
    return pl.pallas_call(
        _ada_body,
        grid=(depth, n // tn),
        in_specs=[
            pl.BlockSpec((bsz, d), lambda l, j: (0, 0)),
            pl.BlockSpec((None, d, tn), lambda l, j: (l, 0, j)),
            pl.BlockSpec((None, 1, tn), lambda l, j: (l, 0, j)),
        ],
        out_specs=pl.BlockSpec((None, bsz, tn), lambda l, j: (l, 0, j)),
        out_shape=jax.ShapeDtypeStruct((depth, bsz, n), F32),
        compiler_params=_params("parallel", "parallel"),
        name="ada",
    )(c, ada_w, ada_b.reshape(depth, 1, n))


def _residual_epilogue(x, y, mod_ref, lng_ref, lnb_ref, alpha, x1_ref):
    x1_ref[...] = _layer_norm(alpha * x + mod_ref[2:3, :] * y, lng_ref[...], lnb_ref[...])


def _moe_input(x1_ref, mod_ref):
    return x1_ref[...] * (1.0 + mod_ref[4:5, :]) + mod_ref[3:4, :]


def _conv_body(x_ref, halo_ref, mod_ref, w1_ref, b1_ref, wdw_ref, bdw_ref, cg_ref, cb_ref, w2_ref, b2_ref,
               lng_ref, lnb_ref, x1_ref, win_ref, v_ref, *, alpha):
    ts, d = x_ref.shape
    i = pl.program_id(1)
    xw = jnp.concatenate([halo_ref[...], x_ref[...]], axis=0)
    h = (xw * (1.0 + mod_ref[1:2, :]) + mod_ref[0:1, :]).astype(BF16)
    rows = min(CONV_ROWS, ts)
    cols = min(CONV_COLS, d)

    def glu(c0):
        a = _dot(h, w1_ref[:, c0:c0 + cols]) + b1_ref[:, c0:c0 + cols]
        g = _dot(h, w1_ref[:, d + c0:d + c0 + cols]) + b1_ref[:, d + c0:d + c0 + cols]
        return a * jax.nn.sigmoid(g)

    off = CONV_HALO - (CONV_WIDTH - 1)
    n_shift = ts + CONV_HALO - SUBLANES
    in_seq = (lax.broadcasted_iota(I32, (ts + CONV_HALO, cols), 0) >= CONV_HALO) | (i > 0)
    starts = list(range(0, d, cols))
    ahead = [glu(c) for c in starts[:GLU_LOOKAHEAD]]
    for n, c0 in enumerate(starts):
        chan = slice(c0, c0 + cols)
        win_ref[0, :, chan] = jnp.where(in_seq, ahead.pop(0), 0.0)
        if n + GLU_LOOKAHEAD < len(starts):
            ahead.append(glu(starts[n + GLU_LOOKAHEAD]))
        for b in range(1, SUBLANES):
            win_ref[b, 0:n_shift, chan] = win_ref[0, b:b + n_shift, chan]
        for r0 in range(0, ts, rows):
            acc = jnp.zeros((rows, cols), F32)
            for j in range(CONV_WIDTH):
                a, b = divmod(off + j, SUBLANES)
                r = r0 + a * SUBLANES
                acc = acc + wdw_ref[j:j + 1, c0:c0 + cols] * win_ref[b, r:r + rows, c0:c0 + cols]
            v_ref[r0:r0 + rows, c0:c0 + cols] = acc
    v = _layer_norm(v_ref[...] + bdw_ref[...], cg_ref[...], cb_ref[...])
    v = (v * jax.nn.sigmoid(v)).astype(BF16)
    y = _dot(v, w2_ref[...]) + b2_ref[...]
    _residual_epilogue(x_ref[...], y, mod_ref, lng_ref, lnb_ref, alpha, x1_ref)


def _conv(x, mods_l, w_pw1, b_pw1, w_dw, b_dw, cln_g, cln_b, w_pw2, b_pw2, ln_g, ln_b, alpha):
    bsz, s, d = x.shape
    ts = min(SEQ_TILE, s)
    hb = ts // CONV_HALO
    row = lambda a: a.reshape(1, d)
    tile = pl.BlockSpec((None, ts, d), lambda b, i: (b, i, 0))
    vec = pl.BlockSpec((1, d), lambda b, i: (0, 0))
    return pl.pallas_call(
        functools.partial(_conv_body, alpha=alpha),
        grid=(bsz, s // ts),
        in_specs=[
            tile,
            pl.BlockSpec((None, CONV_HALO, d), lambda b, i: (b, jnp.maximum(i * hb - 1, 0), 0)),
            pl.BlockSpec((None, 6, d), lambda b, i: (b, 0, 0)),
            pl.BlockSpec((d, 2 * d), lambda b, i: (0, 0)),
            pl.BlockSpec((1, 2 * d), lambda b, i: (0, 0)),
            pl.BlockSpec((CONV_WIDTH, d), lambda b, i: (0, 0)),
            vec, vec, vec,
            pl.BlockSpec((d, d), lambda b, i: (0, 0)),
            vec, vec, vec,
        ],
        out_specs=tile,
        out_shape=jax.ShapeDtypeStruct((bsz, s, d), F32),
        scratch_shapes=[pltpu.VMEM((SUBLANES, ts + CONV_HALO, d), F32), pltpu.VMEM((ts, d), F32)],
        compiler_params=_params("parallel", "parallel"),
        name="conv",
    )(x, x, mods_l, w_pw1.astype(BF16), b_pw1.reshape(1, 2 * d), w_dw, row(b_dw), row(cln_g), row(cln_b),
      w_pw2.astype(BF16), row(b_pw2), row(ln_g), row(ln_b))


def _qkv_body(x_ref, mod_ref, wkv_ref, wq_ref, q_ref, k_ref, vt_ref):
    d = x_ref.shape[-1]
    x = x_ref[...]
    kv = _dot(x.astype(BF16), wkv_ref[...])
    k_ref[...] = kv[:, :d].astype(BF16)
    hd2 = 2 * HEAD_DIM
    for c0 in range(0, d, hd2):
        vt_ref[c0:c0 + hd2, :] = kv[:, d + c0:d + c0 + hd2].T.astype(BF16)
    h = (x * (1.0 + mod_ref[1:2, :]) + mod_ref[0:1, :]).astype(BF16)
    q_ref[...] = (_dot(h, wq_ref[...]) * (HEAD_DIM ** -0.5 * LOG2_E)).astype(BF16)


def _qkv(x, mods_l, w_kv, w_q):
    bsz, s, d = x.shape
    ts = min(PROJ_TILE, s)
    tile = pl.BlockSpec((None, ts, d), lambda b, i: (b, i, 0))
    return pl.pallas_call(
        _qkv_body,
        grid=(bsz, s // ts),
        in_specs=[
            tile,
            pl.BlockSpec((None, 6, d), lambda b, i: (b, 0, 0)),
            pl.BlockSpec((d, 2 * d), lambda b, i: (0, 0)),
            pl.BlockSpec((d, d), lambda b, i: (0, 0)),
        ],
        out_specs=[tile, tile, pl.BlockSpec((None, d, ts), lambda b, i: (b, 0, i))],
        out_shape=[jax.ShapeDtypeStruct((bsz, s, d), BF16)] * 2 + [jax.ShapeDtypeStruct((bsz, d, s), BF16)],
        compiler_params=_params("parallel", "parallel"),
        name="qkv",
    )(x, mods_l, w_kv.astype(BF16), w_q.astype(BF16))


def _t5_bucket(rel):
    nb = REL_BUCKETS // 2
    ret = jnp.where(rel > 0, nb, 0)
    n = jnp.abs(rel)
    max_exact = nb // 2
    large = max_exact + (jnp.log(jnp.maximum(n, 1).astype(F32) / max_exact)
                         / math.log(REL_MAX_DIST / max_exact) * (nb - max_exact)).astype(I32)
    large = jnp.minimum(large, nb - 1)
    return ret + jnp.where(n < max_exact, n, large)


def _bucket_strip(s):
    r = jnp.arange(Q_BLOCK, dtype=I32)[:, None]
    kp = jnp.arange(s, dtype=I32)[None, :] - (s - Q_BLOCK)
    bucket = _t5_bucket(kp - r)
    visible = jnp.floor_divide(kp, CHUNK) <= (r // CHUNK)
    return jnp.where(visible, bucket, REL_BUCKETS)


def _attn_body(tab_ref, q_ref, k_ref, vt_ref, bkt_ref, lam_ref, sg_ref, ot_ref, bias_ref, *, lambda_init, n_heads):
    h = pl.program_id(0)
    b = pl.program_id(1)
    s = q_ref.shape[0]

    @pl.when(b == 0)
    def _():
        bk = bkt_ref[...]
        acc = jnp.full(bk.shape, MASK_VALUE, F32)
        for r in range(REL_BUCKETS):
            acc = jnp.where(bk == r, tab_ref[r * n_heads + h] * LOG2_E, acc)
        bias_ref[:, 0:Q_BLOCK] = acc
        bias_ref[:, Q_BLOCK:] = acc

    lp = lam_ref[...]
    lam = (jnp.exp(jnp.sum(lp[0:1, :] * lp[1:2, :], axis=-1, keepdims=True))
           - jnp.exp(jnp.sum(lp[2:3, :] * lp[3:4, :], axis=-1, keepdims=True)) + lambda_init)
    lane = lax.broadcasted_iota(I32, (Q_BLOCK, 2 * HEAD_DIM), 1)
    nt = (((1,), (1,)), ((), ()))

    def scores(i):
        n_keys = (i + 1) * Q_BLOCK
        q = q_ref[i * Q_BLOCK:(i + 1) * Q_BLOCK, :]
        qq = jnp.concatenate([jnp.where(lane < HEAD_DIM, q, jnp.zeros_like(q)),
                              jnp.where(lane >= HEAD_DIM, q, jnp.zeros_like(q))], axis=0)
        return lax.dot_general(k_ref[0:n_keys, :], qq, nt, preferred_element_type=F32) + bias_ref[s - n_keys:s, :]

    def values(i, p, denom):
        pv = _dot(vt_ref[:, 0:(i + 1) * Q_BLOCK], p)
        o = pv[:, :Q_BLOCK] * (1.0 / denom[:, :Q_BLOCK]) - pv[:, Q_BLOCK:] * (lam / denom[:, Q_BLOCK:])
        o = o * lax.rsqrt(jnp.mean(o * o, axis=0, keepdims=True) + LN_EPS) * sg_ref[...]
        ot_ref[:, i * Q_BLOCK:(i + 1) * Q_BLOCK] = (o * (1.0 - lambda_init)).astype(BF16)

    n_blocks = s // Q_BLOCK
    ahead = [scores(i) for i in range(min(SCORE_LOOKAHEAD, n_blocks))]
    for i in range(n_blocks):
        sc = ahead.pop(0)
        if i + SCORE_LOOKAHEAD < n_blocks:
            ahead.append(scores(i + SCORE_LOOKAHEAD))
        p = jnp.exp2(sc - jnp.max(sc, axis=0, keepdims=True))
        values(i, p.astype(BF16), jnp.sum(p, axis=0, keepdims=True))


def _attn(q, k, vt, lam_p, subln_g, rel_table, lambda_init):
    bsz, s, d = q.shape
    hd2 = 2 * HEAD_DIM
    n_heads = d // hd2
    head = pl.BlockSpec((None, s, hd2), lambda h, b, tab: (b, 0, h))
    head_t = pl.BlockSpec((None, hd2, s), lambda h, b, tab: (b, h, 0))
    grid_spec = pltpu.PrefetchScalarGridSpec(
        num_scalar_prefetch=1,
        grid=(n_heads, bsz),
        in_specs=[
            head, head, head_t,
            pl.BlockSpec((s, Q_BLOCK), lambda h, b, tab: (0, 0)),
            pl.BlockSpec((4, HEAD_DIM), lambda h, b, tab: (0, 0)),
            pl.BlockSpec((hd2, 1), lambda h, b, tab: (0, 0)),
        ],
        out_specs=head_t,
        scratch_shapes=[pltpu.VMEM((s, 2 * Q_BLOCK), F32)],
    )
    return pl.pallas_call(
        functools.partial(_attn_body, lambda_init=lambda_init, n_heads=n_heads),
        grid_spec=grid_spec,
        out_shape=jax.ShapeDtypeStruct((bsz, d, s), BF16),
        compiler_params=_params("arbitrary", "arbitrary"),
        name="attn",
    )(rel_table.reshape(-1), q, k, vt, _bucket_strip(s).T, lam_p, subln_g.reshape(hd2, 1))


def _attn_out_body(ot_ref, x_ref, mod_ref, wo_ref, lng_ref, lnb_ref, x1_ref, *, alpha):
    y = lax.dot_general(ot_ref[...], wo_ref[...], (((0,), (0,)), ((), ())), preferred_element_type=F32)
    _residual_epilogue(x_ref[...], y, mod_ref, lng_ref, lnb_ref, alpha, x1_ref)


def _attn_out(o, x, mods_l, w_o, ln_g, ln_b, alpha):
    bsz, s, d = x.shape
    ts = min(PROJ_TILE, s)
    tile = pl.BlockSpec((None, ts, d), lambda b, i: (b, i, 0))
    vec = pl.BlockSpec((1, d), lambda b, i: (0, 0))
    return pl.pallas_call(
        functools.partial(_attn_out_body, alpha=alpha),
        grid=(bsz, s // ts),
        in_specs=[pl.BlockSpec((None, d, ts), lambda b, i: (b, 0, i)), tile,
                  pl.BlockSpec((None, 6, d), lambda b, i: (b, 0, 0)),
                  pl.BlockSpec((d, d), lambda b, i: (0, 0)), vec, vec],
        out_specs=tile,
        out_shape=jax.ShapeDtypeStruct((bsz, s, d), F32),
        compiler_params=_params("parallel", "parallel"),
        name="attn_out",
    )(o, x, mods_l, w_o.astype(BF16), ln_g.reshape(1, d), ln_b.reshape(1, d))


def _router_body(x1_ref, mod_ref, whi_ref, wlo_ref, b_ref, tri_ref, idx_ref, gate_ref, rank_ref, cnt_ref):
    h = _moe_input(x1_ref, mod_ref)
    h_hi = h.astype(BF16)
    h_lo = (h - h_hi.astype(F32)).astype(BF16)
    nt = (((1,), (1,)), ((), ()))
    logits = (lax.dot_general(whi_ref[...], h_hi, nt, preferred_element_type=F32)
              + (lax.dot_general(whi_ref[...], h_lo, nt, preferred_element_type=F32)
                 + lax.dot_general(wlo_ref[...], h_hi, nt, preferred_element_type=F32))) + b_ref[...]
    n_exp, tr = logits.shape
    eio = lax.broadcasted_iota(I32, (n_exp, tr), 0)
    work = logits
    vals, idxs = [], []
    for _ in range(TOP_K):
        m = jnp.max(work, axis=0, keepdims=True)
        am = jnp.min(jnp.where(work == m, eio, n_exp), axis=0, keepdims=True)
        vals.append(m)
        idxs.append(am)
        work = jnp.where(eio == am, -jnp.inf, work)
    ex = [jnp.exp(v - vals[0]) for v in vals]
    den = ex[0] + ex[1] + ex[2] + ex[3]
    onehot = jnp.zeros((n_exp, tr), F32)
    for k in range(TOP_K):
        onehot = onehot + (eio == idxs[k]).astype(F32)
    before = _dot(onehot.astype(BF16), tri_ref[...])
    for k in range(TOP_K):
        idx_ref[k:k + 1, :] = idxs[k]
        gate_ref[k:k + 1, :] = ex[k] / den
        rank_ref[k:k + 1, :] = jnp.sum(jnp.where(eio == idxs[k], before, 0.0), axis=0, keepdims=True).astype(I32)
    tl = tr // cnt_ref.shape[0]
    for j in range(cnt_ref.shape[0]):
        cnt_ref[j] = jnp.sum(onehot[:, j * tl:(j + 1) * tl], axis=1, keepdims=True).astype(I32)


def _router(x1, mods_l, seq_len, w_r, b_r):
    t, d = x1.shape
    n_exp = w_r.shape[1]
    tl = min(MOE_TOKENS, t)
    tr = min(ROUTER_TOKENS, seq_len)
    per_seq = seq_len // tr
    wt = w_r.T
    wt_hi = wt.astype(BF16)
    wt_lo = (wt - wt_hi.astype(F32)).astype(BF16)
    pos = jnp.arange(tr, dtype=I32)
    tri = ((pos[:, None] < pos[None, :]) & (pos[:, None] // tl == pos[None, :] // tl)).astype(BF16)
    tok = pl.BlockSpec((TOP_K, tr), lambda i: (0, i))
    return pl.pallas_call(
        _router_body,
        grid=(t // tr,),
        in_specs=[
            pl.BlockSpec((tr, d), lambda i: (i, 0)),
            pl.BlockSpec((None, 6, d), lambda i: (i // per_seq, 0, 0)),
            pl.BlockSpec((n_exp, d), lambda i: (0, 0)),
            pl.BlockSpec((n_exp, d), lambda i: (0, 0)),
            pl.BlockSpec((n_exp, 1), lambda i: (0, 0)),
            pl.BlockSpec((tr, tr), lambda i: (0, 0)),
        ],
        out_specs=[tok, tok, tok, pl.BlockSpec((tr // tl, n_exp, 1), lambda i: (i, 0, 0))],
        out_shape=[jax.ShapeDtypeStruct((TOP_K, t), I32), jax.ShapeDtypeStruct((TOP_K, t), F32),
                   jax.ShapeDtypeStruct((TOP_K, t), I32), jax.ShapeDtypeStruct((t // tl, n_exp, 1), I32)],
        compiler_params=_params("parallel"),
        name="router",
    )(x1, mods_l, wt_hi, wt_lo, b_r.reshape(n_exp, 1), tri)


def _rows(ref, start, cnt):
    aligned = lambda v: v if isinstance(v, int) else pl.multiple_of(v, SUBLANES)
    return ref.at[pl.ds(aligned(start), aligned(cnt))]


def _segment_copy(src, src_row, dst, dst_row, cnt, sem):
    @pl.when(cnt > 0)
    def _():
        pltpu.make_async_copy(_rows(src, src_row, cnt), _rows(dst, dst_row, cnt), sem).start()


def _wait_rows(ref, cnt, sem):
    @pl.when(cnt > 0)
    def _():
        pltpu.make_async_copy(_rows(ref, 0, cnt), _rows(ref, 0, cnt), sem).wait()


def _zero_fill(zero_ref, xs_hbm, start, cnt, sem):
    zr = zero_ref.shape[0]
    n_full = cnt // zr

    def full(r, c):
        _segment_copy(zero_ref, 0, xs_hbm, start + r * zr, zr, sem)
        return c

    lax.fori_loop(0, n_full, full, 0)
    _segment_copy(zero_ref, 0, xs_hbm, start + n_full * zr, cnt - n_full * zr, sem)


def _pack_halves(x):
    hd = x.shape[1] // 2
    lo = lax.bitcast_convert_type(x[:, :hd], U32)
    hi = lax.bitcast_convert_type(x[:, hd:], U32)
    return (lo >> 16) | (hi & jnp.uint32(0xFFFF0000))


def _unpack_halves(words):
    lo = lax.bitcast_convert_type(words << 16, F32).astype(BF16)
    hi = lax.bitcast_convert_type(words & jnp.uint32(0xFFFF0000), F32).astype(BF16)
    return lo, hi


def _one_hot_hits(iota, pos_of):
    hit = iota == pos_of(0)
    for k in range(1, TOP_K):
        hit = hit | (iota == pos_of(k))
    return hit


def _dispatch_body(goff_ref, loff_ref, cp_ref, ltot_ref, pad_start_ref, pad_cnt_ref, x1_ref, mod_ref, lp_ref,
                   gate_ref, xs_hbm, xl_ref, zero_ref, sem, zsem, *, n_exp):
    j = pl.program_id(0)
    tl, d = x1_ref.shape
    lr = xl_ref.shape[1]

    @pl.when(j == 0)
    def _():
        zero_ref[...] = jnp.zeros_like(zero_ref)

        def per_range(e, carry):
            _zero_fill(zero_ref, xs_hbm, pad_start_ref[e], pad_cnt_ref[e], zsem)
            return carry
        lax.fori_loop(0, pad_start_ref.shape[0], per_range, 0)

    def start_segments(tile):
        def per_expert(e, carry):
            seg = tile * n_exp + e
            _segment_copy(xl_ref.at[tile % 2], loff_ref[seg], xs_hbm, goff_ref[seg], cp_ref[seg], sem.at[tile % 2])
            return carry
        lax.fori_loop(0, n_exp, per_expert, 0)

    def wait_segments(tile):
        _wait_rows(xs_hbm, ltot_ref[tile], sem.at[tile % 2])

    @pl.when(j >= 2)
    def _():
        wait_segments(j - 2)

    buf = xl_ref.at[j % 2]
    riota = lax.broadcasted_iota(I32, (lr, tl), 0)
    perm = _one_hot_hits(riota, lambda k: lp_ref[k:k + 1, :]).astype(F32).astype(BF16)
    rows = _dot(perm, _moe_input(x1_ref, mod_ref).astype(BF16))
    gsel = jnp.zeros((lr, tl), F32)
    for k in range(TOP_K):
        gsel = gsel + jnp.where(riota == lp_ref[k:k + 1, :], gate_ref[k:k + 1, :], 0.0)
    rowg = jnp.broadcast_to(jnp.sum(gsel, axis=1, keepdims=True), (lr, LANES))
    buf[:, d // 2:] = lax.bitcast_convert_type(rowg, U32)
    buf[:, 0:d // 2] = _pack_halves(rows)
    start_segments(j)

    @pl.when(j == pl.num_programs(0) - 1)
    def _():
        @pl.when(j >= 1)
        def _():
            wait_segments(j - 1)
        wait_segments(j)
        n_zero = lax.fori_loop(0, pad_cnt_ref.shape[0], lambda e, acc: acc + pad_cnt_ref[e], jnp.int32(0))
        _wait_rows(xs_hbm, n_zero, zsem)


def _dispatch(x1, mods_l, seq_len, lp, gates, goff, loff, cp, ltot, pad_start, pad_cnt, n_rows):
    t, d = x1.shape
    tl = min(MOE_TOKENS, t)
    per_seq = seq_len // tl
    n_exp = goff.shape[0] // (t // tl)
    lr = TOP_K * tl + n_exp * SUBLANES
    tok = pl.BlockSpec((TOP_K, tl), lambda j, *_: (0, j))
    grid_spec = pltpu.PrefetchScalarGridSpec(
        num_scalar_prefetch=6,
        grid=(t // tl,),
        in_specs=[pl.BlockSpec((tl, d), lambda j, *_: (j, 0)),
                  pl.BlockSpec((None, 6, d), lambda j, *_: (j // per_seq, 0, 0)), tok, tok],
        out_specs=pl.BlockSpec(memory_space=pl.ANY),
        scratch_shapes=[pltpu.VMEM((2, lr, d // 2 + LANES), U32), pltpu.VMEM((MOE_TILE, d // 2 + LANES), U32),
                        pltpu.SemaphoreType.DMA((2,)), pltpu.SemaphoreType.DMA],
    )
    return pl.pallas_call(
        functools.partial(_dispatch_body, n_exp=n_exp),
        grid_spec=grid_spec,
        out_shape=jax.ShapeDtypeStruct((n_rows, d // 2 + LANES), U32),
        compiler_params=_params("arbitrary"),
        name="dispatch",
    )(goff, loff, cp, ltot, pad_start, pad_cnt, x1, mods_l, lp, gates)


def _experts_body(be_ref, bsrc_ref, rows_ref, first_ref, slot_ref, next_ref, xs_ref, wgu_hbm, bgu_ref, wdn_hbm,
                  bdn_ref, ys_ref, wgu_f32, wdn_f32, wgu_bf, wdn_bf, sem_gu, sem_dn, *, layer):
    i = pl.program_id(0)
    e = be_ref[i]
    d, f2 = wgu_bf.shape
    f = f2 // 2

    def fetch(expert, slot, start):
        for hbm, buf, sem in ((wgu_hbm, wgu_f32, sem_gu), (wdn_hbm, wdn_f32, sem_dn)):
            dma = pltpu.make_async_copy(hbm.at[layer, expert], buf.at[slot], sem.at[slot])
            dma.start() if start else dma.wait()

    @pl.when(first_ref[i] == 1)
    def _():
        slot = slot_ref[i]

        @pl.when(i == 0)
        def _():
            fetch(e, slot, True)

        fetch(e, slot, False)

        @pl.when(next_ref[i] >= 0)
        def _():
            fetch(next_ref[i], 1 - slot, True)

        def cast(ref_in, ref_out):
            def step(r, c):
                rows = pl.ds(pl.multiple_of(r * CAST_ROWS, CAST_ROWS), CAST_ROWS)
                ref_out[rows, :] = ref_in[rows, :].astype(BF16)
                return c
            lax.fori_loop(0, ref_in.shape[0] // CAST_ROWS, step, 0)
        cast(wgu_f32.at[slot], wgu_bf)
        cast(wdn_f32.at[slot], wdn_bf)

    hd = d // 2
    tm = xs_ref.shape[0]

    def mlp(n):
        x_lo, x_hi = _unpack_halves(xs_ref[0:n, 0:hd])
        proj = lambda cols: _dot(x_lo, wgu_bf[0:hd, cols]) + _dot(x_hi, wgu_bf[hd:, cols]) + bgu_ref[:, cols]
        gate = jnp.minimum(proj(slice(0, f)), SWIGLU_LIMIT)
        lin = jnp.clip(proj(slice(f, f2)), -SWIGLU_LIMIT, SWIGLU_LIMIT)
        act = (gate * jax.nn.sigmoid(SWIGLU_ALPHA * gate) * (lin + 1.0)).astype(BF16)
        row_gate = lax.bitcast_convert_type(xs_ref[0:n, hd:hd + 1], F32)
        y = (_dot(act, wdn_bf[...]) + bdn_ref[...]) * row_gate
        ys_ref[0:n, :] = _pack_halves(y.astype(BF16).astype(F32))
        if n < tm:
            ys_ref[n:, :] = jnp.zeros((tm - n, hd), U32)

    rows = rows_ref[i]
    sizes = [tm // s for s in EXPERT_ROW_SPLITS]
    for n, smaller in zip(sizes, sizes[1:] + [0]):
        @pl.when((rows > smaller) & (rows <= n))
        def _(n=n):
            mlp(n)

    @pl.when(rows == 0)
    def _():
        ys_ref[...] = jnp.zeros_like(ys_ref)


def _experts(xs, block_e, block_src, block_rows, block_first, block_slot, block_next, w_gu, b_gu, w_dn, b_dn, layer):
    n_rows, dx = xs.shape
    depth, n_exp, d, f2 = w_gu.shape
    f = f2 // 2
    tm = MOE_TILE
    grid_spec = pltpu.PrefetchScalarGridSpec(
        num_scalar_prefetch=6,
        grid=(n_rows // tm,),
        in_specs=[
            pl.BlockSpec((tm, dx), lambda i, be, bs, *_: (bs[i], 0)),
            pl.BlockSpec(memory_space=pl.ANY),
            pl.BlockSpec((None, None, 1, f2), lambda i, be, *_: (layer, be[i], 0, 0)),
            pl.BlockSpec(memory_space=pl.ANY),
            pl.BlockSpec((None, None, 1, d), lambda i, be, *_: (layer, be[i], 0, 0)),
        ],
        out_specs=pl.BlockSpec((tm, d // 2), lambda i, *_: (i, 0)),
        scratch_shapes=[pltpu.VMEM((2, d, f2), F32), pltpu.VMEM((2, f, d), F32),
                        pltpu.VMEM((d, f2), BF16), pltpu.VMEM((f, d), BF16),
                        pltpu.SemaphoreType.DMA((2,)), pltpu.SemaphoreType.DMA((2,))],
    )
    return pl.pallas_call(
        functools.partial(_experts_body, layer=layer),
        grid_spec=grid_spec,
        out_shape=jax.ShapeDtypeStruct((n_rows, d // 2), U32),
        compiler_params=_params("arbitrary"),
        name="experts",
    )(block_e, block_src, block_rows, block_first, block_slot, block_next, xs, w_gu,
      b_gu.reshape(depth, n_exp, 1, f2), w_dn, b_dn.reshape(depth, n_exp, 1, d))


def _combine_body(goff_ref, loff_ref, cp_ref, ltot_ref, ys_hbm, x_ref, lp_ref, mod_ref, lng_ref, lnb_ref, o_ref,
                  yl_ref, sem, *, n_exp, alpha):
    j = pl.program_id(0)
    tl = x_ref.shape[0]
    lr, hd = yl_ref.shape[1:]

    def fetch(tile):
        buf = yl_ref.at[tile % 2]

        def per_expert(e, carry):
            seg = tile * n_exp + e
            _segment_copy(ys_hbm, goff_ref[seg], buf, loff_ref[seg], cp_ref[seg], sem.at[tile % 2])
            return carry
        lax.fori_loop(0, n_exp, per_expert, 0)

        def zero_rows(r, carry):
            buf[pl.ds(pl.multiple_of(r * SUBLANES, SUBLANES), SUBLANES), :] = jnp.zeros((SUBLANES, hd), U32)
            return carry
        lax.fori_loop(ltot_ref[tile] // SUBLANES, lr // SUBLANES, zero_rows, 0)

    @pl.when(j == 0)
    def _():
        fetch(j)

    @pl.when(j + 1 < pl.num_programs(0))
    def _():
        fetch(j + 1)

    liota = lax.broadcasted_iota(I32, (tl, lr), 1)
    pick = _one_hot_hits(liota, lambda k: lp_ref[:, k:k + 1]).astype(F32).astype(BF16)
    _wait_rows(ys_hbm, ltot_ref[j], sem.at[j % 2])
    y_lo, y_hi = _unpack_halves(yl_ref[j % 2])
    y = jnp.concatenate([_dot(pick, y_lo), _dot(pick, y_hi)], axis=1)
    o_ref[...] = _layer_norm(alpha * x_ref[...] + mod_ref[5:6, :] * y, lng_ref[...], lnb_ref[...])


def _combine(ys, lp_t, goff, loff, cp, ltot, x1, mods_l, ln_g, ln_b, alpha, seq_len):
    t, d = x1.shape
    tl = min(MOE_TOKENS, t)
    n_exp = goff.shape[0] // (t // tl)
    lr = TOP_K * tl + n_exp * SUBLANES
    per_seq = seq_len // tl
    vec = pl.BlockSpec((1, d), lambda j, *_: (0, 0))
    grid_spec = pltpu.PrefetchScalarGridSpec(
        num_scalar_prefetch=4,
        grid=(t // tl,),
        in_specs=[
            pl.BlockSpec(memory_space=pl.ANY),
            pl.BlockSpec((tl, d), lambda j, *_: (j, 0)),
            pl.BlockSpec((tl, TOP_K), lambda j, *_: (j, 0)),
            pl.BlockSpec((None, 6, d), lambda j, *_: (j // per_seq, 0, 0)),
            vec, vec,
        ],
        out_specs=pl.BlockSpec((tl, d), lambda j, *_: (j, 0)),
        scratch_shapes=[pltpu.VMEM((2, lr, d // 2), U32), pltpu.SemaphoreType.DMA((2,))],
    )
    return pl.pallas_call(
        functools.partial(_combine_body, n_exp=n_exp, alpha=alpha),
        grid_spec=grid_spec,
        out_shape=jax.ShapeDtypeStruct((t, d), F32),
        compiler_params=_params("arbitrary"),
        name="combine",
    )(goff, loff, cp, ltot, ys, x1, lp_t, mods_l, ln_g.reshape(1, d), ln_b.reshape(1, d))


def _moe_layer(x1, mods_l, w_r, b_r, w_gu, b_gu, w_dn, b_dn, layer, ln_g, ln_b, alpha, seq_len):
    t, d = x1.shape
    n_exp = w_r.shape[1]
    tm = MOE_TILE
    tl = min(MOE_TOKENS, t)
    n_tiles = t // tl
    idx, gates, rank, cnt = _router(x1, mods_l, seq_len, w_r, b_r)
    cp = (cnt[:, :, 0] + SUBLANES - 1) // SUBLANES * SUBLANES
    tot = jnp.sum(cp, axis=0)
    padded = (tot + tm - 1) // tm * tm
    pend = jnp.cumsum(padded)
    pstart = pend - padded
    goff = pstart[None, :] + jnp.cumsum(cp, axis=0) - cp
    loff = jnp.cumsum(cp, axis=1) - cp
    ltot = jnp.sum(cp, axis=1).astype(I32)
    is_e = idx[..., None] == jnp.arange(n_exp, dtype=I32)
    lp = jnp.sum(jnp.where(is_e, jnp.repeat(loff, tl, axis=0)[None], 0), axis=-1) + rank
    n_blocks = -(-(t * TOP_K + n_tiles * n_exp * (SUBLANES - 1)) // tm) + n_exp
    n_valid = pend[-1] // tm
    blk = jnp.arange(n_blocks, dtype=I32)
    block_src = jnp.minimum(blk, n_valid - 1)
    block_e = jnp.sum(pend[None, :] <= (block_src * tm)[:, None], axis=1).astype(I32)
    used = padded > 0
    experts = jnp.arange(n_exp, dtype=I32)
    next_used = jnp.flip(lax.cummin(jnp.flip(jnp.where(used, experts, n_exp))))
    next_used = jnp.concatenate([next_used[1:], jnp.full((1,), n_exp, I32)])
    next_used = jnp.where(next_used < n_exp, next_used, -1)
    slot_of = (jnp.cumsum(used.astype(I32)) - 1) % 2
    of_block = lambda table: jnp.sum(jnp.where(block_e[:, None] == experts[None, :], table[None, :], 0), axis=1)
    block_first = ((blk * tm == of_block(pstart)) & (blk < n_valid)).astype(I32)
    block_slot = of_block(slot_of).astype(I32)
    block_next = of_block(next_used).astype(I32)
    block_rows = jnp.where(blk < n_valid, jnp.clip(of_block(pstart + tot) - blk * tm, 0, tm), 0).astype(I32)
    n_rows = n_blocks * tm
    pad_start = jnp.concatenate([pstart + tot, pend[-1:]]).astype(I32)
    pad_cnt = jnp.concatenate([padded - tot, n_rows - pend[-1:]]).astype(I32)
    flat = lambda a: a.reshape(-1).astype(I32)
    seg = (flat(goff), flat(loff), flat(cp), ltot)
    xs = _dispatch(x1, mods_l, seq_len, lp, gates, *seg, pad_start, pad_cnt, n_rows)
    ys = _experts(xs, block_e, block_src.astype(I32), block_rows, block_first, block_slot, block_next,
                  w_gu, b_gu, w_dn, b_dn, layer)
    return _combine(ys, lp.T, *seg, x1, mods_l, ln_g, ln_b, alpha, seq_len)


def kernel(x, c, ada_w, ada_b, post_ln_g, post_ln_b, conv_w_pw1, conv_b_pw1, conv_w_dw, conv_b_dw, conv_ln_g, conv_ln_b, conv_w_pw2, conv_b_pw2, w_kv, attn_w_q, attn_lambda, attn_subln_g, attn_w_o, rel_bias_table, router_w, router_b, expert_w_gate_up, expert_b_gate_up, expert_w_down, expert_b_down):
    bsz, s, d = x.shape
    depth = ada_w.shape[0]
    n_a = depth // 2
    alpha = (2 * depth) ** 0.25
    mods = _ada(c, ada_w, ada_b).reshape(depth, bsz, 6, d)
    q = k = vt = None
    for l in range(depth):
        mods_l = mods[l]
        if l < n_a:
            x1 = _conv(x, mods_l, conv_w_pw1[l], conv_b_pw1[l], conv_w_dw[l], conv_b_dw[l], conv_ln_g[l],
                       conv_ln_b[l], conv_w_pw2[l], conv_b_pw2[l], post_ln_g[l, 0], post_ln_b[l, 0], alpha)
        else:
            j = l - n_a
            if j == 0:
                q, k, vt = _qkv(x, mods_l, w_kv, attn_w_q[j])
            else:
                q = _qkv(x, mods_l, w_kv, attn_w_q[j])[0]
            lambda_init = 0.8 - 0.6 * math.exp(-0.3 * l)
            ot = _attn(q, k, vt, attn_lambda[j], attn_subln_g[j], rel_bias_table, lambda_init)
            x1 = _attn_out(ot, x, mods_l, attn_w_o[j], post_ln_g[l, 0], post_ln_b[l, 0], alpha)
        x = _moe_layer(x1.reshape(bsz * s, d), mods_l, router_w[l], router_b[l],
                       expert_w_gate_up, expert_b_gate_up, expert_w_down, expert_b_down, l,
                       post_ln_g[l, 1], post_ln_b[l, 1], alpha, s).reshape(bsz, s, d)
    return x
```

```python
import functools
import math

import jax
import jax.numpy as jnp
from jax import lax
from jax.experimental import pallas as pl
from jax.experimental.pallas import tpu as pltpu

F32 = jnp.float32
BF16 = jnp.bfloat16
I32 = jnp.int32
U32 = jnp.uint32
HIGHEST = lax.Precision.HIGHEST

CHUNK = 64
CONV_WIDTH = 31
HEAD_DIM = 64
REL_BUCKETS = 32
REL_MAX_DIST = 128
TOP_K = 4
SWIGLU_LIMIT = 7.0
SWIGLU_ALPHA = 1.702
LN_EPS = 1e-5
MASK_VALUE = -1e30
LOG2_E = math.log2(math.e)

SUBLANES = 8
LANES = 128
VMEM_LIMIT_BYTES = 56 * 1024 * 1024

ADA_TN = 1024
SEQ_TILE = 512
PROJ_TILE = 512
CONV_HALO = 32
CONV_ROWS = 256
CONV_COLS = 128
MOE_TOKENS = 256
ROUTER_TOKENS = 1024
MOE_TILE = 1024
Q_BLOCK = 128
EXPERT_ROW_SPLITS = (1, 2, 4, 8)
GLU_LOOKAHEAD = 1
SCORE_LOOKAHEAD = 3
CAST_ROWS = 128


def _params(*sem):
    return pltpu.CompilerParams(dimension_semantics=sem, vmem_limit_bytes=VMEM_LIMIT_BYTES)


def _layer_norm(x, g, b):
    mu = jnp.mean(x, axis=-1, keepdims=True)
    xc = x - mu
    var = jnp.mean(xc * xc, axis=-1, keepdims=True)
    return xc * lax.rsqrt(var + LN_EPS) * g + b


def _dot(a, b):
    return jnp.dot(a, b, preferred_element_type=F32)


def _ada_body(c_ref, w_ref, b_ref, o_ref):
    c = c_ref[...]
    cond = c * jax.nn.sigmoid(c)
    o_ref[...] = jnp.dot(cond, w_ref[...], preferred_element_type=F32, precision=HIGHEST) + b_ref[...]


def _ada(c, ada_w, ada_b):
    depth, d, n = ada_w.shape
    bsz = c.shape[0]
    tn = min(ADA_TN, n)
    return pl.pallas_call(
        _ada_body,
        grid=(depth, n // tn),
        in_specs=[
            pl.BlockSpec((bsz, d), lambda l, j: (0, 0)),
            pl.BlockSpec((None, d, tn), lambda l, j: (l, 0, j)),
            pl.BlockSpec((None, 1, tn), lambda l, j: (l, 0, j)),
        ],
        out_specs=pl.BlockSpec((None, bsz, tn), lambda l, j: (l, 0, j)),
        out_shape=jax.ShapeDtypeStruct((depth, bsz, n), F32),
        compiler_params=_params("parallel", "parallel"),
        name="ada",
    )(c, ada_w, ada_b.reshape(depth, 1, n))


def _residual_epilogue(x, y, mod_ref, lng_ref, lnb_ref, alpha, x1_ref):
    x1_ref[...] = _layer_norm(alpha * x + mod_ref[2:3, :] * y, lng_ref[...], lnb_ref[...])


def _moe_input(x1_ref, mod_ref):
    return x1_ref[...] * (1.0 + mod_ref[4:5, :]) + mod_ref[3:4, :]


def _conv_body(x_ref, halo_ref, mod_ref, w1_ref, b1_ref, wdw_ref, bdw_ref, cg_ref, cb_ref, w2_ref, b2_ref,
               lng_ref, lnb_ref, x1_ref, win_ref, v_ref, *, alpha):
    ts, d = x_ref.shape
    i = pl.program_id(1)
    xw = jnp.concatenate([halo_ref[...], x_ref[...]], axis=0)
    h = (xw * (1.0 + mod_ref[1:2, :]) + mod_ref[0:1, :]).astype(BF16)
    rows = min(CONV_ROWS, ts)
    cols = min(CONV_COLS, d)

    def glu(c0):
        a = _dot(h, w1_ref[:, c0:c0 + cols]) + b1_ref[:, c0:c0 + cols]
        g = _dot(h, w1_ref[:, d + c0:d + c0 + cols]) + b1_ref[:, d + c0:d + c0 + cols]
        return a * jax.nn.sigmoid(g)

    off = CONV_HALO - (CONV_WIDTH - 1)
    n_shift = ts + CONV_HALO - SUBLANES
    in_seq = (lax.broadcasted_iota(I32, (ts + CONV_HALO, cols), 0) >= CONV_HALO) | (i > 0)
    starts = list(range(0, d, cols))
    ahead = [glu(c) for c in starts[:GLU_LOOKAHEAD]]
    for n, c0 in enumerate(starts):
        chan = slice(c0, c0 + cols)
        win_ref[0, :, chan] = jnp.where(in_seq, ahead.pop(0), 0.0)
        if n + GLU_LOOKAHEAD < len(starts):
            ahead.append(glu(starts[n + GLU_LOOKAHEAD]))
        for b in range(1, SUBLANES):
            win_ref[b, 0:n_shift, chan] = win_ref[0, b:b + n_shift, chan]
        for r0 in range(0, ts, rows):
            acc = jnp.zeros((rows, cols), F32)
            for j in range(CONV_WIDTH):
                a, b = divmod(off + j, SUBLANES)
                r = r0 + a * SUBLANES
                acc = acc + wdw_ref[j:j + 1, c0:c0 + cols] * win_ref[b, r:r + rows, c0:c0 + cols]
            v_ref[r0:r0 + rows, c0:c0 + cols] = acc
    v = _layer_norm(v_ref[...] + bdw_ref[...], cg_ref[...], cb_ref[...])
    v = (v * jax.nn.sigmoid(v)).astype(BF16)
    y = _dot(v, w2_ref[...]) + b2_ref[...]
    _residual_epilogue(x_ref[...], y, mod_ref, lng_ref, lnb_ref, alpha, x1_ref)


def _conv(x, mods_l, w_pw1, b_pw1, w_dw, b_dw, cln_g, cln_b, w_pw2, b_pw2, ln_g, ln_b, alpha):
    bsz, s, d = x.shape
    ts = min(SEQ_TILE, s)
    hb = ts // CONV_HALO
    row = lambda a: a.reshape(1, d)
    tile = pl.BlockSpec((None, ts, d), lambda b, i: (b, i, 0))
    vec = pl.BlockSpec((1, d), lambda b, i: (0, 0))
    return pl.pallas_call(
        functools.partial(_conv_body, alpha=alpha),
        grid=(bsz, s // ts),
        in_specs=[
            tile,
            pl.BlockSpec((None, CONV_HALO, d), lambda b, i: (b, jnp.maximum(i * hb - 1, 0), 0)),
            pl.BlockSpec((None, 6, d), lambda b, i: (b, 0, 0)),
            pl.BlockSpec((d, 2 * d), lambda b, i: (0, 0)),
            pl.BlockSpec((1, 2 * d), lambda b, i: (0, 0)),
            pl.BlockSpec((CONV_WIDTH, d), lambda b, i: (0, 0)),
            vec, vec, vec,
            pl.BlockSpec((d, d), lambda b, i: (0, 0)),
            vec, vec, vec,
        ],
        out_specs=tile,
        out_shape=jax.ShapeDtypeStruct((bsz, s, d), F32),
        scratch_shapes=[pltpu.VMEM((SUBLANES, ts + CONV_HALO, d), F32), pltpu.VMEM((ts, d), F32)],
        compiler_params=_params("parallel", "parallel"),
        name="conv",
    )(x, x, mods_l, w_pw1.astype(BF16), b_pw1.reshape(1, 2 * d), w_dw, row(b_dw), row(cln_g), row(cln_b),
      w_pw2.astype(BF16), row(b_pw2), row(ln_g), row(ln_b))


def _qkv_body(x_ref, mod_ref, wkv_ref, wq_ref, q_ref, k_ref, vt_ref):
    d = x_ref.shape[-1]
    x = x_ref[...]
    kv = _dot(x.astype(BF16), wkv_ref[...])
    k_ref[...] = kv[:, :d].astype(BF16)
    hd2 = 2 * HEAD_DIM
    for c0 in range(0, d, hd2):
        vt_ref[c0:c0 + hd2, :] = kv[:, d + c0:d + c0 + hd2].T.astype(BF16)
    h = (x * (1.0 + mod_ref[1:2, :]) + mod_ref[0:1, :]).astype(BF16)
    q_ref[...] = (_dot(h, wq_ref[...]) * (HEAD_DIM ** -0.5 * LOG2_E)).astype(BF16)


def _qkv(x, mods_l, w_kv, w_q):
    bsz, s, d = x.shape
    ts = min(PROJ_TILE, s)
    tile = pl.BlockSpec((None, ts, d), lambda b, i: (b, i, 0))
    return pl.pallas_call(
        _qkv_body,
        grid=(bsz, s // ts),
        in_specs=[
            tile,
            pl.BlockSpec((None, 6, d), lambda b, i: (b, 0, 0)),
            pl.BlockSpec((d, 2 * d), lambda b, i: (0, 0)),
            pl.BlockSpec((d, d), lambda b, i: (0, 0)),
        ],
        out_specs=[tile, tile, pl.BlockSpec((None, d, ts), lambda b, i: (b, 0, i))],
        out_shape=[jax.ShapeDtypeStruct((bsz, s, d), BF16)] * 2 + [jax.ShapeDtypeStruct((bsz, d, s), BF16)],
        compiler_params=_params("parallel", "parallel"),
        name="qkv",
    )(x, mods_l, w_kv.astype(BF16), w_q.astype(BF16))


def _t5_bucket(rel):
    nb = REL_BUCKETS // 2
    ret = jnp.where(rel > 0, nb, 0)
    n = jnp.abs(rel)
    max_exact = nb // 2
    large = max_exact + (jnp.log(jnp.maximum(n, 1).astype(F32) / max_exact)
                         / math.log(REL_MAX_DIST / max_exact) * (nb - max_exact)).astype(I32)
    large = jnp.minimum(large, nb - 1)
    return ret + jnp.where(n < max_exact, n, large)


def _bucket_strip(s):
    r = jnp.arange(Q_BLOCK, dtype=I32)[:, None]
    kp = jnp.arange(s, dtype=I32)[None, :] - (s - Q_BLOCK)
    bucket = _t5_bucket(kp - r)
    visible = jnp.floor_divide(kp, CHUNK) <= (r // CHUNK)
    return jnp.where(visible, bucket, REL_BUCKETS)


def _attn_body(tab_ref, q_ref, k_ref, vt_ref, bkt_ref, lam_ref, sg_ref, ot_ref, bias_ref, *, lambda_init, n_heads):
    h = pl.program_id(0)
    b = pl.program_id(1)
    s = q_ref.shape[0]

    @pl.when(b == 0)
    def _():
        bk = bkt_ref[...]
        acc = jnp.full(bk.shape, MASK_VALUE, F32)
        for r in range(REL_BUCKETS):
            acc = jnp.where(bk == r, tab_ref[r * n_heads + h] * LOG2_E, acc)
        bias_ref[:, 0:Q_BLOCK] = acc
        bias_ref[:, Q_BLOCK:] = acc

    lp = lam_ref[...]
    lam = (jnp.exp(jnp.sum(lp[0:1, :] * lp[1:2, :], axis=-1, keepdims=True))
           - jnp.exp(jnp.sum(lp[2:3, :] * lp[3:4, :], axis=-1, keepdims=True)) + lambda_init)
    lane = lax.broadcasted_iota(I32, (Q_BLOCK, 2 * HEAD_DIM), 1)
    nt = (((1,), (1,)), ((), ()))

    def scores(i):
        n_keys = (i + 1) * Q_BLOCK
        q = q_ref[i * Q_BLOCK:(i + 1) * Q_BLOCK, :]
        qq = jnp.concatenate([jnp.where(lane < HEAD_DIM, q, jnp.zeros_like(q)),
                              jnp.where(lane >= HEAD_DIM, q, jnp.zeros_like(q))], axis=0)
        return lax.dot_general(k_ref[0:n_keys, :], qq, nt, preferred_element_type=F32) + bias_ref[s - n_keys:s, :]

    def values(i, p, denom):
        pv = _dot(vt_ref[:, 0:(i + 1) * Q_BLOCK], p)
        o = pv[:, :Q_BLOCK] * (1.0 / denom[:, :Q_BLOCK]) - pv[:, Q_BLOCK:] * (lam / denom[:, Q_BLOCK:])
        o = o * lax.rsqrt(jnp.mean(o * o, axis=0, keepdims=True) + LN_EPS) * sg_ref[...]
        ot_ref[:, i * Q_BLOCK:(i + 1) * Q_BLOCK] = (o * (1.0 - lambda_init)).astype(BF16)

    n_blocks = s // Q_BLOCK
    ahead = [scores(i) for i in range(min(SCORE_LOOKAHEAD, n_blocks))]
    for i in range(n_blocks):
        sc = ahead.pop(0)
        if i + SCORE_LOOKAHEAD < n_blocks:
            ahead.append(scores(i + SCORE_LOOKAHEAD))
        p = jnp.exp2(sc - jnp.max(sc, axis=0, keepdims=True))
        values(i, p.astype(BF16), jnp.sum(p, axis=0, keepdims=True))


def _attn(q, k, vt, lam_p, subln_g, rel_table, lambda_init):
    bsz, s, d = q.shape
    hd2 = 2 * HEAD_DIM
    n_heads = d // hd2
    head = pl.BlockSpec((None, s, hd2), lambda h, b, tab: (b, 0, h))
    head_t = pl.BlockSpec((None, hd2, s), lambda h, b, tab: (b, h, 0))
    grid_spec = pltpu.PrefetchScalarGridSpec(
        num_scalar_prefetch=1,
        grid=(n_heads, bsz),
        in_specs=[
            head, head, head_t,
            pl.BlockSpec((s, Q_BLOCK), lambda h, b, tab: (0, 0)),
            pl.BlockSpec((4, HEAD_DIM), lambda h, b, tab: (0, 0)),
            pl.BlockSpec((hd2, 1), lambda h, b, tab: (0, 0)),
        ],
        out_specs=head_t,
        scratch_shapes=[pltpu.VMEM((s, 2 * Q_BLOCK), F32)],
    )
    return pl.pallas_call(
        functools.partial(_attn_body, lambda_init=lambda_init, n_heads=n_heads),
        grid_spec=grid_spec,
        out_shape=jax.ShapeDtypeStruct((bsz, d, s), BF16),
        compiler_params=_params("arbitrary", "arbitrary"),
        name="attn",
    )(rel_table.reshape(-1), q, k, vt, _bucket_strip(s).T, lam_p, subln_g.reshape(hd2, 1))


def _attn_out_body(ot_ref, x_ref, mod_ref, wo_ref, lng_ref, lnb_ref, x1_ref, *, alpha):
    y = lax.dot_general(ot_ref[...], wo_ref[...], (((0,), (0,)), ((), ())), preferred_element_type=F32)
    _residual_epilogue(x_ref[...], y, mod_ref, lng_ref, lnb_ref, alpha, x1_ref)


def _attn_out(o, x, mods_l, w_o, ln_g, ln_b, alpha):
    bsz, s, d = x.shape
    ts = min(PROJ_TILE, s)
    tile = pl.BlockSpec((None, ts, d), lambda b, i: (b, i, 0))
    vec = pl.BlockSpec((1, d), lambda b, i: (0, 0))
    return pl.pallas_call(
        functools.partial(_attn_out_body, alpha=alpha),
        grid=(bsz, s // ts),
        in_specs=[pl.BlockSpec((None, d, ts), lambda b, i: (b, 0, i)), tile,
                  pl.BlockSpec((None, 6, d), lambda b, i: (b, 0, 0)),
                  pl.BlockSpec((d, d), lambda b, i: (0, 0)), vec, vec],
        out_specs=tile,
        out_shape=jax.ShapeDtypeStruct((bsz, s, d), F32),
        compiler_params=_params("parallel", "parallel"),
        name="attn_out",
    )(o, x, mods_l, w_o.astype(BF16), ln_g.reshape(1, d), ln_b.reshape(1, d))


def _router_body(x1_ref, mod_ref, whi_ref, wlo_ref, b_ref, tri_ref, idx_ref, gate_ref, rank_ref, cnt_ref):
    h = _moe_input(x1_ref, mod_ref)
    h_hi = h.astype(BF16)
    h_lo = (h - h_hi.astype(F32)).astype(BF16)
    nt = (((1,), (1,)), ((), ()))
    logits = (lax.dot_general(whi_ref[...], h_hi, nt, preferred_element_type=F32)
              + (lax.dot_general(whi_ref[...], h_lo, nt, preferred_element_type=F32)
                 + lax.dot_general(wlo_ref[...], h_hi, nt, preferred_element_type=F32))) + b_ref[...]
    n_exp, tr = logits.shape
    eio = lax.broadcasted_iota(I32, (n_exp, tr), 0)
    work = logits
    vals, idxs = [], []
    for _ in range(TOP_K):
        m = jnp.max(work, axis=0, keepdims=True)
        am = jnp.min(jnp.where(work == m, eio, n_exp), axis=0, keepdims=True)
        vals.append(m)
        idxs.append(am)
        work = jnp.where(eio == am, -jnp.inf, work)
    ex = [jnp.exp(v - vals[0]) for v in vals]
    den = ex[0] + ex[1] + ex[2] + ex[3]
    onehot = jnp.zeros((n_exp, tr), F32)
    for k in range(TOP_K):
        onehot = onehot + (eio == idxs[k]).astype(F32)
    before = _dot(onehot.astype(BF16), tri_ref[...])
    for k in range(TOP_K):
        idx_ref[k:k + 1, :] = idxs[k]
        gate_ref[k:k + 1, :] = ex[k] / den
        rank_ref[k:k + 1, :] = jnp.sum(jnp.where(eio == idxs[k], before, 0.0), axis=0, keepdims=True).astype(I32)
    tl = tr // cnt_ref.shape[0]
    for j in range(cnt_ref.shape[0]):
        cnt_ref[j] = jnp.sum(onehot[:, j * tl:(j + 1) * tl], axis=1, keepdims=True).astype(I32)


def _router(x1, mods_l, seq_len, w_r, b_r):
    t, d = x1.shape
    n_exp = w_r.shape[1]
    tl = min(MOE_TOKENS, t)
    tr = min(ROUTER_TOKENS, seq_len)
    per_seq = seq_len // tr
    wt = w_r.T
    wt_hi = wt.astype(BF16)
    wt_lo = (wt - wt_hi.astype(F32)).astype(BF16)
    pos = jnp.arange(tr, dtype=I32)
    tri = ((pos[:, None] < pos[None, :]) & (pos[:, None] // tl == pos[None, :] // tl)).astype(BF16)
    tok = pl.BlockSpec((TOP_K, tr), lambda i: (0, i))
    return pl.pallas_call(
        _router_body,
        grid=(t // tr,),
        in_specs=[
            pl.BlockSpec((tr, d), lambda i: (i, 0)),
            pl.BlockSpec((None, 6, d), lambda i: (i // per_seq, 0, 0)),
            pl.BlockSpec((n_exp, d), lambda i: (0, 0)),
            pl.BlockSpec((n_exp, d), lambda i: (0, 0)),
            pl.BlockSpec((n_exp, 1), lambda i: (0, 0)),
            pl.BlockSpec((tr, tr), lambda i: (0, 0)),
        ],
        out_specs=[tok, tok, tok, pl.BlockSpec((tr // tl, n_exp, 1), lambda i: (i, 0, 0))],
        out_shape=[jax.ShapeDtypeStruct((TOP_K, t), I32), jax.ShapeDtypeStruct((TOP_K, t), F32),
                   jax.ShapeDtypeStruct((TOP_K, t), I32), jax.ShapeDtypeStruct((t // tl, n_exp, 1), I32)],
        compiler_params=_params("parallel"),
        name="router",
    )(x1, mods_l, wt_hi, wt_lo, b_r.reshape(n_exp, 1), tri)


def _rows(ref, start, cnt):
    aligned = lambda v: v if isinstance(v, int) else pl.multiple_of(v, SUBLANES)
    return ref.at[pl.ds(aligned(start), aligned(cnt))]


def _segment_copy(src, src_row, dst, dst_row, cnt, sem):
    @pl.when(cnt > 0)
    def _():
        pltpu.make_async_copy(_rows(src, src_row, cnt), _rows(dst, dst_row, cnt), sem).start()


def _wait_rows(ref, cnt, sem):
    @pl.when(cnt > 0)
    def _():
        pltpu.make_async_copy(_rows(ref, 0, cnt), _rows(ref, 0, cnt), sem).wait()


def _zero_fill(zero_ref, xs_hbm, start, cnt, sem):
    zr = zero_ref.shape[0]
    n_full = cnt // zr

    def full(r, c):
        _segment_copy(zero_ref, 0, xs_hbm, start + r * zr, zr, sem)
        return c

    lax.fori_loop(0, n_full, full, 0)
    _segment_copy(zero_ref, 0, xs_hbm, start + n_full * zr, cnt - n_full * zr, sem)


def _pack_halves(x):
    hd = x.shape[1] // 2
    lo = lax.bitcast_convert_type(x[:, :hd], U32)
    hi = lax.bitcast_convert_type(x[:, hd:], U32)
    return (lo >> 16) | (hi & jnp.uint32(0xFFFF0000))


def _unpack_halves(words):
    lo = lax.bitcast_convert_type(words << 16, F32).astype(BF16)
    hi = lax.bitcast_convert_type(words & jnp.uint32(0xFFFF0000), F32).astype(BF16)
    return lo, hi


def _one_hot_hits(iota, pos_of):
    hit = iota == pos_of(0)
    for k in range(1, TOP_K):
        hit = hit | (iota == pos_of(k))
    return hit


def _dispatch_body(goff_ref, loff_ref, cp_ref, ltot_ref, pad_start_ref, pad_cnt_ref, x1_ref, mod_ref, lp_ref,
                   gate_ref, xs_hbm, xl_ref, zero_ref, sem, zsem, *, n_exp):
    j = pl.program_id(0)
    tl, d = x1_ref.shape
    lr = xl_ref.shape[1]

    @pl.when(j == 0)
    def _():
        zero_ref[...] = jnp.zeros_like(zero_ref)

        def per_range(e, carry):
            _zero_fill(zero_ref, xs_hbm, pad_start_ref[e], pad_cnt_ref[e], zsem)
            return carry
        lax.fori_loop(0, pad_start_ref.shape[0], per_range, 0)

    def start_segments(tile):
        def per_expert(e, carry):
            seg = tile * n_exp + e
            _segment_copy(xl_ref.at[tile % 2], loff_ref[seg], xs_hbm, goff_ref[seg], cp_ref[seg], sem.at[tile % 2])
            return carry
        lax.fori_loop(0, n_exp, per_expert, 0)

    def wait_segments(tile):
        _wait_rows(xs_hbm, ltot_ref[tile], sem.at[tile % 2])

    @pl.when(j >= 2)
    def _():
        wait_segments(j - 2)

    buf = xl_ref.at[j % 2]
    riota = lax.broadcasted_iota(I32, (lr, tl), 0)
    perm = _one_hot_hits(riota, lambda k: lp_ref[k:k + 1, :]).astype(F32).astype(BF16)
    rows = _dot(perm, _moe_input(x1_ref, mod_ref).astype(BF16))
    gsel = jnp.zeros((lr, tl), F32)
    for k in range(TOP_K):
        gsel = gsel + jnp.where(riota == lp_ref[k:k + 1, :], gate_ref[k:k + 1, :], 0.0)
    rowg = jnp.broadcast_to(jnp.sum(gsel, axis=1, keepdims=True), (lr, LANES))
    buf[:, d // 2:] = lax.bitcast_convert_type(rowg, U32)
    buf[:, 0:d // 2] = _pack_halves(rows)
    start_segments(j)

    @pl.when(j == pl.num_programs(0) - 1)
    def _():
        @pl.when(j >= 1)
        def _():
            wait_segments(j - 1)
        wait_segments(j)
        n_zero = lax.fori_loop(0, pad_cnt_ref.shape[0], lambda e, acc: acc + pad_cnt_ref[e], jnp.int32(0))
        _wait_rows(xs_hbm, n_zero, zsem)


def _dispatch(x1, mods_l, seq_len, lp, gates, goff, loff, cp, ltot, pad_start, pad_cnt, n_rows):
    t, d = x1.shape
    tl = min(MOE_TOKENS, t)
    per_seq = seq_len // tl
    n_exp = goff.shape[0] // (t // tl)
    lr = TOP_K * tl + n_exp * SUBLANES
    tok = pl.BlockSpec((TOP_K, tl), lambda j, *_: (0, j))
    grid_spec = pltpu.PrefetchScalarGridSpec(
        num_scalar_prefetch=6,
        grid=(t // tl,),
        in_specs=[pl.BlockSpec((tl, d), lambda j, *_: (j, 0)),
                  pl.BlockSpec((None, 6, d), lambda j, *_: (j // per_seq, 0, 0)), tok, tok],
        out_specs=pl.BlockSpec(memory_space=pl.ANY),
        scratch_shapes=[pltpu.VMEM((2, lr, d // 2 + LANES), U32), pltpu.VMEM((MOE_TILE, d // 2 + LANES), U32),
                        pltpu.SemaphoreType.DMA((2,)), pltpu.SemaphoreType.DMA],
    )
    return pl.pallas_call(
        functools.partial(_dispatch_body, n_exp=n_exp),
        grid_spec=grid_spec,
        out_shape=jax.ShapeDtypeStruct((n_rows, d // 2 + LANES), U32),
        compiler_params=_params("arbitrary"),
        name="dispatch",
    )(goff, loff, cp, ltot, pad_start, pad_cnt, x1, mods_l, lp, gates)


def _experts_body(be_ref, bsrc_ref, rows_ref, first_ref, slot_ref, next_ref, xs_ref, wgu_hbm, bgu_ref, wdn_hbm,
                  bdn_ref, ys_ref, wgu_f32, wdn_f32, wgu_bf, wdn_bf, sem_gu, sem_dn, *, layer):
    i = pl.program_id(0)
    e = be_ref[i]
    d, f2 = wgu_bf.shape
    f = f2 // 2

    def fetch(expert, slot, start):
        for hbm, buf, sem in ((wgu_hbm, wgu_f32, sem_gu), (wdn_hbm, wdn_f32, sem_dn)):
            dma = pltpu.make_async_copy(hbm.at[layer, expert], buf.at[slot], sem.at[slot])
            dma.start() if start else dma.wait()

    @pl.when(first_ref[i] == 1)
    def _():
        slot = slot_ref[i]

        @pl.when(i == 0)
        def _():
            fetch(e, slot, True)

        fetch(e, slot, False)

        @pl.when(next_ref[i] >= 0)
        def _():
            fetch(next_ref[i], 1 - slot, True)

        def cast(ref_in, ref_out):
            def step(r, c):
                rows = pl.ds(pl.multiple_of(r * CAST_ROWS, CAST_ROWS), CAST_ROWS)
                ref_out[rows, :] = ref_in[rows, :].astype(BF16)
                return c
            lax.fori_loop(0, ref_in.shape[0] // CAST_ROWS, step, 0)
        cast(wgu_f32.at[slot], wgu_bf)
        cast(wdn_f32.at[slot], wdn_bf)

    hd = d // 2
    tm = xs_ref.shape[0]

    def mlp(n):
        x_lo, x_hi = _unpack_halves(xs_ref[0:n, 0:hd])
        proj = lambda cols: _dot(x_lo, wgu_bf[0:hd, cols]) + _dot(x_hi, wgu_bf[hd:, cols]) + bgu_ref[:, cols]
        gate = jnp.minimum(proj(slice(0, f)), SWIGLU_LIMIT)
        lin = jnp.clip(proj(slice(f, f2)), -SWIGLU_LIMIT, SWIGLU_LIMIT)
        act = (gate * jax.nn.sigmoid(SWIGLU_ALPHA * gate) * (lin + 1.0)).astype(BF16)
        row_gate = lax.bitcast_convert_type(xs_ref[0:n, hd:hd + 1], F32)
        y = (_dot(act, wdn_bf[...]) + bdn_ref[...]) * row_gate
        ys_ref[0:n, :] = _pack_halves(y.astype(BF16).astype(F32))
        if n < tm:
            ys_ref[n:, :] = jnp.zeros((tm - n, hd), U32)

    rows = rows_ref[i]
    sizes = [tm // s for s in EXPERT_ROW_SPLITS]
    for n, smaller in zip(sizes, sizes[1:] + [0]):
        @pl.when((rows > smaller) & (rows <= n))
        def _(n=n):
            mlp(n)

    @pl.when(rows == 0)
    def _():
        ys_ref[...] = jnp.zeros_like(ys_ref)


def _experts(xs, block_e, block_src, block_rows, block_first, block_slot, block_next, w_gu, b_gu, w_dn, b_dn, layer):
    n_rows, dx = xs.shape
    depth, n_exp, d, f2 = w_gu.shape
    f = f2 // 2
    tm = MOE_TILE
    grid_spec = pltpu.PrefetchScalarGridSpec(
        num_scalar_prefetch=6,
        grid=(n_rows // tm,),
        in_specs=[
            pl.BlockSpec((tm, dx), lambda i, be, bs, *_: (bs[i], 0)),
            pl.BlockSpec(memory_space=pl.ANY),
            pl.BlockSpec((None, None, 1, f2), lambda i, be, *_: (layer, be[i], 0, 0)),
            pl.BlockSpec(memory_space=pl.ANY),
            pl.BlockSpec((None, None, 1, d), lambda i, be, *_: (layer, be[i], 0, 0)),
        ],
        out_specs=pl.BlockSpec((tm, d // 2), lambda i, *_: (i, 0)),
        scratch_shapes=[pltpu.VMEM((2, d, f2), F32), pltpu.VMEM((2, f, d), F32),
                        pltpu.VMEM((d, f2), BF16), pltpu.VMEM((f, d), BF16),
                        pltpu.SemaphoreType.DMA((2,)), pltpu.SemaphoreType.DMA((2,))],
    )
    return pl.pallas_call(
        functools.partial(_experts_body, layer=layer),
        grid_spec=grid_spec,
        out_shape=jax.ShapeDtypeStruct((n_rows, d // 2), U32),
        compiler_params=_params("arbitrary"),
        name="experts",
    )(block_e, block_src, block_rows, block_first, block_slot, block_next, xs, w_gu,
      b_gu.reshape(depth, n_exp, 1, f2), w_dn, b_dn.reshape(depth, n_exp, 1, d))


def _combine_body(goff_ref, loff_ref, cp_ref, ltot_ref, ys_hbm, x_ref, lp_ref, mod_ref, lng_ref, lnb_ref, o_ref,
                  yl_ref, sem, *, n_exp, alpha):
    j = pl.program_id(0)
    tl = x_ref.shape[0]
    lr, hd = yl_ref.shape[1:]

    def fetch(tile):
        buf = yl_ref.at[tile % 2]

        def per_expert(e, carry):
            seg = tile * n_exp + e
            _segment_copy(ys_hbm, goff_ref[seg], buf, loff_ref[seg], cp_ref[seg], sem.at[tile % 2])
            return carry
        lax.fori_loop(0, n_exp, per_expert, 0)

        def zero_rows(r, carry):
            buf[pl.ds(pl.multiple_of(r * SUBLANES, SUBLANES), SUBLANES), :] = jnp.zeros((SUBLANES, hd), U32)
            return carry
        lax.fori_loop(ltot_ref[tile] // SUBLANES, lr // SUBLANES, zero_rows, 0)

    @pl.when(j == 0)
    def _():
        fetch(j)

    @pl.when(j + 1 < pl.num_programs(0))
    def _():
        fetch(j + 1)

    liota = lax.broadcasted_iota(I32, (tl, lr), 1)
    pick = _one_hot_hits(liota, lambda k: lp_ref[:, k:k + 1]).astype(F32).astype(BF16)
    _wait_rows(ys_hbm, ltot_ref[j], sem.at[j % 2])
    y_lo, y_hi = _unpack_halves(yl_ref[j % 2])
    y = jnp.concatenate([_dot(pick, y_lo), _dot(pick, y_hi)], axis=1)
    o_ref[...] = _layer_norm(alpha * x_ref[...] + mod_ref[5:6, :] * y, lng_ref[...], lnb_ref[...])


def _combine(ys, lp_t, goff, loff, cp, ltot, x1, mods_l, ln_g, ln_b, alpha, seq_len):
    t, d = x1.shape
    tl = min(MOE_TOKENS, t)
    n_exp = goff.shape[0] // (t // tl)
    lr = TOP_K * tl + n_exp * SUBLANES
    per_seq = seq_len // tl
    vec = pl.BlockSpec((1, d), lambda j, *_: (0, 0))
    grid_spec = pltpu.PrefetchScalarGridSpec(
        num_scalar_prefetch=4,
        grid=(t // tl,),
        in_specs=[
            pl.BlockSpec(memory_space=pl.ANY),
            pl.BlockSpec((tl, d), lambda j, *_: (j, 0)),
            pl.BlockSpec((tl, TOP_K), lambda j, *_: (j, 0)),
            pl.BlockSpec((None, 6, d), lambda j, *_: (j // per_seq, 0, 0)),
            vec, vec,
        ],
        out_specs=pl.BlockSpec((tl, d), lambda j, *_: (j, 0)),
        scratch_shapes=[pltpu.VMEM((2, lr, d // 2), U32), pltpu.SemaphoreType.DMA((2,))],
    )
    return pl.pallas_call(
        functools.partial(_combine_body, n_exp=n_exp, alpha=alpha),
        grid_spec=grid_spec,
        out_shape=jax.ShapeDtypeStruct((t, d), F32),
        compiler_params=_params("arbitrary"),
        name="combine",
    )(goff, loff, cp, ltot, ys, x1, lp_t, mods_l, ln_g.reshape(1, d), ln_b.reshape(1, d))


def _moe_layer(x1, mods_l, w_r, b_r, w_gu, b_gu, w_dn, b_dn, layer, ln_g, ln_b, alpha, seq_len):
    t, d = x1.shape
    n_exp = w_r.shape[1]
    tm = MOE_TILE
    tl = min(MOE_TOKENS, t)
    n_tiles = t // tl
    idx, gates, rank, cnt = _router(x1, mods_l, seq_len, w_r, b_r)
    cp = (cnt[:, :, 0] + SUBLANES - 1) // SUBLANES * SUBLANES
    tot = jnp.sum(cp, axis=0)
    padded = (tot + tm - 1) // tm * tm
    pend = jnp.cumsum(padded)
    pstart = pend - padded
    goff = pstart[None, :] + jnp.cumsum(cp, axis=0) - cp
    loff = jnp.cumsum(cp, axis=1) - cp
    ltot = jnp.sum(cp, axis=1).astype(I32)
    is_e = idx[..., None] == jnp.arange(n_exp, dtype=I32)
    lp = jnp.sum(jnp.where(is_e, jnp.repeat(loff, tl, axis=0)[None], 0), axis=-1) + rank
    n_blocks = -(-(t * TOP_K + n_tiles * n_exp * (SUBLANES - 1)) // tm) + n_exp
    n_valid = pend[-1] // tm
    blk = jnp.arange(n_blocks, dtype=I32)
    block_src = jnp.minimum(blk, n_valid - 1)
    block_e = jnp.sum(pend[None, :] <= (block_src * tm)[:, None], axis=1).astype(I32)
    used = padded > 0
    experts = jnp.arange(n_exp, dtype=I32)
    next_used = jnp.flip(lax.cummin(jnp.flip(jnp.where(used, experts, n_exp))))
    next_used = jnp.concatenate([next_used[1:], jnp.full((1,), n_exp, I32)])
    next_used = jnp.where(next_used < n_exp, next_used, -1)
    slot_of = (jnp.cumsum(used.astype(I32)) - 1) % 2
    of_block = lambda table: jnp.sum(jnp.where(block_e[:, None] == experts[None, :], table[None, :], 0), axis=1)
    block_first = ((blk * tm == of_block(pstart)) & (blk < n_valid)).astype(I32)
    block_slot = of_block(slot_of).astype(I32)
    block_next = of_block(next_used).astype(I32)
    block_rows = jnp.where(blk < n_valid, jnp.clip(of_block(pstart + tot) - blk * tm, 0, tm), 0).astype(I32)
    n_rows = n_blocks * tm
    pad_start = jnp.concatenate([pstart + tot, pend[-1:]]).astype(I32)
    pad_cnt = jnp.concatenate([padded - tot, n_rows - pend[-1:]]).astype(I32)
    flat = lambda a: a.reshape(-1).astype(I32)
    seg = (flat(goff), flat(loff), flat(cp), ltot)
    xs = _dispatch(x1, mods_l, seq_len, lp, gates, *seg, pad_start, pad_cnt, n_rows)
    ys = _experts(xs, block_e, block_src.astype(I32), block_rows, block_first, block_slot, block_next,
                  w_gu, b_gu, w_dn, b_dn, layer)
    return _combine(ys, lp.T, *seg, x1, mods_l, ln_g, ln_b, alpha, seq_len)


def kernel(x, c, ada_w, ada_b, post_ln_g, post_ln_b, conv_w_pw1, conv_b_pw1, conv_w_dw, conv_b_dw, conv_ln_g, conv_ln_b, conv_w_pw2, conv_b_pw2, w_kv, attn_w_q, attn_lambda, attn_subln_g, attn_w_o, rel_bias_table, router_w, router_b, expert_w_gate_up, expert_b_gate_up, expert_w_down, expert_b_down):
    bsz, s, d = x.shape
    depth = ada_w.shape[0]
    n_a = depth // 2
    alpha = (2 * depth) ** 0.25
    mods = _ada(c, ada_w, ada_b).reshape(depth, bsz, 6, d)
    q = k = vt = None
    for l in range(depth):
        mods_l = mods[l]
        if l < n_a:
            x1 = _conv(x, mods_l, conv_w_pw1[l], conv_b_pw1[l], conv_w_dw[l], conv_b_dw[l], conv_ln_g[l],
                       conv_ln_b[l], conv_w_pw2[l], conv_b_pw2[l], post_ln_g[l, 0], post_ln_b[l, 0], alpha)
        else:
            j = l - n_a
            if j == 0:
                q, k, vt = _qkv(x, mods_l, w_kv, attn_w_q[j])
            else:
                q = _qkv(x, mods_l, w_kv, attn_w_q[j])[0]
            lambda_init = 0.8 - 0.6 * math.exp(-0.3 * l)
            ot = _attn(q, k, vt, attn_lambda[j], attn_subln_g[j], rel_bias_table, lambda_init)
            x1 = _attn_out(ot, x, mods_l, attn_w_o[j], post_ln_g[l, 0], post_ln_b[l, 0], alpha)
        x = _moe_layer(x1.reshape(bsz * s, d), mods_l, router_w[l], router_b[l],
                       expert_w_gate_up, expert_b_gate_up, expert_w_down, expert_b_down, l,
                       post_ln_g[l, 1], post_ln_b[l, 1], alpha, s).reshape(bsz, s, d)
    return x
```

```python
import functools
import math

import jax
import jax.numpy as jnp
from jax import lax
from jax.experimental import pallas as pl
from jax.experimental.pallas import tpu as pltpu

F32 = jnp.float32
BF16 = jnp.bfloat16
I32 = jnp.int32
U32 = jnp.uint32
HIGHEST = lax.Precision.HIGHEST

CHUNK = 64
CONV_WIDTH = 31
HEAD_DIM = 64
REL_BUCKETS = 32
REL_MAX_DIST = 128
TOP_K = 4
SWIGLU_LIMIT = 7.0
SWIGLU_ALPHA = 1.702
LN_EPS = 1e-5
MASK_VALUE = -1e30
LOG2_E = math.log2(math.e)

SUBLANES = 8
LANES = 128
VMEM_LIMIT_BYTES = 56 * 1024 * 1024

ADA_TN = 1024
SEQ_TILE = 512
PROJ_TILE = 512
CONV_HALO = 32
CONV_ROWS = 256
CONV_COLS = 128
MOE_TOKENS = 256
ROUTER_TOKENS = 1024
MOE_TILE = 512
Q_BLOCK = 128
EXPERT_ROW_SPLITS = (1, 2, 4)
GLU_LOOKAHEAD = 1
SCORE_LOOKAHEAD = 3
CAST_ROWS = 128


def _params(*sem):
    return pltpu.CompilerParams(dimension_semantics=sem, vmem_limit_bytes=VMEM_LIMIT_BYTES)


def _layer_norm(x, g, b):
    mu = jnp.mean(x, axis=-1, keepdims=True)
    xc = x - mu
    var = jnp.mean(xc * xc, axis=-1, keepdims=True)
    return xc * lax.rsqrt(var + LN_EPS) * g + b


def _dot(a, b):
    return jnp.dot(a, b, preferred_element_type=F32)


def _ada_body(c_ref, w_ref, b_ref, o_ref):
    c = c_ref[...]
    cond = c * jax.nn.sigmoid(c)
    o_ref[...] = jnp.dot(cond, w_ref[...], preferred_element_type=F32, precision=HIGHEST) + b_ref[...]


def _ada(c, ada_w, ada_b):
    depth, d, n = ada_w.shape
    bsz = c.shape[0]
    tn = min(ADA_TN, n)
    return pl.pallas_call(
        _ada_body,
        grid=(depth, n // tn),
        in_specs=[
            pl.BlockSpec((bsz, d), lambda l, j: (0, 0)),
            pl.BlockSpec((None, d, tn), lambda l, j: (l, 0, j)),
            pl.BlockSpec((None, 1, tn), lambda l, j: (l, 0, j)),
        ],
        out_specs=pl.BlockSpec((None, bsz, tn), lambda l, j: (l, 0, j)),
        out_shape=jax.ShapeDtypeStruct((depth, bsz, n), F32),
        compiler_params=_params("parallel", "parallel"),
        name="ada",
    )(c, ada_w, ada_b.reshape(depth, 1, n))


def _residual_epilogue(x, y, mod_ref, lng_ref, lnb_ref, alpha, x1_ref):
    x1_ref[...] = _layer_norm(alpha * x + mod_ref[2:3, :] * y, lng_ref[...], lnb_ref[...])


def _moe_input(x1_ref, mod_ref):
    return x1_ref[...] * (1.0 + mod_ref[4:5, :]) + mod_ref[3:4, :]


def _conv_body(x_ref, halo_ref, mod_ref, w1_ref, b1_ref, wdw_ref, bdw_ref, cg_ref, cb_ref, w2_ref, b2_ref,
               lng_ref, lnb_ref, x1_ref, win_ref, v_ref, *, alpha):
    ts, d = x_ref.shape
    i = pl.program_id(1)
    xw = jnp.concatenate([halo_ref[...], x_ref[...]], axis=0)
    h = (xw * (1.0 + mod_ref[1:2, :]) + mod_ref[0:1, :]).astype(BF16)
    rows = min(CONV_ROWS, ts)
    cols = min(CONV_COLS, d)

    def glu(c0):
        a = _dot(h, w1_ref[:, c0:c0 + cols]) + b1_ref[:, c0:c0 + cols]
        g = _dot(h, w1_ref[:, d + c0:d + c0 + cols]) + b1_ref[:, d + c0:d + c0 + cols]
        return a * jax.nn.sigmoid(g)

    off = CONV_HALO - (CONV_WIDTH - 1)
    n_shift = ts + CONV_HALO - SUBLANES
    in_seq = (lax.broadcasted_iota(I32, (ts + CONV_HALO, cols), 0) >= CONV_HALO) | (i > 0)
    starts = list(range(0, d, cols))
    ahead = [glu(c) for c in starts[:GLU_LOOKAHEAD]]
    for n, c0 in enumerate(starts):
        chan = slice(c0, c0 + cols)
        win_ref[0, :, chan] = jnp.where(in_seq, ahead.pop(0), 0.0)
        if n + GLU_LOOKAHEAD < len(starts):
            ahead.append(glu(starts[n + GLU_LOOKAHEAD]))
        for b in range(1, SUBLANES):
            win_ref[b, 0:n_shift, chan] = win_ref[0, b:b + n_shift, chan]
        for r0 in range(0, ts, rows):
            acc = jnp.zeros((rows, cols), F32)
            for j in range(CONV_WIDTH):
                a, b = divmod(off + j, SUBLANES)
                r = r0 + a * SUBLANES
                acc = acc + wdw_ref[j:j + 1, c0:c0 + cols] * win_ref[b, r:r + rows, c0:c0 + cols]
            v_ref[r0:r0 + rows, c0:c0 + cols] = acc
    v = _layer_norm(v_ref[...] + bdw_ref[...], cg_ref[...], cb_ref[...])
    v = (v * jax.nn.sigmoid(v)).astype(BF16)
    y = _dot(v, w2_ref[...]) + b2_ref[...]
    _residual_epilogue(x_ref[...], y, mod_ref, lng_ref, lnb_ref, alpha, x1_ref)


def _conv(x, mods_l, w_pw1, b_pw1, w_dw, b_dw, cln_g, cln_b, w_pw2, b_pw2, ln_g, ln_b, alpha):
    bsz, s, d = x.shape
    ts = min(SEQ_TILE, s)
    hb = ts // CONV_HALO
    row = lambda a: a.reshape(1, d)
    tile = pl.BlockSpec((None, ts, d), lambda b, i: (b, i, 0))
    vec = pl.BlockSpec((1, d), lambda b, i: (0, 0))
    return pl.pallas_call(
        functools.partial(_conv_body, alpha=alpha),
        grid=(bsz, s // ts),
        in_specs=[
            tile,
            pl.BlockSpec((None, CONV_HALO, d), lambda b, i: (b, jnp.maximum(i * hb - 1, 0), 0)),
            pl.BlockSpec((None, 6, d), lambda b, i: (b, 0, 0)),
            pl.BlockSpec((d, 2 * d), lambda b, i: (0, 0)),
            pl.BlockSpec((1, 2 * d), lambda b, i: (0, 0)),
            pl.BlockSpec((CONV_WIDTH, d), lambda b, i: (0, 0)),
            vec, vec, vec,
            pl.BlockSpec((d, d), lambda b, i: (0, 0)),
            vec, vec, vec,
        ],
        out_specs=tile,
        out_shape=jax.ShapeDtypeStruct((bsz, s, d), F32),
        scratch_shapes=[pltpu.VMEM((SUBLANES, ts + CONV_HALO, d), F32), pltpu.VMEM((ts, d), F32)],
        compiler_params=_params("parallel", "parallel"),
        name="conv",
    )(x, x, mods_l, w_pw1.astype(BF16), b_pw1.reshape(1, 2 * d), w_dw, row(b_dw), row(cln_g), row(cln_b),
      w_pw2.astype(BF16), row(b_pw2), row(ln_g), row(ln_b))


def _qkv_body(x_ref, mod_ref, wkv_ref, wq_ref, q_ref, k_ref, vt_ref):
    d = x_ref.shape[-1]
    x = x_ref[...]
    kv = _dot(x.astype(BF16), wkv_ref[...])
    k_ref[...] = kv[:, :d].astype(BF16)
    hd2 = 2 * HEAD_DIM
    for c0 in range(0, d, hd2):
        vt_ref[c0:c0 + hd2, :] = kv[:, d + c0:d + c0 + hd2].T.astype(BF16)
    h = (x * (1.0 + mod_ref[1:2, :]) + mod_ref[0:1, :]).astype(BF16)
    q_ref[...] = (_dot(h, wq_ref[...]) * (HEAD_DIM ** -0.5 * LOG2_E)).astype(BF16)


def _qkv(x, mods_l, w_kv, w_q):
    bsz, s, d = x.shape
    ts = min(PROJ_TILE, s)
    tile = pl.BlockSpec((None, ts, d), lambda b, i: (b, i, 0))
    return pl.pallas_call(
        _qkv_body,
        grid=(bsz, s // ts),
        in_specs=[
            tile,
            pl.BlockSpec((None, 6, d), lambda b, i: (b, 0, 0)),
            pl.BlockSpec((d, 2 * d), lambda b, i: (0, 0)),
            pl.BlockSpec((d, d), lambda b, i: (0, 0)),
        ],
        out_specs=[tile, tile, pl.BlockSpec((None, d, ts), lambda b, i: (b, 0, i))],
        out_shape=[jax.ShapeDtypeStruct((bsz, s, d), BF16)] * 2 + [jax.ShapeDtypeStruct((bsz, d, s), BF16)],
        compiler_params=_params("parallel", "parallel"),
        name="qkv",
    )(x, mods_l, w_kv.astype(BF16), w_q.astype(BF16))


def _t5_bucket(rel):
    nb = REL_BUCKETS // 2
    ret = jnp.where(rel > 0, nb, 0)
    n = jnp.abs(rel)
    max_exact = nb // 2
    large = max_exact + (jnp.log(jnp.maximum(n, 1).astype(F32) / max_exact)
                         / math.log(REL_MAX_DIST / max_exact) * (nb - max_exact)).astype(I32)
    large = jnp.minimum(large, nb - 1)
    return ret + jnp.where(n < max_exact, n, large)


def _saturation_distance():
    nb = REL_BUCKETS // 2
    max_exact = nb // 2
    n = max_exact
    while max_exact + math.log(n / max_exact) / math.log(REL_MAX_DIST / max_exact) * (nb - max_exact) < nb - 1 + 1e-3:
        n += 1
    return n


def _bucket_strip(s):
    r = jnp.arange(Q_BLOCK, dtype=I32)[:, None]
    kp = jnp.arange(s, dtype=I32)[None, :] - (s - Q_BLOCK)
    bucket = _t5_bucket(kp - r)
    visible = jnp.floor_divide(kp, CHUNK) <= (r // CHUNK)
    return jnp.where(visible, bucket, REL_BUCKETS)


def _attn_body(tab_ref, q_ref, k_ref, vt_ref, bkt_ref, lam_ref, sg_ref, ot_ref, bias_ref, *, lambda_init, n_heads):
    h = pl.program_id(0)
    b = pl.program_id(1)
    s = q_ref.shape[0]

    near = bias_ref.shape[0]

    @pl.when(b == 0)
    def _():
        bk = bkt_ref[...]
        acc = jnp.full(bk.shape, MASK_VALUE, F32)
        for r in range(REL_BUCKETS):
            acc = jnp.where(bk == r, tab_ref[r * n_heads + h] * LOG2_E, acc)
        bias_ref[:, 0:Q_BLOCK] = acc
        bias_ref[:, Q_BLOCK:] = acc

    far_bias = tab_ref[(REL_BUCKETS // 2 - 1) * n_heads + h] * LOG2_E

    lp = lam_ref[...]
    lam = (jnp.exp(jnp.sum(lp[0:1, :] * lp[1:2, :], axis=-1, keepdims=True))
           - jnp.exp(jnp.sum(lp[2:3, :] * lp[3:4, :], axis=-1, keepdims=True)) + lambda_init)
    lane = lax.broadcasted_iota(I32, (Q_BLOCK, 2 * HEAD_DIM), 1)
    nt = (((1,), (1,)), ((), ()))

    def scores(i):
        n_keys = (i + 1) * Q_BLOCK
        q = q_ref[i * Q_BLOCK:(i + 1) * Q_BLOCK, :]
        qq = jnp.concatenate([jnp.where(lane < HEAD_DIM, q, jnp.zeros_like(q)),
                              jnp.where(lane >= HEAD_DIM, q, jnp.zeros_like(q))], axis=0)
        return lax.dot_general(k_ref[0:n_keys, :], qq, nt, preferred_element_type=F32)

    def values(i, p, denom):
        pv = _dot(vt_ref[:, 0:(i + 1) * Q_BLOCK], p)
        o = pv[:, :Q_BLOCK] * (1.0 / denom[:, :Q_BLOCK]) - pv[:, Q_BLOCK:] * (lam / denom[:, Q_BLOCK:])
        o = o * lax.rsqrt(jnp.mean(o * o, axis=0, keepdims=True) + LN_EPS) * sg_ref[...]
        ot_ref[:, i * Q_BLOCK:(i + 1) * Q_BLOCK] = (o * (1.0 - lambda_init)).astype(BF16)

    n_blocks = s // Q_BLOCK
    ahead = [scores(i) for i in range(min(SCORE_LOOKAHEAD, n_blocks))]
    for i in range(n_blocks):
        raw = ahead.pop(0)
        if i + SCORE_LOOKAHEAD < n_blocks:
            ahead.append(scores(i + SCORE_LOOKAHEAD))
        n_keys = (i + 1) * Q_BLOCK
        n_near = min(near, n_keys)
        n_far = n_keys - n_near
        sc = raw[n_far:] + bias_ref[near - n_near:, :]
        top = jnp.max(sc, axis=0, keepdims=True)
        if n_far:
            top = jnp.maximum(top, jnp.max(raw[:n_far], axis=0, keepdims=True) + far_bias)
            p = jnp.concatenate([jnp.exp2(raw[:n_far] - (top - far_bias)), jnp.exp2(sc - top)], axis=0)
        else:
            p = jnp.exp2(sc - top)
        values(i, p.astype(BF16), jnp.sum(p, axis=0, keepdims=True))


def _attn(q, k, vt, lam_p, subln_g, rel_table, lambda_init):
    bsz, s, d = q.shape
    hd2 = 2 * HEAD_DIM
    n_heads = d // hd2
    assert REL_MAX_DIST >= CHUNK and _saturation_distance() <= REL_MAX_DIST
    near = min(Q_BLOCK + REL_MAX_DIST, s)
    head = pl.BlockSpec((None, s, hd2), lambda h, b, tab: (b, 0, h))
    head_t = pl.BlockSpec((None, hd2, s), lambda h, b, tab: (b, h, 0))
    grid_spec = pltpu.PrefetchScalarGridSpec(
        num_scalar_prefetch=1,
        grid=(n_heads, bsz),
        in_specs=[
            head, head, head_t,
            pl.BlockSpec((near, Q_BLOCK), lambda h, b, tab: (0, 0)),
            pl.BlockSpec((4, HEAD_DIM), lambda h, b, tab: (0, 0)),
            pl.BlockSpec((hd2, 1), lambda h, b, tab: (0, 0)),
        ],
        out_specs=head_t,
        scratch_shapes=[pltpu.VMEM((near, 2 * Q_BLOCK), F32)],
    )
    return pl.pallas_call(
        functools.partial(_attn_body, lambda_init=lambda_init, n_heads=n_heads),
        grid_spec=grid_spec,
        out_shape=jax.ShapeDtypeStruct((bsz, d, s), BF16),
        compiler_params=_params("arbitrary", "arbitrary"),
        name="attn",
    )(rel_table.reshape(-1), q, k, vt, _bucket_strip(s).T[s - near:], lam_p, subln_g.reshape(hd2, 1))


def _attn_out_body(ot_ref, x_ref, mod_ref, wo_ref, lng_ref, lnb_ref, x1_ref, *, alpha):
    y = lax.dot_general(ot_ref[...], wo_ref[...], (((0,), (0,)), ((), ())), preferred_element_type=F32)
    _residual_epilogue(x_ref[...], y, mod_ref, lng_ref, lnb_ref, alpha, x1_ref)


def _attn_out(o, x, mods_l, w_o, ln_g, ln_b, alpha):
    bsz, s, d = x.shape
    ts = min(PROJ_TILE, s)
    tile = pl.BlockSpec((None, ts, d), lambda b, i: (b, i, 0))
    vec = pl.BlockSpec((1, d), lambda b, i: (0, 0))
    return pl.pallas_call(
        functools.partial(_attn_out_body, alpha=alpha),
        grid=(bsz, s // ts),
        in_specs=[pl.BlockSpec((None, d, ts), lambda b, i: (b, 0, i)), tile,
                  pl.BlockSpec((None, 6, d), lambda b, i: (b, 0, 0)),
                  pl.BlockSpec((d, d), lambda b, i: (0, 0)), vec, vec],
        out_specs=tile,
        out_shape=jax.ShapeDtypeStruct((bsz, s, d), F32),
        compiler_params=_params("parallel", "parallel"),
        name="attn_out",
    )(o, x, mods_l, w_o.astype(BF16), ln_g.reshape(1, d), ln_b.reshape(1, d))


def _router_body(x1_ref, mod_ref, whi_ref, wlo_ref, b_ref, tri_ref, idx_ref, gate_ref, rank_ref, cnt_ref):
    h = _moe_input(x1_ref, mod_ref)
    h_hi = h.astype(BF16)
    h_lo = (h - h_hi.astype(F32)).astype(BF16)
    nt = (((1,), (1,)), ((), ()))
    logits = (lax.dot_general(whi_ref[...], h_hi, nt, preferred_element_type=F32)
              + (lax.dot_general(whi_ref[...], h_lo, nt, preferred_element_type=F32)
                 + lax.dot_general(wlo_ref[...], h_hi, nt, preferred_element_type=F32))) + b_ref[...]
    n_exp, tr = logits.shape
    eio = lax.broadcasted_iota(I32, (n_exp, tr), 0)
    work = logits
    vals, idxs = [], []
    for _ in range(TOP_K):
        m = jnp.max(work, axis=0, keepdims=True)
        am = jnp.min(jnp.where(work == m, eio, n_exp), axis=0, keepdims=True)
        vals.append(m)
        idxs.append(am)
        work = jnp.where(eio == am, -jnp.inf, work)
    ex = [jnp.exp(v - vals[0]) for v in vals]
    den = ex[0] + ex[1] + ex[2] + ex[3]
    onehot = jnp.zeros((n_exp, tr), F32)
    for k in range(TOP_K):
        onehot = onehot + (eio == idxs[k]).astype(F32)
    before = _dot(onehot.astype(BF16), tri_ref[...])
    for k in range(TOP_K):
        idx_ref[k:k + 1, :] = idxs[k]
        gate_ref[k:k + 1, :] = ex[k] / den
        rank_ref[k:k + 1, :] = jnp.sum(jnp.where(eio == idxs[k], before, 0.0), axis=0, keepdims=True).astype(I32)
    tl = tr // cnt_ref.shape[0]
    for j in range(cnt_ref.shape[0]):
        cnt_ref[j] = jnp.sum(onehot[:, j * tl:(j + 1) * tl], axis=1, keepdims=True).astype(I32)


def _router(x1, mods_l, seq_len, w_r, b_r):
    t, d = x1.shape
    n_exp = w_r.shape[1]
    tl = min(MOE_TOKENS, t)
    tr = min(ROUTER_TOKENS, seq_len)
    per_seq = seq_len // tr
    wt = w_r.T
    wt_hi = wt.astype(BF16)
    wt_lo = (wt - wt_hi.astype(F32)).astype(BF16)
    pos = jnp.arange(tr, dtype=I32)
    tri = ((pos[:, None] < pos[None, :]) & (pos[:, None] // tl == pos[None, :] // tl)).astype(BF16)
    tok = pl.BlockSpec((TOP_K, tr), lambda i: (0, i))
    return pl.pallas_call(
        _router_body,
        grid=(t // tr,),
        in_specs=[
            pl.BlockSpec((tr, d), lambda i: (i, 0)),
            pl.BlockSpec((None, 6, d), lambda i: (i // per_seq, 0, 0)),
            pl.BlockSpec((n_exp, d), lambda i: (0, 0)),
            pl.BlockSpec((n_exp, d), lambda i: (0, 0)),
            pl.BlockSpec((n_exp, 1), lambda i: (0, 0)),
            pl.BlockSpec((tr, tr), lambda i: (0, 0)),
        ],
        out_specs=[tok, tok, tok, pl.BlockSpec((tr // tl, n_exp, 1), lambda i: (i, 0, 0))],
        out_shape=[jax.ShapeDtypeStruct((TOP_K, t), I32), jax.ShapeDtypeStruct((TOP_K, t), F32),
                   jax.ShapeDtypeStruct((TOP_K, t), I32), jax.ShapeDtypeStruct((t // tl, n_exp, 1), I32)],
        compiler_params=_params("parallel"),
        name="router",
    )(x1, mods_l, wt_hi, wt_lo, b_r.reshape(n_exp, 1), tri)


def _rows(ref, start, cnt):
    aligned = lambda v: v if isinstance(v, int) else pl.multiple_of(v, SUBLANES)
    return ref.at[pl.ds(aligned(start), aligned(cnt))]


def _segment_copy(src, src_row, dst, dst_row, cnt, sem):
    @pl.when(cnt > 0)
    def _():
        pltpu.make_async_copy(_rows(src, src_row, cnt), _rows(dst, dst_row, cnt), sem).start()


def _wait_rows(ref, cnt, sem):
    @pl.when(cnt > 0)
    def _():
        pltpu.make_async_copy(_rows(ref, 0, cnt), _rows(ref, 0, cnt), sem).wait()


def _zero_fill(zero_ref, xs_hbm, start, cnt, sem):
    zr = zero_ref.shape[0]
    n_full = cnt // zr

    def full(r, c):
        _segment_copy(zero_ref, 0, xs_hbm, start + r * zr, zr, sem)
        return c

    lax.fori_loop(0, n_full, full, 0)
    _segment_copy(zero_ref, 0, xs_hbm, start + n_full * zr, cnt - n_full * zr, sem)


def _pack_halves(x):
    hd = x.shape[1] // 2
    lo = lax.bitcast_convert_type(x[:, :hd], U32)
    hi = lax.bitcast_convert_type(x[:, hd:], U32)
    return (lo >> 16) | (hi & jnp.uint32(0xFFFF0000))


def _unpack_halves(words):
    lo = lax.bitcast_convert_type(words << 16, F32).astype(BF16)
    hi = lax.bitcast_convert_type(words & jnp.uint32(0xFFFF0000), F32).astype(BF16)
    return lo, hi


def _one_hot_hits(iota, pos_of):
    hit = iota == pos_of(0)
    for k in range(1, TOP_K):
        hit = hit | (iota == pos_of(k))
    return hit


def _dispatch_body(goff_ref, loff_ref, cp_ref, ltot_ref, pad_start_ref, pad_cnt_ref, x1_ref, mod_ref, lp_ref,
                   gate_ref, xs_hbm, xl_ref, zero_ref, sem, zsem, *, n_exp):
    j = pl.program_id(0)
    tl, d = x1_ref.shape
    lr = xl_ref.shape[1]

    @pl.when(j == 0)
    def _():
        zero_ref[...] = jnp.zeros_like(zero_ref)

        def per_range(e, carry):
            _zero_fill(zero_ref, xs_hbm, pad_start_ref[e], pad_cnt_ref[e], zsem)
            return carry
        lax.fori_loop(0, pad_start_ref.shape[0], per_range, 0)

    def start_segments(tile):
        def per_expert(e, carry):
            seg = tile * n_exp + e
            _segment_copy(xl_ref.at[tile % 2], loff_ref[seg], xs_hbm, goff_ref[seg], cp_ref[seg], sem.at[tile % 2])
            return carry
        lax.fori_loop(0, n_exp, per_expert, 0)

    def wait_segments(tile):
        _wait_rows(xs_hbm, ltot_ref[tile], sem.at[tile % 2])

    @pl.when(j >= 2)
    def _():
        wait_segments(j - 2)

    buf = xl_ref.at[j % 2]
    riota = lax.broadcasted_iota(I32, (lr, tl), 0)
    perm = _one_hot_hits(riota, lambda k: lp_ref[k:k + 1, :]).astype(F32).astype(BF16)
    rows = _dot(perm, _moe_input(x1_ref, mod_ref).astype(BF16))
    gsel = jnp.zeros((lr, tl), F32)
    for k in range(TOP_K):
        gsel = gsel + jnp.where(riota == lp_ref[k:k + 1, :], gate_ref[k:k + 1, :], 0.0)
    rowg = jnp.broadcast_to(jnp.sum(gsel, axis=1, keepdims=True), (lr, LANES))
    buf[:, d // 2:] = lax.bitcast_convert_type(rowg, U32)
    buf[:, 0:d // 2] = _pack_halves(rows)
    start_segments(j)

    @pl.when(j == pl.num_programs(0) - 1)
    def _():
        @pl.when(j >= 1)
        def _():
            wait_segments(j - 1)
        wait_segments(j)
        n_zero = lax.fori_loop(0, pad_cnt_ref.shape[0], lambda e, acc: acc + pad_cnt_ref[e], jnp.int32(0))
        _wait_rows(xs_hbm, n_zero, zsem)


def _dispatch(x1, mods_l, seq_len, lp, gates, goff, loff, cp, ltot, pad_start, pad_cnt, n_rows):
    t, d = x1.shape
    tl = min(MOE_TOKENS, t)
    per_seq = seq_len // tl
    n_exp = goff.shape[0] // (t // tl)
    lr = TOP_K * tl + n_exp * SUBLANES
    tok = pl.BlockSpec((TOP_K, tl), lambda j, *_: (0, j))
    grid_spec = pltpu.PrefetchScalarGridSpec(
        num_scalar_prefetch=6,
        grid=(t // tl,),
        in_specs=[pl.BlockSpec((tl, d), lambda j, *_: (j, 0)),
                  pl.BlockSpec((None, 6, d), lambda j, *_: (j // per_seq, 0, 0)), tok, tok],
        out_specs=pl.BlockSpec(memory_space=pl.ANY),
        scratch_shapes=[pltpu.VMEM((2, lr, d // 2 + LANES), U32), pltpu.VMEM((MOE_TILE, d // 2 + LANES), U32),
                        pltpu.SemaphoreType.DMA((2,)), pltpu.SemaphoreType.DMA],
    )
    return pl.pallas_call(
        functools.partial(_dispatch_body, n_exp=n_exp),
        grid_spec=grid_spec,
        out_shape=jax.ShapeDtypeStruct((n_rows, d // 2 + LANES), U32),
        compiler_params=_params("arbitrary"),
        name="dispatch",
    )(goff, loff, cp, ltot, pad_start, pad_cnt, x1, mods_l, lp, gates)


def _experts_body(be_ref, bsrc_ref, rows_ref, first_ref, slot_ref, next_ref, xs_ref, wgu_hbm, bgu_ref, wdn_hbm,
                  bdn_ref, ys_ref, wgu_f32, wdn_f32, wgu_bf, wdn_bf, sem_gu, sem_dn, *, layer):
    i = pl.program_id(0)
    e = be_ref[i]
    d, f2 = wgu_bf.shape
    f = f2 // 2

    def fetch(expert, slot, start):
        for hbm, buf, sem in ((wgu_hbm, wgu_f32, sem_gu), (wdn_hbm, wdn_f32, sem_dn)):
            dma = pltpu.make_async_copy(hbm.at[layer, expert], buf.at[slot], sem.at[slot])
            dma.start() if start else dma.wait()

    @pl.when(first_ref[i] == 1)
    def _():
        slot = slot_ref[i]

        @pl.when(i == 0)
        def _():
            fetch(e, slot, True)

        fetch(e, slot, False)

        @pl.when(next_ref[i] >= 0)
        def _():
            fetch(next_ref[i], 1 - slot, True)

        def cast(ref_in, ref_out):
            def step(r, c):
                rows = pl.ds(pl.multiple_of(r * CAST_ROWS, CAST_ROWS), CAST_ROWS)
                ref_out[rows, :] = ref_in[rows, :].astype(BF16)
                return c
            lax.fori_loop(0, ref_in.shape[0] // CAST_ROWS, step, 0)
        cast(wgu_f32.at[slot], wgu_bf)
        cast(wdn_f32.at[slot], wdn_bf)

    hd = d // 2
    tm = xs_ref.shape[0]

    def mlp(n):
        x_lo, x_hi = _unpack_halves(xs_ref[0:n, 0:hd])
        proj = lambda cols: _dot(x_lo, wgu_bf[0:hd, cols]) + _dot(x_hi, wgu_bf[hd:, cols]) + bgu_ref[:, cols]
        gate = jnp.minimum(proj(slice(0, f)), SWIGLU_LIMIT)
        lin = jnp.clip(proj(slice(f, f2)), -SWIGLU_LIMIT, SWIGLU_LIMIT)
        act = (gate * jax.nn.sigmoid(SWIGLU_ALPHA * gate) * (lin + 1.0)).astype(BF16)
        row_gate = lax.bitcast_convert_type(xs_ref[0:n, hd:hd + 1], F32)
        y = (_dot(act, wdn_bf[...]) + bdn_ref[...]) * row_gate
        ys_ref[0:n, :] = _pack_halves(y.astype(BF16).astype(F32))
        if n < tm:
            ys_ref[n:, :] = jnp.zeros((tm - n, hd), U32)

    rows = rows_ref[i]
    sizes = [tm // s for s in EXPERT_ROW_SPLITS]
    for n, smaller in zip(sizes, sizes[1:] + [0]):
        @pl.when((rows > smaller) & (rows <= n))
        def _(n=n):
            mlp(n)

    @pl.when(rows == 0)
    def _():
        ys_ref[...] = jnp.zeros_like(ys_ref)


def _experts(xs, block_e, block_src, block_rows, block_first, block_slot, block_next, w_gu, b_gu, w_dn, b_dn, layer):
    n_rows, dx = xs.shape
    depth, n_exp, d, f2 = w_gu.shape
    f = f2 // 2
    tm = MOE_TILE
    grid_spec = pltpu.PrefetchScalarGridSpec(
        num_scalar_prefetch=6,
        grid=(n_rows // tm,),
        in_specs=[
            pl.BlockSpec((tm, dx), lambda i, be, bs, *_: (bs[i], 0)),
            pl.BlockSpec(memory_space=pl.ANY),
            pl.BlockSpec((None, None, 1, f2), lambda i, be, *_: (layer, be[i], 0, 0)),
            pl.BlockSpec(memory_space=pl.ANY),
            pl.BlockSpec((None, None, 1, d), lambda i, be, *_: (layer, be[i], 0, 0)),
        ],
        out_specs=pl.BlockSpec((tm, d // 2), lambda i, *_: (i, 0)),
        scratch_shapes=[pltpu.VMEM((2, d, f2), F32), pltpu.VMEM((2, f, d), F32),
                        pltpu.VMEM((d, f2), BF16), pltpu.VMEM((f, d), BF16),
                        pltpu.SemaphoreType.DMA((2,)), pltpu.SemaphoreType.DMA((2,))],
    )
    return pl.pallas_call(
        functools.partial(_experts_body, layer=layer),
        grid_spec=grid_spec,
        out_shape=jax.ShapeDtypeStruct((n_rows, d // 2), U32),
        compiler_params=_params("arbitrary"),
        name="experts",
    )(block_e, block_src, block_rows, block_first, block_slot, block_next, xs, w_gu,
      b_gu.reshape(depth, n_exp, 1, f2), w_dn, b_dn.reshape(depth, n_exp, 1, d))


def _combine_body(goff_ref, loff_ref, cp_ref, ltot_ref, ys_hbm, x_ref, lp_ref, mod_ref, lng_ref, lnb_ref, o_ref,
                  yl_ref, sem, *, n_exp, alpha):
    j = pl.program_id(0)
    tl = x_ref.shape[0]
    lr, hd = yl_ref.shape[1:]

    def fetch(tile):
        buf = yl_ref.at[tile % 2]

        def per_expert(e, carry):
            seg = tile * n_exp + e
            _segment_copy(ys_hbm, goff_ref[seg], buf, loff_ref[seg], cp_ref[seg], sem.at[tile % 2])
            return carry
        lax.fori_loop(0, n_exp, per_expert, 0)

        def zero_rows(r, carry):
            buf[pl.ds(pl.multiple_of(r * SUBLANES, SUBLANES), SUBLANES), :] = jnp.zeros((SUBLANES, hd), U32)
            return carry
        lax.fori_loop(ltot_ref[tile] // SUBLANES, lr // SUBLANES, zero_rows, 0)

    @pl.when(j == 0)
    def _():
        fetch(j)

    @pl.when(j + 1 < pl.num_programs(0))
    def _():
        fetch(j + 1)

    liota = lax.broadcasted_iota(I32, (tl, lr), 1)
    pick = _one_hot_hits(liota, lambda k: lp_ref[:, k:k + 1]).astype(F32).astype(BF16)
    _wait_rows(ys_hbm, ltot_ref[j], sem.at[j % 2])
    y_lo, y_hi = _unpack_halves(yl_ref[j % 2])
    y = jnp.concatenate([_dot(pick, y_lo), _dot(pick, y_hi)], axis=1)
    o_ref[...] = _layer_norm(alpha * x_ref[...] + mod_ref[5:6, :] * y, lng_ref[...], lnb_ref[...])


def _combine(ys, lp_t, goff, loff, cp, ltot, x1, mods_l, ln_g, ln_b, alpha, seq_len):
    t, d = x1.shape
    tl = min(MOE_TOKENS, t)
    n_exp = goff.shape[0] // (t // tl)
    lr = TOP_K * tl + n_exp * SUBLANES
    per_seq = seq_len // tl
    vec = pl.BlockSpec((1, d), lambda j, *_: (0, 0))
    grid_spec = pltpu.PrefetchScalarGridSpec(
        num_scalar_prefetch=4,
        grid=(t // tl,),
        in_specs=[
            pl.BlockSpec(memory_space=pl.ANY),
            pl.BlockSpec((tl, d), lambda j, *_: (j, 0)),
            pl.BlockSpec((tl, TOP_K), lambda j, *_: (j, 0)),
            pl.BlockSpec((None, 6, d), lambda j, *_: (j // per_seq, 0, 0)),
            vec, vec,
        ],
        out_specs=pl.BlockSpec((tl, d), lambda j, *_: (j, 0)),
        scratch_shapes=[pltpu.VMEM((2, lr, d // 2), U32), pltpu.SemaphoreType.DMA((2,))],
    )
    return pl.pallas_call(
        functools.partial(_combine_body, n_exp=n_exp, alpha=alpha),
        grid_spec=grid_spec,
        out_shape=jax.ShapeDtypeStruct((t, d), F32),
        compiler_params=_params("arbitrary"),
        name="combine",
    )(goff, loff, cp, ltot, ys, x1, lp_t, mods_l, ln_g.reshape(1, d), ln_b.reshape(1, d))


def _moe_layer(x1, mods_l, w_r, b_r, w_gu, b_gu, w_dn, b_dn, layer, ln_g, ln_b, alpha, seq_len):
    t, d = x1.shape
    n_exp = w_r.shape[1]
    tm = MOE_TILE
    tl = min(MOE_TOKENS, t)
    n_tiles = t // tl
    idx, gates, rank, cnt = _router(x1, mods_l, seq_len, w_r, b_r)
    cp = (cnt[:, :, 0] + SUBLANES - 1) // SUBLANES * SUBLANES
    tot = jnp.sum(cp, axis=0)
    padded = (tot + tm - 1) // tm * tm
    pend = jnp.cumsum(padded)
    pstart = pend - padded
    goff = pstart[None, :] + jnp.cumsum(cp, axis=0) - cp
    loff = jnp.cumsum(cp, axis=1) - cp
    ltot = jnp.sum(cp, axis=1).astype(I32)
    is_e = idx[..., None] == jnp.arange(n_exp, dtype=I32)
    lp = jnp.sum(jnp.where(is_e, jnp.repeat(loff, tl, axis=0)[None], 0), axis=-1) + rank
    n_blocks = -(-(t * TOP_K + n_tiles * n_exp * (SUBLANES - 1)) // tm) + n_exp
    n_valid = pend[-1] // tm
    blk = jnp.arange(n_blocks, dtype=I32)
    block_src = jnp.minimum(blk, n_valid - 1)
    block_e = jnp.sum(pend[None, :] <= (block_src * tm)[:, None], axis=1).astype(I32)
    used = padded > 0
    experts = jnp.arange(n_exp, dtype=I32)
    next_used = jnp.flip(lax.cummin(jnp.flip(jnp.where(used, experts, n_exp))))
    next_used = jnp.concatenate([next_used[1:], jnp.full((1,), n_exp, I32)])
    next_used = jnp.where(next_used < n_exp, next_used, -1)
    slot_of = (jnp.cumsum(used.astype(I32)) - 1) % 2
    of_block = lambda table: jnp.sum(jnp.where(block_e[:, None] == experts[None, :], table[None, :], 0), axis=1)
    block_first = ((blk * tm == of_block(pstart)) & (blk < n_valid)).astype(I32)
    block_slot = of_block(slot_of).astype(I32)
    block_next = of_block(next_used).astype(I32)
    block_rows = jnp.where(blk < n_valid, jnp.clip(of_block(pstart + tot) - blk * tm, 0, tm), 0).astype(I32)
    n_rows = n_blocks * tm
    pad_start = jnp.concatenate([pstart + tot, pend[-1:]]).astype(I32)
    pad_cnt = jnp.concatenate([padded - tot, n_rows - pend[-1:]]).astype(I32)
    flat = lambda a: a.reshape(-1).astype(I32)
    seg = (flat(goff), flat(loff), flat(cp), ltot)
    xs = _dispatch(x1, mods_l, seq_len, lp, gates, *seg, pad_start, pad_cnt, n_rows)
    ys = _experts(xs, block_e, block_src.astype(I32), block_rows, block_first, block_slot, block_next,
                  w_gu, b_gu, w_dn, b_dn, layer)
    return _combine(ys, lp.T, *seg, x1, mods_l, ln_g, ln_b, alpha, seq_len)


def kernel(x, c, ada_w, ada_b, post_ln_g, post_ln_b, conv_w_pw1, conv_b_pw1, conv_w_dw, conv_b_dw, conv_ln_g, conv_ln_b, conv_w_pw2, conv_b_pw2, w_kv, attn_w_q, attn_lambda, attn_subln_g, attn_w_o, rel_bias_table, router_w, router_b, expert_w_gate_up, expert_b_gate_up, expert_w_down, expert_b_down):
    bsz, s, d = x.shape
    depth = ada_w.shape[0]
    n_a = depth // 2
    alpha = (2 * depth) ** 0.25
    mods = _ada(c, ada_w, ada_b).reshape(depth, bsz, 6, d)
    q = k = vt = None
    for l in range(depth):
        mods_l = mods[l]
        if l < n_a:
            x1 = _conv(x, mods_l, conv_w_pw1[l], conv_b_pw1[l], conv_w_dw[l], conv_b_dw[l], conv_ln_g[l],
                       conv_ln_b[l], conv_w_pw2[l], conv_b_pw2[l], post_ln_g[l, 0], post_ln_b[l, 0], alpha)
        else:
            j = l - n_a
            if j == 0:
                q, k, vt = _qkv(x, mods_l, w_kv, attn_w_q[j])
            else:
                q = _qkv(x, mods_l, w_kv, attn_w_q[j])[0]
            lambda_init = 0.8 - 0.6 * math.exp(-0.3 * l)
            ot = _attn(q, k, vt, attn_lambda[j], attn_subln_g[j], rel_bias_table, lambda_init)
            x1 = _attn_out(ot, x, mods_l, attn_w_o[j], post_ln_g[l, 0], post_ln_b[l, 0], alpha)
        x = _moe_layer(x1.reshape(bsz * s, d), mods_l, router_w[l], router_b[l],
                       expert_w_gate_up, expert_b_gate_up, expert_w_down, expert_b_down, l,
                       post_ln_g[l, 1], post_ln_b[l, 1], alpha, s).reshape(bsz, s, d)
    return x
```

```python
import functools
import math

import jax
import jax.numpy as jnp
from jax import lax
from jax.experimental import pallas as pl
from jax.experimental.pallas import tpu as pltpu

F32 = jnp.float32
BF16 = jnp.bfloat16
I32 = jnp.int32
U32 = jnp.uint32
HIGHEST = lax.Precision.HIGHEST

CHUNK = 64
CONV_WIDTH = 31
HEAD_DIM = 64
REL_BUCKETS = 32
REL_MAX_DIST = 128
TOP_K = 4
SWIGLU_LIMIT = 7.0
SWIGLU_ALPHA = 1.702
LN_EPS = 1e-5
MASK_VALUE = -1e30
LOG2_E = math.log2(math.e)

SUBLANES = 8
LANES = 128
VMEM_LIMIT_BYTES = 56 * 1024 * 1024

ADA_TN = 1024
SEQ_TILE = 512
PROJ_TILE = 512
CONV_HALO = 32
CONV_ROWS = 256
CONV_COLS = 128
MOE_TOKENS = 256
ROUTER_TOKENS = 2048
MOE_TILE = 512
Q_BLOCK = 128
EXPERT_ROW_EIGHTHS = (8, 6, 4, 2)
GLU_LOOKAHEAD = 1
SCORE_LOOKAHEAD = 3
CAST_ROWS = 128


def _params(*sem):
    return pltpu.CompilerParams(dimension_semantics=sem, vmem_limit_bytes=VMEM_LIMIT_BYTES)


def _layer_norm(x, g, b):
    mu = jnp.mean(x, axis=-1, keepdims=True)
    xc = x - mu
    var = jnp.mean(xc * xc, axis=-1, keepdims=True)
    return xc * lax.rsqrt(var + LN_EPS) * g + b


def _dot(a, b):
    return jnp.dot(a, b, preferred_element_type=F32)


def _ada_body(c_ref, w_ref, b_ref, o_ref):
    c = c_ref[...]
    cond = c * jax.nn.sigmoid(c)
    o_ref[...] = jnp.dot(cond, w_ref[...], preferred_element_type=F32, precision=HIGHEST) + b_ref[...]


def _ada(c, ada_w, ada_b):
    depth, d, n = ada_w.shape
    bsz = c.shape[0]
    tn = min(ADA_TN, n)
    return pl.pallas_call(
        _ada_body,
        grid=(depth, n // tn),
        in_specs=[
            pl.BlockSpec((bsz, d), lambda l, j: (0, 0)),
            pl.BlockSpec((None, d, tn), lambda l, j: (l, 0, j)),
            pl.BlockSpec((None, 1, tn), lambda l, j: (l, 0, j)),
        ],
        out_specs=pl.BlockSpec((None, bsz, tn), lambda l, j: (l, 0, j)),
        out_shape=jax.ShapeDtypeStruct((depth, bsz, n), F32),
        compiler_params=_params("parallel", "parallel"),
        name="ada",
    )(c, ada_w, ada_b.reshape(depth, 1, n))


def _residual_epilogue(x, y, mod_ref, lng_ref, lnb_ref, alpha, x1_ref):
    x1_ref[...] = _layer_norm(alpha * x + mod_ref[2:3, :] * y, lng_ref[...], lnb_ref[...])


def _moe_input(x1_ref, mod_ref):
    return x1_ref[...] * (1.0 + mod_ref[4:5, :]) + mod_ref[3:4, :]


def _conv_body(x_ref, halo_ref, mod_ref, w1_ref, b1_ref, wdw_ref, bdw_ref, cg_ref, cb_ref, w2_ref, b2_ref,
               lng_ref, lnb_ref, x1_ref, win_ref, v_ref, *, alpha):
    ts, d = x_ref.shape
    i = pl.program_id(1)
    xw = jnp.concatenate([halo_ref[...], x_ref[...]], axis=0)
    h = (xw * (1.0 + mod_ref[1:2, :]) + mod_ref[0:1, :]).astype(BF16)
    rows = min(CONV_ROWS, ts)
    cols = min(CONV_COLS, d)

    def glu(c0):
        a = _dot(h, w1_ref[:, c0:c0 + cols]) + b1_ref[:, c0:c0 + cols]
        g = _dot(h, w1_ref[:, d + c0:d + c0 + cols]) + b1_ref[:, d + c0:d + c0 + cols]
        return a * jax.nn.sigmoid(g)

    off = CONV_HALO - (CONV_WIDTH - 1)
    n_shift = ts + CONV_HALO - SUBLANES
    in_seq = (lax.broadcasted_iota(I32, (ts + CONV_HALO, cols), 0) >= CONV_HALO) | (i > 0)
    starts = list(range(0, d, cols))
    ahead = [glu(c) for c in starts[:GLU_LOOKAHEAD]]
    for n, c0 in enumerate(starts):
        chan = slice(c0, c0 + cols)
        win_ref[0, :, chan] = jnp.where(in_seq, ahead.pop(0), 0.0)
        if n + GLU_LOOKAHEAD < len(starts):
            ahead.append(glu(starts[n + GLU_LOOKAHEAD]))
        for b in range(1, SUBLANES):
            win_ref[b, 0:n_shift, chan] = win_ref[0, b:b + n_shift, chan]
        for r0 in range(0, ts, rows):
            acc = jnp.zeros((rows, cols), F32)
            for j in range(CONV_WIDTH):
                a, b = divmod(off + j, SUBLANES)
                r = r0 + a * SUBLANES
                acc = acc + wdw_ref[j:j + 1, c0:c0 + cols] * win_ref[b, r:r + rows, c0:c0 + cols]
            v_ref[r0:r0 + rows, c0:c0 + cols] = acc
    v = _layer_norm(v_ref[...] + bdw_ref[...], cg_ref[...], cb_ref[...])
    v = (v * jax.nn.sigmoid(v)).astype(BF16)
    y = _dot(v, w2_ref[...]) + b2_ref[...]
    _residual_epilogue(x_ref[...], y, mod_ref, lng_ref, lnb_ref, alpha, x1_ref)


def _conv(x, mods_l, w_pw1, b_pw1, w_dw, b_dw, cln_g, cln_b, w_pw2, b_pw2, ln_g, ln_b, alpha):
    bsz, s, d = x.shape
    ts = min(SEQ_TILE, s)
    hb = ts // CONV_HALO
    row = lambda a: a.reshape(1, d)
    tile = pl.BlockSpec((None, ts, d), lambda b, i: (b, i, 0))
    vec = pl.BlockSpec((1, d), lambda b, i: (0, 0))
    return pl.pallas_call(
        functools.partial(_conv_body, alpha=alpha),
        grid=(bsz, s // ts),
        in_specs=[
            tile,
            pl.BlockSpec((None, CONV_HALO, d), lambda b, i: (b, jnp.maximum(i * hb - 1, 0), 0)),
            pl.BlockSpec((None, 6, d), lambda b, i: (b, 0, 0)),
            pl.BlockSpec((d, 2 * d), lambda b, i: (0, 0)),
            pl.BlockSpec((1, 2 * d), lambda b, i: (0, 0)),
            pl.BlockSpec((CONV_WIDTH, d), lambda b, i: (0, 0)),
            vec, vec, vec,
            pl.BlockSpec((d, d), lambda b, i: (0, 0)),
            vec, vec, vec,
        ],
        out_specs=tile,
        out_shape=jax.ShapeDtypeStruct((bsz, s, d), F32),
        scratch_shapes=[pltpu.VMEM((SUBLANES, ts + CONV_HALO, d), F32), pltpu.VMEM((ts, d), F32)],
        compiler_params=_params("parallel", "parallel"),
        name="conv",
    )(x, x, mods_l, w_pw1.astype(BF16), b_pw1.reshape(1, 2 * d), w_dw, row(b_dw), row(cln_g), row(cln_b),
      w_pw2.astype(BF16), row(b_pw2), row(ln_g), row(ln_b))


def _qkv_body(x_ref, mod_ref, wkv_ref, wq_ref, q_ref, k_ref, vt_ref):
    d = x_ref.shape[-1]
    x = x_ref[...]
    kv = _dot(x.astype(BF16), wkv_ref[...])
    k_ref[...] = kv[:, :d].astype(BF16)
    hd2 = 2 * HEAD_DIM
    for c0 in range(0, d, hd2):
        vt_ref[c0:c0 + hd2, :] = kv[:, d + c0:d + c0 + hd2].T.astype(BF16)
    h = (x * (1.0 + mod_ref[1:2, :]) + mod_ref[0:1, :]).astype(BF16)
    q_ref[...] = (_dot(h, wq_ref[...]) * (HEAD_DIM ** -0.5 * LOG2_E)).astype(BF16)


def _qkv(x, mods_l, w_kv, w_q):
    bsz, s, d = x.shape
    ts = min(PROJ_TILE, s)
    tile = pl.BlockSpec((None, ts, d), lambda b, i: (b, i, 0))
    return pl.pallas_call(
        _qkv_body,
        grid=(bsz, s // ts),
        in_specs=[
            tile,
            pl.BlockSpec((None, 6, d), lambda b, i: (b, 0, 0)),
            pl.BlockSpec((d, 2 * d), lambda b, i: (0, 0)),
            pl.BlockSpec((d, d), lambda b, i: (0, 0)),
        ],
        out_specs=[tile, tile, pl.BlockSpec((None, d, ts), lambda b, i: (b, 0, i))],
        out_shape=[jax.ShapeDtypeStruct((bsz, s, d), BF16)] * 2 + [jax.ShapeDtypeStruct((bsz, d, s), BF16)],
        compiler_params=_params("parallel", "parallel"),
        name="qkv",
    )(x, mods_l, w_kv.astype(BF16), w_q.astype(BF16))


def _t5_bucket(rel):
    nb = REL_BUCKETS // 2
    ret = jnp.where(rel > 0, nb, 0)
    n = jnp.abs(rel)
    max_exact = nb // 2
    large = max_exact + (jnp.log(jnp.maximum(n, 1).astype(F32) / max_exact)
                         / math.log(REL_MAX_DIST / max_exact) * (nb - max_exact)).astype(I32)
    large = jnp.minimum(large, nb - 1)
    return ret + jnp.where(n < max_exact, n, large)


def _saturation_distance():
    nb = REL_BUCKETS // 2
    max_exact = nb // 2
    n = max_exact
    while max_exact + math.log(n / max_exact) / math.log(REL_MAX_DIST / max_exact) * (nb - max_exact) < nb - 1 + 1e-3:
        n += 1
    return n


def _bucket_strip(s):
    r = jnp.arange(Q_BLOCK, dtype=I32)[:, None]
    kp = jnp.arange(s, dtype=I32)[None, :] - (s - Q_BLOCK)
    bucket = _t5_bucket(kp - r)
    visible = jnp.floor_divide(kp, CHUNK) <= (r // CHUNK)
    return jnp.where(visible, bucket, REL_BUCKETS)


def _attn_body(tab_ref, q_ref, k_ref, vt_ref, bkt_ref, lam_ref, sg_ref, ot_ref, bias_ref, *, lambda_init, n_heads):
    h = pl.program_id(0)
    b = pl.program_id(1)
    s = q_ref.shape[0]

    near = bias_ref.shape[0]

    @pl.when(b == 0)
    def _():
        bk = bkt_ref[...]
        acc = jnp.full(bk.shape, MASK_VALUE, F32)
        for r in range(REL_BUCKETS):
            acc = jnp.where(bk == r, tab_ref[r * n_heads + h] * LOG2_E, acc)
        bias_ref[:, 0:Q_BLOCK] = acc
        bias_ref[:, Q_BLOCK:] = acc

    far_bias = tab_ref[(REL_BUCKETS // 2 - 1) * n_heads + h] * LOG2_E

    lp = lam_ref[...]
    lam = (jnp.exp(jnp.sum(lp[0:1, :] * lp[1:2, :], axis=-1, keepdims=True))
           - jnp.exp(jnp.sum(lp[2:3, :] * lp[3:4, :], axis=-1, keepdims=True)) + lambda_init)
    lane = lax.broadcasted_iota(I32, (Q_BLOCK, 2 * HEAD_DIM), 1)
    nt = (((1,), (1,)), ((), ()))

    def scores(i):
        n_keys = (i + 1) * Q_BLOCK
        q = q_ref[i * Q_BLOCK:(i + 1) * Q_BLOCK, :]
        qq = jnp.concatenate([jnp.where(lane < HEAD_DIM, q, jnp.zeros_like(q)),
                              jnp.where(lane >= HEAD_DIM, q, jnp.zeros_like(q))], axis=0)
        return lax.dot_general(k_ref[0:n_keys, :], qq, nt, preferred_element_type=F32)

    def values(i, p, denom):
        pv = _dot(vt_ref[:, 0:(i + 1) * Q_BLOCK], p)
        o = pv[:, :Q_BLOCK] * (1.0 / denom[:, :Q_BLOCK]) - pv[:, Q_BLOCK:] * (lam / denom[:, Q_BLOCK:])
        o = o * lax.rsqrt(jnp.mean(o * o, axis=0, keepdims=True) + LN_EPS) * sg_ref[...]
        ot_ref[:, i * Q_BLOCK:(i + 1) * Q_BLOCK] = (o * (1.0 - lambda_init)).astype(BF16)

    n_blocks = s // Q_BLOCK
    ahead = [scores(i) for i in range(min(SCORE_LOOKAHEAD, n_blocks))]
    for i in range(n_blocks):
        raw = ahead.pop(0)
        if i + SCORE_LOOKAHEAD < n_blocks:
            ahead.append(scores(i + SCORE_LOOKAHEAD))
        n_keys = (i + 1) * Q_BLOCK
        n_near = min(near, n_keys)
        n_far = n_keys - n_near
        sc = raw[n_far:] + bias_ref[near - n_near:, :]
        top = jnp.max(sc, axis=0, keepdims=True)
        if n_far:
            top = jnp.maximum(top, jnp.max(raw[:n_far], axis=0, keepdims=True) + far_bias)
            p = jnp.concatenate([jnp.exp2(raw[:n_far] - (top - far_bias)), jnp.exp2(sc - top)], axis=0)
        else:
            p = jnp.exp2(sc - top)
        values(i, p.astype(BF16), jnp.sum(p, axis=0, keepdims=True))


def _attn(q, k, vt, lam_p, subln_g, rel_table, lambda_init):
    bsz, s, d = q.shape
    hd2 = 2 * HEAD_DIM
    n_heads = d // hd2
    assert REL_MAX_DIST >= CHUNK and _saturation_distance() <= REL_MAX_DIST
    near = min(Q_BLOCK + REL_MAX_DIST, s)
    head = pl.BlockSpec((None, s, hd2), lambda h, b, tab: (b, 0, h))
    head_t = pl.BlockSpec((None, hd2, s), lambda h, b, tab: (b, h, 0))
    grid_spec = pltpu.PrefetchScalarGridSpec(
        num_scalar_prefetch=1,
        grid=(n_heads, bsz),
        in_specs=[
            head, head, head_t,
            pl.BlockSpec((near, Q_BLOCK), lambda h, b, tab: (0, 0)),
            pl.BlockSpec((4, HEAD_DIM), lambda h, b, tab: (0, 0)),
            pl.BlockSpec((hd2, 1), lambda h, b, tab: (0, 0)),
        ],
        out_specs=head_t,
        scratch_shapes=[pltpu.VMEM((near, 2 * Q_BLOCK), F32)],
    )
    return pl.pallas_call(
        functools.partial(_attn_body, lambda_init=lambda_init, n_heads=n_heads),
        grid_spec=grid_spec,
        out_shape=jax.ShapeDtypeStruct((bsz, d, s), BF16),
        compiler_params=_params("arbitrary", "arbitrary"),
        name="attn",
    )(rel_table.reshape(-1), q, k, vt, _bucket_strip(s).T[s - near:], lam_p, subln_g.reshape(hd2, 1))


def _attn_out_body(ot_ref, x_ref, mod_ref, wo_ref, lng_ref, lnb_ref, x1_ref, *, alpha):
    y = lax.dot_general(ot_ref[...], wo_ref[...], (((0,), (0,)), ((), ())), preferred_element_type=F32)
    _residual_epilogue(x_ref[...], y, mod_ref, lng_ref, lnb_ref, alpha, x1_ref)


def _attn_out(o, x, mods_l, w_o, ln_g, ln_b, alpha):
    bsz, s, d = x.shape
    ts = min(PROJ_TILE, s)
    tile = pl.BlockSpec((None, ts, d), lambda b, i: (b, i, 0))
    vec = pl.BlockSpec((1, d), lambda b, i: (0, 0))
    return pl.pallas_call(
        functools.partial(_attn_out_body, alpha=alpha),
        grid=(bsz, s // ts),
        in_specs=[pl.BlockSpec((None, d, ts), lambda b, i: (b, 0, i)), tile,
                  pl.BlockSpec((None, 6, d), lambda b, i: (b, 0, 0)),
                  pl.BlockSpec((d, d), lambda b, i: (0, 0)), vec, vec],
        out_specs=tile,
        out_shape=jax.ShapeDtypeStruct((bsz, s, d), F32),
        compiler_params=_params("parallel", "parallel"),
        name="attn_out",
    )(o, x, mods_l, w_o.astype(BF16), ln_g.reshape(1, d), ln_b.reshape(1, d))


def _router_body(x1_ref, mod_ref, whi_ref, wlo_ref, b_ref, tri_ref, idx_ref, gate_ref, rank_ref, cnt_ref):
    h = _moe_input(x1_ref, mod_ref)
    h_hi = h.astype(BF16)
    h_lo = (h - h_hi.astype(F32)).astype(BF16)
    nt = (((1,), (1,)), ((), ()))
    logits = (lax.dot_general(whi_ref[...], h_hi, nt, preferred_element_type=F32)
              + (lax.dot_general(whi_ref[...], h_lo, nt, preferred_element_type=F32)
                 + lax.dot_general(wlo_ref[...], h_hi, nt, preferred_element_type=F32))) + b_ref[...]
    n_exp, tr = logits.shape
    eio = lax.broadcasted_iota(I32, (n_exp, tr), 0)
    work = logits
    vals, idxs = [], []
    for _ in range(TOP_K):
        m = jnp.max(work, axis=0, keepdims=True)
        am = jnp.min(jnp.where(work == m, eio, n_exp), axis=0, keepdims=True)
        vals.append(m)
        idxs.append(am)
        work = jnp.where(eio == am, -jnp.inf, work)
    ex = [jnp.exp(v - vals[0]) for v in vals]
    den = ex[0] + ex[1] + ex[2] + ex[3]
    onehot = jnp.zeros((n_exp, tr), F32)
    for k in range(TOP_K):
        onehot = onehot + (eio == idxs[k]).astype(F32)
    tl = tri_ref.shape[0]
    tiles = [onehot[:, t0:t0 + tl] for t0 in range(0, tr, tl)]
    before = jnp.concatenate([_dot(oh.astype(BF16), tri_ref[...]) for oh in tiles], axis=1)
    for k in range(TOP_K):
        idx_ref[k:k + 1, :] = idxs[k]
        gate_ref[k:k + 1, :] = ex[k] / den
        rank_ref[k:k + 1, :] = jnp.sum(jnp.where(eio == idxs[k], before, 0.0), axis=0, keepdims=True).astype(I32)
    for j, oh in enumerate(tiles):
        cnt_ref[j] = jnp.sum(oh, axis=1, keepdims=True).astype(I32)


def _router(x1, mods_l, seq_len, w_r, b_r):
    t, d = x1.shape
    n_exp = w_r.shape[1]
    tl = min(MOE_TOKENS, t)
    tr = min(ROUTER_TOKENS, seq_len)
    per_seq = seq_len // tr
    wt = w_r.T
    wt_hi = wt.astype(BF16)
    wt_lo = (wt - wt_hi.astype(F32)).astype(BF16)
    pos = jnp.arange(tl, dtype=I32)
    tri = (pos[:, None] < pos[None, :]).astype(BF16)
    tok = pl.BlockSpec((TOP_K, tr), lambda i: (0, i))
    return pl.pallas_call(
        _router_body,
        grid=(t // tr,),
        in_specs=[
            pl.BlockSpec((tr, d), lambda i: (i, 0)),
            pl.BlockSpec((None, 6, d), lambda i: (i // per_seq, 0, 0)),
            pl.BlockSpec((n_exp, d), lambda i: (0, 0)),
            pl.BlockSpec((n_exp, d), lambda i: (0, 0)),
            pl.BlockSpec((n_exp, 1), lambda i: (0, 0)),
            pl.BlockSpec((tl, tl), lambda i: (0, 0)),
        ],
        out_specs=[tok, tok, tok, pl.BlockSpec((tr // tl, n_exp, 1), lambda i: (i, 0, 0))],
        out_shape=[jax.ShapeDtypeStruct((TOP_K, t), I32), jax.ShapeDtypeStruct((TOP_K, t), F32),
                   jax.ShapeDtypeStruct((TOP_K, t), I32), jax.ShapeDtypeStruct((t // tl, n_exp, 1), I32)],
        compiler_params=_params("parallel"),
        name="router",
    )(x1, mods_l, wt_hi, wt_lo, b_r.reshape(n_exp, 1), tri)


def _rows(ref, start, cnt):
    aligned = lambda v: v if isinstance(v, int) else pl.multiple_of(v, SUBLANES)
    return ref.at[pl.ds(aligned(start), aligned(cnt))]


def _segment_copy(src, src_row, dst, dst_row, cnt, sem):
    @pl.when(cnt > 0)
    def _():
        pltpu.make_async_copy(_rows(src, src_row, cnt), _rows(dst, dst_row, cnt), sem).start()


def _wait_rows(ref, cnt, sem):
    @pl.when(cnt > 0)
    def _():
        pltpu.make_async_copy(_rows(ref, 0, cnt), _rows(ref, 0, cnt), sem).wait()


def _zero_fill(zero_ref, xs_hbm, start, cnt, sem):
    zr = zero_ref.shape[0]
    n_full = cnt // zr

    def full(r, c):
        _segment_copy(zero_ref, 0, xs_hbm, start + r * zr, zr, sem)
        return c

    lax.fori_loop(0, n_full, full, 0)
    _segment_copy(zero_ref, 0, xs_hbm, start + n_full * zr, cnt - n_full * zr, sem)


def _pack_halves(x):
    hd = x.shape[1] // 2
    lo = lax.bitcast_convert_type(x[:, :hd], U32)
    hi = lax.bitcast_convert_type(x[:, hd:], U32)
    return (lo >> 16) | (hi & jnp.uint32(0xFFFF0000))


def _unpack_halves(words):
    lo = lax.bitcast_convert_type(words << 16, F32).astype(BF16)
    hi = lax.bitcast_convert_type(words & jnp.uint32(0xFFFF0000), F32).astype(BF16)
    return lo, hi


def _one_hot_hits(iota, pos_of):
    hit = iota == pos_of(0)
    for k in range(1, TOP_K):
        hit = hit | (iota == pos_of(k))
    return hit


def _dispatch_body(goff_ref, loff_ref, cp_ref, ltot_ref, pad_start_ref, pad_cnt_ref, x1_ref, mod_ref, lp_ref,
                   gate_ref, xs_hbm, xl_ref, zero_ref, sem, zsem, *, n_exp):
    j = pl.program_id(0)
    tl, d = x1_ref.shape
    lr = xl_ref.shape[1]

    @pl.when(j == 0)
    def _():
        zero_ref[...] = jnp.zeros_like(zero_ref)

        def per_range(e, carry):
            _zero_fill(zero_ref, xs_hbm, pad_start_ref[e], pad_cnt_ref[e], zsem)
            return carry
        lax.fori_loop(0, pad_start_ref.shape[0], per_range, 0)

    def start_segments(tile):
        def per_expert(e, carry):
            seg = tile * n_exp + e
            _segment_copy(xl_ref.at[tile % 2], loff_ref[seg], xs_hbm, goff_ref[seg], cp_ref[seg], sem.at[tile % 2])
            return carry
        lax.fori_loop(0, n_exp, per_expert, 0)

    def wait_segments(tile):
        _wait_rows(xs_hbm, ltot_ref[tile], sem.at[tile % 2])

    @pl.when(j >= 2)
    def _():
        wait_segments(j - 2)

    buf = xl_ref.at[j % 2]
    riota = lax.broadcasted_iota(I32, (lr, tl), 0)
    perm = _one_hot_hits(riota, lambda k: lp_ref[k:k + 1, :]).astype(F32).astype(BF16)
    rows = _dot(perm, _moe_input(x1_ref, mod_ref).astype(BF16))
    gsel = jnp.zeros((lr, tl), F32)
    for k in range(TOP_K):
        gsel = gsel + jnp.where(riota == lp_ref[k:k + 1, :], gate_ref[k:k + 1, :], 0.0)
    rowg = jnp.broadcast_to(jnp.sum(gsel, axis=1, keepdims=True), (lr, LANES))
    buf[:, d // 2:] = lax.bitcast_convert_type(rowg, U32)
    buf[:, 0:d // 2] = _pack_halves(rows)
    start_segments(j)

    @pl.when(j == pl.num_programs(0) - 1)
    def _():
        @pl.when(j >= 1)
        def _():
            wait_segments(j - 1)
        wait_segments(j)
        n_zero = lax.fori_loop(0, pad_cnt_ref.shape[0], lambda e, acc: acc + pad_cnt_ref[e], jnp.int32(0))
        _wait_rows(xs_hbm, n_zero, zsem)


def _dispatch(x1, mods_l, seq_len, lp, gates, goff, loff, cp, ltot, pad_start, pad_cnt, n_rows):
    t, d = x1.shape
    tl = min(MOE_TOKENS, t)
    per_seq = seq_len // tl
    n_exp = goff.shape[0] // (t // tl)
    lr = TOP_K * tl + n_exp * SUBLANES
    tok = pl.BlockSpec((TOP_K, tl), lambda j, *_: (0, j))
    grid_spec = pltpu.PrefetchScalarGridSpec(
        num_scalar_prefetch=6,
        grid=(t // tl,),
        in_specs=[pl.BlockSpec((tl, d), lambda j, *_: (j, 0)),
                  pl.BlockSpec((None, 6, d), lambda j, *_: (j // per_seq, 0, 0)), tok, tok],
        out_specs=pl.BlockSpec(memory_space=pl.ANY),
        scratch_shapes=[pltpu.VMEM((2, lr, d // 2 + LANES), U32), pltpu.VMEM((MOE_TILE, d // 2 + LANES), U32),
                        pltpu.SemaphoreType.DMA((2,)), pltpu.SemaphoreType.DMA],
    )
    return pl.pallas_call(
        functools.partial(_dispatch_body, n_exp=n_exp),
        grid_spec=grid_spec,
        out_shape=jax.ShapeDtypeStruct((n_rows, d // 2 + LANES), U32),
        compiler_params=_params("arbitrary"),
        name="dispatch",
    )(goff, loff, cp, ltot, pad_start, pad_cnt, x1, mods_l, lp, gates)


def _experts_body(be_ref, bsrc_ref, rows_ref, first_ref, slot_ref, next_ref, xs_ref, wgu_hbm, bgu_ref, wdn_hbm,
                  bdn_ref, ys_ref, wgu_f32, wdn_f32, wgu_bf, wdn_bf, sem_gu, sem_dn, *, layer):
    i = pl.program_id(0)
    e = be_ref[i]
    d, f2 = wgu_bf.shape
    f = f2 // 2

    def fetch(expert, slot, start):
        for hbm, buf, sem in ((wgu_hbm, wgu_f32, sem_gu), (wdn_hbm, wdn_f32, sem_dn)):
            dma = pltpu.make_async_copy(hbm.at[layer, expert], buf.at[slot], sem.at[slot])
            dma.start() if start else dma.wait()

    @pl.when(first_ref[i] == 1)
    def _():
        slot = slot_ref[i]

        @pl.when(i == 0)
        def _():
            fetch(e, slot, True)

        fetch(e, slot, False)

        @pl.when(next_ref[i] >= 0)
        def _():
            fetch(next_ref[i], 1 - slot, True)

        def cast(ref_in, ref_out):
            def step(r, c):
                rows = pl.ds(pl.multiple_of(r * CAST_ROWS, CAST_ROWS), CAST_ROWS)
                ref_out[rows, :] = ref_in[rows, :].astype(BF16)
                return c
            lax.fori_loop(0, ref_in.shape[0] // CAST_ROWS, step, 0)
        cast(wgu_f32.at[slot], wgu_bf)
        cast(wdn_f32.at[slot], wdn_bf)

    hd = d // 2
    tm = xs_ref.shape[0]

    def mlp(n):
        x_lo, x_hi = _unpack_halves(xs_ref[0:n, 0:hd])
        proj = lambda cols: _dot(x_lo, wgu_bf[0:hd, cols]) + _dot(x_hi, wgu_bf[hd:, cols]) + bgu_ref[:, cols]
        gate = jnp.minimum(proj(slice(0, f)), SWIGLU_LIMIT)
        lin = jnp.clip(proj(slice(f, f2)), -SWIGLU_LIMIT, SWIGLU_LIMIT)
        act = (gate * jax.nn.sigmoid(SWIGLU_ALPHA * gate) * (lin + 1.0)).astype(BF16)
        row_gate = lax.bitcast_convert_type(xs_ref[0:n, hd:hd + 1], F32)
        y = (_dot(act, wdn_bf[...]) + bdn_ref[...]) * row_gate
        ys_ref[0:n, :] = _pack_halves(y.astype(BF16).astype(F32))
        if n < tm:
            ys_ref[n:, :] = jnp.zeros((tm - n, hd), U32)

    rows = rows_ref[i]
    sizes = [tm * e // 8 for e in EXPERT_ROW_EIGHTHS]
    for n, smaller in zip(sizes, sizes[1:] + [0]):
        @pl.when((rows > smaller) & (rows <= n))
        def _(n=n):
            mlp(n)

    @pl.when(rows == 0)
    def _():
        ys_ref[...] = jnp.zeros_like(ys_ref)


def _experts(xs, block_e, block_src, block_rows, block_first, block_slot, block_next, w_gu, b_gu, w_dn, b_dn, layer):
    n_rows, dx = xs.shape
    depth, n_exp, d, f2 = w_gu.shape
    f = f2 // 2
    tm = MOE_TILE
    grid_spec = pltpu.PrefetchScalarGridSpec(
        num_scalar_prefetch=6,
        grid=(n_rows // tm,),
        in_specs=[
            pl.BlockSpec((tm, dx), lambda i, be, bs, *_: (bs[i], 0)),
            pl.BlockSpec(memory_space=pl.ANY),
            pl.BlockSpec((None, None, 1, f2), lambda i, be, *_: (layer, be[i], 0, 0)),
            pl.BlockSpec(memory_space=pl.ANY),
            pl.BlockSpec((None, None, 1, d), lambda i, be, *_: (layer, be[i], 0, 0)),
        ],
        out_specs=pl.BlockSpec((tm, d // 2), lambda i, *_: (i, 0)),
        scratch_shapes=[pltpu.VMEM((2, d, f2), F32), pltpu.VMEM((2, f, d), F32),
                        pltpu.VMEM((d, f2), BF16), pltpu.VMEM((f, d), BF16),
                        pltpu.SemaphoreType.DMA((2,)), pltpu.SemaphoreType.DMA((2,))],
    )
    return pl.pallas_call(
        functools.partial(_experts_body, layer=layer),
        grid_spec=grid_spec,
        out_shape=jax.ShapeDtypeStruct((n_rows, d // 2), U32),
        compiler_params=_params("arbitrary"),
        name="experts",
    )(block_e, block_src, block_rows, block_first, block_slot, block_next, xs, w_gu,
      b_gu.reshape(depth, n_exp, 1, f2), w_dn, b_dn.reshape(depth, n_exp, 1, d))


def _combine_body(goff_ref, loff_ref, cp_ref, ltot_ref, ys_hbm, x_ref, lp_ref, mod_ref, lng_ref, lnb_ref, o_ref,
                  yl_ref, sem, *, n_exp, alpha):
    j = pl.program_id(0)
    tl = x_ref.shape[0]
    lr, hd = yl_ref.shape[1:]

    def fetch(tile):
        buf = yl_ref.at[tile % 2]

        def per_expert(e, carry):
            seg = tile * n_exp + e
            _segment_copy(ys_hbm, goff_ref[seg], buf, loff_ref[seg], cp_ref[seg], sem.at[tile % 2])
            return carry
        lax.fori_loop(0, n_exp, per_expert, 0)

        def zero_rows(r, carry):
            buf[pl.ds(pl.multiple_of(r * SUBLANES, SUBLANES), SUBLANES), :] = jnp.zeros((SUBLANES, hd), U32)
            return carry
        lax.fori_loop(ltot_ref[tile] // SUBLANES, lr // SUBLANES, zero_rows, 0)

    @pl.when(j == 0)
    def _():
        fetch(j)

    @pl.when(j + 1 < pl.num_programs(0))
    def _():
        fetch(j + 1)

    liota = lax.broadcasted_iota(I32, (tl, lr), 1)
    pick = _one_hot_hits(liota, lambda k: lp_ref[:, k:k + 1]).astype(F32).astype(BF16)
    _wait_rows(ys_hbm, ltot_ref[j], sem.at[j % 2])
    y_lo, y_hi = _unpack_halves(yl_ref[j % 2])
    y = jnp.concatenate([_dot(pick, y_lo), _dot(pick, y_hi)], axis=1)
    o_ref[...] = _layer_norm(alpha * x_ref[...] + mod_ref[5:6, :] * y, lng_ref[...], lnb_ref[...])


def _combine(ys, lp_t, goff, loff, cp, ltot, x1, mods_l, ln_g, ln_b, alpha, seq_len):
    t, d = x1.shape
    tl = min(MOE_TOKENS, t)
    n_exp = goff.shape[0] // (t // tl)
    lr = TOP_K * tl + n_exp * SUBLANES
    per_seq = seq_len // tl
    vec = pl.BlockSpec((1, d), lambda j, *_: (0, 0))
    grid_spec = pltpu.PrefetchScalarGridSpec(
        num_scalar_prefetch=4,
        grid=(t // tl,),
        in_specs=[
            pl.BlockSpec(memory_space=pl.ANY),
            pl.BlockSpec((tl, d), lambda j, *_: (j, 0)),
            pl.BlockSpec((tl, TOP_K), lambda j, *_: (j, 0)),
            pl.BlockSpec((None, 6, d), lambda j, *_: (j // per_seq, 0, 0)),
            vec, vec,
        ],
        out_specs=pl.BlockSpec((tl, d), lambda j, *_: (j, 0)),
        scratch_shapes=[pltpu.VMEM((2, lr, d // 2), U32), pltpu.SemaphoreType.DMA((2,))],
    )
    return pl.pallas_call(
        functools.partial(_combine_body, n_exp=n_exp, alpha=alpha),
        grid_spec=grid_spec,
        out_shape=jax.ShapeDtypeStruct((t, d), F32),
        compiler_params=_params("arbitrary"),
        name="combine",
    )(goff, loff, cp, ltot, ys, x1, lp_t, mods_l, ln_g.reshape(1, d), ln_b.reshape(1, d))


def _moe_layer(x1, mods_l, w_r, b_r, w_gu, b_gu, w_dn, b_dn, layer, ln_g, ln_b, alpha, seq_len):
    t, d = x1.shape
    n_exp = w_r.shape[1]
    tm = MOE_TILE
    tl = min(MOE_TOKENS, t)
    n_tiles = t // tl
    idx, gates, rank, cnt = _router(x1, mods_l, seq_len, w_r, b_r)
    cp = (cnt[:, :, 0] + SUBLANES - 1) // SUBLANES * SUBLANES
    tot = jnp.sum(cp, axis=0)
    padded = (tot + tm - 1) // tm * tm
    pend = jnp.cumsum(padded)
    pstart = pend - padded
    goff = pstart[None, :] + jnp.cumsum(cp, axis=0) - cp
    loff = jnp.cumsum(cp, axis=1) - cp
    ltot = jnp.sum(cp, axis=1).astype(I32)
    is_e = idx[..., None] == jnp.arange(n_exp, dtype=I32)
    lp = jnp.sum(jnp.where(is_e, jnp.repeat(loff, tl, axis=0)[None], 0), axis=-1) + rank
    n_blocks = -(-(t * TOP_K + n_tiles * n_exp * (SUBLANES - 1)) // tm) + n_exp
    n_valid = pend[-1] // tm
    blk = jnp.arange(n_blocks, dtype=I32)
    block_src = jnp.minimum(blk, n_valid - 1)
    block_e = jnp.sum(pend[None, :] <= (block_src * tm)[:, None], axis=1).astype(I32)
    used = padded > 0
    experts = jnp.arange(n_exp, dtype=I32)
    next_used = jnp.flip(lax.cummin(jnp.flip(jnp.where(used, experts, n_exp))))
    next_used = jnp.concatenate([next_used[1:], jnp.full((1,), n_exp, I32)])
    next_used = jnp.where(next_used < n_exp, next_used, -1)
    slot_of = (jnp.cumsum(used.astype(I32)) - 1) % 2
    of_block = lambda table: jnp.sum(jnp.where(block_e[:, None] == experts[None, :], table[None, :], 0), axis=1)
    block_first = ((blk * tm == of_block(pstart)) & (blk < n_valid)).astype(I32)
    block_slot = of_block(slot_of).astype(I32)
    block_next = of_block(next_used).astype(I32)
    block_rows = jnp.where(blk < n_valid, jnp.clip(of_block(pstart + tot) - blk * tm, 0, tm), 0).astype(I32)
    n_rows = n_blocks * tm
    pad_start = jnp.concatenate([pstart + tot, pend[-1:]]).astype(I32)
    pad_cnt = jnp.concatenate([padded - tot, n_rows - pend[-1:]]).astype(I32)
    flat = lambda a: a.reshape(-1).astype(I32)
    seg = (flat(goff), flat(loff), flat(cp), ltot)
    xs = _dispatch(x1, mods_l, seq_len, lp, gates, *seg, pad_start, pad_cnt, n_rows)
    ys = _experts(xs, block_e, block_src.astype(I32), block_rows, block_first, block_slot, block_next,
                  w_gu, b_gu, w_dn, b_dn, layer)
    return _combine(ys, lp.T, *seg, x1, mods_l, ln_g, ln_b, alpha, seq_len)


def kernel(x, c, ada_w, ada_b, post_ln_g, post_ln_b, conv_w_pw1, conv_b_pw1, conv_w_dw, conv_b_dw, conv_ln_g, conv_ln_b, conv_w_pw2, conv_b_pw2, w_kv, attn_w_q, attn_lambda, attn_subln_g, attn_w_o, rel_bias_table, router_w, router_b, expert_w_gate_up, expert_b_gate_up, expert_w_down, expert_b_down):
    bsz, s, d = x.shape
    depth = ada_w.shape[0]
    n_a = depth // 2
    alpha = (2 * depth) ** 0.25
    mods = _ada(c, ada_w, ada_b).reshape(depth, bsz, 6, d)
    q = k = vt = None
    for l in range(depth):
        mods_l = mods[l]
        if l < n_a:
            x1 = _conv(x, mods_l, conv_w_pw1[l], conv_b_pw1[l], conv_w_dw[l], conv_b_dw[l], conv_ln_g[l],
                       conv_ln_b[l], conv_w_pw2[l], conv_b_pw2[l], post_ln_g[l, 0], post_ln_b[l, 0], alpha)
        else:
            j = l - n_a
            if j == 0:
                q, k, vt = _qkv(x, mods_l, w_kv, attn_w_q[j])
            else:
                q = _qkv(x, mods_l, w_kv, attn_w_q[j])[0]
            lambda_init = 0.8 - 0.6 * math.exp(-0.3 * l)
            ot = _attn(q, k, vt, attn_lambda[j], attn_subln_g[j], rel_bias_table, lambda_init)
            x1 = _attn_out(ot, x, mods_l, attn_w_o[j], post_ln_g[l, 0], post_ln_b[l, 0], alpha)
        x = _moe_layer(x1.reshape(bsz * s, d), mods_l, router_w[l], router_b[l],
                       expert_w_gate_up, expert_b_gate_up, expert_w_down, expert_b_down, l,
                       post_ln_g[l, 1], post_ln_b[l, 1], alpha, s).reshape(bsz, s, d)
    return x
```

```python
import functools
import math

import jax
import jax.numpy as jnp
from jax import lax
from jax.experimental import pallas as pl
from jax.experimental.pallas import tpu as pltpu

F32 = jnp.float32
BF16 = jnp.bfloat16
I32 = jnp.int32
U32 = jnp.uint32
HIGHEST = lax.Precision.HIGHEST

CHUNK = 64
CONV_WIDTH = 31
HEAD_DIM = 64
REL_BUCKETS = 32
REL_MAX_DIST = 128
TOP_K = 4
SWIGLU_LIMIT = 7.0
SWIGLU_ALPHA = 1.702
LN_EPS = 1e-5
MASK_VALUE = -1e30
LOG2_E = math.log2(math.e)

SUBLANES = 8
LANES = 128
VMEM_LIMIT_BYTES = 56 * 1024 * 1024

ADA_TN = 1024
SEQ_TILE = 512
PROJ_TILE = 512
CONV_HALO = 32
CONV_ROWS = 256
CONV_COLS = 128
MOE_TOKENS = 256
ROUTER_TOKENS = 2048
MOE_TILES_PER_STEP = 2
MOE_TILE = 512
Q_BLOCK = 128
EXPERT_ROW_EIGHTHS = (8, 6, 4, 2)
GLU_LOOKAHEAD = 1
SCORE_LOOKAHEAD = 3
CAST_ROWS = 128


def _params(*sem):
    return pltpu.CompilerParams(dimension_semantics=sem, vmem_limit_bytes=VMEM_LIMIT_BYTES)


def _layer_norm(x, g, b):
    mu = jnp.mean(x, axis=-1, keepdims=True)
    xc = x - mu
    var = jnp.mean(xc * xc, axis=-1, keepdims=True)
    return xc * lax.rsqrt(var + LN_EPS) * g + b


def _dot(a, b):
    return jnp.dot(a, b, preferred_element_type=F32)


def _ada_body(c_ref, w_ref, b_ref, o_ref):
    c = c_ref[...]
    cond = c * jax.nn.sigmoid(c)
    o_ref[...] = jnp.dot(cond, w_ref[...], preferred_element_type=F32, precision=HIGHEST) + b_ref[...]


def _ada(c, ada_w, ada_b):
    depth, d, n = ada_w.shape
    bsz = c.shape[0]
    tn = min(ADA_TN, n)
    return pl.pallas_call(
        _ada_body,
        grid=(depth, n // tn),
        in_specs=[
            pl.BlockSpec((bsz, d), lambda l, j: (0, 0)),
            pl.BlockSpec((None, d, tn), lambda l, j: (l, 0, j)),
            pl.BlockSpec((None, 1, tn), lambda l, j: (l, 0, j)),
        ],
        out_specs=pl.BlockSpec((None, bsz, tn), lambda l, j: (l, 0, j)),
        out_shape=jax.ShapeDtypeStruct((depth, bsz, n), F32),
        compiler_params=_params("parallel", "parallel"),
        name="ada",
    )(c, ada_w, ada_b.reshape(depth, 1, n))


def _residual_epilogue(x, y, mod_ref, lng_ref, lnb_ref, alpha, x1_ref):
    x1_ref[...] = _layer_norm(alpha * x + mod_ref[2:3, :] * y, lng_ref[...], lnb_ref[...])


def _moe_input(x1_ref, mod_ref, rows=slice(None)):
    return x1_ref[rows, :] * (1.0 + mod_ref[4:5, :]) + mod_ref[3:4, :]


def _conv_body(x_ref, halo_ref, mod_ref, w1_ref, b1_ref, wdw_ref, bdw_ref, cg_ref, cb_ref, w2_ref, b2_ref,
               lng_ref, lnb_ref, x1_ref, win_ref, v_ref, *, alpha):
    ts, d = x_ref.shape
    i = pl.program_id(1)
    xw = jnp.concatenate([halo_ref[...], x_ref[...]], axis=0)
    h = (xw * (1.0 + mod_ref[1:2, :]) + mod_ref[0:1, :]).astype(BF16)
    rows = min(CONV_ROWS, ts)
    cols = min(CONV_COLS, d)

    def glu(c0):
        a = _dot(h, w1_ref[:, c0:c0 + cols]) + b1_ref[:, c0:c0 + cols]
        g = _dot(h, w1_ref[:, d + c0:d + c0 + cols]) + b1_ref[:, d + c0:d + c0 + cols]
        return a * jax.nn.sigmoid(g)

    off = CONV_HALO - (CONV_WIDTH - 1)
    n_shift = ts + CONV_HALO - SUBLANES
    in_seq = (lax.broadcasted_iota(I32, (ts + CONV_HALO, cols), 0) >= CONV_HALO) | (i > 0)
    starts = list(range(0, d, cols))
    ahead = [glu(c) for c in starts[:GLU_LOOKAHEAD]]
    for n, c0 in enumerate(starts):
        chan = slice(c0, c0 + cols)
        win_ref[0, :, chan] = jnp.where(in_seq, ahead.pop(0), 0.0)
        if n + GLU_LOOKAHEAD < len(starts):
            ahead.append(glu(starts[n + GLU_LOOKAHEAD]))
        for b in range(1, SUBLANES):
            win_ref[b, 0:n_shift, chan] = win_ref[0, b:b + n_shift, chan]
        for r0 in range(0, ts, rows):
            acc = jnp.zeros((rows, cols), F32)
            for j in range(CONV_WIDTH):
                a, b = divmod(off + j, SUBLANES)
                r = r0 + a * SUBLANES
                acc = acc + wdw_ref[j:j + 1, c0:c0 + cols] * win_ref[b, r:r + rows, c0:c0 + cols]
            v_ref[r0:r0 + rows, c0:c0 + cols] = acc
    v = _layer_norm(v_ref[...] + bdw_ref[...], cg_ref[...], cb_ref[...])
    v = (v * jax.nn.sigmoid(v)).astype(BF16)
    y = _dot(v, w2_ref[...]) + b2_ref[...]
    _residual_epilogue(x_ref[...], y, mod_ref, lng_ref, lnb_ref, alpha, x1_ref)


def _conv(x, mods_l, w_pw1, b_pw1, w_dw, b_dw, cln_g, cln_b, w_pw2, b_pw2, ln_g, ln_b, alpha):
    bsz, s, d = x.shape
    ts = min(SEQ_TILE, s)
    hb = ts // CONV_HALO
    row = lambda a: a.reshape(1, d)
    tile = pl.BlockSpec((None, ts, d), lambda b, i: (b, i, 0))
    vec = pl.BlockSpec((1, d), lambda b, i: (0, 0))
    return pl.pallas_call(
        functools.partial(_conv_body, alpha=alpha),
        grid=(bsz, s // ts),
        in_specs=[
            tile,
            pl.BlockSpec((None, CONV_HALO, d), lambda b, i: (b, jnp.maximum(i * hb - 1, 0), 0)),
            pl.BlockSpec((None, 6, d), lambda b, i: (b, 0, 0)),
            pl.BlockSpec((d, 2 * d), lambda b, i: (0, 0)),
            pl.BlockSpec((1, 2 * d), lambda b, i: (0, 0)),
            pl.BlockSpec((CONV_WIDTH, d), lambda b, i: (0, 0)),
            vec, vec, vec,
            pl.BlockSpec((d, d), lambda b, i: (0, 0)),
            vec, vec, vec,
        ],
        out_specs=tile,
        out_shape=jax.ShapeDtypeStruct((bsz, s, d), F32),
        scratch_shapes=[pltpu.VMEM((SUBLANES, ts + CONV_HALO, d), F32), pltpu.VMEM((ts, d), F32)],
        compiler_params=_params("parallel", "parallel"),
        name="conv",
    )(x, x, mods_l, w_pw1.astype(BF16), b_pw1.reshape(1, 2 * d), w_dw, row(b_dw), row(cln_g), row(cln_b),
      w_pw2.astype(BF16), row(b_pw2), row(ln_g), row(ln_b))


def _qkv_body(x_ref, mod_ref, wkv_ref, wq_ref, q_ref, k_ref, vt_ref):
    d = x_ref.shape[-1]
    x = x_ref[...]
    kv = _dot(x.astype(BF16), wkv_ref[...])
    k_ref[...] = kv[:, :d].astype(BF16)
    hd2 = 2 * HEAD_DIM
    for c0 in range(0, d, hd2):
        vt_ref[c0:c0 + hd2, :] = kv[:, d + c0:d + c0 + hd2].T.astype(BF16)
    h = (x * (1.0 + mod_ref[1:2, :]) + mod_ref[0:1, :]).astype(BF16)
    q_ref[...] = (_dot(h, wq_ref[...]) * (HEAD_DIM ** -0.5 * LOG2_E)).astype(BF16)


def _qkv(x, mods_l, w_kv, w_q):
    bsz, s, d = x.shape
    ts = min(PROJ_TILE, s)
    tile = pl.BlockSpec((None, ts, d), lambda b, i: (b, i, 0))
    return pl.pallas_call(
        _qkv_body,
        grid=(bsz, s // ts),
        in_specs=[
            tile,
            pl.BlockSpec((None, 6, d), lambda b, i: (b, 0, 0)),
            pl.BlockSpec((d, 2 * d), lambda b, i: (0, 0)),
            pl.BlockSpec((d, d), lambda b, i: (0, 0)),
        ],
        out_specs=[tile, tile, pl.BlockSpec((None, d, ts), lambda b, i: (b, 0, i))],
        out_shape=[jax.ShapeDtypeStruct((bsz, s, d), BF16)] * 2 + [jax.ShapeDtypeStruct((bsz, d, s), BF16)],
        compiler_params=_params("parallel", "parallel"),
        name="qkv",
    )(x, mods_l, w_kv.astype(BF16), w_q.astype(BF16))


def _t5_bucket(rel):
    nb = REL_BUCKETS // 2
    ret = jnp.where(rel > 0, nb, 0)
    n = jnp.abs(rel)
    max_exact = nb // 2
    large = max_exact + (jnp.log(jnp.maximum(n, 1).astype(F32) / max_exact)
                         / math.log(REL_MAX_DIST / max_exact) * (nb - max_exact)).astype(I32)
    large = jnp.minimum(large, nb - 1)
    return ret + jnp.where(n < max_exact, n, large)


def _saturation_distance():
    nb = REL_BUCKETS // 2
    max_exact = nb // 2
    n = max_exact
    while max_exact + math.log(n / max_exact) / math.log(REL_MAX_DIST / max_exact) * (nb - max_exact) < nb - 1 + 1e-3:
        n += 1
    return n


def _bucket_strip(s):
    r = jnp.arange(Q_BLOCK, dtype=I32)[:, None]
    kp = jnp.arange(s, dtype=I32)[None, :] - (s - Q_BLOCK)
    bucket = _t5_bucket(kp - r)
    visible = jnp.floor_divide(kp, CHUNK) <= (r // CHUNK)
    return jnp.where(visible, bucket, REL_BUCKETS)


def _attn_body(tab_ref, q_ref, k_ref, vt_ref, bkt_ref, lam_ref, sg_ref, ot_ref, bias_ref, *, lambda_init, n_heads):
    h = pl.program_id(0)
    b = pl.program_id(1)
    s = q_ref.shape[0]

    near = bias_ref.shape[0]

    @pl.when(b == 0)
    def _():
        bk = bkt_ref[...]
        acc = jnp.full(bk.shape, MASK_VALUE, F32)
        for r in range(REL_BUCKETS):
            acc = jnp.where(bk == r, tab_ref[r * n_heads + h] * LOG2_E, acc)
        bias_ref[:, 0:Q_BLOCK] = acc
        bias_ref[:, Q_BLOCK:] = acc

    far_bias = tab_ref[(REL_BUCKETS // 2 - 1) * n_heads + h] * LOG2_E

    lp = lam_ref[...]
    lam = (jnp.exp(jnp.sum(lp[0:1, :] * lp[1:2, :], axis=-1, keepdims=True))
           - jnp.exp(jnp.sum(lp[2:3, :] * lp[3:4, :], axis=-1, keepdims=True)) + lambda_init)
    lane = lax.broadcasted_iota(I32, (Q_BLOCK, 2 * HEAD_DIM), 1)
    nt = (((1,), (1,)), ((), ()))

    def scores(i):
        n_keys = (i + 1) * Q_BLOCK
        q = q_ref[i * Q_BLOCK:(i + 1) * Q_BLOCK, :]
        qq = jnp.concatenate([jnp.where(lane < HEAD_DIM, q, jnp.zeros_like(q)),
                              jnp.where(lane >= HEAD_DIM, q, jnp.zeros_like(q))], axis=0)
        return lax.dot_general(k_ref[0:n_keys, :], qq, nt, preferred_element_type=F32)

    def values(i, p, denom):
        pv = _dot(vt_ref[:, 0:(i + 1) * Q_BLOCK], p)
        o = pv[:, :Q_BLOCK] * (1.0 / denom[:, :Q_BLOCK]) - pv[:, Q_BLOCK:] * (lam / denom[:, Q_BLOCK:])
        o = o * lax.rsqrt(jnp.mean(o * o, axis=0, keepdims=True) + LN_EPS) * sg_ref[...]
        ot_ref[:, i * Q_BLOCK:(i + 1) * Q_BLOCK] = (o * (1.0 - lambda_init)).astype(BF16)

    n_blocks = s // Q_BLOCK
    ahead = [scores(i) for i in range(min(SCORE_LOOKAHEAD, n_blocks))]
    for i in range(n_blocks):
        raw = ahead.pop(0)
        if i + SCORE_LOOKAHEAD < n_blocks:
            ahead.append(scores(i + SCORE_LOOKAHEAD))
        n_keys = (i + 1) * Q_BLOCK
        n_near = min(near, n_keys)
        n_far = n_keys - n_near
        sc = raw[n_far:] + bias_ref[near - n_near:, :]
        top = jnp.max(sc, axis=0, keepdims=True)
        if n_far:
            top = jnp.maximum(top, jnp.max(raw[:n_far], axis=0, keepdims=True) + far_bias)
            p = jnp.concatenate([jnp.exp2(raw[:n_far] - (top - far_bias)), jnp.exp2(sc - top)], axis=0)
        else:
            p = jnp.exp2(sc - top)
        values(i, p.astype(BF16), jnp.sum(p, axis=0, keepdims=True))


def _attn(q, k, vt, lam_p, subln_g, rel_table, lambda_init):
    bsz, s, d = q.shape
    hd2 = 2 * HEAD_DIM
    n_heads = d // hd2
    assert REL_MAX_DIST >= CHUNK and _saturation_distance() <= REL_MAX_DIST
    near = min(Q_BLOCK + REL_MAX_DIST, s)
    head = pl.BlockSpec((None, s, hd2), lambda h, b, tab: (b, 0, h))
    head_t = pl.BlockSpec((None, hd2, s), lambda h, b, tab: (b, h, 0))
    grid_spec = pltpu.PrefetchScalarGridSpec(
        num_scalar_prefetch=1,
        grid=(n_heads, bsz),
        in_specs=[
            head, head, head_t,
            pl.BlockSpec((near, Q_BLOCK), lambda h, b, tab: (0, 0)),
            pl.BlockSpec((4, HEAD_DIM), lambda h, b, tab: (0, 0)),
            pl.BlockSpec((hd2, 1), lambda h, b, tab: (0, 0)),
        ],
        out_specs=head_t,
        scratch_shapes=[pltpu.VMEM((near, 2 * Q_BLOCK), F32)],
    )
    return pl.pallas_call(
        functools.partial(_attn_body, lambda_init=lambda_init, n_heads=n_heads),
        grid_spec=grid_spec,
        out_shape=jax.ShapeDtypeStruct((bsz, d, s), BF16),
        compiler_params=_params("arbitrary", "arbitrary"),
        name="attn",
    )(rel_table.reshape(-1), q, k, vt, _bucket_strip(s).T[s - near:], lam_p, subln_g.reshape(hd2, 1))


def _attn_out_body(ot_ref, x_ref, mod_ref, wo_ref, lng_ref, lnb_ref, x1_ref, *, alpha):
    y = lax.dot_general(ot_ref[...], wo_ref[...], (((0,), (0,)), ((), ())), preferred_element_type=F32)
    _residual_epilogue(x_ref[...], y, mod_ref, lng_ref, lnb_ref, alpha, x1_ref)


def _attn_out(o, x, mods_l, w_o, ln_g, ln_b, alpha):
    bsz, s, d = x.shape
    ts = min(PROJ_TILE, s)
    tile = pl.BlockSpec((None, ts, d), lambda b, i: (b, i, 0))
    vec = pl.BlockSpec((1, d), lambda b, i: (0, 0))
    return pl.pallas_call(
        functools.partial(_attn_out_body, alpha=alpha),
        grid=(bsz, s // ts),
        in_specs=[pl.BlockSpec((None, d, ts), lambda b, i: (b, 0, i)), tile,
                  pl.BlockSpec((None, 6, d), lambda b, i: (b, 0, 0)),
                  pl.BlockSpec((d, d), lambda b, i: (0, 0)), vec, vec],
        out_specs=tile,
        out_shape=jax.ShapeDtypeStruct((bsz, s, d), F32),
        compiler_params=_params("parallel", "parallel"),
        name="attn_out",
    )(o, x, mods_l, w_o.astype(BF16), ln_g.reshape(1, d), ln_b.reshape(1, d))


def _router_body(x1_ref, mod_ref, whi_ref, wlo_ref, b_ref, tri_ref, idx_ref, gate_ref, rank_ref, cnt_ref):
    h = _moe_input(x1_ref, mod_ref)
    h_hi = h.astype(BF16)
    h_lo = (h - h_hi.astype(F32)).astype(BF16)
    nt = (((1,), (1,)), ((), ()))
    logits = (lax.dot_general(whi_ref[...], h_hi, nt, preferred_element_type=F32)
              + (lax.dot_general(whi_ref[...], h_lo, nt, preferred_element_type=F32)
                 + lax.dot_general(wlo_ref[...], h_hi, nt, preferred_element_type=F32))) + b_ref[...]
    n_exp, tr = logits.shape
    eio = lax.broadcasted_iota(I32, (n_exp, tr), 0)
    work = logits
    vals, idxs = [], []
    for _ in range(TOP_K):
        m = jnp.max(work, axis=0, keepdims=True)
        am = jnp.min(jnp.where(work == m, eio, n_exp), axis=0, keepdims=True)
        vals.append(m)
        idxs.append(am)
        work = jnp.where(eio == am, -jnp.inf, work)
    ex = [jnp.exp(v - vals[0]) for v in vals]
    den = ex[0] + ex[1] + ex[2] + ex[3]
    onehot = jnp.zeros((n_exp, tr), F32)
    for k in range(TOP_K):
        onehot = onehot + (eio == idxs[k]).astype(F32)
    tl = tri_ref.shape[0]
    tiles = [onehot[:, t0:t0 + tl] for t0 in range(0, tr, tl)]
    before = jnp.concatenate([_dot(oh.astype(BF16), tri_ref[...]) for oh in tiles], axis=1)
    for k in range(TOP_K):
        idx_ref[k:k + 1, :] = idxs[k]
        gate_ref[k:k + 1, :] = ex[k] / den
        rank_ref[k:k + 1, :] = jnp.sum(jnp.where(eio == idxs[k], before, 0.0), axis=0, keepdims=True).astype(I32)
    for j, oh in enumerate(tiles):
        cnt_ref[j] = jnp.sum(oh, axis=1, keepdims=True).astype(I32)


def _router(x1, mods_l, seq_len, w_r, b_r):
    t, d = x1.shape
    n_exp = w_r.shape[1]
    tl = min(MOE_TOKENS, t)
    tr = min(ROUTER_TOKENS, seq_len)
    per_seq = seq_len // tr
    wt = w_r.T
    wt_hi = wt.astype(BF16)
    wt_lo = (wt - wt_hi.astype(F32)).astype(BF16)
    pos = jnp.arange(tl, dtype=I32)
    tri = (pos[:, None] < pos[None, :]).astype(BF16)
    tok = pl.BlockSpec((TOP_K, tr), lambda i: (0, i))
    return pl.pallas_call(
        _router_body,
        grid=(t // tr,),
        in_specs=[
            pl.BlockSpec((tr, d), lambda i: (i, 0)),
            pl.BlockSpec((None, 6, d), lambda i: (i // per_seq, 0, 0)),
            pl.BlockSpec((n_exp, d), lambda i: (0, 0)),
            pl.BlockSpec((n_exp, d), lambda i: (0, 0)),
            pl.BlockSpec((n_exp, 1), lambda i: (0, 0)),
            pl.BlockSpec((tl, tl), lambda i: (0, 0)),
        ],
        out_specs=[tok, tok, tok, pl.BlockSpec((tr // tl, n_exp, 1), lambda i: (i, 0, 0))],
        out_shape=[jax.ShapeDtypeStruct((TOP_K, t), I32), jax.ShapeDtypeStruct((TOP_K, t), F32),
                   jax.ShapeDtypeStruct((TOP_K, t), I32), jax.ShapeDtypeStruct((t // tl, n_exp, 1), I32)],
        compiler_params=_params("parallel"),
        name="router",
    )(x1, mods_l, wt_hi, wt_lo, b_r.reshape(n_exp, 1), tri)


def _rows(ref, start, cnt):
    aligned = lambda v: v if isinstance(v, int) else pl.multiple_of(v, SUBLANES)
    return ref.at[pl.ds(aligned(start), aligned(cnt))]


def _segment_copy(src, src_row, dst, dst_row, cnt, sem):
    @pl.when(cnt > 0)
    def _():
        pltpu.make_async_copy(_rows(src, src_row, cnt), _rows(dst, dst_row, cnt), sem).start()


def _wait_rows(ref, cnt, sem):
    @pl.when(cnt > 0)
    def _():
        pltpu.make_async_copy(_rows(ref, 0, cnt), _rows(ref, 0, cnt), sem).wait()


def _zero_fill(zero_ref, xs_hbm, start, cnt, sem):
    zr = zero_ref.shape[0]
    n_full = cnt // zr

    def full(r, c):
        _segment_copy(zero_ref, 0, xs_hbm, start + r * zr, zr, sem)
        return c

    lax.fori_loop(0, n_full, full, 0)
    _segment_copy(zero_ref, 0, xs_hbm, start + n_full * zr, cnt - n_full * zr, sem)


def _pack_halves(x):
    hd = x.shape[1] // 2
    lo = lax.bitcast_convert_type(x[:, :hd], U32)
    hi = lax.bitcast_convert_type(x[:, hd:], U32)
    return (lo >> 16) | (hi & jnp.uint32(0xFFFF0000))


def _unpack_halves(words):
    lo = lax.bitcast_convert_type(words << 16, F32).astype(BF16)
    hi = lax.bitcast_convert_type(words & jnp.uint32(0xFFFF0000), F32).astype(BF16)
    return lo, hi


def _tiles_per_step(seq_len, tl):
    return MOE_TILES_PER_STEP if seq_len % (tl * MOE_TILES_PER_STEP) == 0 else 1


def _one_hot_hits(iota, pos_of):
    hit = iota == pos_of(0)
    for k in range(1, TOP_K):
        hit = hit | (iota == pos_of(k))
    return hit


def _dispatch_body(goff_ref, loff_ref, cp_ref, ltot_ref, pad_start_ref, pad_cnt_ref, x1_ref, mod_ref, lp_ref,
                   gate_ref, xs_hbm, xl_ref, zero_ref, sem, zsem, *, n_exp, tiles_per_step):
    j = pl.program_id(0)
    d = x1_ref.shape[1]
    tl = x1_ref.shape[0] // tiles_per_step
    lr = xl_ref.shape[1]

    @pl.when(j == 0)
    def _():
        zero_ref[...] = jnp.zeros_like(zero_ref)

        def per_range(e, carry):
            _zero_fill(zero_ref, xs_hbm, pad_start_ref[e], pad_cnt_ref[e], zsem)
            return carry
        lax.fori_loop(0, pad_start_ref.shape[0], per_range, 0)

    def start_segments(tile):
        def per_expert(e, carry):
            seg = tile * n_exp + e
            _segment_copy(xl_ref.at[tile % 2], loff_ref[seg], xs_hbm, goff_ref[seg], cp_ref[seg], sem.at[tile % 2])
            return carry
        lax.fori_loop(0, n_exp, per_expert, 0)

    def wait_segments(tile):
        _wait_rows(xs_hbm, ltot_ref[tile], sem.at[tile % 2])

    riota = lax.broadcasted_iota(I32, (lr, tl), 0)
    for t in range(tiles_per_step):
        tile = j * tiles_per_step + t
        tok = slice(t * tl, (t + 1) * tl)

        @pl.when(tile >= 2)
        def _():
            wait_segments(tile - 2)

        buf = xl_ref.at[tile % 2]
        perm = _one_hot_hits(riota, lambda k: lp_ref[k:k + 1, tok]).astype(F32).astype(BF16)
        rows = _dot(perm, _moe_input(x1_ref, mod_ref, tok).astype(BF16))
        gsel = jnp.zeros((lr, tl), F32)
        for k in range(TOP_K):
            gsel = gsel + jnp.where(riota == lp_ref[k:k + 1, tok], gate_ref[k:k + 1, tok], 0.0)
        rowg = jnp.broadcast_to(jnp.sum(gsel, axis=1, keepdims=True), (lr, LANES))
        buf[:, d // 2:] = lax.bitcast_convert_type(rowg, U32)
        buf[:, 0:d // 2] = _pack_halves(rows)
        start_segments(tile)

    @pl.when(j == pl.num_programs(0) - 1)
    def _():
        last = pl.num_programs(0) * tiles_per_step - 1

        @pl.when(last >= 1)
        def _():
            wait_segments(last - 1)
        wait_segments(last)
        n_zero = lax.fori_loop(0, pad_cnt_ref.shape[0], lambda e, acc: acc + pad_cnt_ref[e], jnp.int32(0))
        _wait_rows(xs_hbm, n_zero, zsem)


def _dispatch(x1, mods_l, seq_len, lp, gates, goff, loff, cp, ltot, pad_start, pad_cnt, n_rows):
    t, d = x1.shape
    tl = min(MOE_TOKENS, t)
    n_exp = goff.shape[0] // (t // tl)
    lr = TOP_K * tl + n_exp * SUBLANES
    tps = _tiles_per_step(seq_len, tl)
    ts = tl * tps
    per_seq = seq_len // ts
    tok = pl.BlockSpec((TOP_K, ts), lambda j, *_: (0, j))
    grid_spec = pltpu.PrefetchScalarGridSpec(
        num_scalar_prefetch=6,
        grid=(t // ts,),
        in_specs=[pl.BlockSpec((ts, d), lambda j, *_: (j, 0)),
                  pl.BlockSpec((None, 6, d), lambda j, *_: (j // per_seq, 0, 0)), tok, tok],
        out_specs=pl.BlockSpec(memory_space=pl.ANY),
        scratch_shapes=[pltpu.VMEM((2, lr, d // 2 + LANES), U32), pltpu.VMEM((MOE_TILE, d // 2 + LANES), U32),
                        pltpu.SemaphoreType.DMA((2,)), pltpu.SemaphoreType.DMA],
    )
    return pl.pallas_call(
        functools.partial(_dispatch_body, n_exp=n_exp, tiles_per_step=tps),
        grid_spec=grid_spec,
        out_shape=jax.ShapeDtypeStruct((n_rows, d // 2 + LANES), U32),
        compiler_params=_params("arbitrary"),
        name="dispatch",
    )(goff, loff, cp, ltot, pad_start, pad_cnt, x1, mods_l, lp, gates)


def _experts_body(be_ref, bsrc_ref, rows_ref, first_ref, slot_ref, next_ref, xs_ref, wgu_hbm, bgu_ref, wdn_hbm,
                  bdn_ref, ys_ref, wgu_f32, wdn_f32, wgu_bf, wdn_bf, sem_gu, sem_dn, *, layer):
    i = pl.program_id(0)
    e = be_ref[i]
    d, f2 = wgu_bf.shape
    f = f2 // 2

    def fetch(expert, slot, start):
        for hbm, buf, sem in ((wgu_hbm, wgu_f32, sem_gu), (wdn_hbm, wdn_f32, sem_dn)):
            dma = pltpu.make_async_copy(hbm.at[layer, expert], buf.at[slot], sem.at[slot])
            dma.start() if start else dma.wait()

    @pl.when(first_ref[i] == 1)
    def _():
        slot = slot_ref[i]

        @pl.when(i == 0)
        def _():
            fetch(e, slot, True)

        fetch(e, slot, False)

        @pl.when(next_ref[i] >= 0)
        def _():
            fetch(next_ref[i], 1 - slot, True)

        def cast(ref_in, ref_out):
            def step(r, c):
                rows = pl.ds(pl.multiple_of(r * CAST_ROWS, CAST_ROWS), CAST_ROWS)
                ref_out[rows, :] = ref_in[rows, :].astype(BF16)
                return c
            lax.fori_loop(0, ref_in.shape[0] // CAST_ROWS, step, 0)
        cast(wgu_f32.at[slot], wgu_bf)
        cast(wdn_f32.at[slot], wdn_bf)

    hd = d // 2
    tm = xs_ref.shape[0]

    def mlp(n):
        x_lo, x_hi = _unpack_halves(xs_ref[0:n, 0:hd])
        proj = lambda cols: _dot(x_lo, wgu_bf[0:hd, cols]) + _dot(x_hi, wgu_bf[hd:, cols]) + bgu_ref[:, cols]
        gate = jnp.minimum(proj(slice(0, f)), SWIGLU_LIMIT)
        lin = jnp.clip(proj(slice(f, f2)), -SWIGLU_LIMIT, SWIGLU_LIMIT)
        act = (gate * jax.nn.sigmoid(SWIGLU_ALPHA * gate) * (lin + 1.0)).astype(BF16)
        row_gate = lax.bitcast_convert_type(xs_ref[0:n, hd:hd + 1], F32)
        y = (_dot(act, wdn_bf[...]) + bdn_ref[...]) * row_gate
        ys_ref[0:n, :] = _pack_halves(y.astype(BF16).astype(F32))
        if n < tm:
            ys_ref[n:, :] = jnp.zeros((tm - n, hd), U32)

    rows = rows_ref[i]
    sizes = [tm * e // 8 for e in EXPERT_ROW_EIGHTHS]
    for n, smaller in zip(sizes, sizes[1:] + [0]):
        @pl.when((rows > smaller) & (rows <= n))
        def _(n=n):
            mlp(n)

    @pl.when(rows == 0)
    def _():
        ys_ref[...] = jnp.zeros_like(ys_ref)


def _experts(xs, block_e, block_src, block_rows, block_first, block_slot, block_next, w_gu, b_gu, w_dn, b_dn, layer):
    n_rows, dx = xs.shape
    depth, n_exp, d, f2 = w_gu.shape
    f = f2 // 2
    tm = MOE_TILE
    grid_spec = pltpu.PrefetchScalarGridSpec(
        num_scalar_prefetch=6,
        grid=(n_rows // tm,),
        in_specs=[
            pl.BlockSpec((tm, dx), lambda i, be, bs, *_: (bs[i], 0)),
            pl.BlockSpec(memory_space=pl.ANY),
            pl.BlockSpec((None, None, 1, f2), lambda i, be, *_: (layer, be[i], 0, 0)),
            pl.BlockSpec(memory_space=pl.ANY),
            pl.BlockSpec((None, None, 1, d), lambda i, be, *_: (layer, be[i], 0, 0)),
        ],
        out_specs=pl.BlockSpec((tm, d // 2), lambda i, *_: (i, 0)),
        scratch_shapes=[pltpu.VMEM((2, d, f2), F32), pltpu.VMEM((2, f, d), F32),
                        pltpu.VMEM((d, f2), BF16), pltpu.VMEM((f, d), BF16),
                        pltpu.SemaphoreType.DMA((2,)), pltpu.SemaphoreType.DMA((2,))],
    )
    return pl.pallas_call(
        functools.partial(_experts_body, layer=layer),
        grid_spec=grid_spec,
        out_shape=jax.ShapeDtypeStruct((n_rows, d // 2), U32),
        compiler_params=_params("arbitrary"),
        name="experts",
    )(block_e, block_src, block_rows, block_first, block_slot, block_next, xs, w_gu,
      b_gu.reshape(depth, n_exp, 1, f2), w_dn, b_dn.reshape(depth, n_exp, 1, d))


def _combine_body(goff_ref, loff_ref, cp_ref, ltot_ref, ys_hbm, x_ref, lp_ref, mod_ref, lng_ref, lnb_ref, o_ref,
                  yl_ref, sem, *, n_exp, alpha, tiles_per_step):
    j = pl.program_id(0)
    tl = x_ref.shape[0] // tiles_per_step
    n_tiles = pl.num_programs(0) * tiles_per_step
    lr, hd = yl_ref.shape[1:]

    def fetch(tile):
        buf = yl_ref.at[tile % 2]

        def per_expert(e, carry):
            seg = tile * n_exp + e
            _segment_copy(ys_hbm, goff_ref[seg], buf, loff_ref[seg], cp_ref[seg], sem.at[tile % 2])
            return carry
        lax.fori_loop(0, n_exp, per_expert, 0)

        def zero_rows(r, carry):
            buf[pl.ds(pl.multiple_of(r * SUBLANES, SUBLANES), SUBLANES), :] = jnp.zeros((SUBLANES, hd), U32)
            return carry
        lax.fori_loop(ltot_ref[tile] // SUBLANES, lr // SUBLANES, zero_rows, 0)

    @pl.when(j == 0)
    def _():
        fetch(j)

    liota = lax.broadcasted_iota(I32, (tl, lr), 1)
    for t in range(tiles_per_step):
        tile = j * tiles_per_step + t
        tok = slice(t * tl, (t + 1) * tl)

        @pl.when(tile + 1 < n_tiles)
        def _():
            fetch(tile + 1)

        pick = _one_hot_hits(liota, lambda k: lp_ref[tok, k:k + 1]).astype(F32).astype(BF16)
        _wait_rows(ys_hbm, ltot_ref[tile], sem.at[tile % 2])
        y_lo, y_hi = _unpack_halves(yl_ref[tile % 2])
        y = jnp.concatenate([_dot(pick, y_lo), _dot(pick, y_hi)], axis=1)
        o_ref[tok, :] = _layer_norm(alpha * x_ref[tok, :] + mod_ref[5:6, :] * y, lng_ref[...], lnb_ref[...])


def _combine(ys, lp_t, goff, loff, cp, ltot, x1, mods_l, ln_g, ln_b, alpha, seq_len):
    t, d = x1.shape
    tl = min(MOE_TOKENS, t)
    n_exp = goff.shape[0] // (t // tl)
    lr = TOP_K * tl + n_exp * SUBLANES
    tps = _tiles_per_step(seq_len, tl)
    ts = tl * tps
    per_seq = seq_len // ts
    vec = pl.BlockSpec((1, d), lambda j, *_: (0, 0))
    grid_spec = pltpu.PrefetchScalarGridSpec(
        num_scalar_prefetch=4,
        grid=(t // ts,),
        in_specs=[
            pl.BlockSpec(memory_space=pl.ANY),
            pl.BlockSpec((ts, d), lambda j, *_: (j, 0)),
            pl.BlockSpec((ts, TOP_K), lambda j, *_: (j, 0)),
            pl.BlockSpec((None, 6, d), lambda j, *_: (j // per_seq, 0, 0)),
            vec, vec,
        ],
        out_specs=pl.BlockSpec((ts, d), lambda j, *_: (j, 0)),
        scratch_shapes=[pltpu.VMEM((2, lr, d // 2), U32), pltpu.SemaphoreType.DMA((2,))],
    )
    return pl.pallas_call(
        functools.partial(_combine_body, n_exp=n_exp, alpha=alpha, tiles_per_step=tps),
        grid_spec=grid_spec,
        out_shape=jax.ShapeDtypeStruct((t, d), F32),
        compiler_params=_params("arbitrary"),
        name="combine",
    )(goff, loff, cp, ltot, ys, x1, lp_t, mods_l, ln_g.reshape(1, d), ln_b.reshape(1, d))


def _moe_layer(x1, mods_l, w_r, b_r, w_gu, b_gu, w_dn, b_dn, layer, ln_g, ln_b, alpha, seq_len):
    t, d = x1.shape
    n_exp = w_r.shape[1]
    tm = MOE_TILE
    tl = min(MOE_TOKENS, t)
    n_tiles = t // tl
    idx, gates, rank, cnt = _router(x1, mods_l, seq_len, w_r, b_r)
    cp = (cnt[:, :, 0] + SUBLANES - 1) // SUBLANES * SUBLANES
    tot = jnp.sum(cp, axis=0)
    padded = (tot + tm - 1) // tm * tm
    pend = jnp.cumsum(padded)
    pstart = pend - padded
    goff = pstart[None, :] + jnp.cumsum(cp, axis=0) - cp
    loff = jnp.cumsum(cp, axis=1) - cp
    ltot = jnp.sum(cp, axis=1).astype(I32)
    is_e = idx[..., None] == jnp.arange(n_exp, dtype=I32)
    lp = jnp.sum(jnp.where(is_e, jnp.repeat(loff, tl, axis=0)[None], 0), axis=-1) + rank
    n_blocks = -(-(t * TOP_K + n_tiles * n_exp * (SUBLANES - 1)) // tm) + n_exp
    n_valid = pend[-1] // tm
    blk = jnp.arange(n_blocks, dtype=I32)
    block_src = jnp.minimum(blk, n_valid - 1)
    block_e = jnp.sum(pend[None, :] <= (block_src * tm)[:, None], axis=1).astype(I32)
    used = padded > 0
    experts = jnp.arange(n_exp, dtype=I32)
    next_used = jnp.flip(lax.cummin(jnp.flip(jnp.where(used, experts, n_exp))))
    next_used = jnp.concatenate([next_used[1:], jnp.full((1,), n_exp, I32)])
    next_used = jnp.where(next_used < n_exp, next_used, -1)
    slot_of = (jnp.cumsum(used.astype(I32)) - 1) % 2
    of_block = lambda table: jnp.sum(jnp.where(block_e[:, None] == experts[None, :], table[None, :], 0), axis=1)
    block_first = ((blk * tm == of_block(pstart)) & (blk < n_valid)).astype(I32)
    block_slot = of_block(slot_of).astype(I32)
    block_next = of_block(next_used).astype(I32)
    block_rows = jnp.where(blk < n_valid, jnp.clip(of_block(pstart + tot) - blk * tm, 0, tm), 0).astype(I32)
    n_rows = n_blocks * tm
    pad_start = jnp.concatenate([pstart + tot, pend[-1:]]).astype(I32)
    pad_cnt = jnp.concatenate([padded - tot, n_rows - pend[-1:]]).astype(I32)
    flat = lambda a: a.reshape(-1).astype(I32)
    seg = (flat(goff), flat(loff), flat(cp), ltot)
    xs = _dispatch(x1, mods_l, seq_len, lp, gates, *seg, pad_start, pad_cnt, n_rows)
    ys = _experts(xs, block_e, block_src.astype(I32), block_rows, block_first, block_slot, block_next,
                  w_gu, b_gu, w_dn, b_dn, layer)
    return _combine(ys, lp.T, *seg, x1, mods_l, ln_g, ln_b, alpha, seq_len)


def kernel(x, c, ada_w, ada_b, post_ln_g, post_ln_b, conv_w_pw1, conv_b_pw1, conv_w_dw, conv_b_dw, conv_ln_g, conv_ln_b, conv_w_pw2, conv_b_pw2, w_kv, attn_w_q, attn_lambda, attn_subln_g, attn_w_o, rel_bias_table, router_w, router_b, expert_w_gate_up, expert_b_gate_up, expert_w_down, expert_b_down):
    bsz, s, d = x.shape
    depth = ada_w.shape[0]
    n_a = depth // 2
    alpha = (2 * depth) ** 0.25
    mods = _ada(c, ada_w, ada_b).reshape(depth, bsz, 6, d)
    q = k = vt = None
    for l in range(depth):
        mods_l = mods[l]
        if l < n_a:
            x1 = _conv(x, mods_l, conv_w_pw1[l], conv_b_pw1[l], conv_w_dw[l], conv_b_dw[l], conv_ln_g[l],
                       conv_ln_b[l], conv_w_pw2[l], conv_b_pw2[l], post_ln_g[l, 0], post_ln_b[l, 0], alpha)
        else:
            j = l - n_a
            if j == 0:
                q, k, vt = _qkv(x, mods_l, w_kv, attn_w_q[j])
            else:
                q = _qkv(x, mods_l, w_kv, attn_w_q[j])[0]
            lambda_init = 0.8 - 0.6 * math.exp(-0.3 * l)
            ot = _attn(q, k, vt, attn_lambda[j], attn_subln_g[j], rel_bias_table, lambda_init)
            x1 = _attn_out(ot, x, mods_l, attn_w_o[j], post_ln_g[l, 0], post_ln_b[l, 0], alpha)
        x = _moe_layer(x1.reshape(bsz * s, d), mods_l, router_w[l], router_b[l],
                       expert_w_gate_up, expert_b_gate_up, expert_w_down, expert_b_down, l,
                       post_ln_g[l, 1], post_ln_b[l, 1], alpha, s).reshape(bsz, s, d)
    return x
```

```python
import functools
import math

import jax
import jax.numpy as jnp
from jax import lax
from jax.experimental import pallas as pl
from jax.experimental.pallas import tpu as pltpu

F32 = jnp.float32
BF16 = jnp.bfloat16
I32 = jnp.int32
U32 = jnp.uint32
HIGHEST = lax.Precision.HIGHEST

CHUNK = 64
CONV_WIDTH = 31
HEAD_DIM = 64
REL_BUCKETS = 32
REL_MAX_DIST = 128
TOP_K = 4
SWIGLU_LIMIT = 7.0
SWIGLU_ALPHA = 1.702
LN_EPS = 1e-5
MASK_VALUE = -1e30
LOG2_E = math.log2(math.e)

SUBLANES = 8
LANES = 128
VMEM_LIMIT_BYTES = 56 * 1024 * 1024

ADA_TN = 1024
SEQ_TILE = 512
PROJ_TILE = 512
CONV_HALO = 32
CONV_ROWS = 256
CONV_COLS = 128
MOE_TOKENS = 256
ROUTER_TOKENS = 2048
MOE_TILES_PER_STEP = 4
MOE_TILE = 512
Q_BLOCK = 128
EXPERT_ROW_EIGHTHS = (8, 6, 4, 2)
GLU_LOOKAHEAD = 1
SCORE_LOOKAHEAD = 4
CAST_ROWS = 128


def _params(*sem):
    return pltpu.CompilerParams(dimension_semantics=sem, vmem_limit_bytes=VMEM_LIMIT_BYTES)


def _layer_norm(x, g, b):
    mu = jnp.mean(x, axis=-1, keepdims=True)
    xc = x - mu
    var = jnp.mean(xc * xc, axis=-1, keepdims=True)
    return xc * lax.rsqrt(var + LN_EPS) * g + b


def _dot(a, b):
    return jnp.dot(a, b, preferred_element_type=F32)


def _ada_body(c_ref, w_ref, b_ref, o_ref):
    c = c_ref[...]
    cond = c * jax.nn.sigmoid(c)
    o_ref[...] = jnp.dot(cond, w_ref[...], preferred_element_type=F32, precision=HIGHEST) + b_ref[...]


def _ada(c, ada_w, ada_b):
    depth, d, n = ada_w.shape
    bsz = c.shape[0]
    tn = min(ADA_TN, n)
    return pl.pallas_call(
        _ada_body,
        grid=(depth, n // tn),
        in_specs=[
            pl.BlockSpec((bsz, d), lambda l, j: (0, 0)),
            pl.BlockSpec((None, d, tn), lambda l, j: (l, 0, j)),
            pl.BlockSpec((None, 1, tn), lambda l, j: (l, 0, j)),
        ],
        out_specs=pl.BlockSpec((None, bsz, tn), lambda l, j: (l, 0, j)),
        out_shape=jax.ShapeDtypeStruct((depth, bsz, n), F32),
        compiler_params=_params("parallel", "parallel"),
        name="ada",
    )(c, ada_w, ada_b.reshape(depth, 1, n))


def _residual_epilogue(x, y, mod_ref, lng_ref, lnb_ref, alpha, x1_ref):
    x1_ref[...] = _layer_norm(alpha * x + mod_ref[2:3, :] * y, lng_ref[...], lnb_ref[...])


def _moe_input(x1_ref, mod_ref, rows=slice(None)):
    return x1_ref[rows, :] * (1.0 + mod_ref[4:5, :]) + mod_ref[3:4, :]


def _conv_body(x_ref, halo_ref, mod_ref, w1_ref, b1_ref, wdw_ref, bdw_ref, cg_ref, cb_ref, w2_ref, b2_ref,
               lng_ref, lnb_ref, x1_ref, win_ref, v_ref, *, alpha):
    ts, d = x_ref.shape
    i = pl.program_id(1)
    xw = jnp.concatenate([halo_ref[...], x_ref[...]], axis=0)
    h = (xw * (1.0 + mod_ref[1:2, :]) + mod_ref[0:1, :]).astype(BF16)
    rows = min(CONV_ROWS, ts)
    cols = min(CONV_COLS, d)

    def glu(c0):
        a = _dot(h, w1_ref[:, c0:c0 + cols]) + b1_ref[:, c0:c0 + cols]
        g = _dot(h, w1_ref[:, d + c0:d + c0 + cols]) + b1_ref[:, d + c0:d + c0 + cols]
        return a * jax.nn.sigmoid(g)

    off = CONV_HALO - (CONV_WIDTH - 1)
    n_shift = ts + CONV_HALO - SUBLANES
    in_seq = (lax.broadcasted_iota(I32, (ts + CONV_HALO, cols), 0) >= CONV_HALO) | (i > 0)
    starts = list(range(0, d, cols))
    ahead = [glu(c) for c in starts[:GLU_LOOKAHEAD]]
    for n, c0 in enumerate(starts):
        chan = slice(c0, c0 + cols)
        win_ref[0, :, chan] = jnp.where(in_seq, ahead.pop(0), 0.0)
        if n + GLU_LOOKAHEAD < len(starts):
            ahead.append(glu(starts[n + GLU_LOOKAHEAD]))
        for b in range(1, SUBLANES):
            win_ref[b, 0:n_shift, chan] = win_ref[0, b:b + n_shift, chan]
        for r0 in range(0, ts, rows):
            acc = jnp.zeros((rows, cols), F32)
            for j in range(CONV_WIDTH):
                a, b = divmod(off + j, SUBLANES)
                r = r0 + a * SUBLANES
                acc = acc + wdw_ref[j:j + 1, c0:c0 + cols] * win_ref[b, r:r + rows, c0:c0 + cols]
            v_ref[r0:r0 + rows, c0:c0 + cols] = acc
    v = _layer_norm(v_ref[...] + bdw_ref[...], cg_ref[...], cb_ref[...])
    v = (v * jax.nn.sigmoid(v)).astype(BF16)
    y = _dot(v, w2_ref[...]) + b2_ref[...]
    _residual_epilogue(x_ref[...], y, mod_ref, lng_ref, lnb_ref, alpha, x1_ref)


def _conv(x, mods_l, w_pw1, b_pw1, w_dw, b_dw, cln_g, cln_b, w_pw2, b_pw2, ln_g, ln_b, alpha):
    bsz, s, d = x.shape
    ts = min(SEQ_TILE, s)
    hb = ts // CONV_HALO
    row = lambda a: a.reshape(1, d)
    tile = pl.BlockSpec((None, ts, d), lambda b, i: (b, i, 0))
    vec = pl.BlockSpec((1, d), lambda b, i: (0, 0))
    return pl.pallas_call(
        functools.partial(_conv_body, alpha=alpha),
        grid=(bsz, s // ts),
        in_specs=[
            tile,
            pl.BlockSpec((None, CONV_HALO, d), lambda b, i: (b, jnp.maximum(i * hb - 1, 0), 0)),
            pl.BlockSpec((None, 6, d), lambda b, i: (b, 0, 0)),
            pl.BlockSpec((d, 2 * d), lambda b, i: (0, 0)),
            pl.BlockSpec((1, 2 * d), lambda b, i: (0, 0)),
            pl.BlockSpec((CONV_WIDTH, d), lambda b, i: (0, 0)),
            vec, vec, vec,
            pl.BlockSpec((d, d), lambda b, i: (0, 0)),
            vec, vec, vec,
        ],
        out_specs=tile,
        out_shape=jax.ShapeDtypeStruct((bsz, s, d), F32),
        scratch_shapes=[pltpu.VMEM((SUBLANES, ts + CONV_HALO, d), F32), pltpu.VMEM((ts, d), F32)],
        compiler_params=_params("parallel", "parallel"),
        name="conv",
    )(x, x, mods_l, w_pw1.astype(BF16), b_pw1.reshape(1, 2 * d), w_dw, row(b_dw), row(cln_g), row(cln_b),
      w_pw2.astype(BF16), row(b_pw2), row(ln_g), row(ln_b))


def _qkv_body(x_ref, mod_ref, wkv_ref, wq_ref, q_ref, k_ref, vt_ref):
    d = x_ref.shape[-1]
    x = x_ref[...]
    kv = _dot(x.astype(BF16), wkv_ref[...])
    k_ref[...] = kv[:, :d].astype(BF16)
    hd2 = 2 * HEAD_DIM
    for c0 in range(0, d, hd2):
        vt_ref[c0:c0 + hd2, :] = kv[:, d + c0:d + c0 + hd2].T.astype(BF16)
    h = (x * (1.0 + mod_ref[1:2, :]) + mod_ref[0:1, :]).astype(BF16)
    q_ref[...] = (_dot(h, wq_ref[...]) * (HEAD_DIM ** -0.5 * LOG2_E)).astype(BF16)


def _qkv(x, mods_l, w_kv, w_q):
    bsz, s, d = x.shape
    ts = min(PROJ_TILE, s)
    tile = pl.BlockSpec((None, ts, d), lambda b, i: (b, i, 0))
    return pl.pallas_call(
        _qkv_body,
        grid=(bsz, s // ts),
        in_specs=[
            tile,
            pl.BlockSpec((None, 6, d), lambda b, i: (b, 0, 0)),
            pl.BlockSpec((d, 2 * d), lambda b, i: (0, 0)),
            pl.BlockSpec((d, d), lambda b, i: (0, 0)),
        ],
        out_specs=[tile, tile, pl.BlockSpec((None, d, ts), lambda b, i: (b, 0, i))],
        out_shape=[jax.ShapeDtypeStruct((bsz, s, d), BF16)] * 2 + [jax.ShapeDtypeStruct((bsz, d, s), BF16)],
        compiler_params=_params("parallel", "parallel"),
        name="qkv",
    )(x, mods_l, w_kv.astype(BF16), w_q.astype(BF16))


def _t5_bucket(rel):
    nb = REL_BUCKETS // 2
    ret = jnp.where(rel > 0, nb, 0)
    n = jnp.abs(rel)
    max_exact = nb // 2
    large = max_exact + (jnp.log(jnp.maximum(n, 1).astype(F32) / max_exact)
                         / math.log(REL_MAX_DIST / max_exact) * (nb - max_exact)).astype(I32)
    large = jnp.minimum(large, nb - 1)
    return ret + jnp.where(n < max_exact, n, large)


def _saturation_distance():
    nb = REL_BUCKETS // 2
    max_exact = nb // 2
    n = max_exact
    while max_exact + math.log(n / max_exact) / math.log(REL_MAX_DIST / max_exact) * (nb - max_exact) < nb - 1 + 1e-3:
        n += 1
    return n


def _bucket_strip(s):
    r = jnp.arange(Q_BLOCK, dtype=I32)[:, None]
    kp = jnp.arange(s, dtype=I32)[None, :] - (s - Q_BLOCK)
    bucket = _t5_bucket(kp - r)
    visible = jnp.floor_divide(kp, CHUNK) <= (r // CHUNK)
    return jnp.where(visible, bucket, REL_BUCKETS)


def _attn_body(tab_ref, q_ref, k_ref, vt_ref, bkt_ref, lam_ref, sg_ref, ot_ref, bias_ref, *, lambda_init, n_heads):
    h = pl.program_id(0)
    b = pl.program_id(1)
    s = q_ref.shape[0]

    near = bias_ref.shape[0]

    @pl.when(b == 0)
    def _():
        bk = bkt_ref[...]
        acc = jnp.full(bk.shape, MASK_VALUE, F32)
        for r in range(REL_BUCKETS):
            acc = jnp.where(bk == r, tab_ref[r * n_heads + h] * LOG2_E, acc)
        bias_ref[:, 0:Q_BLOCK] = acc
        bias_ref[:, Q_BLOCK:] = acc

    far_bias = tab_ref[(REL_BUCKETS // 2 - 1) * n_heads + h] * LOG2_E

    lp = lam_ref[...]
    lam = (jnp.exp(jnp.sum(lp[0:1, :] * lp[1:2, :], axis=-1, keepdims=True))
           - jnp.exp(jnp.sum(lp[2:3, :] * lp[3:4, :], axis=-1, keepdims=True)) + lambda_init)
    lane = lax.broadcasted_iota(I32, (Q_BLOCK, 2 * HEAD_DIM), 1)
    nt = (((1,), (1,)), ((), ()))

    def scores(i):
        n_keys = (i + 1) * Q_BLOCK
        q = q_ref[i * Q_BLOCK:(i + 1) * Q_BLOCK, :]
        qq = jnp.concatenate([jnp.where(lane < HEAD_DIM, q, jnp.zeros_like(q)),
                              jnp.where(lane >= HEAD_DIM, q, jnp.zeros_like(q))], axis=0)
        return lax.dot_general(k_ref[0:n_keys, :], qq, nt, preferred_element_type=F32)

    def values(i, p, denom):
        pv = _dot(vt_ref[:, 0:(i + 1) * Q_BLOCK], p)
        o = pv[:, :Q_BLOCK] * (1.0 / denom[:, :Q_BLOCK]) - pv[:, Q_BLOCK:] * (lam / denom[:, Q_BLOCK:])
        o = o * lax.rsqrt(jnp.mean(o * o, axis=0, keepdims=True) + LN_EPS) * sg_ref[...]
        ot_ref[:, i * Q_BLOCK:(i + 1) * Q_BLOCK] = (o * (1.0 - lambda_init)).astype(BF16)

    n_blocks = s // Q_BLOCK
    ahead = [scores(i) for i in range(min(SCORE_LOOKAHEAD, n_blocks))]
    for i in range(n_blocks):
        raw = ahead.pop(0)
        if i + SCORE_LOOKAHEAD < n_blocks:
            ahead.append(scores(i + SCORE_LOOKAHEAD))
        n_keys = (i + 1) * Q_BLOCK
        n_near = min(near, n_keys)
        n_far = n_keys - n_near
        sc = raw[n_far:] + bias_ref[near - n_near:, :]
        top = jnp.max(sc, axis=0, keepdims=True)
        if n_far:
            top = jnp.maximum(top, jnp.max(raw[:n_far], axis=0, keepdims=True) + far_bias)
            p = jnp.concatenate([jnp.exp2(raw[:n_far] - (top - far_bias)), jnp.exp2(sc - top)], axis=0)
        else:
            p = jnp.exp2(sc - top)
        values(i, p.astype(BF16), jnp.sum(p, axis=0, keepdims=True))


def _attn(q, k, vt, lam_p, subln_g, rel_table, lambda_init):
    bsz, s, d = q.shape
    hd2 = 2 * HEAD_DIM
    n_heads = d // hd2
    assert REL_MAX_DIST >= CHUNK and _saturation_distance() <= REL_MAX_DIST
    near = min(Q_BLOCK + REL_MAX_DIST, s)
    head = pl.BlockSpec((None, s, hd2), lambda h, b, tab: (b, 0, h))
    head_t = pl.BlockSpec((None, hd2, s), lambda h, b, tab: (b, h, 0))
    grid_spec = pltpu.PrefetchScalarGridSpec(
        num_scalar_prefetch=1,
        grid=(n_heads, bsz),
        in_specs=[
            head, head, head_t,
            pl.BlockSpec((near, Q_BLOCK), lambda h, b, tab: (0, 0)),
            pl.BlockSpec((4, HEAD_DIM), lambda h, b, tab: (0, 0)),
            pl.BlockSpec((hd2, 1), lambda h, b, tab: (0, 0)),
        ],
        out_specs=head_t,
        scratch_shapes=[pltpu.VMEM((near, 2 * Q_BLOCK), F32)],
    )
    return pl.pallas_call(
        functools.partial(_attn_body, lambda_init=lambda_init, n_heads=n_heads),
        grid_spec=grid_spec,
        out_shape=jax.ShapeDtypeStruct((bsz, d, s), BF16),
        compiler_params=_params("arbitrary", "arbitrary"),
        name="attn",
    )(rel_table.reshape(-1), q, k, vt, _bucket_strip(s).T[s - near:], lam_p, subln_g.reshape(hd2, 1))


def _attn_out_body(ot_ref, x_ref, mod_ref, wo_ref, lng_ref, lnb_ref, x1_ref, *, alpha):
    y = lax.dot_general(ot_ref[...], wo_ref[...], (((0,), (0,)), ((), ())), preferred_element_type=F32)
    _residual_epilogue(x_ref[...], y, mod_ref, lng_ref, lnb_ref, alpha, x1_ref)


def _attn_out(o, x, mods_l, w_o, ln_g, ln_b, alpha):
    bsz, s, d = x.shape
    ts = min(PROJ_TILE, s)
    tile = pl.BlockSpec((None, ts, d), lambda b, i: (b, i, 0))
    vec = pl.BlockSpec((1, d), lambda b, i: (0, 0))
    return pl.pallas_call(
        functools.partial(_attn_out_body, alpha=alpha),
        grid=(bsz, s // ts),
        in_specs=[pl.BlockSpec((None, d, ts), lambda b, i: (b, 0, i)), tile,
                  pl.BlockSpec((None, 6, d), lambda b, i: (b, 0, 0)),
                  pl.BlockSpec((d, d), lambda b, i: (0, 0)), vec, vec],
        out_specs=tile,
        out_shape=jax.ShapeDtypeStruct((bsz, s, d), F32),
        compiler_params=_params("parallel", "parallel"),
        name="attn_out",
    )(o, x, mods_l, w_o.astype(BF16), ln_g.reshape(1, d), ln_b.reshape(1, d))


def _router_body(x1_ref, mod_ref, whi_ref, wlo_ref, b_ref, tri_ref, idx_ref, gate_ref, rank_ref, cnt_ref):
    h = _moe_input(x1_ref, mod_ref)
    h_hi = h.astype(BF16)
    h_lo = (h - h_hi.astype(F32)).astype(BF16)
    nt = (((1,), (1,)), ((), ()))
    logits = (lax.dot_general(whi_ref[...], h_hi, nt, preferred_element_type=F32)
              + (lax.dot_general(whi_ref[...], h_lo, nt, preferred_element_type=F32)
                 + lax.dot_general(wlo_ref[...], h_hi, nt, preferred_element_type=F32))) + b_ref[...]
    n_exp, tr = logits.shape
    eio = lax.broadcasted_iota(I32, (n_exp, tr), 0)
    work = logits
    vals, idxs = [], []
    for _ in range(TOP_K):
        m = jnp.max(work, axis=0, keepdims=True)
        am = jnp.min(jnp.where(work == m, eio, n_exp), axis=0, keepdims=True)
        vals.append(m)
        idxs.append(am)
        work = jnp.where(eio == am, -jnp.inf, work)
    ex = [jnp.exp(v - vals[0]) for v in vals]
    den = ex[0] + ex[1] + ex[2] + ex[3]
    onehot = jnp.zeros((n_exp, tr), F32)
    for k in range(TOP_K):
        onehot = onehot + (eio == idxs[k]).astype(F32)
    tl = tri_ref.shape[0]
    tiles = [onehot[:, t0:t0 + tl] for t0 in range(0, tr, tl)]
    before = jnp.concatenate([_dot(oh.astype(BF16), tri_ref[...]) for oh in tiles], axis=1)
    for k in range(TOP_K):
        idx_ref[k:k + 1, :] = idxs[k]
        gate_ref[k:k + 1, :] = ex[k] / den
        rank_ref[k:k + 1, :] = jnp.sum(jnp.where(eio == idxs[k], before, 0.0), axis=0, keepdims=True).astype(I32)
    for j, oh in enumerate(tiles):
        cnt_ref[j] = jnp.sum(oh, axis=1, keepdims=True).astype(I32)


def _router(x1, mods_l, seq_len, w_r, b_r):
    t, d = x1.shape
    n_exp = w_r.shape[1]
    tl = min(MOE_TOKENS, t)
    tr = min(ROUTER_TOKENS, seq_len)
    per_seq = seq_len // tr
    wt = w_r.T
    wt_hi = wt.astype(BF16)
    wt_lo = (wt - wt_hi.astype(F32)).astype(BF16)
    pos = jnp.arange(tl, dtype=I32)
    tri = (pos[:, None] < pos[None, :]).astype(BF16)
    tok = pl.BlockSpec((TOP_K, tr), lambda i: (0, i))
    return pl.pallas_call(
        _router_body,
        grid=(t // tr,),
        in_specs=[
            pl.BlockSpec((tr, d), lambda i: (i, 0)),
            pl.BlockSpec((None, 6, d), lambda i: (i // per_seq, 0, 0)),
            pl.BlockSpec((n_exp, d), lambda i: (0, 0)),
            pl.BlockSpec((n_exp, d), lambda i: (0, 0)),
            pl.BlockSpec((n_exp, 1), lambda i: (0, 0)),
            pl.BlockSpec((tl, tl), lambda i: (0, 0)),
        ],
        out_specs=[tok, tok, tok, pl.BlockSpec((tr // tl, n_exp, 1), lambda i: (i, 0, 0))],
        out_shape=[jax.ShapeDtypeStruct((TOP_K, t), I32), jax.ShapeDtypeStruct((TOP_K, t), F32),
                   jax.ShapeDtypeStruct((TOP_K, t), I32), jax.ShapeDtypeStruct((t // tl, n_exp, 1), I32)],
        compiler_params=_params("parallel"),
        name="router",
    )(x1, mods_l, wt_hi, wt_lo, b_r.reshape(n_exp, 1), tri)


def _rows(ref, start, cnt):
    aligned = lambda v: v if isinstance(v, int) else pl.multiple_of(v, SUBLANES)
    return ref.at[pl.ds(aligned(start), aligned(cnt))]


def _segment_copy(src, src_row, dst, dst_row, cnt, sem):
    @pl.when(cnt > 0)
    def _():
        pltpu.make_async_copy(_rows(src, src_row, cnt), _rows(dst, dst_row, cnt), sem).start()


def _wait_rows(ref, cnt, sem):
    @pl.when(cnt > 0)
    def _():
        pltpu.make_async_copy(_rows(ref, 0, cnt), _rows(ref, 0, cnt), sem).wait()


def _zero_fill(zero_ref, xs_hbm, start, cnt, sem):
    zr = zero_ref.shape[0]
    n_full = cnt // zr

    def full(r, c):
        _segment_copy(zero_ref, 0, xs_hbm, start + r * zr, zr, sem)
        return c

    lax.fori_loop(0, n_full, full, 0)
    _segment_copy(zero_ref, 0, xs_hbm, start + n_full * zr, cnt - n_full * zr, sem)


def _pack_halves(x):
    hd = x.shape[1] // 2
    lo = lax.bitcast_convert_type(x[:, :hd], U32)
    hi = lax.bitcast_convert_type(x[:, hd:], U32)
    return (lo >> 16) | (hi & jnp.uint32(0xFFFF0000))


def _unpack_halves(words):
    lo = lax.bitcast_convert_type(words << 16, F32).astype(BF16)
    hi = lax.bitcast_convert_type(words & jnp.uint32(0xFFFF0000), F32).astype(BF16)
    return lo, hi


def _tiles_per_step(seq_len, tl):
    return MOE_TILES_PER_STEP if seq_len % (tl * MOE_TILES_PER_STEP) == 0 else 1


def _one_hot_hits(iota, pos_of):
    hit = iota == pos_of(0)
    for k in range(1, TOP_K):
        hit = hit | (iota == pos_of(k))
    return hit


def _dispatch_body(goff_ref, loff_ref, cp_ref, ltot_ref, pad_start_ref, pad_cnt_ref, x1_ref, mod_ref, lp_ref,
                   gate_ref, xs_hbm, xl_ref, zero_ref, sem, zsem, *, n_exp, tiles_per_step):
    j = pl.program_id(0)
    d = x1_ref.shape[1]
    tl = x1_ref.shape[0] // tiles_per_step
    lr = xl_ref.shape[1]

    @pl.when(j == 0)
    def _():
        zero_ref[...] = jnp.zeros_like(zero_ref)

        def per_range(e, carry):
            _zero_fill(zero_ref, xs_hbm, pad_start_ref[e], pad_cnt_ref[e], zsem)
            return carry
        lax.fori_loop(0, pad_start_ref.shape[0], per_range, 0)

    def start_segments(tile):
        def per_expert(e, carry):
            seg = tile * n_exp + e
            _segment_copy(xl_ref.at[tile % 2], loff_ref[seg], xs_hbm, goff_ref[seg], cp_ref[seg], sem.at[tile % 2])
            return carry
        lax.fori_loop(0, n_exp, per_expert, 0)

    def wait_segments(tile):
        _wait_rows(xs_hbm, ltot_ref[tile], sem.at[tile % 2])

    riota = lax.broadcasted_iota(I32, (lr, tl), 0)
    for t in range(tiles_per_step):
        tile = j * tiles_per_step + t
        tok = slice(t * tl, (t + 1) * tl)

        @pl.when(tile >= 2)
        def _():
            wait_segments(tile - 2)

        buf = xl_ref.at[tile % 2]
        perm = _one_hot_hits(riota, lambda k: lp_ref[k:k + 1, tok]).astype(F32).astype(BF16)
        rows = _dot(perm, _moe_input(x1_ref, mod_ref, tok).astype(BF16))
        gsel = jnp.zeros((lr, tl), F32)
        for k in range(TOP_K):
            gsel = gsel + jnp.where(riota == lp_ref[k:k + 1, tok], gate_ref[k:k + 1, tok], 0.0)
        rowg = jnp.broadcast_to(jnp.sum(gsel, axis=1, keepdims=True), (lr, LANES))
        buf[:, d // 2:] = lax.bitcast_convert_type(rowg, U32)
        buf[:, 0:d // 2] = _pack_halves(rows)
        start_segments(tile)

    @pl.when(j == pl.num_programs(0) - 1)
    def _():
        last = pl.num_programs(0) * tiles_per_step - 1

        @pl.when(last >= 1)
        def _():
            wait_segments(last - 1)
        wait_segments(last)
        n_zero = lax.fori_loop(0, pad_cnt_ref.shape[0], lambda e, acc: acc + pad_cnt_ref[e], jnp.int32(0))
        _wait_rows(xs_hbm, n_zero, zsem)


def _dispatch(x1, mods_l, seq_len, lp, gates, goff, loff, cp, ltot, pad_start, pad_cnt, n_rows):
    t, d = x1.shape
    tl = min(MOE_TOKENS, t)
    n_exp = goff.shape[0] // (t // tl)
    lr = TOP_K * tl + n_exp * SUBLANES
    tps = _tiles_per_step(seq_len, tl)
    ts = tl * tps
    per_seq = seq_len // ts
    tok = pl.BlockSpec((TOP_K, ts), lambda j, *_: (0, j))
    grid_spec = pltpu.PrefetchScalarGridSpec(
        num_scalar_prefetch=6,
        grid=(t // ts,),
        in_specs=[pl.BlockSpec((ts, d), lambda j, *_: (j, 0)),
                  pl.BlockSpec((None, 6, d), lambda j, *_: (j // per_seq, 0, 0)), tok, tok],
        out_specs=pl.BlockSpec(memory_space=pl.ANY),
        scratch_shapes=[pltpu.VMEM((2, lr, d // 2 + LANES), U32), pltpu.VMEM((MOE_TILE, d // 2 + LANES), U32),
                        pltpu.SemaphoreType.DMA((2,)), pltpu.SemaphoreType.DMA],
    )
    return pl.pallas_call(
        functools.partial(_dispatch_body, n_exp=n_exp, tiles_per_step=tps),
        grid_spec=grid_spec,
        out_shape=jax.ShapeDtypeStruct((n_rows, d // 2 + LANES), U32),
        compiler_params=_params("arbitrary"),
        name="dispatch",
    )(goff, loff, cp, ltot, pad_start, pad_cnt, x1, mods_l, lp, gates)


def _experts_body(be_ref, bsrc_ref, rows_ref, first_ref, slot_ref, next_ref, xs_ref, wgu_hbm, bgu_ref, wdn_hbm,
                  bdn_ref, ys_ref, wgu_f32, wdn_f32, wgu_bf, wdn_bf, sem_gu, sem_dn, *, layer):
    i = pl.program_id(0)
    e = be_ref[i]
    d, f2 = wgu_bf.shape
    f = f2 // 2

    def fetch(expert, slot, start):
        for hbm, buf, sem in ((wgu_hbm, wgu_f32, sem_gu), (wdn_hbm, wdn_f32, sem_dn)):
            dma = pltpu.make_async_copy(hbm.at[layer, expert], buf.at[slot], sem.at[slot])
            dma.start() if start else dma.wait()

    @pl.when(first_ref[i] == 1)
    def _():
        slot = slot_ref[i]

        @pl.when(i == 0)
        def _():
            fetch(e, slot, True)

        fetch(e, slot, False)

        @pl.when(next_ref[i] >= 0)
        def _():
            fetch(next_ref[i], 1 - slot, True)

        def cast(ref_in, ref_out):
            def step(r, c):
                rows = pl.ds(pl.multiple_of(r * CAST_ROWS, CAST_ROWS), CAST_ROWS)
                ref_out[rows, :] = ref_in[rows, :].astype(BF16)
                return c
            lax.fori_loop(0, ref_in.shape[0] // CAST_ROWS, step, 0)
        cast(wgu_f32.at[slot], wgu_bf)
        cast(wdn_f32.at[slot], wdn_bf)

    hd = d // 2
    tm = xs_ref.shape[0]

    def mlp(n):
        x_lo, x_hi = _unpack_halves(xs_ref[0:n, 0:hd])
        proj = lambda cols: _dot(x_lo, wgu_bf[0:hd, cols]) + _dot(x_hi, wgu_bf[hd:, cols]) + bgu_ref[:, cols]
        gate = jnp.minimum(proj(slice(0, f)), SWIGLU_LIMIT)
        lin = jnp.clip(proj(slice(f, f2)), -SWIGLU_LIMIT, SWIGLU_LIMIT)
        act = (gate * jax.nn.sigmoid(SWIGLU_ALPHA * gate) * (lin + 1.0)).astype(BF16)
        row_gate = lax.bitcast_convert_type(xs_ref[0:n, hd:hd + 1], F32)
        y = (_dot(act, wdn_bf[...]) + bdn_ref[...]) * row_gate
        ys_ref[0:n, :] = _pack_halves(y.astype(BF16).astype(F32))
        if n < tm:
            ys_ref[n:, :] = jnp.zeros((tm - n, hd), U32)

    rows = rows_ref[i]
    sizes = [tm * e // 8 for e in EXPERT_ROW_EIGHTHS]
    for n, smaller in zip(sizes, sizes[1:] + [0]):
        @pl.when((rows > smaller) & (rows <= n))
        def _(n=n):
            mlp(n)

    @pl.when(rows == 0)
    def _():
        ys_ref[...] = jnp.zeros_like(ys_ref)


def _experts(xs, block_e, block_src, block_rows, block_first, block_slot, block_next, w_gu, b_gu, w_dn, b_dn, layer):
    n_rows, dx = xs.shape
    depth, n_exp, d, f2 = w_gu.shape
    f = f2 // 2
    tm = MOE_TILE
    grid_spec = pltpu.PrefetchScalarGridSpec(
        num_scalar_prefetch=6,
        grid=(n_rows // tm,),
        in_specs=[
            pl.BlockSpec((tm, dx), lambda i, be, bs, *_: (bs[i], 0)),
            pl.BlockSpec(memory_space=pl.ANY),
            pl.BlockSpec((None, None, 1, f2), lambda i, be, *_: (layer, be[i], 0, 0)),
            pl.BlockSpec(memory_space=pl.ANY),
            pl.BlockSpec((None, None, 1, d), lambda i, be, *_: (layer, be[i], 0, 0)),
        ],
        out_specs=pl.BlockSpec((tm, d // 2), lambda i, *_: (i, 0)),
        scratch_shapes=[pltpu.VMEM((2, d, f2), F32), pltpu.VMEM((2, f, d), F32),
                        pltpu.VMEM((d, f2), BF16), pltpu.VMEM((f, d), BF16),
                        pltpu.SemaphoreType.DMA((2,)), pltpu.SemaphoreType.DMA((2,))],
    )
    return pl.pallas_call(
        functools.partial(_experts_body, layer=layer),
        grid_spec=grid_spec,
        out_shape=jax.ShapeDtypeStruct((n_rows, d // 2), U32),
        compiler_params=_params("arbitrary"),
        name="experts",
    )(block_e, block_src, block_rows, block_first, block_slot, block_next, xs, w_gu,
      b_gu.reshape(depth, n_exp, 1, f2), w_dn, b_dn.reshape(depth, n_exp, 1, d))


def _combine_body(goff_ref, loff_ref, cp_ref, ltot_ref, ys_hbm, x_ref, lp_ref, mod_ref, lng_ref, lnb_ref, o_ref,
                  yl_ref, sem, *, n_exp, alpha, tiles_per_step):
    j = pl.program_id(0)
    tl = x_ref.shape[0] // tiles_per_step
    n_tiles = pl.num_programs(0) * tiles_per_step
    lr, hd = yl_ref.shape[1:]

    def fetch(tile):
        buf = yl_ref.at[tile % 2]

        def per_expert(e, carry):
            seg = tile * n_exp + e
            _segment_copy(ys_hbm, goff_ref[seg], buf, loff_ref[seg], cp_ref[seg], sem.at[tile % 2])
            return carry
        lax.fori_loop(0, n_exp, per_expert, 0)

        def zero_rows(r, carry):
            buf[pl.ds(pl.multiple_of(r * SUBLANES, SUBLANES), SUBLANES), :] = jnp.zeros((SUBLANES, hd), U32)
            return carry
        lax.fori_loop(ltot_ref[tile] // SUBLANES, lr // SUBLANES, zero_rows, 0)

    @pl.when(j == 0)
    def _():
        fetch(j)

    liota = lax.broadcasted_iota(I32, (tl, lr), 1)
    for t in range(tiles_per_step):
        tile = j * tiles_per_step + t
        tok = slice(t * tl, (t + 1) * tl)

        @pl.when(tile + 1 < n_tiles)
        def _():
            fetch(tile + 1)

        pick = _one_hot_hits(liota, lambda k: lp_ref[tok, k:k + 1]).astype(F32).astype(BF16)
        _wait_rows(ys_hbm, ltot_ref[tile], sem.at[tile % 2])
        y_lo, y_hi = _unpack_halves(yl_ref[tile % 2])
        y = jnp.concatenate([_dot(pick, y_lo), _dot(pick, y_hi)], axis=1)
        o_ref[tok, :] = _layer_norm(alpha * x_ref[tok, :] + mod_ref[5:6, :] * y, lng_ref[...], lnb_ref[...])


def _combine(ys, lp_t, goff, loff, cp, ltot, x1, mods_l, ln_g, ln_b, alpha, seq_len):
    t, d = x1.shape
    tl = min(MOE_TOKENS, t)
    n_exp = goff.shape[0] // (t // tl)
    lr = TOP_K * tl + n_exp * SUBLANES
    tps = _tiles_per_step(seq_len, tl)
    ts = tl * tps
    per_seq = seq_len // ts
    vec = pl.BlockSpec((1, d), lambda j, *_: (0, 0))
    grid_spec = pltpu.PrefetchScalarGridSpec(
        num_scalar_prefetch=4,
        grid=(t // ts,),
        in_specs=[
            pl.BlockSpec(memory_space=pl.ANY),
            pl.BlockSpec((ts, d), lambda j, *_: (j, 0)),
            pl.BlockSpec((ts, TOP_K), lambda j, *_: (j, 0)),
            pl.BlockSpec((None, 6, d), lambda j, *_: (j // per_seq, 0, 0)),
            vec, vec,
        ],
        out_specs=pl.BlockSpec((ts, d), lambda j, *_: (j, 0)),
        scratch_shapes=[pltpu.VMEM((2, lr, d // 2), U32), pltpu.SemaphoreType.DMA((2,))],
    )
    return pl.pallas_call(
        functools.partial(_combine_body, n_exp=n_exp, alpha=alpha, tiles_per_step=tps),
        grid_spec=grid_spec,
        out_shape=jax.ShapeDtypeStruct((t, d), F32),
        compiler_params=_params("arbitrary"),
        name="combine",
    )(goff, loff, cp, ltot, ys, x1, lp_t, mods_l, ln_g.reshape(1, d), ln_b.reshape(1, d))


def _moe_layer(x1, mods_l, w_r, b_r, w_gu, b_gu, w_dn, b_dn, layer, ln_g, ln_b, alpha, seq_len):
    t, d = x1.shape
    n_exp = w_r.shape[1]
    tm = MOE_TILE
    tl = min(MOE_TOKENS, t)
    n_tiles = t // tl
    idx, gates, rank, cnt = _router(x1, mods_l, seq_len, w_r, b_r)
    cp = (cnt[:, :, 0] + SUBLANES - 1) // SUBLANES * SUBLANES
    tot = jnp.sum(cp, axis=0)
    padded = (tot + tm - 1) // tm * tm
    pend = jnp.cumsum(padded)
    pstart = pend - padded
    goff = pstart[None, :] + jnp.cumsum(cp, axis=0) - cp
    loff = jnp.cumsum(cp, axis=1) - cp
    ltot = jnp.sum(cp, axis=1).astype(I32)
    is_e = idx[..., None] == jnp.arange(n_exp, dtype=I32)
    lp = jnp.sum(jnp.where(is_e, jnp.repeat(loff, tl, axis=0)[None], 0), axis=-1) + rank
    n_blocks = -(-(t * TOP_K + n_tiles * n_exp * (SUBLANES - 1)) // tm) + n_exp
    n_valid = pend[-1] // tm
    blk = jnp.arange(n_blocks, dtype=I32)
    block_src = jnp.minimum(blk, n_valid - 1)
    block_e = jnp.sum(pend[None, :] <= (block_src * tm)[:, None], axis=1).astype(I32)
    used = padded > 0
    experts = jnp.arange(n_exp, dtype=I32)
    next_used = jnp.flip(lax.cummin(jnp.flip(jnp.where(used, experts, n_exp))))
    next_used = jnp.concatenate([next_used[1:], jnp.full((1,), n_exp, I32)])
    next_used = jnp.where(next_used < n_exp, next_used, -1)
    slot_of = (jnp.cumsum(used.astype(I32)) - 1) % 2
    of_block = lambda table: jnp.sum(jnp.where(block_e[:, None] == experts[None, :], table[None, :], 0), axis=1)
    block_first = ((blk * tm == of_block(pstart)) & (blk < n_valid)).astype(I32)
    block_slot = of_block(slot_of).astype(I32)
    block_next = of_block(next_used).astype(I32)
    block_rows = jnp.where(blk < n_valid, jnp.clip(of_block(pstart + tot) - blk * tm, 0, tm), 0).astype(I32)
    n_rows = n_blocks * tm
    pad_start = jnp.concatenate([pstart + tot, pend[-1:]]).astype(I32)
    pad_cnt = jnp.concatenate([padded - tot, n_rows - pend[-1:]]).astype(I32)
    flat = lambda a: a.reshape(-1).astype(I32)
    seg = (flat(goff), flat(loff), flat(cp), ltot)
    xs = _dispatch(x1, mods_l, seq_len, lp, gates, *seg, pad_start, pad_cnt, n_rows)
    ys = _experts(xs, block_e, block_src.astype(I32), block_rows, block_first, block_slot, block_next,
                  w_gu, b_gu, w_dn, b_dn, layer)
    return _combine(ys, lp.T, *seg, x1, mods_l, ln_g, ln_b, alpha, seq_len)


def kernel(x, c, ada_w, ada_b, post_ln_g, post_ln_b, conv_w_pw1, conv_b_pw1, conv_w_dw, conv_b_dw, conv_ln_g, conv_ln_b, conv_w_pw2, conv_b_pw2, w_kv, attn_w_q, attn_lambda, attn_subln_g, attn_w_o, rel_bias_table, router_w, router_b, expert_w_gate_up, expert_b_gate_up, expert_w_down, expert_b_down):
    bsz, s, d = x.shape
    depth = ada_w.shape[0]
    n_a = depth // 2
    alpha = (2 * depth) ** 0.25
    mods = _ada(c, ada_w, ada_b).reshape(depth, bsz, 6, d)
    q = k = vt = None
    for l in range(depth):
        mods_l = mods[l]
        if l < n_a:
            x1 = _conv(x, mods_l, conv_w_pw1[l], conv_b_pw1[l], conv_w_dw[l], conv_b_dw[l], conv_ln_g[l],
                       conv_ln_b[l], conv_w_pw2[l], conv_b_pw2[l], post_ln_g[l, 0], post_ln_b[l, 0], alpha)
        else:
            j = l - n_a
            if j == 0:
                q, k, vt = _qkv(x, mods_l, w_kv, attn_w_q[j])
            else:
                q = _qkv(x, mods_l, w_kv, attn_w_q[j])[0]
            lambda_init = 0.8 - 0.6 * math.exp(-0.3 * l)
            ot = _attn(q, k, vt, attn_lambda[j], attn_subln_g[j], rel_bias_table, lambda_init)
            x1 = _attn_out(ot, x, mods_l, attn_w_o[j], post_ln_g[l, 0], post_ln_b[l, 0], alpha)
        x = _moe_layer(x1.reshape(bsz * s, d), mods_l, router_w[l], router_b[l],
                       expert_w_gate_up, expert_b_gate_up, expert_w_down, expert_b_down, l,
                       post_ln_g[l, 1], post_ln_b[l, 1], alpha, s).reshape(bsz, s, d)
    return x
```

```python
import functools
import math

import jax
import jax.numpy as jnp
from jax import lax
from jax.experimental import pallas as pl
from jax.experimental.pallas import tpu as pltpu

F32 = jnp.float32
BF16 = jnp.bfloat16
I32 = jnp.int32
U32 = jnp.uint32
HIGHEST = lax.Precision.HIGHEST

CHUNK = 64
CONV_WIDTH = 31
HEAD_DIM = 64
REL_BUCKETS = 32
REL_MAX_DIST = 128
TOP_K = 4
SWIGLU_LIMIT = 7.0
SWIGLU_ALPHA = 1.702
LN_EPS = 1e-5
MASK_VALUE = -1e30
LOG2_E = math.log2(math.e)

SUBLANES = 8
LANES = 128
VMEM_LIMIT_BYTES = 56 * 1024 * 1024

ADA_TN = 1024
SEQ_TILE = 512
PROJ_TILE = 512
CONV_HALO = 32
CONV_ROWS = 256
CONV_COLS = 128
MOE_TOKENS = 256
ROUTER_TOKENS = 2048
MOE_TILES_PER_STEP = 2
MOE_TILE = 512
Q_BLOCK = 128
EXPERT_ROW_EIGHTHS = (8, 6, 4, 2)
GLU_LOOKAHEAD = 1
SCORE_LOOKAHEAD = 3
CAST_ROWS = 128


def _params(*sem):
    return pltpu.CompilerParams(dimension_semantics=sem, vmem_limit_bytes=VMEM_LIMIT_BYTES)


def _layer_norm(x, g, b):
    mu = jnp.mean(x, axis=-1, keepdims=True)
    xc = x - mu
    var = jnp.mean(xc * xc, axis=-1, keepdims=True)
    return xc * lax.rsqrt(var + LN_EPS) * g + b


def _dot(a, b):
    return jnp.dot(a, b, preferred_element_type=F32)


def _ada_body(c_ref, w_ref, b_ref, o_ref):
    c = c_ref[...]
    cond = c * jax.nn.sigmoid(c)
    o_ref[...] = jnp.dot(cond, w_ref[...], preferred_element_type=F32, precision=HIGHEST) + b_ref[...]


def _ada(c, ada_w, ada_b):
    depth, d, n = ada_w.shape
    bsz = c.shape[0]
    tn = min(ADA_TN, n)
    return pl.pallas_call(
        _ada_body,
        grid=(depth, n // tn),
        in_specs=[
            pl.BlockSpec((bsz, d), lambda l, j: (0, 0)),
            pl.BlockSpec((None, d, tn), lambda l, j: (l, 0, j)),
            pl.BlockSpec((None, 1, tn), lambda l, j: (l, 0, j)),
        ],
        out_specs=pl.BlockSpec((None, bsz, tn), lambda l, j: (l, 0, j)),
        out_shape=jax.ShapeDtypeStruct((depth, bsz, n), F32),
        compiler_params=_params("parallel", "parallel"),
        name="ada",
    )(c, ada_w, ada_b.reshape(depth, 1, n))


def _residual_epilogue(x, y, mod_ref, lng_ref, lnb_ref, alpha, x1_ref):
    x1_ref[...] = _layer_norm(alpha * x + mod_ref[2:3, :] * y, lng_ref[...], lnb_ref[...])


def _moe_input(x1_ref, mod_ref, rows=slice(None)):
    return x1_ref[rows, :] * (1.0 + mod_ref[4:5, :]) + mod_ref[3:4, :]


def _conv_body(x_ref, halo_ref, mod_ref, w1_ref, b1_ref, wdw_ref, bdw_ref, cg_ref, cb_ref, w2_ref, b2_ref,
               lng_ref, lnb_ref, x1_ref, win_ref, v_ref, *, alpha):
    ts, d = x_ref.shape
    i = pl.program_id(1)
    xw = jnp.concatenate([halo_ref[...], x_ref[...]], axis=0)
    h = (xw * (1.0 + mod_ref[1:2, :]) + mod_ref[0:1, :]).astype(BF16)
    rows = min(CONV_ROWS, ts)
    cols = min(CONV_COLS, d)

    def glu(c0):
        a = _dot(h, w1_ref[:, c0:c0 + cols]) + b1_ref[:, c0:c0 + cols]
        g = _dot(h, w1_ref[:, d + c0:d + c0 + cols]) + b1_ref[:, d + c0:d + c0 + cols]
        return a * jax.nn.sigmoid(g)

    off = CONV_HALO - (CONV_WIDTH - 1)
    n_shift = ts + CONV_HALO - SUBLANES
    in_seq = (lax.broadcasted_iota(I32, (ts + CONV_HALO, cols), 0) >= CONV_HALO) | (i > 0)
    starts = list(range(0, d, cols))
    ahead = [glu(c) for c in starts[:GLU_LOOKAHEAD]]
    for n, c0 in enumerate(starts):
        chan = slice(c0, c0 + cols)
        win_ref[0, :, chan] = jnp.where(in_seq, ahead.pop(0), 0.0)
        if n + GLU_LOOKAHEAD < len(starts):
            ahead.append(glu(starts[n + GLU_LOOKAHEAD]))
        for b in range(1, SUBLANES):
            win_ref[b, 0:n_shift, chan] = win_ref[0, b:b + n_shift, chan]
        for r0 in range(0, ts, rows):
            acc = jnp.zeros((rows, cols), F32)
            for j in range(CONV_WIDTH):
                a, b = divmod(off + j, SUBLANES)
                r = r0 + a * SUBLANES
                acc = acc + wdw_ref[j:j + 1, c0:c0 + cols] * win_ref[b, r:r + rows, c0:c0 + cols]
            v_ref[r0:r0 + rows, c0:c0 + cols] = acc
    v = _layer_norm(v_ref[...] + bdw_ref[...], cg_ref[...], cb_ref[...])
    v = (v * jax.nn.sigmoid(v)).astype(BF16)
    y = _dot(v, w2_ref[...]) + b2_ref[...]
    _residual_epilogue(x_ref[...], y, mod_ref, lng_ref, lnb_ref, alpha, x1_ref)


def _conv(x, mods_l, w_pw1, b_pw1, w_dw, b_dw, cln_g, cln_b, w_pw2, b_pw2, ln_g, ln_b, alpha):
    bsz, s, d = x.shape
    ts = min(SEQ_TILE, s)
    hb = ts // CONV_HALO
    row = lambda a: a.reshape(1, d)
    tile = pl.BlockSpec((None, ts, d), lambda b, i: (b, i, 0))
    vec = pl.BlockSpec((1, d), lambda b, i: (0, 0))
    return pl.pallas_call(
        functools.partial(_conv_body, alpha=alpha),
        grid=(bsz, s // ts),
        in_specs=[
            tile,
            pl.BlockSpec((None, CONV_HALO, d), lambda b, i: (b, jnp.maximum(i * hb - 1, 0), 0)),
            pl.BlockSpec((None, 6, d), lambda b, i: (b, 0, 0)),
            pl.BlockSpec((d, 2 * d), lambda b, i: (0, 0)),
            pl.BlockSpec((1, 2 * d), lambda b, i: (0, 0)),
            pl.BlockSpec((CONV_WIDTH, d), lambda b, i: (0, 0)),
            vec, vec, vec,
            pl.BlockSpec((d, d), lambda b, i: (0, 0)),
            vec, vec, vec,
        ],
        out_specs=tile,
        out_shape=jax.ShapeDtypeStruct((bsz, s, d), F32),
        scratch_shapes=[pltpu.VMEM((SUBLANES, ts + CONV_HALO, d), F32), pltpu.VMEM((ts, d), F32)],
        compiler_params=_params("parallel", "parallel"),
        name="conv",
    )(x, x, mods_l, w_pw1.astype(BF16), b_pw1.reshape(1, 2 * d), w_dw, row(b_dw), row(cln_g), row(cln_b),
      w_pw2.astype(BF16), row(b_pw2), row(ln_g), row(ln_b))


def _qkv_body(x_ref, mod_ref, wkv_ref, wq_ref, q_ref, k_ref, vt_ref):
    d = x_ref.shape[-1]
    x = x_ref[...]
    kv = _dot(x.astype(BF16), wkv_ref[...])
    k_ref[...] = kv[:, :d].astype(BF16)
    hd2 = 2 * HEAD_DIM
    for c0 in range(0, d, hd2):
        vt_ref[c0:c0 + hd2, :] = kv[:, d + c0:d + c0 + hd2].T.astype(BF16)
    h = (x * (1.0 + mod_ref[1:2, :]) + mod_ref[0:1, :]).astype(BF16)
    q_ref[...] = (_dot(h, wq_ref[...]) * (HEAD_DIM ** -0.5 * LOG2_E)).astype(BF16)


def _qkv(x, mods_l, w_kv, w_q):
    bsz, s, d = x.shape
    ts = min(PROJ_TILE, s)
    tile = pl.BlockSpec((None, ts, d), lambda b, i: (b, i, 0))
    return pl.pallas_call(
        _qkv_body,
        grid=(bsz, s // ts),
        in_specs=[
            tile,
            pl.BlockSpec((None, 6, d), lambda b, i: (b, 0, 0)),
            pl.BlockSpec((d, 2 * d), lambda b, i: (0, 0)),
            pl.BlockSpec((d, d), lambda b, i: (0, 0)),
        ],
        out_specs=[tile, tile, pl.BlockSpec((None, d, ts), lambda b, i: (b, 0, i))],
        out_shape=[jax.ShapeDtypeStruct((bsz, s, d), BF16)] * 2 + [jax.ShapeDtypeStruct((bsz, d, s), BF16)],
        compiler_params=_params("parallel", "parallel"),
        name="qkv",
    )(x, mods_l, w_kv.astype(BF16), w_q.astype(BF16))


def _t5_bucket(rel):
    nb = REL_BUCKETS // 2
    ret = jnp.where(rel > 0, nb, 0)
    n = jnp.abs(rel)
    max_exact = nb // 2
    large = max_exact + (jnp.log(jnp.maximum(n, 1).astype(F32) / max_exact)
                         / math.log(REL_MAX_DIST / max_exact) * (nb - max_exact)).astype(I32)
    large = jnp.minimum(large, nb - 1)
    return ret + jnp.where(n < max_exact, n, large)


def _saturation_distance():
    nb = REL_BUCKETS // 2
    max_exact = nb // 2
    n = max_exact
    while max_exact + math.log(n / max_exact) / math.log(REL_MAX_DIST / max_exact) * (nb - max_exact) < nb - 1 + 1e-3:
        n += 1
    return n


def _bucket_strip(s):
    r = jnp.arange(Q_BLOCK, dtype=I32)[:, None]
    kp = jnp.arange(s, dtype=I32)[None, :] - (s - Q_BLOCK)
    bucket = _t5_bucket(kp - r)
    visible = jnp.floor_divide(kp, CHUNK) <= (r // CHUNK)
    return jnp.where(visible, bucket, REL_BUCKETS)


def _attn_body(tab_ref, q_ref, k_ref, vt_ref, bkt_ref, lam_ref, sg_ref, ot_ref, bias_ref, *, lambda_init, n_heads):
    h = pl.program_id(0)
    b = pl.program_id(1)
    s = q_ref.shape[0]

    near = bias_ref.shape[0]

    @pl.when(b == 0)
    def _():
        bk = bkt_ref[...]
        acc = jnp.full(bk.shape, MASK_VALUE, F32)
        for r in range(REL_BUCKETS):
            acc = jnp.where(bk == r, tab_ref[r * n_heads + h] * LOG2_E, acc)
        bias_ref[:, 0:Q_BLOCK] = acc
        bias_ref[:, Q_BLOCK:] = acc

    far_bias = tab_ref[(REL_BUCKETS // 2 - 1) * n_heads + h] * LOG2_E

    lp = lam_ref[...]
    lam = (jnp.exp(jnp.sum(lp[0:1, :] * lp[1:2, :], axis=-1, keepdims=True))
           - jnp.exp(jnp.sum(lp[2:3, :] * lp[3:4, :], axis=-1, keepdims=True)) + lambda_init)
    lane = lax.broadcasted_iota(I32, (Q_BLOCK, 2 * HEAD_DIM), 1)
    nt = (((1,), (1,)), ((), ()))

    def scores(i):
        n_keys = (i + 1) * Q_BLOCK
        q = q_ref[i * Q_BLOCK:(i + 1) * Q_BLOCK, :]
        qq = jnp.concatenate([jnp.where(lane < HEAD_DIM, q, jnp.zeros_like(q)),
                              jnp.where(lane >= HEAD_DIM, q, jnp.zeros_like(q))], axis=0)
        return lax.dot_general(k_ref[0:n_keys, :], qq, nt, preferred_element_type=F32)

    def values(i, p, denom):
        pv = _dot(vt_ref[:, 0:(i + 1) * Q_BLOCK], p)
        o = pv[:, :Q_BLOCK] * (1.0 / denom[:, :Q_BLOCK]) - pv[:, Q_BLOCK:] * (lam / denom[:, Q_BLOCK:])
        o = o * lax.rsqrt(jnp.mean(o * o, axis=0, keepdims=True) + LN_EPS) * sg_ref[...]
        ot_ref[:, i * Q_BLOCK:(i + 1) * Q_BLOCK] = (o * (1.0 - lambda_init)).astype(BF16)

    n_blocks = s // Q_BLOCK
    ahead = [scores(i) for i in range(min(SCORE_LOOKAHEAD, n_blocks))]
    for i in range(n_blocks):
        raw = ahead.pop(0)
        if i + SCORE_LOOKAHEAD < n_blocks:
            ahead.append(scores(i + SCORE_LOOKAHEAD))
        n_keys = (i + 1) * Q_BLOCK
        n_near = min(near, n_keys)
        n_far = n_keys - n_near
        sc = raw[n_far:] + bias_ref[near - n_near:, :]
        top = jnp.max(sc, axis=0, keepdims=True)
        if n_far:
            top = jnp.maximum(top, jnp.max(raw[:n_far], axis=0, keepdims=True) + far_bias)
            p = jnp.concatenate([jnp.exp2(raw[:n_far] - (top - far_bias)), jnp.exp2(sc - top)], axis=0)
        else:
            p = jnp.exp2(sc - top)
        values(i, p.astype(BF16), jnp.sum(p, axis=0, keepdims=True))


def _attn(q, k, vt, lam_p, subln_g, rel_table, lambda_init):
    bsz, s, d = q.shape
    hd2 = 2 * HEAD_DIM
    n_heads = d // hd2
    assert REL_MAX_DIST >= CHUNK and _saturation_distance() <= REL_MAX_DIST
    near = min(Q_BLOCK + REL_MAX_DIST, s)
    head = pl.BlockSpec((None, s, hd2), lambda h, b, tab: (b, 0, h))
    head_t = pl.BlockSpec((None, hd2, s), lambda h, b, tab: (b, h, 0))
    grid_spec = pltpu.PrefetchScalarGridSpec(
        num_scalar_prefetch=1,
        grid=(n_heads, bsz),
        in_specs=[
            head, head, head_t,
            pl.BlockSpec((near, Q_BLOCK), lambda h, b, tab: (0, 0)),
            pl.BlockSpec((4, HEAD_DIM), lambda h, b, tab: (0, 0)),
            pl.BlockSpec((hd2, 1), lambda h, b, tab: (0, 0)),
        ],
        out_specs=head_t,
        scratch_shapes=[pltpu.VMEM((near, 2 * Q_BLOCK), F32)],
    )
    return pl.pallas_call(
        functools.partial(_attn_body, lambda_init=lambda_init, n_heads=n_heads),
        grid_spec=grid_spec,
        out_shape=jax.ShapeDtypeStruct((bsz, d, s), BF16),
        compiler_params=_params("arbitrary", "arbitrary"),
        name="attn",
    )(rel_table.reshape(-1), q, k, vt, _bucket_strip(s).T[s - near:], lam_p, subln_g.reshape(hd2, 1))


def _attn_out_body(ot_ref, x_ref, mod_ref, wo_ref, lng_ref, lnb_ref, x1_ref, *, alpha):
    y = lax.dot_general(ot_ref[...], wo_ref[...], (((0,), (0,)), ((), ())), preferred_element_type=F32)
    _residual_epilogue(x_ref[...], y, mod_ref, lng_ref, lnb_ref, alpha, x1_ref)


def _attn_out(o, x, mods_l, w_o, ln_g, ln_b, alpha):
    bsz, s, d = x.shape
    ts = min(PROJ_TILE, s)
    tile = pl.BlockSpec((None, ts, d), lambda b, i: (b, i, 0))
    vec = pl.BlockSpec((1, d), lambda b, i: (0, 0))
    return pl.pallas_call(
        functools.partial(_attn_out_body, alpha=alpha),
        grid=(bsz, s // ts),
        in_specs=[pl.BlockSpec((None, d, ts), lambda b, i: (b, 0, i)), tile,
                  pl.BlockSpec((None, 6, d), lambda b, i: (b, 0, 0)),
                  pl.BlockSpec((d, d), lambda b, i: (0, 0)), vec, vec],
        out_specs=tile,
        out_shape=jax.ShapeDtypeStruct((bsz, s, d), F32),
        compiler_params=_params("parallel", "parallel"),
        name="attn_out",
    )(o, x, mods_l, w_o.astype(BF16), ln_g.reshape(1, d), ln_b.reshape(1, d))


def _router_body(x1_ref, mod_ref, whi_ref, wlo_ref, b_ref, tri_ref, idx_ref, gate_ref, rank_ref, cnt_ref):
    h = _moe_input(x1_ref, mod_ref)
    h_hi = h.astype(BF16)
    h_lo = (h - h_hi.astype(F32)).astype(BF16)
    nt = (((1,), (1,)), ((), ()))
    logits = (lax.dot_general(whi_ref[...], h_hi, nt, preferred_element_type=F32)
              + (lax.dot_general(whi_ref[...], h_lo, nt, preferred_element_type=F32)
                 + lax.dot_general(wlo_ref[...], h_hi, nt, preferred_element_type=F32))) + b_ref[...]
    n_exp, tr = logits.shape
    eio = lax.broadcasted_iota(I32, (n_exp, tr), 0)
    work = logits
    vals, idxs = [], []
    for _ in range(TOP_K):
        m = jnp.max(work, axis=0, keepdims=True)
        am = jnp.min(jnp.where(work == m, eio, n_exp), axis=0, keepdims=True)
        vals.append(m)
        idxs.append(am)
        work = jnp.where(eio == am, -jnp.inf, work)
    ex = [jnp.exp(v - vals[0]) for v in vals]
    den = ex[0] + ex[1] + ex[2] + ex[3]
    onehot = jnp.zeros((n_exp, tr), F32)
    for k in range(TOP_K):
        onehot = onehot + (eio == idxs[k]).astype(F32)
    tl = tri_ref.shape[0]
    tiles = [onehot[:, t0:t0 + tl] for t0 in range(0, tr, tl)]
    before = jnp.concatenate([_dot(oh.astype(BF16), tri_ref[...]) for oh in tiles], axis=1)
    for k in range(TOP_K):
        idx_ref[k:k + 1, :] = idxs[k]
        gate_ref[k:k + 1, :] = ex[k] / den
        rank_ref[k:k + 1, :] = jnp.sum(jnp.where(eio == idxs[k], before, 0.0), axis=0, keepdims=True).astype(I32)
    for j, oh in enumerate(tiles):
        cnt_ref[j] = jnp.sum(oh, axis=1, keepdims=True).astype(I32)


def _router(x1, mods_l, seq_len, w_r, b_r):
    t, d = x1.shape
    n_exp = w_r.shape[1]
    tl = min(MOE_TOKENS, t)
    tr = min(ROUTER_TOKENS, seq_len)
    per_seq = seq_len // tr
    wt = w_r.T
    wt_hi = wt.astype(BF16)
    wt_lo = (wt - wt_hi.astype(F32)).astype(BF16)
    pos = jnp.arange(tl, dtype=I32)
    tri = (pos[:, None] < pos[None, :]).astype(BF16)
    tok = pl.BlockSpec((TOP_K, tr), lambda i: (0, i))
    return pl.pallas_call(
        _router_body,
        grid=(t // tr,),
        in_specs=[
            pl.BlockSpec((tr, d), lambda i: (i, 0)),
            pl.BlockSpec((None, 6, d), lambda i: (i // per_seq, 0, 0)),
            pl.BlockSpec((n_exp, d), lambda i: (0, 0)),
            pl.BlockSpec((n_exp, d), lambda i: (0, 0)),
            pl.BlockSpec((n_exp, 1), lambda i: (0, 0)),
            pl.BlockSpec((tl, tl), lambda i: (0, 0)),
        ],
        out_specs=[tok, tok, tok, pl.BlockSpec((tr // tl, n_exp, 1), lambda i: (i, 0, 0))],
        out_shape=[jax.ShapeDtypeStruct((TOP_K, t), I32), jax.ShapeDtypeStruct((TOP_K, t), F32),
                   jax.ShapeDtypeStruct((TOP_K, t), I32), jax.ShapeDtypeStruct((t // tl, n_exp, 1), I32)],
        compiler_params=_params("parallel"),
        name="router",
    )(x1, mods_l, wt_hi, wt_lo, b_r.reshape(n_exp, 1), tri)


def _rows(ref, start, cnt):
    aligned = lambda v: v if isinstance(v, int) else pl.multiple_of(v, SUBLANES)
    return ref.at[pl.ds(aligned(start), aligned(cnt))]


def _segment_copy(src, src_row, dst, dst_row, cnt, sem):
    @pl.when(cnt > 0)
    def _():
        pltpu.make_async_copy(_rows(src, src_row, cnt), _rows(dst, dst_row, cnt), sem).start()


def _wait_rows(ref, cnt, sem):
    @pl.when(cnt > 0)
    def _():
        pltpu.make_async_copy(_rows(ref, 0, cnt), _rows(ref, 0, cnt), sem).wait()


def _zero_fill(zero_ref, xs_hbm, start, cnt, sem):
    zr = zero_ref.shape[0]
    n_full = cnt // zr

    def full(r, c):
        _segment_copy(zero_ref, 0, xs_hbm, start + r * zr, zr, sem)
        return c

    lax.fori_loop(0, n_full, full, 0)
    _segment_copy(zero_ref, 0, xs_hbm, start + n_full * zr, cnt - n_full * zr, sem)


def _pack_halves(x):
    hd = x.shape[1] // 2
    lo = lax.bitcast_convert_type(x[:, :hd], U32)
    hi = lax.bitcast_convert_type(x[:, hd:], U32)
    return (lo >> 16) | (hi & jnp.uint32(0xFFFF0000))


def _unpack_halves(words):
    lo = lax.bitcast_convert_type(words << 16, F32).astype(BF16)
    hi = lax.bitcast_convert_type(words & jnp.uint32(0xFFFF0000), F32).astype(BF16)
    return lo, hi


def _tiles_per_step(seq_len, tl):
    return MOE_TILES_PER_STEP if seq_len % (tl * MOE_TILES_PER_STEP) == 0 else 1


def _one_hot_hits(iota, pos_of):
    hit = iota == pos_of(0)
    for k in range(1, TOP_K):
        hit = hit | (iota == pos_of(k))
    return hit


def _dispatch_body(goff_ref, loff_ref, cp_ref, ltot_ref, pad_start_ref, pad_cnt_ref, x1_ref, mod_ref, lp_ref,
                   gate_ref, xs_hbm, xl_ref, zero_ref, sem, zsem, *, n_exp, tiles_per_step):
    j = pl.program_id(0)
    d = x1_ref.shape[1]
    tl = x1_ref.shape[0] // tiles_per_step
    lr = xl_ref.shape[1]

    @pl.when(j == 0)
    def _():
        zero_ref[...] = jnp.zeros_like(zero_ref)

        def per_range(e, carry):
            _zero_fill(zero_ref, xs_hbm, pad_start_ref[e], pad_cnt_ref[e], zsem)
            return carry
        lax.fori_loop(0, pad_start_ref.shape[0], per_range, 0)

    def start_segments(tile):
        def per_expert(e, carry):
            seg = tile * n_exp + e
            _segment_copy(xl_ref.at[tile % 2], loff_ref[seg], xs_hbm, goff_ref[seg], cp_ref[seg], sem.at[tile % 2])
            return carry
        lax.fori_loop(0, n_exp, per_expert, 0)

    def wait_segments(tile):
        _wait_rows(xs_hbm, ltot_ref[tile], sem.at[tile % 2])

    riota = lax.broadcasted_iota(I32, (lr, tl), 0)
    for t in range(tiles_per_step):
        tile = j * tiles_per_step + t
        tok = slice(t * tl, (t + 1) * tl)

        @pl.when(tile >= 2)
        def _():
            wait_segments(tile - 2)

        buf = xl_ref.at[tile % 2]
        perm = _one_hot_hits(riota, lambda k: lp_ref[k:k + 1, tok]).astype(F32).astype(BF16)
        rows = _dot(perm, _moe_input(x1_ref, mod_ref, tok).astype(BF16))
        gsel = jnp.zeros((lr, tl), F32)
        for k in range(TOP_K):
            gsel = gsel + jnp.where(riota == lp_ref[k:k + 1, tok], gate_ref[k:k + 1, tok], 0.0)
        rowg = jnp.broadcast_to(jnp.sum(gsel, axis=1, keepdims=True), (lr, LANES))
        buf[:, d // 2:] = lax.bitcast_convert_type(rowg, U32)
        buf[:, 0:d // 2] = _pack_halves(rows)
        start_segments(tile)

    @pl.when(j == pl.num_programs(0) - 1)
    def _():
        last = pl.num_programs(0) * tiles_per_step - 1

        @pl.when(last >= 1)
        def _():
            wait_segments(last - 1)
        wait_segments(last)
        n_zero = lax.fori_loop(0, pad_cnt_ref.shape[0], lambda e, acc: acc + pad_cnt_ref[e], jnp.int32(0))
        _wait_rows(xs_hbm, n_zero, zsem)


def _dispatch(x1, mods_l, seq_len, lp, gates, goff, loff, cp, ltot, pad_start, pad_cnt, n_rows):
    t, d = x1.shape
    tl = min(MOE_TOKENS, t)
    n_exp = goff.shape[0] // (t // tl)
    lr = TOP_K * tl + n_exp * SUBLANES
    tps = _tiles_per_step(seq_len, tl)
    ts = tl * tps
    per_seq = seq_len // ts
    tok = pl.BlockSpec((TOP_K, ts), lambda j, *_: (0, j))
    grid_spec = pltpu.PrefetchScalarGridSpec(
        num_scalar_prefetch=6,
        grid=(t // ts,),
        in_specs=[pl.BlockSpec((ts, d), lambda j, *_: (j, 0)),
                  pl.BlockSpec((None, 6, d), lambda j, *_: (j // per_seq, 0, 0)), tok, tok],
        out_specs=pl.BlockSpec(memory_space=pl.ANY),
        scratch_shapes=[pltpu.VMEM((2, lr, d // 2 + LANES), U32), pltpu.VMEM((MOE_TILE, d // 2 + LANES), U32),
                        pltpu.SemaphoreType.DMA((2,)), pltpu.SemaphoreType.DMA],
    )
    return pl.pallas_call(
        functools.partial(_dispatch_body, n_exp=n_exp, tiles_per_step=tps),
        grid_spec=grid_spec,
        out_shape=jax.ShapeDtypeStruct((n_rows, d // 2 + LANES), U32),
        compiler_params=_params("arbitrary"),
        name="dispatch",
    )(goff, loff, cp, ltot, pad_start, pad_cnt, x1, mods_l, lp, gates)


def _experts_body(be_ref, bsrc_ref, rows_ref, first_ref, slot_ref, next_ref, xs_ref, wgu_hbm, bgu_ref, wdn_hbm,
                  bdn_ref, ys_ref, wgu_f32, wdn_f32, wgu_bf, wdn_bf, sem_gu, sem_dn, *, layer):
    i = pl.program_id(0)
    e = be_ref[i]
    d, f2 = wgu_bf.shape
    f = f2 // 2

    def fetch(expert, slot, start):
        for hbm, buf, sem in ((wgu_hbm, wgu_f32, sem_gu), (wdn_hbm, wdn_f32, sem_dn)):
            dma = pltpu.make_async_copy(hbm.at[layer, expert], buf.at[slot], sem.at[slot])
            dma.start() if start else dma.wait()

    @pl.when(first_ref[i] == 1)
    def _():
        slot = slot_ref[i]

        @pl.when(i == 0)
        def _():
            fetch(e, slot, True)

        fetch(e, slot, False)

        @pl.when(next_ref[i] >= 0)
        def _():
            fetch(next_ref[i], 1 - slot, True)

        def cast(ref_in, ref_out):
            def step(r, c):
                rows = pl.ds(pl.multiple_of(r * CAST_ROWS, CAST_ROWS), CAST_ROWS)
                ref_out[rows, :] = ref_in[rows, :].astype(BF16)
                return c
            lax.fori_loop(0, ref_in.shape[0] // CAST_ROWS, step, 0)
        cast(wgu_f32.at[slot], wgu_bf)
        cast(wdn_f32.at[slot], wdn_bf)

    hd = d // 2
    tm = xs_ref.shape[0]

    def mlp(n):
        x_lo, x_hi = _unpack_halves(xs_ref[0:n, 0:hd])
        proj = lambda cols: _dot(x_lo, wgu_bf[0:hd, cols]) + _dot(x_hi, wgu_bf[hd:, cols]) + bgu_ref[e, :, cols]
        gate = jnp.minimum(proj(slice(0, f)), SWIGLU_LIMIT)
        lin = jnp.clip(proj(slice(f, f2)), -SWIGLU_LIMIT, SWIGLU_LIMIT)
        act = (gate * jax.nn.sigmoid(SWIGLU_ALPHA * gate) * (lin + 1.0)).astype(BF16)
        row_gate = lax.bitcast_convert_type(xs_ref[0:n, hd:hd + 1], F32)
        y = (_dot(act, wdn_bf[...]) + bdn_ref[e]) * row_gate
        ys_ref[0:n, :] = _pack_halves(y.astype(BF16).astype(F32))
        if n < tm:
            ys_ref[n:, :] = jnp.zeros((tm - n, hd), U32)

    rows = rows_ref[i]
    sizes = [tm * e // 8 for e in EXPERT_ROW_EIGHTHS]
    for n, smaller in zip(sizes, sizes[1:] + [0]):
        @pl.when((rows > smaller) & (rows <= n))
        def _(n=n):
            mlp(n)

    @pl.when(rows == 0)
    def _():
        ys_ref[...] = jnp.zeros_like(ys_ref)


def _experts(xs, block_e, block_src, block_rows, block_first, block_slot, block_next, w_gu, b_gu, w_dn, b_dn, layer):
    n_rows, dx = xs.shape
    depth, n_exp, d, f2 = w_gu.shape
    f = f2 // 2
    tm = MOE_TILE
    grid_spec = pltpu.PrefetchScalarGridSpec(
        num_scalar_prefetch=6,
        grid=(n_rows // tm,),
        in_specs=[
            pl.BlockSpec((tm, dx), lambda i, be, bs, *_: (bs[i], 0)),
            pl.BlockSpec(memory_space=pl.ANY),
            pl.BlockSpec((None, n_exp, 1, f2), lambda i, *_: (layer, 0, 0, 0)),
            pl.BlockSpec(memory_space=pl.ANY),
            pl.BlockSpec((None, n_exp, 1, d), lambda i, *_: (layer, 0, 0, 0)),
        ],
        out_specs=pl.BlockSpec((tm, d // 2), lambda i, *_: (i, 0)),
        scratch_shapes=[pltpu.VMEM((2, d, f2), F32), pltpu.VMEM((2, f, d), F32),
                        pltpu.VMEM((d, f2), BF16), pltpu.VMEM((f, d), BF16),
                        pltpu.SemaphoreType.DMA((2,)), pltpu.SemaphoreType.DMA((2,))],
    )
    return pl.pallas_call(
        functools.partial(_experts_body, layer=layer),
        grid_spec=grid_spec,
        out_shape=jax.ShapeDtypeStruct((n_rows, d // 2), U32),
        compiler_params=_params("arbitrary"),
        name="experts",
    )(block_e, block_src, block_rows, block_first, block_slot, block_next, xs, w_gu,
      b_gu.reshape(depth, n_exp, 1, f2), w_dn, b_dn.reshape(depth, n_exp, 1, d))


def _combine_body(goff_ref, loff_ref, cp_ref, ltot_ref, ys_hbm, x_ref, lp_ref, mod_ref, lng_ref, lnb_ref, o_ref,
                  yl_ref, sem, *, n_exp, alpha, tiles_per_step):
    j = pl.program_id(0)
    tl = x_ref.shape[0] // tiles_per_step
    n_tiles = pl.num_programs(0) * tiles_per_step
    lr, hd = yl_ref.shape[1:]

    def fetch(tile):
        buf = yl_ref.at[tile % 2]

        def per_expert(e, carry):
            seg = tile * n_exp + e
            _segment_copy(ys_hbm, goff_ref[seg], buf, loff_ref[seg], cp_ref[seg], sem.at[tile % 2])
            return carry
        lax.fori_loop(0, n_exp, per_expert, 0)

        def zero_rows(r, carry):
            buf[pl.ds(pl.multiple_of(r * SUBLANES, SUBLANES), SUBLANES), :] = jnp.zeros((SUBLANES, hd), U32)
            return carry
        lax.fori_loop(ltot_ref[tile] // SUBLANES, lr // SUBLANES, zero_rows, 0)

    @pl.when(j == 0)
    def _():
        fetch(j)

    liota = lax.broadcasted_iota(I32, (tl, lr), 1)
    for t in range(tiles_per_step):
        tile = j * tiles_per_step + t
        tok = slice(t * tl, (t + 1) * tl)

        @pl.when(tile + 1 < n_tiles)
        def _():
            fetch(tile + 1)

        pick = _one_hot_hits(liota, lambda k: lp_ref[tok, k:k + 1]).astype(F32).astype(BF16)
        _wait_rows(ys_hbm, ltot_ref[tile], sem.at[tile % 2])
        y_lo, y_hi = _unpack_halves(yl_ref[tile % 2])
        y = jnp.concatenate([_dot(pick, y_lo), _dot(pick, y_hi)], axis=1)
        o_ref[tok, :] = _layer_norm(alpha * x_ref[tok, :] + mod_ref[5:6, :] * y, lng_ref[...], lnb_ref[...])


def _combine(ys, lp_t, goff, loff, cp, ltot, x1, mods_l, ln_g, ln_b, alpha, seq_len):
    t, d = x1.shape
    tl = min(MOE_TOKENS, t)
    n_exp = goff.shape[0] // (t // tl)
    lr = TOP_K * tl + n_exp * SUBLANES
    tps = _tiles_per_step(seq_len, tl)
    ts = tl * tps
    per_seq = seq_len // ts
    vec = pl.BlockSpec((1, d), lambda j, *_: (0, 0))
    grid_spec = pltpu.PrefetchScalarGridSpec(
        num_scalar_prefetch=4,
        grid=(t // ts,),
        in_specs=[
            pl.BlockSpec(memory_space=pl.ANY),
            pl.BlockSpec((ts, d), lambda j, *_: (j, 0)),
            pl.BlockSpec((ts, TOP_K), lambda j, *_: (j, 0)),
            pl.BlockSpec((None, 6, d), lambda j, *_: (j // per_seq, 0, 0)),
            vec, vec,
        ],
        out_specs=pl.BlockSpec((ts, d), lambda j, *_: (j, 0)),
        scratch_shapes=[pltpu.VMEM((2, lr, d // 2), U32), pltpu.SemaphoreType.DMA((2,))],
    )
    return pl.pallas_call(
        functools.partial(_combine_body, n_exp=n_exp, alpha=alpha, tiles_per_step=tps),
        grid_spec=grid_spec,
        out_shape=jax.ShapeDtypeStruct((t, d), F32),
        compiler_params=_params("arbitrary"),
        name="combine",
    )(goff, loff, cp, ltot, ys, x1, lp_t, mods_l, ln_g.reshape(1, d), ln_b.reshape(1, d))


def _moe_layer(x1, mods_l, w_r, b_r, w_gu, b_gu, w_dn, b_dn, layer, ln_g, ln_b, alpha, seq_len):
    t, d = x1.shape
    n_exp = w_r.shape[1]
    tm = MOE_TILE
    tl = min(MOE_TOKENS, t)
    n_tiles = t // tl
    idx, gates, rank, cnt = _router(x1, mods_l, seq_len, w_r, b_r)
    cp = (cnt[:, :, 0] + SUBLANES - 1) // SUBLANES * SUBLANES
    tot = jnp.sum(cp, axis=0)
    padded = (tot + tm - 1) // tm * tm
    pend = jnp.cumsum(padded)
    pstart = pend - padded
    goff = pstart[None, :] + jnp.cumsum(cp, axis=0) - cp
    loff = jnp.cumsum(cp, axis=1) - cp
    ltot = jnp.sum(cp, axis=1).astype(I32)
    is_e = idx[..., None] == jnp.arange(n_exp, dtype=I32)
    lp = jnp.sum(jnp.where(is_e, jnp.repeat(loff, tl, axis=0)[None], 0), axis=-1) + rank
    n_blocks = -(-(t * TOP_K + n_tiles * n_exp * (SUBLANES - 1)) // tm) + n_exp
    n_valid = pend[-1] // tm
    blk = jnp.arange(n_blocks, dtype=I32)
    block_src = jnp.minimum(blk, n_valid - 1)
    block_e = jnp.sum(pend[None, :] <= (block_src * tm)[:, None], axis=1).astype(I32)
    used = padded > 0
    experts = jnp.arange(n_exp, dtype=I32)
    next_used = jnp.flip(lax.cummin(jnp.flip(jnp.where(used, experts, n_exp))))
    next_used = jnp.concatenate([next_used[1:], jnp.full((1,), n_exp, I32)])
    next_used = jnp.where(next_used < n_exp, next_used, -1)
    slot_of = (jnp.cumsum(used.astype(I32)) - 1) % 2
    of_block = lambda table: jnp.sum(jnp.where(block_e[:, None] == experts[None, :], table[None, :], 0), axis=1)
    block_first = ((blk * tm == of_block(pstart)) & (blk < n_valid)).astype(I32)
    block_slot = of_block(slot_of).astype(I32)
    block_next = of_block(next_used).astype(I32)
    block_rows = jnp.where(blk < n_valid, jnp.clip(of_block(pstart + tot) - blk * tm, 0, tm), 0).astype(I32)
    n_rows = n_blocks * tm
    pad_start = jnp.concatenate([pstart + tot, pend[-1:]]).astype(I32)
    pad_cnt = jnp.concatenate([padded - tot, n_rows - pend[-1:]]).astype(I32)
    flat = lambda a: a.reshape(-1).astype(I32)
    seg = (flat(goff), flat(loff), flat(cp), ltot)
    xs = _dispatch(x1, mods_l, seq_len, lp, gates, *seg, pad_start, pad_cnt, n_rows)
    ys = _experts(xs, block_e, block_src.astype(I32), block_rows, block_first, block_slot, block_next,
                  w_gu, b_gu, w_dn, b_dn, layer)
    return _combine(ys, lp.T, *seg, x1, mods_l, ln_g, ln_b, alpha, seq_len)


def kernel(x, c, ada_w, ada_b, post_ln_g, post_ln_b, conv_w_pw1, conv_b_pw1, conv_w_dw, conv_b_dw, conv_ln_g, conv_ln_b, conv_w_pw2, conv_b_pw2, w_kv, attn_w_q, attn_lambda, attn_subln_g, attn_w_o, rel_bias_table, router_w, router_b, expert_w_gate_up, expert_b_gate_up, expert_w_down, expert_b_down):
    bsz, s, d = x.shape
    depth = ada_w.shape[0]
    n_a = depth // 2
    alpha = (2 * depth) ** 0.25
    mods = _ada(c, ada_w, ada_b).reshape(depth, bsz, 6, d)
    q = k = vt = None
    for l in range(depth):
        mods_l = mods[l]
        if l < n_a:
            x1 = _conv(x, mods_l, conv_w_pw1[l], conv_b_pw1[l], conv_w_dw[l], conv_b_dw[l], conv_ln_g[l],
                       conv_ln_b[l], conv_w_pw2[l], conv_b_pw2[l], post_ln_g[l, 0], post_ln_b[l, 0], alpha)
        else:
            j = l - n_a
            if j == 0:
                q, k, vt = _qkv(x, mods_l, w_kv, attn_w_q[j])
            else:
                q = _qkv(x, mods_l, w_kv, attn_w_q[j])[0]
            lambda_init = 0.8 - 0.6 * math.exp(-0.3 * l)
            ot = _attn(q, k, vt, attn_lambda[j], attn_subln_g[j], rel_bias_table, lambda_init)
            x1 = _attn_out(ot, x, mods_l, attn_w_o[j], post_ln_g[l, 0], post_ln_b[l, 0], alpha)
        x = _moe_layer(x1.reshape(bsz * s, d), mods_l, router_w[l], router_b[l],
                       expert_w_gate_up, expert_b_gate_up, expert_w_down, expert_b_down, l,
                       post_ln_g[l, 1], post_ln_b[l, 1], alpha, s).reshape(bsz, s, d)
    return x
```

```python
import functools
import math

import jax
import jax.numpy as jnp
from jax import lax
from jax.experimental import pallas as pl
from jax.experimental.pallas import tpu as pltpu

F32 = jnp.float32
BF16 = jnp.bfloat16
I32 = jnp.int32
U32 = jnp.uint32
HIGHEST = lax.Precision.HIGHEST

CHUNK = 64
CONV_WIDTH = 31
HEAD_DIM = 64
REL_BUCKETS = 32
REL_MAX_DIST = 128
TOP_K = 4
SWIGLU_LIMIT = 7.0
SWIGLU_ALPHA = 1.702
LN_EPS = 1e-5
MASK_VALUE = -1e30
LOG2_E = math.log2(math.e)

SUBLANES = 8
LANES = 128
VMEM_LIMIT_BYTES = 56 * 1024 * 1024

ADA_TN = 1024
SEQ_TILE = 512
PROJ_TILE = 512
CONV_HALO = 32
CONV_ROWS = 256
CONV_COLS = 128
MOE_TOKENS = 256
ROUTER_TOKENS = 2048
MOE_TILES_PER_STEP = 2
MOE_TILE = 512
EXPERT_ROW_EIGHTHS = (8, 6, 4, 2)
Q_BLOCK = 128
GLU_LOOKAHEAD = 1
SCORE_LOOKAHEAD = 3
CAST_ROWS = 128

HALF_BITS = 16
HIGH_HALF_MASK = 0xFFFF0000


def _params(*sem):
    return pltpu.CompilerParams(dimension_semantics=sem, vmem_limit_bytes=VMEM_LIMIT_BYTES)


def _layer_norm(x, g, b):
    mu = jnp.mean(x, axis=-1, keepdims=True)
    xc = x - mu
    var = jnp.mean(xc * xc, axis=-1, keepdims=True)
    return xc * lax.rsqrt(var + LN_EPS) * g + b


def _dot(a, b):
    return jnp.dot(a, b, preferred_element_type=F32)


def _ada_body(c_ref, w_ref, b_ref, o_ref):
    c = c_ref[...]
    cond = c * jax.nn.sigmoid(c)
    o_ref[...] = jnp.dot(cond, w_ref[...], preferred_element_type=F32, precision=HIGHEST) + b_ref[...]


def _ada(c, ada_w, ada_b):
    depth, d, n = ada_w.shape
    bsz = c.shape[0]
    tn = min(ADA_TN, n)
    return pl.pallas_call(
        _ada_body,
        grid=(depth, n // tn),
        in_specs=[
            pl.BlockSpec((bsz, d), lambda l, j: (0, 0)),
            pl.BlockSpec((None, d, tn), lambda l, j: (l, 0, j)),
            pl.BlockSpec((None, 1, tn), lambda l, j: (l, 0, j)),
        ],
        out_specs=pl.BlockSpec((None, bsz, tn), lambda l, j: (l, 0, j)),
        out_shape=jax.ShapeDtypeStruct((depth, bsz, n), F32),
        compiler_params=_params("parallel", "parallel"),
        name="ada",
    )(c, ada_w, ada_b.reshape(depth, 1, n))


def _residual_epilogue(x, y, mod_ref, lng_ref, lnb_ref, alpha, x1_ref):
    x1_ref[...] = _layer_norm(alpha * x + mod_ref[2:3, :] * y, lng_ref[...], lnb_ref[...])


def _moe_input(x1_ref, mod_ref, rows=slice(None)):
    return x1_ref[rows, :] * (1.0 + mod_ref[4:5, :]) + mod_ref[3:4, :]


def _conv_body(x_ref, halo_ref, mod_ref, w1_ref, b1_ref, wdw_ref, bdw_ref, cg_ref, cb_ref, w2_ref, b2_ref,
               lng_ref, lnb_ref, x1_ref, win_ref, v_ref, *, alpha):
    ts, d = x_ref.shape
    i = pl.program_id(1)
    xw = jnp.concatenate([halo_ref[...], x_ref[...]], axis=0)
    h = (xw * (1.0 + mod_ref[1:2, :]) + mod_ref[0:1, :]).astype(BF16)
    rows = min(CONV_ROWS, ts)
    cols = min(CONV_COLS, d)

    def glu(c0):
        a = _dot(h, w1_ref[:, c0:c0 + cols]) + b1_ref[:, c0:c0 + cols]
        g = _dot(h, w1_ref[:, d + c0:d + c0 + cols]) + b1_ref[:, d + c0:d + c0 + cols]
        return a * jax.nn.sigmoid(g)

    off = CONV_HALO - (CONV_WIDTH - 1)
    n_shift = ts + CONV_HALO - SUBLANES
    in_seq = (lax.broadcasted_iota(I32, (ts + CONV_HALO, cols), 0) >= CONV_HALO) | (i > 0)
    starts = list(range(0, d, cols))
    ahead = [glu(c) for c in starts[:GLU_LOOKAHEAD]]
    for n, c0 in enumerate(starts):
        chan = slice(c0, c0 + cols)
        win_ref[0, :, chan] = jnp.where(in_seq, ahead.pop(0), 0.0)
        if n + GLU_LOOKAHEAD < len(starts):
            ahead.append(glu(starts[n + GLU_LOOKAHEAD]))
        for b in range(1, SUBLANES):
            win_ref[b, 0:n_shift, chan] = win_ref[0, b:b + n_shift, chan]
        for r0 in range(0, ts, rows):
            acc = jnp.zeros((rows, cols), F32)
            for j in range(CONV_WIDTH):
                a, b = divmod(off + j, SUBLANES)
                r = r0 + a * SUBLANES
                acc = acc + wdw_ref[j:j + 1, c0:c0 + cols] * win_ref[b, r:r + rows, c0:c0 + cols]
            v_ref[r0:r0 + rows, c0:c0 + cols] = acc
    v = _layer_norm(v_ref[...] + bdw_ref[...], cg_ref[...], cb_ref[...])
    v = (v * jax.nn.sigmoid(v)).astype(BF16)
    y = _dot(v, w2_ref[...]) + b2_ref[...]
    _residual_epilogue(x_ref[...], y, mod_ref, lng_ref, lnb_ref, alpha, x1_ref)


def _conv(x, mods_l, w_pw1, b_pw1, w_dw, b_dw, cln_g, cln_b, w_pw2, b_pw2, ln_g, ln_b, alpha):
    bsz, s, d = x.shape
    ts = min(SEQ_TILE, s)
    hb = ts // CONV_HALO
    row = lambda a: a.reshape(1, d)
    tile = pl.BlockSpec((None, ts, d), lambda b, i: (b, i, 0))
    vec = pl.BlockSpec((1, d), lambda b, i: (0, 0))
    return pl.pallas_call(
        functools.partial(_conv_body, alpha=alpha),
        grid=(bsz, s // ts),
        in_specs=[
            tile,
            pl.BlockSpec((None, CONV_HALO, d), lambda b, i: (b, jnp.maximum(i * hb - 1, 0), 0)),
            pl.BlockSpec((None, 6, d), lambda b, i: (b, 0, 0)),
            pl.BlockSpec((d, 2 * d), lambda b, i: (0, 0)),
            pl.BlockSpec((1, 2 * d), lambda b, i: (0, 0)),
            pl.BlockSpec((CONV_WIDTH, d), lambda b, i: (0, 0)),
            vec, vec, vec,
            pl.BlockSpec((d, d), lambda b, i: (0, 0)),
            vec, vec, vec,
        ],
        out_specs=tile,
        out_shape=jax.ShapeDtypeStruct((bsz, s, d), F32),
        scratch_shapes=[pltpu.VMEM((SUBLANES, ts + CONV_HALO, d), F32), pltpu.VMEM((ts, d), F32)],
        compiler_params=_params("parallel", "parallel"),
        name="conv",
    )(x, x, mods_l, w_pw1.astype(BF16), b_pw1.reshape(1, 2 * d), w_dw, row(b_dw), row(cln_g), row(cln_b),
      w_pw2.astype(BF16), row(b_pw2), row(ln_g), row(ln_b))


def _qkv_body(x_ref, mod_ref, wkv_ref, wq_ref, q_ref, k_ref, vt_ref):
    d = x_ref.shape[-1]
    x = x_ref[...]
    kv = _dot(x.astype(BF16), wkv_ref[...])
    k_ref[...] = kv[:, :d].astype(BF16)
    hd2 = 2 * HEAD_DIM
    for c0 in range(0, d, hd2):
        vt_ref[c0:c0 + hd2, :] = kv[:, d + c0:d + c0 + hd2].T.astype(BF16)
    h = (x * (1.0 + mod_ref[1:2, :]) + mod_ref[0:1, :]).astype(BF16)
    q_ref[...] = (_dot(h, wq_ref[...]) * (HEAD_DIM ** -0.5 * LOG2_E)).astype(BF16)


def _qkv(x, mods_l, w_kv, w_q):
    bsz, s, d = x.shape
    ts = min(PROJ_TILE, s)
    tile = pl.BlockSpec((None, ts, d), lambda b, i: (b, i, 0))
    return pl.pallas_call(
        _qkv_body,
        grid=(bsz, s // ts),
        in_specs=[
            tile,
            pl.BlockSpec((None, 6, d), lambda b, i: (b, 0, 0)),
            pl.BlockSpec((d, 2 * d), lambda b, i: (0, 0)),
            pl.BlockSpec((d, d), lambda b, i: (0, 0)),
        ],
        out_specs=[tile, tile, pl.BlockSpec((None, d, ts), lambda b, i: (b, 0, i))],
        out_shape=[jax.ShapeDtypeStruct((bsz, s, d), BF16)] * 2 + [jax.ShapeDtypeStruct((bsz, d, s), BF16)],
        compiler_params=_params("parallel", "parallel"),
        name="qkv",
    )(x, mods_l, w_kv.astype(BF16), w_q.astype(BF16))


def _t5_bucket(rel):
    nb = REL_BUCKETS // 2
    ret = jnp.where(rel > 0, nb, 0)
    n = jnp.abs(rel)
    max_exact = nb // 2
    large = max_exact + (jnp.log(jnp.maximum(n, 1).astype(F32) / max_exact)
                         / math.log(REL_MAX_DIST / max_exact) * (nb - max_exact)).astype(I32)
    large = jnp.minimum(large, nb - 1)
    return ret + jnp.where(n < max_exact, n, large)


def _saturation_distance():
    nb = REL_BUCKETS // 2
    max_exact = nb // 2
    n = max_exact
    while max_exact + math.log(n / max_exact) / math.log(REL_MAX_DIST / max_exact) * (nb - max_exact) < nb - 1 + 1e-3:
        n += 1
    return n


def _bucket_strip(s):
    r = jnp.arange(Q_BLOCK, dtype=I32)[:, None]
    kp = jnp.arange(s, dtype=I32)[None, :] - (s - Q_BLOCK)
    bucket = _t5_bucket(kp - r)
    visible = jnp.floor_divide(kp, CHUNK) <= (r // CHUNK)
    return jnp.where(visible, bucket, REL_BUCKETS)


def _attn_body(tab_ref, q_ref, k_ref, vt_ref, bkt_ref, lam_ref, sg_ref, ot_ref, bias_ref, *, lambda_init, n_heads):
    h = pl.program_id(0)
    b = pl.program_id(1)
    s = q_ref.shape[0]

    near = bias_ref.shape[0]

    @pl.when(b == 0)
    def _():
        bk = bkt_ref[...]
        acc = jnp.full(bk.shape, MASK_VALUE, F32)
        for r in range(REL_BUCKETS):
            acc = jnp.where(bk == r, tab_ref[r * n_heads + h] * LOG2_E, acc)
        bias_ref[:, 0:Q_BLOCK] = acc
        bias_ref[:, Q_BLOCK:] = acc

    far_bias = tab_ref[(REL_BUCKETS // 2 - 1) * n_heads + h] * LOG2_E

    lp = lam_ref[...]
    lam = (jnp.exp(jnp.sum(lp[0:1, :] * lp[1:2, :], axis=-1, keepdims=True))
           - jnp.exp(jnp.sum(lp[2:3, :] * lp[3:4, :], axis=-1, keepdims=True)) + lambda_init)
    lane = lax.broadcasted_iota(I32, (Q_BLOCK, 2 * HEAD_DIM), 1)
    nt = (((1,), (1,)), ((), ()))

    def scores(i):
        n_keys = (i + 1) * Q_BLOCK
        q = q_ref[i * Q_BLOCK:(i + 1) * Q_BLOCK, :]
        qq = jnp.concatenate([jnp.where(lane < HEAD_DIM, q, jnp.zeros_like(q)),
                              jnp.where(lane >= HEAD_DIM, q, jnp.zeros_like(q))], axis=0)
        return lax.dot_general(k_ref[0:n_keys, :], qq, nt, preferred_element_type=F32)

    def values(i, p, denom):
        pv = _dot(vt_ref[:, 0:(i + 1) * Q_BLOCK], p)
        o = pv[:, :Q_BLOCK] * (1.0 / denom[:, :Q_BLOCK]) - pv[:, Q_BLOCK:] * (lam / denom[:, Q_BLOCK:])
        o = o * lax.rsqrt(jnp.mean(o * o, axis=0, keepdims=True) + LN_EPS) * sg_ref[...]
        ot_ref[:, i * Q_BLOCK:(i + 1) * Q_BLOCK] = (o * (1.0 - lambda_init)).astype(BF16)

    n_blocks = s // Q_BLOCK
    ahead = [scores(i) for i in range(min(SCORE_LOOKAHEAD, n_blocks))]
    for i in range(n_blocks):
        raw = ahead.pop(0)
        if i + SCORE_LOOKAHEAD < n_blocks:
            ahead.append(scores(i + SCORE_LOOKAHEAD))
        n_keys = (i + 1) * Q_BLOCK
        n_near = min(near, n_keys)
        n_far = n_keys - n_near
        sc = raw[n_far:] + bias_ref[near - n_near:, :]
        top = jnp.max(sc, axis=0, keepdims=True)
        if n_far:
            top = jnp.maximum(top, jnp.max(raw[:n_far], axis=0, keepdims=True) + far_bias)
            p = jnp.concatenate([jnp.exp2(raw[:n_far] - (top - far_bias)), jnp.exp2(sc - top)], axis=0)
        else:
            p = jnp.exp2(sc - top)
        values(i, p.astype(BF16), jnp.sum(p, axis=0, keepdims=True))


def _attn(q, k, vt, lam_p, subln_g, rel_table, lambda_init):
    bsz, s, d = q.shape
    hd2 = 2 * HEAD_DIM
    n_heads = d // hd2
    assert REL_MAX_DIST >= CHUNK and _saturation_distance() <= REL_MAX_DIST
    near = min(Q_BLOCK + REL_MAX_DIST, s)
    head = pl.BlockSpec((None, s, hd2), lambda h, b, tab: (b, 0, h))
    head_t = pl.BlockSpec((None, hd2, s), lambda h, b, tab: (b, h, 0))
    grid_spec = pltpu.PrefetchScalarGridSpec(
        num_scalar_prefetch=1,
        grid=(n_heads, bsz),
        in_specs=[
            head, head, head_t,
            pl.BlockSpec((near, Q_BLOCK), lambda h, b, tab: (0, 0)),
            pl.BlockSpec((4, HEAD_DIM), lambda h, b, tab: (0, 0)),
            pl.BlockSpec((hd2, 1), lambda h, b, tab: (0, 0)),
        ],
        out_specs=head_t,
        scratch_shapes=[pltpu.VMEM((near, 2 * Q_BLOCK), F32)],
    )
    return pl.pallas_call(
        functools.partial(_attn_body, lambda_init=lambda_init, n_heads=n_heads),
        grid_spec=grid_spec,
        out_shape=jax.ShapeDtypeStruct((bsz, d, s), BF16),
        compiler_params=_params("arbitrary", "arbitrary"),
        name="attn",
    )(rel_table.reshape(-1), q, k, vt, _bucket_strip(s).T[s - near:], lam_p, subln_g.reshape(hd2, 1))


def _attn_out_body(ot_ref, x_ref, mod_ref, wo_ref, lng_ref, lnb_ref, x1_ref, *, alpha):
    y = lax.dot_general(ot_ref[...], wo_ref[...], (((0,), (0,)), ((), ())), preferred_element_type=F32)
    _residual_epilogue(x_ref[...], y, mod_ref, lng_ref, lnb_ref, alpha, x1_ref)


def _attn_out(o, x, mods_l, w_o, ln_g, ln_b, alpha):
    bsz, s, d = x.shape
    ts = min(PROJ_TILE, s)
    tile = pl.BlockSpec((None, ts, d), lambda b, i: (b, i, 0))
    vec = pl.BlockSpec((1, d), lambda b, i: (0, 0))
    return pl.pallas_call(
        functools.partial(_attn_out_body, alpha=alpha),
        grid=(bsz, s // ts),
        in_specs=[pl.BlockSpec((None, d, ts), lambda b, i: (b, 0, i)), tile,
                  pl.BlockSpec((None, 6, d), lambda b, i: (b, 0, 0)),
                  pl.BlockSpec((d, d), lambda b, i: (0, 0)), vec, vec],
        out_specs=tile,
        out_shape=jax.ShapeDtypeStruct((bsz, s, d), F32),
        compiler_params=_params("parallel", "parallel"),
        name="attn_out",
    )(o, x, mods_l, w_o.astype(BF16), ln_g.reshape(1, d), ln_b.reshape(1, d))


def _router_body(x1_ref, mod_ref, whi_ref, wlo_ref, b_ref, tri_ref, idx_ref, gate_ref, rank_ref, cnt_ref):
    h = _moe_input(x1_ref, mod_ref)
    h_hi = h.astype(BF16)
    h_lo = (h - h_hi.astype(F32)).astype(BF16)
    nt = (((1,), (1,)), ((), ()))
    logits = (lax.dot_general(whi_ref[...], h_hi, nt, preferred_element_type=F32)
              + (lax.dot_general(whi_ref[...], h_lo, nt, preferred_element_type=F32)
                 + lax.dot_general(wlo_ref[...], h_hi, nt, preferred_element_type=F32))) + b_ref[...]
    n_exp, tr = logits.shape
    eio = lax.broadcasted_iota(I32, (n_exp, tr), 0)
    work = logits
    vals, idxs = [], []
    for _ in range(TOP_K):
        m = jnp.max(work, axis=0, keepdims=True)
        am = jnp.min(jnp.where(work == m, eio, n_exp), axis=0, keepdims=True)
        vals.append(m)
        idxs.append(am)
        work = jnp.where(eio == am, -jnp.inf, work)
    ex = [jnp.exp(v - vals[0]) for v in vals]
    den = ex[0] + ex[1] + ex[2] + ex[3]
    onehot = jnp.zeros((n_exp, tr), F32)
    for k in range(TOP_K):
        onehot = onehot + (eio == idxs[k]).astype(F32)
    tl = tri_ref.shape[0]
    tiles = [onehot[:, t0:t0 + tl] for t0 in range(0, tr, tl)]
    before = jnp.concatenate([_dot(oh.astype(BF16), tri_ref[...]) for oh in tiles], axis=1)
    for k in range(TOP_K):
        idx_ref[k:k + 1, :] = idxs[k]
        gate_ref[k:k + 1, :] = ex[k] / den
        rank_ref[k:k + 1, :] = jnp.sum(jnp.where(eio == idxs[k], before, 0.0), axis=0, keepdims=True).astype(I32)
    for j, oh in enumerate(tiles):
        cnt_ref[j] = jnp.sum(oh, axis=1, keepdims=True).astype(I32)


def _router(x1, mods_l, seq_len, w_r, b_r):
    t, d = x1.shape
    n_exp = w_r.shape[1]
    tl = min(MOE_TOKENS, t)
    tr = min(ROUTER_TOKENS, seq_len)
    per_seq = seq_len // tr
    wt = w_r.T
    wt_hi = wt.astype(BF16)
    wt_lo = (wt - wt_hi.astype(F32)).astype(BF16)
    pos = jnp.arange(tl, dtype=I32)
    tri = (pos[:, None] < pos[None, :]).astype(BF16)
    tok = pl.BlockSpec((TOP_K, tr), lambda i: (0, i))
    return pl.pallas_call(
        _router_body,
        grid=(t // tr,),
        in_specs=[
            pl.BlockSpec((tr, d), lambda i: (i, 0)),
            pl.BlockSpec((None, 6, d), lambda i: (i // per_seq, 0, 0)),
            pl.BlockSpec((n_exp, d), lambda i: (0, 0)),
            pl.BlockSpec((n_exp, d), lambda i: (0, 0)),
            pl.BlockSpec((n_exp, 1), lambda i: (0, 0)),
            pl.BlockSpec((tl, tl), lambda i: (0, 0)),
        ],
        out_specs=[tok, tok, tok, pl.BlockSpec((tr // tl, n_exp, 1), lambda i: (i, 0, 0))],
        out_shape=[jax.ShapeDtypeStruct((TOP_K, t), I32), jax.ShapeDtypeStruct((TOP_K, t), F32),
                   jax.ShapeDtypeStruct((TOP_K, t), I32), jax.ShapeDtypeStruct((t // tl, n_exp, 1), I32)],
        compiler_params=_params("parallel"),
        name="router",
    )(x1, mods_l, wt_hi, wt_lo, b_r.reshape(n_exp, 1), tri)


def _rows(ref, start, cnt):
    aligned = lambda v: v if isinstance(v, int) else pl.multiple_of(v, SUBLANES)
    return ref.at[pl.ds(aligned(start), aligned(cnt))]


def _segment_copy(src, src_row, dst, dst_row, cnt, sem):
    @pl.when(cnt > 0)
    def _():
        pltpu.make_async_copy(_rows(src, src_row, cnt), _rows(dst, dst_row, cnt), sem).start()


def _wait_rows(ref, cnt, sem):
    @pl.when(cnt > 0)
    def _():
        pltpu.make_async_copy(_rows(ref, 0, cnt), _rows(ref, 0, cnt), sem).wait()


def _zero_fill(zero_ref, xs_hbm, start, cnt, sem):
    zr = zero_ref.shape[0]
    n_full = cnt // zr

    def full(r, c):
        _segment_copy(zero_ref, 0, xs_hbm, start + r * zr, zr, sem)
        return c

    lax.fori_loop(0, n_full, full, 0)
    _segment_copy(zero_ref, 0, xs_hbm, start + n_full * zr, cnt - n_full * zr, sem)


def _pack_halves(x):
    hd = x.shape[1] // 2
    lo = lax.bitcast_convert_type(x[:, :hd], U32)
    hi = lax.bitcast_convert_type(x[:, hd:], U32)
    return (lo >> HALF_BITS) | (hi & jnp.uint32(HIGH_HALF_MASK))


def _unpack_halves(words):
    lo = lax.bitcast_convert_type(words << HALF_BITS, F32).astype(BF16)
    hi = lax.bitcast_convert_type(words & jnp.uint32(HIGH_HALF_MASK), F32).astype(BF16)
    return lo, hi


def _tiles_per_step(seq_len, tl):
    return MOE_TILES_PER_STEP if seq_len % (tl * MOE_TILES_PER_STEP) == 0 else 1


def _one_hot_hits(iota, pos_of):
    hit = iota == pos_of(0)
    for k in range(1, TOP_K):
        hit = hit | (iota == pos_of(k))
    return hit


def _dispatch_body(goff_ref, loff_ref, cp_ref, ltot_ref, pad_start_ref, pad_cnt_ref, x1_ref, mod_ref, lp_ref,
                   gate_ref, xs_hbm, xl_ref, zero_ref, sem, zsem, *, n_exp, tiles_per_step):
    j = pl.program_id(0)
    d = x1_ref.shape[1]
    tl = x1_ref.shape[0] // tiles_per_step
    lr = xl_ref.shape[1]

    @pl.when(j == 0)
    def _():
        zero_ref[...] = jnp.zeros_like(zero_ref)

        def per_range(e, carry):
            _zero_fill(zero_ref, xs_hbm, pad_start_ref[e], pad_cnt_ref[e], zsem)
            return carry
        lax.fori_loop(0, pad_start_ref.shape[0], per_range, 0)

    def start_segments(tile):
        def per_expert(e, carry):
            seg = tile * n_exp + e
            _segment_copy(xl_ref.at[tile % 2], loff_ref[seg], xs_hbm, goff_ref[seg], cp_ref[seg], sem.at[tile % 2])
            return carry
        lax.fori_loop(0, n_exp, per_expert, 0)

    def wait_segments(tile):
        _wait_rows(xs_hbm, ltot_ref[tile], sem.at[tile % 2])

    riota = lax.broadcasted_iota(I32, (lr, tl), 0)
    for t in range(tiles_per_step):
        tile = j * tiles_per_step + t
        tok = slice(t * tl, (t + 1) * tl)

        @pl.when(tile >= 2)
        def _():
            wait_segments(tile - 2)

        buf = xl_ref.at[tile % 2]
        perm = _one_hot_hits(riota, lambda k: lp_ref[k:k + 1, tok]).astype(F32).astype(BF16)
        rows = _dot(perm, _moe_input(x1_ref, mod_ref, tok).astype(BF16))
        gsel = jnp.zeros((lr, tl), F32)
        for k in range(TOP_K):
            gsel = gsel + jnp.where(riota == lp_ref[k:k + 1, tok], gate_ref[k:k + 1, tok], 0.0)
        rowg = jnp.broadcast_to(jnp.sum(gsel, axis=1, keepdims=True), (lr, LANES))
        buf[:, d // 2:] = lax.bitcast_convert_type(rowg, U32)
        buf[:, 0:d // 2] = _pack_halves(rows)
        start_segments(tile)

    @pl.when(j == pl.num_programs(0) - 1)
    def _():
        last = pl.num_programs(0) * tiles_per_step - 1

        @pl.when(last >= 1)
        def _():
            wait_segments(last - 1)
        wait_segments(last)
        n_zero = lax.fori_loop(0, pad_cnt_ref.shape[0], lambda e, acc: acc + pad_cnt_ref[e], jnp.int32(0))
        _wait_rows(xs_hbm, n_zero, zsem)


def _dispatch(x1, mods_l, seq_len, lp, gates, goff, loff, cp, ltot, pad_start, pad_cnt, n_rows):
    t, d = x1.shape
    tl = min(MOE_TOKENS, t)
    n_exp = goff.shape[0] // (t // tl)
    lr = TOP_K * tl + n_exp * SUBLANES
    tps = _tiles_per_step(seq_len, tl)
    ts = tl * tps
    per_seq = seq_len // ts
    tok = pl.BlockSpec((TOP_K, ts), lambda j, *_: (0, j))
    grid_spec = pltpu.PrefetchScalarGridSpec(
        num_scalar_prefetch=6,
        grid=(t // ts,),
        in_specs=[pl.BlockSpec((ts, d), lambda j, *_: (j, 0)),
                  pl.BlockSpec((None, 6, d), lambda j, *_: (j // per_seq, 0, 0)), tok, tok],
        out_specs=pl.BlockSpec(memory_space=pl.ANY),
        scratch_shapes=[pltpu.VMEM((2, lr, d // 2 + LANES), U32), pltpu.VMEM((MOE_TILE, d // 2 + LANES), U32),
                        pltpu.SemaphoreType.DMA((2,)), pltpu.SemaphoreType.DMA],
    )
    return pl.pallas_call(
        functools.partial(_dispatch_body, n_exp=n_exp, tiles_per_step=tps),
        grid_spec=grid_spec,
        out_shape=jax.ShapeDtypeStruct((n_rows, d // 2 + LANES), U32),
        compiler_params=_params("arbitrary"),
        name="dispatch",
    )(goff, loff, cp, ltot, pad_start, pad_cnt, x1, mods_l, lp, gates)


def _experts_body(be_ref, bsrc_ref, rows_ref, first_ref, slot_ref, next_ref, xs_ref, wgu_hbm, bgu_ref, wdn_hbm,
                  bdn_ref, ys_ref, wgu_f32, wdn_f32, wgu_bf, wdn_bf, sem_gu, sem_dn, *, layer):
    i = pl.program_id(0)
    e = be_ref[i]
    d, f2 = wgu_bf.shape
    f = f2 // 2

    def fetch(expert, slot, start):
        for hbm, buf, sem in ((wgu_hbm, wgu_f32, sem_gu), (wdn_hbm, wdn_f32, sem_dn)):
            dma = pltpu.make_async_copy(hbm.at[layer, expert], buf.at[slot], sem.at[slot])
            dma.start() if start else dma.wait()

    @pl.when(first_ref[i] == 1)
    def _():
        slot = slot_ref[i]

        @pl.when(i == 0)
        def _():
            fetch(e, slot, True)

        fetch(e, slot, False)

        @pl.when(next_ref[i] >= 0)
        def _():
            fetch(next_ref[i], 1 - slot, True)

        def cast(ref_in, ref_out):
            def step(r, c):
                rows = pl.ds(pl.multiple_of(r * CAST_ROWS, CAST_ROWS), CAST_ROWS)
                ref_out[rows, :] = ref_in[rows, :].astype(BF16)
                return c
            lax.fori_loop(0, ref_in.shape[0] // CAST_ROWS, step, 0)
        cast(wgu_f32.at[slot], wgu_bf)
        cast(wdn_f32.at[slot], wdn_bf)

    hd = d // 2
    tm = xs_ref.shape[0]

    def mlp(n):
        x_lo, x_hi = _unpack_halves(xs_ref[0:n, 0:hd])
        proj = lambda cols: _dot(x_lo, wgu_bf[0:hd, cols]) + _dot(x_hi, wgu_bf[hd:, cols]) + bgu_ref[e, :, cols]
        gate = jnp.minimum(proj(slice(0, f)), SWIGLU_LIMIT)
        lin = jnp.clip(proj(slice(f, f2)), -SWIGLU_LIMIT, SWIGLU_LIMIT)
        act = (gate * jax.nn.sigmoid(SWIGLU_ALPHA * gate) * (lin + 1.0)).astype(BF16)
        row_gate = lax.bitcast_convert_type(xs_ref[0:n, hd:hd + 1], F32)
        y = (_dot(act, wdn_bf[...]) + bdn_ref[e]) * row_gate
        ys_ref[0:n, :] = _pack_halves(y.astype(BF16).astype(F32))
        if n < tm:
            ys_ref[n:, :] = jnp.zeros((tm - n, hd), U32)

    rows = rows_ref[i]
    sizes = [tm * e // 8 for e in EXPERT_ROW_EIGHTHS]
    for n, smaller in zip(sizes, sizes[1:] + [0]):
        @pl.when((rows > smaller) & (rows <= n))
        def _(n=n):
            mlp(n)

    @pl.when(rows == 0)
    def _():
        ys_ref[...] = jnp.zeros_like(ys_ref)


def _experts(xs, block_e, block_src, block_rows, block_first, block_slot, block_next, w_gu, b_gu, w_dn, b_dn, layer):
    n_rows, dx = xs.shape
    depth, n_exp, d, f2 = w_gu.shape
    f = f2 // 2
    tm = MOE_TILE
    grid_spec = pltpu.PrefetchScalarGridSpec(
        num_scalar_prefetch=6,
        grid=(n_rows // tm,),
        in_specs=[
            pl.BlockSpec((tm, dx), lambda i, be, bs, *_: (bs[i], 0)),
            pl.BlockSpec(memory_space=pl.ANY),
            pl.BlockSpec((None, n_exp, 1, f2), lambda i, *_: (layer, 0, 0, 0)),
            pl.BlockSpec(memory_space=pl.ANY),
            pl.BlockSpec((None, n_exp, 1, d), lambda i, *_: (layer, 0, 0, 0)),
        ],
        out_specs=pl.BlockSpec((tm, d // 2), lambda i, *_: (i, 0)),
        scratch_shapes=[pltpu.VMEM((2, d, f2), F32), pltpu.VMEM((2, f, d), F32),
                        pltpu.VMEM((d, f2), BF16), pltpu.VMEM((f, d), BF16),
                        pltpu.SemaphoreType.DMA((2,)), pltpu.SemaphoreType.DMA((2,))],
    )
    return pl.pallas_call(
        functools.partial(_experts_body, layer=layer),
        grid_spec=grid_spec,
        out_shape=jax.ShapeDtypeStruct((n_rows, d // 2), U32),
        compiler_params=_params("arbitrary"),
        name="experts",
    )(block_e, block_src, block_rows, block_first, block_slot, block_next, xs, w_gu,
      b_gu.reshape(depth, n_exp, 1, f2), w_dn, b_dn.reshape(depth, n_exp, 1, d))


def _combine_body(goff_ref, loff_ref, cp_ref, ltot_ref, ys_hbm, x_ref, lp_ref, mod_ref, lng_ref, lnb_ref, o_ref,
                  yl_ref, sem, *, n_exp, alpha, tiles_per_step):
    j = pl.program_id(0)
    tl = x_ref.shape[0] // tiles_per_step
    n_tiles = pl.num_programs(0) * tiles_per_step
    lr, hd = yl_ref.shape[1:]

    def fetch(tile):
        buf = yl_ref.at[tile % 2]

        def per_expert(e, carry):
            seg = tile * n_exp + e
            _segment_copy(ys_hbm, goff_ref[seg], buf, loff_ref[seg], cp_ref[seg], sem.at[tile % 2])
            return carry
        lax.fori_loop(0, n_exp, per_expert, 0)

        def zero_rows(r, carry):
            buf[pl.ds(pl.multiple_of(r * SUBLANES, SUBLANES), SUBLANES), :] = jnp.zeros((SUBLANES, hd), U32)
            return carry
        lax.fori_loop(ltot_ref[tile] // SUBLANES, lr // SUBLANES, zero_rows, 0)

    @pl.when(j == 0)
    def _():
        fetch(j)

    liota = lax.broadcasted_iota(I32, (tl, lr), 1)
    for t in range(tiles_per_step):
        tile = j * tiles_per_step + t
        tok = slice(t * tl, (t + 1) * tl)

        @pl.when(tile + 1 < n_tiles)
        def _():
            fetch(tile + 1)

        pick = _one_hot_hits(liota, lambda k: lp_ref[tok, k:k + 1]).astype(F32).astype(BF16)
        _wait_rows(ys_hbm, ltot_ref[tile], sem.at[tile % 2])
        y_lo, y_hi = _unpack_halves(yl_ref[tile % 2])
        y = jnp.concatenate([_dot(pick, y_lo), _dot(pick, y_hi)], axis=1)
        o_ref[tok, :] = _layer_norm(alpha * x_ref[tok, :] + mod_ref[5:6, :] * y, lng_ref[...], lnb_ref[...])


def _combine(ys, lp_t, goff, loff, cp, ltot, x1, mods_l, ln_g, ln_b, alpha, seq_len):
    t, d = x1.shape
    tl = min(MOE_TOKENS, t)
    n_exp = goff.shape[0] // (t // tl)
    lr = TOP_K * tl + n_exp * SUBLANES
    tps = _tiles_per_step(seq_len, tl)
    ts = tl * tps
    per_seq = seq_len // ts
    vec = pl.BlockSpec((1, d), lambda j, *_: (0, 0))
    grid_spec = pltpu.PrefetchScalarGridSpec(
        num_scalar_prefetch=4,
        grid=(t // ts,),
        in_specs=[
            pl.BlockSpec(memory_space=pl.ANY),
            pl.BlockSpec((ts, d), lambda j, *_: (j, 0)),
            pl.BlockSpec((ts, TOP_K), lambda j, *_: (j, 0)),
            pl.BlockSpec((None, 6, d), lambda j, *_: (j // per_seq, 0, 0)),
            vec, vec,
        ],
        out_specs=pl.BlockSpec((ts, d), lambda j, *_: (j, 0)),
        scratch_shapes=[pltpu.VMEM((2, lr, d // 2), U32), pltpu.SemaphoreType.DMA((2,))],
    )
    return pl.pallas_call(
        functools.partial(_combine_body, n_exp=n_exp, alpha=alpha, tiles_per_step=tps),
        grid_spec=grid_spec,
        out_shape=jax.ShapeDtypeStruct((t, d), F32),
        compiler_params=_params("arbitrary"),
        name="combine",
    )(goff, loff, cp, ltot, ys, x1, lp_t, mods_l, ln_g.reshape(1, d), ln_b.reshape(1, d))


def _moe_layer(x1, mods_l, w_r, b_r, w_gu, b_gu, w_dn, b_dn, layer, ln_g, ln_b, alpha, seq_len):
    t, d = x1.shape
    n_exp = w_r.shape[1]
    tm = MOE_TILE
    tl = min(MOE_TOKENS, t)
    n_tiles = t // tl
    idx, gates, rank, cnt = _router(x1, mods_l, seq_len, w_r, b_r)
    cp = (cnt[:, :, 0] + SUBLANES - 1) // SUBLANES * SUBLANES
    tot = jnp.sum(cp, axis=0)
    padded = (tot + tm - 1) // tm * tm
    pend = jnp.cumsum(padded)
    pstart = pend - padded
    goff = pstart[None, :] + jnp.cumsum(cp, axis=0) - cp
    loff = jnp.cumsum(cp, axis=1) - cp
    ltot = jnp.sum(cp, axis=1).astype(I32)
    is_e = idx[..., None] == jnp.arange(n_exp, dtype=I32)
    lp = jnp.sum(jnp.where(is_e, jnp.repeat(loff, tl, axis=0)[None], 0), axis=-1) + rank
    n_blocks = -(-(t * TOP_K + n_tiles * n_exp * (SUBLANES - 1)) // tm) + n_exp
    n_valid = pend[-1] // tm
    blk = jnp.arange(n_blocks, dtype=I32)
    block_src = jnp.minimum(blk, n_valid - 1)
    block_e = jnp.sum(pend[None, :] <= (block_src * tm)[:, None], axis=1).astype(I32)
    used = padded > 0
    experts = jnp.arange(n_exp, dtype=I32)
    next_used = jnp.flip(lax.cummin(jnp.flip(jnp.where(used, experts, n_exp))))
    next_used = jnp.concatenate([next_used[1:], jnp.full((1,), n_exp, I32)])
    next_used = jnp.where(next_used < n_exp, next_used, -1)
    slot_of = (jnp.cumsum(used.astype(I32)) - 1) % 2
    of_block = lambda table: jnp.sum(jnp.where(block_e[:, None] == experts[None, :], table[None, :], 0), axis=1)
    block_first = ((blk * tm == of_block(pstart)) & (blk < n_valid)).astype(I32)
    block_slot = of_block(slot_of).astype(I32)
    block_next = of_block(next_used).astype(I32)
    block_rows = jnp.where(blk < n_valid, jnp.clip(of_block(pstart + tot) - blk * tm, 0, tm), 0).astype(I32)
    n_rows = n_blocks * tm
    pad_start = jnp.concatenate([pstart + tot, pend[-1:]]).astype(I32)
    pad_cnt = jnp.concatenate([padded - tot, n_rows - pend[-1:]]).astype(I32)
    flat = lambda a: a.reshape(-1).astype(I32)
    seg = (flat(goff), flat(loff), flat(cp), ltot)
    xs = _dispatch(x1, mods_l, seq_len, lp, gates, *seg, pad_start, pad_cnt, n_rows)
    ys = _experts(xs, block_e, block_src.astype(I32), block_rows, block_first, block_slot, block_next,
                  w_gu, b_gu, w_dn, b_dn, layer)
    return _combine(ys, lp.T, *seg, x1, mods_l, ln_g, ln_b, alpha, seq_len)


def kernel(x, c, ada_w, ada_b, post_ln_g, post_ln_b, conv_w_pw1, conv_b_pw1, conv_w_dw, conv_b_dw, conv_ln_g, conv_ln_b, conv_w_pw2, conv_b_pw2, w_kv, attn_w_q, attn_lambda, attn_subln_g, attn_w_o, rel_bias_table, router_w, router_b, expert_w_gate_up, expert_b_gate_up, expert_w_down, expert_b_down):
    bsz, s, d = x.shape
    depth = ada_w.shape[0]
    n_a = depth // 2
    alpha = (2 * depth) ** 0.25
    mods = _ada(c, ada_w, ada_b).reshape(depth, bsz, 6, d)
    q = k = vt = None
    for l in range(depth):
        mods_l = mods[l]
        if l < n_a:
            x1 = _conv(x, mods_l, conv_w_pw1[l], conv_b_pw1[l], conv_w_dw[l], conv_b_dw[l], conv_ln_g[l],
                       conv_ln_b[l], conv_w_pw2[l], conv_b_pw2[l], post_ln_g[l, 0], post_ln_b[l, 0], alpha)
        else:
            j = l - n_a
            if j == 0:
                q, k, vt = _qkv(x, mods_l, w_kv, attn_w_q[j])
            else:
                q = _qkv(x, mods_l, w_kv, attn_w_q[j])[0]
            lambda_init = 0.8 - 0.6 * math.exp(-0.3 * l)
            ot = _attn(q, k, vt, attn_lambda[j], attn_subln_g[j], rel_bias_table, lambda_init)
            x1 = _attn_out(ot, x, mods_l, attn_w_o[j], post_ln_g[l, 0], post_ln_b[l, 0], alpha)
        x = _moe_layer(x1.reshape(bsz * s, d), mods_l, router_w[l], router_b[l],
                       expert_w_gate_up, expert_b_gate_up, expert_w_down, expert_b_down, l,
                       post_ln_g[l, 1], post_ln_b[l, 1], alpha, s).reshape(bsz, s, d)
    return x
```

```python
import functools
import math

import jax
import jax.numpy as jnp
from jax import lax
from jax.experimental import pallas as pl
from jax.experimental.pallas import tpu as pltpu

F32 = jnp.float32
BF16 = jnp.bfloat16
I32 = jnp.int32
U32 = jnp.uint32
HIGHEST = lax.Precision.HIGHEST

CHUNK = 64
CONV_WIDTH = 31
HEAD_DIM = 64
REL_BUCKETS = 32
REL_MAX_DIST = 128
TOP_K = 4
SWIGLU_LIMIT = 7.0
SWIGLU_ALPHA = 1.702
LN_EPS = 1e-5
MASK_VALUE = -1e30
LOG2_E = math.log2(math.e)

SUBLANES = 8
LANES = 128
VMEM_LIMIT_BYTES = 56 * 1024 * 1024

ADA_TN = 1024
SEQ_TILE = 512
PROJ_TILE = 1024
CONV_HALO = 32
CONV_ROWS = 256
CONV_COLS = 128
MOE_TOKENS = 256
ROUTER_TOKENS = 2048
MOE_TILES_PER_STEP = 2
MOE_TILE = 512
EXPERT_ROW_EIGHTHS = (8, 6, 4, 2)
Q_BLOCK = 128
GLU_LOOKAHEAD = 1
SCORE_LOOKAHEAD = 3
CAST_ROWS = 128

HALF_BITS = 16
HIGH_HALF_MASK = 0xFFFF0000


def _params(*sem):
    return pltpu.CompilerParams(dimension_semantics=sem, vmem_limit_bytes=VMEM_LIMIT_BYTES)


def _layer_norm(x, g, b):
    mu = jnp.mean(x, axis=-1, keepdims=True)
    xc = x - mu
    var = jnp.mean(xc * xc, axis=-1, keepdims=True)
    return xc * lax.rsqrt(var + LN_EPS) * g + b


def _dot(a, b):
    return jnp.dot(a, b, preferred_element_type=F32)


def _ada_body(c_ref, w_ref, b_ref, o_ref):
    c = c_ref[...]
    cond = c * jax.nn.sigmoid(c)
    o_ref[...] = jnp.dot(cond, w_ref[...], preferred_element_type=F32, precision=HIGHEST) + b_ref[...]


def _ada(c, ada_w, ada_b):
    depth, d, n = ada_w.shape
    bsz = c.shape[0]
    tn = min(ADA_TN, n)
    return pl.pallas_call(
        _ada_body,
        grid=(depth, n // tn),
        in_specs=[
            pl.BlockSpec((bsz, d), lambda l, j: (0, 0)),
            pl.BlockSpec((None, d, tn), lambda l, j: (l, 0, j)),
            pl.BlockSpec((None, 1, tn), lambda l, j: (l, 0, j)),
        ],
        out_specs=pl.BlockSpec((None, bsz, tn), lambda l, j: (l, 0, j)),
        out_shape=jax.ShapeDtypeStruct((depth, bsz, n), F32),
        compiler_params=_params("parallel", "parallel"),
        name="ada",
    )(c, ada_w, ada_b.reshape(depth, 1, n))


def _residual_epilogue(x, y, mod_ref, lng_ref, lnb_ref, alpha, x1_ref):
    x1_ref[...] = _layer_norm(alpha * x + mod_ref[2:3, :] * y, lng_ref[...], lnb_ref[...])


def _moe_input(x1_ref, mod_ref, rows=slice(None)):
    return x1_ref[rows, :] * (1.0 + mod_ref[4:5, :]) + mod_ref[3:4, :]


def _conv_body(x_ref, halo_ref, mod_ref, w1_ref, b1_ref, wdw_ref, bdw_ref, cg_ref, cb_ref, w2_ref, b2_ref,
               lng_ref, lnb_ref, x1_ref, win_ref, v_ref, *, alpha):
    ts, d = x_ref.shape
    i = pl.program_id(1)
    xw = jnp.concatenate([halo_ref[...], x_ref[...]], axis=0)
    h = (xw * (1.0 + mod_ref[1:2, :]) + mod_ref[0:1, :]).astype(BF16)
    rows = min(CONV_ROWS, ts)
    cols = min(CONV_COLS, d)

    def glu(c0):
        a = _dot(h, w1_ref[:, c0:c0 + cols]) + b1_ref[:, c0:c0 + cols]
        g = _dot(h, w1_ref[:, d + c0:d + c0 + cols]) + b1_ref[:, d + c0:d + c0 + cols]
        return a * jax.nn.sigmoid(g)

    off = CONV_HALO - (CONV_WIDTH - 1)
    n_shift = ts + CONV_HALO - SUBLANES
    in_seq = (lax.broadcasted_iota(I32, (ts + CONV_HALO, cols), 0) >= CONV_HALO) | (i > 0)
    starts = list(range(0, d, cols))
    ahead = [glu(c) for c in starts[:GLU_LOOKAHEAD]]
    for n, c0 in enumerate(starts):
        chan = slice(c0, c0 + cols)
        win_ref[0, :, chan] = jnp.where(in_seq, ahead.pop(0), 0.0)
        if n + GLU_LOOKAHEAD < len(starts):
            ahead.append(glu(starts[n + GLU_LOOKAHEAD]))
        for b in range(1, SUBLANES):
            win_ref[b, 0:n_shift, chan] = win_ref[0, b:b + n_shift, chan]
        for r0 in range(0, ts, rows):
            acc = jnp.zeros((rows, cols), F32)
            for j in range(CONV_WIDTH):
                a, b = divmod(off + j, SUBLANES)
                r = r0 + a * SUBLANES
                acc = acc + wdw_ref[j:j + 1, c0:c0 + cols] * win_ref[b, r:r + rows, c0:c0 + cols]
            v_ref[r0:r0 + rows, c0:c0 + cols] = acc
    v = _layer_norm(v_ref[...] + bdw_ref[...], cg_ref[...], cb_ref[...])
    v = (v * jax.nn.sigmoid(v)).astype(BF16)
    y = _dot(v, w2_ref[...]) + b2_ref[...]
    _residual_epilogue(x_ref[...], y, mod_ref, lng_ref, lnb_ref, alpha, x1_ref)


def _conv(x, mods_l, w_pw1, b_pw1, w_dw, b_dw, cln_g, cln_b, w_pw2, b_pw2, ln_g, ln_b, alpha):
    bsz, s, d = x.shape
    ts = min(SEQ_TILE, s)
    hb = ts // CONV_HALO
    row = lambda a: a.reshape(1, d)
    tile = pl.BlockSpec((None, ts, d), lambda b, i: (b, i, 0))
    vec = pl.BlockSpec((1, d), lambda b, i: (0, 0))
    return pl.pallas_call(
        functools.partial(_conv_body, alpha=alpha),
        grid=(bsz, s // ts),
        in_specs=[
            tile,
            pl.BlockSpec((None, CONV_HALO, d), lambda b, i: (b, jnp.maximum(i * hb - 1, 0), 0)),
            pl.BlockSpec((None, 6, d), lambda b, i: (b, 0, 0)),
            pl.BlockSpec((d, 2 * d), lambda b, i: (0, 0)),
            pl.BlockSpec((1, 2 * d), lambda b, i: (0, 0)),
            pl.BlockSpec((CONV_WIDTH, d), lambda b, i: (0, 0)),
            vec, vec, vec,
            pl.BlockSpec((d, d), lambda b, i: (0, 0)),
            vec, vec, vec,
        ],
        out_specs=tile,
        out_shape=jax.ShapeDtypeStruct((bsz, s, d), F32),
        scratch_shapes=[pltpu.VMEM((SUBLANES, ts + CONV_HALO, d), F32), pltpu.VMEM((ts, d), F32)],
        compiler_params=_params("parallel", "parallel"),
        name="conv",
    )(x, x, mods_l, w_pw1.astype(BF16), b_pw1.reshape(1, 2 * d), w_dw, row(b_dw), row(cln_g), row(cln_b),
      w_pw2.astype(BF16), row(b_pw2), row(ln_g), row(ln_b))


def _qkv_body(x_ref, mod_ref, wkv_ref, wq_ref, q_ref, k_ref, vt_ref):
    d = x_ref.shape[-1]
    x = x_ref[...]
    kv = _dot(x.astype(BF16), wkv_ref[...])
    k_ref[...] = kv[:, :d].astype(BF16)
    hd2 = 2 * HEAD_DIM
    for c0 in range(0, d, hd2):
        vt_ref[c0:c0 + hd2, :] = kv[:, d + c0:d + c0 + hd2].T.astype(BF16)
    h = (x * (1.0 + mod_ref[1:2, :]) + mod_ref[0:1, :]).astype(BF16)
    q_ref[...] = (_dot(h, wq_ref[...]) * (HEAD_DIM ** -0.5 * LOG2_E)).astype(BF16)


def _qkv(x, mods_l, w_kv, w_q):
    bsz, s, d = x.shape
    ts = min(PROJ_TILE, s)
    tile = pl.BlockSpec((None, ts, d), lambda b, i: (b, i, 0))
    return pl.pallas_call(
        _qkv_body,
        grid=(bsz, s // ts),
        in_specs=[
            tile,
            pl.BlockSpec((None, 6, d), lambda b, i: (b, 0, 0)),
            pl.BlockSpec((d, 2 * d), lambda b, i: (0, 0)),
            pl.BlockSpec((d, d), lambda b, i: (0, 0)),
        ],
        out_specs=[tile, tile, pl.BlockSpec((None, d, ts), lambda b, i: (b, 0, i))],
        out_shape=[jax.ShapeDtypeStruct((bsz, s, d), BF16)] * 2 + [jax.ShapeDtypeStruct((bsz, d, s), BF16)],
        compiler_params=_params("parallel", "parallel"),
        name="qkv",
    )(x, mods_l, w_kv.astype(BF16), w_q.astype(BF16))


def _t5_bucket(rel):
    nb = REL_BUCKETS // 2
    ret = jnp.where(rel > 0, nb, 0)
    n = jnp.abs(rel)
    max_exact = nb // 2
    large = max_exact + (jnp.log(jnp.maximum(n, 1).astype(F32) / max_exact)
                         / math.log(REL_MAX_DIST / max_exact) * (nb - max_exact)).astype(I32)
    large = jnp.minimum(large, nb - 1)
    return ret + jnp.where(n < max_exact, n, large)


def _saturation_distance():
    nb = REL_BUCKETS // 2
    max_exact = nb // 2
    n = max_exact
    while max_exact + math.log(n / max_exact) / math.log(REL_MAX_DIST / max_exact) * (nb - max_exact) < nb - 1 + 1e-3:
        n += 1
    return n


def _bucket_strip(s):
    r = jnp.arange(Q_BLOCK, dtype=I32)[:, None]
    kp = jnp.arange(s, dtype=I32)[None, :] - (s - Q_BLOCK)
    bucket = _t5_bucket(kp - r)
    visible = jnp.floor_divide(kp, CHUNK) <= (r // CHUNK)
    return jnp.where(visible, bucket, REL_BUCKETS)


def _attn_body(tab_ref, q_ref, k_ref, vt_ref, bkt_ref, lam_ref, sg_ref, ot_ref, bias_ref, *, lambda_init, n_heads):
    h = pl.program_id(0)
    b = pl.program_id(1)
    s = q_ref.shape[0]

    near = bias_ref.shape[0]

    @pl.when(b == 0)
    def _():
        bk = bkt_ref[...]
        acc = jnp.full(bk.shape, MASK_VALUE, F32)
        for r in range(REL_BUCKETS):
            acc = jnp.where(bk == r, tab_ref[r * n_heads + h] * LOG2_E, acc)
        bias_ref[:, 0:Q_BLOCK] = acc
        bias_ref[:, Q_BLOCK:] = acc

    far_bias = tab_ref[(REL_BUCKETS // 2 - 1) * n_heads + h] * LOG2_E

    lp = lam_ref[...]
    lam = (jnp.exp(jnp.sum(lp[0:1, :] * lp[1:2, :], axis=-1, keepdims=True))
           - jnp.exp(jnp.sum(lp[2:3, :] * lp[3:4, :], axis=-1, keepdims=True)) + lambda_init)
    lane = lax.broadcasted_iota(I32, (Q_BLOCK, 2 * HEAD_DIM), 1)
    nt = (((1,), (1,)), ((), ()))

    def scores(i):
        n_keys = (i + 1) * Q_BLOCK
        q = q_ref[i * Q_BLOCK:(i + 1) * Q_BLOCK, :]
        qq = jnp.concatenate([jnp.where(lane < HEAD_DIM, q, jnp.zeros_like(q)),
                              jnp.where(lane >= HEAD_DIM, q, jnp.zeros_like(q))], axis=0)
        return lax.dot_general(k_ref[0:n_keys, :], qq, nt, preferred_element_type=F32)

    def values(i, p, denom):
        pv = _dot(vt_ref[:, 0:(i + 1) * Q_BLOCK], p)
        o = pv[:, :Q_BLOCK] * (1.0 / denom[:, :Q_BLOCK]) - pv[:, Q_BLOCK:] * (lam / denom[:, Q_BLOCK:])
        o = o * lax.rsqrt(jnp.mean(o * o, axis=0, keepdims=True) + LN_EPS) * sg_ref[...]
        ot_ref[:, i * Q_BLOCK:(i + 1) * Q_BLOCK] = (o * (1.0 - lambda_init)).astype(BF16)

    n_blocks = s // Q_BLOCK
    ahead = [scores(i) for i in range(min(SCORE_LOOKAHEAD, n_blocks))]
    for i in range(n_blocks):
        raw = ahead.pop(0)
        if i + SCORE_LOOKAHEAD < n_blocks:
            ahead.append(scores(i + SCORE_LOOKAHEAD))
        n_keys = (i + 1) * Q_BLOCK
        n_near = min(near, n_keys)
        n_far = n_keys - n_near
        sc = raw[n_far:] + bias_ref[near - n_near:, :]
        top = jnp.max(sc, axis=0, keepdims=True)
        if n_far:
            top = jnp.maximum(top, jnp.max(raw[:n_far], axis=0, keepdims=True) + far_bias)
            p = jnp.concatenate([jnp.exp2(raw[:n_far] - (top - far_bias)), jnp.exp2(sc - top)], axis=0)
        else:
            p = jnp.exp2(sc - top)
        values(i, p.astype(BF16), jnp.sum(p, axis=0, keepdims=True))


def _attn(q, k, vt, lam_p, subln_g, rel_table, lambda_init):
    bsz, s, d = q.shape
    hd2 = 2 * HEAD_DIM
    n_heads = d // hd2
    assert REL_MAX_DIST >= CHUNK and _saturation_distance() <= REL_MAX_DIST
    near = min(Q_BLOCK + REL_MAX_DIST, s)
    head = pl.BlockSpec((None, s, hd2), lambda h, b, tab: (b, 0, h))
    head_t = pl.BlockSpec((None, hd2, s), lambda h, b, tab: (b, h, 0))
    grid_spec = pltpu.PrefetchScalarGridSpec(
        num_scalar_prefetch=1,
        grid=(n_heads, bsz),
        in_specs=[
            head, head, head_t,
            pl.BlockSpec((near, Q_BLOCK), lambda h, b, tab: (0, 0)),
            pl.BlockSpec((4, HEAD_DIM), lambda h, b, tab: (0, 0)),
            pl.BlockSpec((hd2, 1), lambda h, b, tab: (0, 0)),
        ],
        out_specs=head_t,
        scratch_shapes=[pltpu.VMEM((near, 2 * Q_BLOCK), F32)],
    )
    return pl.pallas_call(
        functools.partial(_attn_body, lambda_init=lambda_init, n_heads=n_heads),
        grid_spec=grid_spec,
        out_shape=jax.ShapeDtypeStruct((bsz, d, s), BF16),
        compiler_params=_params("arbitrary", "arbitrary"),
        name="attn",
    )(rel_table.reshape(-1), q, k, vt, _bucket_strip(s).T[s - near:], lam_p, subln_g.reshape(hd2, 1))


def _attn_out_body(ot_ref, x_ref, mod_ref, wo_ref, lng_ref, lnb_ref, x1_ref, *, alpha):
    y = lax.dot_general(ot_ref[...], wo_ref[...], (((0,), (0,)), ((), ())), preferred_element_type=F32)
    _residual_epilogue(x_ref[...], y, mod_ref, lng_ref, lnb_ref, alpha, x1_ref)


def _attn_out(o, x, mods_l, w_o, ln_g, ln_b, alpha):
    bsz, s, d = x.shape
    ts = min(PROJ_TILE, s)
    tile = pl.BlockSpec((None, ts, d), lambda b, i: (b, i, 0))
    vec = pl.BlockSpec((1, d), lambda b, i: (0, 0))
    return pl.pallas_call(
        functools.partial(_attn_out_body, alpha=alpha),
        grid=(bsz, s // ts),
        in_specs=[pl.BlockSpec((None, d, ts), lambda b, i: (b, 0, i)), tile,
                  pl.BlockSpec((None, 6, d), lambda b, i: (b, 0, 0)),
                  pl.BlockSpec((d, d), lambda b, i: (0, 0)), vec, vec],
        out_specs=tile,
        out_shape=jax.ShapeDtypeStruct((bsz, s, d), F32),
        compiler_params=_params("parallel", "parallel"),
        name="attn_out",
    )(o, x, mods_l, w_o.astype(BF16), ln_g.reshape(1, d), ln_b.reshape(1, d))


def _router_body(x1_ref, mod_ref, whi_ref, wlo_ref, b_ref, tri_ref, idx_ref, gate_ref, rank_ref, cnt_ref):
    h = _moe_input(x1_ref, mod_ref)
    h_hi = h.astype(BF16)
    h_lo = (h - h_hi.astype(F32)).astype(BF16)
    nt = (((1,), (1,)), ((), ()))
    logits = (lax.dot_general(whi_ref[...], h_hi, nt, preferred_element_type=F32)
              + (lax.dot_general(whi_ref[...], h_lo, nt, preferred_element_type=F32)
                 + lax.dot_general(wlo_ref[...], h_hi, nt, preferred_element_type=F32))) + b_ref[...]
    n_exp, tr = logits.shape
    eio = lax.broadcasted_iota(I32, (n_exp, tr), 0)
    work = logits
    vals, idxs = [], []
    for _ in range(TOP_K):
        m = jnp.max(work, axis=0, keepdims=True)
        am = jnp.min(jnp.where(work == m, eio, n_exp), axis=0, keepdims=True)
        vals.append(m)
        idxs.append(am)
        work = jnp.where(eio == am, -jnp.inf, work)
    ex = [jnp.exp(v - vals[0]) for v in vals]
    den = ex[0] + ex[1] + ex[2] + ex[3]
    onehot = jnp.zeros((n_exp, tr), F32)
    for k in range(TOP_K):
        onehot = onehot + (eio == idxs[k]).astype(F32)
    tl = tri_ref.shape[0]
    tiles = [onehot[:, t0:t0 + tl] for t0 in range(0, tr, tl)]
    before = jnp.concatenate([_dot(oh.astype(BF16), tri_ref[...]) for oh in tiles], axis=1)
    for k in range(TOP_K):
        idx_ref[k:k + 1, :] = idxs[k]
        gate_ref[k:k + 1, :] = ex[k] / den
        rank_ref[k:k + 1, :] = jnp.sum(jnp.where(eio == idxs[k], before, 0.0), axis=0, keepdims=True).astype(I32)
    for j, oh in enumerate(tiles):
        cnt_ref[j] = jnp.sum(oh, axis=1, keepdims=True).astype(I32)


def _router(x1, mods_l, seq_len, w_r, b_r):
    t, d = x1.shape
    n_exp = w_r.shape[1]
    tl = min(MOE_TOKENS, t)
    tr = min(ROUTER_TOKENS, seq_len)
    per_seq = seq_len // tr
    wt = w_r.T
    wt_hi = wt.astype(BF16)
    wt_lo = (wt - wt_hi.astype(F32)).astype(BF16)
    pos = jnp.arange(tl, dtype=I32)
    tri = (pos[:, None] < pos[None, :]).astype(BF16)
    tok = pl.BlockSpec((TOP_K, tr), lambda i: (0, i))
    return pl.pallas_call(
        _router_body,
        grid=(t // tr,),
        in_specs=[
            pl.BlockSpec((tr, d), lambda i: (i, 0)),
            pl.BlockSpec((None, 6, d), lambda i: (i // per_seq, 0, 0)),
            pl.BlockSpec((n_exp, d), lambda i: (0, 0)),
            pl.BlockSpec((n_exp, d), lambda i: (0, 0)),
            pl.BlockSpec((n_exp, 1), lambda i: (0, 0)),
            pl.BlockSpec((tl, tl), lambda i: (0, 0)),
        ],
        out_specs=[tok, tok, tok, pl.BlockSpec((tr // tl, n_exp, 1), lambda i: (i, 0, 0))],
        out_shape=[jax.ShapeDtypeStruct((TOP_K, t), I32), jax.ShapeDtypeStruct((TOP_K, t), F32),
                   jax.ShapeDtypeStruct((TOP_K, t), I32), jax.ShapeDtypeStruct((t // tl, n_exp, 1), I32)],
        compiler_params=_params("parallel"),
        name="router",
    )(x1, mods_l, wt_hi, wt_lo, b_r.reshape(n_exp, 1), tri)


def _rows(ref, start, cnt):
    aligned = lambda v: v if isinstance(v, int) else pl.multiple_of(v, SUBLANES)
    return ref.at[pl.ds(aligned(start), aligned(cnt))]


def _segment_copy(src, src_row, dst, dst_row, cnt, sem):
    @pl.when(cnt > 0)
    def _():
        pltpu.make_async_copy(_rows(src, src_row, cnt), _rows(dst, dst_row, cnt), sem).start()


def _wait_rows(ref, cnt, sem):
    @pl.when(cnt > 0)
    def _():
        pltpu.make_async_copy(_rows(ref, 0, cnt), _rows(ref, 0, cnt), sem).wait()


def _zero_fill(zero_ref, xs_hbm, start, cnt, sem):
    zr = zero_ref.shape[0]
    n_full = cnt // zr

    def full(r, c):
        _segment_copy(zero_ref, 0, xs_hbm, start + r * zr, zr, sem)
        return c

    lax.fori_loop(0, n_full, full, 0)
    _segment_copy(zero_ref, 0, xs_hbm, start + n_full * zr, cnt - n_full * zr, sem)


def _pack_halves(x):
    hd = x.shape[1] // 2
    lo = lax.bitcast_convert_type(x[:, :hd], U32)
    hi = lax.bitcast_convert_type(x[:, hd:], U32)
    return (lo >> HALF_BITS) | (hi & jnp.uint32(HIGH_HALF_MASK))


def _unpack_halves(words):
    lo = lax.bitcast_convert_type(words << HALF_BITS, F32).astype(BF16)
    hi = lax.bitcast_convert_type(words & jnp.uint32(HIGH_HALF_MASK), F32).astype(BF16)
    return lo, hi


def _tiles_per_step(seq_len, tl):
    return MOE_TILES_PER_STEP if seq_len % (tl * MOE_TILES_PER_STEP) == 0 else 1


def _one_hot_hits(iota, pos_of):
    hit = iota == pos_of(0)
    for k in range(1, TOP_K):
        hit = hit | (iota == pos_of(k))
    return hit


def _dispatch_body(goff_ref, loff_ref, cp_ref, ltot_ref, pad_start_ref, pad_cnt_ref, x1_ref, mod_ref, lp_ref,
                   gate_ref, xs_hbm, xl_ref, zero_ref, sem, zsem, *, n_exp, tiles_per_step):
    j = pl.program_id(0)
    d = x1_ref.shape[1]
    tl = x1_ref.shape[0] // tiles_per_step
    lr = xl_ref.shape[1]

    @pl.when(j == 0)
    def _():
        zero_ref[...] = jnp.zeros_like(zero_ref)

        def per_range(e, carry):
            _zero_fill(zero_ref, xs_hbm, pad_start_ref[e], pad_cnt_ref[e], zsem)
            return carry
        lax.fori_loop(0, pad_start_ref.shape[0], per_range, 0)

    def start_segments(tile):
        def per_expert(e, carry):
            seg = tile * n_exp + e
            _segment_copy(xl_ref.at[tile % 2], loff_ref[seg], xs_hbm, goff_ref[seg], cp_ref[seg], sem.at[tile % 2])
            return carry
        lax.fori_loop(0, n_exp, per_expert, 0)

    def wait_segments(tile):
        _wait_rows(xs_hbm, ltot_ref[tile], sem.at[tile % 2])

    riota = lax.broadcasted_iota(I32, (lr, tl), 0)
    for t in range(tiles_per_step):
        tile = j * tiles_per_step + t
        tok = slice(t * tl, (t + 1) * tl)

        @pl.when(tile >= 2)
        def _():
            wait_segments(tile - 2)

        buf = xl_ref.at[tile % 2]
        perm = _one_hot_hits(riota, lambda k: lp_ref[k:k + 1, tok]).astype(F32).astype(BF16)
        rows = _dot(perm, _moe_input(x1_ref, mod_ref, tok).astype(BF16))
        gsel = jnp.zeros((lr, tl), F32)
        for k in range(TOP_K):
            gsel = gsel + jnp.where(riota == lp_ref[k:k + 1, tok], gate_ref[k:k + 1, tok], 0.0)
        rowg = jnp.broadcast_to(jnp.sum(gsel, axis=1, keepdims=True), (lr, LANES))
        buf[:, d // 2:] = lax.bitcast_convert_type(rowg, U32)
        buf[:, 0:d // 2] = _pack_halves(rows)
        start_segments(tile)

    @pl.when(j == pl.num_programs(0) - 1)
    def _():
        last = pl.num_programs(0) * tiles_per_step - 1

        @pl.when(last >= 1)
        def _():
            wait_segments(last - 1)
        wait_segments(last)
        n_zero = lax.fori_loop(0, pad_cnt_ref.shape[0], lambda e, acc: acc + pad_cnt_ref[e], jnp.int32(0))
        _wait_rows(xs_hbm, n_zero, zsem)


def _dispatch(x1, mods_l, seq_len, lp, gates, goff, loff, cp, ltot, pad_start, pad_cnt, n_rows):
    t, d = x1.shape
    tl = min(MOE_TOKENS, t)
    n_exp = goff.shape[0] // (t // tl)
    lr = TOP_K * tl + n_exp * SUBLANES
    tps = _tiles_per_step(seq_len, tl)
    ts = tl * tps
    per_seq = seq_len // ts
    tok = pl.BlockSpec((TOP_K, ts), lambda j, *_: (0, j))
    grid_spec = pltpu.PrefetchScalarGridSpec(
        num_scalar_prefetch=6,
        grid=(t // ts,),
        in_specs=[pl.BlockSpec((ts, d), lambda j, *_: (j, 0)),
                  pl.BlockSpec((None, 6, d), lambda j, *_: (j // per_seq, 0, 0)), tok, tok],
        out_specs=pl.BlockSpec(memory_space=pl.ANY),
        scratch_shapes=[pltpu.VMEM((2, lr, d // 2 + LANES), U32), pltpu.VMEM((MOE_TILE, d // 2 + LANES), U32),
                        pltpu.SemaphoreType.DMA((2,)), pltpu.SemaphoreType.DMA],
    )
    return pl.pallas_call(
        functools.partial(_dispatch_body, n_exp=n_exp, tiles_per_step=tps),
        grid_spec=grid_spec,
        out_shape=jax.ShapeDtypeStruct((n_rows, d // 2 + LANES), U32),
        compiler_params=_params("arbitrary"),
        name="dispatch",
    )(goff, loff, cp, ltot, pad_start, pad_cnt, x1, mods_l, lp, gates)


def _experts_body(be_ref, bsrc_ref, rows_ref, first_ref, slot_ref, next_ref, xs_ref, wgu_hbm, bgu_ref, wdn_hbm,
                  bdn_ref, ys_ref, wgu_f32, wdn_f32, wgu_bf, wdn_bf, sem_gu, sem_dn, *, layer):
    i = pl.program_id(0)
    e = be_ref[i]
    d, f2 = wgu_bf.shape
    f = f2 // 2

    def fetch(expert, slot, start):
        for hbm, buf, sem in ((wgu_hbm, wgu_f32, sem_gu), (wdn_hbm, wdn_f32, sem_dn)):
            dma = pltpu.make_async_copy(hbm.at[layer, expert], buf.at[slot], sem.at[slot])
            dma.start() if start else dma.wait()

    @pl.when(first_ref[i] == 1)
    def _():
        slot = slot_ref[i]

        @pl.when(i == 0)
        def _():
            fetch(e, slot, True)

        fetch(e, slot, False)

        @pl.when(next_ref[i] >= 0)
        def _():
            fetch(next_ref[i], 1 - slot, True)

        def cast(ref_in, ref_out):
            def step(r, c):
                rows = pl.ds(pl.multiple_of(r * CAST_ROWS, CAST_ROWS), CAST_ROWS)
                ref_out[rows, :] = ref_in[rows, :].astype(BF16)
                return c
            lax.fori_loop(0, ref_in.shape[0] // CAST_ROWS, step, 0)
        cast(wgu_f32.at[slot], wgu_bf)
        cast(wdn_f32.at[slot], wdn_bf)

    hd = d // 2
    tm = xs_ref.shape[0]

    def mlp(n):
        x_lo, x_hi = _unpack_halves(xs_ref[0:n, 0:hd])
        proj = lambda cols: _dot(x_lo, wgu_bf[0:hd, cols]) + _dot(x_hi, wgu_bf[hd:, cols]) + bgu_ref[e, :, cols]
        gate = jnp.minimum(proj(slice(0, f)), SWIGLU_LIMIT)
        lin = jnp.clip(proj(slice(f, f2)), -SWIGLU_LIMIT, SWIGLU_LIMIT)
        act = (gate * jax.nn.sigmoid(SWIGLU_ALPHA * gate) * (lin + 1.0)).astype(BF16)
        row_gate = lax.bitcast_convert_type(xs_ref[0:n, hd:hd + 1], F32)
        y = (_dot(act, wdn_bf[...]) + bdn_ref[e]) * row_gate
        ys_ref[0:n, :] = _pack_halves(y.astype(BF16).astype(F32))
        if n < tm:
            ys_ref[n:, :] = jnp.zeros((tm - n, hd), U32)

    rows = rows_ref[i]
    sizes = [tm * e // 8 for e in EXPERT_ROW_EIGHTHS]
    for n, smaller in zip(sizes, sizes[1:] + [0]):
        @pl.when((rows > smaller) & (rows <= n))
        def _(n=n):
            mlp(n)

    @pl.when(rows == 0)
    def _():
        ys_ref[...] = jnp.zeros_like(ys_ref)


def _experts(xs, block_e, block_src, block_rows, block_first, block_slot, block_next, w_gu, b_gu, w_dn, b_dn, layer):
    n_rows, dx = xs.shape
    depth, n_exp, d, f2 = w_gu.shape
    f = f2 // 2
    tm = MOE_TILE
    grid_spec = pltpu.PrefetchScalarGridSpec(
        num_scalar_prefetch=6,
        grid=(n_rows // tm,),
        in_specs=[
            pl.BlockSpec((tm, dx), lambda i, be, bs, *_: (bs[i], 0)),
            pl.BlockSpec(memory_space=pl.ANY),
            pl.BlockSpec((None, n_exp, 1, f2), lambda i, *_: (layer, 0, 0, 0)),
            pl.BlockSpec(memory_space=pl.ANY),
            pl.BlockSpec((None, n_exp, 1, d), lambda i, *_: (layer, 0, 0, 0)),
        ],
        out_specs=pl.BlockSpec((tm, d // 2), lambda i, *_: (i, 0)),
        scratch_shapes=[pltpu.VMEM((2, d, f2), F32), pltpu.VMEM((2, f, d), F32),
                        pltpu.VMEM((d, f2), BF16), pltpu.VMEM((f, d), BF16),
                        pltpu.SemaphoreType.DMA((2,)), pltpu.SemaphoreType.DMA((2,))],
    )
    return pl.pallas_call(
        functools.partial(_experts_body, layer=layer),
        grid_spec=grid_spec,
        out_shape=jax.ShapeDtypeStruct((n_rows, d // 2), U32),
        compiler_params=_params("arbitrary"),
        name="experts",
    )(block_e, block_src, block_rows, block_first, block_slot, block_next, xs, w_gu,
      b_gu.reshape(depth, n_exp, 1, f2), w_dn, b_dn.reshape(depth, n_exp, 1, d))


def _combine_body(goff_ref, loff_ref, cp_ref, ltot_ref, ys_hbm, x_ref, lp_ref, mod_ref, lng_ref, lnb_ref, o_ref,
                  yl_ref, sem, *, n_exp, alpha, tiles_per_step):
    j = pl.program_id(0)
    tl = x_ref.shape[0] // tiles_per_step
    n_tiles = pl.num_programs(0) * tiles_per_step
    lr, hd = yl_ref.shape[1:]

    def fetch(tile):
        buf = yl_ref.at[tile % 2]

        def per_expert(e, carry):
            seg = tile * n_exp + e
            _segment_copy(ys_hbm, goff_ref[seg], buf, loff_ref[seg], cp_ref[seg], sem.at[tile % 2])
            return carry
        lax.fori_loop(0, n_exp, per_expert, 0)

        def zero_rows(r, carry):
            buf[pl.ds(pl.multiple_of(r * SUBLANES, SUBLANES), SUBLANES), :] = jnp.zeros((SUBLANES, hd), U32)
            return carry
        lax.fori_loop(ltot_ref[tile] // SUBLANES, lr // SUBLANES, zero_rows, 0)

    @pl.when(j == 0)
    def _():
        fetch(j)

    liota = lax.broadcasted_iota(I32, (tl, lr), 1)
    for t in range(tiles_per_step):
        tile = j * tiles_per_step + t
        tok = slice(t * tl, (t + 1) * tl)

        @pl.when(tile + 1 < n_tiles)
        def _():
            fetch(tile + 1)

        pick = _one_hot_hits(liota, lambda k: lp_ref[tok, k:k + 1]).astype(F32).astype(BF16)
        _wait_rows(ys_hbm, ltot_ref[tile], sem.at[tile % 2])
        y_lo, y_hi = _unpack_halves(yl_ref[tile % 2])
        y = jnp.concatenate([_dot(pick, y_lo), _dot(pick, y_hi)], axis=1)
        o_ref[tok, :] = _layer_norm(alpha * x_ref[tok, :] + mod_ref[5:6, :] * y, lng_ref[...], lnb_ref[...])


def _combine(ys, lp_t, goff, loff, cp, ltot, x1, mods_l, ln_g, ln_b, alpha, seq_len):
    t, d = x1.shape
    tl = min(MOE_TOKENS, t)
    n_exp = goff.shape[0] // (t // tl)
    lr = TOP_K * tl + n_exp * SUBLANES
    tps = _tiles_per_step(seq_len, tl)
    ts = tl * tps
    per_seq = seq_len // ts
    vec = pl.BlockSpec((1, d), lambda j, *_: (0, 0))
    grid_spec = pltpu.PrefetchScalarGridSpec(
        num_scalar_prefetch=4,
        grid=(t // ts,),
        in_specs=[
            pl.BlockSpec(memory_space=pl.ANY),
            pl.BlockSpec((ts, d), lambda j, *_: (j, 0)),
            pl.BlockSpec((ts, TOP_K), lambda j, *_: (j, 0)),
            pl.BlockSpec((None, 6, d), lambda j, *_: (j // per_seq, 0, 0)),
            vec, vec,
        ],
        out_specs=pl.BlockSpec((ts, d), lambda j, *_: (j, 0)),
        scratch_shapes=[pltpu.VMEM((2, lr, d // 2), U32), pltpu.SemaphoreType.DMA((2,))],
    )
    return pl.pallas_call(
        functools.partial(_combine_body, n_exp=n_exp, alpha=alpha, tiles_per_step=tps),
        grid_spec=grid_spec,
        out_shape=jax.ShapeDtypeStruct((t, d), F32),
        compiler_params=_params("arbitrary"),
        name="combine",
    )(goff, loff, cp, ltot, ys, x1, lp_t, mods_l, ln_g.reshape(1, d), ln_b.reshape(1, d))


def _moe_layer(x1, mods_l, w_r, b_r, w_gu, b_gu, w_dn, b_dn, layer, ln_g, ln_b, alpha, seq_len):
    t, d = x1.shape
    n_exp = w_r.shape[1]
    tm = MOE_TILE
    tl = min(MOE_TOKENS, t)
    n_tiles = t // tl
    idx, gates, rank, cnt = _router(x1, mods_l, seq_len, w_r, b_r)
    cp = (cnt[:, :, 0] + SUBLANES - 1) // SUBLANES * SUBLANES
    tot = jnp.sum(cp, axis=0)
    padded = (tot + tm - 1) // tm * tm
    pend = jnp.cumsum(padded)
    pstart = pend - padded
    goff = pstart[None, :] + jnp.cumsum(cp, axis=0) - cp
    loff = jnp.cumsum(cp, axis=1) - cp
    ltot = jnp.sum(cp, axis=1).astype(I32)
    is_e = idx[..., None] == jnp.arange(n_exp, dtype=I32)
    lp = jnp.sum(jnp.where(is_e, jnp.repeat(loff, tl, axis=0)[None], 0), axis=-1) + rank
    n_blocks = -(-(t * TOP_K + n_tiles * n_exp * (SUBLANES - 1)) // tm) + n_exp
    n_valid = pend[-1] // tm
    blk = jnp.arange(n_blocks, dtype=I32)
    block_src = jnp.minimum(blk, n_valid - 1)
    block_e = jnp.sum(pend[None, :] <= (block_src * tm)[:, None], axis=1).astype(I32)
    used = padded > 0
    experts = jnp.arange(n_exp, dtype=I32)
    next_used = jnp.flip(lax.cummin(jnp.flip(jnp.where(used, experts, n_exp))))
    next_used = jnp.concatenate([next_used[1:], jnp.full((1,), n_exp, I32)])
    next_used = jnp.where(next_used < n_exp, next_used, -1)
    slot_of = (jnp.cumsum(used.astype(I32)) - 1) % 2
    of_block = lambda table: jnp.sum(jnp.where(block_e[:, None] == experts[None, :], table[None, :], 0), axis=1)
    block_first = ((blk * tm == of_block(pstart)) & (blk < n_valid)).astype(I32)
    block_slot = of_block(slot_of).astype(I32)
    block_next = of_block(next_used).astype(I32)
    block_rows = jnp.where(blk < n_valid, jnp.clip(of_block(pstart + tot) - blk * tm, 0, tm), 0).astype(I32)
    n_rows = n_blocks * tm
    pad_start = jnp.concatenate([pstart + tot, pend[-1:]]).astype(I32)
    pad_cnt = jnp.concatenate([padded - tot, n_rows - pend[-1:]]).astype(I32)
    flat = lambda a: a.reshape(-1).astype(I32)
    seg = (flat(goff), flat(loff), flat(cp), ltot)
    xs = _dispatch(x1, mods_l, seq_len, lp, gates, *seg, pad_start, pad_cnt, n_rows)
    ys = _experts(xs, block_e, block_src.astype(I32), block_rows, block_first, block_slot, block_next,
                  w_gu, b_gu, w_dn, b_dn, layer)
    return _combine(ys, lp.T, *seg, x1, mods_l, ln_g, ln_b, alpha, seq_len)


def kernel(x, c, ada_w, ada_b, post_ln_g, post_ln_b, conv_w_pw1, conv_b_pw1, conv_w_dw, conv_b_dw, conv_ln_g, conv_ln_b, conv_w_pw2, conv_b_pw2, w_kv, attn_w_q, attn_lambda, attn_subln_g, attn_w_o, rel_bias_table, router_w, router_b, expert_w_gate_up, expert_b_gate_up, expert_w_down, expert_b_down):
    bsz, s, d = x.shape
    depth = ada_w.shape[0]
    n_a = depth // 2
    alpha = (2 * depth) ** 0.25
    mods = _ada(c, ada_w, ada_b).reshape(depth, bsz, 6, d)
    q = k = vt = None
    for l in range(depth):
        mods_l = mods[l]
        if l < n_a:
            x1 = _conv(x, mods_l, conv_w_pw1[l], conv_b_pw1[l], conv_w_dw[l], conv_b_dw[l], conv_ln_g[l],
                       conv_ln_b[l], conv_w_pw2[l], conv_b_pw2[l], post_ln_g[l, 0], post_ln_b[l, 0], alpha)
        else:
            j = l - n_a
            if j == 0:
                q, k, vt = _qkv(x, mods_l, w_kv, attn_w_q[j])
            else:
                q = _qkv(x, mods_l, w_kv, attn_w_q[j])[0]
            lambda_init = 0.8 - 0.6 * math.exp(-0.3 * l)
            ot = _attn(q, k, vt, attn_lambda[j], attn_subln_g[j], rel_bias_table, lambda_init)
            x1 = _attn_out(ot, x, mods_l, attn_w_o[j], post_ln_g[l, 0], post_ln_b[l, 0], alpha)
        x = _moe_layer(x1.reshape(bsz * s, d), mods_l, router_w[l], router_b[l],
                       expert_w_gate_up, expert_b_gate_up, expert_w_down, expert_b_down, l,
                       post_ln_g[l, 1], post_ln_b[l, 1], alpha, s).reshape(bsz, s, d)
    return x
```

```python
import functools
import math

import jax
import jax.numpy as jnp
from jax import lax
from jax.experimental import pallas as pl
from jax.experimental.pallas import tpu as pltpu

F32 = jnp.float32
BF16 = jnp.bfloat16
I32 = jnp.int32
U32 = jnp.uint32
HIGHEST = lax.Precision.HIGHEST

CHUNK = 64
CONV_WIDTH = 31
HEAD_DIM = 64
REL_BUCKETS = 32
REL_MAX_DIST = 128
TOP_K = 4
SWIGLU_LIMIT = 7.0
SWIGLU_ALPHA = 1.702
LN_EPS = 1e-5
MASK_VALUE = -1e30
LOG2_E = math.log2(math.e)

SUBLANES = 8
LANES = 128
VMEM_LIMIT_BYTES = 56 * 1024 * 1024

ADA_TN = 1024
SEQ_TILE = 512
PROJ_TILE = 1024
CONV_HALO = 32
CONV_ROWS = 256
CONV_COLS = 128
MOE_TOKENS = 256
ROUTER_TOKENS = 2048
MOE_TILES_PER_STEP = 2
MOE_TILE = 512
EXPERT_ROW_EIGHTHS = (8, 7, 6, 5, 4, 3, 2, 1)
Q_BLOCK = 128
GLU_LOOKAHEAD = 1
SCORE_LOOKAHEAD = 3
CAST_ROWS = 128

HALF_BITS = 16
HIGH_HALF_MASK = 0xFFFF0000


def _params(*sem):
    return pltpu.CompilerParams(dimension_semantics=sem, vmem_limit_bytes=VMEM_LIMIT_BYTES)


def _layer_norm(x, g, b):
    mu = jnp.mean(x, axis=-1, keepdims=True)
    xc = x - mu
    var = jnp.mean(xc * xc, axis=-1, keepdims=True)
    return xc * lax.rsqrt(var + LN_EPS) * g + b


def _dot(a, b):
    return jnp.dot(a, b, preferred_element_type=F32)


def _ada_body(c_ref, w_ref, b_ref, o_ref):
    c = c_ref[...]
    cond = c * jax.nn.sigmoid(c)
    o_ref[...] = jnp.dot(cond, w_ref[...], preferred_element_type=F32, precision=HIGHEST) + b_ref[...]


def _ada(c, ada_w, ada_b):
    depth, d, n = ada_w.shape
    bsz = c.shape[0]
    tn = min(ADA_TN, n)
    return pl.pallas_call(
        _ada_body,
        grid=(depth, n // tn),
        in_specs=[
            pl.BlockSpec((bsz, d), lambda l, j: (0, 0)),
            pl.BlockSpec((None, d, tn), lambda l, j: (l, 0, j)),
            pl.BlockSpec((None, 1, tn), lambda l, j: (l, 0, j)),
        ],
        out_specs=pl.BlockSpec((None, bsz, tn), lambda l, j: (l, 0, j)),
        out_shape=jax.ShapeDtypeStruct((depth, bsz, n), F32),
        compiler_params=_params("parallel", "parallel"),
        name="ada",
    )(c, ada_w, ada_b.reshape(depth, 1, n))


def _residual_epilogue(x, y, mod_ref, lng_ref, lnb_ref, alpha, x1_ref):
    x1_ref[...] = _layer_norm(alpha * x + mod_ref[2:3, :] * y, lng_ref[...], lnb_ref[...])


def _moe_input(x1_ref, mod_ref, rows=slice(None)):
    return x1_ref[rows, :] * (1.0 + mod_ref[4:5, :]) + mod_ref[3:4, :]


def _conv_body(x_ref, halo_ref, mod_ref, w1_ref, b1_ref, wdw_ref, bdw_ref, cg_ref, cb_ref, w2_ref, b2_ref,
               lng_ref, lnb_ref, x1_ref, win_ref, v_ref, *, alpha):
    ts, d = x_ref.shape
    i = pl.program_id(1)
    xw = jnp.concatenate([halo_ref[...], x_ref[...]], axis=0)
    h = (xw * (1.0 + mod_ref[1:2, :]) + mod_ref[0:1, :]).astype(BF16)
    rows = min(CONV_ROWS, ts)
    cols = min(CONV_COLS, d)

    def glu(c0):
        a = _dot(h, w1_ref[:, c0:c0 + cols]) + b1_ref[:, c0:c0 + cols]
        g = _dot(h, w1_ref[:, d + c0:d + c0 + cols]) + b1_ref[:, d + c0:d + c0 + cols]
        return a * jax.nn.sigmoid(g)

    off = CONV_HALO - (CONV_WIDTH - 1)
    n_shift = ts + CONV_HALO - SUBLANES
    in_seq = (lax.broadcasted_iota(I32, (ts + CONV_HALO, cols), 0) >= CONV_HALO) | (i > 0)
    starts = list(range(0, d, cols))
    ahead = [glu(c) for c in starts[:GLU_LOOKAHEAD]]
    for n, c0 in enumerate(starts):
        chan = slice(c0, c0 + cols)
        win_ref[0, :, chan] = jnp.where(in_seq, ahead.pop(0), 0.0)
        if n + GLU_LOOKAHEAD < len(starts):
            ahead.append(glu(starts[n + GLU_LOOKAHEAD]))
        for b in range(1, SUBLANES):
            win_ref[b, 0:n_shift, chan] = win_ref[0, b:b + n_shift, chan]
        for r0 in range(0, ts, rows):
            acc = jnp.zeros((rows, cols), F32)
            for j in range(CONV_WIDTH):
                a, b = divmod(off + j, SUBLANES)
                r = r0 + a * SUBLANES
                acc = acc + wdw_ref[j:j + 1, c0:c0 + cols] * win_ref[b, r:r + rows, c0:c0 + cols]
            v_ref[r0:r0 + rows, c0:c0 + cols] = acc
    v = _layer_norm(v_ref[...] + bdw_ref[...], cg_ref[...], cb_ref[...])
    v = (v * jax.nn.sigmoid(v)).astype(BF16)
    y = _dot(v, w2_ref[...]) + b2_ref[...]
    _residual_epilogue(x_ref[...], y, mod_ref, lng_ref, lnb_ref, alpha, x1_ref)


def _conv(x, mods_l, w_pw1, b_pw1, w_dw, b_dw, cln_g, cln_b, w_pw2, b_pw2, ln_g, ln_b, alpha):
    bsz, s, d = x.shape
    ts = min(SEQ_TILE, s)
    hb = ts // CONV_HALO
    row = lambda a: a.reshape(1, d)
    tile = pl.BlockSpec((None, ts, d), lambda b, i: (b, i, 0))
    vec = pl.BlockSpec((1, d), lambda b, i: (0, 0))
    return pl.pallas_call(
        functools.partial(_conv_body, alpha=alpha),
        grid=(bsz, s // ts),
        in_specs=[
            tile,
            pl.BlockSpec((None, CONV_HALO, d), lambda b, i: (b, jnp.maximum(i * hb - 1, 0), 0)),
            pl.BlockSpec((None, 6, d), lambda b, i: (b, 0, 0)),
            pl.BlockSpec((d, 2 * d), lambda b, i: (0, 0)),
            pl.BlockSpec((1, 2 * d), lambda b, i: (0, 0)),
            pl.BlockSpec((CONV_WIDTH, d), lambda b, i: (0, 0)),
            vec, vec, vec,
            pl.BlockSpec((d, d), lambda b, i: (0, 0)),
            vec, vec, vec,
        ],
        out_specs=tile,
        out_shape=jax.ShapeDtypeStruct((bsz, s, d), F32),
        scratch_shapes=[pltpu.VMEM((SUBLANES, ts + CONV_HALO, d), F32), pltpu.VMEM((ts, d), F32)],
        compiler_params=_params("parallel", "parallel"),
        name="conv",
    )(x, x, mods_l, w_pw1.astype(BF16), b_pw1.reshape(1, 2 * d), w_dw, row(b_dw), row(cln_g), row(cln_b),
      w_pw2.astype(BF16), row(b_pw2), row(ln_g), row(ln_b))


def _qkv_body(x_ref, mod_ref, wkv_ref, wq_ref, q_ref, k_ref, vt_ref):
    d = x_ref.shape[-1]
    x = x_ref[...]
    kv = _dot(x.astype(BF16), wkv_ref[...])
    k_ref[...] = kv[:, :d].astype(BF16)
    hd2 = 2 * HEAD_DIM
    for c0 in range(0, d, hd2):
        vt_ref[c0:c0 + hd2, :] = kv[:, d + c0:d + c0 + hd2].T.astype(BF16)
    h = (x * (1.0 + mod_ref[1:2, :]) + mod_ref[0:1, :]).astype(BF16)
    q_ref[...] = (_dot(h, wq_ref[...]) * (HEAD_DIM ** -0.5 * LOG2_E)).astype(BF16)


def _qkv(x, mods_l, w_kv, w_q):
    bsz, s, d = x.shape
    ts = min(PROJ_TILE, s)
    tile = pl.BlockSpec((None, ts, d), lambda b, i: (b, i, 0))
    return pl.pallas_call(
        _qkv_body,
        grid=(bsz, s // ts),
        in_specs=[
            tile,
            pl.BlockSpec((None, 6, d), lambda b, i: (b, 0, 0)),
            pl.BlockSpec((d, 2 * d), lambda b, i: (0, 0)),
            pl.BlockSpec((d, d), lambda b, i: (0, 0)),
        ],
        out_specs=[tile, tile, pl.BlockSpec((None, d, ts), lambda b, i: (b, 0, i))],
        out_shape=[jax.ShapeDtypeStruct((bsz, s, d), BF16)] * 2 + [jax.ShapeDtypeStruct((bsz, d, s), BF16)],
        compiler_params=_params("parallel", "parallel"),
        name="qkv",
    )(x, mods_l, w_kv.astype(BF16), w_q.astype(BF16))


def _t5_bucket(rel):
    nb = REL_BUCKETS // 2
    ret = jnp.where(rel > 0, nb, 0)
    n = jnp.abs(rel)
    max_exact = nb // 2
    large = max_exact + (jnp.log(jnp.maximum(n, 1).astype(F32) / max_exact)
                         / math.log(REL_MAX_DIST / max_exact) * (nb - max_exact)).astype(I32)
    large = jnp.minimum(large, nb - 1)
    return ret + jnp.where(n < max_exact, n, large)


def _saturation_distance():
    nb = REL_BUCKETS // 2
    max_exact = nb // 2
    n = max_exact
    while max_exact + math.log(n / max_exact) / math.log(REL_MAX_DIST / max_exact) * (nb - max_exact) < nb - 1 + 1e-3:
        n += 1
    return n


def _bucket_strip(s):
    r = jnp.arange(Q_BLOCK, dtype=I32)[:, None]
    kp = jnp.arange(s, dtype=I32)[None, :] - (s - Q_BLOCK)
    bucket = _t5_bucket(kp - r)
    visible = jnp.floor_divide(kp, CHUNK) <= (r // CHUNK)
    return jnp.where(visible, bucket, REL_BUCKETS)


def _attn_body(tab_ref, q_ref, k_ref, vt_ref, bkt_ref, lam_ref, sg_ref, ot_ref, bias_ref, *, lambda_init, n_heads):
    h = pl.program_id(0)
    b = pl.program_id(1)
    s = q_ref.shape[0]

    near = bias_ref.shape[0]

    @pl.when(b == 0)
    def _():
        bk = bkt_ref[...]
        acc = jnp.full(bk.shape, MASK_VALUE, F32)
        for r in range(REL_BUCKETS):
            acc = jnp.where(bk == r, tab_ref[r * n_heads + h] * LOG2_E, acc)
        bias_ref[:, 0:Q_BLOCK] = acc
        bias_ref[:, Q_BLOCK:] = acc

    far_bias = tab_ref[(REL_BUCKETS // 2 - 1) * n_heads + h] * LOG2_E

    lp = lam_ref[...]
    lam = (jnp.exp(jnp.sum(lp[0:1, :] * lp[1:2, :], axis=-1, keepdims=True))
           - jnp.exp(jnp.sum(lp[2:3, :] * lp[3:4, :], axis=-1, keepdims=True)) + lambda_init)
    lane = lax.broadcasted_iota(I32, (Q_BLOCK, 2 * HEAD_DIM), 1)
    nt = (((1,), (1,)), ((), ()))

    def scores(i):
        n_keys = (i + 1) * Q_BLOCK
        q = q_ref[i * Q_BLOCK:(i + 1) * Q_BLOCK, :]
        qq = jnp.concatenate([jnp.where(lane < HEAD_DIM, q, jnp.zeros_like(q)),
                              jnp.where(lane >= HEAD_DIM, q, jnp.zeros_like(q))], axis=0)
        return lax.dot_general(k_ref[0:n_keys, :], qq, nt, preferred_element_type=F32)

    def values(i, p, denom):
        pv = _dot(vt_ref[:, 0:(i + 1) * Q_BLOCK], p)
        o = pv[:, :Q_BLOCK] * (1.0 / denom[:, :Q_BLOCK]) - pv[:, Q_BLOCK:] * (lam / denom[:, Q_BLOCK:])
        o = o * lax.rsqrt(jnp.mean(o * o, axis=0, keepdims=True) + LN_EPS) * sg_ref[...]
        ot_ref[:, i * Q_BLOCK:(i + 1) * Q_BLOCK] = (o * (1.0 - lambda_init)).astype(BF16)

    n_blocks = s // Q_BLOCK
    ahead = [scores(i) for i in range(min(SCORE_LOOKAHEAD, n_blocks))]
    for i in range(n_blocks):
        raw = ahead.pop(0)
        if i + SCORE_LOOKAHEAD < n_blocks:
            ahead.append(scores(i + SCORE_LOOKAHEAD))
        n_keys = (i + 1) * Q_BLOCK
        n_near = min(near, n_keys)
        n_far = n_keys - n_near
        sc = raw[n_far:] + bias_ref[near - n_near:, :]
        top = jnp.max(sc, axis=0, keepdims=True)
        if n_far:
            top = jnp.maximum(top, jnp.max(raw[:n_far], axis=0, keepdims=True) + far_bias)
            p = jnp.concatenate([jnp.exp2(raw[:n_far] - (top - far_bias)), jnp.exp2(sc - top)], axis=0)
        else:
            p = jnp.exp2(sc - top)
        values(i, p.astype(BF16), jnp.sum(p, axis=0, keepdims=True))


def _attn(q, k, vt, lam_p, subln_g, rel_table, lambda_init):
    bsz, s, d = q.shape
    hd2 = 2 * HEAD_DIM
    n_heads = d // hd2
    assert REL_MAX_DIST >= CHUNK and _saturation_distance() <= REL_MAX_DIST
    near = min(Q_BLOCK + REL_MAX_DIST, s)
    head = pl.BlockSpec((None, s, hd2), lambda h, b, tab: (b, 0, h))
    head_t = pl.BlockSpec((None, hd2, s), lambda h, b, tab: (b, h, 0))
    grid_spec = pltpu.PrefetchScalarGridSpec(
        num_scalar_prefetch=1,
        grid=(n_heads, bsz),
        in_specs=[
            head, head, head_t,
            pl.BlockSpec((near, Q_BLOCK), lambda h, b, tab: (0, 0)),
            pl.BlockSpec((4, HEAD_DIM), lambda h, b, tab: (0, 0)),
            pl.BlockSpec((hd2, 1), lambda h, b, tab: (0, 0)),
        ],
        out_specs=head_t,
        scratch_shapes=[pltpu.VMEM((near, 2 * Q_BLOCK), F32)],
    )
    return pl.pallas_call(
        functools.partial(_attn_body, lambda_init=lambda_init, n_heads=n_heads),
        grid_spec=grid_spec,
        out_shape=jax.ShapeDtypeStruct((bsz, d, s), BF16),
        compiler_params=_params("arbitrary", "arbitrary"),
        name="attn",
    )(rel_table.reshape(-1), q, k, vt, _bucket_strip(s).T[s - near:], lam_p, subln_g.reshape(hd2, 1))


def _attn_out_body(ot_ref, x_ref, mod_ref, wo_ref, lng_ref, lnb_ref, x1_ref, *, alpha):
    y = lax.dot_general(ot_ref[...], wo_ref[...], (((0,), (0,)), ((), ())), preferred_element_type=F32)
    _residual_epilogue(x_ref[...], y, mod_ref, lng_ref, lnb_ref, alpha, x1_ref)


def _attn_out(o, x, mods_l, w_o, ln_g, ln_b, alpha):
    bsz, s, d = x.shape
    ts = min(PROJ_TILE, s)
    tile = pl.BlockSpec((None, ts, d), lambda b, i: (b, i, 0))
    vec = pl.BlockSpec((1, d), lambda b, i: (0, 0))
    return pl.pallas_call(
        functools.partial(_attn_out_body, alpha=alpha),
        grid=(bsz, s // ts),
        in_specs=[pl.BlockSpec((None, d, ts), lambda b, i: (b, 0, i)), tile,
                  pl.BlockSpec((None, 6, d), lambda b, i: (b, 0, 0)),
                  pl.BlockSpec((d, d), lambda b, i: (0, 0)), vec, vec],
        out_specs=tile,
        out_shape=jax.ShapeDtypeStruct((bsz, s, d), F32),
        compiler_params=_params("parallel", "parallel"),
        name="attn_out",
    )(o, x, mods_l, w_o.astype(BF16), ln_g.reshape(1, d), ln_b.reshape(1, d))


def _router_body(x1_ref, mod_ref, whi_ref, wlo_ref, b_ref, tri_ref, idx_ref, gate_ref, rank_ref, cnt_ref):
    h = _moe_input(x1_ref, mod_ref)
    h_hi = h.astype(BF16)
    h_lo = (h - h_hi.astype(F32)).astype(BF16)
    nt = (((1,), (1,)), ((), ()))
    logits = (lax.dot_general(whi_ref[...], h_hi, nt, preferred_element_type=F32)
              + (lax.dot_general(whi_ref[...], h_lo, nt, preferred_element_type=F32)
                 + lax.dot_general(wlo_ref[...], h_hi, nt, preferred_element_type=F32))) + b_ref[...]
    n_exp, tr = logits.shape
    eio = lax.broadcasted_iota(I32, (n_exp, tr), 0)
    work = logits
    vals, idxs = [], []
    for _ in range(TOP_K):
        m = jnp.max(work, axis=0, keepdims=True)
        am = jnp.min(jnp.where(work == m, eio, n_exp), axis=0, keepdims=True)
        vals.append(m)
        idxs.append(am)
        work = jnp.where(eio == am, -jnp.inf, work)
    ex = [jnp.exp(v - vals[0]) for v in vals]
    den = ex[0] + ex[1] + ex[2] + ex[3]
    onehot = jnp.zeros((n_exp, tr), F32)
    for k in range(TOP_K):
        onehot = onehot + (eio == idxs[k]).astype(F32)
    tl = tri_ref.shape[0]
    tiles = [onehot[:, t0:t0 + tl] for t0 in range(0, tr, tl)]
    before = jnp.concatenate([_dot(oh.astype(BF16), tri_ref[...]) for oh in tiles], axis=1)
    for k in range(TOP_K):
        idx_ref[k:k + 1, :] = idxs[k]
        gate_ref[k:k + 1, :] = ex[k] / den
        rank_ref[k:k + 1, :] = jnp.sum(jnp.where(eio == idxs[k], before, 0.0), axis=0, keepdims=True).astype(I32)
    for j, oh in enumerate(tiles):
        cnt_ref[j] = jnp.sum(oh, axis=1, keepdims=True).astype(I32)


def _router(x1, mods_l, seq_len, w_r, b_r):
    t, d = x1.shape
    n_exp = w_r.shape[1]
    tl = min(MOE_TOKENS, t)
    tr = min(ROUTER_TOKENS, seq_len)
    per_seq = seq_len // tr
    wt = w_r.T
    wt_hi = wt.astype(BF16)
    wt_lo = (wt - wt_hi.astype(F32)).astype(BF16)
    pos = jnp.arange(tl, dtype=I32)
    tri = (pos[:, None] < pos[None, :]).astype(BF16)
    tok = pl.BlockSpec((TOP_K, tr), lambda i: (0, i))
    return pl.pallas_call(
        _router_body,
        grid=(t // tr,),
        in_specs=[
            pl.BlockSpec((tr, d), lambda i: (i, 0)),
            pl.BlockSpec((None, 6, d), lambda i: (i // per_seq, 0, 0)),
            pl.BlockSpec((n_exp, d), lambda i: (0, 0)),
            pl.BlockSpec((n_exp, d), lambda i: (0, 0)),
            pl.BlockSpec((n_exp, 1), lambda i: (0, 0)),
            pl.BlockSpec((tl, tl), lambda i: (0, 0)),
        ],
        out_specs=[tok, tok, tok, pl.BlockSpec((tr // tl, n_exp, 1), lambda i: (i, 0, 0))],
        out_shape=[jax.ShapeDtypeStruct((TOP_K, t), I32), jax.ShapeDtypeStruct((TOP_K, t), F32),
                   jax.ShapeDtypeStruct((TOP_K, t), I32), jax.ShapeDtypeStruct((t // tl, n_exp, 1), I32)],
        compiler_params=_params("parallel"),
        name="router",
    )(x1, mods_l, wt_hi, wt_lo, b_r.reshape(n_exp, 1), tri)


def _rows(ref, start, cnt):
    aligned = lambda v: v if isinstance(v, int) else pl.multiple_of(v, SUBLANES)
    return ref.at[pl.ds(aligned(start), aligned(cnt))]


def _segment_copy(src, src_row, dst, dst_row, cnt, sem):
    @pl.when(cnt > 0)
    def _():
        pltpu.make_async_copy(_rows(src, src_row, cnt), _rows(dst, dst_row, cnt), sem).start()


def _wait_rows(ref, cnt, sem):
    @pl.when(cnt > 0)
    def _():
        pltpu.make_async_copy(_rows(ref, 0, cnt), _rows(ref, 0, cnt), sem).wait()


def _zero_fill(zero_ref, xs_hbm, start, cnt, sem):
    zr = zero_ref.shape[0]
    n_full = cnt // zr

    def full(r, c):
        _segment_copy(zero_ref, 0, xs_hbm, start + r * zr, zr, sem)
        return c

    lax.fori_loop(0, n_full, full, 0)
    _segment_copy(zero_ref, 0, xs_hbm, start + n_full * zr, cnt - n_full * zr, sem)


def _pack_halves(x):
    hd = x.shape[1] // 2
    lo = lax.bitcast_convert_type(x[:, :hd], U32)
    hi = lax.bitcast_convert_type(x[:, hd:], U32)
    return (lo >> HALF_BITS) | (hi & jnp.uint32(HIGH_HALF_MASK))


def _unpack_halves(words):
    lo = lax.bitcast_convert_type(words << HALF_BITS, F32).astype(BF16)
    hi = lax.bitcast_convert_type(words & jnp.uint32(HIGH_HALF_MASK), F32).astype(BF16)
    return lo, hi


def _tiles_per_step(seq_len, tl):
    return MOE_TILES_PER_STEP if seq_len % (tl * MOE_TILES_PER_STEP) == 0 else 1


def _one_hot_hits(iota, pos_of):
    hit = iota == pos_of(0)
    for k in range(1, TOP_K):
        hit = hit | (iota == pos_of(k))
    return hit


def _dispatch_body(goff_ref, loff_ref, cp_ref, ltot_ref, pad_start_ref, pad_cnt_ref, x1_ref, mod_ref, lp_ref,
                   gate_ref, xs_hbm, xl_ref, zero_ref, sem, zsem, *, n_exp, tiles_per_step):
    j = pl.program_id(0)
    d = x1_ref.shape[1]
    tl = x1_ref.shape[0] // tiles_per_step
    lr = xl_ref.shape[1]

    @pl.when(j == 0)
    def _():
        zero_ref[...] = jnp.zeros_like(zero_ref)

        def per_range(e, carry):
            _zero_fill(zero_ref, xs_hbm, pad_start_ref[e], pad_cnt_ref[e], zsem)
            return carry
        lax.fori_loop(0, pad_start_ref.shape[0], per_range, 0)

    def start_segments(tile):
        def per_expert(e, carry):
            seg = tile * n_exp + e
            _segment_copy(xl_ref.at[tile % 2], loff_ref[seg], xs_hbm, goff_ref[seg], cp_ref[seg], sem.at[tile % 2])
            return carry
        lax.fori_loop(0, n_exp, per_expert, 0)

    def wait_segments(tile):
        _wait_rows(xs_hbm, ltot_ref[tile], sem.at[tile % 2])

    riota = lax.broadcasted_iota(I32, (lr, tl), 0)
    for t in range(tiles_per_step):
        tile = j * tiles_per_step + t
        tok = slice(t * tl, (t + 1) * tl)

        @pl.when(tile >= 2)
        def _():
            wait_segments(tile - 2)

        buf = xl_ref.at[tile % 2]
        perm = _one_hot_hits(riota, lambda k: lp_ref[k:k + 1, tok]).astype(F32).astype(BF16)
        rows = _dot(perm, _moe_input(x1_ref, mod_ref, tok).astype(BF16))
        gsel = jnp.zeros((lr, tl), F32)
        for k in range(TOP_K):
            gsel = gsel + jnp.where(riota == lp_ref[k:k + 1, tok], gate_ref[k:k + 1, tok], 0.0)
        rowg = jnp.broadcast_to(jnp.sum(gsel, axis=1, keepdims=True), (lr, LANES))
        buf[:, d // 2:] = lax.bitcast_convert_type(rowg, U32)
        buf[:, 0:d // 2] = _pack_halves(rows)
        start_segments(tile)

    @pl.when(j == pl.num_programs(0) - 1)
    def _():
        last = pl.num_programs(0) * tiles_per_step - 1

        @pl.when(last >= 1)
        def _():
            wait_segments(last - 1)
        wait_segments(last)
        n_zero = lax.fori_loop(0, pad_cnt_ref.shape[0], lambda e, acc: acc + pad_cnt_ref[e], jnp.int32(0))
        _wait_rows(xs_hbm, n_zero, zsem)


def _dispatch(x1, mods_l, seq_len, lp, gates, goff, loff, cp, ltot, pad_start, pad_cnt, n_rows):
    t, d = x1.shape
    tl = min(MOE_TOKENS, t)
    n_exp = goff.shape[0] // (t // tl)
    lr = TOP_K * tl + n_exp * SUBLANES
    tps = _tiles_per_step(seq_len, tl)
    ts = tl * tps
    per_seq = seq_len // ts
    tok = pl.BlockSpec((TOP_K, ts), lambda j, *_: (0, j))
    grid_spec = pltpu.PrefetchScalarGridSpec(
        num_scalar_prefetch=6,
        grid=(t // ts,),
        in_specs=[pl.BlockSpec((ts, d), lambda j, *_: (j, 0)),
                  pl.BlockSpec((None, 6, d), lambda j, *_: (j // per_seq, 0, 0)), tok, tok],
        out_specs=pl.BlockSpec(memory_space=pl.ANY),
        scratch_shapes=[pltpu.VMEM((2, lr, d // 2 + LANES), U32), pltpu.VMEM((MOE_TILE, d // 2 + LANES), U32),
                        pltpu.SemaphoreType.DMA((2,)), pltpu.SemaphoreType.DMA],
    )
    return pl.pallas_call(
        functools.partial(_dispatch_body, n_exp=n_exp, tiles_per_step=tps),
        grid_spec=grid_spec,
        out_shape=jax.ShapeDtypeStruct((n_rows, d // 2 + LANES), U32),
        compiler_params=_params("arbitrary"),
        name="dispatch",
    )(goff, loff, cp, ltot, pad_start, pad_cnt, x1, mods_l, lp, gates)


def _experts_body(be_ref, bsrc_ref, rows_ref, first_ref, slot_ref, next_ref, xs_ref, wgu_hbm, bgu_ref, wdn_hbm,
                  bdn_ref, ys_ref, wgu_f32, wdn_f32, wgu_bf, wdn_bf, sem_gu, sem_dn, *, layer):
    i = pl.program_id(0)
    e = be_ref[i]
    d, f2 = wgu_bf.shape
    f = f2 // 2

    def fetch(expert, slot, start):
        for hbm, buf, sem in ((wgu_hbm, wgu_f32, sem_gu), (wdn_hbm, wdn_f32, sem_dn)):
            dma = pltpu.make_async_copy(hbm.at[layer, expert], buf.at[slot], sem.at[slot])
            dma.start() if start else dma.wait()

    @pl.when(first_ref[i] == 1)
    def _():
        slot = slot_ref[i]

        @pl.when(i == 0)
        def _():
            fetch(e, slot, True)

        fetch(e, slot, False)

        @pl.when(next_ref[i] >= 0)
        def _():
            fetch(next_ref[i], 1 - slot, True)

        def cast(ref_in, ref_out):
            def step(r, c):
                rows = pl.ds(pl.multiple_of(r * CAST_ROWS, CAST_ROWS), CAST_ROWS)
                ref_out[rows, :] = ref_in[rows, :].astype(BF16)
                return c
            lax.fori_loop(0, ref_in.shape[0] // CAST_ROWS, step, 0)
        cast(wgu_f32.at[slot], wgu_bf)
        cast(wdn_f32.at[slot], wdn_bf)

    hd = d // 2
    tm = xs_ref.shape[0]

    def mlp(n):
        x_lo, x_hi = _unpack_halves(xs_ref[0:n, 0:hd])
        proj = lambda cols: _dot(x_lo, wgu_bf[0:hd, cols]) + _dot(x_hi, wgu_bf[hd:, cols]) + bgu_ref[e, :, cols]
        gate = jnp.minimum(proj(slice(0, f)), SWIGLU_LIMIT)
        lin = jnp.clip(proj(slice(f, f2)), -SWIGLU_LIMIT, SWIGLU_LIMIT)
        act = (gate * jax.nn.sigmoid(SWIGLU_ALPHA * gate) * (lin + 1.0)).astype(BF16)
        row_gate = lax.bitcast_convert_type(xs_ref[0:n, hd:hd + 1], F32)
        y = (_dot(act, wdn_bf[...]) + bdn_ref[e]) * row_gate
        ys_ref[0:n, :] = _pack_halves(y.astype(BF16).astype(F32))
        if n < tm:
            ys_ref[n:, :] = jnp.zeros((tm - n, hd), U32)

    rows = rows_ref[i]
    sizes = [tm * e // 8 for e in EXPERT_ROW_EIGHTHS]
    for n, smaller in zip(sizes, sizes[1:] + [0]):
        @pl.when((rows > smaller) & (rows <= n))
        def _(n=n):
            mlp(n)

    @pl.when(rows == 0)
    def _():
        ys_ref[...] = jnp.zeros_like(ys_ref)


def _experts(xs, block_e, block_src, block_rows, block_first, block_slot, block_next, w_gu, b_gu, w_dn, b_dn, layer):
    n_rows, dx = xs.shape
    depth, n_exp, d, f2 = w_gu.shape
    f = f2 // 2
    tm = MOE_TILE
    grid_spec = pltpu.PrefetchScalarGridSpec(
        num_scalar_prefetch=6,
        grid=(n_rows // tm,),
        in_specs=[
            pl.BlockSpec((tm, dx), lambda i, be, bs, *_: (bs[i], 0)),
            pl.BlockSpec(memory_space=pl.ANY),
            pl.BlockSpec((None, n_exp, 1, f2), lambda i, *_: (layer, 0, 0, 0)),
            pl.BlockSpec(memory_space=pl.ANY),
            pl.BlockSpec((None, n_exp, 1, d), lambda i, *_: (layer, 0, 0, 0)),
        ],
        out_specs=pl.BlockSpec((tm, d // 2), lambda i, *_: (i, 0)),
        scratch_shapes=[pltpu.VMEM((2, d, f2), F32), pltpu.VMEM((2, f, d), F32),
                        pltpu.VMEM((d, f2), BF16), pltpu.VMEM((f, d), BF16),
                        pltpu.SemaphoreType.DMA((2,)), pltpu.SemaphoreType.DMA((2,))],
    )
    return pl.pallas_call(
        functools.partial(_experts_body, layer=layer),
        grid_spec=grid_spec,
        out_shape=jax.ShapeDtypeStruct((n_rows, d // 2), U32),
        compiler_params=_params("arbitrary"),
        name="experts",
    )(block_e, block_src, block_rows, block_first, block_slot, block_next, xs, w_gu,
      b_gu.reshape(depth, n_exp, 1, f2), w_dn, b_dn.reshape(depth, n_exp, 1, d))


def _combine_body(goff_ref, loff_ref, cp_ref, ltot_ref, ys_hbm, x_ref, lp_ref, mod_ref, lng_ref, lnb_ref, o_ref,
                  yl_ref, sem, *, n_exp, alpha, tiles_per_step):
    j = pl.program_id(0)
    tl = x_ref.shape[0] // tiles_per_step
    n_tiles = pl.num_programs(0) * tiles_per_step
    lr, hd = yl_ref.shape[1:]

    def fetch(tile):
        buf = yl_ref.at[tile % 2]

        def per_expert(e, carry):
            seg = tile * n_exp + e
            _segment_copy(ys_hbm, goff_ref[seg], buf, loff_ref[seg], cp_ref[seg], sem.at[tile % 2])
            return carry
        lax.fori_loop(0, n_exp, per_expert, 0)

        def zero_rows(r, carry):
            buf[pl.ds(pl.multiple_of(r * SUBLANES, SUBLANES), SUBLANES), :] = jnp.zeros((SUBLANES, hd), U32)
            return carry
        lax.fori_loop(ltot_ref[tile] // SUBLANES, lr // SUBLANES, zero_rows, 0)

    @pl.when(j == 0)
    def _():
        fetch(j)

    liota = lax.broadcasted_iota(I32, (tl, lr), 1)
    for t in range(tiles_per_step):
        tile = j * tiles_per_step + t
        tok = slice(t * tl, (t + 1) * tl)

        @pl.when(tile + 1 < n_tiles)
        def _():
            fetch(tile + 1)

        pick = _one_hot_hits(liota, lambda k: lp_ref[tok, k:k + 1]).astype(F32).astype(BF16)
        _wait_rows(ys_hbm, ltot_ref[tile], sem.at[tile % 2])
        y_lo, y_hi = _unpack_halves(yl_ref[tile % 2])
        y = jnp.concatenate([_dot(pick, y_lo), _dot(pick, y_hi)], axis=1)
        o_ref[tok, :] = _layer_norm(alpha * x_ref[tok, :] + mod_ref[5:6, :] * y, lng_ref[...], lnb_ref[...])


def _combine(ys, lp_t, goff, loff, cp, ltot, x1, mods_l, ln_g, ln_b, alpha, seq_len):
    t, d = x1.shape
    tl = min(MOE_TOKENS, t)
    n_exp = goff.shape[0] // (t // tl)
    lr = TOP_K * tl + n_exp * SUBLANES
    tps = _tiles_per_step(seq_len, tl)
    ts = tl * tps
    per_seq = seq_len // ts
    vec = pl.BlockSpec((1, d), lambda j, *_: (0, 0))
    grid_spec = pltpu.PrefetchScalarGridSpec(
        num_scalar_prefetch=4,
        grid=(t // ts,),
        in_specs=[
            pl.BlockSpec(memory_space=pl.ANY),
            pl.BlockSpec((ts, d), lambda j, *_: (j, 0)),
            pl.BlockSpec((ts, TOP_K), lambda j, *_: (j, 0)),
            pl.BlockSpec((None, 6, d), lambda j, *_: (j // per_seq, 0, 0)),
            vec, vec,
        ],
        out_specs=pl.BlockSpec((ts, d), lambda j, *_: (j, 0)),
        scratch_shapes=[pltpu.VMEM((2, lr, d // 2), U32), pltpu.SemaphoreType.DMA((2,))],
    )
    return pl.pallas_call(
        functools.partial(_combine_body, n_exp=n_exp, alpha=alpha, tiles_per_step=tps),
        grid_spec=grid_spec,
        out_shape=jax.ShapeDtypeStruct((t, d), F32),
        compiler_params=_params("arbitrary"),
        name="combine",
    )(goff, loff, cp, ltot, ys, x1, lp_t, mods_l, ln_g.reshape(1, d), ln_b.reshape(1, d))


def _moe_layer(x1, mods_l, w_r, b_r, w_gu, b_gu, w_dn, b_dn, layer, ln_g, ln_b, alpha, seq_len):
    t, d = x1.shape
    n_exp = w_r.shape[1]
    tm = MOE_TILE
    tl = min(MOE_TOKENS, t)
    n_tiles = t // tl
    idx, gates, rank, cnt = _router(x1, mods_l, seq_len, w_r, b_r)
    cp = (cnt[:, :, 0] + SUBLANES - 1) // SUBLANES * SUBLANES
    tot = jnp.sum(cp, axis=0)
    padded = (tot + tm - 1) // tm * tm
    pend = jnp.cumsum(padded)
    pstart = pend - padded
    goff = pstart[None, :] + jnp.cumsum(cp, axis=0) - cp
    loff = jnp.cumsum(cp, axis=1) - cp
    ltot = jnp.sum(cp, axis=1).astype(I32)
    is_e = idx[..., None] == jnp.arange(n_exp, dtype=I32)
    lp = jnp.sum(jnp.where(is_e, jnp.repeat(loff, tl, axis=0)[None], 0), axis=-1) + rank
    n_blocks = -(-(t * TOP_K + n_tiles * n_exp * (SUBLANES - 1)) // tm) + n_exp
    n_valid = pend[-1] // tm
    blk = jnp.arange(n_blocks, dtype=I32)
    block_src = jnp.minimum(blk, n_valid - 1)
    block_e = jnp.sum(pend[None, :] <= (block_src * tm)[:, None], axis=1).astype(I32)
    used = padded > 0
    experts = jnp.arange(n_exp, dtype=I32)
    next_used = jnp.flip(lax.cummin(jnp.flip(jnp.where(used, experts, n_exp))))
    next_used = jnp.concatenate([next_used[1:], jnp.full((1,), n_exp, I32)])
    next_used = jnp.where(next_used < n_exp, next_used, -1)
    slot_of = (jnp.cumsum(used.astype(I32)) - 1) % 2
    of_block = lambda table: jnp.sum(jnp.where(block_e[:, None] == experts[None, :], table[None, :], 0), axis=1)
    block_first = ((blk * tm == of_block(pstart)) & (blk < n_valid)).astype(I32)
    block_slot = of_block(slot_of).astype(I32)
    block_next = of_block(next_used).astype(I32)
    block_rows = jnp.where(blk < n_valid, jnp.clip(of_block(pstart + tot) - blk * tm, 0, tm), 0).astype(I32)
    n_rows = n_blocks * tm
    pad_start = jnp.concatenate([pstart + tot, pend[-1:]]).astype(I32)
    pad_cnt = jnp.concatenate([padded - tot, n_rows - pend[-1:]]).astype(I32)
    flat = lambda a: a.reshape(-1).astype(I32)
    seg = (flat(goff), flat(loff), flat(cp), ltot)
    xs = _dispatch(x1, mods_l, seq_len, lp, gates, *seg, pad_start, pad_cnt, n_rows)
    ys = _experts(xs, block_e, block_src.astype(I32), block_rows, block_first, block_slot, block_next,
                  w_gu, b_gu, w_dn, b_dn, layer)
    return _combine(ys, lp.T, *seg, x1, mods_l, ln_g, ln_b, alpha, seq_len)


def kernel(x, c, ada_w, ada_b, post_ln_g, post_ln_b, conv_w_pw1, conv_b_pw1, conv_w_dw, conv_b_dw, conv_ln_g, conv_ln_b, conv_w_pw2, conv_b_pw2, w_kv, attn_w_q, attn_lambda, attn_subln_g, attn_w_o, rel_bias_table, router_w, router_b, expert_w_gate_up, expert_b_gate_up, expert_w_down, expert_b_down):
    bsz, s, d = x.shape
    depth = ada_w.shape[0]
    n_a = depth // 2
    alpha = (2 * depth) ** 0.25
    mods = _ada(c, ada_w, ada_b).reshape(depth, bsz, 6, d)
    q = k = vt = None
    for l in range(depth):
        mods_l = mods[l]
        if l < n_a:
            x1 = _conv(x, mods_l, conv_w_pw1[l], conv_b_pw1[l], conv_w_dw[l], conv_b_dw[l], conv_ln_g[l],
                       conv_ln_b[l], conv_w_pw2[l], conv_b_pw2[l], post_ln_g[l, 0], post_ln_b[l, 0], alpha)
        else:
            j = l - n_a
            if j == 0:
                q, k, vt = _qkv(x, mods_l, w_kv, attn_w_q[j])
            else:
                q = _qkv(x, mods_l, w_kv, attn_w_q[j])[0]
            lambda_init = 0.8 - 0.6 * math.exp(-0.3 * l)
            ot = _attn(q, k, vt, attn_lambda[j], attn_subln_g[j], rel_bias_table, lambda_init)
            x1 = _attn_out(ot, x, mods_l, attn_w_o[j], post_ln_g[l, 0], post_ln_b[l, 0], alpha)
        x = _moe_layer(x1.reshape(bsz * s, d), mods_l, router_w[l], router_b[l],
                       expert_w_gate_up, expert_b_gate_up, expert_w_down, expert_b_down, l,
                       post_ln_g[l, 1], post_ln_b[l, 1], alpha, s).reshape(bsz, s, d)
    return x
```

```python
import functools
import math

import jax
import jax.numpy as jnp
from jax import lax
from jax.experimental import pallas as pl
from jax.experimental.pallas import tpu as pltpu

F32 = jnp.float32
BF16 = jnp.bfloat16
I32 = jnp.int32
U32 = jnp.uint32
HIGHEST = lax.Precision.HIGHEST

CHUNK = 64
CONV_WIDTH = 31
HEAD_DIM = 64
REL_BUCKETS = 32
REL_MAX_DIST = 128
TOP_K = 4
SWIGLU_LIMIT = 7.0
SWIGLU_ALPHA = 1.702
LN_EPS = 1e-5
MASK_VALUE = -1e30
LOG2_E = math.log2(math.e)

SUBLANES = 8
LANES = 128
VMEM_LIMIT_BYTES = 56 * 1024 * 1024

ADA_TN = 1024
SEQ_TILE = 512
PROJ_TILE = 1024
CONV_HALO = 32
CONV_ROWS = 256
CONV_COLS = 128
MOE_TOKENS = 256
ROUTER_TOKENS = 2048
MOE_TILES_PER_STEP = 2
MOE_TILE = 512
EXPERT_ROW_EIGHTHS = (8, 6, 4, 2)
Q_BLOCK = 128
GLU_LOOKAHEAD = 1
SCORE_LOOKAHEAD = 2
CAST_ROWS = 128

HALF_BITS = 16
HIGH_HALF_MASK = 0xFFFF0000


def _params(*sem):
    return pltpu.CompilerParams(dimension_semantics=sem, vmem_limit_bytes=VMEM_LIMIT_BYTES)


def _layer_norm(x, g, b):
    mu = jnp.mean(x, axis=-1, keepdims=True)
    xc = x - mu
    var = jnp.mean(xc * xc, axis=-1, keepdims=True)
    return xc * lax.rsqrt(var + LN_EPS) * g + b


def _dot(a, b):
    return jnp.dot(a, b, preferred_element_type=F32)


def _ada_body(c_ref, w_ref, b_ref, o_ref):
    c = c_ref[...]
    cond = c * jax.nn.sigmoid(c)
    o_ref[...] = jnp.dot(cond, w_ref[...], preferred_element_type=F32, precision=HIGHEST) + b_ref[...]


def _ada(c, ada_w, ada_b):
    depth, d, n = ada_w.shape
    bsz = c.shape[0]
    tn = min(ADA_TN, n)
    return pl.pallas_call(
        _ada_body,
        grid=(depth, n // tn),
        in_specs=[
            pl.BlockSpec((bsz, d), lambda l, j: (0, 0)),
            pl.BlockSpec((None, d, tn), lambda l, j: (l, 0, j)),
            pl.BlockSpec((None, 1, tn), lambda l, j: (l, 0, j)),
        ],
        out_specs=pl.BlockSpec((None, bsz, tn), lambda l, j: (l, 0, j)),
        out_shape=jax.ShapeDtypeStruct((depth, bsz, n), F32),
        compiler_params=_params("parallel", "parallel"),
        name="ada",
    )(c, ada_w, ada_b.reshape(depth, 1, n))


def _residual_epilogue(x, y, mod_ref, lng_ref, lnb_ref, alpha, x1_ref):
    x1_ref[...] = _layer_norm(alpha * x + mod_ref[2:3, :] * y, lng_ref[...], lnb_ref[...])


def _moe_input(x1_ref, mod_ref, rows=slice(None)):
    return x1_ref[rows, :] * (1.0 + mod_ref[4:5, :]) + mod_ref[3:4, :]


def _conv_body(x_ref, halo_ref, mod_ref, w1_ref, b1_ref, wdw_ref, bdw_ref, cg_ref, cb_ref, w2_ref, b2_ref,
               lng_ref, lnb_ref, x1_ref, win_ref, v_ref, *, alpha):
    ts, d = x_ref.shape
    i = pl.program_id(1)
    xw = jnp.concatenate([halo_ref[...], x_ref[...]], axis=0)
    h = (xw * (1.0 + mod_ref[1:2, :]) + mod_ref[0:1, :]).astype(BF16)
    rows = min(CONV_ROWS, ts)
    cols = min(CONV_COLS, d)

    def glu(c0):
        a = _dot(h, w1_ref[:, c0:c0 + cols]) + b1_ref[:, c0:c0 + cols]
        g = _dot(h, w1_ref[:, d + c0:d + c0 + cols]) + b1_ref[:, d + c0:d + c0 + cols]
        return a * jax.nn.sigmoid(g)

    off = CONV_HALO - (CONV_WIDTH - 1)
    n_shift = ts + CONV_HALO - SUBLANES
    in_seq = (lax.broadcasted_iota(I32, (ts + CONV_HALO, cols), 0) >= CONV_HALO) | (i > 0)
    starts = list(range(0, d, cols))
    ahead = [glu(c) for c in starts[:GLU_LOOKAHEAD]]
    for n, c0 in enumerate(starts):
        chan = slice(c0, c0 + cols)
        win_ref[0, :, chan] = jnp.where(in_seq, ahead.pop(0), 0.0)
        if n + GLU_LOOKAHEAD < len(starts):
            ahead.append(glu(starts[n + GLU_LOOKAHEAD]))
        for b in range(1, SUBLANES):
            win_ref[b, 0:n_shift, chan] = win_ref[0, b:b + n_shift, chan]
        for r0 in range(0, ts, rows):
            acc = jnp.zeros((rows, cols), F32)
            for j in range(CONV_WIDTH):
                a, b = divmod(off + j, SUBLANES)
                r = r0 + a * SUBLANES
                acc = acc + wdw_ref[j:j + 1, c0:c0 + cols] * win_ref[b, r:r + rows, c0:c0 + cols]
            v_ref[r0:r0 + rows, c0:c0 + cols] = acc
    v = _layer_norm(v_ref[...] + bdw_ref[...], cg_ref[...], cb_ref[...])
    v = (v * jax.nn.sigmoid(v)).astype(BF16)
    y = _dot(v, w2_ref[...]) + b2_ref[...]
    _residual_epilogue(x_ref[...], y, mod_ref, lng_ref, lnb_ref, alpha, x1_ref)


def _conv(x, mods_l, w_pw1, b_pw1, w_dw, b_dw, cln_g, cln_b, w_pw2, b_pw2, ln_g, ln_b, alpha):
    bsz, s, d = x.shape
    ts = min(SEQ_TILE, s)
    hb = ts // CONV_HALO
    row = lambda a: a.reshape(1, d)
    tile = pl.BlockSpec((None, ts, d), lambda b, i: (b, i, 0))
    vec = pl.BlockSpec((1, d), lambda b, i: (0, 0))
    return pl.pallas_call(
        functools.partial(_conv_body, alpha=alpha),
        grid=(bsz, s // ts),
        in_specs=[
            tile,
            pl.BlockSpec((None, CONV_HALO, d), lambda b, i: (b, jnp.maximum(i * hb - 1, 0), 0)),
            pl.BlockSpec((None, 6, d), lambda b, i: (b, 0, 0)),
            pl.BlockSpec((d, 2 * d), lambda b, i: (0, 0)),
            pl.BlockSpec((1, 2 * d), lambda b, i: (0, 0)),
            pl.BlockSpec((CONV_WIDTH, d), lambda b, i: (0, 0)),
            vec, vec, vec,
            pl.BlockSpec((d, d), lambda b, i: (0, 0)),
            vec, vec, vec,
        ],
        out_specs=tile,
        out_shape=jax.ShapeDtypeStruct((bsz, s, d), F32),
        scratch_shapes=[pltpu.VMEM((SUBLANES, ts + CONV_HALO, d), F32), pltpu.VMEM((ts, d), F32)],
        compiler_params=_params("parallel", "parallel"),
        name="conv",
    )(x, x, mods_l, w_pw1.astype(BF16), b_pw1.reshape(1, 2 * d), w_dw, row(b_dw), row(cln_g), row(cln_b),
      w_pw2.astype(BF16), row(b_pw2), row(ln_g), row(ln_b))


def _qkv_body(x_ref, mod_ref, wkv_ref, wq_ref, q_ref, k_ref, vt_ref):
    d = x_ref.shape[-1]
    x = x_ref[...]
    kv = _dot(x.astype(BF16), wkv_ref[...])
    k_ref[...] = kv[:, :d].astype(BF16)
    hd2 = 2 * HEAD_DIM
    for c0 in range(0, d, hd2):
        vt_ref[c0:c0 + hd2, :] = kv[:, d + c0:d + c0 + hd2].T.astype(BF16)
    h = (x * (1.0 + mod_ref[1:2, :]) + mod_ref[0:1, :]).astype(BF16)
    q_ref[...] = (_dot(h, wq_ref[...]) * (HEAD_DIM ** -0.5 * LOG2_E)).astype(BF16)


def _qkv(x, mods_l, w_kv, w_q):
    bsz, s, d = x.shape
    ts = min(PROJ_TILE, s)
    tile = pl.BlockSpec((None, ts, d), lambda b, i: (b, i, 0))
    return pl.pallas_call(
        _qkv_body,
        grid=(bsz, s // ts),
        in_specs=[
            tile,
            pl.BlockSpec((None, 6, d), lambda b, i: (b, 0, 0)),
            pl.BlockSpec((d, 2 * d), lambda b, i: (0, 0)),
            pl.BlockSpec((d, d), lambda b, i: (0, 0)),
        ],
        out_specs=[tile, tile, pl.BlockSpec((None, d, ts), lambda b, i: (b, 0, i))],
        out_shape=[jax.ShapeDtypeStruct((bsz, s, d), BF16)] * 2 + [jax.ShapeDtypeStruct((bsz, d, s), BF16)],
        compiler_params=_params("parallel", "parallel"),
        name="qkv",
    )(x, mods_l, w_kv.astype(BF16), w_q.astype(BF16))


def _t5_bucket(rel):
    nb = REL_BUCKETS // 2
    ret = jnp.where(rel > 0, nb, 0)
    n = jnp.abs(rel)
    max_exact = nb // 2
    large = max_exact + (jnp.log(jnp.maximum(n, 1).astype(F32) / max_exact)
                         / math.log(REL_MAX_DIST / max_exact) * (nb - max_exact)).astype(I32)
    large = jnp.minimum(large, nb - 1)
    return ret + jnp.where(n < max_exact, n, large)


def _saturation_distance():
    nb = REL_BUCKETS // 2
    max_exact = nb // 2
    n = max_exact
    while max_exact + math.log(n / max_exact) / math.log(REL_MAX_DIST / max_exact) * (nb - max_exact) < nb - 1 + 1e-3:
        n += 1
    return n


def _bucket_strip(s):
    r = jnp.arange(Q_BLOCK, dtype=I32)[:, None]
    kp = jnp.arange(s, dtype=I32)[None, :] - (s - Q_BLOCK)
    bucket = _t5_bucket(kp - r)
    visible = jnp.floor_divide(kp, CHUNK) <= (r // CHUNK)
    return jnp.where(visible, bucket, REL_BUCKETS)


def _attn_body(tab_ref, q_ref, k_ref, vt_ref, bkt_ref, lam_ref, sg_ref, ot_ref, bias_ref, *, lambda_init, n_heads):
    h = pl.program_id(0)
    b = pl.program_id(1)
    s = q_ref.shape[0]

    near = bias_ref.shape[0]

    @pl.when(b == 0)
    def _():
        bk = bkt_ref[...]
        acc = jnp.full(bk.shape, MASK_VALUE, F32)
        for r in range(REL_BUCKETS):
            acc = jnp.where(bk == r, tab_ref[r * n_heads + h] * LOG2_E, acc)
        bias_ref[:, 0:Q_BLOCK] = acc
        bias_ref[:, Q_BLOCK:] = acc

    far_bias = tab_ref[(REL_BUCKETS // 2 - 1) * n_heads + h] * LOG2_E

    lp = lam_ref[...]
    lam = (jnp.exp(jnp.sum(lp[0:1, :] * lp[1:2, :], axis=-1, keepdims=True))
           - jnp.exp(jnp.sum(lp[2:3, :] * lp[3:4, :], axis=-1, keepdims=True)) + lambda_init)
    lane = lax.broadcasted_iota(I32, (Q_BLOCK, 2 * HEAD_DIM), 1)
    nt = (((1,), (1,)), ((), ()))

    def scores(i):
        n_keys = (i + 1) * Q_BLOCK
        q = q_ref[i * Q_BLOCK:(i + 1) * Q_BLOCK, :]
        qq = jnp.concatenate([jnp.where(lane < HEAD_DIM, q, jnp.zeros_like(q)),
                              jnp.where(lane >= HEAD_DIM, q, jnp.zeros_like(q))], axis=0)
        return lax.dot_general(k_ref[0:n_keys, :], qq, nt, preferred_element_type=F32)

    def values(i, p, denom):
        pv = _dot(vt_ref[:, 0:(i + 1) * Q_BLOCK], p)
        o = pv[:, :Q_BLOCK] * (1.0 / denom[:, :Q_BLOCK]) - pv[:, Q_BLOCK:] * (lam / denom[:, Q_BLOCK:])
        o = o * lax.rsqrt(jnp.mean(o * o, axis=0, keepdims=True) + LN_EPS) * sg_ref[...]
        ot_ref[:, i * Q_BLOCK:(i + 1) * Q_BLOCK] = (o * (1.0 - lambda_init)).astype(BF16)

    n_blocks = s // Q_BLOCK
    ahead = [scores(i) for i in range(min(SCORE_LOOKAHEAD, n_blocks))]
    for i in range(n_blocks):
        raw = ahead.pop(0)
        if i + SCORE_LOOKAHEAD < n_blocks:
            ahead.append(scores(i + SCORE_LOOKAHEAD))
        n_keys = (i + 1) * Q_BLOCK
        n_near = min(near, n_keys)
        n_far = n_keys - n_near
        sc = raw[n_far:] + bias_ref[near - n_near:, :]
        top = jnp.max(sc, axis=0, keepdims=True)
        if n_far:
            top = jnp.maximum(top, jnp.max(raw[:n_far], axis=0, keepdims=True) + far_bias)
            p = jnp.concatenate([jnp.exp2(raw[:n_far] - (top - far_bias)), jnp.exp2(sc - top)], axis=0)
        else:
            p = jnp.exp2(sc - top)
        values(i, p.astype(BF16), jnp.sum(p, axis=0, keepdims=True))


def _attn(q, k, vt, lam_p, subln_g, rel_table, lambda_init):
    bsz, s, d = q.shape
    hd2 = 2 * HEAD_DIM
    n_heads = d // hd2
    assert REL_MAX_DIST >= CHUNK and _saturation_distance() <= REL_MAX_DIST
    near = min(Q_BLOCK + REL_MAX_DIST, s)
    head = pl.BlockSpec((None, s, hd2), lambda h, b, tab: (b, 0, h))
    head_t = pl.BlockSpec((None, hd2, s), lambda h, b, tab: (b, h, 0))
    grid_spec = pltpu.PrefetchScalarGridSpec(
        num_scalar_prefetch=1,
        grid=(n_heads, bsz),
        in_specs=[
            head, head, head_t,
            pl.BlockSpec((near, Q_BLOCK), lambda h, b, tab: (0, 0)),
            pl.BlockSpec((4, HEAD_DIM), lambda h, b, tab: (0, 0)),
            pl.BlockSpec((hd2, 1), lambda h, b, tab: (0, 0)),
        ],
        out_specs=head_t,
        scratch_shapes=[pltpu.VMEM((near, 2 * Q_BLOCK), F32)],
    )
    return pl.pallas_call(
        functools.partial(_attn_body, lambda_init=lambda_init, n_heads=n_heads),
        grid_spec=grid_spec,
        out_shape=jax.ShapeDtypeStruct((bsz, d, s), BF16),
        compiler_params=_params("arbitrary", "arbitrary"),
        name="attn",
    )(rel_table.reshape(-1), q, k, vt, _bucket_strip(s).T[s - near:], lam_p, subln_g.reshape(hd2, 1))


def _attn_out_body(ot_ref, x_ref, mod_ref, wo_ref, lng_ref, lnb_ref, x1_ref, *, alpha):
    y = lax.dot_general(ot_ref[...], wo_ref[...], (((0,), (0,)), ((), ())), preferred_element_type=F32)
    _residual_epilogue(x_ref[...], y, mod_ref, lng_ref, lnb_ref, alpha, x1_ref)


def _attn_out(o, x, mods_l, w_o, ln_g, ln_b, alpha):
    bsz, s, d = x.shape
    ts = min(PROJ_TILE, s)
    tile = pl.BlockSpec((None, ts, d), lambda b, i: (b, i, 0))
    vec = pl.BlockSpec((1, d), lambda b, i: (0, 0))
    return pl.pallas_call(
        functools.partial(_attn_out_body, alpha=alpha),
        grid=(bsz, s // ts),
        in_specs=[pl.BlockSpec((None, d, ts), lambda b, i: (b, 0, i)), tile,
                  pl.BlockSpec((None, 6, d), lambda b, i: (b, 0, 0)),
                  pl.BlockSpec((d, d), lambda b, i: (0, 0)), vec, vec],
        out_specs=tile,
        out_shape=jax.ShapeDtypeStruct((bsz, s, d), F32),
        compiler_params=_params("parallel", "parallel"),
        name="attn_out",
    )(o, x, mods_l, w_o.astype(BF16), ln_g.reshape(1, d), ln_b.reshape(1, d))


def _router_body(x1_ref, mod_ref, whi_ref, wlo_ref, b_ref, tri_ref, idx_ref, gate_ref, rank_ref, cnt_ref):
    h = _moe_input(x1_ref, mod_ref)
    h_hi = h.astype(BF16)
    h_lo = (h - h_hi.astype(F32)).astype(BF16)
    nt = (((1,), (1,)), ((), ()))
    logits = (lax.dot_general(whi_ref[...], h_hi, nt, preferred_element_type=F32)
              + (lax.dot_general(whi_ref[...], h_lo, nt, preferred_element_type=F32)
                 + lax.dot_general(wlo_ref[...], h_hi, nt, preferred_element_type=F32))) + b_ref[...]
    n_exp, tr = logits.shape
    eio = lax.broadcasted_iota(I32, (n_exp, tr), 0)
    work = logits
    vals, idxs = [], []
    for _ in range(TOP_K):
        m = jnp.max(work, axis=0, keepdims=True)
        am = jnp.min(jnp.where(work == m, eio, n_exp), axis=0, keepdims=True)
        vals.append(m)
        idxs.append(am)
        work = jnp.where(eio == am, -jnp.inf, work)
    ex = [jnp.exp(v - vals[0]) for v in vals]
    den = ex[0] + ex[1] + ex[2] + ex[3]
    onehot = jnp.zeros((n_exp, tr), F32)
    for k in range(TOP_K):
        onehot = onehot + (eio == idxs[k]).astype(F32)
    tl = tri_ref.shape[0]
    tiles = [onehot[:, t0:t0 + tl] for t0 in range(0, tr, tl)]
    before = jnp.concatenate([_dot(oh.astype(BF16), tri_ref[...]) for oh in tiles], axis=1)
    for k in range(TOP_K):
        idx_ref[k:k + 1, :] = idxs[k]
        gate_ref[k:k + 1, :] = ex[k] / den
        rank_ref[k:k + 1, :] = jnp.sum(jnp.where(eio == idxs[k], before, 0.0), axis=0, keepdims=True).astype(I32)
    for j, oh in enumerate(tiles):
        cnt_ref[j] = jnp.sum(oh, axis=1, keepdims=True).astype(I32)


def _router(x1, mods_l, seq_len, w_r, b_r):
    t, d = x1.shape
    n_exp = w_r.shape[1]
    tl = min(MOE_TOKENS, t)
    tr = min(ROUTER_TOKENS, seq_len)
    per_seq = seq_len // tr
    wt = w_r.T
    wt_hi = wt.astype(BF16)
    wt_lo = (wt - wt_hi.astype(F32)).astype(BF16)
    pos = jnp.arange(tl, dtype=I32)
    tri = (pos[:, None] < pos[None, :]).astype(BF16)
    tok = pl.BlockSpec((TOP_K, tr), lambda i: (0, i))
    return pl.pallas_call(
        _router_body,
        grid=(t // tr,),
        in_specs=[
            pl.BlockSpec((tr, d), lambda i: (i, 0)),
            pl.BlockSpec((None, 6, d), lambda i: (i // per_seq, 0, 0)),
            pl.BlockSpec((n_exp, d), lambda i: (0, 0)),
            pl.BlockSpec((n_exp, d), lambda i: (0, 0)),
            pl.BlockSpec((n_exp, 1), lambda i: (0, 0)),
            pl.BlockSpec((tl, tl), lambda i: (0, 0)),
        ],
        out_specs=[tok, tok, tok, pl.BlockSpec((tr // tl, n_exp, 1), lambda i: (i, 0, 0))],
        out_shape=[jax.ShapeDtypeStruct((TOP_K, t), I32), jax.ShapeDtypeStruct((TOP_K, t), F32),
                   jax.ShapeDtypeStruct((TOP_K, t), I32), jax.ShapeDtypeStruct((t // tl, n_exp, 1), I32)],
        compiler_params=_params("parallel"),
        name="router",
    )(x1, mods_l, wt_hi, wt_lo, b_r.reshape(n_exp, 1), tri)


def _rows(ref, start, cnt):
    aligned = lambda v: v if isinstance(v, int) else pl.multiple_of(v, SUBLANES)
    return ref.at[pl.ds(aligned(start), aligned(cnt))]


def _segment_copy(src, src_row, dst, dst_row, cnt, sem):
    @pl.when(cnt > 0)
    def _():
        pltpu.make_async_copy(_rows(src, src_row, cnt), _rows(dst, dst_row, cnt), sem).start()


def _wait_rows(ref, cnt, sem):
    @pl.when(cnt > 0)
    def _():
        pltpu.make_async_copy(_rows(ref, 0, cnt), _rows(ref, 0, cnt), sem).wait()


def _zero_fill(zero_ref, xs_hbm, start, cnt, sem):
    zr = zero_ref.shape[0]
    n_full = cnt // zr

    def full(r, c):
        _segment_copy(zero_ref, 0, xs_hbm, start + r * zr, zr, sem)
        return c

    lax.fori_loop(0, n_full, full, 0)
    _segment_copy(zero_ref, 0, xs_hbm, start + n_full * zr, cnt - n_full * zr, sem)


def _pack_halves(x):
    hd = x.shape[1] // 2
    lo = lax.bitcast_convert_type(x[:, :hd], U32)
    hi = lax.bitcast_convert_type(x[:, hd:], U32)
    return (lo >> HALF_BITS) | (hi & jnp.uint32(HIGH_HALF_MASK))


def _unpack_halves(words):
    lo = lax.bitcast_convert_type(words << HALF_BITS, F32).astype(BF16)
    hi = lax.bitcast_convert_type(words & jnp.uint32(HIGH_HALF_MASK), F32).astype(BF16)
    return lo, hi


def _tiles_per_step(seq_len, tl):
    return MOE_TILES_PER_STEP if seq_len % (tl * MOE_TILES_PER_STEP) == 0 else 1


def _one_hot_hits(iota, pos_of):
    hit = iota == pos_of(0)
    for k in range(1, TOP_K):
        hit = hit | (iota == pos_of(k))
    return hit


def _dispatch_body(goff_ref, loff_ref, cp_ref, ltot_ref, pad_start_ref, pad_cnt_ref, x1_ref, mod_ref, lp_ref,
                   gate_ref, xs_hbm, xl_ref, zero_ref, sem, zsem, *, n_exp, tiles_per_step):
    j = pl.program_id(0)
    d = x1_ref.shape[1]
    tl = x1_ref.shape[0] // tiles_per_step
    lr = xl_ref.shape[1]

    @pl.when(j == 0)
    def _():
        zero_ref[...] = jnp.zeros_like(zero_ref)

        def per_range(e, carry):
            _zero_fill(zero_ref, xs_hbm, pad_start_ref[e], pad_cnt_ref[e], zsem)
            return carry
        lax.fori_loop(0, pad_start_ref.shape[0], per_range, 0)

    def start_segments(tile):
        def per_expert(e, carry):
            seg = tile * n_exp + e
            _segment_copy(xl_ref.at[tile % 2], loff_ref[seg], xs_hbm, goff_ref[seg], cp_ref[seg], sem.at[tile % 2])
            return carry
        lax.fori_loop(0, n_exp, per_expert, 0)

    def wait_segments(tile):
        _wait_rows(xs_hbm, ltot_ref[tile], sem.at[tile % 2])

    riota = lax.broadcasted_iota(I32, (lr, tl), 0)
    for t in range(tiles_per_step):
        tile = j * tiles_per_step + t
        tok = slice(t * tl, (t + 1) * tl)

        @pl.when(tile >= 2)
        def _():
            wait_segments(tile - 2)

        buf = xl_ref.at[tile % 2]
        perm = _one_hot_hits(riota, lambda k: lp_ref[k:k + 1, tok]).astype(F32).astype(BF16)
        rows = _dot(perm, _moe_input(x1_ref, mod_ref, tok).astype(BF16))
        gsel = jnp.zeros((lr, tl), F32)
        for k in range(TOP_K):
            gsel = gsel + jnp.where(riota == lp_ref[k:k + 1, tok], gate_ref[k:k + 1, tok], 0.0)
        rowg = jnp.broadcast_to(jnp.sum(gsel, axis=1, keepdims=True), (lr, LANES))
        buf[:, d // 2:] = lax.bitcast_convert_type(rowg, U32)
        buf[:, 0:d // 2] = _pack_halves(rows)
        start_segments(tile)

    @pl.when(j == pl.num_programs(0) - 1)
    def _():
        last = pl.num_programs(0) * tiles_per_step - 1

        @pl.when(last >= 1)
        def _():
            wait_segments(last - 1)
        wait_segments(last)
        n_zero = lax.fori_loop(0, pad_cnt_ref.shape[0], lambda e, acc: acc + pad_cnt_ref[e], jnp.int32(0))
        _wait_rows(xs_hbm, n_zero, zsem)


def _dispatch(x1, mods_l, seq_len, lp, gates, goff, loff, cp, ltot, pad_start, pad_cnt, n_rows):
    t, d = x1.shape
    tl = min(MOE_TOKENS, t)
    n_exp = goff.shape[0] // (t // tl)
    lr = TOP_K * tl + n_exp * SUBLANES
    tps = _tiles_per_step(seq_len, tl)
    ts = tl * tps
    per_seq = seq_len // ts
    tok = pl.BlockSpec((TOP_K, ts), lambda j, *_: (0, j))
    grid_spec = pltpu.PrefetchScalarGridSpec(
        num_scalar_prefetch=6,
        grid=(t // ts,),
        in_specs=[pl.BlockSpec((ts, d), lambda j, *_: (j, 0)),
                  pl.BlockSpec((None, 6, d), lambda j, *_: (j // per_seq, 0, 0)), tok, tok],
        out_specs=pl.BlockSpec(memory_space=pl.ANY),
        scratch_shapes=[pltpu.VMEM((2, lr, d // 2 + LANES), U32), pltpu.VMEM((MOE_TILE, d // 2 + LANES), U32),
                        pltpu.SemaphoreType.DMA((2,)), pltpu.SemaphoreType.DMA],
    )
    return pl.pallas_call(
        functools.partial(_dispatch_body, n_exp=n_exp, tiles_per_step=tps),
        grid_spec=grid_spec,
        out_shape=jax.ShapeDtypeStruct((n_rows, d // 2 + LANES), U32),
        compiler_params=_params("arbitrary"),
        name="dispatch",
    )(goff, loff, cp, ltot, pad_start, pad_cnt, x1, mods_l, lp, gates)


def _experts_body(be_ref, bsrc_ref, rows_ref, first_ref, slot_ref, next_ref, xs_ref, wgu_hbm, bgu_ref, wdn_hbm,
                  bdn_ref, ys_ref, wgu_f32, wdn_f32, wgu_bf, wdn_bf, sem_gu, sem_dn, *, layer):
    i = pl.program_id(0)
    e = be_ref[i]
    d, f2 = wgu_bf.shape
    f = f2 // 2

    def fetch(expert, slot, start):
        for hbm, buf, sem in ((wgu_hbm, wgu_f32, sem_gu), (wdn_hbm, wdn_f32, sem_dn)):
            dma = pltpu.make_async_copy(hbm.at[layer, expert], buf.at[slot], sem.at[slot])
            dma.start() if start else dma.wait()

    @pl.when(first_ref[i] == 1)
    def _():
        slot = slot_ref[i]

        @pl.when(i == 0)
        def _():
            fetch(e, slot, True)

        fetch(e, slot, False)

        @pl.when(next_ref[i] >= 0)
        def _():
            fetch(next_ref[i], 1 - slot, True)

        def cast(ref_in, ref_out):
            def step(r, c):
                rows = pl.ds(pl.multiple_of(r * CAST_ROWS, CAST_ROWS), CAST_ROWS)
                ref_out[rows, :] = ref_in[rows, :].astype(BF16)
                return c
            lax.fori_loop(0, ref_in.shape[0] // CAST_ROWS, step, 0)
        cast(wgu_f32.at[slot], wgu_bf)
        cast(wdn_f32.at[slot], wdn_bf)

    hd = d // 2
    tm = xs_ref.shape[0]

    def mlp(n):
        x_lo, x_hi = _unpack_halves(xs_ref[0:n, 0:hd])
        proj = lambda cols: _dot(x_lo, wgu_bf[0:hd, cols]) + _dot(x_hi, wgu_bf[hd:, cols]) + bgu_ref[e, :, cols]
        gate = jnp.minimum(proj(slice(0, f)), SWIGLU_LIMIT)
        lin = jnp.clip(proj(slice(f, f2)), -SWIGLU_LIMIT, SWIGLU_LIMIT)
        act = (gate * jax.nn.sigmoid(SWIGLU_ALPHA * gate) * (lin + 1.0)).astype(BF16)
        row_gate = lax.bitcast_convert_type(xs_ref[0:n, hd:hd + 1], F32)
        y = (_dot(act, wdn_bf[...]) + bdn_ref[e]) * row_gate
        ys_ref[0:n, :] = _pack_halves(y.astype(BF16).astype(F32))
        if n < tm:
            ys_ref[n:, :] = jnp.zeros((tm - n, hd), U32)

    rows = rows_ref[i]
    sizes = [tm * e // 8 for e in EXPERT_ROW_EIGHTHS]
    for n, smaller in zip(sizes, sizes[1:] + [0]):
        @pl.when((rows > smaller) & (rows <= n))
        def _(n=n):
            mlp(n)

    @pl.when(rows == 0)
    def _():
        ys_ref[...] = jnp.zeros_like(ys_ref)


def _experts(xs, block_e, block_src, block_rows, block_first, block_slot, block_next, w_gu, b_gu, w_dn, b_dn, layer):
    n_rows, dx = xs.shape
    depth, n_exp, d, f2 = w_gu.shape
    f = f2 // 2
    tm = MOE_TILE
    grid_spec = pltpu.PrefetchScalarGridSpec(
        num_scalar_prefetch=6,
        grid=(n_rows // tm,),
        in_specs=[
            pl.BlockSpec((tm, dx), lambda i, be, bs, *_: (bs[i], 0)),
            pl.BlockSpec(memory_space=pl.ANY),
            pl.BlockSpec((None, n_exp, 1, f2), lambda i, *_: (layer, 0, 0, 0)),
            pl.BlockSpec(memory_space=pl.ANY),
            pl.BlockSpec((None, n_exp, 1, d), lambda i, *_: (layer, 0, 0, 0)),
        ],
        out_specs=pl.BlockSpec((tm, d // 2), lambda i, *_: (i, 0)),
        scratch_shapes=[pltpu.VMEM((2, d, f2), F32), pltpu.VMEM((2, f, d), F32),
                        pltpu.VMEM((d, f2), BF16), pltpu.VMEM((f, d), BF16),
                        pltpu.SemaphoreType.DMA((2,)), pltpu.SemaphoreType.DMA((2,))],
    )
    return pl.pallas_call(
        functools.partial(_experts_body, layer=layer),
        grid_spec=grid_spec,
        out_shape=jax.ShapeDtypeStruct((n_rows, d // 2), U32),
        compiler_params=_params("arbitrary"),
        name="experts",
    )(block_e, block_src, block_rows, block_first, block_slot, block_next, xs, w_gu,
      b_gu.reshape(depth, n_exp, 1, f2), w_dn, b_dn.reshape(depth, n_exp, 1, d))


def _combine_body(goff_ref, loff_ref, cp_ref, ltot_ref, ys_hbm, x_ref, lp_ref, mod_ref, lng_ref, lnb_ref, o_ref,
                  yl_ref, sem, *, n_exp, alpha, tiles_per_step):
    j = pl.program_id(0)
    tl = x_ref.shape[0] // tiles_per_step
    n_tiles = pl.num_programs(0) * tiles_per_step
    lr, hd = yl_ref.shape[1:]

    def fetch(tile):
        buf = yl_ref.at[tile % 2]

        def per_expert(e, carry):
            seg = tile * n_exp + e
            _segment_copy(ys_hbm, goff_ref[seg], buf, loff_ref[seg], cp_ref[seg], sem.at[tile % 2])
            return carry
        lax.fori_loop(0, n_exp, per_expert, 0)

        def zero_rows(r, carry):
            buf[pl.ds(pl.multiple_of(r * SUBLANES, SUBLANES), SUBLANES), :] = jnp.zeros((SUBLANES, hd), U32)
            return carry
        lax.fori_loop(ltot_ref[tile] // SUBLANES, lr // SUBLANES, zero_rows, 0)

    @pl.when(j == 0)
    def _():
        fetch(j)

    liota = lax.broadcasted_iota(I32, (tl, lr), 1)
    for t in range(tiles_per_step):
        tile = j * tiles_per_step + t
        tok = slice(t * tl, (t + 1) * tl)

        @pl.when(tile + 1 < n_tiles)
        def _():
            fetch(tile + 1)

        pick = _one_hot_hits(liota, lambda k: lp_ref[tok, k:k + 1]).astype(F32).astype(BF16)
        _wait_rows(ys_hbm, ltot_ref[tile], sem.at[tile % 2])
        y_lo, y_hi = _unpack_halves(yl_ref[tile % 2])
        y = jnp.concatenate([_dot(pick, y_lo), _dot(pick, y_hi)], axis=1)
        o_ref[tok, :] = _layer_norm(alpha * x_ref[tok, :] + mod_ref[5:6, :] * y, lng_ref[...], lnb_ref[...])


def _combine(ys, lp_t, goff, loff, cp, ltot, x1, mods_l, ln_g, ln_b, alpha, seq_len):
    t, d = x1.shape
    tl = min(MOE_TOKENS, t)
    n_exp = goff.shape[0] // (t // tl)
    lr = TOP_K * tl + n_exp * SUBLANES
    tps = _tiles_per_step(seq_len, tl)
    ts = tl * tps
    per_seq = seq_len // ts
    vec = pl.BlockSpec((1, d), lambda j, *_: (0, 0))
    grid_spec = pltpu.PrefetchScalarGridSpec(
        num_scalar_prefetch=4,
        grid=(t // ts,),
        in_specs=[
            pl.BlockSpec(memory_space=pl.ANY),
            pl.BlockSpec((ts, d), lambda j, *_: (j, 0)),
            pl.BlockSpec((ts, TOP_K), lambda j, *_: (j, 0)),
            pl.BlockSpec((None, 6, d), lambda j, *_: (j // per_seq, 0, 0)),
            vec, vec,
        ],
        out_specs=pl.BlockSpec((ts, d), lambda j, *_: (j, 0)),
        scratch_shapes=[pltpu.VMEM((2, lr, d // 2), U32), pltpu.SemaphoreType.DMA((2,))],
    )
    return pl.pallas_call(
        functools.partial(_combine_body, n_exp=n_exp, alpha=alpha, tiles_per_step=tps),
        grid_spec=grid_spec,
        out_shape=jax.ShapeDtypeStruct((t, d), F32),
        compiler_params=_params("arbitrary"),
        name="combine",
    )(goff, loff, cp, ltot, ys, x1, lp_t, mods_l, ln_g.reshape(1, d), ln_b.reshape(1, d))


def _moe_layer(x1, mods_l, w_r, b_r, w_gu, b_gu, w_dn, b_dn, layer, ln_g, ln_b, alpha, seq_len):
    t, d = x1.shape
    n_exp = w_r.shape[1]
    tm = MOE_TILE
    tl = min(MOE_TOKENS, t)
    n_tiles = t // tl
    idx, gates, rank, cnt = _router(x1, mods_l, seq_len, w_r, b_r)
    cp = (cnt[:, :, 0] + SUBLANES - 1) // SUBLANES * SUBLANES
    tot = jnp.sum(cp, axis=0)
    padded = (tot + tm - 1) // tm * tm
    pend = jnp.cumsum(padded)
    pstart = pend - padded
    goff = pstart[None, :] + jnp.cumsum(cp, axis=0) - cp
    loff = jnp.cumsum(cp, axis=1) - cp
    ltot = jnp.sum(cp, axis=1).astype(I32)
    is_e = idx[..., None] == jnp.arange(n_exp, dtype=I32)
    lp = jnp.sum(jnp.where(is_e, jnp.repeat(loff, tl, axis=0)[None], 0), axis=-1) + rank
    n_blocks = -(-(t * TOP_K + n_tiles * n_exp * (SUBLANES - 1)) // tm) + n_exp
    n_valid = pend[-1] // tm
    blk = jnp.arange(n_blocks, dtype=I32)
    block_src = jnp.minimum(blk, n_valid - 1)
    block_e = jnp.sum(pend[None, :] <= (block_src * tm)[:, None], axis=1).astype(I32)
    used = padded > 0
    experts = jnp.arange(n_exp, dtype=I32)
    next_used = jnp.flip(lax.cummin(jnp.flip(jnp.where(used, experts, n_exp))))
    next_used = jnp.concatenate([next_used[1:], jnp.full((1,), n_exp, I32)])
    next_used = jnp.where(next_used < n_exp, next_used, -1)
    slot_of = (jnp.cumsum(used.astype(I32)) - 1) % 2
    of_block = lambda table: jnp.sum(jnp.where(block_e[:, None] == experts[None, :], table[None, :], 0), axis=1)
    block_first = ((blk * tm == of_block(pstart)) & (blk < n_valid)).astype(I32)
    block_slot = of_block(slot_of).astype(I32)
    block_next = of_block(next_used).astype(I32)
    block_rows = jnp.where(blk < n_valid, jnp.clip(of_block(pstart + tot) - blk * tm, 0, tm), 0).astype(I32)
    n_rows = n_blocks * tm
    pad_start = jnp.concatenate([pstart + tot, pend[-1:]]).astype(I32)
    pad_cnt = jnp.concatenate([padded - tot, n_rows - pend[-1:]]).astype(I32)
    flat = lambda a: a.reshape(-1).astype(I32)
    seg = (flat(goff), flat(loff), flat(cp), ltot)
    xs = _dispatch(x1, mods_l, seq_len, lp, gates, *seg, pad_start, pad_cnt, n_rows)
    ys = _experts(xs, block_e, block_src.astype(I32), block_rows, block_first, block_slot, block_next,
                  w_gu, b_gu, w_dn, b_dn, layer)
    return _combine(ys, lp.T, *seg, x1, mods_l, ln_g, ln_b, alpha, seq_len)


def kernel(x, c, ada_w, ada_b, post_ln_g, post_ln_b, conv_w_pw1, conv_b_pw1, conv_w_dw, conv_b_dw, conv_ln_g, conv_ln_b, conv_w_pw2, conv_b_pw2, w_kv, attn_w_q, attn_lambda, attn_subln_g, attn_w_o, rel_bias_table, router_w, router_b, expert_w_gate_up, expert_b_gate_up, expert_w_down, expert_b_down):
    bsz, s, d = x.shape
    depth = ada_w.shape[0]
    n_a = depth // 2
    alpha = (2 * depth) ** 0.25
    mods = _ada(c, ada_w, ada_b).reshape(depth, bsz, 6, d)
    q = k = vt = None
    for l in range(depth):
        mods_l = mods[l]
        if l < n_a:
            x1 = _conv(x, mods_l, conv_w_pw1[l], conv_b_pw1[l], conv_w_dw[l], conv_b_dw[l], conv_ln_g[l],
                       conv_ln_b[l], conv_w_pw2[l], conv_b_pw2[l], post_ln_g[l, 0], post_ln_b[l, 0], alpha)
        else:
            j = l - n_a
            if j == 0:
                q, k, vt = _qkv(x, mods_l, w_kv, attn_w_q[j])
            else:
                q = _qkv(x, mods_l, w_kv, attn_w_q[j])[0]
            lambda_init = 0.8 - 0.6 * math.exp(-0.3 * l)
            ot = _attn(q, k, vt, attn_lambda[j], attn_subln_g[j], rel_bias_table, lambda_init)
            x1 = _attn_out(ot, x, mods_l, attn_w_o[j], post_ln_g[l, 0], post_ln_b[l, 0], alpha)
        x = _moe_layer(x1.reshape(bsz * s, d), mods_l, router_w[l], router_b[l],
                       expert_w_gate_up, expert_b_gate_up, expert_w_down, expert_b_down, l,
                       post_ln_g[l, 1], post_ln_b[l, 1], alpha, s).reshape(bsz, s, d)
    return x
```

```python
import functools
import math

import jax
import jax.numpy as jnp
from jax import lax
from jax.experimental import pallas as pl
from jax.experimental.pallas import tpu as pltpu

F32 = jnp.float32
BF16 = jnp.bfloat16
I32 = jnp.int32
U32 = jnp.uint32
HIGHEST = lax.Precision.HIGHEST

CHUNK = 64
CONV_WIDTH = 31
HEAD_DIM = 64
REL_BUCKETS = 32
REL_MAX_DIST = 128
TOP_K = 4
SWIGLU_LIMIT = 7.0
SWIGLU_ALPHA = 1.702
LN_EPS = 1e-5
MASK_VALUE = -1e30
LOG2_E = math.log2(math.e)

SUBLANES = 8
LANES = 128
VMEM_LIMIT_BYTES = 56 * 1024 * 1024

ADA_TN = 1024
SEQ_TILE = 512
PROJ_TILE = 1024
CONV_HALO = 32
CONV_ROWS = 256
CONV_COLS = 128
MOE_TOKENS = 256
ROUTER_TOKENS = 2048
MOE_TILES_PER_STEP = 2
MOE_TILE = 512
EXPERT_ROW_EIGHTHS = (8, 6, 4, 2)
Q_BLOCK = 128
GLU_LOOKAHEAD = 1
SCORE_LOOKAHEAD = 3
CAST_ROWS = 128

HALF_BITS = 16
HIGH_HALF_MASK = 0xFFFF0000


def _params(*sem):
    return pltpu.CompilerParams(dimension_semantics=sem, vmem_limit_bytes=VMEM_LIMIT_BYTES)


def _layer_norm(x, g, b):
    mu = jnp.mean(x, axis=-1, keepdims=True)
    xc = x - mu
    var = jnp.mean(xc * xc, axis=-1, keepdims=True)
    return xc * lax.rsqrt(var + LN_EPS) * g + b


def _dot(a, b):
    return jnp.dot(a, b, preferred_element_type=F32)


def _ada_body(c_ref, w_ref, b_ref, o_ref):
    c = c_ref[...]
    cond = c * jax.nn.sigmoid(c)
    o_ref[...] = jnp.dot(cond, w_ref[...], preferred_element_type=F32, precision=HIGHEST) + b_ref[...]


def _ada(c, ada_w, ada_b):
    depth, d, n = ada_w.shape
    bsz = c.shape[0]
    tn = min(ADA_TN, n)
    return pl.pallas_call(
        _ada_body,
        grid=(depth, n // tn),
        in_specs=[
            pl.BlockSpec((bsz, d), lambda l, j: (0, 0)),
            pl.BlockSpec((None, d, tn), lambda l, j: (l, 0, j)),
            pl.BlockSpec((None, 1, tn), lambda l, j: (l, 0, j)),
        ],
        out_specs=pl.BlockSpec((None, bsz, tn), lambda l, j: (l, 0, j)),
        out_shape=jax.ShapeDtypeStruct((depth, bsz, n), F32),
        compiler_params=_params("parallel", "parallel"),
        name="ada",
    )(c, ada_w, ada_b.reshape(depth, 1, n))


def _residual_epilogue(x, y, mod_ref, lng_ref, lnb_ref, alpha, x1_ref):
    x1_ref[...] = _layer_norm(alpha * x + mod_ref[2:3, :] * y, lng_ref[...], lnb_ref[...])


def _moe_input(x1_ref, mod_ref, rows=slice(None)):
    return x1_ref[rows, :] * (1.0 + mod_ref[4:5, :]) + mod_ref[3:4, :]


def _conv_body(x_ref, halo_ref, mod_ref, w1_ref, b1_ref, wdw_ref, bdw_ref, cg_ref, cb_ref, w2_ref, b2_ref,
               lng_ref, lnb_ref, x1_ref, win_ref, v_ref, *, alpha):
    ts, d = x_ref.shape
    i = pl.program_id(1)
    xw = jnp.concatenate([halo_ref[...], x_ref[...]], axis=0)
    h = (xw * (1.0 + mod_ref[1:2, :]) + mod_ref[0:1, :]).astype(BF16)
    rows = min(CONV_ROWS, ts)
    cols = min(CONV_COLS, d)

    def glu(c0):
        a = _dot(h, w1_ref[:, c0:c0 + cols]) + b1_ref[:, c0:c0 + cols]
        g = _dot(h, w1_ref[:, d + c0:d + c0 + cols]) + b1_ref[:, d + c0:d + c0 + cols]
        return a * jax.nn.sigmoid(g)

    off = CONV_HALO - (CONV_WIDTH - 1)
    n_shift = ts + CONV_HALO - SUBLANES
    in_seq = (lax.broadcasted_iota(I32, (ts + CONV_HALO, cols), 0) >= CONV_HALO) | (i > 0)
    starts = list(range(0, d, cols))
    ahead = [glu(c) for c in starts[:GLU_LOOKAHEAD]]
    for n, c0 in enumerate(starts):
        chan = slice(c0, c0 + cols)
        win_ref[0, :, chan] = jnp.where(in_seq, ahead.pop(0), 0.0)
        if n + GLU_LOOKAHEAD < len(starts):
            ahead.append(glu(starts[n + GLU_LOOKAHEAD]))
        for b in range(1, SUBLANES):
            win_ref[b, 0:n_shift, chan] = win_ref[0, b:b + n_shift, chan]
        for r0 in range(0, ts, rows):
            acc = jnp.zeros((rows, cols), F32)
            for j in range(CONV_WIDTH):
                a, b = divmod(off + j, SUBLANES)
                r = r0 + a * SUBLANES
                acc = acc + wdw_ref[j:j + 1, c0:c0 + cols] * win_ref[b, r:r + rows, c0:c0 + cols]
            v_ref[r0:r0 + rows, c0:c0 + cols] = acc
    v = _layer_norm(v_ref[...] + bdw_ref[...], cg_ref[...], cb_ref[...])
    v = (v * jax.nn.sigmoid(v)).astype(BF16)
    y = _dot(v, w2_ref[...]) + b2_ref[...]
    _residual_epilogue(x_ref[...], y, mod_ref, lng_ref, lnb_ref, alpha, x1_ref)


def _conv(x, mods_l, w_pw1, b_pw1, w_dw, b_dw, cln_g, cln_b, w_pw2, b_pw2, ln_g, ln_b, alpha):
    bsz, s, d = x.shape
    ts = min(SEQ_TILE, s)
    hb = ts // CONV_HALO
    row = lambda a: a.reshape(1, d)
    tile = pl.BlockSpec((None, ts, d), lambda b, i: (b, i, 0))
    vec = pl.BlockSpec((1, d), lambda b, i: (0, 0))
    return pl.pallas_call(
        functools.partial(_conv_body, alpha=alpha),
        grid=(bsz, s // ts),
        in_specs=[
            tile,
            pl.BlockSpec((None, CONV_HALO, d), lambda b, i: (b, jnp.maximum(i * hb - 1, 0), 0)),
            pl.BlockSpec((None, 6, d), lambda b, i: (b, 0, 0)),
            pl.BlockSpec((d, 2 * d), lambda b, i: (0, 0)),
            pl.BlockSpec((1, 2 * d), lambda b, i: (0, 0)),
            pl.BlockSpec((CONV_WIDTH, d), lambda b, i: (0, 0)),
            vec, vec, vec,
            pl.BlockSpec((d, d), lambda b, i: (0, 0)),
            vec, vec, vec,
        ],
        out_specs=tile,
        out_shape=jax.ShapeDtypeStruct((bsz, s, d), F32),
        scratch_shapes=[pltpu.VMEM((SUBLANES, ts + CONV_HALO, d), F32), pltpu.VMEM((ts, d), F32)],
        compiler_params=_params("parallel", "parallel"),
        name="conv",
    )(x, x, mods_l, w_pw1.astype(BF16), b_pw1.reshape(1, 2 * d), w_dw, row(b_dw), row(cln_g), row(cln_b),
      w_pw2.astype(BF16), row(b_pw2), row(ln_g), row(ln_b))


def _qkv_body(x_ref, mod_ref, wkv_ref, wq_ref, q_ref, k_ref, vt_ref):
    d = x_ref.shape[-1]
    x = x_ref[...]
    kv = _dot(x.astype(BF16), wkv_ref[...])
    k_ref[...] = kv[:, :d].astype(BF16)
    hd2 = 2 * HEAD_DIM
    for c0 in range(0, d, hd2):
        vt_ref[c0:c0 + hd2, :] = kv[:, d + c0:d + c0 + hd2].T.astype(BF16)
    h = (x * (1.0 + mod_ref[1:2, :]) + mod_ref[0:1, :]).astype(BF16)
    q_ref[...] = (_dot(h, wq_ref[...]) * (HEAD_DIM ** -0.5 * LOG2_E)).astype(BF16)


def _qkv(x, mods_l, w_kv, w_q):
    bsz, s, d = x.shape
    ts = min(PROJ_TILE, s)
    tile = pl.BlockSpec((None, ts, d), lambda b, i: (b, i, 0))
    return pl.pallas_call(
        _qkv_body,
        grid=(bsz, s // ts),
        in_specs=[
            tile,
            pl.BlockSpec((None, 6, d), lambda b, i: (b, 0, 0)),
            pl.BlockSpec((d, 2 * d), lambda b, i: (0, 0)),
            pl.BlockSpec((d, d), lambda b, i: (0, 0)),
        ],
        out_specs=[tile, tile, pl.BlockSpec((None, d, ts), lambda b, i: (b, 0, i))],
        out_shape=[jax.ShapeDtypeStruct((bsz, s, d), BF16)] * 2 + [jax.ShapeDtypeStruct((bsz, d, s), BF16)],
        compiler_params=_params("parallel", "parallel"),
        name="qkv",
    )(x, mods_l, w_kv.astype(BF16), w_q.astype(BF16))


def _t5_bucket(rel):
    nb = REL_BUCKETS // 2
    ret = jnp.where(rel > 0, nb, 0)
    n = jnp.abs(rel)
    max_exact = nb // 2
    large = max_exact + (jnp.log(jnp.maximum(n, 1).astype(F32) / max_exact)
                         / math.log(REL_MAX_DIST / max_exact) * (nb - max_exact)).astype(I32)
    large = jnp.minimum(large, nb - 1)
    return ret + jnp.where(n < max_exact, n, large)


def _saturation_distance():
    nb = REL_BUCKETS // 2
    max_exact = nb // 2
    n = max_exact
    while max_exact + math.log(n / max_exact) / math.log(REL_MAX_DIST / max_exact) * (nb - max_exact) < nb - 1 + 1e-3:
        n += 1
    return n


def _bucket_strip(s):
    r = jnp.arange(Q_BLOCK, dtype=I32)[:, None]
    kp = jnp.arange(s, dtype=I32)[None, :] - (s - Q_BLOCK)
    bucket = _t5_bucket(kp - r)
    visible = jnp.floor_divide(kp, CHUNK) <= (r // CHUNK)
    return jnp.where(visible, bucket, REL_BUCKETS)


def _attn_body(tab_ref, q_ref, k_ref, vt_ref, bkt_ref, lam_ref, sg_ref, ot_ref, bias_ref, *, lambda_init, n_heads):
    h = pl.program_id(0)
    b = pl.program_id(1)
    s = q_ref.shape[0]

    near = bias_ref.shape[0]

    @pl.when(b == 0)
    def _():
        bk = bkt_ref[...]
        acc = jnp.full(bk.shape, MASK_VALUE, F32)
        for r in range(REL_BUCKETS):
            acc = jnp.where(bk == r, tab_ref[r * n_heads + h] * LOG2_E, acc)
        bias_ref[:, 0:Q_BLOCK] = acc
        bias_ref[:, Q_BLOCK:] = acc

    far_bias = tab_ref[(REL_BUCKETS // 2 - 1) * n_heads + h] * LOG2_E

    lp = lam_ref[...]
    lam = (jnp.exp(jnp.sum(lp[0:1, :] * lp[1:2, :], axis=-1, keepdims=True))
           - jnp.exp(jnp.sum(lp[2:3, :] * lp[3:4, :], axis=-1, keepdims=True)) + lambda_init)
    lane = lax.broadcasted_iota(I32, (Q_BLOCK, 2 * HEAD_DIM), 1)
    nt = (((1,), (1,)), ((), ()))

    def scores(i):
        n_keys = (i + 1) * Q_BLOCK
        q = q_ref[i * Q_BLOCK:(i + 1) * Q_BLOCK, :]
        qq = jnp.concatenate([jnp.where(lane < HEAD_DIM, q, jnp.zeros_like(q)),
                              jnp.where(lane >= HEAD_DIM, q, jnp.zeros_like(q))], axis=0)
        return lax.dot_general(k_ref[0:n_keys, :], qq, nt, preferred_element_type=F32)

    def values(i, p, denom):
        pv = _dot(vt_ref[:, 0:(i + 1) * Q_BLOCK], p)
        o = pv[:, :Q_BLOCK] * (1.0 / denom[:, :Q_BLOCK]) - pv[:, Q_BLOCK:] * (lam / denom[:, Q_BLOCK:])
        o = o * lax.rsqrt(jnp.mean(o * o, axis=0, keepdims=True) + LN_EPS) * sg_ref[...]
        ot_ref[:, i * Q_BLOCK:(i + 1) * Q_BLOCK] = (o * (1.0 - lambda_init)).astype(BF16)

    n_blocks = s // Q_BLOCK
    ahead = [scores(i) for i in range(min(SCORE_LOOKAHEAD, n_blocks))]
    for i in range(n_blocks):
        raw = ahead.pop(0)
        if i + SCORE_LOOKAHEAD < n_blocks:
            ahead.append(scores(i + SCORE_LOOKAHEAD))
        n_keys = (i + 1) * Q_BLOCK
        n_near = min(near, n_keys)
        n_far = n_keys - n_near
        sc = raw[n_far:] + bias_ref[near - n_near:, :]
        top = jnp.max(sc, axis=0, keepdims=True)
        if n_far:
            top = jnp.maximum(top, jnp.max(raw[:n_far], axis=0, keepdims=True) + far_bias)
            p = jnp.concatenate([jnp.exp2(raw[:n_far] - (top - far_bias)), jnp.exp2(sc - top)], axis=0)
        else:
            p = jnp.exp2(sc - top)
        values(i, p.astype(BF16), jnp.sum(p, axis=0, keepdims=True))


def _attn(q, k, vt, lam_p, subln_g, rel_table, lambda_init):
    bsz, s, d = q.shape
    hd2 = 2 * HEAD_DIM
    n_heads = d // hd2
    assert REL_MAX_DIST >= CHUNK and _saturation_distance() <= REL_MAX_DIST
    near = min(Q_BLOCK + REL_MAX_DIST, s)
    head = pl.BlockSpec((None, s, hd2), lambda h, b, tab: (b, 0, h))
    head_t = pl.BlockSpec((None, hd2, s), lambda h, b, tab: (b, h, 0))
    grid_spec = pltpu.PrefetchScalarGridSpec(
        num_scalar_prefetch=1,
        grid=(n_heads, bsz),
        in_specs=[
            head, head, head_t,
            pl.BlockSpec((near, Q_BLOCK), lambda h, b, tab: (0, 0)),
            pl.BlockSpec((4, HEAD_DIM), lambda h, b, tab: (0, 0)),
            pl.BlockSpec((hd2, 1), lambda h, b, tab: (0, 0)),
        ],
        out_specs=head_t,
        scratch_shapes=[pltpu.VMEM((near, 2 * Q_BLOCK), F32)],
    )
    return pl.pallas_call(
        functools.partial(_attn_body, lambda_init=lambda_init, n_heads=n_heads),
        grid_spec=grid_spec,
        out_shape=jax.ShapeDtypeStruct((bsz, d, s), BF16),
        compiler_params=_params("arbitrary", "arbitrary"),
        name="attn",
    )(rel_table.reshape(-1), q, k, vt, _bucket_strip(s).T[s - near:], lam_p, subln_g.reshape(hd2, 1))


def _attn_out_body(ot_ref, x_ref, mod_ref, wo_ref, lng_ref, lnb_ref, x1_ref, *, alpha):
    y = lax.dot_general(ot_ref[...], wo_ref[...], (((0,), (0,)), ((), ())), preferred_element_type=F32)
    _residual_epilogue(x_ref[...], y, mod_ref, lng_ref, lnb_ref, alpha, x1_ref)


def _attn_out(o, x, mods_l, w_o, ln_g, ln_b, alpha):
    bsz, s, d = x.shape
    ts = min(PROJ_TILE, s)
    tile = pl.BlockSpec((None, ts, d), lambda b, i: (b, i, 0))
    vec = pl.BlockSpec((1, d), lambda b, i: (0, 0))
    return pl.pallas_call(
        functools.partial(_attn_out_body, alpha=alpha),
        grid=(bsz, s // ts),
        in_specs=[pl.BlockSpec((None, d, ts), lambda b, i: (b, 0, i)), tile,
                  pl.BlockSpec((None, 6, d), lambda b, i: (b, 0, 0)),
                  pl.BlockSpec((d, d), lambda b, i: (0, 0)), vec, vec],
        out_specs=tile,
        out_shape=jax.ShapeDtypeStruct((bsz, s, d), F32),
        compiler_params=_params("parallel", "parallel"),
        name="attn_out",
    )(o, x, mods_l, w_o.astype(BF16), ln_g.reshape(1, d), ln_b.reshape(1, d))


def _router_body(x1_ref, mod_ref, whi_ref, wlo_ref, b_ref, tri_ref, idx_ref, gate_ref, rank_ref, cnt_ref):
    h = _moe_input(x1_ref, mod_ref)
    h_hi = h.astype(BF16)
    h_lo = (h - h_hi.astype(F32)).astype(BF16)
    nt = (((1,), (1,)), ((), ()))
    logits = (lax.dot_general(whi_ref[...], h_hi, nt, preferred_element_type=F32)
              + (lax.dot_general(whi_ref[...], h_lo, nt, preferred_element_type=F32)
                 + lax.dot_general(wlo_ref[...], h_hi, nt, preferred_element_type=F32))) + b_ref[...]
    n_exp, tr = logits.shape
    eio = lax.broadcasted_iota(I32, (n_exp, tr), 0)
    work = logits
    vals, idxs = [], []
    for _ in range(TOP_K):
        m = jnp.max(work, axis=0, keepdims=True)
        am = jnp.min(jnp.where(work == m, eio, n_exp), axis=0, keepdims=True)
        vals.append(m)
        idxs.append(am)
        work = jnp.where(eio == am, -jnp.inf, work)
    ex = [jnp.exp(v - vals[0]) for v in vals]
    den = ex[0] + ex[1] + ex[2] + ex[3]
    onehot = jnp.zeros((n_exp, tr), F32)
    for k in range(TOP_K):
        onehot = onehot + (eio == idxs[k]).astype(F32)
    tl = tri_ref.shape[0]
    tiles = [onehot[:, t0:t0 + tl] for t0 in range(0, tr, tl)]
    before = jnp.concatenate([_dot(oh.astype(BF16), tri_ref[...]) for oh in tiles], axis=1)
    for k in range(TOP_K):
        idx_ref[k:k + 1, :] = idxs[k]
        gate_ref[k:k + 1, :] = ex[k] / den
        rank_ref[k:k + 1, :] = jnp.sum(jnp.where(eio == idxs[k], before, 0.0), axis=0, keepdims=True).astype(I32)
    for j, oh in enumerate(tiles):
        cnt_ref[j] = jnp.sum(oh, axis=1, keepdims=True).astype(I32)


def _router(x1, mods_l, seq_len, w_r, b_r):
    t, d = x1.shape
    n_exp = w_r.shape[1]
    tl = min(MOE_TOKENS, t)
    tr = min(ROUTER_TOKENS, seq_len)
    per_seq = seq_len // tr
    wt = w_r.T
    wt_hi = wt.astype(BF16)
    wt_lo = (wt - wt_hi.astype(F32)).astype(BF16)
    pos = jnp.arange(tl, dtype=I32)
    tri = (pos[:, None] < pos[None, :]).astype(BF16)
    tok = pl.BlockSpec((TOP_K, tr), lambda i: (0, i))
    return pl.pallas_call(
        _router_body,
        grid=(t // tr,),
        in_specs=[
            pl.BlockSpec((tr, d), lambda i: (i, 0)),
            pl.BlockSpec((None, 6, d), lambda i: (i // per_seq, 0, 0)),
            pl.BlockSpec((n_exp, d), lambda i: (0, 0)),
            pl.BlockSpec((n_exp, d), lambda i: (0, 0)),
            pl.BlockSpec((n_exp, 1), lambda i: (0, 0)),
            pl.BlockSpec((tl, tl), lambda i: (0, 0)),
        ],
        out_specs=[tok, tok, tok, pl.BlockSpec((tr // tl, n_exp, 1), lambda i: (i, 0, 0))],
        out_shape=[jax.ShapeDtypeStruct((TOP_K, t), I32), jax.ShapeDtypeStruct((TOP_K, t), F32),
                   jax.ShapeDtypeStruct((TOP_K, t), I32), jax.ShapeDtypeStruct((t // tl, n_exp, 1), I32)],
        compiler_params=_params("parallel"),
        name="router",
    )(x1, mods_l, wt_hi, wt_lo, b_r.reshape(n_exp, 1), tri)


def _rows(ref, start, cnt):
    aligned = lambda v: v if isinstance(v, int) else pl.multiple_of(v, SUBLANES)
    return ref.at[pl.ds(aligned(start), aligned(cnt))]


def _segment_copy(src, src_row, dst, dst_row, cnt, sem):
    @pl.when(cnt > 0)
    def _():
        pltpu.make_async_copy(_rows(src, src_row, cnt), _rows(dst, dst_row, cnt), sem).start()


def _wait_rows(ref, cnt, sem):
    @pl.when(cnt > 0)
    def _():
        pltpu.make_async_copy(_rows(ref, 0, cnt), _rows(ref, 0, cnt), sem).wait()


def _zero_fill(zero_ref, xs_hbm, start, cnt, sem):
    zr = zero_ref.shape[0]
    n_full = cnt // zr

    def full(r, c):
        _segment_copy(zero_ref, 0, xs_hbm, start + r * zr, zr, sem)
        return c

    lax.fori_loop(0, n_full, full, 0)
    _segment_copy(zero_ref, 0, xs_hbm, start + n_full * zr, cnt - n_full * zr, sem)


def _pack_halves(x):
    hd = x.shape[1] // 2
    lo = lax.bitcast_convert_type(x[:, :hd], U32)
    hi = lax.bitcast_convert_type(x[:, hd:], U32)
    return (lo >> HALF_BITS) | (hi & jnp.uint32(HIGH_HALF_MASK))


def _unpack_halves(words):
    lo = lax.bitcast_convert_type(words << HALF_BITS, F32).astype(BF16)
    hi = lax.bitcast_convert_type(words & jnp.uint32(HIGH_HALF_MASK), F32).astype(BF16)
    return lo, hi


def _tiles_per_step(seq_len, tl):
    return MOE_TILES_PER_STEP if seq_len % (tl * MOE_TILES_PER_STEP) == 0 else 1


def _one_hot_hits(iota, pos_of):
    hit = iota == pos_of(0)
    for k in range(1, TOP_K):
        hit = hit | (iota == pos_of(k))
    return hit


def _dispatch_body(goff_ref, loff_ref, cp_ref, ltot_ref, pad_start_ref, pad_cnt_ref, x1_ref, mod_ref, lp_ref,
                   xs_hbm, xl_ref, zero_ref, sem, zsem, *, n_exp, tiles_per_step):
    j = pl.program_id(0)
    d = x1_ref.shape[1]
    tl = x1_ref.shape[0] // tiles_per_step
    lr = xl_ref.shape[1]

    @pl.when(j == 0)
    def _():
        zero_ref[...] = jnp.zeros_like(zero_ref)

        def per_range(e, carry):
            _zero_fill(zero_ref, xs_hbm, pad_start_ref[e], pad_cnt_ref[e], zsem)
            return carry
        lax.fori_loop(0, pad_start_ref.shape[0], per_range, 0)

    def start_segments(tile):
        def per_expert(e, carry):
            seg = tile * n_exp + e
            _segment_copy(xl_ref.at[tile % 2], loff_ref[seg], xs_hbm, goff_ref[seg], cp_ref[seg], sem.at[tile % 2])
            return carry
        lax.fori_loop(0, n_exp, per_expert, 0)

    def wait_segments(tile):
        _wait_rows(xs_hbm, ltot_ref[tile], sem.at[tile % 2])

    riota = lax.broadcasted_iota(I32, (lr, tl), 0)
    for t in range(tiles_per_step):
        tile = j * tiles_per_step + t
        tok = slice(t * tl, (t + 1) * tl)

        @pl.when(tile >= 2)
        def _():
            wait_segments(tile - 2)

        buf = xl_ref.at[tile % 2]
        perm = _one_hot_hits(riota, lambda k: lp_ref[k:k + 1, tok]).astype(F32).astype(BF16)
        rows = _dot(perm, _moe_input(x1_ref, mod_ref, tok).astype(BF16))
        buf[...] = _pack_halves(rows)
        start_segments(tile)

    @pl.when(j == pl.num_programs(0) - 1)
    def _():
        last = pl.num_programs(0) * tiles_per_step - 1

        @pl.when(last >= 1)
        def _():
            wait_segments(last - 1)
        wait_segments(last)
        n_zero = lax.fori_loop(0, pad_cnt_ref.shape[0], lambda e, acc: acc + pad_cnt_ref[e], jnp.int32(0))
        _wait_rows(xs_hbm, n_zero, zsem)


def _dispatch(x1, mods_l, seq_len, lp, goff, loff, cp, ltot, pad_start, pad_cnt, n_rows):
    t, d = x1.shape
    tl = min(MOE_TOKENS, t)
    n_exp = goff.shape[0] // (t // tl)
    lr = TOP_K * tl + n_exp * SUBLANES
    tps = _tiles_per_step(seq_len, tl)
    ts = tl * tps
    per_seq = seq_len // ts
    tok = pl.BlockSpec((TOP_K, ts), lambda j, *_: (0, j))
    grid_spec = pltpu.PrefetchScalarGridSpec(
        num_scalar_prefetch=6,
        grid=(t // ts,),
        in_specs=[pl.BlockSpec((ts, d), lambda j, *_: (j, 0)),
                  pl.BlockSpec((None, 6, d), lambda j, *_: (j // per_seq, 0, 0)), tok],
        out_specs=pl.BlockSpec(memory_space=pl.ANY),
        scratch_shapes=[pltpu.VMEM((2, lr, d // 2), U32), pltpu.VMEM((MOE_TILE, d // 2), U32),
                        pltpu.SemaphoreType.DMA((2,)), pltpu.SemaphoreType.DMA],
    )
    return pl.pallas_call(
        functools.partial(_dispatch_body, n_exp=n_exp, tiles_per_step=tps),
        grid_spec=grid_spec,
        out_shape=jax.ShapeDtypeStruct((n_rows, d // 2), U32),
        compiler_params=_params("arbitrary"),
        name="dispatch",
    )(goff, loff, cp, ltot, pad_start, pad_cnt, x1, mods_l, lp)


def _experts_body(be_ref, bsrc_ref, rows_ref, first_ref, slot_ref, next_ref, xs_ref, wgu_hbm, bgu_ref, wdn_hbm,
                  bdn_ref, ys_ref, wgu_f32, wdn_f32, wgu_bf, wdn_bf, sem_gu, sem_dn, *, layer):
    i = pl.program_id(0)
    e = be_ref[i]
    d, f2 = wgu_bf.shape
    f = f2 // 2

    def fetch(expert, slot, start):
        for hbm, buf, sem in ((wgu_hbm, wgu_f32, sem_gu), (wdn_hbm, wdn_f32, sem_dn)):
            dma = pltpu.make_async_copy(hbm.at[layer, expert], buf.at[slot], sem.at[slot])
            dma.start() if start else dma.wait()

    @pl.when(first_ref[i] == 1)
    def _():
        slot = slot_ref[i]

        @pl.when(i == 0)
        def _():
            fetch(e, slot, True)

        fetch(e, slot, False)

        @pl.when(next_ref[i] >= 0)
        def _():
            fetch(next_ref[i], 1 - slot, True)

        def cast(ref_in, ref_out):
            def step(r, c):
                rows = pl.ds(pl.multiple_of(r * CAST_ROWS, CAST_ROWS), CAST_ROWS)
                ref_out[rows, :] = ref_in[rows, :].astype(BF16)
                return c
            lax.fori_loop(0, ref_in.shape[0] // CAST_ROWS, step, 0)
        cast(wgu_f32.at[slot], wgu_bf)
        cast(wdn_f32.at[slot], wdn_bf)

    hd = d // 2
    tm = xs_ref.shape[0]

    def mlp(n):
        x_lo, x_hi = _unpack_halves(xs_ref[0:n, :])
        proj = lambda cols: _dot(x_lo, wgu_bf[0:hd, cols]) + _dot(x_hi, wgu_bf[hd:, cols]) + bgu_ref[e, :, cols]
        gate = jnp.minimum(proj(slice(0, f)), SWIGLU_LIMIT)
        lin = jnp.clip(proj(slice(f, f2)), -SWIGLU_LIMIT, SWIGLU_LIMIT)
        act = (gate * jax.nn.sigmoid(SWIGLU_ALPHA * gate) * (lin + 1.0)).astype(BF16)
        y = _dot(act, wdn_bf[...]) + bdn_ref[e]
        ys_ref[0:n, :] = _pack_halves(y.astype(BF16).astype(F32))
        if n < tm:
            ys_ref[n:, :] = jnp.zeros((tm - n, hd), U32)

    rows = rows_ref[i]
    sizes = [tm * e // 8 for e in EXPERT_ROW_EIGHTHS]
    for n, smaller in zip(sizes, sizes[1:] + [0]):
        @pl.when((rows > smaller) & (rows <= n))
        def _(n=n):
            mlp(n)

    @pl.when(rows == 0)
    def _():
        ys_ref[...] = jnp.zeros_like(ys_ref)


def _experts(xs, block_e, block_src, block_rows, block_first, block_slot, block_next, w_gu, b_gu, w_dn, b_dn, layer):
    n_rows, dx = xs.shape
    depth, n_exp, d, f2 = w_gu.shape
    f = f2 // 2
    tm = MOE_TILE
    grid_spec = pltpu.PrefetchScalarGridSpec(
        num_scalar_prefetch=6,
        grid=(n_rows // tm,),
        in_specs=[
            pl.BlockSpec((tm, dx), lambda i, be, bs, *_: (bs[i], 0)),
            pl.BlockSpec(memory_space=pl.ANY),
            pl.BlockSpec((None, n_exp, 1, f2), lambda i, *_: (layer, 0, 0, 0)),
            pl.BlockSpec(memory_space=pl.ANY),
            pl.BlockSpec((None, n_exp, 1, d), lambda i, *_: (layer, 0, 0, 0)),
        ],
        out_specs=pl.BlockSpec((tm, d // 2), lambda i, *_: (i, 0)),
        scratch_shapes=[pltpu.VMEM((2, d, f2), F32), pltpu.VMEM((2, f, d), F32),
                        pltpu.VMEM((d, f2), BF16), pltpu.VMEM((f, d), BF16),
                        pltpu.SemaphoreType.DMA((2,)), pltpu.SemaphoreType.DMA((2,))],
    )
    return pl.pallas_call(
        functools.partial(_experts_body, layer=layer),
        grid_spec=grid_spec,
        out_shape=jax.ShapeDtypeStruct((n_rows, d // 2), U32),
        compiler_params=_params("arbitrary"),
        name="experts",
    )(block_e, block_src, block_rows, block_first, block_slot, block_next, xs, w_gu,
      b_gu.reshape(depth, n_exp, 1, f2), w_dn, b_dn.reshape(depth, n_exp, 1, d))


def _combine_body(goff_ref, loff_ref, cp_ref, ltot_ref, ys_hbm, x_ref, lp_ref, gate_ref, mod_ref, lng_ref, lnb_ref,
                  o_ref, yl_ref, sem, *, n_exp, alpha, tiles_per_step):
    j = pl.program_id(0)
    tl = x_ref.shape[0] // tiles_per_step
    n_tiles = pl.num_programs(0) * tiles_per_step
    lr, hd = yl_ref.shape[1:]

    def fetch(tile):
        buf = yl_ref.at[tile % 2]

        def per_expert(e, carry):
            seg = tile * n_exp + e
            _segment_copy(ys_hbm, goff_ref[seg], buf, loff_ref[seg], cp_ref[seg], sem.at[tile % 2])
            return carry
        lax.fori_loop(0, n_exp, per_expert, 0)

        def zero_rows(r, carry):
            buf[pl.ds(pl.multiple_of(r * SUBLANES, SUBLANES), SUBLANES), :] = jnp.zeros((SUBLANES, hd), U32)
            return carry
        lax.fori_loop(ltot_ref[tile] // SUBLANES, lr // SUBLANES, zero_rows, 0)

    @pl.when(j == 0)
    def _():
        fetch(j)

    liota = lax.broadcasted_iota(I32, (tl, lr), 1)
    for t in range(tiles_per_step):
        tile = j * tiles_per_step + t
        tok = slice(t * tl, (t + 1) * tl)

        @pl.when(tile + 1 < n_tiles)
        def _():
            fetch(tile + 1)

        pick = jnp.zeros((tl, lr), F32)
        for k in range(TOP_K):
            pick = pick + jnp.where(liota == lp_ref[tok, k:k + 1], gate_ref[tok, k:k + 1], 0.0)
        pick = pick.astype(BF16)
        _wait_rows(ys_hbm, ltot_ref[tile], sem.at[tile % 2])
        y_lo, y_hi = _unpack_halves(yl_ref[tile % 2])
        y = jnp.concatenate([_dot(pick, y_lo), _dot(pick, y_hi)], axis=1)
        o_ref[tok, :] = _layer_norm(alpha * x_ref[tok, :] + mod_ref[5:6, :] * y, lng_ref[...], lnb_ref[...])


def _combine(ys, lp_t, gates_t, goff, loff, cp, ltot, x1, mods_l, ln_g, ln_b, alpha, seq_len):
    t, d = x1.shape
    tl = min(MOE_TOKENS, t)
    n_exp = goff.shape[0] // (t // tl)
    lr = TOP_K * tl + n_exp * SUBLANES
    tps = _tiles_per_step(seq_len, tl)
    ts = tl * tps
    per_seq = seq_len // ts
    vec = pl.BlockSpec((1, d), lambda j, *_: (0, 0))
    grid_spec = pltpu.PrefetchScalarGridSpec(
        num_scalar_prefetch=4,
        grid=(t // ts,),
        in_specs=[
            pl.BlockSpec(memory_space=pl.ANY),
            pl.BlockSpec((ts, d), lambda j, *_: (j, 0)),
            pl.BlockSpec((ts, TOP_K), lambda j, *_: (j, 0)),
            pl.BlockSpec((ts, TOP_K), lambda j, *_: (j, 0)),
            pl.BlockSpec((None, 6, d), lambda j, *_: (j // per_seq, 0, 0)),
            vec, vec,
        ],
        out_specs=pl.BlockSpec((ts, d), lambda j, *_: (j, 0)),
        scratch_shapes=[pltpu.VMEM((2, lr, d // 2), U32), pltpu.SemaphoreType.DMA((2,))],
    )
    return pl.pallas_call(
        functools.partial(_combine_body, n_exp=n_exp, alpha=alpha, tiles_per_step=tps),
        grid_spec=grid_spec,
        out_shape=jax.ShapeDtypeStruct((t, d), F32),
        compiler_params=_params("arbitrary"),
        name="combine",
    )(goff, loff, cp, ltot, ys, x1, lp_t, gates_t, mods_l, ln_g.reshape(1, d), ln_b.reshape(1, d))


def _moe_layer(x1, mods_l, w_r, b_r, w_gu, b_gu, w_dn, b_dn, layer, ln_g, ln_b, alpha, seq_len):
    t, d = x1.shape
    n_exp = w_r.shape[1]
    tm = MOE_TILE
    tl = min(MOE_TOKENS, t)
    n_tiles = t // tl
    idx, gates, rank, cnt = _router(x1, mods_l, seq_len, w_r, b_r)
    cp = (cnt[:, :, 0] + SUBLANES - 1) // SUBLANES * SUBLANES
    tot = jnp.sum(cp, axis=0)
    padded = (tot + tm - 1) // tm * tm
    pend = jnp.cumsum(padded)
    pstart = pend - padded
    goff = pstart[None, :] + jnp.cumsum(cp, axis=0) - cp
    loff = jnp.cumsum(cp, axis=1) - cp
    ltot = jnp.sum(cp, axis=1).astype(I32)
    is_e = idx[..., None] == jnp.arange(n_exp, dtype=I32)
    lp = jnp.sum(jnp.where(is_e, jnp.repeat(loff, tl, axis=0)[None], 0), axis=-1) + rank
    n_blocks = -(-(t * TOP_K + n_tiles * n_exp * (SUBLANES - 1)) // tm) + n_exp
    n_valid = pend[-1] // tm
    blk = jnp.arange(n_blocks, dtype=I32)
    block_src = jnp.minimum(blk, n_valid - 1)
    block_e = jnp.sum(pend[None, :] <= (block_src * tm)[:, None], axis=1).astype(I32)
    used = padded > 0
    experts = jnp.arange(n_exp, dtype=I32)
    next_used = jnp.flip(lax.cummin(jnp.flip(jnp.where(used, experts, n_exp))))
    next_used = jnp.concatenate([next_used[1:], jnp.full((1,), n_exp, I32)])
    next_used = jnp.where(next_used < n_exp, next_used, -1)
    slot_of = (jnp.cumsum(used.astype(I32)) - 1) % 2
    of_block = lambda table: jnp.sum(jnp.where(block_e[:, None] == experts[None, :], table[None, :], 0), axis=1)
    block_first = ((blk * tm == of_block(pstart)) & (blk < n_valid)).astype(I32)
    block_slot = of_block(slot_of).astype(I32)
    block_next = of_block(next_used).astype(I32)
    block_rows = jnp.where(blk < n_valid, jnp.clip(of_block(pstart + tot) - blk * tm, 0, tm), 0).astype(I32)
    n_rows = n_blocks * tm
    pad_start = jnp.concatenate([pstart + tot, pend[-1:]]).astype(I32)
    pad_cnt = jnp.concatenate([padded - tot, n_rows - pend[-1:]]).astype(I32)
    flat = lambda a: a.reshape(-1).astype(I32)
    seg = (flat(goff), flat(loff), flat(cp), ltot)
    xs = _dispatch(x1, mods_l, seq_len, lp, *seg, pad_start, pad_cnt, n_rows)
    ys = _experts(xs, block_e, block_src.astype(I32), block_rows, block_first, block_slot, block_next,
                  w_gu, b_gu, w_dn, b_dn, layer)
    return _combine(ys, lp.T, gates.T, *seg, x1, mods_l, ln_g, ln_b, alpha, seq_len)


def kernel(x, c, ada_w, ada_b, post_ln_g, post_ln_b, conv_w_pw1, conv_b_pw1, conv_w_dw, conv_b_dw, conv_ln_g, conv_ln_b, conv_w_pw2, conv_b_pw2, w_kv, attn_w_q, attn_lambda, attn_subln_g, attn_w_o, rel_bias_table, router_w, router_b, expert_w_gate_up, expert_b_gate_up, expert_w_down, expert_b_down):
    bsz, s, d = x.shape
    depth = ada_w.shape[0]
    n_a = depth // 2
    alpha = (2 * depth) ** 0.25
    mods = _ada(c, ada_w, ada_b).reshape(depth, bsz, 6, d)
    q = k = vt = None
    for l in range(depth):
        mods_l = mods[l]
        if l < n_a:
            x1 = _conv(x, mods_l, conv_w_pw1[l], conv_b_pw1[l], conv_w_dw[l], conv_b_dw[l], conv_ln_g[l],
                       conv_ln_b[l], conv_w_pw2[l], conv_b_pw2[l], post_ln_g[l, 0], post_ln_b[l, 0], alpha)
        else:
            j = l - n_a
            if j == 0:
                q, k, vt = _qkv(x, mods_l, w_kv, attn_w_q[j])
            else:
                q = _qkv(x, mods_l, w_kv, attn_w_q[j])[0]
            lambda_init = 0.8 - 0.6 * math.exp(-0.3 * l)
            ot = _attn(q, k, vt, attn_lambda[j], attn_subln_g[j], rel_bias_table, lambda_init)
            x1 = _attn_out(ot, x, mods_l, attn_w_o[j], post_ln_g[l, 0], post_ln_b[l, 0], alpha)
        x = _moe_layer(x1.reshape(bsz * s, d), mods_l, router_w[l], router_b[l],
                       expert_w_gate_up, expert_b_gate_up, expert_w_down, expert_b_down, l,
                       post_ln_g[l, 1], post_ln_b[l, 1], alpha, s).reshape(bsz, s, d)
    return x
```

```python
import functools
import math

import jax
import jax.numpy as jnp
from jax import lax
from jax.experimental import pallas as pl
from jax.experimental.pallas import tpu as pltpu

F32 = jnp.float32
BF16 = jnp.bfloat16
I32 = jnp.int32
U32 = jnp.uint32
HIGHEST = lax.Precision.HIGHEST

CHUNK = 64
CONV_WIDTH = 31
HEAD_DIM = 64
REL_BUCKETS = 32
REL_MAX_DIST = 128
TOP_K = 4
SWIGLU_LIMIT = 7.0
SWIGLU_ALPHA = 1.702
LN_EPS = 1e-5
MASK_VALUE = -1e30
LOG2_E = math.log2(math.e)

SUBLANES = 8
LANES = 128
VMEM_LIMIT_BYTES = 56 * 1024 * 1024

ADA_TN = 1024
SEQ_TILE = 512
PROJ_TILE = 1024
CONV_HALO = 32
CONV_ROWS = 256
CONV_COLS = 128
MOE_TOKENS = 256
ROUTER_TOKENS = 2048
MOE_TILES_PER_STEP = 2
MOE_TILE = 512
EXPERT_ROW_EIGHTHS = (8, 6, 4, 2)
Q_BLOCK = 128
GLU_LOOKAHEAD = 1
SCORE_LOOKAHEAD = 3
CAST_ROWS = 128

HALF_BITS = 16
HIGH_HALF_MASK = 0xFFFF0000


def _params(*sem):
    return pltpu.CompilerParams(dimension_semantics=sem, vmem_limit_bytes=VMEM_LIMIT_BYTES)


def _layer_norm(x, g, b):
    mu = jnp.mean(x, axis=-1, keepdims=True)
    xc = x - mu
    var = jnp.mean(xc * xc, axis=-1, keepdims=True)
    return xc * lax.rsqrt(var + LN_EPS) * g + b


def _dot(a, b):
    return jnp.dot(a, b, preferred_element_type=F32)


def _ada_body(c_ref, w_ref, b_ref, o_ref):
    c = c_ref[...]
    cond = c * jax.nn.sigmoid(c)
    o_ref[...] = jnp.dot(cond, w_ref[...], preferred_element_type=F32, precision=HIGHEST) + b_ref[...]


def _ada(c, ada_w, ada_b):
    depth, d, n = ada_w.shape
    bsz = c.shape[0]
    tn = min(ADA_TN, n)
    return pl.pallas_call(
        _ada_body,
        grid=(depth, n // tn),
        in_specs=[
            pl.BlockSpec((bsz, d), lambda l, j: (0, 0)),
            pl.BlockSpec((None, d, tn), lambda l, j: (l, 0, j)),
            pl.BlockSpec((None, 1, tn), lambda l, j: (l, 0, j)),
        ],
        out_specs=pl.BlockSpec((None, bsz, tn), lambda l, j: (l, 0, j)),
        out_shape=jax.ShapeDtypeStruct((depth, bsz, n), F32),
        compiler_params=_params("parallel", "parallel"),
        name="ada",
    )(c, ada_w, ada_b.reshape(depth, 1, n))


def _residual_epilogue(x, y, mod_ref, lng_ref, lnb_ref, alpha, x1_ref):
    x1_ref[...] = _layer_norm(alpha * x + mod_ref[2:3, :] * y, lng_ref[...], lnb_ref[...])


def _moe_input(x1_ref, mod_ref, rows=slice(None)):
    return x1_ref[rows, :] * (1.0 + mod_ref[4:5, :]) + mod_ref[3:4, :]


def _conv_body(x_ref, halo_ref, mod_ref, w1_ref, b1_ref, wdw_ref, bdw_ref, cg_ref, cb_ref, w2_ref, b2_ref,
               lng_ref, lnb_ref, x1_ref, win_ref, v_ref, *, alpha):
    ts, d = x_ref.shape
    i = pl.program_id(1)
    xw = jnp.concatenate([halo_ref[...], x_ref[...]], axis=0)
    h = (xw * (1.0 + mod_ref[1:2, :]) + mod_ref[0:1, :]).astype(BF16)
    rows = min(CONV_ROWS, ts)
    cols = min(CONV_COLS, d)

    def glu(c0):
        a = _dot(h, w1_ref[:, c0:c0 + cols]) + b1_ref[:, c0:c0 + cols]
        g = _dot(h, w1_ref[:, d + c0:d + c0 + cols]) + b1_ref[:, d + c0:d + c0 + cols]
        return a * jax.nn.sigmoid(g)

    off = CONV_HALO - (CONV_WIDTH - 1)
    n_shift = ts + CONV_HALO - SUBLANES
    in_seq = (lax.broadcasted_iota(I32, (ts + CONV_HALO, cols), 0) >= CONV_HALO) | (i > 0)
    starts = list(range(0, d, cols))
    ahead = [glu(c) for c in starts[:GLU_LOOKAHEAD]]
    for n, c0 in enumerate(starts):
        chan = slice(c0, c0 + cols)
        win_ref[0, :, chan] = jnp.where(in_seq, ahead.pop(0), 0.0)
        if n + GLU_LOOKAHEAD < len(starts):
            ahead.append(glu(starts[n + GLU_LOOKAHEAD]))
        for b in range(1, SUBLANES):
            win_ref[b, 0:n_shift, chan] = win_ref[0, b:b + n_shift, chan]
        for r0 in range(0, ts, rows):
            acc = jnp.zeros((rows, cols), F32)
            for j in range(CONV_WIDTH):
                a, b = divmod(off + j, SUBLANES)
                r = r0 + a * SUBLANES
                acc = acc + wdw_ref[j:j + 1, c0:c0 + cols] * win_ref[b, r:r + rows, c0:c0 + cols]
            v_ref[r0:r0 + rows, c0:c0 + cols] = acc
    v = _layer_norm(v_ref[...] + bdw_ref[...], cg_ref[...], cb_ref[...])
    v = (v * jax.nn.sigmoid(v)).astype(BF16)
    y = _dot(v, w2_ref[...]) + b2_ref[...]
    _residual_epilogue(x_ref[...], y, mod_ref, lng_ref, lnb_ref, alpha, x1_ref)


def _conv(x, mods_l, w_pw1, b_pw1, w_dw, b_dw, cln_g, cln_b, w_pw2, b_pw2, ln_g, ln_b, alpha):
    bsz, s, d = x.shape
    ts = min(SEQ_TILE, s)
    hb = ts // CONV_HALO
    row = lambda a: a.reshape(1, d)
    tile = pl.BlockSpec((None, ts, d), lambda b, i: (b, i, 0))
    vec = pl.BlockSpec((1, d), lambda b, i: (0, 0))
    return pl.pallas_call(
        functools.partial(_conv_body, alpha=alpha),
        grid=(bsz, s // ts),
        in_specs=[
            tile,
            pl.BlockSpec((None, CONV_HALO, d), lambda b, i: (b, jnp.maximum(i * hb - 1, 0), 0)),
            pl.BlockSpec((None, 6, d), lambda b, i: (b, 0, 0)),
            pl.BlockSpec((d, 2 * d), lambda b, i: (0, 0)),
            pl.BlockSpec((1, 2 * d), lambda b, i: (0, 0)),
            pl.BlockSpec((CONV_WIDTH, d), lambda b, i: (0, 0)),
            vec, vec, vec,
            pl.BlockSpec((d, d), lambda b, i: (0, 0)),
            vec, vec, vec,
        ],
        out_specs=tile,
        out_shape=jax.ShapeDtypeStruct((bsz, s, d), F32),
        scratch_shapes=[pltpu.VMEM((SUBLANES, ts + CONV_HALO, d), F32), pltpu.VMEM((ts, d), F32)],
        compiler_params=_params("parallel", "parallel"),
        name="conv",
    )(x, x, mods_l, w_pw1.astype(BF16), b_pw1.reshape(1, 2 * d), w_dw, row(b_dw), row(cln_g), row(cln_b),
      w_pw2.astype(BF16), row(b_pw2), row(ln_g), row(ln_b))


def _qkv_body(x_ref, mod_ref, wkv_ref, wq_ref, q_ref, k_ref, vt_ref):
    d = x_ref.shape[-1]
    x = x_ref[...]
    kv = _dot(x.astype(BF16), wkv_ref[...])
    k_ref[...] = kv[:, :d].astype(BF16)
    hd2 = 2 * HEAD_DIM
    for c0 in range(0, d, hd2):
        vt_ref[c0:c0 + hd2, :] = kv[:, d + c0:d + c0 + hd2].T.astype(BF16)
    h = (x * (1.0 + mod_ref[1:2, :]) + mod_ref[0:1, :]).astype(BF16)
    q_ref[...] = (_dot(h, wq_ref[...]) * (HEAD_DIM ** -0.5 * LOG2_E)).astype(BF16)


def _qkv(x, mods_l, w_kv, w_q):
    bsz, s, d = x.shape
    ts = min(PROJ_TILE, s)
    tile = pl.BlockSpec((None, ts, d), lambda b, i: (b, i, 0))
    return pl.pallas_call(
        _qkv_body,
        grid=(bsz, s // ts),
        in_specs=[
            tile,
            pl.BlockSpec((None, 6, d), lambda b, i: (b, 0, 0)),
            pl.BlockSpec((d, 2 * d), lambda b, i: (0, 0)),
            pl.BlockSpec((d, d), lambda b, i: (0, 0)),
        ],
        out_specs=[tile, tile, pl.BlockSpec((None, d, ts), lambda b, i: (b, 0, i))],
        out_shape=[jax.ShapeDtypeStruct((bsz, s, d), BF16)] * 2 + [jax.ShapeDtypeStruct((bsz, d, s), BF16)],
        compiler_params=_params("parallel", "parallel"),
        name="qkv",
    )(x, mods_l, w_kv.astype(BF16), w_q.astype(BF16))


def _t5_bucket(rel):
    nb = REL_BUCKETS // 2
    ret = jnp.where(rel > 0, nb, 0)
    n = jnp.abs(rel)
    max_exact = nb // 2
    large = max_exact + (jnp.log(jnp.maximum(n, 1).astype(F32) / max_exact)
                         / math.log(REL_MAX_DIST / max_exact) * (nb - max_exact)).astype(I32)
    large = jnp.minimum(large, nb - 1)
    return ret + jnp.where(n < max_exact, n, large)


def _saturation_distance():
    nb = REL_BUCKETS // 2
    max_exact = nb // 2
    n = max_exact
    while max_exact + math.log(n / max_exact) / math.log(REL_MAX_DIST / max_exact) * (nb - max_exact) < nb - 1 + 1e-3:
        n += 1
    return n


def _bucket_strip(s):
    r = jnp.arange(Q_BLOCK, dtype=I32)[:, None]
    kp = jnp.arange(s, dtype=I32)[None, :] - (s - Q_BLOCK)
    bucket = _t5_bucket(kp - r)
    visible = jnp.floor_divide(kp, CHUNK) <= (r // CHUNK)
    return jnp.where(visible, bucket, REL_BUCKETS)


def _attn_body(tab_ref, q_ref, k_ref, vt_ref, bkt_ref, lam_ref, sg_ref, ot_ref, bias_ref, *, lambda_init, n_heads):
    h = pl.program_id(0)
    b = pl.program_id(1)
    s = q_ref.shape[0]

    near = bias_ref.shape[0]

    @pl.when(b == 0)
    def _():
        bk = bkt_ref[...]
        acc = jnp.full(bk.shape, MASK_VALUE, F32)
        for r in range(REL_BUCKETS):
            acc = jnp.where(bk == r, tab_ref[r * n_heads + h] * LOG2_E, acc)
        bias_ref[:, 0:Q_BLOCK] = acc
        bias_ref[:, Q_BLOCK:] = acc

    far_bias = tab_ref[(REL_BUCKETS // 2 - 1) * n_heads + h] * LOG2_E

    lp = lam_ref[...]
    lam = (jnp.exp(jnp.sum(lp[0:1, :] * lp[1:2, :], axis=-1, keepdims=True))
           - jnp.exp(jnp.sum(lp[2:3, :] * lp[3:4, :], axis=-1, keepdims=True)) + lambda_init)
    lane = lax.broadcasted_iota(I32, (Q_BLOCK, 2 * HEAD_DIM), 1)
    nt = (((1,), (1,)), ((), ()))

    def scores(i):
        n_keys = (i + 1) * Q_BLOCK
        q = q_ref[i * Q_BLOCK:(i + 1) * Q_BLOCK, :]
        qq = jnp.concatenate([jnp.where(lane < HEAD_DIM, q, jnp.zeros_like(q)),
                              jnp.where(lane >= HEAD_DIM, q, jnp.zeros_like(q))], axis=0)
        return lax.dot_general(k_ref[0:n_keys, :], qq, nt, preferred_element_type=F32)

    def values(i, p, denom):
        pv = _dot(vt_ref[:, 0:(i + 1) * Q_BLOCK], p)
        o = pv[:, :Q_BLOCK] * (1.0 / denom[:, :Q_BLOCK]) - pv[:, Q_BLOCK:] * (lam / denom[:, Q_BLOCK:])
        o = o * lax.rsqrt(jnp.mean(o * o, axis=0, keepdims=True) + LN_EPS) * sg_ref[...]
        ot_ref[:, i * Q_BLOCK:(i + 1) * Q_BLOCK] = (o * (1.0 - lambda_init)).astype(BF16)

    n_blocks = s // Q_BLOCK
    ahead = [scores(i) for i in range(min(SCORE_LOOKAHEAD, n_blocks))]
    for i in range(n_blocks):
        raw = ahead.pop(0)
        if i + SCORE_LOOKAHEAD < n_blocks:
            ahead.append(scores(i + SCORE_LOOKAHEAD))
        n_keys = (i + 1) * Q_BLOCK
        n_near = min(near, n_keys)
        n_far = n_keys - n_near
        sc = raw[n_far:] + bias_ref[near - n_near:, :]
        top = jnp.max(sc, axis=0, keepdims=True)
        if n_far:
            top = jnp.maximum(top, jnp.max(raw[:n_far], axis=0, keepdims=True) + far_bias)
            p = jnp.concatenate([jnp.exp2(raw[:n_far] - (top - far_bias)), jnp.exp2(sc - top)], axis=0)
        else:
            p = jnp.exp2(sc - top)
        values(i, p.astype(BF16), jnp.sum(p, axis=0, keepdims=True))


def _attn(q, k, vt, lam_p, subln_g, rel_table, lambda_init):
    bsz, s, d = q.shape
    hd2 = 2 * HEAD_DIM
    n_heads = d // hd2
    assert REL_MAX_DIST >= CHUNK and _saturation_distance() <= REL_MAX_DIST
    near = min(Q_BLOCK + REL_MAX_DIST, s)
    head = pl.BlockSpec((None, s, hd2), lambda h, b, tab: (b, 0, h))
    head_t = pl.BlockSpec((None, hd2, s), lambda h, b, tab: (b, h, 0))
    grid_spec = pltpu.PrefetchScalarGridSpec(
        num_scalar_prefetch=1,
        grid=(n_heads, bsz),
        in_specs=[
            head, head, head_t,
            pl.BlockSpec((near, Q_BLOCK), lambda h, b, tab: (0, 0)),
            pl.BlockSpec((4, HEAD_DIM), lambda h, b, tab: (0, 0)),
            pl.BlockSpec((hd2, 1), lambda h, b, tab: (0, 0)),
        ],
        out_specs=head_t,
        scratch_shapes=[pltpu.VMEM((near, 2 * Q_BLOCK), F32)],
    )
    return pl.pallas_call(
        functools.partial(_attn_body, lambda_init=lambda_init, n_heads=n_heads),
        grid_spec=grid_spec,
        out_shape=jax.ShapeDtypeStruct((bsz, d, s), BF16),
        compiler_params=_params("arbitrary", "arbitrary"),
        name="attn",
    )(rel_table.reshape(-1), q, k, vt, _bucket_strip(s).T[s - near:], lam_p, subln_g.reshape(hd2, 1))


def _attn_out_body(ot_ref, x_ref, mod_ref, wo_ref, lng_ref, lnb_ref, x1_ref, *, alpha):
    y = lax.dot_general(ot_ref[...], wo_ref[...], (((0,), (0,)), ((), ())), preferred_element_type=F32)
    _residual_epilogue(x_ref[...], y, mod_ref, lng_ref, lnb_ref, alpha, x1_ref)


def _attn_out(o, x, mods_l, w_o, ln_g, ln_b, alpha):
    bsz, s, d = x.shape
    ts = min(PROJ_TILE, s)
    tile = pl.BlockSpec((None, ts, d), lambda b, i: (b, i, 0))
    vec = pl.BlockSpec((1, d), lambda b, i: (0, 0))
    return pl.pallas_call(
        functools.partial(_attn_out_body, alpha=alpha),
        grid=(bsz, s // ts),
        in_specs=[pl.BlockSpec((None, d, ts), lambda b, i: (b, 0, i)), tile,
                  pl.BlockSpec((None, 6, d), lambda b, i: (b, 0, 0)),
                  pl.BlockSpec((d, d), lambda b, i: (0, 0)), vec, vec],
        out_specs=tile,
        out_shape=jax.ShapeDtypeStruct((bsz, s, d), F32),
        compiler_params=_params("parallel", "parallel"),
        name="attn_out",
    )(o, x, mods_l, w_o.astype(BF16), ln_g.reshape(1, d), ln_b.reshape(1, d))


def _router_body(x1_ref, mod_ref, whi_ref, wlo_ref, b_ref, tri_ref, idx_ref, gate_ref, rank_ref, cnt_ref):
    h = _moe_input(x1_ref, mod_ref)
    h_hi = h.astype(BF16)
    h_lo = (h - h_hi.astype(F32)).astype(BF16)
    nt = (((1,), (1,)), ((), ()))
    logits = (lax.dot_general(whi_ref[...], h_hi, nt, preferred_element_type=F32)
              + (lax.dot_general(whi_ref[...], h_lo, nt, preferred_element_type=F32)
                 + lax.dot_general(wlo_ref[...], h_hi, nt, preferred_element_type=F32))) + b_ref[...]
    n_exp, tr = logits.shape
    eio = lax.broadcasted_iota(I32, (n_exp, tr), 0)
    work = logits
    vals, idxs = [], []
    for _ in range(TOP_K):
        m = jnp.max(work, axis=0, keepdims=True)
        am = jnp.min(jnp.where(work == m, eio, n_exp), axis=0, keepdims=True)
        vals.append(m)
        idxs.append(am)
        work = jnp.where(eio == am, -jnp.inf, work)
    ex = [jnp.exp(v - vals[0]) for v in vals]
    den = ex[0] + ex[1] + ex[2] + ex[3]
    onehot = jnp.zeros((n_exp, tr), F32)
    for k in range(TOP_K):
        onehot = onehot + (eio == idxs[k]).astype(F32)
    tl = tri_ref.shape[0]
    tiles = [onehot[:, t0:t0 + tl] for t0 in range(0, tr, tl)]
    before = jnp.concatenate([_dot(oh.astype(BF16), tri_ref[...]) for oh in tiles], axis=1)
    for k in range(TOP_K):
        idx_ref[k:k + 1, :] = idxs[k]
        gate_ref[k:k + 1, :] = ex[k] / den
        rank_ref[k:k + 1, :] = jnp.sum(jnp.where(eio == idxs[k], before, 0.0), axis=0, keepdims=True).astype(I32)
    for j, oh in enumerate(tiles):
        cnt_ref[j] = jnp.sum(oh, axis=1, keepdims=True).astype(I32)


def _router(x1, mods_l, seq_len, w_r, b_r):
    t, d = x1.shape
    n_exp = w_r.shape[1]
    tl = min(MOE_TOKENS, t)
    tr = min(ROUTER_TOKENS, seq_len)
    per_seq = seq_len // tr
    wt = w_r.T
    wt_hi = wt.astype(BF16)
    wt_lo = (wt - wt_hi.astype(F32)).astype(BF16)
    pos = jnp.arange(tl, dtype=I32)
    tri = (pos[:, None] < pos[None, :]).astype(BF16)
    tok = pl.BlockSpec((TOP_K, tr), lambda i: (0, i))
    return pl.pallas_call(
        _router_body,
        grid=(t // tr,),
        in_specs=[
            pl.BlockSpec((tr, d), lambda i: (i, 0)),
            pl.BlockSpec((None, 6, d), lambda i: (i // per_seq, 0, 0)),
            pl.BlockSpec((n_exp, d), lambda i: (0, 0)),
            pl.BlockSpec((n_exp, d), lambda i: (0, 0)),
            pl.BlockSpec((n_exp, 1), lambda i: (0, 0)),
            pl.BlockSpec((tl, tl), lambda i: (0, 0)),
        ],
        out_specs=[tok, tok, tok, pl.BlockSpec((tr // tl, n_exp, 1), lambda i: (i, 0, 0))],
        out_shape=[jax.ShapeDtypeStruct((TOP_K, t), I32), jax.ShapeDtypeStruct((TOP_K, t), F32),
                   jax.ShapeDtypeStruct((TOP_K, t), I32), jax.ShapeDtypeStruct((t // tl, n_exp, 1), I32)],
        compiler_params=_params("parallel"),
        name="router",
    )(x1, mods_l, wt_hi, wt_lo, b_r.reshape(n_exp, 1), tri)


def _rows(ref, start, cnt):
    aligned = lambda v: v if isinstance(v, int) else pl.multiple_of(v, SUBLANES)
    return ref.at[pl.ds(aligned(start), aligned(cnt))]


def _segment_copy(src, src_row, dst, dst_row, cnt, sem):
    @pl.when(cnt > 0)
    def _():
        pltpu.make_async_copy(_rows(src, src_row, cnt), _rows(dst, dst_row, cnt), sem).start()


def _wait_rows(ref, cnt, sem):
    @pl.when(cnt > 0)
    def _():
        pltpu.make_async_copy(_rows(ref, 0, cnt), _rows(ref, 0, cnt), sem).wait()


def _zero_fill(zero_ref, xs_hbm, start, cnt, sem):
    zr = zero_ref.shape[0]
    n_full = cnt // zr

    def full(r, c):
        _segment_copy(zero_ref, 0, xs_hbm, start + r * zr, zr, sem)
        return c

    lax.fori_loop(0, n_full, full, 0)
    _segment_copy(zero_ref, 0, xs_hbm, start + n_full * zr, cnt - n_full * zr, sem)


def _pack_halves(x):
    hd = x.shape[1] // 2
    lo = lax.bitcast_convert_type(x[:, :hd], U32)
    hi = lax.bitcast_convert_type(x[:, hd:], U32)
    return (lo >> HALF_BITS) | (hi & jnp.uint32(HIGH_HALF_MASK))


def _unpack_halves(words):
    lo = lax.bitcast_convert_type(words << HALF_BITS, F32).astype(BF16)
    hi = lax.bitcast_convert_type(words & jnp.uint32(HIGH_HALF_MASK), F32).astype(BF16)
    return lo, hi


def _tiles_per_step(seq_len, tl):
    return MOE_TILES_PER_STEP if seq_len % (tl * MOE_TILES_PER_STEP) == 0 else 1


def _one_hot_hits(iota, pos_of):
    hit = iota == pos_of(0)
    for k in range(1, TOP_K):
        hit = hit | (iota == pos_of(k))
    return hit


def _dispatch_body(goff_ref, loff_ref, cp_ref, ltot_ref, pad_start_ref, pad_cnt_ref, x1_ref, mod_ref, lp_ref,
                   xs_hbm, xl_ref, zero_ref, sem, zsem, *, n_exp, tiles_per_step):
    j = pl.program_id(0)
    d = x1_ref.shape[1]
    tl = x1_ref.shape[0] // tiles_per_step
    lr = xl_ref.shape[1]

    @pl.when(j == 0)
    def _():
        zero_ref[...] = jnp.zeros_like(zero_ref)

        def per_range(e, carry):
            _zero_fill(zero_ref, xs_hbm, pad_start_ref[e], pad_cnt_ref[e], zsem)
            return carry
        lax.fori_loop(0, pad_start_ref.shape[0], per_range, 0)

    def start_segments(tile):
        def per_expert(e, carry):
            seg = tile * n_exp + e
            _segment_copy(xl_ref.at[tile % 2], loff_ref[seg], xs_hbm, goff_ref[seg], cp_ref[seg], sem.at[tile % 2])
            return carry
        lax.fori_loop(0, n_exp, per_expert, 0)

    def wait_segments(tile):
        _wait_rows(xs_hbm, ltot_ref[tile], sem.at[tile % 2])

    riota = lax.broadcasted_iota(I32, (lr, tl), 0)
    for t in range(tiles_per_step):
        tile = j * tiles_per_step + t
        tok = slice(t * tl, (t + 1) * tl)

        @pl.when(tile >= 2)
        def _():
            wait_segments(tile - 2)

        buf = xl_ref.at[tile % 2]
        perm = _one_hot_hits(riota, lambda k: lp_ref[k:k + 1, tok]).astype(F32).astype(BF16)
        rows = _dot(perm, _moe_input(x1_ref, mod_ref, tok).astype(BF16))
        buf[...] = _pack_halves(rows)
        start_segments(tile)

    @pl.when(j == pl.num_programs(0) - 1)
    def _():
        last = pl.num_programs(0) * tiles_per_step - 1

        @pl.when(last >= 1)
        def _():
            wait_segments(last - 1)
        wait_segments(last)
        n_zero = lax.fori_loop(0, pad_cnt_ref.shape[0], lambda e, acc: acc + pad_cnt_ref[e], jnp.int32(0))
        _wait_rows(xs_hbm, n_zero, zsem)


def _dispatch(x1, mods_l, seq_len, lp, goff, loff, cp, ltot, pad_start, pad_cnt, n_rows):
    t, d = x1.shape
    tl = min(MOE_TOKENS, t)
    n_exp = goff.shape[0] // (t // tl)
    lr = TOP_K * tl + n_exp * SUBLANES
    tps = _tiles_per_step(seq_len, tl)
    ts = tl * tps
    per_seq = seq_len // ts
    tok = pl.BlockSpec((TOP_K, ts), lambda j, *_: (0, j))
    grid_spec = pltpu.PrefetchScalarGridSpec(
        num_scalar_prefetch=6,
        grid=(t // ts,),
        in_specs=[pl.BlockSpec((ts, d), lambda j, *_: (j, 0)),
                  pl.BlockSpec((None, 6, d), lambda j, *_: (j // per_seq, 0, 0)), tok],
        out_specs=pl.BlockSpec(memory_space=pl.ANY),
        scratch_shapes=[pltpu.VMEM((2, lr, d // 2), U32), pltpu.VMEM((MOE_TILE, d // 2), U32),
                        pltpu.SemaphoreType.DMA((2,)), pltpu.SemaphoreType.DMA],
    )
    return pl.pallas_call(
        functools.partial(_dispatch_body, n_exp=n_exp, tiles_per_step=tps),
        grid_spec=grid_spec,
        out_shape=jax.ShapeDtypeStruct((n_rows, d // 2), U32),
        compiler_params=_params("arbitrary"),
        name="dispatch",
    )(goff, loff, cp, ltot, pad_start, pad_cnt, x1, mods_l, lp)


def _experts_body(be_ref, bsrc_ref, rows_ref, first_ref, slot_ref, next_ref, xs_ref, wgu_hbm, bgu_ref, wdn_hbm,
                  bdn_ref, ys_ref, wgu_f32, wdn_f32, wgu_bf, wdn_bf, sem_gu, sem_dn, *, layer):
    i = pl.program_id(0)
    e = be_ref[i]
    d, f2 = wgu_bf.shape
    f = f2 // 2

    def fetch(expert, slot, start):
        for hbm, buf, sem in ((wgu_hbm, wgu_f32, sem_gu), (wdn_hbm, wdn_f32, sem_dn)):
            dma = pltpu.make_async_copy(hbm.at[layer, expert], buf.at[slot], sem.at[slot])
            dma.start() if start else dma.wait()

    @pl.when(first_ref[i] == 1)
    def _():
        slot = slot_ref[i]

        @pl.when(i == 0)
        def _():
            fetch(e, slot, True)

        fetch(e, slot, False)

        @pl.when(next_ref[i] >= 0)
        def _():
            fetch(next_ref[i], 1 - slot, True)

        def cast(ref_in, ref_out):
            def step(r, c):
                rows = pl.ds(pl.multiple_of(r * CAST_ROWS, CAST_ROWS), CAST_ROWS)
                ref_out[rows, :] = ref_in[rows, :].astype(BF16)
                return c
            lax.fori_loop(0, ref_in.shape[0] // CAST_ROWS, step, 0)
        cast(wgu_f32.at[slot], wgu_bf)
        cast(wdn_f32.at[slot], wdn_bf)

    hd = d // 2
    tm = xs_ref.shape[0]

    def mlp(n):
        x_lo, x_hi = _unpack_halves(xs_ref[0:n, :])
        proj = lambda cols: _dot(x_lo, wgu_bf[0:hd, cols]) + _dot(x_hi, wgu_bf[hd:, cols]) + bgu_ref[e, :, cols]
        gate = jnp.minimum(proj(slice(0, f)), SWIGLU_LIMIT)
        lin = jnp.clip(proj(slice(f, f2)), -SWIGLU_LIMIT, SWIGLU_LIMIT)
        act = (gate * jax.nn.sigmoid(SWIGLU_ALPHA * gate) * (lin + 1.0)).astype(BF16)
        y = _dot(act, wdn_bf[...]) + bdn_ref[e]
        ys_ref[0:n, :] = _pack_halves(y.astype(BF16).astype(F32))
        if n < tm:
            ys_ref[n:, :] = jnp.zeros((tm - n, hd), U32)

    rows = rows_ref[i]
    sizes = [tm * e // 8 for e in EXPERT_ROW_EIGHTHS]
    for n, smaller in zip(sizes, sizes[1:] + [0]):
        @pl.when((rows > smaller) & (rows <= n))
        def _(n=n):
            mlp(n)

    @pl.when(rows == 0)
    def _():
        ys_ref[...] = jnp.zeros_like(ys_ref)


def _experts(xs, block_e, block_src, block_rows, block_first, block_slot, block_next, w_gu, b_gu, w_dn, b_dn, layer):
    n_rows, dx = xs.shape
    depth, n_exp, d, f2 = w_gu.shape
    f = f2 // 2
    tm = MOE_TILE
    grid_spec = pltpu.PrefetchScalarGridSpec(
        num_scalar_prefetch=6,
        grid=(n_rows // tm,),
        in_specs=[
            pl.BlockSpec((tm, dx), lambda i, be, bs, *_: (bs[i], 0)),
            pl.BlockSpec(memory_space=pl.ANY),
            pl.BlockSpec((None, n_exp, 1, f2), lambda i, *_: (layer, 0, 0, 0)),
            pl.BlockSpec(memory_space=pl.ANY),
            pl.BlockSpec((None, n_exp, 1, d), lambda i, *_: (layer, 0, 0, 0)),
        ],
        out_specs=pl.BlockSpec((tm, d // 2), lambda i, *_: (i, 0)),
        scratch_shapes=[pltpu.VMEM((2, d, f2), F32), pltpu.VMEM((2, f, d), F32),
                        pltpu.VMEM((d, f2), BF16), pltpu.VMEM((f, d), BF16),
                        pltpu.SemaphoreType.DMA((2,)), pltpu.SemaphoreType.DMA((2,))],
    )
    return pl.pallas_call(
        functools.partial(_experts_body, layer=layer),
        grid_spec=grid_spec,
        out_shape=jax.ShapeDtypeStruct((n_rows, d // 2), U32),
        compiler_params=_params("arbitrary"),
        name="experts",
    )(block_e, block_src, block_rows, block_first, block_slot, block_next, xs, w_gu,
      b_gu.reshape(depth, n_exp, 1, f2), w_dn, b_dn.reshape(depth, n_exp, 1, d))


def _combine_body(goff_ref, loff_ref, cp_ref, ltot_ref, ys_hbm, x_ref, lp_ref, gate_ref, mod_ref, lng_ref, lnb_ref,
                  o_ref, yl_ref, sem, *, n_exp, alpha, tiles_per_step):
    j = pl.program_id(0)
    tl = x_ref.shape[0] // tiles_per_step
    n_tiles = pl.num_programs(0) * tiles_per_step
    lr, hd = yl_ref.shape[1:]

    def fetch(tile):
        buf = yl_ref.at[tile % 2]

        def per_expert(e, carry):
            seg = tile * n_exp + e
            _segment_copy(ys_hbm, goff_ref[seg], buf, loff_ref[seg], cp_ref[seg], sem.at[tile % 2])
            return carry
        lax.fori_loop(0, n_exp, per_expert, 0)

        def zero_rows(r, carry):
            buf[pl.ds(pl.multiple_of(r * SUBLANES, SUBLANES), SUBLANES), :] = jnp.zeros((SUBLANES, hd), U32)
            return carry
        lax.fori_loop(ltot_ref[tile] // SUBLANES, lr // SUBLANES, zero_rows, 0)

    @pl.when(j == 0)
    def _():
        fetch(j)

    liota = lax.broadcasted_iota(I32, (tl, lr), 1)
    for t in range(tiles_per_step):
        tile = j * tiles_per_step + t
        tok = slice(t * tl, (t + 1) * tl)

        @pl.when(tile + 1 < n_tiles)
        def _():
            fetch(tile + 1)

        pick = jnp.zeros((tl, lr), F32)
        for k in range(TOP_K):
            pick = jnp.where(liota == lp_ref[tok, k:k + 1], gate_ref[tok, k:k + 1], pick)
        pick = pick.astype(BF16)
        _wait_rows(ys_hbm, ltot_ref[tile], sem.at[tile % 2])
        y_lo, y_hi = _unpack_halves(yl_ref[tile % 2])
        y = jnp.concatenate([_dot(pick, y_lo), _dot(pick, y_hi)], axis=1)
        o_ref[tok, :] = _layer_norm(alpha * x_ref[tok, :] + mod_ref[5:6, :] * y, lng_ref[...], lnb_ref[...])


def _combine(ys, lp_t, gates_t, goff, loff, cp, ltot, x1, mods_l, ln_g, ln_b, alpha, seq_len):
    t, d = x1.shape
    tl = min(MOE_TOKENS, t)
    n_exp = goff.shape[0] // (t // tl)
    lr = TOP_K * tl + n_exp * SUBLANES
    tps = _tiles_per_step(seq_len, tl)
    ts = tl * tps
    per_seq = seq_len // ts
    vec = pl.BlockSpec((1, d), lambda j, *_: (0, 0))
    grid_spec = pltpu.PrefetchScalarGridSpec(
        num_scalar_prefetch=4,
        grid=(t // ts,),
        in_specs=[
            pl.BlockSpec(memory_space=pl.ANY),
            pl.BlockSpec((ts, d), lambda j, *_: (j, 0)),
            pl.BlockSpec((ts, TOP_K), lambda j, *_: (j, 0)),
            pl.BlockSpec((ts, TOP_K), lambda j, *_: (j, 0)),
            pl.BlockSpec((None, 6, d), lambda j, *_: (j // per_seq, 0, 0)),
            vec, vec,
        ],
        out_specs=pl.BlockSpec((ts, d), lambda j, *_: (j, 0)),
        scratch_shapes=[pltpu.VMEM((2, lr, d // 2), U32), pltpu.SemaphoreType.DMA((2,))],
    )
    return pl.pallas_call(
        functools.partial(_combine_body, n_exp=n_exp, alpha=alpha, tiles_per_step=tps),
        grid_spec=grid_spec,
        out_shape=jax.ShapeDtypeStruct((t, d), F32),
        compiler_params=_params("arbitrary"),
        name="combine",
    )(goff, loff, cp, ltot, ys, x1, lp_t, gates_t, mods_l, ln_g.reshape(1, d), ln_b.reshape(1, d))


def _moe_layer(x1, mods_l, w_r, b_r, w_gu, b_gu, w_dn, b_dn, layer, ln_g, ln_b, alpha, seq_len):
    t, d = x1.shape
    n_exp = w_r.shape[1]
    tm = MOE_TILE
    tl = min(MOE_TOKENS, t)
    n_tiles = t // tl
    idx, gates, rank, cnt = _router(x1, mods_l, seq_len, w_r, b_r)
    cp = (cnt[:, :, 0] + SUBLANES - 1) // SUBLANES * SUBLANES
    tot = jnp.sum(cp, axis=0)
    padded = (tot + tm - 1) // tm * tm
    pend = jnp.cumsum(padded)
    pstart = pend - padded
    goff = pstart[None, :] + jnp.cumsum(cp, axis=0) - cp
    loff = jnp.cumsum(cp, axis=1) - cp
    ltot = jnp.sum(cp, axis=1).astype(I32)
    is_e = idx[..., None] == jnp.arange(n_exp, dtype=I32)
    lp = jnp.sum(jnp.where(is_e, jnp.repeat(loff, tl, axis=0)[None], 0), axis=-1) + rank
    n_blocks = -(-(t * TOP_K + n_tiles * n_exp * (SUBLANES - 1)) // tm) + n_exp
    n_valid = pend[-1] // tm
    blk = jnp.arange(n_blocks, dtype=I32)
    block_src = jnp.minimum(blk, n_valid - 1)
    block_e = jnp.sum(pend[None, :] <= (block_src * tm)[:, None], axis=1).astype(I32)
    used = padded > 0
    experts = jnp.arange(n_exp, dtype=I32)
    next_used = jnp.flip(lax.cummin(jnp.flip(jnp.where(used, experts, n_exp))))
    next_used = jnp.concatenate([next_used[1:], jnp.full((1,), n_exp, I32)])
    next_used = jnp.where(next_used < n_exp, next_used, -1)
    slot_of = (jnp.cumsum(used.astype(I32)) - 1) % 2
    of_block = lambda table: jnp.sum(jnp.where(block_e[:, None] == experts[None, :], table[None, :], 0), axis=1)
    block_first = ((blk * tm == of_block(pstart)) & (blk < n_valid)).astype(I32)
    block_slot = of_block(slot_of).astype(I32)
    block_next = of_block(next_used).astype(I32)
    block_rows = jnp.where(blk < n_valid, jnp.clip(of_block(pstart + tot) - blk * tm, 0, tm), 0).astype(I32)
    n_rows = n_blocks * tm
    pad_start = jnp.concatenate([pstart + tot, pend[-1:]]).astype(I32)
    pad_cnt = jnp.concatenate([padded - tot, n_rows - pend[-1:]]).astype(I32)
    flat = lambda a: a.reshape(-1).astype(I32)
    seg = (flat(goff), flat(loff), flat(cp), ltot)
    xs = _dispatch(x1, mods_l, seq_len, lp, *seg, pad_start, pad_cnt, n_rows)
    ys = _experts(xs, block_e, block_src.astype(I32), block_rows, block_first, block_slot, block_next,
                  w_gu, b_gu, w_dn, b_dn, layer)
    return _combine(ys, lp.T, gates.T, *seg, x1, mods_l, ln_g, ln_b, alpha, seq_len)


def kernel(x, c, ada_w, ada_b, post_ln_g, post_ln_b, conv_w_pw1, conv_b_pw1, conv_w_dw, conv_b_dw, conv_ln_g, conv_ln_b, conv_w_pw2, conv_b_pw2, w_kv, attn_w_q, attn_lambda, attn_subln_g, attn_w_o, rel_bias_table, router_w, router_b, expert_w_gate_up, expert_b_gate_up, expert_w_down, expert_b_down):
    bsz, s, d = x.shape
    depth = ada_w.shape[0]
    n_a = depth // 2
    alpha = (2 * depth) ** 0.25
    mods = _ada(c, ada_w, ada_b).reshape(depth, bsz, 6, d)
    q = k = vt = None
    for l in range(depth):
        mods_l = mods[l]
        if l < n_a:
            x1 = _conv(x, mods_l, conv_w_pw1[l], conv_b_pw1[l], conv_w_dw[l], conv_b_dw[l], conv_ln_g[l],
                       conv_ln_b[l], conv_w_pw2[l], conv_b_pw2[l], post_ln_g[l, 0], post_ln_b[l, 0], alpha)
        else:
            j = l - n_a
            if j == 0:
                q, k, vt = _qkv(x, mods_l, w_kv, attn_w_q[j])
            else:
                q = _qkv(x, mods_l, w_kv, attn_w_q[j])[0]
            lambda_init = 0.8 - 0.6 * math.exp(-0.3 * l)
            ot = _attn(q, k, vt, attn_lambda[j], attn_subln_g[j], rel_bias_table, lambda_init)
            x1 = _attn_out(ot, x, mods_l, attn_w_o[j], post_ln_g[l, 0], post_ln_b[l, 0], alpha)
        x = _moe_layer(x1.reshape(bsz * s, d), mods_l, router_w[l], router_b[l],
                       expert_w_gate_up, expert_b_gate_up, expert_w_down, expert_b_down, l,
                       post_ln_g[l, 1], post_ln_b[l, 1], alpha, s).reshape(bsz, s, d)
    return x
```
